```python
import math
import jax, jax.numpy as jnp
from jax import lax
import numpy as np

D_MODEL = 1024
BATCH = 8
SEQ = 4096
DEPTH = 1

D_MIX = D_MODEL
D_POOL = D_MIX // 2
D_ATTN = D_MIX - D_POOL
POOL_WINDOWS = (2, 4, 8, 16)
N_POOL_GROUPS = len(POOL_WINDOWS)
POOL_GROUP_DIM = D_POOL // N_POOL_GROUPS
HEAD_DIM = 64
N_HEADS = D_ATTN // HEAD_DIM
Q_BLOCK = 128
D_FF = 4 * D_MODEL
D_IN_PROJ = D_POOL + 3 * D_ATTN
EPS = 1e-6

kernel_name = "hymba_pool_stickbreak_block"


def rmsnorm(x, g):
    xf = x.astype(jnp.float32)
    r = lax.rsqrt(jnp.mean(xf * xf, axis=-1, keepdims=True) + EPS)
    return (xf * r * g.astype(jnp.float32)).astype(x.dtype)


def pool_mixer(u, pool_w, pool_scale):
    B, S, _ = u.shape
    ug = u.reshape(B, S, N_POOL_GROUPS, POOL_GROUP_DIM)
    pos = jnp.arange(S, dtype=jnp.int32)
    outs = []
    for g, w in enumerate(POOL_WINDOWS):
        xg = ug[:, :, g, :].astype(jnp.float32)
        cs = jnp.cumsum(xg, axis=1)
        cs_lag = jnp.pad(cs, ((0, 0), (w, 0), (0, 0)))[:, :S]
        count = jnp.minimum(pos + 1, w).astype(jnp.float32)[None, :, None]
        mean = (cs - cs_lag) / count
        outs.append(mean - xg)
    pooled = jnp.stack(outs, axis=2)
    mapped = jnp.einsum('bsgc,gcd->bsgd', pooled, pool_w.astype(jnp.float32))
    y = mapped.reshape(B, S, D_POOL) * pool_scale.astype(jnp.float32)
    return y.astype(u.dtype)


def stick_breaking_attention(q, k, v):
    B, H, S, Dh = q.shape
    scale = 1.0 / math.sqrt(Dh)
    n_blocks = S // Q_BLOCK
    outs = []
    for i in range(n_blocks):
        q0, end = i * Q_BLOCK, (i + 1) * Q_BLOCK
        qb = q[:, :, q0:end]
        kb = k[:, :, :end]
        vb = v[:, :, :end]
        z = jnp.einsum('bhqd,bhkd->bhqk', qb, kb).astype(jnp.float32) * scale
        t_idx = q0 + jnp.arange(Q_BLOCK, dtype=jnp.int32)
        s_idx = jnp.arange(end, dtype=jnp.int32)
        mask = s_idx[None, :] < t_idx[:, None]
        log1m = jnp.where(mask, jax.nn.log_sigmoid(-z), 0.0)
        tail = lax.cumsum(log1m, axis=3, reverse=True) - log1m
        log_a = jax.nn.log_sigmoid(z) + tail
        a = jnp.where(mask, jnp.exp(log_a), 0.0)
        ob = jnp.einsum('bhqk,bhkd->bhqd', a, vb.astype(jnp.float32))
        outs.append(ob)
    o = jnp.concatenate(outs, axis=2)
    return o.astype(q.dtype)


def _fwd_setup_inputs(seed: int = 0) -> dict:
    key = jax.random.key(seed)
    ks = jax.random.split(key, 12)
    f32 = jnp.float32
    x = jax.random.normal(ks[0], (BATCH, SEQ, D_MODEL), f32)
    norm1_g = 1.0 + 0.02 * jax.random.normal(ks[1], (D_MODEL,), f32)
    w_in = jax.random.normal(ks[2], (D_MODEL, D_IN_PROJ), f32) * D_MODEL ** -0.5
    pool_w = jax.random.normal(ks[3], (N_POOL_GROUPS, POOL_GROUP_DIM, POOL_GROUP_DIM), f32) * POOL_GROUP_DIM ** -0.5
    pool_scale = 0.5 + 0.02 * jax.random.normal(ks[4], (D_POOL,), f32)
    pool_out_g = 1.0 + 0.02 * jax.random.normal(ks[5], (D_POOL,), f32)
    attn_out_g = 1.0 + 0.02 * jax.random.normal(ks[6], (D_ATTN,), f32)
    w_out = jax.random.normal(ks[7], (D_MIX, D_MODEL), f32) * D_MIX ** -0.5
    norm2_g = 1.0 + 0.02 * jax.random.normal(ks[8], (D_MODEL,), f32)
    w_up = jax.random.normal(ks[9], (D_MODEL, D_FF), f32) * D_MODEL ** -0.5
    w_down = jax.random.normal(ks[10], (D_FF, D_MODEL), f32) * D_FF ** -0.5
    final_g = 1.0 + 0.02 * jax.random.normal(ks[11], (D_MODEL,), f32)
    return {"x": x, "norm1_g": norm1_g, "w_in": w_in, "pool_w": pool_w,
            "pool_scale": pool_scale, "pool_out_g": pool_out_g, "attn_out_g": attn_out_g,
            "w_out": w_out, "norm2_g": norm2_g, "w_up": w_up, "w_down": w_down,
            "final_g": final_g}


def _fwd_reference(x, norm1_g, w_in, pool_w, pool_scale, pool_out_g, attn_out_g,
              w_out, norm2_g, w_up, w_down, final_g):
    B, S, _ = x.shape
    h = x
    for _ in range(DEPTH):
        hn = rmsnorm(h, norm1_g)
        proj = jnp.einsum('bsd,de->bse', hn, w_in)
        u_pool = proj[..., :D_POOL]
        q = proj[..., D_POOL:D_POOL + D_ATTN]
        k = proj[..., D_POOL + D_ATTN:D_POOL + 2 * D_ATTN]
        v = proj[..., D_POOL + 2 * D_ATTN:]
        to_heads = lambda t: t.reshape(B, S, N_HEADS, HEAD_DIM).transpose(0, 2, 1, 3)
        y_pool = pool_mixer(u_pool, pool_w, pool_scale)
        o = stick_breaking_attention(to_heads(q), to_heads(k), to_heads(v))
        y_attn = o.transpose(0, 2, 1, 3).reshape(B, S, D_ATTN)
        mixed = jnp.concatenate([rmsnorm(y_pool, pool_out_g),
                                 rmsnorm(y_attn, attn_out_g)], axis=-1)
        h = h + jnp.einsum('bse,ed->bsd', mixed, w_out)
        hn2 = rmsnorm(h, norm2_g)
        up = jnp.einsum('bsd,df->bsf', hn2, w_up)
        act = jnp.square(jax.nn.relu(up))
        h = h + jnp.einsum('bsf,fd->bsd', act, w_down)
    return rmsnorm(h, final_g)


import jax as _jax
import jax.numpy as _jnp

TWIN_FORMAT = 'train_step'
FWD_PARAMS = ['x', 'norm1_g', 'w_in', 'pool_w', 'pool_scale', 'pool_out_g', 'attn_out_g', 'w_out', 'norm2_g', 'w_up', 'w_down', 'final_g']
TWIN_WEIGHTS = ['norm1_g', 'w_in', 'pool_w', 'pool_scale', 'pool_out_g', 'attn_out_g', 'w_out', 'norm2_g', 'w_up', 'w_down', 'final_g']
TWIN_DIFF_INPUT = 'x'
TWIN_INPUTS = ['x', 'norm1_g', 'w_in', 'pool_w', 'pool_scale', 'pool_out_g', 'attn_out_g', 'w_out', 'norm2_g', 'w_up', 'w_down', 'final_g', 'loss_target', 'm_norm1_g', 'm_w_in', 'm_pool_w', 'm_pool_scale', 'm_pool_out_g', 'm_attn_out_g', 'm_w_out', 'm_norm2_g', 'm_w_up', 'm_w_down', 'm_final_g', 'v_norm1_g', 'v_w_in', 'v_pool_w', 'v_pool_scale', 'v_pool_out_g', 'v_attn_out_g', 'v_w_out', 'v_norm2_g', 'v_w_up', 'v_w_down', 'v_final_g']
TWIN_OUTPUTS = ['loss', 'grad_x', 'grad_norm1_g', 'grad_w_in', 'grad_pool_w', 'grad_pool_scale', 'grad_pool_out_g', 'grad_attn_out_g', 'grad_w_out', 'grad_norm2_g', 'grad_w_up', 'grad_w_down', 'grad_final_g', 'delta_norm1_g', 'delta_w_in', 'delta_pool_w', 'delta_pool_scale', 'delta_pool_out_g', 'delta_attn_out_g', 'delta_w_out', 'delta_norm2_g', 'delta_w_up', 'delta_w_down', 'delta_final_g', 'new_m_norm1_g', 'new_m_w_in', 'new_m_pool_w', 'new_m_pool_scale', 'new_m_pool_out_g', 'new_m_attn_out_g', 'new_m_w_out', 'new_m_norm2_g', 'new_m_w_up', 'new_m_w_down', 'new_m_final_g', 'new_v_norm1_g', 'new_v_w_in', 'new_v_pool_w', 'new_v_pool_scale', 'new_v_pool_out_g', 'new_v_attn_out_g', 'new_v_w_out', 'new_v_norm2_g', 'new_v_w_up', 'new_v_w_down', 'new_v_final_g']
TWIN_LEAF_KINDS = {'loss': 'loss', 'grad_x': 'grad_x', 'grad_norm1_g': 'grad_w', 'grad_w_in': 'grad_w', 'grad_pool_w': 'grad_w', 'grad_pool_scale': 'grad_w', 'grad_pool_out_g': 'grad_w', 'grad_attn_out_g': 'grad_w', 'grad_w_out': 'grad_w', 'grad_norm2_g': 'grad_w', 'grad_w_up': 'grad_w', 'grad_w_down': 'grad_w', 'grad_final_g': 'grad_w', 'delta_norm1_g': 'delta_w', 'delta_w_in': 'delta_w', 'delta_pool_w': 'delta_w', 'delta_pool_scale': 'delta_w', 'delta_pool_out_g': 'delta_w', 'delta_attn_out_g': 'delta_w', 'delta_w_out': 'delta_w', 'delta_norm2_g': 'delta_w', 'delta_w_up': 'delta_w', 'delta_w_down': 'delta_w', 'delta_final_g': 'delta_w', 'new_m_norm1_g': 'new_m', 'new_m_w_in': 'new_m', 'new_m_pool_w': 'new_m', 'new_m_pool_scale': 'new_m', 'new_m_pool_out_g': 'new_m', 'new_m_attn_out_g': 'new_m', 'new_m_w_out': 'new_m', 'new_m_norm2_g': 'new_m', 'new_m_w_up': 'new_m', 'new_m_w_down': 'new_m', 'new_m_final_g': 'new_m', 'new_v_norm1_g': 'new_v', 'new_v_w_in': 'new_v', 'new_v_pool_w': 'new_v', 'new_v_pool_scale': 'new_v', 'new_v_pool_out_g': 'new_v', 'new_v_attn_out_g': 'new_v', 'new_v_w_out': 'new_v', 'new_v_norm2_g': 'new_v', 'new_v_w_up': 'new_v', 'new_v_w_down': 'new_v', 'new_v_final_g': 'new_v'}


def _forward(args):
    return _fwd_reference(*[args[k] for k in FWD_PARAMS])


def _output_shape():
    def fwd():
        inp = _fwd_setup_inputs(0)
        return _fwd_reference(*[inp[k] for k in FWD_PARAMS])
    out = _jax.eval_shape(fwd)
    return out.shape, out.dtype

N_MICROBATCH = 1
ADAM_LR = 0.001
ADAM_B1 = 0.9
ADAM_B2 = 0.999
ADAM_EPS = 1e-08
ADAM_WD = 0.01
ADAM_STEP = 10
PER_EXAMPLE_BATCH_AXIS = {'x': 0, 'loss_target': 0}
SHARED_INPUTS = []
_WEIGHT_DTYPES = {'norm1_g': _jnp.float32, 'w_in': _jnp.float32, 'pool_w': _jnp.float32, 'pool_scale': _jnp.float32, 'pool_out_g': _jnp.float32, 'attn_out_g': _jnp.float32, 'w_out': _jnp.float32, 'norm2_g': _jnp.float32, 'w_up': _jnp.float32, 'w_down': _jnp.float32, 'final_g': _jnp.float32}
MOMENT_SCALE = {'norm1_g': 1.566351e-01, 'w_in': 1.048203e-01, 'pool_w': 1.349112e-01, 'pool_scale': 3.096212e-01, 'pool_out_g': 1.541438e-01, 'attn_out_g': 1.396069e-01, 'w_out': 1.355630e-01, 'norm2_g': 1.381348e-01, 'w_up': 6.747298e-02, 'w_down': 1.352843e-01, 'final_g': 3.218213e+01}


def _to_microbatches(a, axis):
    t = _jnp.moveaxis(a, axis, 0)
    t = t.reshape((N_MICROBATCH, t.shape[0] // N_MICROBATCH) + t.shape[1:])
    return _jnp.moveaxis(t, 1, axis + 1)


def setup_inputs(seed: int = 0) -> dict:
    inp = _fwd_setup_inputs(seed)
    key = _jax.random.fold_in(_jax.random.key(seed), 7919)
    shape, _ = _output_shape()
    out = dict(inp)
    out["loss_target"] = _jax.random.normal(_jax.random.fold_in(key, 0), shape, _jnp.float32)
    for i, name in enumerate(TWIN_WEIGHTS):
        w = inp[name].astype(_jnp.float32)
        if MOMENT_SCALE is None:
            s = _jnp.sqrt(_jnp.mean(_jnp.square(w)) + 1e-30)
        else:
            s = MOMENT_SCALE[name]
        km, kv = _jax.random.split(_jax.random.fold_in(key, i + 1))
        out[name] = w
        out["m_" + name] = s * _jax.random.normal(km, w.shape, _jnp.float32)
        out["v_" + name] = (s * s) * _jax.random.uniform(kv, w.shape, _jnp.float32, 0.5, 1.5)
    if N_MICROBATCH > 1:
        for name, axis in PER_EXAMPLE_BATCH_AXIS.items():
            out[name] = _to_microbatches(out[name], axis)
    return {'x': out['x'], 'norm1_g': out['norm1_g'], 'w_in': out['w_in'], 'pool_w': out['pool_w'], 'pool_scale': out['pool_scale'], 'pool_out_g': out['pool_out_g'], 'attn_out_g': out['attn_out_g'], 'w_out': out['w_out'], 'norm2_g': out['norm2_g'], 'w_up': out['w_up'], 'w_down': out['w_down'], 'final_g': out['final_g'], 'loss_target': out['loss_target'], 'm_norm1_g': out['m_norm1_g'], 'm_w_in': out['m_w_in'], 'm_pool_w': out['m_pool_w'], 'm_pool_scale': out['m_pool_scale'], 'm_pool_out_g': out['m_pool_out_g'], 'm_attn_out_g': out['m_attn_out_g'], 'm_w_out': out['m_w_out'], 'm_norm2_g': out['m_norm2_g'], 'm_w_up': out['m_w_up'], 'm_w_down': out['m_w_down'], 'm_final_g': out['m_final_g'], 'v_norm1_g': out['v_norm1_g'], 'v_w_in': out['v_w_in'], 'v_pool_w': out['v_pool_w'], 'v_pool_scale': out['v_pool_scale'], 'v_pool_out_g': out['v_pool_out_g'], 'v_attn_out_g': out['v_attn_out_g'], 'v_w_out': out['v_w_out'], 'v_norm2_g': out['v_norm2_g'], 'v_w_up': out['v_w_up'], 'v_w_down': out['v_w_down'], 'v_final_g': out['v_final_g']}


def _loss(weights, diff, rest, loss_target):
    with _jax.named_scope("forward"):
        args = {**rest, TWIN_DIFF_INPUT: diff, **{k: w.astype(_WEIGHT_DTYPES[k]) for k, w in weights.items()}}
        y = _forward(args)
    with _jax.named_scope("loss_head"):
        err = _jnp.square(y.astype(_jnp.float32) - loss_target)
        return 0.5 * _jnp.sum(_jnp.mean(err, axis=-1)) if err.ndim else 0.5 * err


def _adamw(w, g, m, v):
    m = ADAM_B1 * m + (1.0 - ADAM_B1) * g
    v = ADAM_B2 * v + (1.0 - ADAM_B2) * _jnp.square(g)
    m_hat = m / (1.0 - ADAM_B1 ** ADAM_STEP)
    v_hat = v / (1.0 - ADAM_B2 ** ADAM_STEP)
    delta = -ADAM_LR * (m_hat / (_jnp.sqrt(v_hat) + ADAM_EPS) + ADAM_WD * w)
    return delta, m, v


def reference(x, norm1_g, w_in, pool_w, pool_scale, pool_out_g, attn_out_g, w_out, norm2_g, w_up, w_down, final_g, loss_target, m_norm1_g, m_w_in, m_pool_w, m_pool_scale, m_pool_out_g, m_attn_out_g, m_w_out, m_norm2_g, m_w_up, m_w_down, m_final_g, v_norm1_g, v_w_in, v_pool_w, v_pool_scale, v_pool_out_g, v_attn_out_g, v_w_out, v_norm2_g, v_w_up, v_w_down, v_final_g):
    given = dict(x=x, norm1_g=norm1_g, w_in=w_in, pool_w=pool_w, pool_scale=pool_scale, pool_out_g=pool_out_g, attn_out_g=attn_out_g, w_out=w_out, norm2_g=norm2_g, w_up=w_up, w_down=w_down, final_g=final_g, loss_target=loss_target, m_norm1_g=m_norm1_g, m_w_in=m_w_in, m_pool_w=m_pool_w, m_pool_scale=m_pool_scale, m_pool_out_g=m_pool_out_g, m_attn_out_g=m_attn_out_g, m_w_out=m_w_out, m_norm2_g=m_norm2_g, m_w_up=m_w_up, m_w_down=m_w_down, m_final_g=m_final_g, v_norm1_g=v_norm1_g, v_w_in=v_w_in, v_pool_w=v_pool_w, v_pool_scale=v_pool_scale, v_pool_out_g=v_pool_out_g, v_attn_out_g=v_attn_out_g, v_w_out=v_w_out, v_norm2_g=v_norm2_g, v_w_up=v_w_up, v_w_down=v_w_down, v_final_g=v_final_g)
    weights = {n: given[n] for n in TWIN_WEIGHTS}
    shared = {n: given[n] for n in SHARED_INPUTS}
    per_example = {n: given[n] for n in ['x']}
    grad_fn = _jax.value_and_grad(_loss, argnums=(0, 1))

    def one_microbatch(ex, loss_target):
        ex = dict(ex)
        diff = ex.pop(TWIN_DIFF_INPUT)
        return grad_fn(weights, diff, {**shared, **ex}, loss_target)

    if N_MICROBATCH == 1:
        loss, (grad_w, grad_x) = one_microbatch(per_example, given["loss_target"])
    else:
        def body(carry, xs):
            loss_sum, grad_sum = carry
            l_k, (gw_k, gx_k) = one_microbatch(xs[0], xs[1])
            with _jax.named_scope("update"):
                return (loss_sum + l_k, _jax.tree.map(_jnp.add, grad_sum, gw_k)), gx_k

        init = (_jnp.zeros((), _jnp.float32), _jax.tree.map(_jnp.zeros_like, weights))
        (loss, grad_w), grad_x = _jax.lax.scan(body, init, (per_example, given["loss_target"]))
    with _jax.named_scope("update"):
        delta_w, new_m, new_v = {}, {}, {}
        for n in TWIN_WEIGHTS:
            delta_w[n], new_m[n], new_v[n] = _adamw(weights[n], grad_w[n], given["m_" + n], given["v_" + n])
    return (loss, grad_x, *[grad_w[n] for n in TWIN_WEIGHTS], *[delta_w[n] for n in TWIN_WEIGHTS],
            *[new_m[n] for n in TWIN_WEIGHTS], *[new_v[n] for n in TWIN_WEIGHTS])
```

```python
import functools

import jax
import jax.numpy as jnp
from jax import lax
from jax.experimental import pallas as pl
from jax.experimental.pallas import tpu as pltpu

F32 = jnp.float32
BF16 = jnp.bfloat16
MESH = pl.DeviceIdType.MESH

N_DEV = 8
D_MODEL = 1024
D_POOL = 512
D_ATTN = 512
N_GROUPS = 4
GROUP_DIM = 128
HEAD_DIM = 64
D_FF = 4096
D_IN_PROJ = 2048
EPS = 1e-6
HALO = 16
ATTN_TILE = 256
LANES = 128

ADAM_LR = 0.001
ADAM_B1 = 0.9
ADAM_B2 = 0.999
ADAM_EPS = 1e-08
ADAM_WD = 0.01
ADAM_STEP = 10

VMEM_LIMIT = 56 * 1024 * 1024


def _params(semantics=None, vmem=VMEM_LIMIT):
    return pltpu.CompilerParams(dimension_semantics=semantics, vmem_limit_bytes=vmem)


def _dot(a, b):
    return jnp.dot(a, b, preferred_element_type=F32)


def _dot_nt(a, b):
    return lax.dot_general(a, b, (((1,), (1,)), ((), ())), preferred_element_type=F32)


def _dot_tn(a, b):
    return lax.dot_general(a, b, (((0,), (0,)), ((), ())), preferred_element_type=F32)


def _split2(x):
    hi = x.astype(BF16)
    lo = (x - hi.astype(F32)).astype(BF16)
    return hi, lo


def _split3(x):
    hi = x.astype(BF16)
    r = x - hi.astype(F32)
    mid = r.astype(BF16)
    lo = (r - mid.astype(F32)).astype(BF16)
    return hi, mid, lo


def _rstd(h):
    return lax.rsqrt(jnp.mean(h * h, axis=-1, keepdims=True) + EPS)


def _rms_bwd(dout, hhat, r, g):
    dg = jnp.sum(dout * hhat, axis=0, keepdims=True)
    dxh = dout * g
    dh = r * (dxh - hhat * jnp.mean(dxh * hhat, axis=-1, keepdims=True))
    return dh, dg


def _my_index():
    return 4 * lax.axis_index("x") + 2 * lax.axis_index("y") + lax.axis_index("c")


def _peer(k):
    x, y, c = lax.axis_index("x"), lax.axis_index("y"), lax.axis_index("c")
    px = 1 - x if (k >> 2) & 1 else x
    py = 1 - y if (k >> 1) & 1 else y
    pc = 1 - c if k & 1 else c
    return (px, py, pc), 4 * px + 2 * py + pc


def _exchange(blocks, gather, name):
    n = len(blocks)

    def body(*refs):
        ins, outs = refs[:n], refs[n : 2 * n]
        send_sems, recv_sems, local_sems = refs[2 * n :]
        me = _my_index()
        started = []
        for t in range(n):
            src_own = ins[t] if gather else ins[t].at[me]
            own = pltpu.make_async_copy(src_own, outs[t].at[me], local_sems.at[t])
            own.start()
            started.append(own)
        sends = []
        for t in range(n):
            for k in range(1, N_DEV):
                peer, peer_idx = _peer(k)
                src = ins[t] if gather else ins[t].at[peer_idx]
                cp = pltpu.make_async_remote_copy(
                    src_ref=src,
                    dst_ref=outs[t].at[me],
                    send_sem=send_sems.at[t * (N_DEV - 1) + k - 1],
                    recv_sem=recv_sems.at[t * (N_DEV - 1) + k - 1],
                    device_id=peer,
                    device_id_type=MESH,
                )
                cp.start()
                sends.append(cp)
        for t in range(n):
            for k in range(1, N_DEV):
                peer, peer_idx = _peer(k)
                src = ins[t] if gather else ins[t].at[peer_idx]
                pltpu.make_async_remote_copy(
                    src_ref=src,
                    dst_ref=outs[t].at[peer_idx],
                    send_sem=send_sems.at[t * (N_DEV - 1) + k - 1],
                    recv_sem=recv_sems.at[t * (N_DEV - 1) + k - 1],
                    device_id=peer,
                    device_id_type=MESH,
                ).wait_recv()
        for cp in sends:
            cp.wait_send()
        for own in started:
            own.wait()

    if gather:
        out_shape = [jax.ShapeDtypeStruct((N_DEV,) + b.shape, b.dtype) for b in blocks]
    else:
        out_shape = [jax.ShapeDtypeStruct(b.shape, b.dtype) for b in blocks]
    any_spec = pl.BlockSpec(memory_space=pl.ANY)
    return pl.pallas_call(
        body,
        name=name,
        out_shape=out_shape,
        in_specs=[any_spec] * n,
        out_specs=[any_spec] * n,
        scratch_shapes=[
            pltpu.SemaphoreType.DMA((n * (N_DEV - 1),)),
            pltpu.SemaphoreType.DMA((n * (N_DEV - 1),)),
            pltpu.SemaphoreType.DMA((n,)),
        ],
        compiler_params=pltpu.CompilerParams(has_side_effects=True),
    )(*blocks)


def _adam(w, g, m, v):
    m2 = ADAM_B1 * m + (1.0 - ADAM_B1) * g
    v2 = ADAM_B2 * v + (1.0 - ADAM_B2) * jnp.square(g)
    m_hat = m2 / (1.0 - ADAM_B1**ADAM_STEP)
    v_hat = v2 / (1.0 - ADAM_B2**ADAM_STEP)
    delta = -ADAM_LR * (m_hat / (jnp.sqrt(v_hat) + ADAM_EPS) + ADAM_WD * w)
    return delta, m2, v2


def _allreduce_adam_small(part, w, m, v):
    rows = part.shape[0]

    def body(p_ref, w_ref, m_ref, v_ref, g_ref, d_ref, m2_ref, v2_ref, land, send_sems, recv_sems):
        me = _my_index()
        land[me] = p_ref[...]
        sends = []
        for k in range(1, N_DEV):
            peer, _ = _peer(k)
            cp = pltpu.make_async_remote_copy(
                src_ref=p_ref,
                dst_ref=land.at[me],
                send_sem=send_sems.at[k - 1],
                recv_sem=recv_sems.at[k - 1],
                device_id=peer,
                device_id_type=MESH,
            )
            cp.start()
            sends.append(cp)
        for k in range(1, N_DEV):
            peer, peer_idx = _peer(k)
            pltpu.make_async_remote_copy(
                src_ref=p_ref,
                dst_ref=land.at[peer_idx],
                send_sem=send_sems.at[k - 1],
                recv_sem=recv_sems.at[k - 1],
                device_id=peer,
                device_id_type=MESH,
            ).wait_recv()
        for cp in sends:
            cp.wait_send()
        g = land[0]
        for s in range(1, N_DEV):
            g = g + land[s]
        g_ref[...] = g
        delta, m2, v2 = _adam(w_ref[...], g, m_ref[...], v_ref[...])
        d_ref[...] = delta
        m2_ref[...] = m2
        v2_ref[...] = v2

    vm = pl.BlockSpec(memory_space=pltpu.VMEM)
    sds = jax.ShapeDtypeStruct((rows, LANES), F32)
    return pl.pallas_call(
        body,
        name="allreduce_adam_small",
        out_shape=[sds] * 4,
        in_specs=[vm] * 4,
        out_specs=[vm] * 4,
        scratch_shapes=[
            pltpu.VMEM((N_DEV, rows, LANES), F32),
            pltpu.SemaphoreType.DMA((N_DEV - 1,)),
            pltpu.SemaphoreType.DMA((N_DEV - 1,)),
        ],
        compiler_params=pltpu.CompilerParams(has_side_effects=True, vmem_limit_bytes=VMEM_LIMIT),
    )(part, w, m, v)


def _reduce_adam(land, w, m, v, name):
    rows, cols = w.shape
    tr = min(rows, max(8, (1 << 19) // cols))

    def body(l_ref, w_ref, m_ref, v_ref, g_ref, d_ref, m2_ref, v2_ref):
        g = l_ref[0].astype(F32)
        for s in range(1, N_DEV):
            g = g + l_ref[s].astype(F32)
        g_ref[...] = g
        delta, m2, v2 = _adam(w_ref[...], g, m_ref[...], v_ref[...])
        d_ref[...] = delta
        m2_ref[...] = m2
        v2_ref[...] = v2

    tile = pl.BlockSpec((tr, cols), lambda i: (i, 0))
    sds = jax.ShapeDtypeStruct((rows, cols), F32)
    return pl.pallas_call(
        body,
        name=name,
        grid=(rows // tr,),
        out_shape=[sds] * 4,
        in_specs=[pl.BlockSpec((N_DEV, tr, cols), lambda i: (0, i, 0)), tile, tile, tile],
        out_specs=[tile] * 4,
        compiler_params=_params(("parallel",)),
    )(land, w, m, v)


def _fwd_in(x, g1, w_in):
    S = x.shape[0]
    tm = min(512, S)

    def body(x_ref, g_ref, w_ref, hn_ref, u_ref, q_ref, k_ref, v_ref):
        xv = x_ref[...]
        hn = (xv * _rstd(xv) * g_ref[...]).astype(BF16)
        hn_ref[...] = hn
        outs = (u_ref, q_ref, k_ref, v_ref)
        for j in range(N_DEV):
            p = _dot(hn, w_ref[j])
            cols = slice(256 * (j % 2), 256 * (j % 2 + 1))
            if j // 2 == 0:
                u_ref[:, cols] = p
            elif j // 2 == 1:
                q_ref[:, cols] = (p * (HEAD_DIM**-0.5)).astype(BF16)
            else:
                outs[j // 2][:, cols] = p.astype(BF16)

    half = pl.BlockSpec((tm, D_POOL), lambda i: (i, 0))
    return pl.pallas_call(
        body,
        name="fwd_in",
        grid=(S // tm,),
        out_shape=[
            jax.ShapeDtypeStruct((S, D_MODEL), BF16),
            jax.ShapeDtypeStruct((S, D_POOL), F32),
            jax.ShapeDtypeStruct((S, D_ATTN), BF16),
            jax.ShapeDtypeStruct((S, D_ATTN), BF16),
            jax.ShapeDtypeStruct((S, D_ATTN), BF16),
        ],
        in_specs=[
            pl.BlockSpec((tm, D_MODEL), lambda i: (i, 0)),
            pl.BlockSpec((1, D_MODEL), lambda i: (0, 0)),
            pl.BlockSpec((N_DEV, D_MODEL, 256), lambda i: (0, 0, 0)),
        ],
        out_specs=[pl.BlockSpec((tm, D_MODEL), lambda i: (i, 0)), half, half, half, half],
        compiler_params=_params(("parallel",)),
    )(x, g1, w_in)


def _pool_chunk(S):
    return min(512, S)


def _pool_fwd(u, pool_w, scale):
    S = u.shape[0]
    R = _pool_chunk(S)

    def body(u_ref, up_ref, pw_ref, sc_ref, pooled_ref, y_ref):
        g = pl.program_id(0)
        c = pl.program_id(1)
        w = jnp.left_shift(jnp.int32(2), g)
        cur = u_ref[...]
        prev = up_ref[...] * (c > 0).astype(F32)
        d = lax.broadcasted_iota(jnp.int32, (R, R), 0) - lax.broadcasted_iota(jnp.int32, (R, R), 1)
        b_cur = jnp.where((d >= 0) & (d < w), 1.0, 0.0).astype(BF16)
        dp = (
            lax.broadcasted_iota(jnp.int32, (R, HALO), 0)
            + HALO
            - lax.broadcasted_iota(jnp.int32, (R, HALO), 1)
        )
        b_prev = jnp.where(dp < w, 1.0, 0.0).astype(BF16)
        wsum = jnp.zeros((R, GROUP_DIM), F32)
        for part in _split3(cur):
            wsum = wsum + _dot(b_cur, part)
        for part in _split3(prev):
            wsum = wsum + _dot(b_prev, part)
        t = c * R + lax.broadcasted_iota(jnp.int32, (R, 1), 0)
        count = jnp.minimum(t + 1, w).astype(F32)
        pooled = (wsum / count - cur).astype(BF16)
        pooled_ref[...] = pooled
        y_ref[...] = _dot(pooled, pw_ref[0].astype(BF16)) * sc_ref[...]

    blk = pl.BlockSpec((R, GROUP_DIM), lambda g, c: (c, g))
    return pl.pallas_call(
        body,
        name="pool_fwd",
        grid=(N_GROUPS, S // R),
        out_shape=[jax.ShapeDtypeStruct((S, D_POOL), BF16), jax.ShapeDtypeStruct((S, D_POOL), F32)],
        in_specs=[
            blk,
            pl.BlockSpec((HALO, GROUP_DIM), lambda g, c: (jnp.maximum(c * (R // HALO) - 1, 0), g)),
            pl.BlockSpec((1, GROUP_DIM, GROUP_DIM), lambda g, c: (g, 0, 0)),
            pl.BlockSpec((1, GROUP_DIM), lambda g, c: (0, g)),
        ],
        out_specs=[blk, blk],
        compiler_params=_params(("parallel", "parallel")),
    )(u, u, pool_w, scale)


def _stick_probs(qh, kb, carry, u_strict, mask):
    z = _dot_nt(qh, kb)
    sp = jnp.log(1.0 + jnp.exp(-jnp.abs(z)))
    ls = jnp.minimum(z, 0.0) - sp
    l1m = -jnp.maximum(z, 0.0) - sp
    if mask is not None:
        l1m = jnp.where(mask, l1m, 0.0)
    hi, lo = _split2(l1m)
    tail = _dot(hi, u_strict) + _dot(lo, u_strict)
    a = jnp.exp(ls + tail + carry)
    if mask is not None:
        a = jnp.where(mask, a, 0.0)
    return ls, l1m, a


def _attn_fwd(q, k, v):
    S = q.shape[0]
    T = ATTN_TILE
    nq = S // T

    def body(q_ref, k_ref, v_ref, o_ref):
        lane = lax.broadcasted_iota(jnp.int32, (T, LANES), 1)
        row = lax.broadcasted_iota(jnp.int32, (T, T), 0)
        col = lax.broadcasted_iota(jnp.int32, (T, T), 1)
        u_strict = jnp.where(row > col, 1.0, 0.0).astype(BF16)
        causal = col < row

        def q_block(qi, _):
            q0 = pl.multiple_of(qi * T, T)
            qb = q_ref[pl.ds(q0, T), :]
            accs = []
            for h in range(2):
                in_head = (lane < HEAD_DIM) if h == 0 else (lane >= HEAD_DIM)
                qh = jnp.where(in_head, qb, jnp.zeros_like(qb))

                def block(k0, carry, acc, mask):
                    kb = k_ref[pl.ds(k0, T), :]
                    vb = v_ref[pl.ds(k0, T), :]
                    _, l1m, a = _stick_probs(qh, kb, carry, u_strict, mask)
                    acc = acc + _dot(a.astype(BF16), vb)
                    carry = carry + jnp.sum(l1m, axis=1, keepdims=True)
                    return carry, acc

                carry, acc = block(q0, jnp.zeros((T, 1), F32), jnp.zeros((T, LANES), F32), causal)

                def k_loop(i, st):
                    k0 = pl.multiple_of((qi - 1 - i) * T, T)
                    return block(k0, st[0], st[1], None)

                carry, acc = lax.fori_loop(0, qi, k_loop, (carry, acc))
                accs.append(acc)
            o_ref[pl.ds(q0, T), :] = jnp.where(lane < HEAD_DIM, accs[0], accs[1])
            return 0

        lax.fori_loop(0, nq, q_block, 0)

    blk = pl.BlockSpec((S, LANES), lambda p: (0, p))
    return pl.pallas_call(
        body,
        name="attn_fwd",
        grid=(D_ATTN // LANES,),
        out_shape=jax.ShapeDtypeStruct((S, D_ATTN), F32),
        in_specs=[blk, blk, blk],
        out_specs=blk,
        compiler_params=_params(("parallel",)),
    )(q, k, v)


SG_ROWS = 8


def _mlp_fwd_bwd(x, y_pool, y_attn, target, g_pool, g_attn, g2, gf, w_out, w_up, w_down):
    S = x.shape[0]
    tm = min(256, S)
    fc = D_FF // N_DEV

    def body(x_ref, yp_ref, ya_ref, t_ref, gp_ref, ga_ref, g2_ref, gf_ref, wo_hbm, wu_hbm, wd_hbm,
             mixed_ref, hn2_ref, act_ref, dup_ref, dh2_ref, dh1_ref, dh1b_ref, dyp_ref, dya_ref, sg_ref,
             wo, wu, wd, up_s, sems):
        @pl.when(pl.program_id(0) == 0)
        def _():
            copies = [
                pltpu.make_async_copy(wo_hbm, wo, sems.at[0]),
                pltpu.make_async_copy(wu_hbm, wu, sems.at[1]),
                pltpu.make_async_copy(wd_hbm, wd, sems.at[2]),
            ]
            for cp in copies:
                cp.start()
            for cp in copies:
                cp.wait()
            sg_ref[...] = jnp.zeros_like(sg_ref)

        gp, ga, g2v, gfv = gp_ref[...], ga_ref[...], g2_ref[...], gf_ref[...]
        yp, ya = yp_ref[...], ya_ref[...]
        rp, ra = _rstd(yp), _rstd(ya)
        yph, yah = yp * rp, ya * ra
        mixed = jnp.concatenate([(yph * gp).astype(BF16), (yah * ga).astype(BF16)], axis=1)
        mixed_ref[...] = mixed
        h1 = x_ref[...] + _dot(mixed, wo[...])
        r2 = _rstd(h1)
        h1h = h1 * r2
        hn2 = (h1h * g2v).astype(BF16)
        hn2_ref[...] = hn2
        h2 = h1
        for j in range(N_DEV):
            cols = slice(fc * j, fc * (j + 1))
            up = _dot(hn2, wu[j])
            up_s[:, cols] = up
            act = jnp.square(jnp.maximum(up, 0.0)).astype(BF16)
            act_ref[:, cols] = act
            h2 = h2 + _dot(act, wd[cols, :])
        rf = _rstd(h2)
        h2h = h2 * rf
        diff = h2h * gfv - t_ref[...]
        loss_rows = 0.5 * jnp.mean(diff * diff, axis=-1, keepdims=True)
        dy = diff * (1.0 / D_MODEL)
        dh2, dgf = _rms_bwd(dy, h2h, rf, gfv)
        dh2b = dh2.astype(BF16)
        dh2_ref[...] = dh2b
        dhn2 = jnp.zeros((tm, D_MODEL), F32)
        for j in range(N_DEV):
            cols = slice(fc * j, fc * (j + 1))
            dact = _dot_nt(dh2b, wd[cols, :])
            dup = (dact * (2.0 * jnp.maximum(up_s[:, cols], 0.0))).astype(BF16)
            dup_ref[:, cols] = dup
            dhn2 = dhn2 + _dot_nt(dup, wu[j])
        dh1n, dg2 = _rms_bwd(dhn2, h1h, r2, g2v)
        dh1 = dh2 + dh1n
        dh1_ref[...] = dh1
        dh1b = dh1.astype(BF16)
        dh1b_ref[...] = dh1b
        dmix = _dot_nt(dh1b, wo[...])
        dyp, dgp = _rms_bwd(dmix[:, :D_POOL], yph, rp, gp)
        dya, dga = _rms_bwd(dmix[:, D_POOL:], yah, ra, ga)
        dyp_ref[...] = dyp
        dya_ref[...] = dya
        sg_ref[0:1, :] += dgf
        sg_ref[1:2, :] += dg2
        sg_ref[2:3, :] += jnp.concatenate([dgp, dga], axis=1)
        sg_ref[3:4, :] += jnp.broadcast_to(jnp.sum(loss_rows, axis=0, keepdims=True), (1, D_MODEL))

    def tok(n):
        return pl.BlockSpec((tm, n), lambda i: (i, 0))

    def vec(n):
        return pl.BlockSpec((1, n), lambda i: (0, 0))

    any_spec = pl.BlockSpec(memory_space=pl.ANY)
    return pl.pallas_call(
        body,
        name="mlp_fwd_bwd",
        grid=(S // tm,),
        out_shape=[
            jax.ShapeDtypeStruct((S, D_MODEL), BF16),
            jax.ShapeDtypeStruct((S, D_MODEL), BF16),
            jax.ShapeDtypeStruct((S, D_FF), BF16),
            jax.ShapeDtypeStruct((S, D_FF), BF16),
            jax.ShapeDtypeStruct((S, D_MODEL), BF16),
            jax.ShapeDtypeStruct((S, D_MODEL), F32),
            jax.ShapeDtypeStruct((S, D_MODEL), BF16),
            jax.ShapeDtypeStruct((S, D_POOL), F32),
            jax.ShapeDtypeStruct((S, D_ATTN), F32),
            jax.ShapeDtypeStruct((SG_ROWS, D_MODEL), F32),
        ],
        in_specs=[tok(D_MODEL), tok(D_POOL), tok(D_ATTN), tok(D_MODEL), vec(D_POOL), vec(D_ATTN),
                  vec(D_MODEL), vec(D_MODEL), any_spec, any_spec, any_spec],
        out_specs=[tok(D_MODEL), tok(D_MODEL), tok(D_FF), tok(D_FF), tok(D_MODEL), tok(D_MODEL),
                   tok(D_MODEL), tok(D_POOL), tok(D_ATTN),
                   pl.BlockSpec((SG_ROWS, D_MODEL), lambda i: (0, 0))],
        scratch_shapes=[
            pltpu.VMEM((D_MODEL, D_MODEL), BF16),
            pltpu.VMEM((N_DEV, D_MODEL, fc), BF16),
            pltpu.VMEM((D_FF, D_MODEL), BF16),
            pltpu.VMEM((tm, D_FF), F32),
            pltpu.SemaphoreType.DMA((3,)),
        ],
        compiler_params=_params(("arbitrary",)),
    )(x, y_pool, y_attn, target, g_pool, g_attn, g2, gf, w_out, w_up, w_down)


def _wgrad(a, b, block_a, name):
    S, ka = a.shape
    nb = b.shape[1]
    ts = min(1024, S)
    if block_a:
        ka //= N_DEV
        a_spec = pl.BlockSpec((ts, ka), lambda i, s: (s, i))
        b_spec = pl.BlockSpec((ts, nb), lambda i, s: (s, 0))
    else:
        nb //= N_DEV
        a_spec = pl.BlockSpec((ts, ka), lambda i, s: (s, 0))
        b_spec = pl.BlockSpec((ts, nb), lambda i, s: (s, i))
    steps = S // ts

    def body(a_ref, b_ref, o_ref, acc):
        s = pl.program_id(1)

        @pl.when(s == 0)
        def _():
            acc[...] = jnp.zeros_like(acc)

        acc[...] += _dot_tn(a_ref[...], b_ref[...])

        @pl.when(s == steps - 1)
        def _():
            o_ref[0] = acc[...].astype(BF16)

    return pl.pallas_call(
        body,
        name=name,
        grid=(N_DEV, steps),
        out_shape=jax.ShapeDtypeStruct((N_DEV, ka, nb), BF16),
        in_specs=[a_spec, b_spec],
        out_specs=pl.BlockSpec((1, ka, nb), lambda i, s: (i, 0, 0)),
        scratch_shapes=[pltpu.VMEM((ka, nb), F32)],
        compiler_params=_params(("parallel", "arbitrary")),
    )(a, b)


def _attn_bwd(q, k, v, do):
    S = q.shape[0]
    T = ATTN_TILE
    nq = S // T

    def body(q_ref, k_ref, v_ref, do_ref, dq_ref, dk_ref, dv_ref, dk_acc, dv_acc, carry_s):
        dk_acc[...] = jnp.zeros_like(dk_acc)
        dv_acc[...] = jnp.zeros_like(dv_acc)
        lane = lax.broadcasted_iota(jnp.int32, (T, LANES), 1)
        row = lax.broadcasted_iota(jnp.int32, (T, T), 0)
        col = lax.broadcasted_iota(jnp.int32, (T, T), 1)
        u_strict = jnp.where(row > col, 1.0, 0.0).astype(BF16)
        causal = col < row
        zero = jnp.zeros((T, 1), F32)

        def q_block(qi, _):
            q0 = pl.multiple_of(qi * T, T)
            qb = q_ref[pl.ds(q0, T), :]
            dob = do_ref[pl.ds(q0, T), :]
            dqs = []
            for h in range(2):
                in_head = (lane < HEAD_DIM) if h == 0 else (lane >= HEAD_DIM)
                qh = jnp.where(in_head, qb, jnp.zeros_like(qb))
                dohb = jnp.where(in_head, dob, 0.0).astype(BF16)

                def l1m_sum(k0, mask):
                    z = _dot_nt(qh, k_ref[pl.ds(k0, T), :])
                    l1m = -jnp.maximum(z, 0.0) - jnp.log(1.0 + jnp.exp(-jnp.abs(z)))
                    if mask is not None:
                        l1m = jnp.where(mask, l1m, 0.0)
                    return jnp.sum(l1m, axis=1, keepdims=True)

                carry_s[qi] = zero

                def right_to_left(i, carry):
                    kj = qi - i
                    k0 = pl.multiple_of(kj * T, T)
                    carry = carry + lax.cond(i == 0, lambda: l1m_sum(k0, causal), lambda: l1m_sum(k0, None))
                    carry_s[kj - 1] = carry
                    return carry

                lax.fori_loop(0, qi, right_to_left, zero)

                def block(kj, gcarry, dq, mask):
                    k0 = pl.multiple_of(kj * T, T)
                    kb = k_ref[pl.ds(k0, T), :]
                    vb = v_ref[pl.ds(k0, T), :]
                    ls, _, a = _stick_probs(qh, kb, carry_s[kj], u_strict, mask)
                    g = a * _dot_nt(dohb, vb)
                    ghi, glo = _split2(g)
                    g_before = gcarry + _dot_nt(ghi, u_strict) + _dot_nt(glo, u_strict)
                    sig = jnp.exp(ls)
                    dz = g * (1.0 - sig) - sig * g_before
                    if mask is not None:
                        dz = jnp.where(mask, dz, 0.0)
                    dzb = dz.astype(BF16)
                    dq = dq + _dot(dzb, kb)
                    dk_acc[pl.ds(k0, T), :] += _dot_tn(dzb, qh)
                    dv_acc[pl.ds(k0, T), :] += _dot_tn(a.astype(BF16), dohb)
                    gcarry = gcarry + jnp.sum(g, axis=1, keepdims=True)
                    return gcarry, dq

                def left_to_right(kj, st):
                    return block(kj, st[0], st[1], None)

                st = lax.fori_loop(0, qi, left_to_right, (zero, jnp.zeros((T, LANES), F32)))
                _, dq = block(qi, st[0], st[1], causal)
                dqs.append(dq)
            dq = jnp.where(lane < HEAD_DIM, dqs[0], dqs[1]) * (HEAD_DIM**-0.5)
            dq_ref[pl.ds(q0, T), :] = dq.astype(BF16)
            return 0

        lax.fori_loop(0, nq, q_block, 0)
        dk_ref[...] = dk_acc[...].astype(BF16)
        dv_ref[...] = dv_acc[...].astype(BF16)

    blk = pl.BlockSpec((S, LANES), lambda p: (0, p))
    sds = jax.ShapeDtypeStruct((S, D_ATTN), BF16)
    return pl.pallas_call(
        body,
        name="attn_bwd",
        grid=(D_ATTN // LANES,),
        out_shape=[sds, sds, sds],
        in_specs=[blk] * 4,
        out_specs=[blk] * 3,
        scratch_shapes=[pltpu.VMEM((S, LANES), F32), pltpu.VMEM((S, LANES), F32), pltpu.VMEM((nq, T, 1), F32)],
        compiler_params=_params(("parallel",)),
    )(q, k, v, do)


def _pool_bwd(dy, pooled, pool_w, scale):
    S = dy.shape[0]
    R = _pool_chunk(S)
    nc = S // R

    def body(dy_ref, dyn_ref, pooled_ref, pw_ref, sc_ref, du_ref, dsc_ref, dpw_ref):
        g = pl.program_id(0)
        c = pl.program_id(1)
        w = jnp.left_shift(jnp.int32(2), g)
        pw = pw_ref[0].astype(BF16)
        sc = sc_ref[...]

        @pl.when(c == 0)
        def _():
            dsc_ref[...] = jnp.zeros_like(dsc_ref)
            dpw_ref[...] = jnp.zeros_like(dpw_ref)

        def d_pooled_of(dyv):
            return _dot_nt((dyv * sc).astype(BF16), pw)

        dyv = dy_ref[...]
        pooled = pooled_ref[...]
        mapped = _dot(pooled, pw)
        dsc_ref[0:1, :] += jnp.sum(dyv * mapped, axis=0, keepdims=True)
        dpw_ref[0] += _dot_tn(pooled, (dyv * sc).astype(BF16))
        dpl = d_pooled_of(dyv)
        dpl_next = d_pooled_of(dyn_ref[...]) * (c < nc - 1).astype(F32)
        t = c * R + lax.broadcasted_iota(jnp.int32, (R, 1), 0)
        cnt = jnp.minimum(t + 1, w).astype(F32)
        tn = (c + 1) * R + lax.broadcasted_iota(jnp.int32, (HALO, 1), 0)
        cnt_next = jnp.minimum(tn + 1, w).astype(F32)
        d = lax.broadcasted_iota(jnp.int32, (R, R), 1) - lax.broadcasted_iota(jnp.int32, (R, R), 0)
        bt_cur = jnp.where((d >= 0) & (d < w), 1.0, 0.0).astype(BF16)
        dn = (
            lax.broadcasted_iota(jnp.int32, (R, HALO), 1)
            + R
            - lax.broadcasted_iota(jnp.int32, (R, HALO), 0)
        )
        bt_next = jnp.where(dn < w, 1.0, 0.0).astype(BF16)
        du = -dpl
        for part in _split2(dpl / cnt):
            du = du + _dot(bt_cur, part)
        for part in _split2(dpl_next / cnt_next):
            du = du + _dot(bt_next, part)
        du_ref[...] = du.astype(BF16)

    blk = pl.BlockSpec((R, GROUP_DIM), lambda g, c: (c, g))
    return pl.pallas_call(
        body,
        name="pool_bwd",
        grid=(N_GROUPS, nc),
        out_shape=[
            jax.ShapeDtypeStruct((S, D_POOL), BF16),
            jax.ShapeDtypeStruct((8, D_POOL), F32),
            jax.ShapeDtypeStruct((N_GROUPS, GROUP_DIM, GROUP_DIM), F32),
        ],
        in_specs=[
            blk,
            pl.BlockSpec((HALO, GROUP_DIM), lambda g, c: (jnp.minimum((c + 1) * (R // HALO), S // HALO - 1), g)),
            blk,
            pl.BlockSpec((1, GROUP_DIM, GROUP_DIM), lambda g, c: (g, 0, 0)),
            pl.BlockSpec((1, GROUP_DIM), lambda g, c: (0, g)),
        ],
        out_specs=[
            blk,
            pl.BlockSpec((8, GROUP_DIM), lambda g, c: (0, g)),
            pl.BlockSpec((1, GROUP_DIM, GROUP_DIM), lambda g, c: (g, 0, 0)),
        ],
        compiler_params=_params(("parallel", "arbitrary")),
    )(dy, dy, pooled, pool_w, scale)


def _bwd_in(du, dq, dk, dv, w_in, x, dh1, g1):
    S = x.shape[0]
    tm = min(512, S)

    def body(du_ref, dq_ref, dk_ref, dv_ref, w_ref, x_ref, dh1_ref, g_ref, dx_ref, dproj_ref, dg_ref):
        @pl.when(pl.program_id(0) == 0)
        def _():
            dg_ref[...] = jnp.zeros_like(dg_ref)

        parts = (du_ref[...], dq_ref[...], dk_ref[...], dv_ref[...])
        dhn = jnp.zeros((tm, D_MODEL), F32)
        for j in range(N_DEV):
            piece = parts[j // 2][:, 256 * (j % 2) : 256 * (j % 2 + 1)]
            dproj_ref[:, 256 * j : 256 * (j + 1)] = piece
            dhn = dhn + _dot_nt(piece, w_ref[j])
        xv = x_ref[...]
        r = _rstd(xv)
        dxn, dg = _rms_bwd(dhn, xv * r, r, g_ref[...])
        dx_ref[...] = dh1_ref[...] + dxn
        dg_ref[0:1, :] += dg

    half = pl.BlockSpec((tm, D_POOL), lambda i: (i, 0))
    full = pl.BlockSpec((tm, D_MODEL), lambda i: (i, 0))
    return pl.pallas_call(
        body,
        name="bwd_in",
        grid=(S // tm,),
        out_shape=[
            jax.ShapeDtypeStruct((S, D_MODEL), F32),
            jax.ShapeDtypeStruct((S, D_IN_PROJ), BF16),
            jax.ShapeDtypeStruct((8, D_MODEL), F32),
        ],
        in_specs=[half, half, half, half,
                  pl.BlockSpec((N_DEV, D_MODEL, 256), lambda i: (0, 0, 0)),
                  full, full, pl.BlockSpec((1, D_MODEL), lambda i: (0, 0))],
        out_specs=[full, pl.BlockSpec((tm, D_IN_PROJ), lambda i: (i, 0)),
                   pl.BlockSpec((8, D_MODEL), lambda i: (0, 0))],
        compiler_params=_params(("arbitrary",)),
    )(du, dq, dk, dv, w_in, x, dh1, g1)


def _rows(a):
    a = a.reshape(-1, LANES)
    pad = (-a.shape[0]) % 8
    return jnp.pad(a, ((0, pad), (0, 0))) if pad else a


SMALL = ("final_g", "norm2_g", "pool_out_g", "attn_out_g", "norm1_g", "pool_scale", "pool_w")


def kernel(x, norm1_g, w_in, pool_w, pool_scale, pool_out_g, attn_out_g, w_out, norm2_g, w_up, w_down, final_g, loss_target, m_norm1_g, m_w_in, m_pool_w, m_pool_scale, m_pool_out_g, m_attn_out_g, m_w_out, m_norm2_g, m_w_up, m_w_down, m_final_g, v_norm1_g, v_w_in, v_pool_w, v_pool_scale, v_pool_out_g, v_attn_out_g, v_w_out, v_norm2_g, v_w_up, v_w_down, v_final_g):
    S = x.shape[1]
    xs = x.reshape(S, D_MODEL)
    tgt = loss_target.reshape(S, D_MODEL)
    row = lambda a: a.reshape(1, -1)

    w_in_g, w_out_g, w_up_g, w_down_g = _exchange(
        [w_in.astype(BF16), w_out.astype(BF16), w_up.astype(BF16), w_down.astype(BF16)], True, "gather_weights"
    )
    w_out_full = w_out_g.reshape(D_MODEL, D_MODEL)
    w_down_full = w_down_g.reshape(D_FF, D_MODEL)

    hn, u_pool, q, k, v = _fwd_in(xs, row(norm1_g), w_in_g)
    pooled, y_pool = _pool_fwd(u_pool, pool_w, row(pool_scale))
    y_attn = _attn_fwd(q, k, v)
    mixed, hn2, act, dup, dh2b, dh1, dh1b, dyp, dya, sg = _mlp_fwd_bwd(
        xs, y_pool, y_attn, tgt, row(pool_out_g), row(attn_out_g), row(norm2_g), row(final_g),
        w_out_full, w_up_g, w_down_full,
    )
    gp_down = _wgrad(act, dh2b, True, "wgrad_down")
    gp_up = _wgrad(hn2, dup, False, "wgrad_up")
    gp_out = _wgrad(mixed, dh1b, True, "wgrad_out")
    dq, dk, dv = _attn_bwd(q, k, v, dya)
    du, dsc, dpw = _pool_bwd(dyp, pooled, pool_w, row(pool_scale))
    dx, dproj, dg1 = _bwd_in(du, dq, dk, dv, w_in_g, xs, dh1, row(norm1_g))
    gp_in = _wgrad(hn, dproj, False, "wgrad_in")

    land_in, land_out, land_up, land_down = _exchange([gp_in, gp_out, gp_up, gp_down], False, "scatter_grads")
    big = {}
    for name, land, w, m, vv in (
        ("w_in", land_in, w_in, m_w_in, v_w_in),
        ("w_out", land_out, w_out, m_w_out, v_w_out),
        ("w_up", land_up, w_up, m_w_up, v_w_up),
        ("w_down", land_down, w_down, m_w_down, v_w_down),
    ):
        big[name] = _reduce_adam(land, w, m, vv, "reduce_adam_" + name)

    part = jnp.concatenate(
        [_rows(sg[0]), _rows(sg[1]), _rows(sg[2]), _rows(dg1[0]), _rows(dsc[0]), _rows(dpw), _rows(sg[3])], axis=0
    )
    given = dict(
        final_g=(final_g, m_final_g, v_final_g), norm2_g=(norm2_g, m_norm2_g, v_norm2_g),
        pool_out_g=(pool_out_g, m_pool_out_g, v_pool_out_g), attn_out_g=(attn_out_g, m_attn_out_g, v_attn_out_g),
        norm1_g=(norm1_g, m_norm1_g, v_norm1_g), pool_scale=(pool_scale, m_pool_scale, v_pool_scale),
        pool_w=(pool_w, m_pool_w, v_pool_w),
    )
    packed = []
    for i in range(3):
        pieces = [given["final_g"][i], given["norm2_g"][i],
                  jnp.concatenate([given["pool_out_g"][i], given["attn_out_g"][i]]),
                  given["norm1_g"][i], given["pool_scale"][i], given["pool_w"][i]]
        packed.append(jnp.concatenate([_rows(p) for p in pieces] + [jnp.zeros((8, LANES), F32)], axis=0))
    small = _allreduce_adam_small(part, *packed)

    def unpack(a):
        out, r = {}, 0
        for name, n in (("final_g", 1024), ("norm2_g", 1024), ("mix_g", 1024), ("norm1_g", 1024),
                        ("pool_scale", 512), ("pool_w", 65536)):
            nr = max(8, n // LANES)
            out[name] = a[r : r + n // LANES].reshape(-1)
            r += nr
        out["pool_out_g"], out["attn_out_g"] = out["mix_g"][:D_POOL], out["mix_g"][D_POOL:]
        out["pool_w"] = out["pool_w"].reshape(N_GROUPS, GROUP_DIM, GROUP_DIM)
        return out, a[r, 0]

    small_out = [unpack(a) for a in small]
    loss = small_out[0][1]
    order = ("norm1_g", "w_in", "pool_w", "pool_scale", "pool_out_g", "attn_out_g", "w_out", "norm2_g", "w_up",
             "w_down", "final_g")
    outs = [loss, dx.reshape(1, S, D_MODEL)]
    for i in range(4):
        for name in order:
            outs.append(big[name][i] if name in big else small_out[i][0][name])
    return tuple(outs)
```

```python
import functools

import jax
import jax.numpy as jnp
from jax import lax
from jax.experimental import pallas as pl
from jax.experimental.pallas import tpu as pltpu

F32 = jnp.float32
BF16 = jnp.bfloat16
MESH = pl.DeviceIdType.MESH

N_DEV = 8
D_MODEL = 1024
D_POOL = 512
D_ATTN = 512
N_GROUPS = 4
GROUP_DIM = 128
HEAD_DIM = 64
D_FF = 4096
D_IN_PROJ = 2048
EPS = 1e-6
HALO = 16
ATTN_TILE = 256
LANES = 128
EXP_UNDERFLOW = -104.0

ADAM_LR = 0.001
ADAM_B1 = 0.9
ADAM_B2 = 0.999
ADAM_EPS = 1e-08
ADAM_WD = 0.01
ADAM_STEP = 10

VMEM_LIMIT = 56 * 1024 * 1024


def _params(semantics=None, vmem=VMEM_LIMIT):
    return pltpu.CompilerParams(dimension_semantics=semantics, vmem_limit_bytes=vmem)


def _dot(a, b):
    return jnp.dot(a, b, preferred_element_type=F32)


def _dot_nt(a, b):
    return lax.dot_general(a, b, (((1,), (1,)), ((), ())), preferred_element_type=F32)


def _dot_tn(a, b):
    return lax.dot_general(a, b, (((0,), (0,)), ((), ())), preferred_element_type=F32)


def _split2(x):
    hi = x.astype(BF16)
    lo = (x - hi.astype(F32)).astype(BF16)
    return hi, lo


def _split3(x):
    hi = x.astype(BF16)
    r = x - hi.astype(F32)
    mid = r.astype(BF16)
    lo = (r - mid.astype(F32)).astype(BF16)
    return hi, mid, lo


def _rstd(h):
    return lax.rsqrt(jnp.mean(h * h, axis=-1, keepdims=True) + EPS)


def _rms_bwd(dout, hhat, r, g):
    dg = jnp.sum(dout * hhat, axis=0, keepdims=True)
    dxh = dout * g
    dh = r * (dxh - hhat * jnp.mean(dxh * hhat, axis=-1, keepdims=True))
    return dh, dg


def _my_index():
    return 4 * lax.axis_index("x") + 2 * lax.axis_index("y") + lax.axis_index("c")


def _peer(k):
    x, y, c = lax.axis_index("x"), lax.axis_index("y"), lax.axis_index("c")
    px = 1 - x if (k >> 2) & 1 else x
    py = 1 - y if (k >> 1) & 1 else y
    pc = 1 - c if k & 1 else c
    return (px, py, pc), 4 * px + 2 * py + pc


def _exchange(blocks, gather, name):
    n = len(blocks)

    def body(*refs):
        ins, outs = refs[:n], refs[n : 2 * n]
        send_sems, recv_sems, local_sems = refs[2 * n :]
        me = _my_index()
        started = []
        for t in range(n):
            src_own = ins[t] if gather else ins[t].at[me]
            own = pltpu.make_async_copy(src_own, outs[t].at[me], local_sems.at[t])
            own.start()
            started.append(own)
        sends = []
        for t in range(n):
            for k in range(1, N_DEV):
                peer, peer_idx = _peer(k)
                src = ins[t] if gather else ins[t].at[peer_idx]
                cp = pltpu.make_async_remote_copy(
                    src_ref=src,
                    dst_ref=outs[t].at[me],
                    send_sem=send_sems.at[t * (N_DEV - 1) + k - 1],
                    recv_sem=recv_sems.at[t * (N_DEV - 1) + k - 1],
                    device_id=peer,
                    device_id_type=MESH,
                )
                cp.start()
                sends.append(cp)
        for t in range(n):
            for k in range(1, N_DEV):
                peer, peer_idx = _peer(k)
                src = ins[t] if gather else ins[t].at[peer_idx]
                pltpu.make_async_remote_copy(
                    src_ref=src,
                    dst_ref=outs[t].at[peer_idx],
                    send_sem=send_sems.at[t * (N_DEV - 1) + k - 1],
                    recv_sem=recv_sems.at[t * (N_DEV - 1) + k - 1],
                    device_id=peer,
                    device_id_type=MESH,
                ).wait_recv()
        for cp in sends:
            cp.wait_send()
        for own in started:
            own.wait()

    if gather:
        out_shape = [jax.ShapeDtypeStruct((N_DEV,) + b.shape, b.dtype) for b in blocks]
    else:
        out_shape = [jax.ShapeDtypeStruct(b.shape, b.dtype) for b in blocks]
    any_spec = pl.BlockSpec(memory_space=pl.ANY)
    return pl.pallas_call(
        body,
        name=name,
        out_shape=out_shape,
        in_specs=[any_spec] * n,
        out_specs=[any_spec] * n,
        scratch_shapes=[
            pltpu.SemaphoreType.DMA((n * (N_DEV - 1),)),
            pltpu.SemaphoreType.DMA((n * (N_DEV - 1),)),
            pltpu.SemaphoreType.DMA((n,)),
        ],
        compiler_params=pltpu.CompilerParams(has_side_effects=True),
    )(*blocks)


def _adam(w, g, m, v):
    m2 = ADAM_B1 * m + (1.0 - ADAM_B1) * g
    v2 = ADAM_B2 * v + (1.0 - ADAM_B2) * jnp.square(g)
    m_hat = m2 / (1.0 - ADAM_B1**ADAM_STEP)
    v_hat = v2 / (1.0 - ADAM_B2**ADAM_STEP)
    delta = -ADAM_LR * (m_hat / (jnp.sqrt(v_hat) + ADAM_EPS) + ADAM_WD * w)
    return delta, m2, v2


def _allreduce_adam_small(part, w, m, v):
    rows = part.shape[0]

    def body(p_ref, w_ref, m_ref, v_ref, g_ref, d_ref, m2_ref, v2_ref, land, send_sems, recv_sems):
        me = _my_index()
        land[me] = p_ref[...]
        sends = []
        for k in range(1, N_DEV):
            peer, _ = _peer(k)
            cp = pltpu.make_async_remote_copy(
                src_ref=p_ref,
                dst_ref=land.at[me],
                send_sem=send_sems.at[k - 1],
                recv_sem=recv_sems.at[k - 1],
                device_id=peer,
                device_id_type=MESH,
            )
            cp.start()
            sends.append(cp)
        for k in range(1, N_DEV):
            peer, peer_idx = _peer(k)
            pltpu.make_async_remote_copy(
                src_ref=p_ref,
                dst_ref=land.at[peer_idx],
                send_sem=send_sems.at[k - 1],
                recv_sem=recv_sems.at[k - 1],
                device_id=peer,
                device_id_type=MESH,
            ).wait_recv()
        for cp in sends:
            cp.wait_send()
        g = land[0]
        for s in range(1, N_DEV):
            g = g + land[s]
        g_ref[...] = g
        delta, m2, v2 = _adam(w_ref[...], g, m_ref[...], v_ref[...])
        d_ref[...] = delta
        m2_ref[...] = m2
        v2_ref[...] = v2

    vm = pl.BlockSpec(memory_space=pltpu.VMEM)
    sds = jax.ShapeDtypeStruct((rows, LANES), F32)
    return pl.pallas_call(
        body,
        name="allreduce_adam_small",
        out_shape=[sds] * 4,
        in_specs=[vm] * 4,
        out_specs=[vm] * 4,
        scratch_shapes=[
            pltpu.VMEM((N_DEV, rows, LANES), F32),
            pltpu.SemaphoreType.DMA((N_DEV - 1,)),
            pltpu.SemaphoreType.DMA((N_DEV - 1,)),
        ],
        compiler_params=pltpu.CompilerParams(has_side_effects=True, vmem_limit_bytes=VMEM_LIMIT),
    )(part, w, m, v)


def _reduce_adam(land, w, m, v, name):
    rows, cols = w.shape
    tr = min(rows, max(8, (1 << 19) // cols))

    def body(l_ref, w_ref, m_ref, v_ref, g_ref, d_ref, m2_ref, v2_ref):
        g = l_ref[0].astype(F32)
        for s in range(1, N_DEV):
            g = g + l_ref[s].astype(F32)
        g_ref[...] = g
        delta, m2, v2 = _adam(w_ref[...], g, m_ref[...], v_ref[...])
        d_ref[...] = delta
        m2_ref[...] = m2
        v2_ref[...] = v2

    tile = pl.BlockSpec((tr, cols), lambda i: (i, 0))
    sds = jax.ShapeDtypeStruct((rows, cols), F32)
    return pl.pallas_call(
        body,
        name=name,
        grid=(rows // tr,),
        out_shape=[sds] * 4,
        in_specs=[pl.BlockSpec((N_DEV, tr, cols), lambda i: (0, i, 0)), tile, tile, tile],
        out_specs=[tile] * 4,
        compiler_params=_params(("parallel",)),
    )(land, w, m, v)


def _fwd_in(x, g1, w_in):
    S = x.shape[0]
    tm = min(512, S)

    def body(x_ref, g_ref, w_ref, hn_ref, u_ref, q_ref, k_ref, v_ref):
        xv = x_ref[...]
        hn = (xv * _rstd(xv) * g_ref[...]).astype(BF16)
        hn_ref[...] = hn
        outs = (u_ref, q_ref, k_ref, v_ref)
        for j in range(N_DEV):
            p = _dot(hn, w_ref[j])
            cols = slice(256 * (j % 2), 256 * (j % 2 + 1))
            if j // 2 == 0:
                u_ref[:, cols] = p
            elif j // 2 == 1:
                q_ref[:, cols] = (p * (HEAD_DIM**-0.5)).astype(BF16)
            else:
                outs[j // 2][:, cols] = p.astype(BF16)

    half = pl.BlockSpec((tm, D_POOL), lambda i: (i, 0))
    return pl.pallas_call(
        body,
        name="fwd_in",
        grid=(S // tm,),
        out_shape=[
            jax.ShapeDtypeStruct((S, D_MODEL), BF16),
            jax.ShapeDtypeStruct((S, D_POOL), F32),
            jax.ShapeDtypeStruct((S, D_ATTN), BF16),
            jax.ShapeDtypeStruct((S, D_ATTN), BF16),
            jax.ShapeDtypeStruct((S, D_ATTN), BF16),
        ],
        in_specs=[
            pl.BlockSpec((tm, D_MODEL), lambda i: (i, 0)),
            pl.BlockSpec((1, D_MODEL), lambda i: (0, 0)),
            pl.BlockSpec((N_DEV, D_MODEL, 256), lambda i: (0, 0, 0)),
        ],
        out_specs=[pl.BlockSpec((tm, D_MODEL), lambda i: (i, 0)), half, half, half, half],
        compiler_params=_params(("parallel",)),
    )(x, g1, w_in)


def _pool_chunk(S):
    return min(512, S)


def _pool_fwd(u, pool_w, scale):
    S = u.shape[0]
    R = _pool_chunk(S)

    def body(u_ref, up_ref, pw_ref, sc_ref, pooled_ref, y_ref):
        g = pl.program_id(0)
        c = pl.program_id(1)
        w = jnp.left_shift(jnp.int32(2), g)
        cur = u_ref[...]
        prev = up_ref[...] * (c > 0).astype(F32)
        d = lax.broadcasted_iota(jnp.int32, (R, R), 0) - lax.broadcasted_iota(jnp.int32, (R, R), 1)
        b_cur = jnp.where((d >= 0) & (d < w), 1.0, 0.0).astype(BF16)
        dp = (
            lax.broadcasted_iota(jnp.int32, (R, HALO), 0)
            + HALO
            - lax.broadcasted_iota(jnp.int32, (R, HALO), 1)
        )
        b_prev = jnp.where(dp < w, 1.0, 0.0).astype(BF16)
        wsum = jnp.zeros((R, GROUP_DIM), F32)
        for part in _split3(cur):
            wsum = wsum + _dot(b_cur, part)
        for part in _split3(prev):
            wsum = wsum + _dot(b_prev, part)
        t = c * R + lax.broadcasted_iota(jnp.int32, (R, 1), 0)
        count = jnp.minimum(t + 1, w).astype(F32)
        pooled = (wsum / count - cur).astype(BF16)
        pooled_ref[...] = pooled
        y_ref[...] = _dot(pooled, pw_ref[0].astype(BF16)) * sc_ref[...]

    blk = pl.BlockSpec((R, GROUP_DIM), lambda g, c: (c, g))
    return pl.pallas_call(
        body,
        name="pool_fwd",
        grid=(N_GROUPS, S // R),
        out_shape=[jax.ShapeDtypeStruct((S, D_POOL), BF16), jax.ShapeDtypeStruct((S, D_POOL), F32)],
        in_specs=[
            blk,
            pl.BlockSpec((HALO, GROUP_DIM), lambda g, c: (jnp.maximum(c * (R // HALO) - 1, 0), g)),
            pl.BlockSpec((1, GROUP_DIM, GROUP_DIM), lambda g, c: (g, 0, 0)),
            pl.BlockSpec((1, GROUP_DIM), lambda g, c: (0, g)),
        ],
        out_specs=[blk, blk],
        compiler_params=_params(("parallel", "parallel")),
    )(u, u, pool_w, scale)


def _stick_probs(qh, kb, carry, u_strict, mask):
    z = _dot_nt(qh, kb)
    sp = jnp.log(1.0 + jnp.exp(-jnp.abs(z)))
    ls = jnp.minimum(z, 0.0) - sp
    l1m = -jnp.maximum(z, 0.0) - sp
    if mask is not None:
        l1m = jnp.where(mask, l1m, 0.0)
    hi, lo = _split2(l1m)
    tail = _dot(hi, u_strict) + _dot(lo, u_strict)
    a = jnp.exp(ls + tail + carry)
    if mask is not None:
        a = jnp.where(mask, a, 0.0)
    return ls, l1m, a


def _reaches(carry):
    return jnp.max(carry) > EXP_UNDERFLOW


def _attn_fwd(q, k, v):
    S = q.shape[0]
    T = ATTN_TILE
    nq = S // T

    def body(q_ref, k_ref, v_ref, o_ref):
        lane = lax.broadcasted_iota(jnp.int32, (T, LANES), 1)
        row = lax.broadcasted_iota(jnp.int32, (T, T), 0)
        col = lax.broadcasted_iota(jnp.int32, (T, T), 1)
        u_strict = jnp.where(row > col, 1.0, 0.0).astype(BF16)
        causal = col < row

        def q_block(qi, _):
            q0 = pl.multiple_of(qi * T, T)
            qb = q_ref[pl.ds(q0, T), :]
            accs = []
            for h in range(2):
                in_head = (lane < HEAD_DIM) if h == 0 else (lane >= HEAD_DIM)
                qh = jnp.where(in_head, qb, jnp.zeros_like(qb))

                def block(k0, carry, acc, mask):
                    kb = k_ref[pl.ds(k0, T), :]
                    vb = v_ref[pl.ds(k0, T), :]
                    _, l1m, a = _stick_probs(qh, kb, carry, u_strict, mask)
                    acc = acc + _dot(a.astype(BF16), vb)
                    carry = carry + jnp.sum(l1m, axis=1, keepdims=True)
                    return carry, acc

                carry, acc = block(q0, jnp.zeros((T, 1), F32), jnp.zeros((T, LANES), F32), causal)

                def more(st):
                    return (st[0] > 0) & _reaches(st[1])

                def k_step(st):
                    kj = st[0] - 1
                    carry, acc = block(pl.multiple_of(kj * T, T), st[1], st[2], None)
                    return kj, carry, acc

                _, carry, acc = lax.while_loop(more, k_step, (qi, carry, acc))
                accs.append(acc)
            o_ref[pl.ds(q0, T), :] = jnp.where(lane < HEAD_DIM, accs[0], accs[1])
            return 0

        lax.fori_loop(0, nq, q_block, 0)

    blk = pl.BlockSpec((S, LANES), lambda p: (0, p))
    return pl.pallas_call(
        body,
        name="attn_fwd",
        grid=(D_ATTN // LANES,),
        out_shape=jax.ShapeDtypeStruct((S, D_ATTN), F32),
        in_specs=[blk, blk, blk],
        out_specs=blk,
        compiler_params=_params(("parallel",)),
    )(q, k, v)


SG_ROWS = 8


def _mlp_fwd_bwd(x, y_pool, y_attn, target, g_pool, g_attn, g2, gf, w_out, w_up, w_down):
    S = x.shape[0]
    tm = min(256, S)
    fc = D_FF // N_DEV

    def body(x_ref, yp_ref, ya_ref, t_ref, gp_ref, ga_ref, g2_ref, gf_ref, wo_hbm, wu_hbm, wd_hbm,
             mixed_ref, hn2_ref, act_ref, dup_ref, dh2_ref, dh1_ref, dh1b_ref, dyp_ref, dya_ref, sg_ref,
             wo, wu, wd, up_s, sems):
        @pl.when(pl.program_id(0) == 0)
        def _():
            copies = [
                pltpu.make_async_copy(wo_hbm, wo, sems.at[0]),
                pltpu.make_async_copy(wu_hbm, wu, sems.at[1]),
                pltpu.make_async_copy(wd_hbm, wd, sems.at[2]),
            ]
            for cp in copies:
                cp.start()
            for cp in copies:
                cp.wait()
            sg_ref[...] = jnp.zeros_like(sg_ref)

        gp, ga, g2v, gfv = gp_ref[...], ga_ref[...], g2_ref[...], gf_ref[...]
        yp, ya = yp_ref[...], ya_ref[...]
        rp, ra = _rstd(yp), _rstd(ya)
        yph, yah = yp * rp, ya * ra
        mixed = jnp.concatenate([(yph * gp).astype(BF16), (yah * ga).astype(BF16)], axis=1)
        mixed_ref[...] = mixed
        h1 = x_ref[...] + _dot(mixed, wo[...])
        r2 = _rstd(h1)
        h1h = h1 * r2
        hn2 = (h1h * g2v).astype(BF16)
        hn2_ref[...] = hn2
        h2 = h1
        for j in range(N_DEV):
            cols = slice(fc * j, fc * (j + 1))
            up = _dot(hn2, wu[j])
            up_s[:, cols] = up
            act = jnp.square(jnp.maximum(up, 0.0)).astype(BF16)
            act_ref[:, cols] = act
            h2 = h2 + _dot(act, wd[cols, :])
        rf = _rstd(h2)
        h2h = h2 * rf
        diff = h2h * gfv - t_ref[...]
        loss_rows = 0.5 * jnp.mean(diff * diff, axis=-1, keepdims=True)
        dy = diff * (1.0 / D_MODEL)
        dh2, dgf = _rms_bwd(dy, h2h, rf, gfv)
        dh2b = dh2.astype(BF16)
        dh2_ref[...] = dh2b
        dhn2 = jnp.zeros((tm, D_MODEL), F32)
        for j in range(N_DEV):
            cols = slice(fc * j, fc * (j + 1))
            dact = _dot_nt(dh2b, wd[cols, :])
            dup = (dact * (2.0 * jnp.maximum(up_s[:, cols], 0.0))).astype(BF16)
            dup_ref[:, cols] = dup
            dhn2 = dhn2 + _dot_nt(dup, wu[j])
        dh1n, dg2 = _rms_bwd(dhn2, h1h, r2, g2v)
        dh1 = dh2 + dh1n
        dh1_ref[...] = dh1
        dh1b = dh1.astype(BF16)
        dh1b_ref[...] = dh1b
        dmix = _dot_nt(dh1b, wo[...])
        dyp, dgp = _rms_bwd(dmix[:, :D_POOL], yph, rp, gp)
        dya, dga = _rms_bwd(dmix[:, D_POOL:], yah, ra, ga)
        dyp_ref[...] = dyp
        dya_ref[...] = dya
        sg_ref[0:1, :] += dgf
        sg_ref[1:2, :] += dg2
        sg_ref[2:3, :] += jnp.concatenate([dgp, dga], axis=1)
        sg_ref[3:4, :] += jnp.broadcast_to(jnp.sum(loss_rows, axis=0, keepdims=True), (1, D_MODEL))

    def tok(n):
        return pl.BlockSpec((tm, n), lambda i: (i, 0))

    def vec(n):
        return pl.BlockSpec((1, n), lambda i: (0, 0))

    any_spec = pl.BlockSpec(memory_space=pl.ANY)
    return pl.pallas_call(
        body,
        name="mlp_fwd_bwd",
        grid=(S // tm,),
        out_shape=[
            jax.ShapeDtypeStruct((S, D_MODEL), BF16),
            jax.ShapeDtypeStruct((S, D_MODEL), BF16),
            jax.ShapeDtypeStruct((S, D_FF), BF16),
            jax.ShapeDtypeStruct((S, D_FF), BF16),
            jax.ShapeDtypeStruct((S, D_MODEL), BF16),
            jax.ShapeDtypeStruct((S, D_MODEL), F32),
            jax.ShapeDtypeStruct((S, D_MODEL), BF16),
            jax.ShapeDtypeStruct((S, D_POOL), F32),
            jax.ShapeDtypeStruct((S, D_ATTN), F32),
            jax.ShapeDtypeStruct((SG_ROWS, D_MODEL), F32),
        ],
        in_specs=[tok(D_MODEL), tok(D_POOL), tok(D_ATTN), tok(D_MODEL), vec(D_POOL), vec(D_ATTN),
                  vec(D_MODEL), vec(D_MODEL), any_spec, any_spec, any_spec],
        out_specs=[tok(D_MODEL), tok(D_MODEL), tok(D_FF), tok(D_FF), tok(D_MODEL), tok(D_MODEL),
                   tok(D_MODEL), tok(D_POOL), tok(D_ATTN),
                   pl.BlockSpec((SG_ROWS, D_MODEL), lambda i: (0, 0))],
        scratch_shapes=[
            pltpu.VMEM((D_MODEL, D_MODEL), BF16),
            pltpu.VMEM((N_DEV, D_MODEL, fc), BF16),
            pltpu.VMEM((D_FF, D_MODEL), BF16),
            pltpu.VMEM((tm, D_FF), F32),
            pltpu.SemaphoreType.DMA((3,)),
        ],
        compiler_params=_params(("arbitrary",)),
    )(x, y_pool, y_attn, target, g_pool, g_attn, g2, gf, w_out, w_up, w_down)


def _wgrad(a, b, block_a, name):
    S, ka = a.shape
    nb = b.shape[1]
    ts = min(1024, S)
    if block_a:
        ka //= N_DEV
        a_spec = pl.BlockSpec((ts, ka), lambda i, s: (s, i))
        b_spec = pl.BlockSpec((ts, nb), lambda i, s: (s, 0))
    else:
        nb //= N_DEV
        a_spec = pl.BlockSpec((ts, ka), lambda i, s: (s, 0))
        b_spec = pl.BlockSpec((ts, nb), lambda i, s: (s, i))
    steps = S // ts

    def body(a_ref, b_ref, o_ref, acc):
        s = pl.program_id(1)

        @pl.when(s == 0)
        def _():
            acc[...] = jnp.zeros_like(acc)

        acc[...] += _dot_tn(a_ref[...], b_ref[...])

        @pl.when(s == steps - 1)
        def _():
            o_ref[0] = acc[...].astype(BF16)

    return pl.pallas_call(
        body,
        name=name,
        grid=(N_DEV, steps),
        out_shape=jax.ShapeDtypeStruct((N_DEV, ka, nb), BF16),
        in_specs=[a_spec, b_spec],
        out_specs=pl.BlockSpec((1, ka, nb), lambda i, s: (i, 0, 0)),
        scratch_shapes=[pltpu.VMEM((ka, nb), F32)],
        compiler_params=_params(("parallel", "arbitrary")),
    )(a, b)


def _attn_bwd(q, k, v, do):
    S = q.shape[0]
    T = ATTN_TILE
    nq = S // T

    def body(q_ref, k_ref, v_ref, do_ref, dq_ref, dk_ref, dv_ref, dk_acc, dv_acc, carry_s):
        dk_acc[...] = jnp.zeros_like(dk_acc)
        dv_acc[...] = jnp.zeros_like(dv_acc)
        lane = lax.broadcasted_iota(jnp.int32, (T, LANES), 1)
        row = lax.broadcasted_iota(jnp.int32, (T, T), 0)
        col = lax.broadcasted_iota(jnp.int32, (T, T), 1)
        u_strict = jnp.where(row > col, 1.0, 0.0).astype(BF16)
        causal = col < row
        zero = jnp.zeros((T, 1), F32)

        def q_block(qi, _):
            q0 = pl.multiple_of(qi * T, T)
            qb = q_ref[pl.ds(q0, T), :]
            dob = do_ref[pl.ds(q0, T), :]
            dqs = []
            for h in range(2):
                in_head = (lane < HEAD_DIM) if h == 0 else (lane >= HEAD_DIM)
                qh = jnp.where(in_head, qb, jnp.zeros_like(qb))
                dohb = jnp.where(in_head, dob, 0.0).astype(BF16)

                def l1m_sum(k0, mask):
                    z = _dot_nt(qh, k_ref[pl.ds(k0, T), :])
                    l1m = -jnp.maximum(z, 0.0) - jnp.log(1.0 + jnp.exp(-jnp.abs(z)))
                    if mask is not None:
                        l1m = jnp.where(mask, l1m, 0.0)
                    return jnp.sum(l1m, axis=1, keepdims=True)

                carry_s[qi] = zero

                def more(st):
                    return (st[0] > 0) & _reaches(st[1])

                def right_to_left(st):
                    kj = st[0] - 1
                    carry_s[kj] = st[1]
                    return kj, st[1] + l1m_sum(pl.multiple_of(kj * T, T), None)

                k_first, _ = lax.while_loop(more, right_to_left, (qi, l1m_sum(q0, causal)))

                def block(kj, gcarry, dq, mask):
                    k0 = pl.multiple_of(kj * T, T)
                    kb = k_ref[pl.ds(k0, T), :]
                    vb = v_ref[pl.ds(k0, T), :]
                    ls, _, a = _stick_probs(qh, kb, carry_s[kj], u_strict, mask)
                    g = a * _dot_nt(dohb, vb)
                    ghi, glo = _split2(g)
                    g_before = gcarry + _dot_nt(ghi, u_strict) + _dot_nt(glo, u_strict)
                    sig = jnp.exp(ls)
                    dz = g * (1.0 - sig) - sig * g_before
                    if mask is not None:
                        dz = jnp.where(mask, dz, 0.0)
                    dzb = dz.astype(BF16)
                    dq = dq + _dot(dzb, kb)
                    dk_acc[pl.ds(k0, T), :] += _dot_tn(dzb, qh)
                    dv_acc[pl.ds(k0, T), :] += _dot_tn(a.astype(BF16), dohb)
                    gcarry = gcarry + jnp.sum(g, axis=1, keepdims=True)
                    return gcarry, dq

                def left_to_right(kj, st):
                    return block(kj, st[0], st[1], None)

                st = lax.fori_loop(k_first, qi, left_to_right, (zero, jnp.zeros((T, LANES), F32)))
                _, dq = block(qi, st[0], st[1], causal)
                dqs.append(dq)
            dq = jnp.where(lane < HEAD_DIM, dqs[0], dqs[1]) * (HEAD_DIM**-0.5)
            dq_ref[pl.ds(q0, T), :] = dq.astype(BF16)
            return 0

        lax.fori_loop(0, nq, q_block, 0)
        dk_ref[...] = dk_acc[...].astype(BF16)
        dv_ref[...] = dv_acc[...].astype(BF16)

    blk = pl.BlockSpec((S, LANES), lambda p: (0, p))
    sds = jax.ShapeDtypeStruct((S, D_ATTN), BF16)
    return pl.pallas_call(
        body,
        name="attn_bwd",
        grid=(D_ATTN // LANES,),
        out_shape=[sds, sds, sds],
        in_specs=[blk] * 4,
        out_specs=[blk] * 3,
        scratch_shapes=[pltpu.VMEM((S, LANES), F32), pltpu.VMEM((S, LANES), F32), pltpu.VMEM((nq, T, 1), F32)],
        compiler_params=_params(("parallel",)),
    )(q, k, v, do)


def _pool_bwd(dy, pooled, pool_w, scale):
    S = dy.shape[0]
    R = _pool_chunk(S)
    nc = S // R

    def body(dy_ref, dyn_ref, pooled_ref, pw_ref, sc_ref, du_ref, dsc_ref, dpw_ref):
        g = pl.program_id(0)
        c = pl.program_id(1)
        w = jnp.left_shift(jnp.int32(2), g)
        pw = pw_ref[0].astype(BF16)
        sc = sc_ref[...]

        @pl.when(c == 0)
        def _():
            dsc_ref[...] = jnp.zeros_like(dsc_ref)
            dpw_ref[...] = jnp.zeros_like(dpw_ref)

        def d_pooled_of(dyv):
            return _dot_nt((dyv * sc).astype(BF16), pw)

        dyv = dy_ref[...]
        pooled = pooled_ref[...]
        mapped = _dot(pooled, pw)
        dsc_ref[0:1, :] += jnp.sum(dyv * mapped, axis=0, keepdims=True)
        dpw_ref[0] += _dot_tn(pooled, (dyv * sc).astype(BF16))
        dpl = d_pooled_of(dyv)
        dpl_next = d_pooled_of(dyn_ref[...]) * (c < nc - 1).astype(F32)
        t = c * R + lax.broadcasted_iota(jnp.int32, (R, 1), 0)
        cnt = jnp.minimum(t + 1, w).astype(F32)
        tn = (c + 1) * R + lax.broadcasted_iota(jnp.int32, (HALO, 1), 0)
        cnt_next = jnp.minimum(tn + 1, w).astype(F32)
        d = lax.broadcasted_iota(jnp.int32, (R, R), 1) - lax.broadcasted_iota(jnp.int32, (R, R), 0)
        bt_cur = jnp.where((d >= 0) & (d < w), 1.0, 0.0).astype(BF16)
        dn = (
            lax.broadcasted_iota(jnp.int32, (R, HALO), 1)
            + R
            - lax.broadcasted_iota(jnp.int32, (R, HALO), 0)
        )
        bt_next = jnp.where(dn < w, 1.0, 0.0).astype(BF16)
        du = -dpl
        for part in _split2(dpl / cnt):
            du = du + _dot(bt_cur, part)
        for part in _split2(dpl_next / cnt_next):
            du = du + _dot(bt_next, part)
        du_ref[...] = du.astype(BF16)

    blk = pl.BlockSpec((R, GROUP_DIM), lambda g, c: (c, g))
    return pl.pallas_call(
        body,
        name="pool_bwd",
        grid=(N_GROUPS, nc),
        out_shape=[
            jax.ShapeDtypeStruct((S, D_POOL), BF16),
            jax.ShapeDtypeStruct((8, D_POOL), F32),
            jax.ShapeDtypeStruct((N_GROUPS, GROUP_DIM, GROUP_DIM), F32),
        ],
        in_specs=[
            blk,
            pl.BlockSpec((HALO, GROUP_DIM), lambda g, c: (jnp.minimum((c + 1) * (R // HALO), S // HALO - 1), g)),
            blk,
            pl.BlockSpec((1, GROUP_DIM, GROUP_DIM), lambda g, c: (g, 0, 0)),
            pl.BlockSpec((1, GROUP_DIM), lambda g, c: (0, g)),
        ],
        out_specs=[
            blk,
            pl.BlockSpec((8, GROUP_DIM), lambda g, c: (0, g)),
            pl.BlockSpec((1, GROUP_DIM, GROUP_DIM), lambda g, c: (g, 0, 0)),
        ],
        compiler_params=_params(("parallel", "arbitrary")),
    )(dy, dy, pooled, pool_w, scale)


def _bwd_in(du, dq, dk, dv, w_in, x, dh1, g1):
    S = x.shape[0]
    tm = min(512, S)

    def body(du_ref, dq_ref, dk_ref, dv_ref, w_ref, x_ref, dh1_ref, g_ref, dx_ref, dproj_ref, dg_ref):
        @pl.when(pl.program_id(0) == 0)
        def _():
            dg_ref[...] = jnp.zeros_like(dg_ref)

        parts = (du_ref[...], dq_ref[...], dk_ref[...], dv_ref[...])
        dhn = jnp.zeros((tm, D_MODEL), F32)
        for j in range(N_DEV):
            piece = parts[j // 2][:, 256 * (j % 2) : 256 * (j % 2 + 1)]
            dproj_ref[:, 256 * j : 256 * (j + 1)] = piece
            dhn = dhn + _dot_nt(piece, w_ref[j])
        xv = x_ref[...]
        r = _rstd(xv)
        dxn, dg = _rms_bwd(dhn, xv * r, r, g_ref[...])
        dx_ref[...] = dh1_ref[...] + dxn
        dg_ref[0:1, :] += dg

    half = pl.BlockSpec((tm, D_POOL), lambda i: (i, 0))
    full = pl.BlockSpec((tm, D_MODEL), lambda i: (i, 0))
    return pl.pallas_call(
        body,
        name="bwd_in",
        grid=(S // tm,),
        out_shape=[
            jax.ShapeDtypeStruct((S, D_MODEL), F32),
            jax.ShapeDtypeStruct((S, D_IN_PROJ), BF16),
            jax.ShapeDtypeStruct((8, D_MODEL), F32),
        ],
        in_specs=[half, half, half, half,
                  pl.BlockSpec((N_DEV, D_MODEL, 256), lambda i: (0, 0, 0)),
                  full, full, pl.BlockSpec((1, D_MODEL), lambda i: (0, 0))],
        out_specs=[full, pl.BlockSpec((tm, D_IN_PROJ), lambda i: (i, 0)),
                   pl.BlockSpec((8, D_MODEL), lambda i: (0, 0))],
        compiler_params=_params(("arbitrary",)),
    )(du, dq, dk, dv, w_in, x, dh1, g1)


def _rows(a):
    a = a.reshape(-1, LANES)
    pad = (-a.shape[0]) % 8
    return jnp.pad(a, ((0, pad), (0, 0))) if pad else a


SMALL = ("final_g", "norm2_g", "pool_out_g", "attn_out_g", "norm1_g", "pool_scale", "pool_w")


def kernel(x, norm1_g, w_in, pool_w, pool_scale, pool_out_g, attn_out_g, w_out, norm2_g, w_up, w_down, final_g, loss_target, m_norm1_g, m_w_in, m_pool_w, m_pool_scale, m_pool_out_g, m_attn_out_g, m_w_out, m_norm2_g, m_w_up, m_w_down, m_final_g, v_norm1_g, v_w_in, v_pool_w, v_pool_scale, v_pool_out_g, v_attn_out_g, v_w_out, v_norm2_g, v_w_up, v_w_down, v_final_g):
    S = x.shape[1]
    xs = x.reshape(S, D_MODEL)
    tgt = loss_target.reshape(S, D_MODEL)
    row = lambda a: a.reshape(1, -1)

    w_in_g, w_out_g, w_up_g, w_down_g = _exchange(
        [w_in.astype(BF16), w_out.astype(BF16), w_up.astype(BF16), w_down.astype(BF16)], True, "gather_weights"
    )
    w_out_full = w_out_g.reshape(D_MODEL, D_MODEL)
    w_down_full = w_down_g.reshape(D_FF, D_MODEL)

    hn, u_pool, q, k, v = _fwd_in(xs, row(norm1_g), w_in_g)
    pooled, y_pool = _pool_fwd(u_pool, pool_w, row(pool_scale))
    y_attn = _attn_fwd(q, k, v)
    mixed, hn2, act, dup, dh2b, dh1, dh1b, dyp, dya, sg = _mlp_fwd_bwd(
        xs, y_pool, y_attn, tgt, row(pool_out_g), row(attn_out_g), row(norm2_g), row(final_g),
        w_out_full, w_up_g, w_down_full,
    )
    gp_down = _wgrad(act, dh2b, True, "wgrad_down")
    gp_up = _wgrad(hn2, dup, False, "wgrad_up")
    gp_out = _wgrad(mixed, dh1b, True, "wgrad_out")
    dq, dk, dv = _attn_bwd(q, k, v, dya)
    du, dsc, dpw = _pool_bwd(dyp, pooled, pool_w, row(pool_scale))
    dx, dproj, dg1 = _bwd_in(du, dq, dk, dv, w_in_g, xs, dh1, row(norm1_g))
    gp_in = _wgrad(hn, dproj, False, "wgrad_in")

    land_in, land_out, land_up, land_down = _exchange([gp_in, gp_out, gp_up, gp_down], False, "scatter_grads")
    big = {}
    for name, land, w, m, vv in (
        ("w_in", land_in, w_in, m_w_in, v_w_in),
        ("w_out", land_out, w_out, m_w_out, v_w_out),
        ("w_up", land_up, w_up, m_w_up, v_w_up),
        ("w_down", land_down, w_down, m_w_down, v_w_down),
    ):
        big[name] = _reduce_adam(land, w, m, vv, "reduce_adam_" + name)

    part = jnp.concatenate(
        [_rows(sg[0]), _rows(sg[1]), _rows(sg[2]), _rows(dg1[0]), _rows(dsc[0]), _rows(dpw), _rows(sg[3])], axis=0
    )
    given = dict(
        final_g=(final_g, m_final_g, v_final_g), norm2_g=(norm2_g, m_norm2_g, v_norm2_g),
        pool_out_g=(pool_out_g, m_pool_out_g, v_pool_out_g), attn_out_g=(attn_out_g, m_attn_out_g, v_attn_out_g),
        norm1_g=(norm1_g, m_norm1_g, v_norm1_g), pool_scale=(pool_scale, m_pool_scale, v_pool_scale),
        pool_w=(pool_w, m_pool_w, v_pool_w),
    )
    packed = []
    for i in range(3):
        pieces = [given["final_g"][i], given["norm2_g"][i],
                  jnp.concatenate([given["pool_out_g"][i], given["attn_out_g"][i]]),
                  given["norm1_g"][i], given["pool_scale"][i], given["pool_w"][i]]
        packed.append(jnp.concatenate([_rows(p) for p in pieces] + [jnp.zeros((8, LANES), F32)], axis=0))
    small = _allreduce_adam_small(part, *packed)

    def unpack(a):
        out, r = {}, 0
        for name, n in (("final_g", 1024), ("norm2_g", 1024), ("mix_g", 1024), ("norm1_g", 1024),
                        ("pool_scale", 512), ("pool_w", 65536)):
            nr = max(8, n // LANES)
            out[name] = a[r : r + n // LANES].reshape(-1)
            r += nr
        out["pool_out_g"], out["attn_out_g"] = out["mix_g"][:D_POOL], out["mix_g"][D_POOL:]
        out["pool_w"] = out["pool_w"].reshape(N_GROUPS, GROUP_DIM, GROUP_DIM)
        return out, a[r, 0]

    small_out = [unpack(a) for a in small]
    loss = small_out[0][1]
    order = ("norm1_g", "w_in", "pool_w", "pool_scale", "pool_out_g", "attn_out_g", "w_out", "norm2_g", "w_up",
             "w_down", "final_g")
    outs = [loss, dx.reshape(1, S, D_MODEL)]
    for i in range(4):
        for name in order:
            outs.append(big[name][i] if name in big else small_out[i][0][name])
    return tuple(outs)
```

```python
import functools

import jax
import jax.numpy as jnp
from jax import lax
from jax.experimental import pallas as pl
from jax.experimental.pallas import tpu as pltpu

F32 = jnp.float32
BF16 = jnp.bfloat16
MESH = pl.DeviceIdType.MESH

N_DEV = 8
D_MODEL = 1024
D_POOL = 512
D_ATTN = 512
N_GROUPS = 4
GROUP_DIM = 128
HEAD_DIM = 64
D_FF = 4096
D_IN_PROJ = 2048
EPS = 1e-6
HALO = 16
ATTN_TILE = 256
LANES = 128
EXP_UNDERFLOW = -104.0

ADAM_LR = 0.001
ADAM_B1 = 0.9
ADAM_B2 = 0.999
ADAM_EPS = 1e-08
ADAM_WD = 0.01
ADAM_STEP = 10

VMEM_LIMIT = 56 * 1024 * 1024


def _params(semantics=None, vmem=VMEM_LIMIT):
    return pltpu.CompilerParams(dimension_semantics=semantics, vmem_limit_bytes=vmem)


def _dot(a, b):
    return jnp.dot(a, b, preferred_element_type=F32)


def _dot_nt(a, b):
    return lax.dot_general(a, b, (((1,), (1,)), ((), ())), preferred_element_type=F32)


def _dot_tn(a, b):
    return lax.dot_general(a, b, (((0,), (0,)), ((), ())), preferred_element_type=F32)


def _split2(x):
    hi = x.astype(BF16)
    lo = (x - hi.astype(F32)).astype(BF16)
    return hi, lo


def _split3(x):
    hi = x.astype(BF16)
    r = x - hi.astype(F32)
    mid = r.astype(BF16)
    lo = (r - mid.astype(F32)).astype(BF16)
    return hi, mid, lo


def _rstd(h):
    return lax.rsqrt(jnp.mean(h * h, axis=-1, keepdims=True) + EPS)


def _rms_bwd(dout, hhat, r, g):
    dg = jnp.sum(dout * hhat, axis=0, keepdims=True)
    dxh = dout * g
    dh = r * (dxh - hhat * jnp.mean(dxh * hhat, axis=-1, keepdims=True))
    return dh, dg


def _my_index():
    return 4 * lax.axis_index("x") + 2 * lax.axis_index("y") + lax.axis_index("c")


def _peer(k):
    x, y, c = lax.axis_index("x"), lax.axis_index("y"), lax.axis_index("c")
    px = 1 - x if (k >> 2) & 1 else x
    py = 1 - y if (k >> 1) & 1 else y
    pc = 1 - c if k & 1 else c
    return (px, py, pc), 4 * px + 2 * py + pc


N_PEERS = N_DEV - 1
ANY_SPEC = pl.BlockSpec(memory_space=pl.ANY)


def _exchange_sems(n):
    return [
        pltpu.SemaphoreType.DMA((n * N_PEERS,)),
        pltpu.SemaphoreType.DMA((n * N_PEERS,)),
        pltpu.SemaphoreType.DMA((n,)),
    ]


def _exchange_shapes(blocks, gather):
    if gather:
        return [jax.ShapeDtypeStruct((N_DEV,) + b.shape, b.dtype) for b in blocks]
    return [jax.ShapeDtypeStruct(b.shape, b.dtype) for b in blocks]


def _direct_exchange(ins, outs, sems, gather):
    send_sems, recv_sems, local_sems = sems
    me = _my_index()
    own, sends, recvs = [], [], []
    for t in range(len(ins)):
        own.append(pltpu.make_async_copy(ins[t] if gather else ins[t].at[me], outs[t].at[me], local_sems.at[t]))
        for k in range(1, N_DEV):
            peer, peer_idx = _peer(k)
            src = ins[t] if gather else ins[t].at[peer_idx]
            for dst, bucket in ((outs[t].at[me], sends), (outs[t].at[peer_idx], recvs)):
                bucket.append(
                    pltpu.make_async_remote_copy(
                        src_ref=src,
                        dst_ref=dst,
                        send_sem=send_sems.at[t * N_PEERS + k - 1],
                        recv_sem=recv_sems.at[t * N_PEERS + k - 1],
                        device_id=peer,
                        device_id_type=MESH,
                    )
                )

    def start():
        for cp in own + sends:
            cp.start()

    def finish():
        for cp in recvs:
            cp.wait_recv()
        for cp in sends:
            cp.wait_send()
        for cp in own:
            cp.wait()

    return start, finish


def _two_level_gather(ins, outs, sems):
    send_sems, recv_sems, local_sems = sems
    x, y, c = lax.axis_index("x"), lax.axis_index("y"), lax.axis_index("c")
    me, sibling = (x, y, c), (x, y, 1 - c)
    chips = [(1 - x, y), (x, 1 - y), (1 - x, 1 - y)]

    def copy(t, k, block, to, from_input=False):
        slot = outs[t].at[4 * block[0] + 2 * block[1] + block[2]]
        return pltpu.make_async_remote_copy(
            src_ref=ins[t] if from_input else slot,
            dst_ref=slot,
            send_sem=send_sems.at[t * N_PEERS + k],
            recv_sem=recv_sems.at[t * N_PEERS + k],
            device_id=to,
            device_id_type=MESH,
        )

    arrays = range(len(ins))
    own = [pltpu.make_async_copy(ins[t], outs[t].at[4 * x + 2 * y + c], local_sems.at[t]) for t in arrays]
    first = [copy(t, 1 + j, me, (*chip, c), True) for t in arrays for j, chip in enumerate(chips)]
    first += [copy(t, 0, me, sibling, True) for t in arrays]
    passed = [(copy(t, 1 + j, (*chip, c), me), copy(t, 4 + j, (*chip, c), sibling))
              for t in arrays for j, chip in enumerate(chips)]
    last = [copy(t, 0, sibling, me) for t in arrays]
    last += [copy(t, 4 + j, (*chip, 1 - c), me) for t in arrays for j, chip in enumerate(chips)]

    def start():
        for cp in own + first:
            cp.start()

    def forward():
        for landed, onward in passed:
            landed.wait_recv()
            onward.start()

    def finish():
        for cp in last:
            cp.wait_recv()
        for cp in first + [onward for _, onward in passed]:
            cp.wait_send()
        for cp in own:
            cp.wait()

    return start, forward, finish


def _exchange(blocks, gather, name):
    n = len(blocks)

    def body(*refs):
        start, finish = _direct_exchange(refs[:n], refs[n : 2 * n], refs[2 * n :], gather)
        start()
        finish()

    return pl.pallas_call(
        body,
        name=name,
        out_shape=_exchange_shapes(blocks, gather),
        in_specs=[ANY_SPEC] * n,
        out_specs=[ANY_SPEC] * n,
        scratch_shapes=_exchange_sems(n),
        compiler_params=pltpu.CompilerParams(has_side_effects=True),
    )(*blocks)


def _adam(w, g, m, v):
    m2 = ADAM_B1 * m + (1.0 - ADAM_B1) * g
    v2 = ADAM_B2 * v + (1.0 - ADAM_B2) * jnp.square(g)
    m_hat = m2 / (1.0 - ADAM_B1**ADAM_STEP)
    v_hat = v2 / (1.0 - ADAM_B2**ADAM_STEP)
    delta = -ADAM_LR * (m_hat / (jnp.sqrt(v_hat) + ADAM_EPS) + ADAM_WD * w)
    return delta, m2, v2


def _allreduce_adam_small(part, w, m, v):
    rows = part.shape[0]

    def body(p_ref, w_ref, m_ref, v_ref, g_ref, d_ref, m2_ref, v2_ref, land, send_sems, recv_sems):
        me = _my_index()
        land[me] = p_ref[...]
        sends = []
        for k in range(1, N_DEV):
            peer, _ = _peer(k)
            cp = pltpu.make_async_remote_copy(
                src_ref=p_ref,
                dst_ref=land.at[me],
                send_sem=send_sems.at[k - 1],
                recv_sem=recv_sems.at[k - 1],
                device_id=peer,
                device_id_type=MESH,
            )
            cp.start()
            sends.append(cp)
        for k in range(1, N_DEV):
            peer, peer_idx = _peer(k)
            pltpu.make_async_remote_copy(
                src_ref=p_ref,
                dst_ref=land.at[peer_idx],
                send_sem=send_sems.at[k - 1],
                recv_sem=recv_sems.at[k - 1],
                device_id=peer,
                device_id_type=MESH,
            ).wait_recv()
        for cp in sends:
            cp.wait_send()
        g = land[0]
        for s in range(1, N_DEV):
            g = g + land[s]
        g_ref[...] = g
        delta, m2, v2 = _adam(w_ref[...], g, m_ref[...], v_ref[...])
        d_ref[...] = delta
        m2_ref[...] = m2
        v2_ref[...] = v2

    vm = pl.BlockSpec(memory_space=pltpu.VMEM)
    sds = jax.ShapeDtypeStruct((rows, LANES), F32)
    return pl.pallas_call(
        body,
        name="allreduce_adam_small",
        out_shape=[sds] * 4,
        in_specs=[vm] * 4,
        out_specs=[vm] * 4,
        scratch_shapes=[
            pltpu.VMEM((N_DEV, rows, LANES), F32),
            pltpu.SemaphoreType.DMA((N_DEV - 1,)),
            pltpu.SemaphoreType.DMA((N_DEV - 1,)),
        ],
        compiler_params=pltpu.CompilerParams(has_side_effects=True, vmem_limit_bytes=VMEM_LIMIT),
    )(part, w, m, v)


def _reduce_adam(land, w, m, v, name):
    rows, cols = w.shape
    tr = min(rows, max(8, (1 << 19) // cols))

    def body(l_ref, w_ref, m_ref, v_ref, g_ref, d_ref, m2_ref, v2_ref):
        g = l_ref[0].astype(F32)
        for s in range(1, N_DEV):
            g = g + l_ref[s].astype(F32)
        g_ref[...] = g
        delta, m2, v2 = _adam(w_ref[...], g, m_ref[...], v_ref[...])
        d_ref[...] = delta
        m2_ref[...] = m2
        v2_ref[...] = v2

    tile = pl.BlockSpec((tr, cols), lambda i: (i, 0))
    sds = jax.ShapeDtypeStruct((rows, cols), F32)
    return pl.pallas_call(
        body,
        name=name,
        grid=(rows // tr,),
        out_shape=[sds] * 4,
        in_specs=[pl.BlockSpec((N_DEV, tr, cols), lambda i: (0, i, 0)), tile, tile, tile],
        out_specs=[tile] * 4,
        compiler_params=_params(("parallel",)),
    )(land, w, m, v)


def _fwd_in(x, g1, w_in):
    S = x.shape[0]
    tm = min(512, S)

    def body(x_ref, g_ref, w_ref, hn_ref, u_ref, q_ref, k_ref, v_ref):
        xv = x_ref[...]
        hn = (xv * _rstd(xv) * g_ref[...]).astype(BF16)
        hn_ref[...] = hn
        outs = (u_ref, q_ref, k_ref, v_ref)
        for j in range(N_DEV):
            p = _dot(hn, w_ref[j])
            cols = slice(256 * (j % 2), 256 * (j % 2 + 1))
            if j // 2 == 0:
                u_ref[:, cols] = p
            elif j // 2 == 1:
                q_ref[:, cols] = (p * (HEAD_DIM**-0.5)).astype(BF16)
            else:
                outs[j // 2][:, cols] = p.astype(BF16)

    half = pl.BlockSpec((tm, D_POOL), lambda i: (i, 0))
    return pl.pallas_call(
        body,
        name="fwd_in",
        grid=(S // tm,),
        out_shape=[
            jax.ShapeDtypeStruct((S, D_MODEL), BF16),
            jax.ShapeDtypeStruct((S, D_POOL), F32),
            jax.ShapeDtypeStruct((S, D_ATTN), BF16),
            jax.ShapeDtypeStruct((S, D_ATTN), BF16),
            jax.ShapeDtypeStruct((S, D_ATTN), BF16),
        ],
        in_specs=[
            pl.BlockSpec((tm, D_MODEL), lambda i: (i, 0)),
            pl.BlockSpec((1, D_MODEL), lambda i: (0, 0)),
            pl.BlockSpec((N_DEV, D_MODEL, 256), lambda i: (0, 0, 0)),
        ],
        out_specs=[pl.BlockSpec((tm, D_MODEL), lambda i: (i, 0)), half, half, half, half],
        compiler_params=_params(("parallel",)),
    )(x, g1, w_in)


def _pool_chunk(S):
    return min(512, S)


def _pool_fwd(u, pool_w, scale):
    S = u.shape[0]
    R = _pool_chunk(S)

    def body(u_ref, up_ref, pw_ref, sc_ref, pooled_ref, y_ref):
        g = pl.program_id(0)
        c = pl.program_id(1)
        w = jnp.left_shift(jnp.int32(2), g)
        cur = u_ref[...]
        prev = up_ref[...] * (c > 0).astype(F32)
        d = lax.broadcasted_iota(jnp.int32, (R, R), 0) - lax.broadcasted_iota(jnp.int32, (R, R), 1)
        b_cur = jnp.where((d >= 0) & (d < w), 1.0, 0.0).astype(BF16)
        dp = (
            lax.broadcasted_iota(jnp.int32, (R, HALO), 0)
            + HALO
            - lax.broadcasted_iota(jnp.int32, (R, HALO), 1)
        )
        b_prev = jnp.where(dp < w, 1.0, 0.0).astype(BF16)
        wsum = jnp.zeros((R, GROUP_DIM), F32)
        for part in _split3(cur):
            wsum = wsum + _dot(b_cur, part)
        for part in _split3(prev):
            wsum = wsum + _dot(b_prev, part)
        t = c * R + lax.broadcasted_iota(jnp.int32, (R, 1), 0)
        count = jnp.minimum(t + 1, w).astype(F32)
        pooled = (wsum / count - cur).astype(BF16)
        pooled_ref[...] = pooled
        y_ref[...] = _dot(pooled, pw_ref[0].astype(BF16)) * sc_ref[...]

    blk = pl.BlockSpec((R, GROUP_DIM), lambda g, c: (c, g))
    return pl.pallas_call(
        body,
        name="pool_fwd",
        grid=(N_GROUPS, S // R),
        out_shape=[jax.ShapeDtypeStruct((S, D_POOL), BF16), jax.ShapeDtypeStruct((S, D_POOL), F32)],
        in_specs=[
            blk,
            pl.BlockSpec((HALO, GROUP_DIM), lambda g, c: (jnp.maximum(c * (R // HALO) - 1, 0), g)),
            pl.BlockSpec((1, GROUP_DIM, GROUP_DIM), lambda g, c: (g, 0, 0)),
            pl.BlockSpec((1, GROUP_DIM), lambda g, c: (0, g)),
        ],
        out_specs=[blk, blk],
        compiler_params=_params(("parallel", "parallel")),
    )(u, u, pool_w, scale)


def _stick_probs(qh, kb, carry, u_strict, mask):
    z = _dot_nt(qh, kb)
    sp = jnp.log(1.0 + jnp.exp(-jnp.abs(z)))
    ls = jnp.minimum(z, 0.0) - sp
    l1m = -jnp.maximum(z, 0.0) - sp
    if mask is not None:
        l1m = jnp.where(mask, l1m, 0.0)
    hi, lo = _split2(l1m)
    tail = _dot(hi, u_strict) + _dot(lo, u_strict)
    a = jnp.exp(ls + tail + carry)
    if mask is not None:
        a = jnp.where(mask, a, 0.0)
    return ls, l1m, a


def _reaches(carry):
    return jnp.max(carry) > EXP_UNDERFLOW


def _attn_fwd(q, k, v, shards):
    S = q.shape[0]
    T = ATTN_TILE
    nq = S // T
    n = len(shards)
    steps = D_ATTN // LANES

    def body(q_ref, k_ref, v_ref, *rest):
        o_ref = rest[n]
        step = pl.program_id(0)

        def gather():
            return _two_level_gather(rest[:n], rest[n + 1 : 2 * n + 1], rest[2 * n + 1 :])

        @pl.when(step == 0)
        def _():
            gather()[0]()

        @pl.when(step == steps - 2)
        def _():
            gather()[1]()

        lane = lax.broadcasted_iota(jnp.int32, (T, LANES), 1)
        row = lax.broadcasted_iota(jnp.int32, (T, T), 0)
        col = lax.broadcasted_iota(jnp.int32, (T, T), 1)
        u_strict = jnp.where(row > col, 1.0, 0.0).astype(BF16)
        causal = col < row

        def q_block(qi, _):
            q0 = pl.multiple_of(qi * T, T)
            qb = q_ref[pl.ds(q0, T), :]
            accs = []
            for h in range(2):
                in_head = (lane < HEAD_DIM) if h == 0 else (lane >= HEAD_DIM)
                qh = jnp.where(in_head, qb, jnp.zeros_like(qb))

                def block(k0, carry, acc, mask):
                    kb = k_ref[pl.ds(k0, T), :]
                    vb = v_ref[pl.ds(k0, T), :]
                    _, l1m, a = _stick_probs(qh, kb, carry, u_strict, mask)
                    acc = acc + _dot(a.astype(BF16), vb)
                    carry = carry + jnp.sum(l1m, axis=1, keepdims=True)
                    return carry, acc

                carry, acc = block(q0, jnp.zeros((T, 1), F32), jnp.zeros((T, LANES), F32), causal)

                def more(st):
                    return (st[0] > 0) & _reaches(st[1])

                def k_step(st):
                    kj = st[0] - 1
                    carry, acc = block(pl.multiple_of(kj * T, T), st[1], st[2], None)
                    return kj, carry, acc

                _, carry, acc = lax.while_loop(more, k_step, (qi, carry, acc))
                accs.append(acc)
            o_ref[pl.ds(q0, T), :] = jnp.where(lane < HEAD_DIM, accs[0], accs[1])
            return 0

        lax.fori_loop(0, nq, q_block, 0)

        @pl.when(step == steps - 1)
        def _():
            gather()[2]()

    blk = pl.BlockSpec((S, LANES), lambda p: (0, p))
    return pl.pallas_call(
        body,
        name="attn_fwd",
        grid=(steps,),
        out_shape=[jax.ShapeDtypeStruct((S, D_ATTN), F32)] + _exchange_shapes(shards, True),
        in_specs=[blk, blk, blk] + [ANY_SPEC] * n,
        out_specs=[blk] + [ANY_SPEC] * n,
        scratch_shapes=_exchange_sems(n),
        compiler_params=pltpu.CompilerParams(
            dimension_semantics=("arbitrary",), vmem_limit_bytes=VMEM_LIMIT, has_side_effects=True
        ),
    )(q, k, v, *shards)


SG_ROWS = 8


def _mlp_fwd_bwd(x, y_pool, y_attn, target, g_pool, g_attn, g2, gf, w_out, w_up, w_down):
    S = x.shape[0]
    tm = min(256, S)
    fc = D_FF // N_DEV

    def body(x_ref, yp_ref, ya_ref, t_ref, gp_ref, ga_ref, g2_ref, gf_ref, wo_hbm, wu_hbm, wd_hbm,
             mixed_ref, hn2_ref, act_ref, dup_ref, dh2_ref, dh1_ref, dh1b_ref, dyp_ref, dya_ref, sg_ref,
             wo, wu, wd, up_s, sems):
        @pl.when(pl.program_id(0) == 0)
        def _():
            copies = [
                pltpu.make_async_copy(wo_hbm, wo, sems.at[0]),
                pltpu.make_async_copy(wu_hbm, wu, sems.at[1]),
                pltpu.make_async_copy(wd_hbm, wd, sems.at[2]),
            ]
            for cp in copies:
                cp.start()
            for cp in copies:
                cp.wait()
            sg_ref[...] = jnp.zeros_like(sg_ref)

        gp, ga, g2v, gfv = gp_ref[...], ga_ref[...], g2_ref[...], gf_ref[...]
        yp, ya = yp_ref[...], ya_ref[...]
        rp, ra = _rstd(yp), _rstd(ya)
        yph, yah = yp * rp, ya * ra
        mixed = jnp.concatenate([(yph * gp).astype(BF16), (yah * ga).astype(BF16)], axis=1)
        mixed_ref[...] = mixed
        h1 = x_ref[...] + _dot(mixed, wo[...])
        r2 = _rstd(h1)
        h1h = h1 * r2
        hn2 = (h1h * g2v).astype(BF16)
        hn2_ref[...] = hn2
        h2 = h1
        for j in range(N_DEV):
            cols = slice(fc * j, fc * (j + 1))
            up = _dot(hn2, wu[j])
            up_s[:, cols] = up
            act = jnp.square(jnp.maximum(up, 0.0)).astype(BF16)
            act_ref[:, cols] = act
            h2 = h2 + _dot(act, wd[cols, :])
        rf = _rstd(h2)
        h2h = h2 * rf
        diff = h2h * gfv - t_ref[...]
        loss_rows = 0.5 * jnp.mean(diff * diff, axis=-1, keepdims=True)
        dy = diff * (1.0 / D_MODEL)
        dh2, dgf = _rms_bwd(dy, h2h, rf, gfv)
        dh2b = dh2.astype(BF16)
        dh2_ref[...] = dh2b
        dhn2 = jnp.zeros((tm, D_MODEL), F32)
        for j in range(N_DEV):
            cols = slice(fc * j, fc * (j + 1))
            dact = _dot_nt(dh2b, wd[cols, :])
            dup = (dact * (2.0 * jnp.maximum(up_s[:, cols], 0.0))).astype(BF16)
            dup_ref[:, cols] = dup
            dhn2 = dhn2 + _dot_nt(dup, wu[j])
        dh1n, dg2 = _rms_bwd(dhn2, h1h, r2, g2v)
        dh1 = dh2 + dh1n
        dh1_ref[...] = dh1
        dh1b = dh1.astype(BF16)
        dh1b_ref[...] = dh1b
        dmix = _dot_nt(dh1b, wo[...])
        dyp, dgp = _rms_bwd(dmix[:, :D_POOL], yph, rp, gp)
        dya, dga = _rms_bwd(dmix[:, D_POOL:], yah, ra, ga)
        dyp_ref[...] = dyp
        dya_ref[...] = dya
        sg_ref[0:1, :] += dgf
        sg_ref[1:2, :] += dg2
        sg_ref[2:3, :] += jnp.concatenate([dgp, dga], axis=1)
        sg_ref[3:4, :] += jnp.broadcast_to(jnp.sum(loss_rows, axis=0, keepdims=True), (1, D_MODEL))

    def tok(n):
        return pl.BlockSpec((tm, n), lambda i: (i, 0))

    def vec(n):
        return pl.BlockSpec((1, n), lambda i: (0, 0))

    any_spec = pl.BlockSpec(memory_space=pl.ANY)
    return pl.pallas_call(
        body,
        name="mlp_fwd_bwd",
        grid=(S // tm,),
        out_shape=[
            jax.ShapeDtypeStruct((S, D_MODEL), BF16),
            jax.ShapeDtypeStruct((S, D_MODEL), BF16),
            jax.ShapeDtypeStruct((S, D_FF), BF16),
            jax.ShapeDtypeStruct((S, D_FF), BF16),
            jax.ShapeDtypeStruct((S, D_MODEL), BF16),
            jax.ShapeDtypeStruct((S, D_MODEL), F32),
            jax.ShapeDtypeStruct((S, D_MODEL), BF16),
            jax.ShapeDtypeStruct((S, D_POOL), F32),
            jax.ShapeDtypeStruct((S, D_ATTN), F32),
            jax.ShapeDtypeStruct((SG_ROWS, D_MODEL), F32),
        ],
        in_specs=[tok(D_MODEL), tok(D_POOL), tok(D_ATTN), tok(D_MODEL), vec(D_POOL), vec(D_ATTN),
                  vec(D_MODEL), vec(D_MODEL), any_spec, any_spec, any_spec],
        out_specs=[tok(D_MODEL), tok(D_MODEL), tok(D_FF), tok(D_FF), tok(D_MODEL), tok(D_MODEL),
                   tok(D_MODEL), tok(D_POOL), tok(D_ATTN),
                   pl.BlockSpec((SG_ROWS, D_MODEL), lambda i: (0, 0))],
        scratch_shapes=[
            pltpu.VMEM((D_MODEL, D_MODEL), BF16),
            pltpu.VMEM((N_DEV, D_MODEL, fc), BF16),
            pltpu.VMEM((D_FF, D_MODEL), BF16),
            pltpu.VMEM((tm, D_FF), F32),
            pltpu.SemaphoreType.DMA((3,)),
        ],
        compiler_params=_params(("arbitrary",)),
    )(x, y_pool, y_attn, target, g_pool, g_attn, g2, gf, w_out, w_up, w_down)


def _wgrad(a, b, block_a, name):
    S, ka = a.shape
    nb = b.shape[1]
    ts = min(1024, S)
    if block_a:
        ka //= N_DEV
        a_spec = pl.BlockSpec((ts, ka), lambda i, s: (s, i))
        b_spec = pl.BlockSpec((ts, nb), lambda i, s: (s, 0))
    else:
        nb //= N_DEV
        a_spec = pl.BlockSpec((ts, ka), lambda i, s: (s, 0))
        b_spec = pl.BlockSpec((ts, nb), lambda i, s: (s, i))
    steps = S // ts

    def body(a_ref, b_ref, o_ref, acc):
        s = pl.program_id(1)

        @pl.when(s == 0)
        def _():
            acc[...] = jnp.zeros_like(acc)

        acc[...] += _dot_tn(a_ref[...], b_ref[...])

        @pl.when(s == steps - 1)
        def _():
            o_ref[0] = acc[...].astype(BF16)

    return pl.pallas_call(
        body,
        name=name,
        grid=(N_DEV, steps),
        out_shape=jax.ShapeDtypeStruct((N_DEV, ka, nb), BF16),
        in_specs=[a_spec, b_spec],
        out_specs=pl.BlockSpec((1, ka, nb), lambda i, s: (i, 0, 0)),
        scratch_shapes=[pltpu.VMEM((ka, nb), F32)],
        compiler_params=_params(("parallel", "arbitrary")),
    )(a, b)


def _attn_bwd(q, k, v, do, partials):
    S = q.shape[0]
    T = ATTN_TILE
    nq = S // T
    n = len(partials)
    steps = D_ATTN // LANES

    def body(q_ref, k_ref, v_ref, do_ref, *rest):
        dq_ref, dk_ref, dv_ref = rest[n : n + 3]
        dk_acc, dv_acc, carry_s = rest[2 * n + 3 : 2 * n + 6]
        step = pl.program_id(0)

        def scatter():
            return _direct_exchange(rest[:n], rest[n + 3 : 2 * n + 3], rest[2 * n + 6 :], False)

        @pl.when(step == 0)
        def _():
            scatter()[0]()

        dk_acc[...] = jnp.zeros_like(dk_acc)
        dv_acc[...] = jnp.zeros_like(dv_acc)
        lane = lax.broadcasted_iota(jnp.int32, (T, LANES), 1)
        row = lax.broadcasted_iota(jnp.int32, (T, T), 0)
        col = lax.broadcasted_iota(jnp.int32, (T, T), 1)
        u_strict = jnp.where(row > col, 1.0, 0.0).astype(BF16)
        causal = col < row
        zero = jnp.zeros((T, 1), F32)

        def q_block(qi, _):
            q0 = pl.multiple_of(qi * T, T)
            qb = q_ref[pl.ds(q0, T), :]
            dob = do_ref[pl.ds(q0, T), :]
            dqs = []
            for h in range(2):
                in_head = (lane < HEAD_DIM) if h == 0 else (lane >= HEAD_DIM)
                qh = jnp.where(in_head, qb, jnp.zeros_like(qb))
                dohb = jnp.where(in_head, dob, 0.0).astype(BF16)

                def l1m_sum(k0, mask):
                    z = _dot_nt(qh, k_ref[pl.ds(k0, T), :])
                    l1m = -jnp.maximum(z, 0.0) - jnp.log(1.0 + jnp.exp(-jnp.abs(z)))
                    if mask is not None:
                        l1m = jnp.where(mask, l1m, 0.0)
                    return jnp.sum(l1m, axis=1, keepdims=True)

                carry_s[qi] = zero

                def more(st):
                    return (st[0] > 0) & _reaches(st[1])

                def right_to_left(st):
                    kj = st[0] - 1
                    carry_s[kj] = st[1]
                    return kj, st[1] + l1m_sum(pl.multiple_of(kj * T, T), None)

                k_first, _ = lax.while_loop(more, right_to_left, (qi, l1m_sum(q0, causal)))

                def block(kj, gcarry, dq, mask):
                    k0 = pl.multiple_of(kj * T, T)
                    kb = k_ref[pl.ds(k0, T), :]
                    vb = v_ref[pl.ds(k0, T), :]
                    ls, _, a = _stick_probs(qh, kb, carry_s[kj], u_strict, mask)
                    g = a * _dot_nt(dohb, vb)
                    ghi, glo = _split2(g)
                    g_before = gcarry + _dot_nt(ghi, u_strict) + _dot_nt(glo, u_strict)
                    sig = jnp.exp(ls)
                    dz = g * (1.0 - sig) - sig * g_before
                    if mask is not None:
                        dz = jnp.where(mask, dz, 0.0)
                    dzb = dz.astype(BF16)
                    dq = dq + _dot(dzb, kb)
                    dk_acc[pl.ds(k0, T), :] += _dot_tn(dzb, qh)
                    dv_acc[pl.ds(k0, T), :] += _dot_tn(a.astype(BF16), dohb)
                    gcarry = gcarry + jnp.sum(g, axis=1, keepdims=True)
                    return gcarry, dq

                def left_to_right(kj, st):
                    return block(kj, st[0], st[1], None)

                st = lax.fori_loop(k_first, qi, left_to_right, (zero, jnp.zeros((T, LANES), F32)))
                _, dq = block(qi, st[0], st[1], causal)
                dqs.append(dq)
            dq = jnp.where(lane < HEAD_DIM, dqs[0], dqs[1]) * (HEAD_DIM**-0.5)
            dq_ref[pl.ds(q0, T), :] = dq.astype(BF16)
            return 0

        lax.fori_loop(0, nq, q_block, 0)
        dk_ref[...] = dk_acc[...].astype(BF16)
        dv_ref[...] = dv_acc[...].astype(BF16)

        @pl.when(step == steps - 1)
        def _():
            scatter()[1]()

    blk = pl.BlockSpec((S, LANES), lambda p: (0, p))
    sds = jax.ShapeDtypeStruct((S, D_ATTN), BF16)
    return pl.pallas_call(
        body,
        name="attn_bwd",
        grid=(steps,),
        out_shape=[sds, sds, sds] + _exchange_shapes(partials, False),
        in_specs=[blk] * 4 + [ANY_SPEC] * n,
        out_specs=[blk] * 3 + [ANY_SPEC] * n,
        scratch_shapes=[pltpu.VMEM((S, LANES), F32), pltpu.VMEM((S, LANES), F32), pltpu.VMEM((nq, T, 1), F32)]
        + _exchange_sems(n),
        compiler_params=pltpu.CompilerParams(
            dimension_semantics=("arbitrary",), vmem_limit_bytes=VMEM_LIMIT, has_side_effects=True
        ),
    )(q, k, v, do, *partials)


def _pool_bwd(dy, pooled, pool_w, scale):
    S = dy.shape[0]
    R = _pool_chunk(S)
    nc = S // R

    def body(dy_ref, dyn_ref, pooled_ref, pw_ref, sc_ref, du_ref, dsc_ref, dpw_ref):
        g = pl.program_id(0)
        c = pl.program_id(1)
        w = jnp.left_shift(jnp.int32(2), g)
        pw = pw_ref[0].astype(BF16)
        sc = sc_ref[...]

        @pl.when(c == 0)
        def _():
            dsc_ref[...] = jnp.zeros_like(dsc_ref)
            dpw_ref[...] = jnp.zeros_like(dpw_ref)

        def d_pooled_of(dyv):
            return _dot_nt((dyv * sc).astype(BF16), pw)

        dyv = dy_ref[...]
        pooled = pooled_ref[...]
        mapped = _dot(pooled, pw)
        dsc_ref[0:1, :] += jnp.sum(dyv * mapped, axis=0, keepdims=True)
        dpw_ref[0] += _dot_tn(pooled, (dyv * sc).astype(BF16))
        dpl = d_pooled_of(dyv)
        dpl_next = d_pooled_of(dyn_ref[...]) * (c < nc - 1).astype(F32)
        t = c * R + lax.broadcasted_iota(jnp.int32, (R, 1), 0)
        cnt = jnp.minimum(t + 1, w).astype(F32)
        tn = (c + 1) * R + lax.broadcasted_iota(jnp.int32, (HALO, 1), 0)
        cnt_next = jnp.minimum(tn + 1, w).astype(F32)
        d = lax.broadcasted_iota(jnp.int32, (R, R), 1) - lax.broadcasted_iota(jnp.int32, (R, R), 0)
        bt_cur = jnp.where((d >= 0) & (d < w), 1.0, 0.0).astype(BF16)
        dn = (
            lax.broadcasted_iota(jnp.int32, (R, HALO), 1)
            + R
            - lax.broadcasted_iota(jnp.int32, (R, HALO), 0)
        )
        bt_next = jnp.where(dn < w, 1.0, 0.0).astype(BF16)
        du = -dpl
        for part in _split2(dpl / cnt):
            du = du + _dot(bt_cur, part)
        for part in _split2(dpl_next / cnt_next):
            du = du + _dot(bt_next, part)
        du_ref[...] = du.astype(BF16)

    blk = pl.BlockSpec((R, GROUP_DIM), lambda g, c: (c, g))
    return pl.pallas_call(
        body,
        name="pool_bwd",
        grid=(N_GROUPS, nc),
        out_shape=[
            jax.ShapeDtypeStruct((S, D_POOL), BF16),
            jax.ShapeDtypeStruct((8, D_POOL), F32),
            jax.ShapeDtypeStruct((N_GROUPS, GROUP_DIM, GROUP_DIM), F32),
        ],
        in_specs=[
            blk,
            pl.BlockSpec((HALO, GROUP_DIM), lambda g, c: (jnp.minimum((c + 1) * (R // HALO), S // HALO - 1), g)),
            blk,
            pl.BlockSpec((1, GROUP_DIM, GROUP_DIM), lambda g, c: (g, 0, 0)),
            pl.BlockSpec((1, GROUP_DIM), lambda g, c: (0, g)),
        ],
        out_specs=[
            blk,
            pl.BlockSpec((8, GROUP_DIM), lambda g, c: (0, g)),
            pl.BlockSpec((1, GROUP_DIM, GROUP_DIM), lambda g, c: (g, 0, 0)),
        ],
        compiler_params=_params(("parallel", "arbitrary")),
    )(dy, dy, pooled, pool_w, scale)


def _bwd_in(du, dq, dk, dv, w_in, x, dh1, g1):
    S = x.shape[0]
    tm = min(512, S)

    def body(du_ref, dq_ref, dk_ref, dv_ref, w_ref, x_ref, dh1_ref, g_ref, dx_ref, dproj_ref, dg_ref):
        @pl.when(pl.program_id(0) == 0)
        def _():
            dg_ref[...] = jnp.zeros_like(dg_ref)

        parts = (du_ref[...], dq_ref[...], dk_ref[...], dv_ref[...])
        dhn = jnp.zeros((tm, D_MODEL), F32)
        for j in range(N_DEV):
            piece = parts[j // 2][:, 256 * (j % 2) : 256 * (j % 2 + 1)]
            dproj_ref[:, 256 * j : 256 * (j + 1)] = piece
            dhn = dhn + _dot_nt(piece, w_ref[j])
        xv = x_ref[...]
        r = _rstd(xv)
        dxn, dg = _rms_bwd(dhn, xv * r, r, g_ref[...])
        dx_ref[...] = dh1_ref[...] + dxn
        dg_ref[0:1, :] += dg

    half = pl.BlockSpec((tm, D_POOL), lambda i: (i, 0))
    full = pl.BlockSpec((tm, D_MODEL), lambda i: (i, 0))
    return pl.pallas_call(
        body,
        name="bwd_in",
        grid=(S // tm,),
        out_shape=[
            jax.ShapeDtypeStruct((S, D_MODEL), F32),
            jax.ShapeDtypeStruct((S, D_IN_PROJ), BF16),
            jax.ShapeDtypeStruct((8, D_MODEL), F32),
        ],
        in_specs=[half, half, half, half,
                  pl.BlockSpec((N_DEV, D_MODEL, 256), lambda i: (0, 0, 0)),
                  full, full, pl.BlockSpec((1, D_MODEL), lambda i: (0, 0))],
        out_specs=[full, pl.BlockSpec((tm, D_IN_PROJ), lambda i: (i, 0)),
                   pl.BlockSpec((8, D_MODEL), lambda i: (0, 0))],
        compiler_params=_params(("arbitrary",)),
    )(du, dq, dk, dv, w_in, x, dh1, g1)


def _rows(a):
    a = a.reshape(-1, LANES)
    pad = (-a.shape[0]) % 8
    return jnp.pad(a, ((0, pad), (0, 0))) if pad else a


SMALL = ("final_g", "norm2_g", "pool_out_g", "attn_out_g", "norm1_g", "pool_scale", "pool_w")


def kernel(x, norm1_g, w_in, pool_w, pool_scale, pool_out_g, attn_out_g, w_out, norm2_g, w_up, w_down, final_g, loss_target, m_norm1_g, m_w_in, m_pool_w, m_pool_scale, m_pool_out_g, m_attn_out_g, m_w_out, m_norm2_g, m_w_up, m_w_down, m_final_g, v_norm1_g, v_w_in, v_pool_w, v_pool_scale, v_pool_out_g, v_attn_out_g, v_w_out, v_norm2_g, v_w_up, v_w_down, v_final_g):
    S = x.shape[1]
    xs = x.reshape(S, D_MODEL)
    tgt = loss_target.reshape(S, D_MODEL)
    row = lambda a: a.reshape(1, -1)

    (w_in_g,) = _exchange([w_in.astype(BF16)], True, "gather_w_in")
    hn, u_pool, q, k, v = _fwd_in(xs, row(norm1_g), w_in_g)
    pooled, y_pool = _pool_fwd(u_pool, pool_w, row(pool_scale))
    y_attn, w_out_g, w_up_g, w_down_g = _attn_fwd(
        q, k, v, [w_out.astype(BF16), w_up.astype(BF16), w_down.astype(BF16)]
    )
    w_out_full = w_out_g.reshape(D_MODEL, D_MODEL)
    w_down_full = w_down_g.reshape(D_FF, D_MODEL)
    mixed, hn2, act, dup, dh2b, dh1, dh1b, dyp, dya, sg = _mlp_fwd_bwd(
        xs, y_pool, y_attn, tgt, row(pool_out_g), row(attn_out_g), row(norm2_g), row(final_g),
        w_out_full, w_up_g, w_down_full,
    )
    gp_down = _wgrad(act, dh2b, True, "wgrad_down")
    gp_up = _wgrad(hn2, dup, False, "wgrad_up")
    gp_out = _wgrad(mixed, dh1b, True, "wgrad_out")
    dq, dk, dv, land_out, land_up, land_down = _attn_bwd(q, k, v, dya, [gp_out, gp_up, gp_down])
    du, dsc, dpw = _pool_bwd(dyp, pooled, pool_w, row(pool_scale))
    dx, dproj, dg1 = _bwd_in(du, dq, dk, dv, w_in_g, xs, dh1, row(norm1_g))
    gp_in = _wgrad(hn, dproj, False, "wgrad_in")
    (land_in,) = _exchange([gp_in], False, "scatter_grad_w_in")
    big = {}
    for name, land, w, m, vv in (
        ("w_in", land_in, w_in, m_w_in, v_w_in),
        ("w_out", land_out, w_out, m_w_out, v_w_out),
        ("w_up", land_up, w_up, m_w_up, v_w_up),
        ("w_down", land_down, w_down, m_w_down, v_w_down),
    ):
        big[name] = _reduce_adam(land, w, m, vv, "reduce_adam_" + name)

    part = jnp.concatenate(
        [_rows(sg[0]), _rows(sg[1]), _rows(sg[2]), _rows(dg1[0]), _rows(dsc[0]), _rows(dpw), _rows(sg[3])], axis=0
    )
    given = dict(
        final_g=(final_g, m_final_g, v_final_g), norm2_g=(norm2_g, m_norm2_g, v_norm2_g),
        pool_out_g=(pool_out_g, m_pool_out_g, v_pool_out_g), attn_out_g=(attn_out_g, m_attn_out_g, v_attn_out_g),
        norm1_g=(norm1_g, m_norm1_g, v_norm1_g), pool_scale=(pool_scale, m_pool_scale, v_pool_scale),
        pool_w=(pool_w, m_pool_w, v_pool_w),
    )
    packed = []
    for i in range(3):
        pieces = [given["final_g"][i], given["norm2_g"][i],
                  jnp.concatenate([given["pool_out_g"][i], given["attn_out_g"][i]]),
                  given["norm1_g"][i], given["pool_scale"][i], given["pool_w"][i]]
        packed.append(jnp.concatenate([_rows(p) for p in pieces] + [jnp.zeros((8, LANES), F32)], axis=0))
    small = _allreduce_adam_small(part, *packed)

    def unpack(a):
        out, r = {}, 0
        for name, n in (("final_g", 1024), ("norm2_g", 1024), ("mix_g", 1024), ("norm1_g", 1024),
                        ("pool_scale", 512), ("pool_w", 65536)):
            nr = max(8, n // LANES)
            out[name] = a[r : r + n // LANES].reshape(-1)
            r += nr
        out["pool_out_g"], out["attn_out_g"] = out["mix_g"][:D_POOL], out["mix_g"][D_POOL:]
        out["pool_w"] = out["pool_w"].reshape(N_GROUPS, GROUP_DIM, GROUP_DIM)
        return out, a[r, 0]

    small_out = [unpack(a) for a in small]
    loss = small_out[0][1]
    order = ("norm1_g", "w_in", "pool_w", "pool_scale", "pool_out_g", "attn_out_g", "w_out", "norm2_g", "w_up",
             "w_down", "final_g")
    outs = [loss, dx.reshape(1, S, D_MODEL)]
    for i in range(4):
        for name in order:
            outs.append(big[name][i] if name in big else small_out[i][0][name])
    return tuple(outs)
```

```python
import functools

import jax
import jax.numpy as jnp
from jax import lax
from jax.experimental import pallas as pl
from jax.experimental.pallas import tpu as pltpu

F32 = jnp.float32
BF16 = jnp.bfloat16
MESH = pl.DeviceIdType.MESH

N_DEV = 8
D_MODEL = 1024
D_POOL = 512
D_ATTN = 512
N_GROUPS = 4
GROUP_DIM = 128
HEAD_DIM = 64
D_FF = 4096
D_IN_PROJ = 2048
EPS = 1e-6
HALO = 16
ATTN_TILE = 128
LANES = 128
EXP_UNDERFLOW = -104.0

ADAM_LR = 0.001
ADAM_B1 = 0.9
ADAM_B2 = 0.999
ADAM_EPS = 1e-08
ADAM_WD = 0.01
ADAM_STEP = 10

VMEM_LIMIT = 56 * 1024 * 1024


def _params(semantics=None, vmem=VMEM_LIMIT):
    return pltpu.CompilerParams(dimension_semantics=semantics, vmem_limit_bytes=vmem)


def _dot(a, b):
    return jnp.dot(a, b, preferred_element_type=F32)


def _dot_nt(a, b):
    return lax.dot_general(a, b, (((1,), (1,)), ((), ())), preferred_element_type=F32)


def _dot_tn(a, b):
    return lax.dot_general(a, b, (((0,), (0,)), ((), ())), preferred_element_type=F32)


def _split2(x):
    hi = x.astype(BF16)
    lo = (x - hi.astype(F32)).astype(BF16)
    return hi, lo


def _split3(x):
    hi = x.astype(BF16)
    r = x - hi.astype(F32)
    mid = r.astype(BF16)
    lo = (r - mid.astype(F32)).astype(BF16)
    return hi, mid, lo


def _rstd(h):
    return lax.rsqrt(jnp.mean(h * h, axis=-1, keepdims=True) + EPS)


def _rms_bwd(dout, hhat, r, g):
    dg = jnp.sum(dout * hhat, axis=0, keepdims=True)
    dxh = dout * g
    dh = r * (dxh - hhat * jnp.mean(dxh * hhat, axis=-1, keepdims=True))
    return dh, dg


def _my_index():
    return 4 * lax.axis_index("x") + 2 * lax.axis_index("y") + lax.axis_index("c")


def _peer(k):
    x, y, c = lax.axis_index("x"), lax.axis_index("y"), lax.axis_index("c")
    px = 1 - x if (k >> 2) & 1 else x
    py = 1 - y if (k >> 1) & 1 else y
    pc = 1 - c if k & 1 else c
    return (px, py, pc), 4 * px + 2 * py + pc


N_PEERS = N_DEV - 1
ANY_SPEC = pl.BlockSpec(memory_space=pl.ANY)


def _exchange_sems(n):
    return [
        pltpu.SemaphoreType.DMA((n * N_PEERS,)),
        pltpu.SemaphoreType.DMA((n * N_PEERS,)),
        pltpu.SemaphoreType.DMA((n,)),
    ]


def _exchange_shapes(blocks, gather):
    if gather:
        return [jax.ShapeDtypeStruct((N_DEV,) + b.shape, b.dtype) for b in blocks]
    return [jax.ShapeDtypeStruct(b.shape, b.dtype) for b in blocks]


def _direct_exchange(ins, outs, sems, gather):
    send_sems, recv_sems, local_sems = sems
    me = _my_index()
    own, sends, recvs = [], [], []
    for t in range(len(ins)):
        own.append(pltpu.make_async_copy(ins[t] if gather else ins[t].at[me], outs[t].at[me], local_sems.at[t]))
        for k in range(1, N_DEV):
            peer, peer_idx = _peer(k)
            src = ins[t] if gather else ins[t].at[peer_idx]
            for dst, bucket in ((outs[t].at[me], sends), (outs[t].at[peer_idx], recvs)):
                bucket.append(
                    pltpu.make_async_remote_copy(
                        src_ref=src,
                        dst_ref=dst,
                        send_sem=send_sems.at[t * N_PEERS + k - 1],
                        recv_sem=recv_sems.at[t * N_PEERS + k - 1],
                        device_id=peer,
                        device_id_type=MESH,
                    )
                )

    def start():
        for cp in own + sends:
            cp.start()

    def finish():
        for cp in recvs:
            cp.wait_recv()
        for cp in sends:
            cp.wait_send()
        for cp in own:
            cp.wait()

    return start, finish


def _two_level_gather(ins, outs, sems):
    send_sems, recv_sems, local_sems = sems
    x, y, c = lax.axis_index("x"), lax.axis_index("y"), lax.axis_index("c")
    me, sibling = (x, y, c), (x, y, 1 - c)
    chips = [(1 - x, y), (x, 1 - y), (1 - x, 1 - y)]

    def copy(t, k, block, to, from_input=False):
        slot = outs[t].at[4 * block[0] + 2 * block[1] + block[2]]
        return pltpu.make_async_remote_copy(
            src_ref=ins[t] if from_input else slot,
            dst_ref=slot,
            send_sem=send_sems.at[t * N_PEERS + k],
            recv_sem=recv_sems.at[t * N_PEERS + k],
            device_id=to,
            device_id_type=MESH,
        )

    arrays = range(len(ins))
    own = [pltpu.make_async_copy(ins[t], outs[t].at[4 * x + 2 * y + c], local_sems.at[t]) for t in arrays]
    first = [copy(t, 1 + j, me, (*chip, c), True) for t in arrays for j, chip in enumerate(chips)]
    first += [copy(t, 0, me, sibling, True) for t in arrays]
    passed = [(copy(t, 1 + j, (*chip, c), me), copy(t, 4 + j, (*chip, c), sibling))
              for t in arrays for j, chip in enumerate(chips)]
    last = [copy(t, 0, sibling, me) for t in arrays]
    last += [copy(t, 4 + j, (*chip, 1 - c), me) for t in arrays for j, chip in enumerate(chips)]

    def start():
        for cp in own + first:
            cp.start()

    def forward():
        for landed, onward in passed:
            landed.wait_recv()
            onward.start()

    def finish():
        for cp in last:
            cp.wait_recv()
        for cp in first + [onward for _, onward in passed]:
            cp.wait_send()
        for cp in own:
            cp.wait()

    return start, forward, finish


def _exchange(blocks, gather, name):
    n = len(blocks)

    def body(*refs):
        start, finish = _direct_exchange(refs[:n], refs[n : 2 * n], refs[2 * n :], gather)
        start()
        finish()

    return pl.pallas_call(
        body,
        name=name,
        out_shape=_exchange_shapes(blocks, gather),
        in_specs=[ANY_SPEC] * n,
        out_specs=[ANY_SPEC] * n,
        scratch_shapes=_exchange_sems(n),
        compiler_params=pltpu.CompilerParams(has_side_effects=True),
    )(*blocks)


def _adam(w, g, m, v):
    m2 = ADAM_B1 * m + (1.0 - ADAM_B1) * g
    v2 = ADAM_B2 * v + (1.0 - ADAM_B2) * jnp.square(g)
    m_hat = m2 / (1.0 - ADAM_B1**ADAM_STEP)
    v_hat = v2 / (1.0 - ADAM_B2**ADAM_STEP)
    delta = -ADAM_LR * (m_hat / (jnp.sqrt(v_hat) + ADAM_EPS) + ADAM_WD * w)
    return delta, m2, v2


def _allreduce_adam_small(part, w, m, v):
    rows = part.shape[0]

    def body(p_ref, w_ref, m_ref, v_ref, g_ref, d_ref, m2_ref, v2_ref, land, send_sems, recv_sems):
        me = _my_index()
        land[me] = p_ref[...]
        sends = []
        for k in range(1, N_DEV):
            peer, _ = _peer(k)
            cp = pltpu.make_async_remote_copy(
                src_ref=p_ref,
                dst_ref=land.at[me],
                send_sem=send_sems.at[k - 1],
                recv_sem=recv_sems.at[k - 1],
                device_id=peer,
                device_id_type=MESH,
            )
            cp.start()
            sends.append(cp)
        for k in range(1, N_DEV):
            peer, peer_idx = _peer(k)
            pltpu.make_async_remote_copy(
                src_ref=p_ref,
                dst_ref=land.at[peer_idx],
                send_sem=send_sems.at[k - 1],
                recv_sem=recv_sems.at[k - 1],
                device_id=peer,
                device_id_type=MESH,
            ).wait_recv()
        for cp in sends:
            cp.wait_send()
        g = land[0]
        for s in range(1, N_DEV):
            g = g + land[s]
        g_ref[...] = g
        delta, m2, v2 = _adam(w_ref[...], g, m_ref[...], v_ref[...])
        d_ref[...] = delta
        m2_ref[...] = m2
        v2_ref[...] = v2

    vm = pl.BlockSpec(memory_space=pltpu.VMEM)
    sds = jax.ShapeDtypeStruct((rows, LANES), F32)
    return pl.pallas_call(
        body,
        name="allreduce_adam_small",
        out_shape=[sds] * 4,
        in_specs=[vm] * 4,
        out_specs=[vm] * 4,
        scratch_shapes=[
            pltpu.VMEM((N_DEV, rows, LANES), F32),
            pltpu.SemaphoreType.DMA((N_DEV - 1,)),
            pltpu.SemaphoreType.DMA((N_DEV - 1,)),
        ],
        compiler_params=pltpu.CompilerParams(has_side_effects=True, vmem_limit_bytes=VMEM_LIMIT),
    )(part, w, m, v)


def _reduce_adam(land, w, m, v, name):
    rows, cols = w.shape
    tr = min(rows, max(8, (1 << 19) // cols))

    def body(l_ref, w_ref, m_ref, v_ref, g_ref, d_ref, m2_ref, v2_ref):
        g = l_ref[0].astype(F32)
        for s in range(1, N_DEV):
            g = g + l_ref[s].astype(F32)
        g_ref[...] = g
        delta, m2, v2 = _adam(w_ref[...], g, m_ref[...], v_ref[...])
        d_ref[...] = delta
        m2_ref[...] = m2
        v2_ref[...] = v2

    tile = pl.BlockSpec((tr, cols), lambda i: (i, 0))
    sds = jax.ShapeDtypeStruct((rows, cols), F32)
    return pl.pallas_call(
        body,
        name=name,
        grid=(rows // tr,),
        out_shape=[sds] * 4,
        in_specs=[pl.BlockSpec((N_DEV, tr, cols), lambda i: (0, i, 0)), tile, tile, tile],
        out_specs=[tile] * 4,
        compiler_params=_params(("parallel",)),
    )(land, w, m, v)


def _fwd_in(x, g1, w_in):
    S = x.shape[0]
    tm = min(512, S)

    def body(x_ref, g_ref, w_ref, hn_ref, u_ref, q_ref, k_ref, v_ref):
        xv = x_ref[...]
        hn = (xv * _rstd(xv) * g_ref[...]).astype(BF16)
        hn_ref[...] = hn
        outs = (u_ref, q_ref, k_ref, v_ref)
        for j in range(N_DEV):
            p = _dot(hn, w_ref[j])
            cols = slice(256 * (j % 2), 256 * (j % 2 + 1))
            if j // 2 == 0:
                u_ref[:, cols] = p
            elif j // 2 == 1:
                q_ref[:, cols] = (p * (HEAD_DIM**-0.5)).astype(BF16)
            else:
                outs[j // 2][:, cols] = p.astype(BF16)

    half = pl.BlockSpec((tm, D_POOL), lambda i: (i, 0))
    return pl.pallas_call(
        body,
        name="fwd_in",
        grid=(S // tm,),
        out_shape=[
            jax.ShapeDtypeStruct((S, D_MODEL), BF16),
            jax.ShapeDtypeStruct((S, D_POOL), F32),
            jax.ShapeDtypeStruct((S, D_ATTN), BF16),
            jax.ShapeDtypeStruct((S, D_ATTN), BF16),
            jax.ShapeDtypeStruct((S, D_ATTN), BF16),
        ],
        in_specs=[
            pl.BlockSpec((tm, D_MODEL), lambda i: (i, 0)),
            pl.BlockSpec((1, D_MODEL), lambda i: (0, 0)),
            pl.BlockSpec((N_DEV, D_MODEL, 256), lambda i: (0, 0, 0)),
        ],
        out_specs=[pl.BlockSpec((tm, D_MODEL), lambda i: (i, 0)), half, half, half, half],
        compiler_params=_params(("parallel",)),
    )(x, g1, w_in)


def _pool_chunk(S):
    return min(512, S)


def _pool_fwd(u, pool_w, scale):
    S = u.shape[0]
    R = _pool_chunk(S)

    def body(u_ref, up_ref, pw_ref, sc_ref, pooled_ref, y_ref):
        g = pl.program_id(0)
        c = pl.program_id(1)
        w = jnp.left_shift(jnp.int32(2), g)
        cur = u_ref[...]
        prev = up_ref[...] * (c > 0).astype(F32)
        d = lax.broadcasted_iota(jnp.int32, (R, R), 0) - lax.broadcasted_iota(jnp.int32, (R, R), 1)
        b_cur = jnp.where((d >= 0) & (d < w), 1.0, 0.0).astype(BF16)
        dp = (
            lax.broadcasted_iota(jnp.int32, (R, HALO), 0)
            + HALO
            - lax.broadcasted_iota(jnp.int32, (R, HALO), 1)
        )
        b_prev = jnp.where(dp < w, 1.0, 0.0).astype(BF16)
        wsum = jnp.zeros((R, GROUP_DIM), F32)
        for part in _split3(cur):
            wsum = wsum + _dot(b_cur, part)
        for part in _split3(prev):
            wsum = wsum + _dot(b_prev, part)
        t = c * R + lax.broadcasted_iota(jnp.int32, (R, 1), 0)
        count = jnp.minimum(t + 1, w).astype(F32)
        pooled = (wsum / count - cur).astype(BF16)
        pooled_ref[...] = pooled
        y_ref[...] = _dot(pooled, pw_ref[0].astype(BF16)) * sc_ref[...]

    blk = pl.BlockSpec((R, GROUP_DIM), lambda g, c: (c, g))
    return pl.pallas_call(
        body,
        name="pool_fwd",
        grid=(N_GROUPS, S // R),
        out_shape=[jax.ShapeDtypeStruct((S, D_POOL), BF16), jax.ShapeDtypeStruct((S, D_POOL), F32)],
        in_specs=[
            blk,
            pl.BlockSpec((HALO, GROUP_DIM), lambda g, c: (jnp.maximum(c * (R // HALO) - 1, 0), g)),
            pl.BlockSpec((1, GROUP_DIM, GROUP_DIM), lambda g, c: (g, 0, 0)),
            pl.BlockSpec((1, GROUP_DIM), lambda g, c: (0, g)),
        ],
        out_specs=[blk, blk],
        compiler_params=_params(("parallel", "parallel")),
    )(u, u, pool_w, scale)


ATTN_WINDOW = 3


def _band_ones(T):
    row = lax.broadcasted_iota(jnp.int32, (T, T), 0)
    col = lax.broadcasted_iota(jnp.int32, (T, T), 1)
    after = jnp.where(row > col, 1.0, 0.0).astype(BF16)
    before = jnp.where(row < col, 1.0, 0.0).astype(BF16)
    return jnp.concatenate([after, after], axis=0), jnp.concatenate([before, before], axis=0), col < row


def _log1m(z, mask):
    sp = jnp.log(1.0 + jnp.exp(-jnp.abs(z)))
    l1m = -jnp.maximum(z, 0.0) - sp
    return (l1m if mask is None else jnp.where(mask, l1m, 0.0)), sp


def _log_gates(qhs, kbs, masks, after2):
    zs = [[_dot_nt(qh, kb) for qh in qhs] for kb in kbs]
    terms = []
    for row, mask in zip(zs, masks):
        terms.append([])
        for z in row:
            l1m, sp = _log1m(z, mask)
            terms[-1].append(
                (jnp.minimum(z, 0.0) - sp, jnp.concatenate(_split2(l1m), axis=1), jnp.sum(l1m, axis=1, keepdims=True))
            )
    return [[(ls, _dot(split, after2), total) for ls, split, total in row] for row in terms]


def _tile_start(index, T):
    return index * T if isinstance(index, int) else pl.multiple_of(index * T, T)


def _weights(ls, tail, carry, mask):
    a = jnp.exp(ls + tail + carry)
    return a if mask is None else jnp.where(mask, a, 0.0)


def _reaches(carry):
    return jnp.max(carry) > EXP_UNDERFLOW


def _attn_fwd(q, k, v, shards):
    S = q.shape[0]
    T = ATTN_TILE
    nq = S // T
    n = len(shards)
    steps = D_ATTN // LANES

    def body(q_ref, k_ref, v_ref, *rest):
        o_ref = rest[n]
        step = pl.program_id(0)

        def gather():
            return _two_level_gather(rest[:n], rest[n + 1 : 2 * n + 1], rest[2 * n + 1 :])

        @pl.when(step == 0)
        def _():
            gather()[0]()

        @pl.when(step == steps - 2)
        def _():
            gather()[1]()

        lane = lax.broadcasted_iota(jnp.int32, (T, LANES), 1)
        heads = (lane < HEAD_DIM, lane >= HEAD_DIM)
        after2, _, causal = _band_ones(T)
        zero = jnp.zeros((T, 1), F32)

        def key_tile(k0):
            vb = v_ref[pl.ds(k0, T), :]
            return k_ref[pl.ds(k0, T), :], jnp.concatenate([jnp.where(m, vb, jnp.zeros_like(vb)) for m in heads], axis=0)

        def q_tile(qi, nb):
            q0 = _tile_start(qi, T)
            qb = q_ref[pl.ds(q0, T), :]
            qhs = [jnp.where(m, qb, jnp.zeros_like(qb)) for m in heads]
            tiles = [key_tile(_tile_start(qi - b, T)) for b in range(nb)]
            values = [v_rows for _, v_rows in tiles]
            gates = _log_gates(qhs, [kb for kb, _ in tiles], [causal] + [None] * (nb - 1), after2)
            carries, probs = [zero, zero], []
            for b in range(nb):
                for h in range(2):
                    ls, tail, total = gates[b][h]
                    probs.append(_weights(ls, tail, carries[h], causal if b == 0 else None).astype(BF16))
                    carries[h] = carries[h] + total
            acc = _dot(jnp.concatenate(probs, axis=1), jnp.concatenate(values, axis=0))

            def more(st):
                return (st[0] > 0) & _reaches(jnp.maximum(st[1], st[2]))

            def k_step(st):
                kj = st[0] - 1
                kb, v_rows = key_tile(pl.multiple_of(kj * T, T))
                new, probs = [], []
                for (ls, tail, total), carry in zip(_log_gates(qhs, [kb], [None], after2)[0], st[1:3]):
                    probs.append(_weights(ls, tail, carry, None).astype(BF16))
                    new.append(carry + total)
                return kj, new[0], new[1], st[3] + _dot(jnp.concatenate(probs, axis=1), v_rows)

            if not isinstance(qi, int):
                acc = lax.while_loop(more, k_step, (qi - (nb - 1), carries[0], carries[1], acc))[3]
            o_ref[pl.ds(q0, T), :] = acc

        for qi in range(min(ATTN_WINDOW - 1, nq)):
            q_tile(qi, qi + 1)

        def q_loop(qi, carry):
            q_tile(qi, ATTN_WINDOW)
            return carry

        lax.fori_loop(ATTN_WINDOW - 1, nq, q_loop, 0)

        @pl.when(step == steps - 1)
        def _():
            gather()[2]()

    blk = pl.BlockSpec((S, LANES), lambda p: (0, p))
    return pl.pallas_call(
        body,
        name="attn_fwd",
        grid=(steps,),
        out_shape=[jax.ShapeDtypeStruct((S, D_ATTN), F32)] + _exchange_shapes(shards, True),
        in_specs=[blk, blk, blk] + [ANY_SPEC] * n,
        out_specs=[blk] + [ANY_SPEC] * n,
        scratch_shapes=_exchange_sems(n),
        compiler_params=pltpu.CompilerParams(
            dimension_semantics=("arbitrary",), vmem_limit_bytes=VMEM_LIMIT, has_side_effects=True
        ),
    )(q, k, v, *shards)


SG_ROWS = 8


def _mlp_fwd_bwd(x, y_pool, y_attn, target, g_pool, g_attn, g2, gf, w_out, w_up, w_down):
    S = x.shape[0]
    tm = min(256, S)
    fc = D_FF // N_DEV

    def body(x_ref, yp_ref, ya_ref, t_ref, gp_ref, ga_ref, g2_ref, gf_ref, wo_hbm, wu_hbm, wd_hbm,
             mixed_ref, hn2_ref, act_ref, dup_ref, dh2_ref, dh1_ref, dh1b_ref, dyp_ref, dya_ref, sg_ref,
             wo, wu, wd, up_s, sems):
        @pl.when(pl.program_id(0) == 0)
        def _():
            copies = [
                pltpu.make_async_copy(wo_hbm, wo, sems.at[0]),
                pltpu.make_async_copy(wu_hbm, wu, sems.at[1]),
                pltpu.make_async_copy(wd_hbm, wd, sems.at[2]),
            ]
            for cp in copies:
                cp.start()
            for cp in copies:
                cp.wait()
            sg_ref[...] = jnp.zeros_like(sg_ref)

        gp, ga, g2v, gfv = gp_ref[...], ga_ref[...], g2_ref[...], gf_ref[...]
        yp, ya = yp_ref[...], ya_ref[...]
        rp, ra = _rstd(yp), _rstd(ya)
        yph, yah = yp * rp, ya * ra
        mixed = jnp.concatenate([(yph * gp).astype(BF16), (yah * ga).astype(BF16)], axis=1)
        mixed_ref[...] = mixed
        h1 = x_ref[...] + _dot(mixed, wo[...])
        r2 = _rstd(h1)
        h1h = h1 * r2
        hn2 = (h1h * g2v).astype(BF16)
        hn2_ref[...] = hn2
        h2 = h1
        for j in range(N_DEV):
            cols = slice(fc * j, fc * (j + 1))
            up = _dot(hn2, wu[j])
            up_s[:, cols] = up
            act = jnp.square(jnp.maximum(up, 0.0)).astype(BF16)
            act_ref[:, cols] = act
            h2 = h2 + _dot(act, wd[cols, :])
        rf = _rstd(h2)
        h2h = h2 * rf
        diff = h2h * gfv - t_ref[...]
        loss_rows = 0.5 * jnp.mean(diff * diff, axis=-1, keepdims=True)
        dy = diff * (1.0 / D_MODEL)
        dh2, dgf = _rms_bwd(dy, h2h, rf, gfv)
        dh2b = dh2.astype(BF16)
        dh2_ref[...] = dh2b
        dhn2 = jnp.zeros((tm, D_MODEL), F32)
        for j in range(N_DEV):
            cols = slice(fc * j, fc * (j + 1))
            dact = _dot_nt(dh2b, wd[cols, :])
            dup = (dact * (2.0 * jnp.maximum(up_s[:, cols], 0.0))).astype(BF16)
            dup_ref[:, cols] = dup
            dhn2 = dhn2 + _dot_nt(dup, wu[j])
        dh1n, dg2 = _rms_bwd(dhn2, h1h, r2, g2v)
        dh1 = dh2 + dh1n
        dh1_ref[...] = dh1
        dh1b = dh1.astype(BF16)
        dh1b_ref[...] = dh1b
        dmix = _dot_nt(dh1b, wo[...])
        dyp, dgp = _rms_bwd(dmix[:, :D_POOL], yph, rp, gp)
        dya, dga = _rms_bwd(dmix[:, D_POOL:], yah, ra, ga)
        dyp_ref[...] = dyp
        dya_ref[...] = dya
        sg_ref[0:1, :] += dgf
        sg_ref[1:2, :] += dg2
        sg_ref[2:3, :] += jnp.concatenate([dgp, dga], axis=1)
        sg_ref[3:4, :] += jnp.broadcast_to(jnp.sum(loss_rows, axis=0, keepdims=True), (1, D_MODEL))

    def tok(n):
        return pl.BlockSpec((tm, n), lambda i: (i, 0))

    def vec(n):
        return pl.BlockSpec((1, n), lambda i: (0, 0))

    any_spec = pl.BlockSpec(memory_space=pl.ANY)
    return pl.pallas_call(
        body,
        name="mlp_fwd_bwd",
        grid=(S // tm,),
        out_shape=[
            jax.ShapeDtypeStruct((S, D_MODEL), BF16),
            jax.ShapeDtypeStruct((S, D_MODEL), BF16),
            jax.ShapeDtypeStruct((S, D_FF), BF16),
            jax.ShapeDtypeStruct((S, D_FF), BF16),
            jax.ShapeDtypeStruct((S, D_MODEL), BF16),
            jax.ShapeDtypeStruct((S, D_MODEL), F32),
            jax.ShapeDtypeStruct((S, D_MODEL), BF16),
            jax.ShapeDtypeStruct((S, D_POOL), F32),
            jax.ShapeDtypeStruct((S, D_ATTN), F32),
            jax.ShapeDtypeStruct((SG_ROWS, D_MODEL), F32),
        ],
        in_specs=[tok(D_MODEL), tok(D_POOL), tok(D_ATTN), tok(D_MODEL), vec(D_POOL), vec(D_ATTN),
                  vec(D_MODEL), vec(D_MODEL), any_spec, any_spec, any_spec],
        out_specs=[tok(D_MODEL), tok(D_MODEL), tok(D_FF), tok(D_FF), tok(D_MODEL), tok(D_MODEL),
                   tok(D_MODEL), tok(D_POOL), tok(D_ATTN),
                   pl.BlockSpec((SG_ROWS, D_MODEL), lambda i: (0, 0))],
        scratch_shapes=[
            pltpu.VMEM((D_MODEL, D_MODEL), BF16),
            pltpu.VMEM((N_DEV, D_MODEL, fc), BF16),
            pltpu.VMEM((D_FF, D_MODEL), BF16),
            pltpu.VMEM((tm, D_FF), F32),
            pltpu.SemaphoreType.DMA((3,)),
        ],
        compiler_params=_params(("arbitrary",)),
    )(x, y_pool, y_attn, target, g_pool, g_attn, g2, gf, w_out, w_up, w_down)


def _wgrad(a, b, block_a, groups, name):
    S, ka = a.shape
    nb = b.shape[1]
    ts = min(1024, S)
    per = N_DEV // groups
    if block_a:
        ka //= groups
        blk = (ka // per, nb)
        a_spec = pl.BlockSpec((ts, ka), lambda g, s: (s, g))
        b_spec = pl.BlockSpec((ts, nb), lambda g, s: (s, 0))
    else:
        nb //= groups
        blk = (ka, nb // per)
        a_spec = pl.BlockSpec((ts, ka), lambda g, s: (s, 0))
        b_spec = pl.BlockSpec((ts, nb), lambda g, s: (s, g))
    steps = S // ts

    def body(a_ref, b_ref, o_ref, acc):
        s = pl.program_id(1)

        @pl.when(s == 0)
        def _():
            acc[...] = jnp.zeros_like(acc)

        acc[...] += _dot_tn(a_ref[...], b_ref[...])

        @pl.when(s == steps - 1)
        def _():
            for j in range(per):
                if block_a:
                    o_ref[j] = acc[blk[0] * j : blk[0] * (j + 1), :].astype(BF16)
                else:
                    o_ref[j] = acc[:, blk[1] * j : blk[1] * (j + 1)].astype(BF16)

    return pl.pallas_call(
        body,
        name=name,
        grid=(groups, steps),
        out_shape=jax.ShapeDtypeStruct((N_DEV,) + blk, BF16),
        in_specs=[a_spec, b_spec],
        out_specs=pl.BlockSpec((per,) + blk, lambda g, s: (g, 0, 0)),
        scratch_shapes=[pltpu.VMEM((ka, nb), F32)],
        compiler_params=_params(("parallel", "arbitrary")),
    )(a, b)


def _attn_bwd(q, k, v, do, partials):
    S = q.shape[0]
    T = ATTN_TILE
    nq = S // T
    n = len(partials)
    steps = D_ATTN // LANES

    def body(q_ref, k_ref, v_ref, do_ref, *rest):
        dq_ref, dk_ref, dv_ref = rest[n : n + 3]
        dk_acc, dv_acc, carry_s = rest[2 * n + 3 : 2 * n + 6]
        step = pl.program_id(0)

        def scatter():
            return _direct_exchange(rest[:n], rest[n + 3 : 2 * n + 3], rest[2 * n + 6 :], False)

        @pl.when(step == 0)
        def _():
            scatter()[0]()

        dk_acc[...] = jnp.zeros_like(dk_acc)
        dv_acc[...] = jnp.zeros_like(dv_acc)
        lane = lax.broadcasted_iota(jnp.int32, (T, LANES), 1)
        heads = (lane < HEAD_DIM, lane >= HEAD_DIM)
        after2, before2, causal = _band_ones(T)
        zero = jnp.zeros((T, 1), F32)

        def key_tile(k0):
            kb = k_ref[pl.ds(k0, T), :]
            k_rows = jnp.concatenate([jnp.where(m, kb, jnp.zeros_like(kb)) for m in heads], axis=0)
            return kb, v_ref[pl.ds(k0, T), :], k_rows

        def q_tile(qi, nb):
            q0 = _tile_start(qi, T)
            qb = q_ref[pl.ds(q0, T), :]
            dob = do_ref[pl.ds(q0, T), :]
            qhs = [jnp.where(m, qb, jnp.zeros_like(qb)) for m in heads]
            dohs = [jnp.where(m, dob, 0.0).astype(BF16) for m in heads]
            q_rows = jnp.concatenate(qhs, axis=0)
            do_rows = jnp.concatenate(dohs, axis=0)

            def grad_tiles(starts, tiles, masks, carry_in, gates, st):
                nt = len(tiles)
                das = [[_dot_nt(doh, vb) for doh in dohs] for _, vb, _ in tiles]
                probs, gs = [], []
                for b in range(nt):
                    probs.append([_weights(gates[b][h][0], gates[b][h][1], carry_in[b][h], masks[b]) for h in range(2)])
                    gs.append([probs[b][h] * das[b][h] for h in range(2)])
                before = [[_dot(jnp.concatenate(_split2(g), axis=1), before2) for g in row] for row in gs]
                g_left, dzs = [st[0], st[1]], [None] * nt
                for b in reversed(range(nt)):
                    dzs[b] = []
                    for h in range(2):
                        sig = jnp.exp(gates[b][h][0])
                        dz = gs[b][h] * (1.0 - sig) - sig * (g_left[h] + before[b][h])
                        if masks[b] is not None:
                            dz = jnp.where(masks[b], dz, 0.0)
                        dzs[b].append(dz.astype(BF16))
                        g_left[h] = g_left[h] + jnp.sum(gs[b][h], axis=1, keepdims=True)
                dq = st[2] + _dot(
                    jnp.concatenate([dz for row in dzs for dz in row], axis=1),
                    jnp.concatenate([k_rows for _, _, k_rows in tiles], axis=0),
                )
                for b in range(nt):
                    dk_acc[pl.ds(starts[b], T), :] += _dot_tn(jnp.concatenate(dzs[b], axis=0), q_rows)
                    dv_acc[pl.ds(starts[b], T), :] += _dot_tn(
                        jnp.concatenate([a.astype(BF16) for a in probs[b]], axis=0), do_rows
                    )
                return g_left[0], g_left[1], dq

            starts = [_tile_start(qi - b, T) for b in range(nb)]
            tiles = [key_tile(k0) for k0 in starts]
            masks = [causal] + [None] * (nb - 1)
            gates = _log_gates(qhs, [kb for kb, _, _ in tiles], masks, after2)
            carries, carry_in = [zero, zero], []
            for b in range(nb):
                carry_in.append(list(carries))
                carries = [carries[h] + gates[b][h][2] for h in range(2)]

            st = (zero, zero, jnp.zeros((T, LANES), F32))
            if not isinstance(qi, int):
                k_left = qi - (nb - 1)

                def more(st):
                    return (st[0] > 0) & _reaches(jnp.maximum(st[1], st[2]))

                def right_to_left(st):
                    kj = st[0] - 1
                    carry_s[0, kj] = st[1]
                    carry_s[1, kj] = st[2]
                    kb = k_ref[pl.ds(pl.multiple_of(kj * T, T), T), :]
                    sums = [jnp.sum(_log1m(_dot_nt(qh, kb), None)[0], axis=1, keepdims=True) for qh in qhs]
                    return kj, st[1] + sums[0], st[2] + sums[1]

                k_first = lax.while_loop(more, right_to_left, (k_left, carries[0], carries[1]))[0]

                def left_to_right(kj, st):
                    k0 = pl.multiple_of(kj * T, T)
                    tile = key_tile(k0)
                    left_gates = _log_gates(qhs, [tile[0]], [None], after2)
                    return grad_tiles([k0], [tile], [None], [[carry_s[0, kj], carry_s[1, kj]]], left_gates, st)

                st = lax.fori_loop(k_first, k_left, left_to_right, st)
            st = grad_tiles(starts, tiles, masks, carry_in, gates, st)
            dq_ref[pl.ds(q0, T), :] = (st[2] * (HEAD_DIM**-0.5)).astype(BF16)

        for qi in range(min(ATTN_WINDOW - 1, nq)):
            q_tile(qi, qi + 1)

        def q_loop(qi, carry):
            q_tile(qi, ATTN_WINDOW)
            return carry

        lax.fori_loop(ATTN_WINDOW - 1, nq, q_loop, 0)
        dk_ref[...] = dk_acc[...].astype(BF16)
        dv_ref[...] = dv_acc[...].astype(BF16)

        @pl.when(step == steps - 1)
        def _():
            scatter()[1]()

    blk = pl.BlockSpec((S, LANES), lambda p: (0, p))
    sds = jax.ShapeDtypeStruct((S, D_ATTN), BF16)
    return pl.pallas_call(
        body,
        name="attn_bwd",
        grid=(steps,),
        out_shape=[sds, sds, sds] + _exchange_shapes(partials, False),
        in_specs=[blk] * 4 + [ANY_SPEC] * n,
        out_specs=[blk] * 3 + [ANY_SPEC] * n,
        scratch_shapes=[pltpu.VMEM((S, LANES), F32), pltpu.VMEM((S, LANES), F32), pltpu.VMEM((2, nq, T, 1), F32)]
        + _exchange_sems(n),
        compiler_params=pltpu.CompilerParams(
            dimension_semantics=("arbitrary",), vmem_limit_bytes=VMEM_LIMIT, has_side_effects=True
        ),
    )(q, k, v, do, *partials)


def _pool_bwd(dy, pooled, pool_w, scale):
    S = dy.shape[0]
    R = _pool_chunk(S)
    nc = S // R

    def body(dy_ref, dyn_ref, pooled_ref, pw_ref, sc_ref, du_ref, dsc_ref, dpw_ref):
        g = pl.program_id(0)
        c = pl.program_id(1)
        w = jnp.left_shift(jnp.int32(2), g)
        pw = pw_ref[0].astype(BF16)
        sc = sc_ref[...]

        @pl.when(c == 0)
        def _():
            dsc_ref[...] = jnp.zeros_like(dsc_ref)
            dpw_ref[...] = jnp.zeros_like(dpw_ref)

        def d_pooled_of(dyv):
            return _dot_nt((dyv * sc).astype(BF16), pw)

        dyv = dy_ref[...]
        pooled = pooled_ref[...]
        mapped = _dot(pooled, pw)
        dsc_ref[0:1, :] += jnp.sum(dyv * mapped, axis=0, keepdims=True)
        dpw_ref[0] += _dot_tn(pooled, (dyv * sc).astype(BF16))
        dpl = d_pooled_of(dyv)
        dpl_next = d_pooled_of(dyn_ref[...]) * (c < nc - 1).astype(F32)
        t = c * R + lax.broadcasted_iota(jnp.int32, (R, 1), 0)
        cnt = jnp.minimum(t + 1, w).astype(F32)
        tn = (c + 1) * R + lax.broadcasted_iota(jnp.int32, (HALO, 1), 0)
        cnt_next = jnp.minimum(tn + 1, w).astype(F32)
        d = lax.broadcasted_iota(jnp.int32, (R, R), 1) - lax.broadcasted_iota(jnp.int32, (R, R), 0)
        bt_cur = jnp.where((d >= 0) & (d < w), 1.0, 0.0).astype(BF16)
        dn = (
            lax.broadcasted_iota(jnp.int32, (R, HALO), 1)
            + R
            - lax.broadcasted_iota(jnp.int32, (R, HALO), 0)
        )
        bt_next = jnp.where(dn < w, 1.0, 0.0).astype(BF16)
        du = -dpl
        for part in _split2(dpl / cnt):
            du = du + _dot(bt_cur, part)
        for part in _split2(dpl_next / cnt_next):
            du = du + _dot(bt_next, part)
        du_ref[...] = du.astype(BF16)

    blk = pl.BlockSpec((R, GROUP_DIM), lambda g, c: (c, g))
    return pl.pallas_call(
        body,
        name="pool_bwd",
        grid=(N_GROUPS, nc),
        out_shape=[
            jax.ShapeDtypeStruct((S, D_POOL), BF16),
            jax.ShapeDtypeStruct((8, D_POOL), F32),
            jax.ShapeDtypeStruct((N_GROUPS, GROUP_DIM, GROUP_DIM), F32),
        ],
        in_specs=[
            blk,
            pl.BlockSpec((HALO, GROUP_DIM), lambda g, c: (jnp.minimum((c + 1) * (R // HALO), S // HALO - 1), g)),
            blk,
            pl.BlockSpec((1, GROUP_DIM, GROUP_DIM), lambda g, c: (g, 0, 0)),
            pl.BlockSpec((1, GROUP_DIM), lambda g, c: (0, g)),
        ],
        out_specs=[
            blk,
            pl.BlockSpec((8, GROUP_DIM), lambda g, c: (0, g)),
            pl.BlockSpec((1, GROUP_DIM, GROUP_DIM), lambda g, c: (g, 0, 0)),
        ],
        compiler_params=_params(("parallel", "arbitrary")),
    )(dy, dy, pooled, pool_w, scale)


def _bwd_in(du, dq, dk, dv, w_in, x, dh1, g1):
    S = x.shape[0]
    tm = min(512, S)

    def body(du_ref, dq_ref, dk_ref, dv_ref, w_ref, x_ref, dh1_ref, g_ref, dx_ref, dproj_ref, dg_ref):
        @pl.when(pl.program_id(0) == 0)
        def _():
            dg_ref[...] = jnp.zeros_like(dg_ref)

        parts = (du_ref[...], dq_ref[...], dk_ref[...], dv_ref[...])
        dhn = jnp.zeros((tm, D_MODEL), F32)
        for j in range(N_DEV):
            piece = parts[j // 2][:, 256 * (j % 2) : 256 * (j % 2 + 1)]
            dproj_ref[:, 256 * j : 256 * (j + 1)] = piece
            dhn = dhn + _dot_nt(piece, w_ref[j])
        xv = x_ref[...]
        r = _rstd(xv)
        dxn, dg = _rms_bwd(dhn, xv * r, r, g_ref[...])
        dx_ref[...] = dh1_ref[...] + dxn
        dg_ref[0:1, :] += dg

    half = pl.BlockSpec((tm, D_POOL), lambda i: (i, 0))
    full = pl.BlockSpec((tm, D_MODEL), lambda i: (i, 0))
    return pl.pallas_call(
        body,
        name="bwd_in",
        grid=(S // tm,),
        out_shape=[
            jax.ShapeDtypeStruct((S, D_MODEL), F32),
            jax.ShapeDtypeStruct((S, D_IN_PROJ), BF16),
            jax.ShapeDtypeStruct((8, D_MODEL), F32),
        ],
        in_specs=[half, half, half, half,
                  pl.BlockSpec((N_DEV, D_MODEL, 256), lambda i: (0, 0, 0)),
                  full, full, pl.BlockSpec((1, D_MODEL), lambda i: (0, 0))],
        out_specs=[full, pl.BlockSpec((tm, D_IN_PROJ), lambda i: (i, 0)),
                   pl.BlockSpec((8, D_MODEL), lambda i: (0, 0))],
        compiler_params=_params(("arbitrary",)),
    )(du, dq, dk, dv, w_in, x, dh1, g1)


def _rows(a):
    a = a.reshape(-1, LANES)
    pad = (-a.shape[0]) % 8
    return jnp.pad(a, ((0, pad), (0, 0))) if pad else a


SMALL = ("final_g", "norm2_g", "pool_out_g", "attn_out_g", "norm1_g", "pool_scale", "pool_w")


def kernel(x, norm1_g, w_in, pool_w, pool_scale, pool_out_g, attn_out_g, w_out, norm2_g, w_up, w_down, final_g, loss_target, m_norm1_g, m_w_in, m_pool_w, m_pool_scale, m_pool_out_g, m_attn_out_g, m_w_out, m_norm2_g, m_w_up, m_w_down, m_final_g, v_norm1_g, v_w_in, v_pool_w, v_pool_scale, v_pool_out_g, v_attn_out_g, v_w_out, v_norm2_g, v_w_up, v_w_down, v_final_g):
    S = x.shape[1]
    xs = x.reshape(S, D_MODEL)
    tgt = loss_target.reshape(S, D_MODEL)
    row = lambda a: a.reshape(1, -1)

    (w_in_g,) = _exchange([w_in.astype(BF16)], True, "gather_w_in")
    hn, u_pool, q, k, v = _fwd_in(xs, row(norm1_g), w_in_g)
    pooled, y_pool = _pool_fwd(u_pool, pool_w, row(pool_scale))
    y_attn, w_out_g, w_up_g, w_down_g = _attn_fwd(
        q, k, v, [w_out.astype(BF16), w_up.astype(BF16), w_down.astype(BF16)]
    )
    w_out_full = w_out_g.reshape(D_MODEL, D_MODEL)
    w_down_full = w_down_g.reshape(D_FF, D_MODEL)
    mixed, hn2, act, dup, dh2b, dh1, dh1b, dyp, dya, sg = _mlp_fwd_bwd(
        xs, y_pool, y_attn, tgt, row(pool_out_g), row(attn_out_g), row(norm2_g), row(final_g),
        w_out_full, w_up_g, w_down_full,
    )
    gp_down = _wgrad(act, dh2b, True, 2, "wgrad_down")
    gp_up = _wgrad(hn2, dup, False, 2, "wgrad_up")
    gp_out = _wgrad(mixed, dh1b, True, 1, "wgrad_out")
    dq, dk, dv, land_out, land_up, land_down = _attn_bwd(q, k, v, dya, [gp_out, gp_up, gp_down])
    du, dsc, dpw = _pool_bwd(dyp, pooled, pool_w, row(pool_scale))
    dx, dproj, dg1 = _bwd_in(du, dq, dk, dv, w_in_g, xs, dh1, row(norm1_g))
    gp_in = _wgrad(hn, dproj, False, 1, "wgrad_in")
    (land_in,) = _exchange([gp_in], False, "scatter_grad_w_in")
    big = {}
    for name, land, w, m, vv in (
        ("w_in", land_in, w_in, m_w_in, v_w_in),
        ("w_out", land_out, w_out, m_w_out, v_w_out),
        ("w_up", land_up, w_up, m_w_up, v_w_up),
        ("w_down", land_down, w_down, m_w_down, v_w_down),
    ):
        big[name] = _reduce_adam(land, w, m, vv, "reduce_adam_" + name)

    part = jnp.concatenate(
        [_rows(sg[0]), _rows(sg[1]), _rows(sg[2]), _rows(dg1[0]), _rows(dsc[0]), _rows(dpw), _rows(sg[3])], axis=0
    )
    given = dict(
        final_g=(final_g, m_final_g, v_final_g), norm2_g=(norm2_g, m_norm2_g, v_norm2_g),
        pool_out_g=(pool_out_g, m_pool_out_g, v_pool_out_g), attn_out_g=(attn_out_g, m_attn_out_g, v_attn_out_g),
        norm1_g=(norm1_g, m_norm1_g, v_norm1_g), pool_scale=(pool_scale, m_pool_scale, v_pool_scale),
        pool_w=(pool_w, m_pool_w, v_pool_w),
    )
    packed = []
    for i in range(3):
        pieces = [given["final_g"][i], given["norm2_g"][i],
                  jnp.concatenate([given["pool_out_g"][i], given["attn_out_g"][i]]),
                  given["norm1_g"][i], given["pool_scale"][i], given["pool_w"][i]]
        packed.append(jnp.concatenate([_rows(p) for p in pieces] + [jnp.zeros((8, LANES), F32)], axis=0))
    small = _allreduce_adam_small(part, *packed)

    def unpack(a):
        out, r = {}, 0
        for name, n in (("final_g", 1024), ("norm2_g", 1024), ("mix_g", 1024), ("norm1_g", 1024),
                        ("pool_scale", 512), ("pool_w", 65536)):
            nr = max(8, n // LANES)
            out[name] = a[r : r + n // LANES].reshape(-1)
            r += nr
        out["pool_out_g"], out["attn_out_g"] = out["mix_g"][:D_POOL], out["mix_g"][D_POOL:]
        out["pool_w"] = out["pool_w"].reshape(N_GROUPS, GROUP_DIM, GROUP_DIM)
        return out, a[r, 0]

    small_out = [unpack(a) for a in small]
    loss = small_out[0][1]
    order = ("norm1_g", "w_in", "pool_w", "pool_scale", "pool_out_g", "attn_out_g", "w_out", "norm2_g", "w_up",
             "w_down", "final_g")
    outs = [loss, dx.reshape(1, S, D_MODEL)]
    for i in range(4):
        for name in order:
            outs.append(big[name][i] if name in big else small_out[i][0][name])
    return tuple(outs)
```

```python
import functools

import jax
import jax.numpy as jnp
from jax import lax
from jax.experimental import pallas as pl
from jax.experimental.pallas import tpu as pltpu

F32 = jnp.float32
BF16 = jnp.bfloat16
MESH = pl.DeviceIdType.MESH

N_DEV = 8
D_MODEL = 1024
D_POOL = 512
D_ATTN = 512
N_GROUPS = 4
GROUP_DIM = 128
HEAD_DIM = 64
D_FF = 4096
D_IN_PROJ = 2048
EPS = 1e-6
HALO = 16
ATTN_TILE = 128
LANES = 128
EXP_UNDERFLOW = -104.0

ADAM_LR = 0.001
ADAM_B1 = 0.9
ADAM_B2 = 0.999
ADAM_EPS = 1e-08
ADAM_WD = 0.01
ADAM_STEP = 10

VMEM_LIMIT = 56 * 1024 * 1024


def _params(semantics=None, vmem=VMEM_LIMIT):
    return pltpu.CompilerParams(dimension_semantics=semantics, vmem_limit_bytes=vmem)


def _dot(a, b):
    return jnp.dot(a, b, preferred_element_type=F32)


def _dot_nt(a, b):
    return lax.dot_general(a, b, (((1,), (1,)), ((), ())), preferred_element_type=F32)


def _dot_tn(a, b):
    return lax.dot_general(a, b, (((0,), (0,)), ((), ())), preferred_element_type=F32)


def _split2(x):
    hi = x.astype(BF16)
    lo = (x - hi.astype(F32)).astype(BF16)
    return hi, lo


def _split3(x):
    hi = x.astype(BF16)
    r = x - hi.astype(F32)
    mid = r.astype(BF16)
    lo = (r - mid.astype(F32)).astype(BF16)
    return hi, mid, lo


def _rstd(h):
    return lax.rsqrt(jnp.mean(h * h, axis=-1, keepdims=True) + EPS)


def _rms_bwd(dout, hhat, r, g):
    dg = jnp.sum(dout * hhat, axis=0, keepdims=True)
    dxh = dout * g
    dh = r * (dxh - hhat * jnp.mean(dxh * hhat, axis=-1, keepdims=True))
    return dh, dg


def _my_index():
    return 4 * lax.axis_index("x") + 2 * lax.axis_index("y") + lax.axis_index("c")


def _peer(k):
    x, y, c = lax.axis_index("x"), lax.axis_index("y"), lax.axis_index("c")
    px = 1 - x if (k >> 2) & 1 else x
    py = 1 - y if (k >> 1) & 1 else y
    pc = 1 - c if k & 1 else c
    return (px, py, pc), 4 * px + 2 * py + pc


N_PEERS = N_DEV - 1
ANY_SPEC = pl.BlockSpec(memory_space=pl.ANY)


def _exchange_sems(n):
    return [
        pltpu.SemaphoreType.DMA((n * N_PEERS,)),
        pltpu.SemaphoreType.DMA((n * N_PEERS,)),
        pltpu.SemaphoreType.DMA((n,)),
    ]


def _exchange_shapes(blocks, gather):
    if gather:
        return [jax.ShapeDtypeStruct((N_DEV,) + b.shape, b.dtype) for b in blocks]
    return [jax.ShapeDtypeStruct(b.shape, b.dtype) for b in blocks]


def _direct_exchange(ins, outs, sems, gather):
    send_sems, recv_sems, local_sems = sems
    me = _my_index()
    own, sends, recvs = [], [], []
    for t in range(len(ins)):
        own.append(pltpu.make_async_copy(ins[t] if gather else ins[t].at[me], outs[t].at[me], local_sems.at[t]))
        for k in range(1, N_DEV):
            peer, peer_idx = _peer(k)
            src = ins[t] if gather else ins[t].at[peer_idx]
            for dst, bucket in ((outs[t].at[me], sends), (outs[t].at[peer_idx], recvs)):
                bucket.append(
                    pltpu.make_async_remote_copy(
                        src_ref=src,
                        dst_ref=dst,
                        send_sem=send_sems.at[t * N_PEERS + k - 1],
                        recv_sem=recv_sems.at[t * N_PEERS + k - 1],
                        device_id=peer,
                        device_id_type=MESH,
                    )
                )

    def start():
        for cp in own + sends:
            cp.start()

    def finish():
        for cp in recvs:
            cp.wait_recv()
        for cp in sends:
            cp.wait_send()
        for cp in own:
            cp.wait()

    return start, finish


def _two_level_gather(ins, outs, sems):
    send_sems, recv_sems, local_sems = sems
    x, y, c = lax.axis_index("x"), lax.axis_index("y"), lax.axis_index("c")
    me, sibling = (x, y, c), (x, y, 1 - c)
    chips = [(1 - x, y), (x, 1 - y), (1 - x, 1 - y)]

    def copy(t, k, block, to, from_input=False):
        slot = outs[t].at[4 * block[0] + 2 * block[1] + block[2]]
        return pltpu.make_async_remote_copy(
            src_ref=ins[t] if from_input else slot,
            dst_ref=slot,
            send_sem=send_sems.at[t * N_PEERS + k],
            recv_sem=recv_sems.at[t * N_PEERS + k],
            device_id=to,
            device_id_type=MESH,
        )

    arrays = range(len(ins))
    own = [pltpu.make_async_copy(ins[t], outs[t].at[4 * x + 2 * y + c], local_sems.at[t]) for t in arrays]
    first = [copy(t, 1 + j, me, (*chip, c), True) for t in arrays for j, chip in enumerate(chips)]
    first += [copy(t, 0, me, sibling, True) for t in arrays]
    passed = [(copy(t, 1 + j, (*chip, c), me), copy(t, 4 + j, (*chip, c), sibling))
              for t in arrays for j, chip in enumerate(chips)]
    last = [copy(t, 0, sibling, me) for t in arrays]
    last += [copy(t, 4 + j, (*chip, 1 - c), me) for t in arrays for j, chip in enumerate(chips)]

    def start():
        for cp in own + first:
            cp.start()

    def forward():
        for landed, onward in passed:
            landed.wait_recv()
            onward.start()

    def finish():
        for cp in last:
            cp.wait_recv()
        for cp in first + [onward for _, onward in passed]:
            cp.wait_send()
        for cp in own:
            cp.wait()

    return start, forward, finish


def _exchange(blocks, gather, name):
    n = len(blocks)

    def body(*refs):
        if gather:
            stages = _two_level_gather(refs[:n], refs[n : 2 * n], refs[2 * n :])
        else:
            stages = _direct_exchange(refs[:n], refs[n : 2 * n], refs[2 * n :], False)
        for stage in stages:
            stage()

    return pl.pallas_call(
        body,
        name=name,
        out_shape=_exchange_shapes(blocks, gather),
        in_specs=[ANY_SPEC] * n,
        out_specs=[ANY_SPEC] * n,
        scratch_shapes=_exchange_sems(n),
        compiler_params=pltpu.CompilerParams(has_side_effects=True),
    )(*blocks)


def _adam(w, g, m, v):
    m2 = ADAM_B1 * m + (1.0 - ADAM_B1) * g
    v2 = ADAM_B2 * v + (1.0 - ADAM_B2) * jnp.square(g)
    m_hat = m2 / (1.0 - ADAM_B1**ADAM_STEP)
    v_hat = v2 / (1.0 - ADAM_B2**ADAM_STEP)
    delta = -ADAM_LR * (m_hat / (jnp.sqrt(v_hat) + ADAM_EPS) + ADAM_WD * w)
    return delta, m2, v2


def _allreduce_adam_small(part, w, m, v):
    rows = part.shape[0]

    def body(p_ref, w_ref, m_ref, v_ref, g_ref, d_ref, m2_ref, v2_ref, land, send_sems, recv_sems):
        me = _my_index()
        land[me] = p_ref[...]
        sends = []
        for k in range(1, N_DEV):
            peer, _ = _peer(k)
            cp = pltpu.make_async_remote_copy(
                src_ref=p_ref,
                dst_ref=land.at[me],
                send_sem=send_sems.at[k - 1],
                recv_sem=recv_sems.at[k - 1],
                device_id=peer,
                device_id_type=MESH,
            )
            cp.start()
            sends.append(cp)
        for k in range(1, N_DEV):
            peer, peer_idx = _peer(k)
            pltpu.make_async_remote_copy(
                src_ref=p_ref,
                dst_ref=land.at[peer_idx],
                send_sem=send_sems.at[k - 1],
                recv_sem=recv_sems.at[k - 1],
                device_id=peer,
                device_id_type=MESH,
            ).wait_recv()
        for cp in sends:
            cp.wait_send()
        g = land[0]
        for s in range(1, N_DEV):
            g = g + land[s]
        g_ref[...] = g
        delta, m2, v2 = _adam(w_ref[...], g, m_ref[...], v_ref[...])
        d_ref[...] = delta
        m2_ref[...] = m2
        v2_ref[...] = v2

    vm = pl.BlockSpec(memory_space=pltpu.VMEM)
    sds = jax.ShapeDtypeStruct((rows, LANES), F32)
    return pl.pallas_call(
        body,
        name="allreduce_adam_small",
        out_shape=[sds] * 4,
        in_specs=[vm] * 4,
        out_specs=[vm] * 4,
        scratch_shapes=[
            pltpu.VMEM((N_DEV, rows, LANES), F32),
            pltpu.SemaphoreType.DMA((N_DEV - 1,)),
            pltpu.SemaphoreType.DMA((N_DEV - 1,)),
        ],
        compiler_params=pltpu.CompilerParams(has_side_effects=True, vmem_limit_bytes=VMEM_LIMIT),
    )(part, w, m, v)


def _reduce_adam(land, w, m, v, name):
    rows, cols = w.shape
    tr = min(rows, max(8, (1 << 19) // cols))

    def body(l_ref, w_ref, m_ref, v_ref, g_ref, d_ref, m2_ref, v2_ref):
        g = l_ref[0].astype(F32)
        for s in range(1, N_DEV):
            g = g + l_ref[s].astype(F32)
        g_ref[...] = g
        delta, m2, v2 = _adam(w_ref[...], g, m_ref[...], v_ref[...])
        d_ref[...] = delta
        m2_ref[...] = m2
        v2_ref[...] = v2

    tile = pl.BlockSpec((tr, cols), lambda i: (i, 0))
    sds = jax.ShapeDtypeStruct((rows, cols), F32)
    return pl.pallas_call(
        body,
        name=name,
        grid=(rows // tr,),
        out_shape=[sds] * 4,
        in_specs=[pl.BlockSpec((N_DEV, tr, cols), lambda i: (0, i, 0)), tile, tile, tile],
        out_specs=[tile] * 4,
        compiler_params=_params(("parallel",)),
    )(land, w, m, v)


def _fwd_in(x, g1, w_in):
    S = x.shape[0]
    tm = min(512, S)

    def body(x_ref, g_ref, w_ref, hn_ref, u_ref, q_ref, k_ref, v_ref):
        xv = x_ref[...]
        hn = (xv * _rstd(xv) * g_ref[...]).astype(BF16)
        hn_ref[...] = hn
        outs = (u_ref, q_ref, k_ref, v_ref)
        for j in range(N_DEV):
            p = _dot(hn, w_ref[j])
            cols = slice(256 * (j % 2), 256 * (j % 2 + 1))
            if j // 2 == 0:
                u_ref[:, cols] = p
            elif j // 2 == 1:
                q_ref[:, cols] = (p * (HEAD_DIM**-0.5)).astype(BF16)
            else:
                outs[j // 2][:, cols] = p.astype(BF16)

    half = pl.BlockSpec((tm, D_POOL), lambda i: (i, 0))
    return pl.pallas_call(
        body,
        name="fwd_in",
        grid=(S // tm,),
        out_shape=[
            jax.ShapeDtypeStruct((S, D_MODEL), BF16),
            jax.ShapeDtypeStruct((S, D_POOL), F32),
            jax.ShapeDtypeStruct((S, D_ATTN), BF16),
            jax.ShapeDtypeStruct((S, D_ATTN), BF16),
            jax.ShapeDtypeStruct((S, D_ATTN), BF16),
        ],
        in_specs=[
            pl.BlockSpec((tm, D_MODEL), lambda i: (i, 0)),
            pl.BlockSpec((1, D_MODEL), lambda i: (0, 0)),
            pl.BlockSpec((N_DEV, D_MODEL, 256), lambda i: (0, 0, 0)),
        ],
        out_specs=[pl.BlockSpec((tm, D_MODEL), lambda i: (i, 0)), half, half, half, half],
        compiler_params=_params(("parallel",)),
    )(x, g1, w_in)


POOL_CHUNK = 256
_POOL_W_SPEC = pl.BlockSpec((1, GROUP_DIM, GROUP_DIM), lambda g: (g, 0, 0))
_POOL_SCALE_SPEC = pl.BlockSpec((1, GROUP_DIM), lambda g: (0, g))


def _lane_sum(wide, n):
    out = wide[:, :GROUP_DIM]
    for i in range(1, n):
        out = out + wide[:, GROUP_DIM * i : GROUP_DIM * (i + 1)]
    return out


def _pool_fwd_group(u_ref, pw_ref, sc_ref, pooled_ref, y_ref, g):
    S = u_ref.shape[0]
    R = min(POOL_CHUNK, S)
    w = jnp.left_shift(jnp.int32(2), g)
    d = lax.broadcasted_iota(jnp.int32, (R, R), 0) - lax.broadcasted_iota(jnp.int32, (R, R), 1)
    b_cur = jnp.where((d >= 0) & (d < w), 1.0, 0.0).astype(BF16)
    dp = lax.broadcasted_iota(jnp.int32, (R, HALO), 0) + HALO - lax.broadcasted_iota(jnp.int32, (R, HALO), 1)
    b_prev = jnp.where(dp < w, 1.0, 0.0).astype(BF16)
    pw = pw_ref[0].astype(BF16)
    sc = sc_ref[...]

    def chunk(r0, first):
        cur = u_ref[pl.ds(r0, R), :]
        wide = _dot(b_cur, jnp.concatenate(_split3(cur), axis=1))
        if not first:
            prev = u_ref[pl.ds(pl.multiple_of(r0 - HALO, HALO), HALO), :]
            wide = wide + _dot(b_prev, jnp.concatenate(_split3(prev), axis=1))
        count = jnp.minimum(r0 + lax.broadcasted_iota(jnp.int32, (R, 1), 0) + 1, w).astype(F32)
        pooled = (_lane_sum(wide, 3) / count - cur).astype(BF16)
        pooled_ref[pl.ds(r0, R), :] = pooled
        y_ref[pl.ds(r0, R), :] = _dot(pooled, pw) * sc

    chunk(0, True)

    def rest(c, carry):
        chunk(pl.multiple_of(c * R, R), False)
        return carry

    lax.fori_loop(1, S // R, rest, 0)


ATTN_WINDOW = 3


def _band_ones(T):
    row = lax.broadcasted_iota(jnp.int32, (T, T), 0)
    col = lax.broadcasted_iota(jnp.int32, (T, T), 1)
    after = jnp.where(row > col, 1.0, 0.0).astype(BF16)
    before = jnp.where(row < col, 1.0, 0.0).astype(BF16)
    return jnp.concatenate([after, after], axis=0), jnp.concatenate([before, before], axis=0), col < row


def _log1m(z, mask):
    sp = jnp.log(1.0 + jnp.exp(-jnp.abs(z)))
    l1m = -jnp.maximum(z, 0.0) - sp
    return (l1m if mask is None else jnp.where(mask, l1m, 0.0)), sp


def _log_gates(qhs, kbs, masks, after2):
    zs = [[_dot_nt(qh, kb) for qh in qhs] for kb in kbs]
    terms = []
    for row, mask in zip(zs, masks):
        terms.append([])
        for z in row:
            l1m, sp = _log1m(z, mask)
            terms[-1].append(
                (jnp.minimum(z, 0.0) - sp, jnp.concatenate(_split2(l1m), axis=1), jnp.sum(l1m, axis=1, keepdims=True))
            )
    return [[(ls, _dot(split, after2), total) for ls, split, total in row] for row in terms]


def _tile_start(index, T):
    return index * T if isinstance(index, int) else pl.multiple_of(index * T, T)


def _weights(ls, tail, carry, mask):
    a = jnp.exp(ls + tail + carry)
    return a if mask is None else jnp.where(mask, a, 0.0)


def _reaches(carry):
    return jnp.max(carry) > EXP_UNDERFLOW


def _mixers_fwd(q, k, v, u, pool_w, scale, shards):
    S = q.shape[0]
    T = ATTN_TILE
    nq = S // T
    n = len(shards)
    steps = D_ATTN // LANES

    def body(q_ref, k_ref, v_ref, u_ref, pw_ref, sc_ref, *rest):
        o_ref, pooled_ref, yp_ref = rest[n : n + 3]
        step = pl.program_id(0)

        def gather():
            return _two_level_gather(rest[:n], rest[n + 3 : 2 * n + 3], rest[2 * n + 3 :])

        @pl.when(step == 0)
        def _():
            gather()[0]()

        @pl.when(step == steps - 2)
        def _():
            gather()[1]()

        _pool_fwd_group(u_ref, pw_ref, sc_ref, pooled_ref, yp_ref, step)

        lane = lax.broadcasted_iota(jnp.int32, (T, LANES), 1)
        heads = (lane < HEAD_DIM, lane >= HEAD_DIM)
        after2, _, causal = _band_ones(T)
        zero = jnp.zeros((T, 1), F32)

        def key_tile(k0):
            vb = v_ref[pl.ds(k0, T), :]
            return k_ref[pl.ds(k0, T), :], jnp.concatenate([jnp.where(m, vb, jnp.zeros_like(vb)) for m in heads], axis=0)

        def q_tile(qi, nb):
            q0 = _tile_start(qi, T)
            qb = q_ref[pl.ds(q0, T), :]
            qhs = [jnp.where(m, qb, jnp.zeros_like(qb)) for m in heads]
            tiles = [key_tile(_tile_start(qi - b, T)) for b in range(nb)]
            values = [v_rows for _, v_rows in tiles]
            gates = _log_gates(qhs, [kb for kb, _ in tiles], [causal] + [None] * (nb - 1), after2)
            carries, probs = [zero, zero], []
            for b in range(nb):
                for h in range(2):
                    ls, tail, total = gates[b][h]
                    probs.append(_weights(ls, tail, carries[h], causal if b == 0 else None).astype(BF16))
                    carries[h] = carries[h] + total
            acc = _dot(jnp.concatenate(probs, axis=1), jnp.concatenate(values, axis=0))

            def more(st):
                return (st[0] > 0) & _reaches(jnp.maximum(st[1], st[2]))

            def k_step(st):
                kj = st[0] - 1
                kb, v_rows = key_tile(pl.multiple_of(kj * T, T))
                new, probs = [], []
                for (ls, tail, total), carry in zip(_log_gates(qhs, [kb], [None], after2)[0], st[1:3]):
                    probs.append(_weights(ls, tail, carry, None).astype(BF16))
                    new.append(carry + total)
                return kj, new[0], new[1], st[3] + _dot(jnp.concatenate(probs, axis=1), v_rows)

            if not isinstance(qi, int):
                acc = lax.while_loop(more, k_step, (qi - (nb - 1), carries[0], carries[1], acc))[3]
            o_ref[pl.ds(q0, T), :] = acc

        for qi in range(min(ATTN_WINDOW - 1, nq)):
            q_tile(qi, qi + 1)

        def q_loop(qi, carry):
            q_tile(qi, ATTN_WINDOW)
            return carry

        lax.fori_loop(ATTN_WINDOW - 1, nq, q_loop, 0)

        @pl.when(step == steps - 1)
        def _():
            gather()[2]()

    blk = pl.BlockSpec((S, LANES), lambda p: (0, p))
    return pl.pallas_call(
        body,
        name="mixers_fwd",
        grid=(steps,),
        out_shape=[
            jax.ShapeDtypeStruct((S, D_ATTN), F32),
            jax.ShapeDtypeStruct((S, D_POOL), BF16),
            jax.ShapeDtypeStruct((S, D_POOL), F32),
        ]
        + _exchange_shapes(shards, True),
        in_specs=[blk, blk, blk, blk, _POOL_W_SPEC, _POOL_SCALE_SPEC] + [ANY_SPEC] * n,
        out_specs=[blk, blk, blk] + [ANY_SPEC] * n,
        scratch_shapes=_exchange_sems(n),
        compiler_params=pltpu.CompilerParams(
            dimension_semantics=("arbitrary",), vmem_limit_bytes=VMEM_LIMIT, has_side_effects=True
        ),
    )(q, k, v, u, pool_w, scale, *shards)


SG_ROWS = 8


def _mlp_fwd_bwd(x, y_pool, y_attn, target, g_pool, g_attn, g2, gf, w_out, w_up, w_down):
    S = x.shape[0]
    tm = min(256, S)
    fc = D_FF // N_DEV

    def body(x_ref, yp_ref, ya_ref, t_ref, gp_ref, ga_ref, g2_ref, gf_ref, wo_hbm, wu_hbm, wd_hbm,
             mixed_ref, hn2_ref, act_ref, dup_ref, dh2_ref, dh1_ref, dh1b_ref, dyp_ref, dya_ref, sg_ref,
             wo, wu, wd, up_s, sems):
        @pl.when(pl.program_id(0) == 0)
        def _():
            copies = [
                pltpu.make_async_copy(wo_hbm, wo, sems.at[0]),
                pltpu.make_async_copy(wu_hbm, wu, sems.at[1]),
                pltpu.make_async_copy(wd_hbm, wd, sems.at[2]),
            ]
            for cp in copies:
                cp.start()
            for cp in copies:
                cp.wait()
            sg_ref[...] = jnp.zeros_like(sg_ref)

        gp, ga, g2v, gfv = gp_ref[...], ga_ref[...], g2_ref[...], gf_ref[...]
        yp, ya = yp_ref[...], ya_ref[...]
        rp, ra = _rstd(yp), _rstd(ya)
        yph, yah = yp * rp, ya * ra
        mixed = jnp.concatenate([(yph * gp).astype(BF16), (yah * ga).astype(BF16)], axis=1)
        mixed_ref[...] = mixed
        h1 = x_ref[...] + _dot(mixed, wo[...])
        r2 = _rstd(h1)
        h1h = h1 * r2
        hn2 = (h1h * g2v).astype(BF16)
        hn2_ref[...] = hn2
        h2 = h1
        for j in range(N_DEV):
            cols = slice(fc * j, fc * (j + 1))
            up = _dot(hn2, wu[j])
            up_s[:, cols] = up
            act = jnp.square(jnp.maximum(up, 0.0)).astype(BF16)
            act_ref[:, cols] = act
            h2 = h2 + _dot(act, wd[cols, :])
        rf = _rstd(h2)
        h2h = h2 * rf
        diff = h2h * gfv - t_ref[...]
        loss_rows = 0.5 * jnp.mean(diff * diff, axis=-1, keepdims=True)
        dy = diff * (1.0 / D_MODEL)
        dh2, dgf = _rms_bwd(dy, h2h, rf, gfv)
        dh2b = dh2.astype(BF16)
        dh2_ref[...] = dh2b
        dhn2 = jnp.zeros((tm, D_MODEL), F32)
        for j in range(N_DEV):
            cols = slice(fc * j, fc * (j + 1))
            dact = _dot_nt(dh2b, wd[cols, :])
            dup = (dact * (2.0 * jnp.maximum(up_s[:, cols], 0.0))).astype(BF16)
            dup_ref[:, cols] = dup
            dhn2 = dhn2 + _dot_nt(dup, wu[j])
        dh1n, dg2 = _rms_bwd(dhn2, h1h, r2, g2v)
        dh1 = dh2 + dh1n
        dh1_ref[...] = dh1
        dh1b = dh1.astype(BF16)
        dh1b_ref[...] = dh1b
        dmix = _dot_nt(dh1b, wo[...])
        dyp, dgp = _rms_bwd(dmix[:, :D_POOL], yph, rp, gp)
        dya, dga = _rms_bwd(dmix[:, D_POOL:], yah, ra, ga)
        dyp_ref[...] = dyp
        dya_ref[...] = dya
        sg_ref[0:1, :] += dgf
        sg_ref[1:2, :] += dg2
        sg_ref[2:3, :] += jnp.concatenate([dgp, dga], axis=1)
        sg_ref[3:4, :] += jnp.broadcast_to(jnp.sum(loss_rows, axis=0, keepdims=True), (1, D_MODEL))

    def tok(n):
        return pl.BlockSpec((tm, n), lambda i: (i, 0))

    def vec(n):
        return pl.BlockSpec((1, n), lambda i: (0, 0))

    any_spec = pl.BlockSpec(memory_space=pl.ANY)
    return pl.pallas_call(
        body,
        name="mlp_fwd_bwd",
        grid=(S // tm,),
        out_shape=[
            jax.ShapeDtypeStruct((S, D_MODEL), BF16),
            jax.ShapeDtypeStruct((S, D_MODEL), BF16),
            jax.ShapeDtypeStruct((S, D_FF), BF16),
            jax.ShapeDtypeStruct((S, D_FF), BF16),
            jax.ShapeDtypeStruct((S, D_MODEL), BF16),
            jax.ShapeDtypeStruct((S, D_MODEL), F32),
            jax.ShapeDtypeStruct((S, D_MODEL), BF16),
            jax.ShapeDtypeStruct((S, D_POOL), F32),
            jax.ShapeDtypeStruct((S, D_ATTN), F32),
            jax.ShapeDtypeStruct((SG_ROWS, D_MODEL), F32),
        ],
        in_specs=[tok(D_MODEL), tok(D_POOL), tok(D_ATTN), tok(D_MODEL), vec(D_POOL), vec(D_ATTN),
                  vec(D_MODEL), vec(D_MODEL), any_spec, any_spec, any_spec],
        out_specs=[tok(D_MODEL), tok(D_MODEL), tok(D_FF), tok(D_FF), tok(D_MODEL), tok(D_MODEL),
                   tok(D_MODEL), tok(D_POOL), tok(D_ATTN),
                   pl.BlockSpec((SG_ROWS, D_MODEL), lambda i: (0, 0))],
        scratch_shapes=[
            pltpu.VMEM((D_MODEL, D_MODEL), BF16),
            pltpu.VMEM((N_DEV, D_MODEL, fc), BF16),
            pltpu.VMEM((D_FF, D_MODEL), BF16),
            pltpu.VMEM((tm, D_FF), F32),
            pltpu.SemaphoreType.DMA((3,)),
        ],
        compiler_params=_params(("arbitrary",)),
    )(x, y_pool, y_attn, target, g_pool, g_attn, g2, gf, w_out, w_up, w_down)


def _wgrad(a, b, block_a, groups, name):
    S, ka = a.shape
    nb = b.shape[1]
    ts = min(1024, S)
    per = N_DEV // groups
    if block_a:
        ka //= groups
        blk = (ka // per, nb)
        a_spec = pl.BlockSpec((ts, ka), lambda g, s: (s, g))
        b_spec = pl.BlockSpec((ts, nb), lambda g, s: (s, 0))
    else:
        nb //= groups
        blk = (ka, nb // per)
        a_spec = pl.BlockSpec((ts, ka), lambda g, s: (s, 0))
        b_spec = pl.BlockSpec((ts, nb), lambda g, s: (s, g))
    steps = S // ts

    def body(a_ref, b_ref, o_ref, acc):
        s = pl.program_id(1)

        @pl.when(s == 0)
        def _():
            acc[...] = jnp.zeros_like(acc)

        acc[...] += _dot_tn(a_ref[...], b_ref[...])

        @pl.when(s == steps - 1)
        def _():
            for j in range(per):
                if block_a:
                    o_ref[j] = acc[blk[0] * j : blk[0] * (j + 1), :].astype(BF16)
                else:
                    o_ref[j] = acc[:, blk[1] * j : blk[1] * (j + 1)].astype(BF16)

    return pl.pallas_call(
        body,
        name=name,
        grid=(groups, steps),
        out_shape=jax.ShapeDtypeStruct((N_DEV,) + blk, BF16),
        in_specs=[a_spec, b_spec],
        out_specs=pl.BlockSpec((per,) + blk, lambda g, s: (g, 0, 0)),
        scratch_shapes=[pltpu.VMEM((ka, nb), F32)],
        compiler_params=_params(("parallel", "arbitrary")),
    )(a, b)


def _mixers_bwd(q, k, v, do, dyp, pooled, pool_w, scale, partials):
    S = q.shape[0]
    T = ATTN_TILE
    nq = S // T
    n = len(partials)
    steps = D_ATTN // LANES

    def body(q_ref, k_ref, v_ref, do_ref, dyp_ref, pooled_ref, pw_ref, sc_ref, *rest):
        dq_ref, dk_ref, dv_ref, du_ref, dsc_ref, dpw_ref = rest[n : n + 6]
        dk_acc, dv_acc, carry_s = rest[2 * n + 6 : 2 * n + 9]
        step = pl.program_id(0)

        def scatter():
            return _direct_exchange(rest[:n], rest[n + 6 : 2 * n + 6], rest[2 * n + 9 :], False)

        @pl.when(step == 0)
        def _():
            scatter()[0]()

        _pool_bwd_group(dyp_ref, pooled_ref, pw_ref, sc_ref, du_ref, dsc_ref, dpw_ref, step)

        dk_acc[...] = jnp.zeros_like(dk_acc)
        dv_acc[...] = jnp.zeros_like(dv_acc)
        lane = lax.broadcasted_iota(jnp.int32, (T, LANES), 1)
        heads = (lane < HEAD_DIM, lane >= HEAD_DIM)
        after2, before2, causal = _band_ones(T)
        zero = jnp.zeros((T, 1), F32)

        def key_tile(k0):
            kb = k_ref[pl.ds(k0, T), :]
            k_rows = jnp.concatenate([jnp.where(m, kb, jnp.zeros_like(kb)) for m in heads], axis=0)
            return kb, v_ref[pl.ds(k0, T), :], k_rows

        def q_tile(qi, nb):
            q0 = _tile_start(qi, T)
            qb = q_ref[pl.ds(q0, T), :]
            dob = do_ref[pl.ds(q0, T), :]
            qhs = [jnp.where(m, qb, jnp.zeros_like(qb)) for m in heads]
            dohs = [jnp.where(m, dob, 0.0).astype(BF16) for m in heads]
            q_rows = jnp.concatenate(qhs, axis=0)
            do_rows = jnp.concatenate(dohs, axis=0)

            def grad_tiles(starts, tiles, masks, carry_in, gates, st):
                nt = len(tiles)
                das = [[_dot_nt(doh, vb) for doh in dohs] for _, vb, _ in tiles]
                probs, gs = [], []
                for b in range(nt):
                    probs.append([_weights(gates[b][h][0], gates[b][h][1], carry_in[b][h], masks[b]) for h in range(2)])
                    gs.append([probs[b][h] * das[b][h] for h in range(2)])
                before = [[_dot(jnp.concatenate(_split2(g), axis=1), before2) for g in row] for row in gs]
                g_left, dzs = [st[0], st[1]], [None] * nt
                for b in reversed(range(nt)):
                    dzs[b] = []
                    for h in range(2):
                        sig = jnp.exp(gates[b][h][0])
                        dz = gs[b][h] * (1.0 - sig) - sig * (g_left[h] + before[b][h])
                        if masks[b] is not None:
                            dz = jnp.where(masks[b], dz, 0.0)
                        dzs[b].append(dz.astype(BF16))
                        g_left[h] = g_left[h] + jnp.sum(gs[b][h], axis=1, keepdims=True)
                dq = st[2] + _dot(
                    jnp.concatenate([dz for row in dzs for dz in row], axis=1),
                    jnp.concatenate([k_rows for _, _, k_rows in tiles], axis=0),
                )
                for b in range(nt):
                    dk_acc[pl.ds(starts[b], T), :] += _dot_tn(jnp.concatenate(dzs[b], axis=0), q_rows)
                    dv_acc[pl.ds(starts[b], T), :] += _dot_tn(
                        jnp.concatenate([a.astype(BF16) for a in probs[b]], axis=0), do_rows
                    )
                return g_left[0], g_left[1], dq

            starts = [_tile_start(qi - b, T) for b in range(nb)]
            tiles = [key_tile(k0) for k0 in starts]
            masks = [causal] + [None] * (nb - 1)
            gates = _log_gates(qhs, [kb for kb, _, _ in tiles], masks, after2)
            carries, carry_in = [zero, zero], []
            for b in range(nb):
                carry_in.append(list(carries))
                carries = [carries[h] + gates[b][h][2] for h in range(2)]

            st = (zero, zero, jnp.zeros((T, LANES), F32))
            if not isinstance(qi, int):
                k_left = qi - (nb - 1)

                def more(st):
                    return (st[0] > 0) & _reaches(jnp.maximum(st[1], st[2]))

                def right_to_left(st):
                    kj = st[0] - 1
                    carry_s[0, kj] = st[1]
                    carry_s[1, kj] = st[2]
                    kb = k_ref[pl.ds(pl.multiple_of(kj * T, T), T), :]
                    sums = [jnp.sum(_log1m(_dot_nt(qh, kb), None)[0], axis=1, keepdims=True) for qh in qhs]
                    return kj, st[1] + sums[0], st[2] + sums[1]

                k_first = lax.while_loop(more, right_to_left, (k_left, carries[0], carries[1]))[0]

                def left_to_right(kj, st):
                    k0 = pl.multiple_of(kj * T, T)
                    tile = key_tile(k0)
                    left_gates = _log_gates(qhs, [tile[0]], [None], after2)
                    return grad_tiles([k0], [tile], [None], [[carry_s[0, kj], carry_s[1, kj]]], left_gates, st)

                st = lax.fori_loop(k_first, k_left, left_to_right, st)
            st = grad_tiles(starts, tiles, masks, carry_in, gates, st)
            dq_ref[pl.ds(q0, T), :] = (st[2] * (HEAD_DIM**-0.5)).astype(BF16)

        for qi in range(min(ATTN_WINDOW - 1, nq)):
            q_tile(qi, qi + 1)

        def q_loop(qi, carry):
            q_tile(qi, ATTN_WINDOW)
            return carry

        lax.fori_loop(ATTN_WINDOW - 1, nq, q_loop, 0)
        dk_ref[...] = dk_acc[...].astype(BF16)
        dv_ref[...] = dv_acc[...].astype(BF16)

        @pl.when(step == steps - 1)
        def _():
            scatter()[1]()

    blk = pl.BlockSpec((S, LANES), lambda p: (0, p))
    sds = jax.ShapeDtypeStruct((S, D_ATTN), BF16)
    return pl.pallas_call(
        body,
        name="mixers_bwd",
        grid=(steps,),
        out_shape=[
            sds,
            sds,
            sds,
            jax.ShapeDtypeStruct((S, D_POOL), BF16),
            jax.ShapeDtypeStruct((8, D_POOL), F32),
            jax.ShapeDtypeStruct((N_GROUPS, GROUP_DIM, GROUP_DIM), F32),
        ]
        + _exchange_shapes(partials, False),
        in_specs=[blk] * 6 + [_POOL_W_SPEC, _POOL_SCALE_SPEC] + [ANY_SPEC] * n,
        out_specs=[blk] * 4 + [pl.BlockSpec((8, GROUP_DIM), lambda p: (0, p)), _POOL_W_SPEC] + [ANY_SPEC] * n,
        scratch_shapes=[pltpu.VMEM((S, LANES), F32), pltpu.VMEM((S, LANES), F32), pltpu.VMEM((2, nq, T, 1), F32)]
        + _exchange_sems(n),
        compiler_params=pltpu.CompilerParams(
            dimension_semantics=("arbitrary",), vmem_limit_bytes=VMEM_LIMIT, has_side_effects=True
        ),
    )(q, k, v, do, dyp, pooled, pool_w, scale, *partials)


def _pool_bwd_group(dy_ref, pooled_ref, pw_ref, sc_ref, du_ref, dsc_ref, dpw_ref, g):
    S = dy_ref.shape[0]
    R = min(POOL_CHUNK, S)
    nc = S // R
    w = jnp.left_shift(jnp.int32(2), g)
    pw = pw_ref[0].astype(BF16)
    sc = sc_ref[...]
    d = lax.broadcasted_iota(jnp.int32, (R, R), 1) - lax.broadcasted_iota(jnp.int32, (R, R), 0)
    bt_cur = jnp.where((d >= 0) & (d < w), 1.0, 0.0).astype(BF16)
    dn = lax.broadcasted_iota(jnp.int32, (R, HALO), 1) + R - lax.broadcasted_iota(jnp.int32, (R, HALO), 0)
    bt_next = jnp.where(dn < w, 1.0, 0.0).astype(BF16)

    def per_count(dpl, r0):
        n = dpl.shape[0]
        count = jnp.minimum(r0 + lax.broadcasted_iota(jnp.int32, (n, 1), 0) + 1, w).astype(F32)
        return jnp.concatenate(_split2(dpl / count), axis=1)

    def chunk(r0, last, sums):
        dyv = dy_ref[pl.ds(r0, R), :]
        pooled = pooled_ref[pl.ds(r0, R), :]
        dmapped = (dyv * sc).astype(BF16)
        dsc = sums[0] + jnp.sum(dyv * _dot(pooled, pw), axis=0, keepdims=True)
        dpw = sums[1] + _dot_tn(pooled, dmapped)
        dpl = _dot_nt(dmapped, pw)
        wide = _dot(bt_cur, per_count(dpl, r0))
        if not last:
            nxt = pl.multiple_of(r0 + R, R)
            dpl_next = _dot_nt((dy_ref[pl.ds(nxt, HALO), :] * sc).astype(BF16), pw)
            wide = wide + _dot(bt_next, per_count(dpl_next, nxt))
        du_ref[pl.ds(r0, R), :] = (_lane_sum(wide, 2) - dpl).astype(BF16)
        return dsc, dpw

    sums = (jnp.zeros((1, GROUP_DIM), F32), jnp.zeros((GROUP_DIM, GROUP_DIM), F32))
    sums = lax.fori_loop(0, nc - 1, lambda c, s: chunk(pl.multiple_of(c * R, R), False, s), sums)
    dsc, dpw = chunk((nc - 1) * R, True, sums)
    dsc_ref[...] = jnp.zeros_like(dsc_ref)
    dsc_ref[0:1, :] = dsc
    dpw_ref[0] = dpw


def _bwd_in(du, dq, dk, dv, w_in, x, dh1, g1):
    S = x.shape[0]
    tm = min(512, S)

    def body(du_ref, dq_ref, dk_ref, dv_ref, w_ref, x_ref, dh1_ref, g_ref, dx_ref, dproj_ref, dg_ref):
        @pl.when(pl.program_id(0) == 0)
        def _():
            dg_ref[...] = jnp.zeros_like(dg_ref)

        parts = (du_ref[...], dq_ref[...], dk_ref[...], dv_ref[...])
        dhn = jnp.zeros((tm, D_MODEL), F32)
        for j in range(N_DEV):
            piece = parts[j // 2][:, 256 * (j % 2) : 256 * (j % 2 + 1)]
            dproj_ref[:, 256 * j : 256 * (j + 1)] = piece
            dhn = dhn + _dot_nt(piece, w_ref[j])
        xv = x_ref[...]
        r = _rstd(xv)
        dxn, dg = _rms_bwd(dhn, xv * r, r, g_ref[...])
        dx_ref[...] = dh1_ref[...] + dxn
        dg_ref[0:1, :] += dg

    half = pl.BlockSpec((tm, D_POOL), lambda i: (i, 0))
    full = pl.BlockSpec((tm, D_MODEL), lambda i: (i, 0))
    return pl.pallas_call(
        body,
        name="bwd_in",
        grid=(S // tm,),
        out_shape=[
            jax.ShapeDtypeStruct((S, D_MODEL), F32),
            jax.ShapeDtypeStruct((S, D_IN_PROJ), BF16),
            jax.ShapeDtypeStruct((8, D_MODEL), F32),
        ],
        in_specs=[half, half, half, half,
                  pl.BlockSpec((N_DEV, D_MODEL, 256), lambda i: (0, 0, 0)),
                  full, full, pl.BlockSpec((1, D_MODEL), lambda i: (0, 0))],
        out_specs=[full, pl.BlockSpec((tm, D_IN_PROJ), lambda i: (i, 0)),
                   pl.BlockSpec((8, D_MODEL), lambda i: (0, 0))],
        compiler_params=_params(("arbitrary",)),
    )(du, dq, dk, dv, w_in, x, dh1, g1)


def _rows(a):
    a = a.reshape(-1, LANES)
    pad = (-a.shape[0]) % 8
    return jnp.pad(a, ((0, pad), (0, 0))) if pad else a


SMALL = ("final_g", "norm2_g", "pool_out_g", "attn_out_g", "norm1_g", "pool_scale", "pool_w")


def kernel(x, norm1_g, w_in, pool_w, pool_scale, pool_out_g, attn_out_g, w_out, norm2_g, w_up, w_down, final_g, loss_target, m_norm1_g, m_w_in, m_pool_w, m_pool_scale, m_pool_out_g, m_attn_out_g, m_w_out, m_norm2_g, m_w_up, m_w_down, m_final_g, v_norm1_g, v_w_in, v_pool_w, v_pool_scale, v_pool_out_g, v_attn_out_g, v_w_out, v_norm2_g, v_w_up, v_w_down, v_final_g):
    S = x.shape[1]
    xs = x.reshape(S, D_MODEL)
    tgt = loss_target.reshape(S, D_MODEL)
    row = lambda a: a.reshape(1, -1)

    (w_in_g,) = _exchange([w_in.astype(BF16)], True, "gather_w_in")
    hn, u_pool, q, k, v = _fwd_in(xs, row(norm1_g), w_in_g)
    y_attn, pooled, y_pool, w_out_g, w_up_g, w_down_g = _mixers_fwd(
        q, k, v, u_pool, pool_w, row(pool_scale), [w_out.astype(BF16), w_up.astype(BF16), w_down.astype(BF16)]
    )
    w_out_full = w_out_g.reshape(D_MODEL, D_MODEL)
    w_down_full = w_down_g.reshape(D_FF, D_MODEL)
    mixed, hn2, act, dup, dh2b, dh1, dh1b, dyp, dya, sg = _mlp_fwd_bwd(
        xs, y_pool, y_attn, tgt, row(pool_out_g), row(attn_out_g), row(norm2_g), row(final_g),
        w_out_full, w_up_g, w_down_full,
    )
    gp_down = _wgrad(act, dh2b, True, 2, "wgrad_down")
    gp_up = _wgrad(hn2, dup, False, 2, "wgrad_up")
    gp_out = _wgrad(mixed, dh1b, True, 1, "wgrad_out")
    dq, dk, dv, du, dsc, dpw, land_out, land_up, land_down = _mixers_bwd(
        q, k, v, dya, dyp, pooled, pool_w, row(pool_scale), [gp_out, gp_up, gp_down]
    )
    dx, dproj, dg1 = _bwd_in(du, dq, dk, dv, w_in_g, xs, dh1, row(norm1_g))
    gp_in = _wgrad(hn, dproj, False, 1, "wgrad_in")
    (land_in,) = _exchange([gp_in], False, "scatter_grad_w_in")
    big = {}
    for name, land, w, m, vv in (
        ("w_in", land_in, w_in, m_w_in, v_w_in),
        ("w_out", land_out, w_out, m_w_out, v_w_out),
        ("w_up", land_up, w_up, m_w_up, v_w_up),
        ("w_down", land_down, w_down, m_w_down, v_w_down),
    ):
        big[name] = _reduce_adam(land, w, m, vv, "reduce_adam_" + name)

    part = jnp.concatenate(
        [_rows(sg[0]), _rows(sg[1]), _rows(sg[2]), _rows(dg1[0]), _rows(dsc[0]), _rows(dpw), _rows(sg[3])], axis=0
    )
    given = dict(
        final_g=(final_g, m_final_g, v_final_g), norm2_g=(norm2_g, m_norm2_g, v_norm2_g),
        pool_out_g=(pool_out_g, m_pool_out_g, v_pool_out_g), attn_out_g=(attn_out_g, m_attn_out_g, v_attn_out_g),
        norm1_g=(norm1_g, m_norm1_g, v_norm1_g), pool_scale=(pool_scale, m_pool_scale, v_pool_scale),
        pool_w=(pool_w, m_pool_w, v_pool_w),
    )
    packed = []
    for i in range(3):
        pieces = [given["final_g"][i], given["norm2_g"][i],
                  jnp.concatenate([given["pool_out_g"][i], given["attn_out_g"][i]]),
                  given["norm1_g"][i], given["pool_scale"][i], given["pool_w"][i]]
        packed.append(jnp.concatenate([_rows(p) for p in pieces] + [jnp.zeros((8, LANES), F32)], axis=0))
    small = _allreduce_adam_small(part, *packed)

    def unpack(a):
        out, r = {}, 0
        for name, n in (("final_g", 1024), ("norm2_g", 1024), ("mix_g", 1024), ("norm1_g", 1024),
                        ("pool_scale", 512), ("pool_w", 65536)):
            nr = max(8, n // LANES)
            out[name] = a[r : r + n // LANES].reshape(-1)
            r += nr
        out["pool_out_g"], out["attn_out_g"] = out["mix_g"][:D_POOL], out["mix_g"][D_POOL:]
        out["pool_w"] = out["pool_w"].reshape(N_GROUPS, GROUP_DIM, GROUP_DIM)
        return out, a[r, 0]

    small_out = [unpack(a) for a in small]
    loss = small_out[0][1]
    order = ("norm1_g", "w_in", "pool_w", "pool_scale", "pool_out_g", "attn_out_g", "w_out", "norm2_g", "w_up",
             "w_down", "final_g")
    outs = [loss, dx.reshape(1, S, D_MODEL)]
    for i in range(4):
        for name in order:
            outs.append(big[name][i] if name in big else small_out[i][0][name])
    return tuple(outs)
```

```python
import functools

import jax
import jax.numpy as jnp
from jax import lax
from jax.experimental import pallas as pl
from jax.experimental.pallas import tpu as pltpu

F32 = jnp.float32
BF16 = jnp.bfloat16
MESH = pl.DeviceIdType.MESH

N_DEV = 8
D_MODEL = 1024
D_POOL = 512
D_ATTN = 512
N_GROUPS = 4
GROUP_DIM = 128
HEAD_DIM = 64
D_FF = 4096
D_IN_PROJ = 2048
EPS = 1e-6
HALO = 16
ATTN_TILE = 128
LANES = 128
EXP_UNDERFLOW = -104.0

ADAM_LR = 0.001
ADAM_B1 = 0.9
ADAM_B2 = 0.999
ADAM_EPS = 1e-08
ADAM_WD = 0.01
ADAM_STEP = 10

VMEM_LIMIT = 56 * 1024 * 1024


def _params(semantics=None, vmem=VMEM_LIMIT):
    return pltpu.CompilerParams(dimension_semantics=semantics, vmem_limit_bytes=vmem)


def _dot(a, b):
    return jnp.dot(a, b, preferred_element_type=F32)


def _dot_nt(a, b):
    return lax.dot_general(a, b, (((1,), (1,)), ((), ())), preferred_element_type=F32)


def _dot_tn(a, b):
    return lax.dot_general(a, b, (((0,), (0,)), ((), ())), preferred_element_type=F32)


def _split2(x):
    hi = x.astype(BF16)
    lo = (x - hi.astype(F32)).astype(BF16)
    return hi, lo


def _split3(x):
    hi = x.astype(BF16)
    r = x - hi.astype(F32)
    mid = r.astype(BF16)
    lo = (r - mid.astype(F32)).astype(BF16)
    return hi, mid, lo


def _rstd(h):
    return lax.rsqrt(jnp.mean(h * h, axis=-1, keepdims=True) + EPS)


def _rms_bwd(dout, hhat, r, g):
    dg = jnp.sum(dout * hhat, axis=0, keepdims=True)
    dxh = dout * g
    dh = r * (dxh - hhat * jnp.mean(dxh * hhat, axis=-1, keepdims=True))
    return dh, dg


def _my_index():
    return 4 * lax.axis_index("x") + 2 * lax.axis_index("y") + lax.axis_index("c")


def _peer(k):
    x, y, c = lax.axis_index("x"), lax.axis_index("y"), lax.axis_index("c")
    px = 1 - x if (k >> 2) & 1 else x
    py = 1 - y if (k >> 1) & 1 else y
    pc = 1 - c if k & 1 else c
    return (px, py, pc), 4 * px + 2 * py + pc


N_PEERS = N_DEV - 1
ANY_SPEC = pl.BlockSpec(memory_space=pl.ANY)


def _exchange_sems(n):
    return [
        pltpu.SemaphoreType.DMA((n * N_PEERS,)),
        pltpu.SemaphoreType.DMA((n * N_PEERS,)),
        pltpu.SemaphoreType.DMA((n,)),
    ]


def _exchange_shapes(blocks, gather):
    if gather:
        return [jax.ShapeDtypeStruct((N_DEV,) + b.shape, b.dtype) for b in blocks]
    return [jax.ShapeDtypeStruct(b.shape, b.dtype) for b in blocks]


def _direct_exchange(ins, outs, sems, gather):
    send_sems, recv_sems, local_sems = sems
    me = _my_index()
    own, sends, recvs = [], [], []
    for t in range(len(ins)):
        own.append(pltpu.make_async_copy(ins[t] if gather else ins[t].at[me], outs[t].at[me], local_sems.at[t]))
        for k in range(1, N_DEV):
            peer, peer_idx = _peer(k)
            src = ins[t] if gather else ins[t].at[peer_idx]
            for dst, bucket in ((outs[t].at[me], sends), (outs[t].at[peer_idx], recvs)):
                bucket.append(
                    pltpu.make_async_remote_copy(
                        src_ref=src,
                        dst_ref=dst,
                        send_sem=send_sems.at[t * N_PEERS + k - 1],
                        recv_sem=recv_sems.at[t * N_PEERS + k - 1],
                        device_id=peer,
                        device_id_type=MESH,
                    )
                )

    def start():
        for cp in own + sends:
            cp.start()

    def finish():
        for cp in recvs:
            cp.wait_recv()
        for cp in sends:
            cp.wait_send()
        for cp in own:
            cp.wait()

    return start, finish


def _two_level_gather(ins, outs, sems):
    send_sems, recv_sems, local_sems = sems
    x, y, c = lax.axis_index("x"), lax.axis_index("y"), lax.axis_index("c")
    me, sibling = (x, y, c), (x, y, 1 - c)
    chips = [(1 - x, y), (x, 1 - y), (1 - x, 1 - y)]

    def copy(t, k, block, to, from_input=False):
        slot = outs[t].at[4 * block[0] + 2 * block[1] + block[2]]
        return pltpu.make_async_remote_copy(
            src_ref=ins[t] if from_input else slot,
            dst_ref=slot,
            send_sem=send_sems.at[t * N_PEERS + k],
            recv_sem=recv_sems.at[t * N_PEERS + k],
            device_id=to,
            device_id_type=MESH,
        )

    arrays = range(len(ins))
    own = [pltpu.make_async_copy(ins[t], outs[t].at[4 * x + 2 * y + c], local_sems.at[t]) for t in arrays]
    first = [copy(t, 1 + j, me, (*chip, c), True) for t in arrays for j, chip in enumerate(chips)]
    first += [copy(t, 0, me, sibling, True) for t in arrays]
    passed = [(copy(t, 1 + j, (*chip, c), me), copy(t, 4 + j, (*chip, c), sibling))
              for t in arrays for j, chip in enumerate(chips)]
    last = [copy(t, 0, sibling, me) for t in arrays]
    last += [copy(t, 4 + j, (*chip, 1 - c), me) for t in arrays for j, chip in enumerate(chips)]

    def start():
        for cp in own + first:
            cp.start()

    def forward():
        for landed, onward in passed:
            landed.wait_recv()
            onward.start()

    def finish():
        for cp in last:
            cp.wait_recv()
        for cp in first + [onward for _, onward in passed]:
            cp.wait_send()
        for cp in own:
            cp.wait()

    return start, forward, finish


def _exchange(blocks, gather, name):
    n = len(blocks)

    def body(*refs):
        if gather:
            stages = _two_level_gather(refs[:n], refs[n : 2 * n], refs[2 * n :])
        else:
            stages = _direct_exchange(refs[:n], refs[n : 2 * n], refs[2 * n :], False)
        for stage in stages:
            stage()

    return pl.pallas_call(
        body,
        name=name,
        out_shape=_exchange_shapes(blocks, gather),
        in_specs=[ANY_SPEC] * n,
        out_specs=[ANY_SPEC] * n,
        scratch_shapes=_exchange_sems(n),
        compiler_params=pltpu.CompilerParams(has_side_effects=True),
    )(*blocks)


def _adam(w, g, m, v):
    m2 = ADAM_B1 * m + (1.0 - ADAM_B1) * g
    v2 = ADAM_B2 * v + (1.0 - ADAM_B2) * jnp.square(g)
    m_hat = m2 / (1.0 - ADAM_B1**ADAM_STEP)
    v_hat = v2 / (1.0 - ADAM_B2**ADAM_STEP)
    delta = -ADAM_LR * (m_hat / (jnp.sqrt(v_hat) + ADAM_EPS) + ADAM_WD * w)
    return delta, m2, v2


def _reduce_adam(land, w, m, v, name):
    rows, cols = w.shape
    tr = min(rows, max(8, (1 << 19) // cols))

    def body(l_ref, w_ref, m_ref, v_ref, g_ref, d_ref, m2_ref, v2_ref):
        g = l_ref[0].astype(F32)
        for s in range(1, N_DEV):
            g = g + l_ref[s].astype(F32)
        g_ref[...] = g
        delta, m2, v2 = _adam(w_ref[...], g, m_ref[...], v_ref[...])
        d_ref[...] = delta
        m2_ref[...] = m2
        v2_ref[...] = v2

    tile = pl.BlockSpec((tr, cols), lambda i: (i, 0))
    sds = jax.ShapeDtypeStruct((rows, cols), F32)
    return pl.pallas_call(
        body,
        name=name,
        grid=(rows // tr,),
        out_shape=[sds] * 4,
        in_specs=[pl.BlockSpec((N_DEV, tr, cols), lambda i: (0, i, 0)), tile, tile, tile],
        out_specs=[tile] * 4,
        compiler_params=_params(("parallel",)),
    )(land, w, m, v)


TAIL_STEPS = 8


def _reduce_adam_tail(lands, ws, ms, vs, gp_in, part, small_w, small_m, small_v):
    nw = len(ws)
    rows = part.shape[0]
    tiles = [w.shape[0] // TAIL_STEPS for w in ws]

    def body(*refs):
        gp_hbm, part_hbm = refs[:2]
        land_refs = refs[2 : 2 + nw]
        w_refs, m_refs, v_refs = (refs[2 + nw * (i + 1) : 2 + nw * (i + 2)] for i in range(3))
        sw_ref, sm_ref, sv_ref = refs[2 + 4 * nw : 5 + 4 * nw]
        outs = refs[5 + 4 * nw :]
        land_in = outs[0]
        big_out = outs[1 : 1 + 4 * nw]
        small_out = outs[1 + 4 * nw : 5 + 4 * nw]
        small_land = outs[5 + 4 * nw]
        sems = outs[6 + 4 * nw :]
        step = pl.program_id(0)

        def exchanges():
            return (
                _direct_exchange([gp_hbm], [land_in], sems[:3], False),
                _direct_exchange([part_hbm], [small_land], sems[3:], True),
            )

        @pl.when(step == 0)
        def _():
            for start, _ in exchanges():
                start()

        for t in range(nw):
            g = land_refs[t][0].astype(F32)
            for s in range(1, N_DEV):
                g = g + land_refs[t][s].astype(F32)
            for ref, val in zip(big_out[4 * t : 4 * t + 4], (g,) + _adam(w_refs[t][...], g, m_refs[t][...], v_refs[t][...])):
                ref[...] = val

        @pl.when(step == TAIL_STEPS - 1)
        def _():
            for _, finish in exchanges():
                finish()
            g = small_land[0]
            for s in range(1, N_DEV):
                g = g + small_land[s]
            for ref, val in zip(small_out, (g,) + _adam(sw_ref[...], g, sm_ref[...], sv_ref[...])):
                ref[...] = val

    def tile(t):
        return pl.BlockSpec((tiles[t], ws[t].shape[1]), lambda i: (i, 0))

    def land_tile(t):
        return pl.BlockSpec((N_DEV, tiles[t], ws[t].shape[1]), lambda i: (0, i, 0))

    whole = pl.BlockSpec((rows, LANES), lambda i: (0, 0))
    small_sds = jax.ShapeDtypeStruct((rows, LANES), F32)
    big_sds = [jax.ShapeDtypeStruct(ws[t].shape, F32) for t in range(nw) for _ in range(4)]
    results = pl.pallas_call(
        body,
        name="reduce_adam_tail",
        grid=(TAIL_STEPS,),
        out_shape=_exchange_shapes([gp_in], False) + big_sds + [small_sds] * 4,
        in_specs=[ANY_SPEC, ANY_SPEC]
        + [land_tile(t) for t in range(nw)]
        + [tile(t) for _ in range(3) for t in range(nw)]
        + [whole] * 3,
        out_specs=[ANY_SPEC] + [tile(t) for t in range(nw) for _ in range(4)] + [whole] * 4,
        scratch_shapes=[pltpu.VMEM((N_DEV, rows, LANES), F32)] + _exchange_sems(1) + _exchange_sems(1),
        compiler_params=pltpu.CompilerParams(
            dimension_semantics=("arbitrary",), vmem_limit_bytes=VMEM_LIMIT, has_side_effects=True
        ),
    )(gp_in, part, *lands, *ws, *ms, *vs, small_w, small_m, small_v)
    return results


def _fwd_in(x, g1, w_in):
    S = x.shape[0]
    tm = min(512, S)

    def body(x_ref, g_ref, w_ref, hn_ref, u_ref, q_ref, k_ref, v_ref):
        xv = x_ref[...]
        hn = (xv * _rstd(xv) * g_ref[...]).astype(BF16)
        hn_ref[...] = hn
        outs = (u_ref, q_ref, k_ref, v_ref)
        for j in range(N_DEV):
            p = _dot(hn, w_ref[j])
            cols = slice(256 * (j % 2), 256 * (j % 2 + 1))
            if j // 2 == 0:
                u_ref[:, cols] = p
            elif j // 2 == 1:
                q_ref[:, cols] = (p * (HEAD_DIM**-0.5)).astype(BF16)
            else:
                outs[j // 2][:, cols] = p.astype(BF16)

    half = pl.BlockSpec((tm, D_POOL), lambda i: (i, 0))
    return pl.pallas_call(
        body,
        name="fwd_in",
        grid=(S // tm,),
        out_shape=[
            jax.ShapeDtypeStruct((S, D_MODEL), BF16),
            jax.ShapeDtypeStruct((S, D_POOL), F32),
            jax.ShapeDtypeStruct((S, D_ATTN), BF16),
            jax.ShapeDtypeStruct((S, D_ATTN), BF16),
            jax.ShapeDtypeStruct((S, D_ATTN), BF16),
        ],
        in_specs=[
            pl.BlockSpec((tm, D_MODEL), lambda i: (i, 0)),
            pl.BlockSpec((1, D_MODEL), lambda i: (0, 0)),
            pl.BlockSpec((N_DEV, D_MODEL, 256), lambda i: (0, 0, 0)),
        ],
        out_specs=[pl.BlockSpec((tm, D_MODEL), lambda i: (i, 0)), half, half, half, half],
        compiler_params=_params(("parallel",)),
    )(x, g1, w_in)


POOL_CHUNK = 256
_POOL_W_SPEC = pl.BlockSpec((1, GROUP_DIM, GROUP_DIM), lambda g: (g, 0, 0))
_POOL_SCALE_SPEC = pl.BlockSpec((1, GROUP_DIM), lambda g: (0, g))


def _lane_sum(wide, n):
    out = wide[:, :GROUP_DIM]
    for i in range(1, n):
        out = out + wide[:, GROUP_DIM * i : GROUP_DIM * (i + 1)]
    return out


def _pool_fwd_group(u_ref, pw_ref, sc_ref, pooled_ref, y_ref, g):
    S = u_ref.shape[0]
    R = min(POOL_CHUNK, S)
    w = jnp.left_shift(jnp.int32(2), g)
    d = lax.broadcasted_iota(jnp.int32, (R, R), 0) - lax.broadcasted_iota(jnp.int32, (R, R), 1)
    b_cur = jnp.where((d >= 0) & (d < w), 1.0, 0.0).astype(BF16)
    dp = lax.broadcasted_iota(jnp.int32, (R, HALO), 0) + HALO - lax.broadcasted_iota(jnp.int32, (R, HALO), 1)
    b_prev = jnp.where(dp < w, 1.0, 0.0).astype(BF16)
    pw = pw_ref[0].astype(BF16)
    sc = sc_ref[...]

    def chunk(r0, first):
        cur = u_ref[pl.ds(r0, R), :]
        wide = _dot(b_cur, jnp.concatenate(_split3(cur), axis=1))
        if not first:
            prev = u_ref[pl.ds(pl.multiple_of(r0 - HALO, HALO), HALO), :]
            wide = wide + _dot(b_prev, jnp.concatenate(_split3(prev), axis=1))
        count = jnp.minimum(r0 + lax.broadcasted_iota(jnp.int32, (R, 1), 0) + 1, w).astype(F32)
        pooled = (_lane_sum(wide, 3) / count - cur).astype(BF16)
        pooled_ref[pl.ds(r0, R), :] = pooled
        y_ref[pl.ds(r0, R), :] = _dot(pooled, pw) * sc

    chunk(0, True)

    def rest(c, carry):
        chunk(pl.multiple_of(c * R, R), False)
        return carry

    lax.fori_loop(1, S // R, rest, 0)


ATTN_WINDOW = 3


def _band_ones(T):
    row = lax.broadcasted_iota(jnp.int32, (T, T), 0)
    col = lax.broadcasted_iota(jnp.int32, (T, T), 1)
    after = jnp.where(row > col, 1.0, 0.0).astype(BF16)
    before = jnp.where(row < col, 1.0, 0.0).astype(BF16)
    return jnp.concatenate([after, after], axis=0), jnp.concatenate([before, before], axis=0), col < row


def _log1m(z, mask):
    sp = jnp.log(1.0 + jnp.exp(-jnp.abs(z)))
    l1m = -jnp.maximum(z, 0.0) - sp
    return (l1m if mask is None else jnp.where(mask, l1m, 0.0)), sp


def _log_gates(qhs, kbs, masks, after2):
    zs = [[_dot_nt(qh, kb) for qh in qhs] for kb in kbs]
    terms = []
    for row, mask in zip(zs, masks):
        terms.append([])
        for z in row:
            l1m, sp = _log1m(z, mask)
            terms[-1].append(
                (jnp.minimum(z, 0.0) - sp, jnp.concatenate(_split2(l1m), axis=1), jnp.sum(l1m, axis=1, keepdims=True))
            )
    return [[(ls, _dot(split, after2), total) for ls, split, total in row] for row in terms]


def _tile_start(index, T):
    return index * T if isinstance(index, int) else pl.multiple_of(index * T, T)


def _weights(ls, tail, carry, mask):
    a = jnp.exp(ls + tail + carry)
    return a if mask is None else jnp.where(mask, a, 0.0)


def _reaches(carry):
    return jnp.max(carry) > EXP_UNDERFLOW


def _mixers_fwd(q, k, v, u, pool_w, scale, shards):
    S = q.shape[0]
    T = ATTN_TILE
    nq = S // T
    n = len(shards)
    steps = D_ATTN // LANES

    def body(q_ref, k_ref, v_ref, u_ref, pw_ref, sc_ref, *rest):
        o_ref, pooled_ref, yp_ref = rest[n : n + 3]
        step = pl.program_id(0)

        def gather():
            return _two_level_gather(rest[:n], rest[n + 3 : 2 * n + 3], rest[2 * n + 3 :])

        @pl.when(step == 0)
        def _():
            gather()[0]()

        _pool_fwd_group(u_ref, pw_ref, sc_ref, pooled_ref, yp_ref, step)

        lane = lax.broadcasted_iota(jnp.int32, (T, LANES), 1)
        heads = (lane < HEAD_DIM, lane >= HEAD_DIM)
        after2, _, causal = _band_ones(T)
        zero = jnp.zeros((T, 1), F32)

        def key_tile(k0):
            vb = v_ref[pl.ds(k0, T), :]
            return k_ref[pl.ds(k0, T), :], jnp.concatenate([jnp.where(m, vb, jnp.zeros_like(vb)) for m in heads], axis=0)

        def q_tile(qi, nb):
            q0 = _tile_start(qi, T)
            qb = q_ref[pl.ds(q0, T), :]
            qhs = [jnp.where(m, qb, jnp.zeros_like(qb)) for m in heads]
            tiles = [key_tile(_tile_start(qi - b, T)) for b in range(nb)]
            values = [v_rows for _, v_rows in tiles]
            gates = _log_gates(qhs, [kb for kb, _ in tiles], [causal] + [None] * (nb - 1), after2)
            carries, probs = [zero, zero], []
            for b in range(nb):
                for h in range(2):
                    ls, tail, total = gates[b][h]
                    probs.append(_weights(ls, tail, carries[h], causal if b == 0 else None).astype(BF16))
                    carries[h] = carries[h] + total
            acc = _dot(jnp.concatenate(probs, axis=1), jnp.concatenate(values, axis=0))

            def more(st):
                return (st[0] > 0) & _reaches(jnp.maximum(st[1], st[2]))

            def k_step(st):
                kj = st[0] - 1
                kb, v_rows = key_tile(pl.multiple_of(kj * T, T))
                new, probs = [], []
                for (ls, tail, total), carry in zip(_log_gates(qhs, [kb], [None], after2)[0], st[1:3]):
                    probs.append(_weights(ls, tail, carry, None).astype(BF16))
                    new.append(carry + total)
                return kj, new[0], new[1], st[3] + _dot(jnp.concatenate(probs, axis=1), v_rows)

            if not isinstance(qi, int):
                acc = lax.while_loop(more, k_step, (qi - (nb - 1), carries[0], carries[1], acc))[3]
            o_ref[pl.ds(q0, T), :] = acc

        for qi in range(min(ATTN_WINDOW - 1, nq)):
            q_tile(qi, qi + 1)

        def q_loop(qi, carry):
            q_tile(qi, ATTN_WINDOW)
            return carry

        lax.fori_loop(ATTN_WINDOW - 1, nq, q_loop, 0)

        @pl.when(step == steps - 1)
        def _():
            _, forward, finish = gather()
            forward()
            finish()

    blk = pl.BlockSpec((S, LANES), lambda p: (0, p))
    return pl.pallas_call(
        body,
        name="mixers_fwd",
        grid=(steps,),
        out_shape=[
            jax.ShapeDtypeStruct((S, D_ATTN), F32),
            jax.ShapeDtypeStruct((S, D_POOL), BF16),
            jax.ShapeDtypeStruct((S, D_POOL), F32),
        ]
        + _exchange_shapes(shards, True),
        in_specs=[blk, blk, blk, blk, _POOL_W_SPEC, _POOL_SCALE_SPEC] + [ANY_SPEC] * n,
        out_specs=[blk, blk, blk] + [ANY_SPEC] * n,
        scratch_shapes=_exchange_sems(n),
        compiler_params=pltpu.CompilerParams(
            dimension_semantics=("arbitrary",), vmem_limit_bytes=VMEM_LIMIT, has_side_effects=True
        ),
    )(q, k, v, u, pool_w, scale, *shards)


SG_ROWS = 8


def _mlp_fwd_bwd(x, y_pool, y_attn, target, g_pool, g_attn, g2, gf, w_out, w_up, w_down):
    S = x.shape[0]
    tm = min(256, S)
    fc = D_FF // N_DEV

    def body(x_ref, yp_ref, ya_ref, t_ref, gp_ref, ga_ref, g2_ref, gf_ref, wo_hbm, wu_hbm, wd_hbm,
             mixed_ref, hn2_ref, act_ref, dup_ref, dh2_ref, dh1_ref, dh1b_ref, dyp_ref, dya_ref, sg_ref,
             wo, wu, wd, up_s, sems):
        @pl.when(pl.program_id(0) == 0)
        def _():
            copies = [
                pltpu.make_async_copy(wo_hbm, wo, sems.at[0]),
                pltpu.make_async_copy(wu_hbm, wu, sems.at[1]),
                pltpu.make_async_copy(wd_hbm, wd, sems.at[2]),
            ]
            for cp in copies:
                cp.start()
            for cp in copies:
                cp.wait()
            sg_ref[...] = jnp.zeros_like(sg_ref)

        gp, ga, g2v, gfv = gp_ref[...], ga_ref[...], g2_ref[...], gf_ref[...]
        yp, ya = yp_ref[...], ya_ref[...]
        rp, ra = _rstd(yp), _rstd(ya)
        yph, yah = yp * rp, ya * ra
        mixed = jnp.concatenate([(yph * gp).astype(BF16), (yah * ga).astype(BF16)], axis=1)
        mixed_ref[...] = mixed
        h1 = x_ref[...] + _dot(mixed, wo[...])
        r2 = _rstd(h1)
        h1h = h1 * r2
        hn2 = (h1h * g2v).astype(BF16)
        hn2_ref[...] = hn2
        h2 = h1
        for j in range(N_DEV):
            cols = slice(fc * j, fc * (j + 1))
            up = _dot(hn2, wu[j])
            up_s[:, cols] = up
            act = jnp.square(jnp.maximum(up, 0.0)).astype(BF16)
            act_ref[:, cols] = act
            h2 = h2 + _dot(act, wd[cols, :])
        rf = _rstd(h2)
        h2h = h2 * rf
        diff = h2h * gfv - t_ref[...]
        loss_rows = 0.5 * jnp.mean(diff * diff, axis=-1, keepdims=True)
        dy = diff * (1.0 / D_MODEL)
        dh2, dgf = _rms_bwd(dy, h2h, rf, gfv)
        dh2b = dh2.astype(BF16)
        dh2_ref[...] = dh2b
        dhn2 = jnp.zeros((tm, D_MODEL), F32)
        for j in range(N_DEV):
            cols = slice(fc * j, fc * (j + 1))
            dact = _dot_nt(dh2b, wd[cols, :])
            dup = (dact * (2.0 * jnp.maximum(up_s[:, cols], 0.0))).astype(BF16)
            dup_ref[:, cols] = dup
            dhn2 = dhn2 + _dot_nt(dup, wu[j])
        dh1n, dg2 = _rms_bwd(dhn2, h1h, r2, g2v)
        dh1 = dh2 + dh1n
        dh1_ref[...] = dh1
        dh1b = dh1.astype(BF16)
        dh1b_ref[...] = dh1b
        dmix = _dot_nt(dh1b, wo[...])
        dyp, dgp = _rms_bwd(dmix[:, :D_POOL], yph, rp, gp)
        dya, dga = _rms_bwd(dmix[:, D_POOL:], yah, ra, ga)
        dyp_ref[...] = dyp
        dya_ref[...] = dya
        sg_ref[0:1, :] += dgf
        sg_ref[1:2, :] += dg2
        sg_ref[2:3, :] += jnp.concatenate([dgp, dga], axis=1)
        sg_ref[3:4, :] += jnp.broadcast_to(jnp.sum(loss_rows, axis=0, keepdims=True), (1, D_MODEL))

    def tok(n):
        return pl.BlockSpec((tm, n), lambda i: (i, 0))

    def vec(n):
        return pl.BlockSpec((1, n), lambda i: (0, 0))

    any_spec = pl.BlockSpec(memory_space=pl.ANY)
    return pl.pallas_call(
        body,
        name="mlp_fwd_bwd",
        grid=(S // tm,),
        out_shape=[
            jax.ShapeDtypeStruct((S, D_MODEL), BF16),
            jax.ShapeDtypeStruct((S, D_MODEL), BF16),
            jax.ShapeDtypeStruct((S, D_FF), BF16),
            jax.ShapeDtypeStruct((S, D_FF), BF16),
            jax.ShapeDtypeStruct((S, D_MODEL), BF16),
            jax.ShapeDtypeStruct((S, D_MODEL), F32),
            jax.ShapeDtypeStruct((S, D_MODEL), BF16),
            jax.ShapeDtypeStruct((S, D_POOL), F32),
            jax.ShapeDtypeStruct((S, D_ATTN), F32),
            jax.ShapeDtypeStruct((SG_ROWS, D_MODEL), F32),
        ],
        in_specs=[tok(D_MODEL), tok(D_POOL), tok(D_ATTN), tok(D_MODEL), vec(D_POOL), vec(D_ATTN),
                  vec(D_MODEL), vec(D_MODEL), any_spec, any_spec, any_spec],
        out_specs=[tok(D_MODEL), tok(D_MODEL), tok(D_FF), tok(D_FF), tok(D_MODEL), tok(D_MODEL),
                   tok(D_MODEL), tok(D_POOL), tok(D_ATTN),
                   pl.BlockSpec((SG_ROWS, D_MODEL), lambda i: (0, 0))],
        scratch_shapes=[
            pltpu.VMEM((D_MODEL, D_MODEL), BF16),
            pltpu.VMEM((N_DEV, D_MODEL, fc), BF16),
            pltpu.VMEM((D_FF, D_MODEL), BF16),
            pltpu.VMEM((tm, D_FF), F32),
            pltpu.SemaphoreType.DMA((3,)),
        ],
        compiler_params=_params(("arbitrary",)),
    )(x, y_pool, y_attn, target, g_pool, g_attn, g2, gf, w_out, w_up, w_down)


def _wgrad(a, b, block_a, groups, name):
    S, ka = a.shape
    nb = b.shape[1]
    ts = min(1024, S)
    per = N_DEV // groups
    if block_a:
        ka //= groups
        blk = (ka // per, nb)
        a_spec = pl.BlockSpec((ts, ka), lambda g, s: (s, g))
        b_spec = pl.BlockSpec((ts, nb), lambda g, s: (s, 0))
    else:
        nb //= groups
        blk = (ka, nb // per)
        a_spec = pl.BlockSpec((ts, ka), lambda g, s: (s, 0))
        b_spec = pl.BlockSpec((ts, nb), lambda g, s: (s, g))
    steps = S // ts

    def body(a_ref, b_ref, o_ref, acc):
        s = pl.program_id(1)

        @pl.when(s == 0)
        def _():
            acc[...] = jnp.zeros_like(acc)

        acc[...] += _dot_tn(a_ref[...], b_ref[...])

        @pl.when(s == steps - 1)
        def _():
            for j in range(per):
                if block_a:
                    o_ref[j] = acc[blk[0] * j : blk[0] * (j + 1), :].astype(BF16)
                else:
                    o_ref[j] = acc[:, blk[1] * j : blk[1] * (j + 1)].astype(BF16)

    return pl.pallas_call(
        body,
        name=name,
        grid=(groups, steps),
        out_shape=jax.ShapeDtypeStruct((N_DEV,) + blk, BF16),
        in_specs=[a_spec, b_spec],
        out_specs=pl.BlockSpec((per,) + blk, lambda g, s: (g, 0, 0)),
        scratch_shapes=[pltpu.VMEM((ka, nb), F32)],
        compiler_params=_params(("parallel", "arbitrary")),
    )(a, b)


def _mixers_bwd(q, k, v, do, dyp, pooled, pool_w, scale, partials):
    S = q.shape[0]
    T = ATTN_TILE
    nq = S // T
    n = len(partials)
    steps = D_ATTN // LANES

    def body(q_ref, k_ref, v_ref, do_ref, dyp_ref, pooled_ref, pw_ref, sc_ref, *rest):
        dq_ref, dk_ref, dv_ref, du_ref, dsc_ref, dpw_ref = rest[n : n + 6]
        dk_acc, dv_acc, carry_s = rest[2 * n + 6 : 2 * n + 9]
        step = pl.program_id(0)

        def scatter():
            return _direct_exchange(rest[:n], rest[n + 6 : 2 * n + 6], rest[2 * n + 9 :], False)

        @pl.when(step == 0)
        def _():
            scatter()[0]()

        _pool_bwd_group(dyp_ref, pooled_ref, pw_ref, sc_ref, du_ref, dsc_ref, dpw_ref, step)

        dk_acc[...] = jnp.zeros_like(dk_acc)
        dv_acc[...] = jnp.zeros_like(dv_acc)
        lane = lax.broadcasted_iota(jnp.int32, (T, LANES), 1)
        heads = (lane < HEAD_DIM, lane >= HEAD_DIM)
        after2, before2, causal = _band_ones(T)
        zero = jnp.zeros((T, 1), F32)

        def key_tile(k0):
            kb = k_ref[pl.ds(k0, T), :]
            k_rows = jnp.concatenate([jnp.where(m, kb, jnp.zeros_like(kb)) for m in heads], axis=0)
            return kb, v_ref[pl.ds(k0, T), :], k_rows

        def q_tile(qi, nb):
            q0 = _tile_start(qi, T)
            qb = q_ref[pl.ds(q0, T), :]
            dob = do_ref[pl.ds(q0, T), :]
            qhs = [jnp.where(m, qb, jnp.zeros_like(qb)) for m in heads]
            dohs = [jnp.where(m, dob, 0.0).astype(BF16) for m in heads]
            q_rows = jnp.concatenate(qhs, axis=0)
            do_rows = jnp.concatenate(dohs, axis=0)

            def grad_tiles(starts, tiles, masks, carry_in, gates, st):
                nt = len(tiles)
                das = [[_dot_nt(doh, vb) for doh in dohs] for _, vb, _ in tiles]
                probs, gs = [], []
                for b in range(nt):
                    probs.append([_weights(gates[b][h][0], gates[b][h][1], carry_in[b][h], masks[b]) for h in range(2)])
                    gs.append([probs[b][h] * das[b][h] for h in range(2)])
                before = [[_dot(jnp.concatenate(_split2(g), axis=1), before2) for g in row] for row in gs]
                g_left, dzs = [st[0], st[1]], [None] * nt
                for b in reversed(range(nt)):
                    dzs[b] = []
                    for h in range(2):
                        sig = jnp.exp(gates[b][h][0])
                        dz = gs[b][h] * (1.0 - sig) - sig * (g_left[h] + before[b][h])
                        if masks[b] is not None:
                            dz = jnp.where(masks[b], dz, 0.0)
                        dzs[b].append(dz.astype(BF16))
                        g_left[h] = g_left[h] + jnp.sum(gs[b][h], axis=1, keepdims=True)
                dq = st[2] + _dot(
                    jnp.concatenate([dz for row in dzs for dz in row], axis=1),
                    jnp.concatenate([k_rows for _, _, k_rows in tiles], axis=0),
                )
                for b in range(nt):
                    dk_acc[pl.ds(starts[b], T), :] += _dot_tn(jnp.concatenate(dzs[b], axis=0), q_rows)
                    dv_acc[pl.ds(starts[b], T), :] += _dot_tn(
                        jnp.concatenate([a.astype(BF16) for a in probs[b]], axis=0), do_rows
                    )
                return g_left[0], g_left[1], dq

            starts = [_tile_start(qi - b, T) for b in range(nb)]
            tiles = [key_tile(k0) for k0 in starts]
            masks = [causal] + [None] * (nb - 1)
            gates = _log_gates(qhs, [kb for kb, _, _ in tiles], masks, after2)
            carries, carry_in = [zero, zero], []
            for b in range(nb):
                carry_in.append(list(carries))
                carries = [carries[h] + gates[b][h][2] for h in range(2)]

            st = (zero, zero, jnp.zeros((T, LANES), F32))
            if not isinstance(qi, int):
                k_left = qi - (nb - 1)

                def more(st):
                    return (st[0] > 0) & _reaches(jnp.maximum(st[1], st[2]))

                def right_to_left(st):
                    kj = st[0] - 1
                    carry_s[0, kj] = st[1]
                    carry_s[1, kj] = st[2]
                    kb = k_ref[pl.ds(pl.multiple_of(kj * T, T), T), :]
                    sums = [jnp.sum(_log1m(_dot_nt(qh, kb), None)[0], axis=1, keepdims=True) for qh in qhs]
                    return kj, st[1] + sums[0], st[2] + sums[1]

                k_first = lax.while_loop(more, right_to_left, (k_left, carries[0], carries[1]))[0]

                def left_to_right(kj, st):
                    k0 = pl.multiple_of(kj * T, T)
                    tile = key_tile(k0)
                    left_gates = _log_gates(qhs, [tile[0]], [None], after2)
                    return grad_tiles([k0], [tile], [None], [[carry_s[0, kj], carry_s[1, kj]]], left_gates, st)

                st = lax.fori_loop(k_first, k_left, left_to_right, st)
            st = grad_tiles(starts, tiles, masks, carry_in, gates, st)
            dq_ref[pl.ds(q0, T), :] = (st[2] * (HEAD_DIM**-0.5)).astype(BF16)

        for qi in range(min(ATTN_WINDOW - 1, nq)):
            q_tile(qi, qi + 1)

        def q_loop(qi, carry):
            q_tile(qi, ATTN_WINDOW)
            return carry

        lax.fori_loop(ATTN_WINDOW - 1, nq, q_loop, 0)
        dk_ref[...] = dk_acc[...].astype(BF16)
        dv_ref[...] = dv_acc[...].astype(BF16)

        @pl.when(step == steps - 1)
        def _():
            scatter()[1]()

    blk = pl.BlockSpec((S, LANES), lambda p: (0, p))
    sds = jax.ShapeDtypeStruct((S, D_ATTN), BF16)
    return pl.pallas_call(
        body,
        name="mixers_bwd",
        grid=(steps,),
        out_shape=[
            sds,
            sds,
            sds,
            jax.ShapeDtypeStruct((S, D_POOL), BF16),
            jax.ShapeDtypeStruct((8, D_POOL), F32),
            jax.ShapeDtypeStruct((N_GROUPS, GROUP_DIM, GROUP_DIM), F32),
        ]
        + _exchange_shapes(partials, False),
        in_specs=[blk] * 6 + [_POOL_W_SPEC, _POOL_SCALE_SPEC] + [ANY_SPEC] * n,
        out_specs=[blk] * 4 + [pl.BlockSpec((8, GROUP_DIM), lambda p: (0, p)), _POOL_W_SPEC] + [ANY_SPEC] * n,
        scratch_shapes=[pltpu.VMEM((S, LANES), F32), pltpu.VMEM((S, LANES), F32), pltpu.VMEM((2, nq, T, 1), F32)]
        + _exchange_sems(n),
        compiler_params=pltpu.CompilerParams(
            dimension_semantics=("arbitrary",), vmem_limit_bytes=VMEM_LIMIT, has_side_effects=True
        ),
    )(q, k, v, do, dyp, pooled, pool_w, scale, *partials)


def _pool_bwd_group(dy_ref, pooled_ref, pw_ref, sc_ref, du_ref, dsc_ref, dpw_ref, g):
    S = dy_ref.shape[0]
    R = min(POOL_CHUNK, S)
    nc = S // R
    w = jnp.left_shift(jnp.int32(2), g)
    pw = pw_ref[0].astype(BF16)
    sc = sc_ref[...]
    d = lax.broadcasted_iota(jnp.int32, (R, R), 1) - lax.broadcasted_iota(jnp.int32, (R, R), 0)
    bt_cur = jnp.where((d >= 0) & (d < w), 1.0, 0.0).astype(BF16)
    dn = lax.broadcasted_iota(jnp.int32, (R, HALO), 1) + R - lax.broadcasted_iota(jnp.int32, (R, HALO), 0)
    bt_next = jnp.where(dn < w, 1.0, 0.0).astype(BF16)

    def per_count(dpl, r0):
        n = dpl.shape[0]
        count = jnp.minimum(r0 + lax.broadcasted_iota(jnp.int32, (n, 1), 0) + 1, w).astype(F32)
        return jnp.concatenate(_split2(dpl / count), axis=1)

    def chunk(r0, last, sums):
        dyv = dy_ref[pl.ds(r0, R), :]
        pooled = pooled_ref[pl.ds(r0, R), :]
        dmapped = (dyv * sc).astype(BF16)
        dsc = sums[0] + jnp.sum(dyv * _dot(pooled, pw), axis=0, keepdims=True)
        dpw = sums[1] + _dot_tn(pooled, dmapped)
        dpl = _dot_nt(dmapped, pw)
        wide = _dot(bt_cur, per_count(dpl, r0))
        if not last:
            nxt = pl.multiple_of(r0 + R, R)
            dpl_next = _dot_nt((dy_ref[pl.ds(nxt, HALO), :] * sc).astype(BF16), pw)
            wide = wide + _dot(bt_next, per_count(dpl_next, nxt))
        du_ref[pl.ds(r0, R), :] = (_lane_sum(wide, 2) - dpl).astype(BF16)
        return dsc, dpw

    sums = (jnp.zeros((1, GROUP_DIM), F32), jnp.zeros((GROUP_DIM, GROUP_DIM), F32))
    sums = lax.fori_loop(0, nc - 1, lambda c, s: chunk(pl.multiple_of(c * R, R), False, s), sums)
    dsc, dpw = chunk((nc - 1) * R, True, sums)
    dsc_ref[...] = jnp.zeros_like(dsc_ref)
    dsc_ref[0:1, :] = dsc
    dpw_ref[0] = dpw


def _bwd_in(du, dq, dk, dv, w_in, x, dh1, g1):
    S = x.shape[0]
    tm = min(512, S)

    def body(du_ref, dq_ref, dk_ref, dv_ref, w_ref, x_ref, dh1_ref, g_ref, dx_ref, dproj_ref, dg_ref):
        @pl.when(pl.program_id(0) == 0)
        def _():
            dg_ref[...] = jnp.zeros_like(dg_ref)

        parts = (du_ref[...], dq_ref[...], dk_ref[...], dv_ref[...])
        dhn = jnp.zeros((tm, D_MODEL), F32)
        for j in range(N_DEV):
            piece = parts[j // 2][:, 256 * (j % 2) : 256 * (j % 2 + 1)]
            dproj_ref[:, 256 * j : 256 * (j + 1)] = piece
            dhn = dhn + _dot_nt(piece, w_ref[j])
        xv = x_ref[...]
        r = _rstd(xv)
        dxn, dg = _rms_bwd(dhn, xv * r, r, g_ref[...])
        dx_ref[...] = dh1_ref[...] + dxn
        dg_ref[0:1, :] += dg

    half = pl.BlockSpec((tm, D_POOL), lambda i: (i, 0))
    full = pl.BlockSpec((tm, D_MODEL), lambda i: (i, 0))
    return pl.pallas_call(
        body,
        name="bwd_in",
        grid=(S // tm,),
        out_shape=[
            jax.ShapeDtypeStruct((S, D_MODEL), F32),
            jax.ShapeDtypeStruct((S, D_IN_PROJ), BF16),
            jax.ShapeDtypeStruct((8, D_MODEL), F32),
        ],
        in_specs=[half, half, half, half,
                  pl.BlockSpec((N_DEV, D_MODEL, 256), lambda i: (0, 0, 0)),
                  full, full, pl.BlockSpec((1, D_MODEL), lambda i: (0, 0))],
        out_specs=[full, pl.BlockSpec((tm, D_IN_PROJ), lambda i: (i, 0)),
                   pl.BlockSpec((8, D_MODEL), lambda i: (0, 0))],
        compiler_params=_params(("arbitrary",)),
    )(du, dq, dk, dv, w_in, x, dh1, g1)


def _rows(a):
    a = a.reshape(-1, LANES)
    pad = (-a.shape[0]) % 8
    return jnp.pad(a, ((0, pad), (0, 0))) if pad else a


SMALL = ("final_g", "norm2_g", "pool_out_g", "attn_out_g", "norm1_g", "pool_scale", "pool_w")


def kernel(x, norm1_g, w_in, pool_w, pool_scale, pool_out_g, attn_out_g, w_out, norm2_g, w_up, w_down, final_g, loss_target, m_norm1_g, m_w_in, m_pool_w, m_pool_scale, m_pool_out_g, m_attn_out_g, m_w_out, m_norm2_g, m_w_up, m_w_down, m_final_g, v_norm1_g, v_w_in, v_pool_w, v_pool_scale, v_pool_out_g, v_attn_out_g, v_w_out, v_norm2_g, v_w_up, v_w_down, v_final_g):
    S = x.shape[1]
    xs = x.reshape(S, D_MODEL)
    tgt = loss_target.reshape(S, D_MODEL)
    row = lambda a: a.reshape(1, -1)

    (w_in_g,) = _exchange([w_in.astype(BF16)], True, "gather_w_in")
    hn, u_pool, q, k, v = _fwd_in(xs, row(norm1_g), w_in_g)
    y_attn, pooled, y_pool, w_out_g, w_up_g, w_down_g = _mixers_fwd(
        q, k, v, u_pool, pool_w, row(pool_scale), [w_out.astype(BF16), w_up.astype(BF16), w_down.astype(BF16)]
    )
    w_out_full = w_out_g.reshape(D_MODEL, D_MODEL)
    w_down_full = w_down_g.reshape(D_FF, D_MODEL)
    mixed, hn2, act, dup, dh2b, dh1, dh1b, dyp, dya, sg = _mlp_fwd_bwd(
        xs, y_pool, y_attn, tgt, row(pool_out_g), row(attn_out_g), row(norm2_g), row(final_g),
        w_out_full, w_up_g, w_down_full,
    )
    gp_down = _wgrad(act, dh2b, True, 2, "wgrad_down")
    gp_up = _wgrad(hn2, dup, False, 2, "wgrad_up")
    gp_out = _wgrad(mixed, dh1b, True, 1, "wgrad_out")
    dq, dk, dv, du, dsc, dpw, land_out, land_up, land_down = _mixers_bwd(
        q, k, v, dya, dyp, pooled, pool_w, row(pool_scale), [gp_out, gp_up, gp_down]
    )
    dx, dproj, dg1 = _bwd_in(du, dq, dk, dv, w_in_g, xs, dh1, row(norm1_g))
    gp_in = _wgrad(hn, dproj, False, 1, "wgrad_in")
    part = jnp.concatenate(
        [_rows(sg[0]), _rows(sg[1]), _rows(sg[2]), _rows(dg1[0]), _rows(dsc[0]), _rows(dpw), _rows(sg[3])], axis=0
    )
    given = dict(
        final_g=(final_g, m_final_g, v_final_g), norm2_g=(norm2_g, m_norm2_g, v_norm2_g),
        pool_out_g=(pool_out_g, m_pool_out_g, v_pool_out_g), attn_out_g=(attn_out_g, m_attn_out_g, v_attn_out_g),
        norm1_g=(norm1_g, m_norm1_g, v_norm1_g), pool_scale=(pool_scale, m_pool_scale, v_pool_scale),
        pool_w=(pool_w, m_pool_w, v_pool_w),
    )
    packed = []
    for i in range(3):
        pieces = [given["final_g"][i], given["norm2_g"][i],
                  jnp.concatenate([given["pool_out_g"][i], given["attn_out_g"][i]]),
                  given["norm1_g"][i], given["pool_scale"][i], given["pool_w"][i]]
        packed.append(jnp.concatenate([_rows(p) for p in pieces] + [jnp.zeros((8, LANES), F32)], axis=0))
    tail = _reduce_adam_tail(
        [land_out, land_up, land_down], [w_out, w_up, w_down], [m_w_out, m_w_up, m_w_down],
        [v_w_out, v_w_up, v_w_down], gp_in, part, *packed,
    )
    big = {name: tail[1 + 4 * t : 5 + 4 * t] for t, name in enumerate(("w_out", "w_up", "w_down"))}
    big["w_in"] = _reduce_adam(tail[0], w_in, m_w_in, v_w_in, "reduce_adam_w_in")
    small = tail[13:17]

    def unpack(a):
        out, r = {}, 0
        for name, n in (("final_g", 1024), ("norm2_g", 1024), ("mix_g", 1024), ("norm1_g", 1024),
                        ("pool_scale", 512), ("pool_w", 65536)):
            nr = max(8, n // LANES)
            out[name] = a[r : r + n // LANES].reshape(-1)
            r += nr
        out["pool_out_g"], out["attn_out_g"] = out["mix_g"][:D_POOL], out["mix_g"][D_POOL:]
        out["pool_w"] = out["pool_w"].reshape(N_GROUPS, GROUP_DIM, GROUP_DIM)
        return out, a[r, 0]

    small_out = [unpack(a) for a in small]
    loss = small_out[0][1]
    order = ("norm1_g", "w_in", "pool_w", "pool_scale", "pool_out_g", "attn_out_g", "w_out", "norm2_g", "w_up",
             "w_down", "final_g")
    outs = [loss, dx.reshape(1, S, D_MODEL)]
    for i in range(4):
        for name in order:
            outs.append(big[name][i] if name in big else small_out[i][0][name])
    return tuple(outs)
```

```python
import functools

import jax
import jax.numpy as jnp
from jax import lax
from jax.experimental import pallas as pl
from jax.experimental.pallas import tpu as pltpu

F32 = jnp.float32
BF16 = jnp.bfloat16
MESH = pl.DeviceIdType.MESH

N_DEV = 8
D_MODEL = 1024
D_POOL = 512
D_ATTN = 512
N_GROUPS = 4
GROUP_DIM = 128
HEAD_DIM = 64
D_FF = 4096
D_IN_PROJ = 2048
EPS = 1e-6
HALO = 16
ATTN_TILE = 128
LANES = 128
EXP_UNDERFLOW = -104.0

ADAM_LR = 0.001
ADAM_B1 = 0.9
ADAM_B2 = 0.999
ADAM_EPS = 1e-08
ADAM_WD = 0.01
ADAM_STEP = 10

VMEM_LIMIT = 56 * 1024 * 1024


def _params(semantics=None, vmem=VMEM_LIMIT):
    return pltpu.CompilerParams(dimension_semantics=semantics, vmem_limit_bytes=vmem)


def _dot(a, b):
    return jnp.dot(a, b, preferred_element_type=F32)


def _dot_nt(a, b):
    return lax.dot_general(a, b, (((1,), (1,)), ((), ())), preferred_element_type=F32)


def _dot_tn(a, b):
    return lax.dot_general(a, b, (((0,), (0,)), ((), ())), preferred_element_type=F32)


def _split2(x):
    hi = x.astype(BF16)
    lo = (x - hi.astype(F32)).astype(BF16)
    return hi, lo


def _split3(x):
    hi = x.astype(BF16)
    r = x - hi.astype(F32)
    mid = r.astype(BF16)
    lo = (r - mid.astype(F32)).astype(BF16)
    return hi, mid, lo


def _rstd(h):
    return lax.rsqrt(jnp.mean(h * h, axis=-1, keepdims=True) + EPS)


def _rms_bwd(dout, hhat, r, g):
    dg = jnp.sum(dout * hhat, axis=0, keepdims=True)
    dxh = dout * g
    dh = r * (dxh - hhat * jnp.mean(dxh * hhat, axis=-1, keepdims=True))
    return dh, dg


def _my_index():
    return 4 * lax.axis_index("x") + 2 * lax.axis_index("y") + lax.axis_index("c")


def _peer(k):
    x, y, c = lax.axis_index("x"), lax.axis_index("y"), lax.axis_index("c")
    px = 1 - x if (k >> 2) & 1 else x
    py = 1 - y if (k >> 1) & 1 else y
    pc = 1 - c if k & 1 else c
    return (px, py, pc), 4 * px + 2 * py + pc


N_PEERS = N_DEV - 1
ANY_SPEC = pl.BlockSpec(memory_space=pl.ANY)


def _exchange_sems(n):
    return [
        pltpu.SemaphoreType.DMA((n * N_PEERS,)),
        pltpu.SemaphoreType.DMA((n * N_PEERS,)),
        pltpu.SemaphoreType.DMA((n,)),
    ]


def _exchange_shapes(blocks, gather):
    if gather:
        return [jax.ShapeDtypeStruct((N_DEV,) + b.shape, b.dtype) for b in blocks]
    return [jax.ShapeDtypeStruct(b.shape, b.dtype) for b in blocks]


def _direct_exchange(ins, outs, sems, gather):
    send_sems, recv_sems, local_sems = sems
    me = _my_index()
    own, sends, recvs = [], [], []
    for t in range(len(ins)):
        own.append(pltpu.make_async_copy(ins[t] if gather else ins[t].at[me], outs[t].at[me], local_sems.at[t]))
        for k in range(1, N_DEV):
            peer, peer_idx = _peer(k)
            src = ins[t] if gather else ins[t].at[peer_idx]
            for dst, bucket in ((outs[t].at[me], sends), (outs[t].at[peer_idx], recvs)):
                bucket.append(
                    pltpu.make_async_remote_copy(
                        src_ref=src,
                        dst_ref=dst,
                        send_sem=send_sems.at[t * N_PEERS + k - 1],
                        recv_sem=recv_sems.at[t * N_PEERS + k - 1],
                        device_id=peer,
                        device_id_type=MESH,
                    )
                )

    def start():
        for cp in own + sends:
            cp.start()

    def finish():
        for cp in recvs:
            cp.wait_recv()
        for cp in sends:
            cp.wait_send()
        for cp in own:
            cp.wait()

    return start, finish


def _two_level_gather(ins, outs, sems):
    send_sems, recv_sems, local_sems = sems
    x, y, c = lax.axis_index("x"), lax.axis_index("y"), lax.axis_index("c")
    me, sibling = (x, y, c), (x, y, 1 - c)
    chips = [(1 - x, y), (x, 1 - y), (1 - x, 1 - y)]

    def copy(t, k, block, to, from_input=False):
        slot = outs[t].at[4 * block[0] + 2 * block[1] + block[2]]
        return pltpu.make_async_remote_copy(
            src_ref=ins[t] if from_input else slot,
            dst_ref=slot,
            send_sem=send_sems.at[t * N_PEERS + k],
            recv_sem=recv_sems.at[t * N_PEERS + k],
            device_id=to,
            device_id_type=MESH,
        )

    arrays = range(len(ins))
    own = [pltpu.make_async_copy(ins[t], outs[t].at[4 * x + 2 * y + c], local_sems.at[t]) for t in arrays]
    first = [copy(t, 1 + j, me, (*chip, c), True) for t in arrays for j, chip in enumerate(chips)]
    first += [copy(t, 0, me, sibling, True) for t in arrays]
    passed = [(copy(t, 1 + j, (*chip, c), me), copy(t, 4 + j, (*chip, c), sibling))
              for t in arrays for j, chip in enumerate(chips)]
    last = [copy(t, 0, sibling, me) for t in arrays]
    last += [copy(t, 4 + j, (*chip, 1 - c), me) for t in arrays for j, chip in enumerate(chips)]

    def start():
        for cp in own + first:
            cp.start()

    def forward():
        for landed, onward in passed:
            landed.wait_recv()
            onward.start()

    def finish():
        for cp in last:
            cp.wait_recv()
        for cp in first + [onward for _, onward in passed]:
            cp.wait_send()
        for cp in own:
            cp.wait()

    return start, forward, finish


def _exchange(blocks, gather, name):
    n = len(blocks)

    def body(*refs):
        if gather:
            stages = _two_level_gather(refs[:n], refs[n : 2 * n], refs[2 * n :])
        else:
            stages = _direct_exchange(refs[:n], refs[n : 2 * n], refs[2 * n :], False)
        for stage in stages:
            stage()

    return pl.pallas_call(
        body,
        name=name,
        out_shape=_exchange_shapes(blocks, gather),
        in_specs=[ANY_SPEC] * n,
        out_specs=[ANY_SPEC] * n,
        scratch_shapes=_exchange_sems(n),
        compiler_params=pltpu.CompilerParams(has_side_effects=True),
    )(*blocks)


def _adam(w, g, m, v):
    m2 = ADAM_B1 * m + (1.0 - ADAM_B1) * g
    v2 = ADAM_B2 * v + (1.0 - ADAM_B2) * jnp.square(g)
    m_hat = m2 / (1.0 - ADAM_B1**ADAM_STEP)
    v_hat = v2 / (1.0 - ADAM_B2**ADAM_STEP)
    delta = -ADAM_LR * (m_hat / (jnp.sqrt(v_hat) + ADAM_EPS) + ADAM_WD * w)
    return delta, m2, v2


def _reduce_adam(land, w, m, v, name):
    rows, cols = w.shape
    tr = min(rows, max(8, (1 << 19) // cols))

    def body(l_ref, w_ref, m_ref, v_ref, g_ref, d_ref, m2_ref, v2_ref):
        g = l_ref[0].astype(F32)
        for s in range(1, N_DEV):
            g = g + l_ref[s].astype(F32)
        g_ref[...] = g
        delta, m2, v2 = _adam(w_ref[...], g, m_ref[...], v_ref[...])
        d_ref[...] = delta
        m2_ref[...] = m2
        v2_ref[...] = v2

    tile = pl.BlockSpec((tr, cols), lambda i: (i, 0))
    sds = jax.ShapeDtypeStruct((rows, cols), F32)
    return pl.pallas_call(
        body,
        name=name,
        grid=(rows // tr,),
        out_shape=[sds] * 4,
        in_specs=[pl.BlockSpec((N_DEV, tr, cols), lambda i: (0, i, 0)), tile, tile, tile],
        out_specs=[tile] * 4,
        compiler_params=_params(("parallel",)),
    )(land, w, m, v)


TAIL_STEPS = 8


def _reduce_adam_tail(lands, ws, ms, vs, gp_in, smalls):
    nw, ns = len(ws), len(smalls)
    tiles = [w.shape[0] // TAIL_STEPS for w in ws]

    def body(*refs):
        gp_hbm = refs[0]
        part_hbm = refs[1 : 1 + ns]
        refs = refs[1 + ns :]
        land_refs = refs[:nw]
        w_refs, m_refs, v_refs = (refs[nw * (i + 1) : nw * (i + 2)] for i in range(3))
        refs = refs[4 * nw :]
        sw_refs, sm_refs, sv_refs = (refs[ns * i : ns * (i + 1)] for i in range(3))
        outs = refs[3 * ns :]
        land_in = outs[0]
        big_out = outs[1 : 1 + 4 * nw]
        small_out = outs[1 + 4 * nw : 1 + 4 * nw + 4 * ns]
        small_land = outs[1 + 4 * nw + 4 * ns : 1 + 4 * nw + 5 * ns]
        sems = outs[1 + 4 * nw + 5 * ns :]
        step = pl.program_id(0)

        def exchanges():
            return (
                _direct_exchange([gp_hbm], [land_in], sems[:3], False),
                _direct_exchange(part_hbm, small_land, sems[3:], True),
            )

        @pl.when(step == 0)
        def _():
            for start, _ in exchanges():
                start()

        for t in range(nw):
            g = land_refs[t][0].astype(F32)
            for s in range(1, N_DEV):
                g = g + land_refs[t][s].astype(F32)
            for ref, val in zip(big_out[4 * t : 4 * t + 4], (g,) + _adam(w_refs[t][...], g, m_refs[t][...], v_refs[t][...])):
                ref[...] = val

        @pl.when(step == TAIL_STEPS - 1)
        def _():
            for _, finish in exchanges():
                finish()
            for i in range(ns):
                g = small_land[i][0]
                for s in range(1, N_DEV):
                    g = g + small_land[i][s]
                results = (g,) + _adam(sw_refs[i][...], g, sm_refs[i][...], sv_refs[i][...])
                for ref, val in zip(small_out[4 * i : 4 * i + 4], results):
                    ref[...] = val

    def tile(t):
        return pl.BlockSpec((tiles[t], ws[t].shape[1]), lambda i: (i, 0))

    def land_tile(t):
        return pl.BlockSpec((N_DEV, tiles[t], ws[t].shape[1]), lambda i: (0, i, 0))

    def whole(i):
        return pl.BlockSpec(smalls[i][0].shape, lambda step: (0, 0))

    big_sds = [jax.ShapeDtypeStruct(ws[t].shape, F32) for t in range(nw) for _ in range(4)]
    small_sds = [jax.ShapeDtypeStruct(smalls[i][0].shape, F32) for i in range(ns) for _ in range(4)]
    return pl.pallas_call(
        body,
        name="reduce_adam_tail",
        grid=(TAIL_STEPS,),
        out_shape=_exchange_shapes([gp_in], False) + big_sds + small_sds,
        in_specs=[ANY_SPEC] * (1 + ns)
        + [land_tile(t) for t in range(nw)]
        + [tile(t) for _ in range(3) for t in range(nw)]
        + [whole(i) for _ in range(3) for i in range(ns)],
        out_specs=[ANY_SPEC]
        + [tile(t) for t in range(nw) for _ in range(4)]
        + [whole(i) for i in range(ns) for _ in range(4)],
        scratch_shapes=[pltpu.VMEM((N_DEV,) + smalls[i][0].shape, F32) for i in range(ns)]
        + _exchange_sems(1)
        + _exchange_sems(ns),
        compiler_params=pltpu.CompilerParams(
            dimension_semantics=("arbitrary",), vmem_limit_bytes=VMEM_LIMIT, has_side_effects=True
        ),
    )(
        gp_in, *[s[0] for s in smalls], *lands, *ws, *ms, *vs,
        *[s[1] for s in smalls], *[s[2] for s in smalls], *[s[3] for s in smalls],
    )


def _fwd_in(x, g1, w_in):
    S = x.shape[0]
    tm = min(512, S)

    def body(x_ref, g_ref, w_ref, hn_ref, u_ref, q_ref, k_ref, v_ref):
        xv = x_ref[...]
        hn = (xv * _rstd(xv) * g_ref[...]).astype(BF16)
        hn_ref[...] = hn
        outs = (u_ref, q_ref, k_ref, v_ref)
        for j in range(N_DEV):
            p = _dot(hn, w_ref[j])
            cols = slice(256 * (j % 2), 256 * (j % 2 + 1))
            if j // 2 == 0:
                u_ref[:, cols] = p
            elif j // 2 == 1:
                q_ref[:, cols] = (p * (HEAD_DIM**-0.5)).astype(BF16)
            else:
                outs[j // 2][:, cols] = p.astype(BF16)

    half = pl.BlockSpec((tm, D_POOL), lambda i: (i, 0))
    return pl.pallas_call(
        body,
        name="fwd_in",
        grid=(S // tm,),
        out_shape=[
            jax.ShapeDtypeStruct((S, D_MODEL), BF16),
            jax.ShapeDtypeStruct((S, D_POOL), F32),
            jax.ShapeDtypeStruct((S, D_ATTN), BF16),
            jax.ShapeDtypeStruct((S, D_ATTN), BF16),
            jax.ShapeDtypeStruct((S, D_ATTN), BF16),
        ],
        in_specs=[
            pl.BlockSpec((tm, D_MODEL), lambda i: (i, 0)),
            pl.BlockSpec((1, D_MODEL), lambda i: (0, 0)),
            pl.BlockSpec((N_DEV, D_MODEL, 256), lambda i: (0, 0, 0)),
        ],
        out_specs=[pl.BlockSpec((tm, D_MODEL), lambda i: (i, 0)), half, half, half, half],
        compiler_params=_params(("parallel",)),
    )(x, g1, w_in)


POOL_CHUNK = 256
_POOL_W_SPEC = pl.BlockSpec((1, GROUP_DIM, GROUP_DIM), lambda g: (g, 0, 0))
_POOL_SCALE_SPEC = pl.BlockSpec((1, GROUP_DIM), lambda g: (0, g))


def _lane_sum(wide, n):
    out = wide[:, :GROUP_DIM]
    for i in range(1, n):
        out = out + wide[:, GROUP_DIM * i : GROUP_DIM * (i + 1)]
    return out


def _pool_fwd_group(u_ref, pw_ref, sc_ref, pooled_ref, y_ref, g):
    S = u_ref.shape[0]
    R = min(POOL_CHUNK, S)
    w = jnp.left_shift(jnp.int32(2), g)
    d = lax.broadcasted_iota(jnp.int32, (R, R), 0) - lax.broadcasted_iota(jnp.int32, (R, R), 1)
    b_cur = jnp.where((d >= 0) & (d < w), 1.0, 0.0).astype(BF16)
    dp = lax.broadcasted_iota(jnp.int32, (R, HALO), 0) + HALO - lax.broadcasted_iota(jnp.int32, (R, HALO), 1)
    b_prev = jnp.where(dp < w, 1.0, 0.0).astype(BF16)
    pw = pw_ref[0].astype(BF16)
    sc = sc_ref[...]

    def chunk(r0, first):
        cur = u_ref[pl.ds(r0, R), :]
        wide = _dot(b_cur, jnp.concatenate(_split3(cur), axis=1))
        if not first:
            prev = u_ref[pl.ds(pl.multiple_of(r0 - HALO, HALO), HALO), :]
            wide = wide + _dot(b_prev, jnp.concatenate(_split3(prev), axis=1))
        count = jnp.minimum(r0 + lax.broadcasted_iota(jnp.int32, (R, 1), 0) + 1, w).astype(F32)
        pooled = (_lane_sum(wide, 3) / count - cur).astype(BF16)
        pooled_ref[pl.ds(r0, R), :] = pooled
        y_ref[pl.ds(r0, R), :] = _dot(pooled, pw) * sc

    chunk(0, True)

    def rest(c, carry):
        chunk(pl.multiple_of(c * R, R), False)
        return carry

    lax.fori_loop(1, S // R, rest, 0)


ATTN_WINDOW = 3


def _band_ones(T):
    row = lax.broadcasted_iota(jnp.int32, (T, T), 0)
    col = lax.broadcasted_iota(jnp.int32, (T, T), 1)
    after = jnp.where(row > col, 1.0, 0.0).astype(BF16)
    before = jnp.where(row < col, 1.0, 0.0).astype(BF16)
    return jnp.concatenate([after, after], axis=0), jnp.concatenate([before, before], axis=0), col < row


def _log1m(z, mask):
    sp = jnp.log(1.0 + jnp.exp(-jnp.abs(z)))
    l1m = -jnp.maximum(z, 0.0) - sp
    return (l1m if mask is None else jnp.where(mask, l1m, 0.0)), sp


def _log_gates(qhs, kbs, masks, after2):
    zs = [[_dot_nt(qh, kb) for qh in qhs] for kb in kbs]
    terms = []
    for row, mask in zip(zs, masks):
        terms.append([])
        for z in row:
            l1m, sp = _log1m(z, mask)
            terms[-1].append(
                (jnp.minimum(z, 0.0) - sp, jnp.concatenate(_split2(l1m), axis=1), jnp.sum(l1m, axis=1, keepdims=True))
            )
    return [[(ls, _dot(split, after2), total) for ls, split, total in row] for row in terms]


def _tile_start(index, T):
    return index * T if isinstance(index, int) else pl.multiple_of(index * T, T)


def _weights(ls, tail, carry, mask):
    a = jnp.exp(ls + tail + carry)
    return a if mask is None else jnp.where(mask, a, 0.0)


def _reaches(carry):
    return jnp.max(carry) > EXP_UNDERFLOW


def _mixers_fwd(q, k, v, u, pool_w, scale, shards):
    S = q.shape[0]
    T = ATTN_TILE
    nq = S // T
    n = len(shards)
    steps = D_ATTN // LANES

    def body(q_ref, k_ref, v_ref, u_ref, pw_ref, sc_ref, *rest):
        o_ref, pooled_ref, yp_ref = rest[n : n + 3]
        step = pl.program_id(0)

        def gather():
            return _two_level_gather(rest[:n], rest[n + 3 : 2 * n + 3], rest[2 * n + 3 :])

        @pl.when(step == 0)
        def _():
            gather()[0]()

        _pool_fwd_group(u_ref, pw_ref, sc_ref, pooled_ref, yp_ref, step)

        lane = lax.broadcasted_iota(jnp.int32, (T, LANES), 1)
        heads = (lane < HEAD_DIM, lane >= HEAD_DIM)
        after2, _, causal = _band_ones(T)
        zero = jnp.zeros((T, 1), F32)

        def key_tile(k0):
            vb = v_ref[pl.ds(k0, T), :]
            return k_ref[pl.ds(k0, T), :], jnp.concatenate([jnp.where(m, vb, jnp.zeros_like(vb)) for m in heads], axis=0)

        def q_tile(qi, nb):
            q0 = _tile_start(qi, T)
            qb = q_ref[pl.ds(q0, T), :]
            qhs = [jnp.where(m, qb, jnp.zeros_like(qb)) for m in heads]
            tiles = [key_tile(_tile_start(qi - b, T)) for b in range(nb)]
            values = [v_rows for _, v_rows in tiles]
            gates = _log_gates(qhs, [kb for kb, _ in tiles], [causal] + [None] * (nb - 1), after2)
            carries, probs = [zero, zero], []
            for b in range(nb):
                for h in range(2):
                    ls, tail, total = gates[b][h]
                    probs.append(_weights(ls, tail, carries[h], causal if b == 0 else None).astype(BF16))
                    carries[h] = carries[h] + total
            acc = _dot(jnp.concatenate(probs, axis=1), jnp.concatenate(values, axis=0))

            def more(st):
                return (st[0] > 0) & _reaches(jnp.maximum(st[1], st[2]))

            def k_step(st):
                kj = st[0] - 1
                kb, v_rows = key_tile(pl.multiple_of(kj * T, T))
                new, probs = [], []
                for (ls, tail, total), carry in zip(_log_gates(qhs, [kb], [None], after2)[0], st[1:3]):
                    probs.append(_weights(ls, tail, carry, None).astype(BF16))
                    new.append(carry + total)
                return kj, new[0], new[1], st[3] + _dot(jnp.concatenate(probs, axis=1), v_rows)

            if not isinstance(qi, int):
                acc = lax.while_loop(more, k_step, (qi - (nb - 1), carries[0], carries[1], acc))[3]
            o_ref[pl.ds(q0, T), :] = acc

        for qi in range(min(ATTN_WINDOW - 1, nq)):
            q_tile(qi, qi + 1)

        def q_loop(qi, carry):
            q_tile(qi, ATTN_WINDOW)
            return carry

        lax.fori_loop(ATTN_WINDOW - 1, nq, q_loop, 0)

        @pl.when(step == steps - 1)
        def _():
            _, forward, finish = gather()
            forward()
            finish()

    blk = pl.BlockSpec((S, LANES), lambda p: (0, p))
    return pl.pallas_call(
        body,
        name="mixers_fwd",
        grid=(steps,),
        out_shape=[
            jax.ShapeDtypeStruct((S, D_ATTN), F32),
            jax.ShapeDtypeStruct((S, D_POOL), BF16),
            jax.ShapeDtypeStruct((S, D_POOL), F32),
        ]
        + _exchange_shapes(shards, True),
        in_specs=[blk, blk, blk, blk, _POOL_W_SPEC, _POOL_SCALE_SPEC] + [ANY_SPEC] * n,
        out_specs=[blk, blk, blk] + [ANY_SPEC] * n,
        scratch_shapes=_exchange_sems(n),
        compiler_params=pltpu.CompilerParams(
            dimension_semantics=("arbitrary",), vmem_limit_bytes=VMEM_LIMIT, has_side_effects=True
        ),
    )(q, k, v, u, pool_w, scale, *shards)


SG_ROWS = 8


def _mlp_fwd_bwd(x, y_pool, y_attn, target, g_pool, g_attn, g2, gf, w_out, w_up, w_down):
    S = x.shape[0]
    tm = min(256, S)
    fc = D_FF // N_DEV

    def body(x_ref, yp_ref, ya_ref, t_ref, gp_ref, ga_ref, g2_ref, gf_ref, wo_hbm, wu_hbm, wd_hbm,
             mixed_ref, hn2_ref, act_ref, dup_ref, dh2_ref, dh1_ref, dh1b_ref, dyp_ref, dya_ref, sg_ref,
             wo, wu, wd, up_s, sems):
        @pl.when(pl.program_id(0) == 0)
        def _():
            copies = [
                pltpu.make_async_copy(wo_hbm, wo, sems.at[0]),
                pltpu.make_async_copy(wu_hbm, wu, sems.at[1]),
                pltpu.make_async_copy(wd_hbm, wd, sems.at[2]),
            ]
            for cp in copies:
                cp.start()
            for cp in copies:
                cp.wait()
            sg_ref[...] = jnp.zeros_like(sg_ref)

        gp, ga, g2v, gfv = gp_ref[...], ga_ref[...], g2_ref[...], gf_ref[...]
        yp, ya = yp_ref[...], ya_ref[...]
        rp, ra = _rstd(yp), _rstd(ya)
        yph, yah = yp * rp, ya * ra
        mixed = jnp.concatenate([(yph * gp).astype(BF16), (yah * ga).astype(BF16)], axis=1)
        mixed_ref[...] = mixed
        h1 = x_ref[...] + _dot(mixed, wo[...])
        r2 = _rstd(h1)
        h1h = h1 * r2
        hn2 = (h1h * g2v).astype(BF16)
        hn2_ref[...] = hn2
        h2 = h1
        for j in range(N_DEV):
            cols = slice(fc * j, fc * (j + 1))
            up = _dot(hn2, wu[j])
            up_s[:, cols] = up
            act = jnp.square(jnp.maximum(up, 0.0)).astype(BF16)
            act_ref[:, cols] = act
            h2 = h2 + _dot(act, wd[cols, :])
        rf = _rstd(h2)
        h2h = h2 * rf
        diff = h2h * gfv - t_ref[...]
        loss_rows = 0.5 * jnp.mean(diff * diff, axis=-1, keepdims=True)
        dy = diff * (1.0 / D_MODEL)
        dh2, dgf = _rms_bwd(dy, h2h, rf, gfv)
        dh2b = dh2.astype(BF16)
        dh2_ref[...] = dh2b
        dhn2 = jnp.zeros((tm, D_MODEL), F32)
        for j in range(N_DEV):
            cols = slice(fc * j, fc * (j + 1))
            dact = _dot_nt(dh2b, wd[cols, :])
            dup = (dact * (2.0 * jnp.maximum(up_s[:, cols], 0.0))).astype(BF16)
            dup_ref[:, cols] = dup
            dhn2 = dhn2 + _dot_nt(dup, wu[j])
        dh1n, dg2 = _rms_bwd(dhn2, h1h, r2, g2v)
        dh1 = dh2 + dh1n
        dh1_ref[...] = dh1
        dh1b = dh1.astype(BF16)
        dh1b_ref[...] = dh1b
        dmix = _dot_nt(dh1b, wo[...])
        dyp, dgp = _rms_bwd(dmix[:, :D_POOL], yph, rp, gp)
        dya, dga = _rms_bwd(dmix[:, D_POOL:], yah, ra, ga)
        dyp_ref[...] = dyp
        dya_ref[...] = dya
        sg_ref[0:1, :] += dgf
        sg_ref[1:2, :] += dg2
        sg_ref[2:3, :] += jnp.concatenate([dgp, dga], axis=1)
        sg_ref[3:4, :] += jnp.broadcast_to(jnp.sum(loss_rows, axis=0, keepdims=True), (1, D_MODEL))

    def tok(n):
        return pl.BlockSpec((tm, n), lambda i: (i, 0))

    def vec(n):
        return pl.BlockSpec((1, n), lambda i: (0, 0))

    any_spec = pl.BlockSpec(memory_space=pl.ANY)
    return pl.pallas_call(
        body,
        name="mlp_fwd_bwd",
        grid=(S // tm,),
        out_shape=[
            jax.ShapeDtypeStruct((S, D_MODEL), BF16),
            jax.ShapeDtypeStruct((S, D_MODEL), BF16),
            jax.ShapeDtypeStruct((S, D_FF), BF16),
            jax.ShapeDtypeStruct((S, D_FF), BF16),
            jax.ShapeDtypeStruct((S, D_MODEL), BF16),
            jax.ShapeDtypeStruct((S, D_MODEL), F32),
            jax.ShapeDtypeStruct((S, D_MODEL), BF16),
            jax.ShapeDtypeStruct((S, D_POOL), F32),
            jax.ShapeDtypeStruct((S, D_ATTN), F32),
            jax.ShapeDtypeStruct((SG_ROWS, D_MODEL), F32),
        ],
        in_specs=[tok(D_MODEL), tok(D_POOL), tok(D_ATTN), tok(D_MODEL), vec(D_POOL), vec(D_ATTN),
                  vec(D_MODEL), vec(D_MODEL), any_spec, any_spec, any_spec],
        out_specs=[tok(D_MODEL), tok(D_MODEL), tok(D_FF), tok(D_FF), tok(D_MODEL), tok(D_MODEL),
                   tok(D_MODEL), tok(D_POOL), tok(D_ATTN),
                   pl.BlockSpec((SG_ROWS, D_MODEL), lambda i: (0, 0))],
        scratch_shapes=[
            pltpu.VMEM((D_MODEL, D_MODEL), BF16),
            pltpu.VMEM((N_DEV, D_MODEL, fc), BF16),
            pltpu.VMEM((D_FF, D_MODEL), BF16),
            pltpu.VMEM((tm, D_FF), F32),
            pltpu.SemaphoreType.DMA((3,)),
        ],
        compiler_params=_params(("arbitrary",)),
    )(x, y_pool, y_attn, target, g_pool, g_attn, g2, gf, w_out, w_up, w_down)


def _wgrad(a, b, block_a, groups, name, travelling=()):
    n = len(travelling)
    S, ka = a.shape
    nb = b.shape[1]
    ts = min(1024, S)
    per = N_DEV // groups
    if block_a:
        ka //= groups
        blk = (ka // per, nb)
        a_spec = pl.BlockSpec((ts, ka), lambda g, s: (s, g))
        b_spec = pl.BlockSpec((ts, nb), lambda g, s: (s, 0))
    else:
        nb //= groups
        blk = (ka, nb // per)
        a_spec = pl.BlockSpec((ts, ka), lambda g, s: (s, 0))
        b_spec = pl.BlockSpec((ts, nb), lambda g, s: (s, g))
    steps = S // ts

    def body(a_ref, b_ref, *rest):
        o_ref, acc = rest[n], rest[2 * n + 1]
        g, s = pl.program_id(0), pl.program_id(1)

        def scatter():
            return _direct_exchange(rest[:n], rest[n + 1 : 2 * n + 1], rest[2 * n + 2 :], False)

        if n:
            @pl.when((g == 0) & (s == 0))
            def _():
                scatter()[0]()

        @pl.when(s == 0)
        def _():
            acc[...] = jnp.zeros_like(acc)

        acc[...] += _dot_tn(a_ref[...], b_ref[...])

        @pl.when(s == steps - 1)
        def _():
            for j in range(per):
                if block_a:
                    o_ref[j] = acc[blk[0] * j : blk[0] * (j + 1), :].astype(BF16)
                else:
                    o_ref[j] = acc[:, blk[1] * j : blk[1] * (j + 1)].astype(BF16)

        if n:
            @pl.when((g == groups - 1) & (s == steps - 1))
            def _():
                scatter()[1]()

    results = pl.pallas_call(
        body,
        name=name,
        grid=(groups, steps),
        out_shape=[jax.ShapeDtypeStruct((N_DEV,) + blk, BF16)] + _exchange_shapes(travelling, False),
        in_specs=[a_spec, b_spec] + [ANY_SPEC] * n,
        out_specs=[pl.BlockSpec((per,) + blk, lambda g, s: (g, 0, 0))] + [ANY_SPEC] * n,
        scratch_shapes=[pltpu.VMEM((ka, nb), F32)] + (_exchange_sems(n) if n else []),
        compiler_params=pltpu.CompilerParams(
            dimension_semantics=("arbitrary", "arbitrary"), vmem_limit_bytes=VMEM_LIMIT, has_side_effects=bool(n)
        ),
    )(a, b, *travelling)
    return results if n else results[0]


def _mixers_bwd(q, k, v, do, dyp, pooled, pool_w, scale, partials):
    S = q.shape[0]
    T = ATTN_TILE
    nq = S // T
    n = len(partials)
    steps = D_ATTN // LANES

    def body(q_ref, k_ref, v_ref, do_ref, dyp_ref, pooled_ref, pw_ref, sc_ref, *rest):
        dq_ref, dk_ref, dv_ref, du_ref, dsc_ref, dpw_ref = rest[n : n + 6]
        dk_acc, dv_acc, carry_s = rest[2 * n + 6 : 2 * n + 9]
        step = pl.program_id(0)

        def scatter():
            return _direct_exchange(rest[:n], rest[n + 6 : 2 * n + 6], rest[2 * n + 9 :], False)

        @pl.when(step == 0)
        def _():
            scatter()[0]()

        _pool_bwd_group(dyp_ref, pooled_ref, pw_ref, sc_ref, du_ref, dsc_ref, dpw_ref, step)

        dk_acc[...] = jnp.zeros_like(dk_acc)
        dv_acc[...] = jnp.zeros_like(dv_acc)
        lane = lax.broadcasted_iota(jnp.int32, (T, LANES), 1)
        heads = (lane < HEAD_DIM, lane >= HEAD_DIM)
        after2, before2, causal = _band_ones(T)
        zero = jnp.zeros((T, 1), F32)

        def key_tile(k0):
            kb = k_ref[pl.ds(k0, T), :]
            k_rows = jnp.concatenate([jnp.where(m, kb, jnp.zeros_like(kb)) for m in heads], axis=0)
            return kb, v_ref[pl.ds(k0, T), :], k_rows

        def q_tile(qi, nb):
            q0 = _tile_start(qi, T)
            qb = q_ref[pl.ds(q0, T), :]
            dob = do_ref[pl.ds(q0, T), :]
            qhs = [jnp.where(m, qb, jnp.zeros_like(qb)) for m in heads]
            dohs = [jnp.where(m, dob, 0.0).astype(BF16) for m in heads]
            q_rows = jnp.concatenate(qhs, axis=0)
            do_rows = jnp.concatenate(dohs, axis=0)

            def grad_tiles(starts, tiles, masks, carry_in, gates, st):
                nt = len(tiles)
                das = [[_dot_nt(doh, vb) for doh in dohs] for _, vb, _ in tiles]
                probs, gs = [], []
                for b in range(nt):
                    probs.append([_weights(gates[b][h][0], gates[b][h][1], carry_in[b][h], masks[b]) for h in range(2)])
                    gs.append([probs[b][h] * das[b][h] for h in range(2)])
                before = [[_dot(jnp.concatenate(_split2(g), axis=1), before2) for g in row] for row in gs]
                g_left, dzs = [st[0], st[1]], [None] * nt
                for b in reversed(range(nt)):
                    dzs[b] = []
                    for h in range(2):
                        sig = jnp.exp(gates[b][h][0])
                        dz = gs[b][h] * (1.0 - sig) - sig * (g_left[h] + before[b][h])
                        if masks[b] is not None:
                            dz = jnp.where(masks[b], dz, 0.0)
                        dzs[b].append(dz.astype(BF16))
                        g_left[h] = g_left[h] + jnp.sum(gs[b][h], axis=1, keepdims=True)
                dq = st[2] + _dot(
                    jnp.concatenate([dz for row in dzs for dz in row], axis=1),
                    jnp.concatenate([k_rows for _, _, k_rows in tiles], axis=0),
                )
                for b in range(nt):
                    dk_acc[pl.ds(starts[b], T), :] += _dot_tn(jnp.concatenate(dzs[b], axis=0), q_rows)
                    dv_acc[pl.ds(starts[b], T), :] += _dot_tn(
                        jnp.concatenate([a.astype(BF16) for a in probs[b]], axis=0), do_rows
                    )
                return g_left[0], g_left[1], dq

            starts = [_tile_start(qi - b, T) for b in range(nb)]
            tiles = [key_tile(k0) for k0 in starts]
            masks = [causal] + [None] * (nb - 1)
            gates = _log_gates(qhs, [kb for kb, _, _ in tiles], masks, after2)
            carries, carry_in = [zero, zero], []
            for b in range(nb):
                carry_in.append(list(carries))
                carries = [carries[h] + gates[b][h][2] for h in range(2)]

            st = (zero, zero, jnp.zeros((T, LANES), F32))
            if not isinstance(qi, int):
                k_left = qi - (nb - 1)

                def more(st):
                    return (st[0] > 0) & _reaches(jnp.maximum(st[1], st[2]))

                def right_to_left(st):
                    kj = st[0] - 1
                    carry_s[0, kj] = st[1]
                    carry_s[1, kj] = st[2]
                    kb = k_ref[pl.ds(pl.multiple_of(kj * T, T), T), :]
                    sums = [jnp.sum(_log1m(_dot_nt(qh, kb), None)[0], axis=1, keepdims=True) for qh in qhs]
                    return kj, st[1] + sums[0], st[2] + sums[1]

                k_first = lax.while_loop(more, right_to_left, (k_left, carries[0], carries[1]))[0]

                def left_to_right(kj, st):
                    k0 = pl.multiple_of(kj * T, T)
                    tile = key_tile(k0)
                    left_gates = _log_gates(qhs, [tile[0]], [None], after2)
                    return grad_tiles([k0], [tile], [None], [[carry_s[0, kj], carry_s[1, kj]]], left_gates, st)

                st = lax.fori_loop(k_first, k_left, left_to_right, st)
            st = grad_tiles(starts, tiles, masks, carry_in, gates, st)
            dq_ref[pl.ds(q0, T), :] = (st[2] * (HEAD_DIM**-0.5)).astype(BF16)

        for qi in range(min(ATTN_WINDOW - 1, nq)):
            q_tile(qi, qi + 1)

        def q_loop(qi, carry):
            q_tile(qi, ATTN_WINDOW)
            return carry

        lax.fori_loop(ATTN_WINDOW - 1, nq, q_loop, 0)
        dk_ref[...] = dk_acc[...].astype(BF16)
        dv_ref[...] = dv_acc[...].astype(BF16)

        @pl.when(step == steps - 1)
        def _():
            scatter()[1]()

    blk = pl.BlockSpec((S, LANES), lambda p: (0, p))
    sds = jax.ShapeDtypeStruct((S, D_ATTN), BF16)
    return pl.pallas_call(
        body,
        name="mixers_bwd",
        grid=(steps,),
        out_shape=[
            sds,
            sds,
            sds,
            jax.ShapeDtypeStruct((S, D_POOL), BF16),
            jax.ShapeDtypeStruct((8, D_POOL), F32),
            jax.ShapeDtypeStruct((N_GROUPS, GROUP_DIM, GROUP_DIM), F32),
        ]
        + _exchange_shapes(partials, False),
        in_specs=[blk] * 6 + [_POOL_W_SPEC, _POOL_SCALE_SPEC] + [ANY_SPEC] * n,
        out_specs=[blk] * 4 + [pl.BlockSpec((8, GROUP_DIM), lambda p: (0, p)), _POOL_W_SPEC] + [ANY_SPEC] * n,
        scratch_shapes=[pltpu.VMEM((S, LANES), F32), pltpu.VMEM((S, LANES), F32), pltpu.VMEM((2, nq, T, 1), F32)]
        + _exchange_sems(n),
        compiler_params=pltpu.CompilerParams(
            dimension_semantics=("arbitrary",), vmem_limit_bytes=VMEM_LIMIT, has_side_effects=True
        ),
    )(q, k, v, do, dyp, pooled, pool_w, scale, *partials)


def _pool_bwd_group(dy_ref, pooled_ref, pw_ref, sc_ref, du_ref, dsc_ref, dpw_ref, g):
    S = dy_ref.shape[0]
    R = min(POOL_CHUNK, S)
    nc = S // R
    w = jnp.left_shift(jnp.int32(2), g)
    pw = pw_ref[0].astype(BF16)
    sc = sc_ref[...]
    d = lax.broadcasted_iota(jnp.int32, (R, R), 1) - lax.broadcasted_iota(jnp.int32, (R, R), 0)
    bt_cur = jnp.where((d >= 0) & (d < w), 1.0, 0.0).astype(BF16)
    dn = lax.broadcasted_iota(jnp.int32, (R, HALO), 1) + R - lax.broadcasted_iota(jnp.int32, (R, HALO), 0)
    bt_next = jnp.where(dn < w, 1.0, 0.0).astype(BF16)

    def per_count(dpl, r0):
        n = dpl.shape[0]
        count = jnp.minimum(r0 + lax.broadcasted_iota(jnp.int32, (n, 1), 0) + 1, w).astype(F32)
        return jnp.concatenate(_split2(dpl / count), axis=1)

    def chunk(r0, last, sums):
        dyv = dy_ref[pl.ds(r0, R), :]
        pooled = pooled_ref[pl.ds(r0, R), :]
        dmapped = (dyv * sc).astype(BF16)
        dsc = sums[0] + jnp.sum(dyv * _dot(pooled, pw), axis=0, keepdims=True)
        dpw = sums[1] + _dot_tn(pooled, dmapped)
        dpl = _dot_nt(dmapped, pw)
        wide = _dot(bt_cur, per_count(dpl, r0))
        if not last:
            nxt = pl.multiple_of(r0 + R, R)
            dpl_next = _dot_nt((dy_ref[pl.ds(nxt, HALO), :] * sc).astype(BF16), pw)
            wide = wide + _dot(bt_next, per_count(dpl_next, nxt))
        du_ref[pl.ds(r0, R), :] = (_lane_sum(wide, 2) - dpl).astype(BF16)
        return dsc, dpw

    sums = (jnp.zeros((1, GROUP_DIM), F32), jnp.zeros((GROUP_DIM, GROUP_DIM), F32))
    sums = lax.fori_loop(0, nc - 1, lambda c, s: chunk(pl.multiple_of(c * R, R), False, s), sums)
    dsc, dpw = chunk((nc - 1) * R, True, sums)
    dsc_ref[...] = jnp.zeros_like(dsc_ref)
    dsc_ref[0:1, :] = dsc
    dpw_ref[0] = dpw


def _bwd_in(du, dq, dk, dv, w_in, x, dh1, g1):
    S = x.shape[0]
    tm = min(512, S)

    def body(du_ref, dq_ref, dk_ref, dv_ref, w_ref, x_ref, dh1_ref, g_ref, dx_ref, dproj_ref, dg_ref):
        @pl.when(pl.program_id(0) == 0)
        def _():
            dg_ref[...] = jnp.zeros_like(dg_ref)

        parts = (du_ref[...], dq_ref[...], dk_ref[...], dv_ref[...])
        dhn = jnp.zeros((tm, D_MODEL), F32)
        for j in range(N_DEV):
            piece = parts[j // 2][:, 256 * (j % 2) : 256 * (j % 2 + 1)]
            dproj_ref[:, 256 * j : 256 * (j + 1)] = piece
            dhn = dhn + _dot_nt(piece, w_ref[j])
        xv = x_ref[...]
        r = _rstd(xv)
        dxn, dg = _rms_bwd(dhn, xv * r, r, g_ref[...])
        dx_ref[...] = dh1_ref[...] + dxn
        dg_ref[0:1, :] += dg

    half = pl.BlockSpec((tm, D_POOL), lambda i: (i, 0))
    full = pl.BlockSpec((tm, D_MODEL), lambda i: (i, 0))
    return pl.pallas_call(
        body,
        name="bwd_in",
        grid=(S // tm,),
        out_shape=[
            jax.ShapeDtypeStruct((S, D_MODEL), F32),
            jax.ShapeDtypeStruct((S, D_IN_PROJ), BF16),
            jax.ShapeDtypeStruct((8, D_MODEL), F32),
        ],
        in_specs=[half, half, half, half,
                  pl.BlockSpec((N_DEV, D_MODEL, 256), lambda i: (0, 0, 0)),
                  full, full, pl.BlockSpec((1, D_MODEL), lambda i: (0, 0))],
        out_specs=[full, pl.BlockSpec((tm, D_IN_PROJ), lambda i: (i, 0)),
                   pl.BlockSpec((8, D_MODEL), lambda i: (0, 0))],
        compiler_params=_params(("arbitrary",)),
    )(du, dq, dk, dv, w_in, x, dh1, g1)


def _rows(a):
    a = a.reshape(-1, LANES)
    pad = (-a.shape[0]) % 8
    return jnp.pad(a, ((0, pad), (0, 0))) if pad else a


def kernel(x, norm1_g, w_in, pool_w, pool_scale, pool_out_g, attn_out_g, w_out, norm2_g, w_up, w_down, final_g, loss_target, m_norm1_g, m_w_in, m_pool_w, m_pool_scale, m_pool_out_g, m_attn_out_g, m_w_out, m_norm2_g, m_w_up, m_w_down, m_final_g, v_norm1_g, v_w_in, v_pool_w, v_pool_scale, v_pool_out_g, v_attn_out_g, v_w_out, v_norm2_g, v_w_up, v_w_down, v_final_g):
    S = x.shape[1]
    xs = x.reshape(S, D_MODEL)
    tgt = loss_target.reshape(S, D_MODEL)
    row = lambda a: a.reshape(1, -1)

    (w_in_g,) = _exchange([w_in.astype(BF16)], True, "gather_w_in")
    hn, u_pool, q, k, v = _fwd_in(xs, row(norm1_g), w_in_g)
    y_attn, pooled, y_pool, w_out_g, w_up_g, w_down_g = _mixers_fwd(
        q, k, v, u_pool, pool_w, row(pool_scale), [w_out.astype(BF16), w_up.astype(BF16), w_down.astype(BF16)]
    )
    w_out_full = w_out_g.reshape(D_MODEL, D_MODEL)
    w_down_full = w_down_g.reshape(D_FF, D_MODEL)
    mixed, hn2, act, dup, dh2b, dh1, dh1b, dyp, dya, sg = _mlp_fwd_bwd(
        xs, y_pool, y_attn, tgt, row(pool_out_g), row(attn_out_g), row(norm2_g), row(final_g),
        w_out_full, w_up_g, w_down_full,
    )
    gp_down = _wgrad(act, dh2b, True, 2, "wgrad_down")
    gp_up = _wgrad(hn2, dup, False, 2, "wgrad_up")
    gp_out = _wgrad(mixed, dh1b, True, 1, "wgrad_out")
    dq, dk, dv, du, dsc, dpw, land_up, land_down = _mixers_bwd(
        q, k, v, dya, dyp, pooled, pool_w, row(pool_scale), [gp_up, gp_down]
    )
    dx, dproj, dg1 = _bwd_in(du, dq, dk, dv, w_in_g, xs, dh1, row(norm1_g))
    gp_in, land_out = _wgrad(hn, dproj, False, 1, "wgrad_in", [gp_out])

    def vectors(final, norm2, pool_out, attn_out, norm1, scale, last):
        pieces = [final, norm2, jnp.concatenate([pool_out, attn_out]), norm1, scale, last]
        return jnp.concatenate([_rows(p) for p in pieces], axis=0)

    no_loss = jnp.zeros((8, LANES), F32)
    smalls = [
        (
            vectors(sg[0], sg[1], sg[2][:D_POOL], sg[2][D_POOL:], dg1[0], dsc[0], sg[3]),
            vectors(final_g, norm2_g, pool_out_g, attn_out_g, norm1_g, pool_scale, no_loss),
            vectors(m_final_g, m_norm2_g, m_pool_out_g, m_attn_out_g, m_norm1_g, m_pool_scale, no_loss),
            vectors(v_final_g, v_norm2_g, v_pool_out_g, v_attn_out_g, v_norm1_g, v_pool_scale, no_loss),
        ),
        tuple(a.reshape(N_GROUPS * GROUP_DIM, GROUP_DIM) for a in (dpw, pool_w, m_pool_w, v_pool_w)),
    ]
    tail = _reduce_adam_tail(
        [land_out, land_up, land_down], [w_out, w_up, w_down], [m_w_out, m_w_up, m_w_down],
        [v_w_out, v_w_up, v_w_down], gp_in, smalls,
    )
    big = {name: tail[1 + 4 * t : 5 + 4 * t] for t, name in enumerate(("w_out", "w_up", "w_down"))}
    big["w_in"] = _reduce_adam(tail[0], w_in, m_w_in, v_w_in, "reduce_adam_w_in")

    def unpack(vec, pw):
        out = {}
        for i, name in enumerate(("final_g", "norm2_g", "mix_g", "norm1_g", "pool_scale")):
            out[name] = vec[8 * i : 8 * i + 8].reshape(-1)
        out["pool_scale"] = out["pool_scale"][:D_POOL]
        out["pool_out_g"], out["attn_out_g"] = out["mix_g"][:D_POOL], out["mix_g"][D_POOL:]
        out["pool_w"] = pw.reshape(N_GROUPS, GROUP_DIM, GROUP_DIM)
        return out, vec[40, 0]

    small_out = [unpack(tail[13 + i], tail[17 + i]) for i in range(4)]
    loss = small_out[0][1]
    order = ("norm1_g", "w_in", "pool_w", "pool_scale", "pool_out_g", "attn_out_g", "w_out", "norm2_g", "w_up",
             "w_down", "final_g")
    outs = [loss, dx.reshape(1, S, D_MODEL)]
    for i in range(4):
        for name in order:
            outs.append(big[name][i] if name in big else small_out[i][0][name])
    return tuple(outs)
```

```python
import functools

import jax
import jax.numpy as jnp
from jax import lax
from jax.experimental import pallas as pl
from jax.experimental.pallas import tpu as pltpu

F32 = jnp.float32
BF16 = jnp.bfloat16
MESH = pl.DeviceIdType.MESH

N_DEV = 8
D_MODEL = 1024
D_POOL = 512
D_ATTN = 512
N_GROUPS = 4
GROUP_DIM = 128
HEAD_DIM = 64
D_FF = 4096
D_IN_PROJ = 2048
EPS = 1e-6
HALO = 16
ATTN_TILE = 128
LANES = 128
EXP_UNDERFLOW = -104.0

ADAM_LR = 0.001
ADAM_B1 = 0.9
ADAM_B2 = 0.999
ADAM_EPS = 1e-08
ADAM_WD = 0.01
ADAM_STEP = 10

VMEM_LIMIT = 56 * 1024 * 1024


def _params(semantics=None, vmem=VMEM_LIMIT):
    return pltpu.CompilerParams(dimension_semantics=semantics, vmem_limit_bytes=vmem)


def _dot(a, b):
    return jnp.dot(a, b, preferred_element_type=F32)


def _dot_nt(a, b):
    return lax.dot_general(a, b, (((1,), (1,)), ((), ())), preferred_element_type=F32)


def _dot_tn(a, b):
    return lax.dot_general(a, b, (((0,), (0,)), ((), ())), preferred_element_type=F32)


def _split2(x):
    hi = x.astype(BF16)
    lo = (x - hi.astype(F32)).astype(BF16)
    return hi, lo


def _split3(x):
    hi = x.astype(BF16)
    r = x - hi.astype(F32)
    mid = r.astype(BF16)
    lo = (r - mid.astype(F32)).astype(BF16)
    return hi, mid, lo


def _rstd(h):
    return lax.rsqrt(jnp.mean(h * h, axis=-1, keepdims=True) + EPS)


def _rms_bwd(dout, hhat, r, g):
    dg = jnp.sum(dout * hhat, axis=0, keepdims=True)
    dxh = dout * g
    dh = r * (dxh - hhat * jnp.mean(dxh * hhat, axis=-1, keepdims=True))
    return dh, dg


def _my_index():
    return 4 * lax.axis_index("x") + 2 * lax.axis_index("y") + lax.axis_index("c")


def _peer(k):
    x, y, c = lax.axis_index("x"), lax.axis_index("y"), lax.axis_index("c")
    px = 1 - x if (k >> 2) & 1 else x
    py = 1 - y if (k >> 1) & 1 else y
    pc = 1 - c if k & 1 else c
    return (px, py, pc), 4 * px + 2 * py + pc


N_PEERS = N_DEV - 1
ANY_SPEC = pl.BlockSpec(memory_space=pl.ANY)


def _exchange_sems(n):
    return [
        pltpu.SemaphoreType.DMA((n * N_PEERS,)),
        pltpu.SemaphoreType.DMA((n * N_PEERS,)),
        pltpu.SemaphoreType.DMA((n,)),
    ]


def _exchange_shapes(blocks, gather):
    if gather:
        return [jax.ShapeDtypeStruct((N_DEV,) + b.shape, b.dtype) for b in blocks]
    return [jax.ShapeDtypeStruct(b.shape, b.dtype) for b in blocks]


def _direct_exchange(ins, outs, sems, gather):
    send_sems, recv_sems, local_sems = sems
    me = _my_index()
    own, sends, recvs = [], [], []
    for t in range(len(ins)):
        own.append(pltpu.make_async_copy(ins[t] if gather else ins[t].at[me], outs[t].at[me], local_sems.at[t]))
        for k in range(1, N_DEV):
            peer, peer_idx = _peer(k)
            src = ins[t] if gather else ins[t].at[peer_idx]
            for dst, bucket in ((outs[t].at[me], sends), (outs[t].at[peer_idx], recvs)):
                bucket.append(
                    pltpu.make_async_remote_copy(
                        src_ref=src,
                        dst_ref=dst,
                        send_sem=send_sems.at[t * N_PEERS + k - 1],
                        recv_sem=recv_sems.at[t * N_PEERS + k - 1],
                        device_id=peer,
                        device_id_type=MESH,
                    )
                )

    def start():
        for cp in own + sends:
            cp.start()

    def finish():
        for cp in recvs:
            cp.wait_recv()
        for cp in sends:
            cp.wait_send()
        for cp in own:
            cp.wait()

    return start, finish


def _two_level_gather(ins, outs, sems):
    send_sems, recv_sems, local_sems = sems
    x, y, c = lax.axis_index("x"), lax.axis_index("y"), lax.axis_index("c")
    me, sibling = (x, y, c), (x, y, 1 - c)
    chips = [(1 - x, y), (x, 1 - y), (1 - x, 1 - y)]

    def copy(t, k, block, to, from_input=False):
        slot = outs[t].at[4 * block[0] + 2 * block[1] + block[2]]
        return pltpu.make_async_remote_copy(
            src_ref=ins[t] if from_input else slot,
            dst_ref=slot,
            send_sem=send_sems.at[t * N_PEERS + k],
            recv_sem=recv_sems.at[t * N_PEERS + k],
            device_id=to,
            device_id_type=MESH,
        )

    arrays = range(len(ins))
    own = [pltpu.make_async_copy(ins[t], outs[t].at[4 * x + 2 * y + c], local_sems.at[t]) for t in arrays]
    first = [copy(t, 1 + j, me, (*chip, c), True) for t in arrays for j, chip in enumerate(chips)]
    first += [copy(t, 0, me, sibling, True) for t in arrays]
    passed = [(copy(t, 1 + j, (*chip, c), me), copy(t, 4 + j, (*chip, c), sibling))
              for t in arrays for j, chip in enumerate(chips)]
    last = [copy(t, 0, sibling, me) for t in arrays]
    last += [copy(t, 4 + j, (*chip, 1 - c), me) for t in arrays for j, chip in enumerate(chips)]

    def start():
        for cp in own + first:
            cp.start()

    def forward():
        for landed, onward in passed:
            landed.wait_recv()
            onward.start()

    def finish():
        for cp in last:
            cp.wait_recv()
        for cp in first + [onward for _, onward in passed]:
            cp.wait_send()
        for cp in own:
            cp.wait()

    return start, forward, finish


N_CHIPS = 4
PAIR_SEMS = (N_CHIPS, N_CHIPS, N_CHIPS - 1, N_CHIPS - 1)


def _pair_reduce_scatter(gp_hbm, own, pair, summed, land, local_sem, sems):
    d2d_send, d2d_recv, ici_send, ici_recv = sems
    x, y, c = lax.axis_index("x"), lax.axis_index("y"), lax.axis_index("c")
    my_chip = 2 * x + y
    sibling = (x, y, 1 - c)
    chips = [(1 - x, y), (x, 1 - y), (1 - x, 1 - y)]
    local = [pltpu.make_async_copy(gp_hbm.at[2 * j + c], own.at[j], local_sem.at[0]) for j in range(N_CHIPS)]
    to_sibling = [
        pltpu.make_async_remote_copy(
            src_ref=gp_hbm.at[2 * j + 1 - c], dst_ref=pair.at[j], send_sem=d2d_send.at[j], recv_sem=d2d_recv.at[j],
            device_id=sibling, device_id_type=MESH,
        )
        for j in range(N_CHIPS)
    ]
    to_chips, from_chips = [], []
    for r, (px, py) in enumerate(chips):
        for dst, bucket in ((land.at[my_chip], to_chips), (land.at[2 * px + py], from_chips)):
            bucket.append(
                pltpu.make_async_remote_copy(
                    src_ref=summed.at[2 * px + py], dst_ref=dst, send_sem=ici_send.at[r], recv_sem=ici_recv.at[r],
                    device_id=(px, py, c), device_id_type=MESH,
                )
            )

    def start():
        for cp in local + to_sibling:
            cp.start()

    def middle():
        for cp in to_sibling:
            cp.wait_recv()
        pltpu.make_async_copy(gp_hbm.at[pl.ds(0, N_CHIPS)], own, local_sem.at[0]).wait()
        for j in range(N_CHIPS):
            summed[j] = (own[j].astype(F32) + pair[j].astype(F32)).astype(BF16)
        for cp in to_chips:
            cp.start()

    def finish():
        for cp in from_chips:
            cp.wait_recv()
        for cp in to_chips + to_sibling:
            cp.wait_send()
        land[my_chip] = summed[my_chip]

    return start, middle, finish


def _exchange(blocks, gather, name):
    n = len(blocks)

    def body(*refs):
        if gather:
            stages = _two_level_gather(refs[:n], refs[n : 2 * n], refs[2 * n :])
        else:
            stages = _direct_exchange(refs[:n], refs[n : 2 * n], refs[2 * n :], False)
        for stage in stages:
            stage()

    return pl.pallas_call(
        body,
        name=name,
        out_shape=_exchange_shapes(blocks, gather),
        in_specs=[ANY_SPEC] * n,
        out_specs=[ANY_SPEC] * n,
        scratch_shapes=_exchange_sems(n),
        compiler_params=pltpu.CompilerParams(has_side_effects=True),
    )(*blocks)


def _adam(w, g, m, v):
    m2 = ADAM_B1 * m + (1.0 - ADAM_B1) * g
    v2 = ADAM_B2 * v + (1.0 - ADAM_B2) * jnp.square(g)
    m_hat = m2 / (1.0 - ADAM_B1**ADAM_STEP)
    v_hat = v2 / (1.0 - ADAM_B2**ADAM_STEP)
    delta = -ADAM_LR * (m_hat / (jnp.sqrt(v_hat) + ADAM_EPS) + ADAM_WD * w)
    return delta, m2, v2


TAIL_STEPS = 8


def _reduce_adam_tail(lands, ws, ms, vs, gp_last, last, smalls):
    nw, ns = len(ws), len(smalls)
    tiles = [w.shape[0] // TAIL_STEPS for w in ws]
    blk = gp_last.shape[1:]

    def body(*refs):
        gp_hbm = refs[0]
        part_hbm = refs[1 : 1 + ns]
        refs = refs[1 + ns :]
        land_refs = refs[:nw]
        w_refs, m_refs, v_refs = (refs[nw * (i + 1) : nw * (i + 2)] for i in range(3))
        last_refs = refs[4 * nw : 4 * nw + 3]
        refs = refs[4 * nw + 3 :]
        sw_refs, sm_refs, sv_refs = (refs[ns * i : ns * (i + 1)] for i in range(3))
        outs = refs[3 * ns :]
        big_out = outs[: 4 * nw]
        last_out = outs[4 * nw : 4 * nw + 4]
        small_out = outs[4 * nw + 4 : 4 * nw + 4 + 4 * ns]
        scratch = outs[4 * nw + 4 + 4 * ns :]
        small_land = scratch[:ns]
        pair_bufs = scratch[ns : ns + 4]
        local_sem = scratch[ns + 4]
        pair_sems = scratch[ns + 5 : ns + 9]
        small_sems = scratch[ns + 9 :]
        step = pl.program_id(0)

        def pair():
            return _pair_reduce_scatter(gp_hbm, *pair_bufs, local_sem, pair_sems)

        def small():
            return _direct_exchange(part_hbm, small_land, small_sems, True)

        @pl.when(step == 0)
        def _():
            pair()[0]()
            small()[0]()

        @pl.when(step == 1)
        def _():
            pair()[1]()

        for t in range(nw):
            g = land_refs[t][0].astype(F32)
            for s in range(1, N_DEV):
                g = g + land_refs[t][s].astype(F32)
            for ref, val in zip(big_out[4 * t : 4 * t + 4], (g,) + _adam(w_refs[t][...], g, m_refs[t][...], v_refs[t][...])):
                ref[...] = val

        @pl.when(step == TAIL_STEPS - 1)
        def _():
            pair()[2]()
            small()[1]()
            land = pair_bufs[3]
            g = land[0].astype(F32)
            for chip in range(1, N_CHIPS):
                g = g + land[chip].astype(F32)
            for ref, val in zip(last_out, (g,) + _adam(last_refs[0][...], g, last_refs[1][...], last_refs[2][...])):
                ref[...] = val
            for i in range(ns):
                g = small_land[i][0].astype(F32)
                for s in range(1, N_DEV):
                    g = g + small_land[i][s].astype(F32)
                results = (g,) + _adam(sw_refs[i][...], g, sm_refs[i][...], sv_refs[i][...])
                for ref, val in zip(small_out[4 * i : 4 * i + 4], results):
                    ref[...] = val

    def tile(t):
        return pl.BlockSpec((tiles[t], ws[t].shape[1]), lambda i: (i, 0))

    def land_tile(t):
        return pl.BlockSpec((N_DEV, tiles[t], ws[t].shape[1]), lambda i: (0, i, 0))

    def whole(shape):
        return pl.BlockSpec(shape, lambda step: (0, 0))

    big_sds = [jax.ShapeDtypeStruct(ws[t].shape, F32) for t in range(nw) for _ in range(4)]
    small_sds = [jax.ShapeDtypeStruct(smalls[i][0].shape, F32) for i in range(ns) for _ in range(4)]
    return pl.pallas_call(
        body,
        name="reduce_adam_tail",
        grid=(TAIL_STEPS,),
        out_shape=big_sds + [jax.ShapeDtypeStruct(blk, F32)] * 4 + small_sds,
        in_specs=[ANY_SPEC] * (1 + ns)
        + [land_tile(t) for t in range(nw)]
        + [tile(t) for _ in range(3) for t in range(nw)]
        + [whole(blk)] * 3
        + [whole(smalls[i][0].shape) for _ in range(3) for i in range(ns)],
        out_specs=[tile(t) for t in range(nw) for _ in range(4)]
        + [whole(blk)] * 4
        + [whole(smalls[i][0].shape) for i in range(ns) for _ in range(4)],
        scratch_shapes=[pltpu.VMEM((N_DEV,) + smalls[i][0].shape, smalls[i][0].dtype) for i in range(ns)]
        + [pltpu.VMEM((N_CHIPS,) + blk, BF16)] * 4
        + [pltpu.SemaphoreType.DMA((1,))]
        + [pltpu.SemaphoreType.DMA((count,)) for count in PAIR_SEMS]
        + _exchange_sems(ns),
        compiler_params=pltpu.CompilerParams(
            dimension_semantics=("arbitrary",), vmem_limit_bytes=VMEM_LIMIT, has_side_effects=True
        ),
    )(
        gp_last, *[s[0] for s in smalls], *lands, *ws, *ms, *vs, *last,
        *[s[1] for s in smalls], *[s[2] for s in smalls], *[s[3] for s in smalls],
    )


def _fwd_in(x, g1, w_in):
    S = x.shape[0]
    tm = min(512, S)

    def body(x_ref, g_ref, w_ref, hn_ref, u_ref, q_ref, k_ref, v_ref):
        xv = x_ref[...]
        hn = (xv * _rstd(xv) * g_ref[...]).astype(BF16)
        hn_ref[...] = hn
        outs = (u_ref, q_ref, k_ref, v_ref)
        for j in range(N_DEV):
            p = _dot(hn, w_ref[j])
            cols = slice(256 * (j % 2), 256 * (j % 2 + 1))
            if j // 2 == 0:
                u_ref[:, cols] = p
            elif j // 2 == 1:
                q_ref[:, cols] = (p * (HEAD_DIM**-0.5)).astype(BF16)
            else:
                outs[j // 2][:, cols] = p.astype(BF16)

    half = pl.BlockSpec((tm, D_POOL), lambda i: (i, 0))
    return pl.pallas_call(
        body,
        name="fwd_in",
        grid=(S // tm,),
        out_shape=[
            jax.ShapeDtypeStruct((S, D_MODEL), BF16),
            jax.ShapeDtypeStruct((S, D_POOL), F32),
            jax.ShapeDtypeStruct((S, D_ATTN), BF16),
            jax.ShapeDtypeStruct((S, D_ATTN), BF16),
            jax.ShapeDtypeStruct((S, D_ATTN), BF16),
        ],
        in_specs=[
            pl.BlockSpec((tm, D_MODEL), lambda i: (i, 0)),
            pl.BlockSpec((1, D_MODEL), lambda i: (0, 0)),
            pl.BlockSpec((N_DEV, D_MODEL, 256), lambda i: (0, 0, 0)),
        ],
        out_specs=[pl.BlockSpec((tm, D_MODEL), lambda i: (i, 0)), half, half, half, half],
        compiler_params=_params(("parallel",)),
    )(x, g1, w_in)


POOL_CHUNK = 256
_POOL_W_SPEC = pl.BlockSpec((1, GROUP_DIM, GROUP_DIM), lambda g: (g, 0, 0))
_POOL_SCALE_SPEC = pl.BlockSpec((1, GROUP_DIM), lambda g: (0, g))


def _lane_sum(wide, n):
    out = wide[:, :GROUP_DIM]
    for i in range(1, n):
        out = out + wide[:, GROUP_DIM * i : GROUP_DIM * (i + 1)]
    return out


def _pool_fwd_group(u_ref, pw_ref, sc_ref, pooled_ref, y_ref, g):
    S = u_ref.shape[0]
    R = min(POOL_CHUNK, S)
    w = jnp.left_shift(jnp.int32(2), g)
    d = lax.broadcasted_iota(jnp.int32, (R, R), 0) - lax.broadcasted_iota(jnp.int32, (R, R), 1)
    b_cur = jnp.where((d >= 0) & (d < w), 1.0, 0.0).astype(BF16)
    dp = lax.broadcasted_iota(jnp.int32, (R, HALO), 0) + HALO - lax.broadcasted_iota(jnp.int32, (R, HALO), 1)
    b_prev = jnp.where(dp < w, 1.0, 0.0).astype(BF16)
    pw = pw_ref[0].astype(BF16)
    sc = sc_ref[...]

    def chunk(r0, first):
        cur = u_ref[pl.ds(r0, R), :]
        wide = _dot(b_cur, jnp.concatenate(_split3(cur), axis=1))
        if not first:
            prev = u_ref[pl.ds(pl.multiple_of(r0 - HALO, HALO), HALO), :]
            wide = wide + _dot(b_prev, jnp.concatenate(_split3(prev), axis=1))
        count = jnp.minimum(r0 + lax.broadcasted_iota(jnp.int32, (R, 1), 0) + 1, w).astype(F32)
        pooled = (_lane_sum(wide, 3) / count - cur).astype(BF16)
        pooled_ref[pl.ds(r0, R), :] = pooled
        y_ref[pl.ds(r0, R), :] = _dot(pooled, pw) * sc

    chunk(0, True)

    def rest(c, carry):
        chunk(pl.multiple_of(c * R, R), False)
        return carry

    lax.fori_loop(1, S // R, rest, 0)


ATTN_WINDOW = 3


def _band_ones(T):
    row = lax.broadcasted_iota(jnp.int32, (T, T), 0)
    col = lax.broadcasted_iota(jnp.int32, (T, T), 1)
    after = jnp.where(row > col, 1.0, 0.0).astype(BF16)
    before = jnp.where(row < col, 1.0, 0.0).astype(BF16)
    return jnp.concatenate([after, after], axis=0), jnp.concatenate([before, before], axis=0), col < row


def _log1m(z, mask):
    sp = jnp.log(1.0 + jnp.exp(-jnp.abs(z)))
    l1m = -jnp.maximum(z, 0.0) - sp
    return (l1m if mask is None else jnp.where(mask, l1m, 0.0)), sp


def _log_gates(qhs, kbs, masks, after2):
    zs = [[_dot_nt(qh, kb) for qh in qhs] for kb in kbs]
    terms = []
    for row, mask in zip(zs, masks):
        terms.append([])
        for z in row:
            l1m, sp = _log1m(z, mask)
            terms[-1].append(
                (jnp.minimum(z, 0.0) - sp, jnp.concatenate(_split2(l1m), axis=1), jnp.sum(l1m, axis=1, keepdims=True))
            )
    return [[(ls, _dot(split, after2), total) for ls, split, total in row] for row in terms]


def _tile_start(index, T):
    return index * T if isinstance(index, int) else pl.multiple_of(index * T, T)


def _weights(ls, tail, carry, mask):
    a = jnp.exp(ls + tail + carry)
    return a if mask is None else jnp.where(mask, a, 0.0)


def _reaches(carry):
    return jnp.max(carry) > EXP_UNDERFLOW


def _mixers_fwd(q, k, v, u, pool_w, scale, shards):
    S = q.shape[0]
    T = ATTN_TILE
    nq = S // T
    n = len(shards)
    steps = D_ATTN // LANES

    def body(q_ref, k_ref, v_ref, u_ref, pw_ref, sc_ref, *rest):
        o_ref, pooled_ref, yp_ref = rest[n : n + 3]
        step = pl.program_id(0)

        def gather():
            return _two_level_gather(rest[:n], rest[n + 3 : 2 * n + 3], rest[2 * n + 3 :])

        @pl.when(step == 0)
        def _():
            gather()[0]()

        _pool_fwd_group(u_ref, pw_ref, sc_ref, pooled_ref, yp_ref, step)

        lane = lax.broadcasted_iota(jnp.int32, (T, LANES), 1)
        heads = (lane < HEAD_DIM, lane >= HEAD_DIM)
        after2, _, causal = _band_ones(T)
        zero = jnp.zeros((T, 1), F32)

        def key_tile(k0):
            vb = v_ref[pl.ds(k0, T), :]
            return k_ref[pl.ds(k0, T), :], jnp.concatenate([jnp.where(m, vb, jnp.zeros_like(vb)) for m in heads], axis=0)

        def q_tile(qi, nb):
            q0 = _tile_start(qi, T)
            qb = q_ref[pl.ds(q0, T), :]
            qhs = [jnp.where(m, qb, jnp.zeros_like(qb)) for m in heads]
            tiles = [key_tile(_tile_start(qi - b, T)) for b in range(nb)]
            values = [v_rows for _, v_rows in tiles]
            gates = _log_gates(qhs, [kb for kb, _ in tiles], [causal] + [None] * (nb - 1), after2)
            carries, probs = [zero, zero], []
            for b in range(nb):
                for h in range(2):
                    ls, tail, total = gates[b][h]
                    probs.append(_weights(ls, tail, carries[h], causal if b == 0 else None).astype(BF16))
                    carries[h] = carries[h] + total
            acc = _dot(jnp.concatenate(probs, axis=1), jnp.concatenate(values, axis=0))

            def more(st):
                return (st[0] > 0) & _reaches(jnp.maximum(st[1], st[2]))

            def k_step(st):
                kj = st[0] - 1
                kb, v_rows = key_tile(pl.multiple_of(kj * T, T))
                new, probs = [], []
                for (ls, tail, total), carry in zip(_log_gates(qhs, [kb], [None], after2)[0], st[1:3]):
                    probs.append(_weights(ls, tail, carry, None).astype(BF16))
                    new.append(carry + total)
                return kj, new[0], new[1], st[3] + _dot(jnp.concatenate(probs, axis=1), v_rows)

            if not isinstance(qi, int):
                acc = lax.while_loop(more, k_step, (qi - (nb - 1), carries[0], carries[1], acc))[3]
            o_ref[pl.ds(q0, T), :] = acc

        for qi in range(min(ATTN_WINDOW - 1, nq)):
            q_tile(qi, qi + 1)

        def q_loop(qi, carry):
            q_tile(qi, ATTN_WINDOW)
            return carry

        lax.fori_loop(ATTN_WINDOW - 1, nq, q_loop, 0)

        @pl.when(step == steps - 1)
        def _():
            _, forward, finish = gather()
            forward()
            finish()

    blk = pl.BlockSpec((S, LANES), lambda p: (0, p))
    return pl.pallas_call(
        body,
        name="mixers_fwd",
        grid=(steps,),
        out_shape=[
            jax.ShapeDtypeStruct((S, D_ATTN), F32),
            jax.ShapeDtypeStruct((S, D_POOL), BF16),
            jax.ShapeDtypeStruct((S, D_POOL), F32),
        ]
        + _exchange_shapes(shards, True),
        in_specs=[blk, blk, blk, blk, _POOL_W_SPEC, _POOL_SCALE_SPEC] + [ANY_SPEC] * n,
        out_specs=[blk, blk, blk] + [ANY_SPEC] * n,
        scratch_shapes=_exchange_sems(n),
        compiler_params=pltpu.CompilerParams(
            dimension_semantics=("arbitrary",), vmem_limit_bytes=VMEM_LIMIT, has_side_effects=True
        ),
    )(q, k, v, u, pool_w, scale, *shards)


SG_ROWS = 8


def _mlp_fwd_bwd(x, y_pool, y_attn, target, g_pool, g_attn, g2, gf, w_out, w_up, w_down):
    S = x.shape[0]
    tm = min(256, S)
    fc = D_FF // N_DEV

    def body(x_ref, yp_ref, ya_ref, t_ref, gp_ref, ga_ref, g2_ref, gf_ref, wo_hbm, wu_hbm, wd_hbm,
             mixed_ref, hn2_ref, act_ref, dup_ref, dh2_ref, dh1_ref, dh1b_ref, dyp_ref, dya_ref, sg_ref,
             wo, wu, wd, up_s, sems):
        @pl.when(pl.program_id(0) == 0)
        def _():
            copies = [
                pltpu.make_async_copy(wo_hbm, wo, sems.at[0]),
                pltpu.make_async_copy(wu_hbm, wu, sems.at[1]),
                pltpu.make_async_copy(wd_hbm, wd, sems.at[2]),
            ]
            for cp in copies:
                cp.start()
            for cp in copies:
                cp.wait()
            sg_ref[...] = jnp.zeros_like(sg_ref)

        gp, ga, g2v, gfv = gp_ref[...], ga_ref[...], g2_ref[...], gf_ref[...]
        yp, ya = yp_ref[...], ya_ref[...]
        rp, ra = _rstd(yp), _rstd(ya)
        yph, yah = yp * rp, ya * ra
        mixed = jnp.concatenate([(yph * gp).astype(BF16), (yah * ga).astype(BF16)], axis=1)
        mixed_ref[...] = mixed
        h1 = x_ref[...] + _dot(mixed, wo[...])
        r2 = _rstd(h1)
        h1h = h1 * r2
        hn2 = (h1h * g2v).astype(BF16)
        hn2_ref[...] = hn2
        h2 = h1
        for j in range(N_DEV):
            cols = slice(fc * j, fc * (j + 1))
            up = _dot(hn2, wu[j])
            up_s[:, cols] = up
            act = jnp.square(jnp.maximum(up, 0.0)).astype(BF16)
            act_ref[:, cols] = act
            h2 = h2 + _dot(act, wd[cols, :])
        rf = _rstd(h2)
        h2h = h2 * rf
        diff = h2h * gfv - t_ref[...]
        loss_rows = 0.5 * jnp.mean(diff * diff, axis=-1, keepdims=True)
        dy = diff * (1.0 / D_MODEL)
        dh2, dgf = _rms_bwd(dy, h2h, rf, gfv)
        dh2b = dh2.astype(BF16)
        dh2_ref[...] = dh2b
        dhn2 = jnp.zeros((tm, D_MODEL), F32)
        for j in range(N_DEV):
            cols = slice(fc * j, fc * (j + 1))
            dact = _dot_nt(dh2b, wd[cols, :])
            dup = (dact * (2.0 * jnp.maximum(up_s[:, cols], 0.0))).astype(BF16)
            dup_ref[:, cols] = dup
            dhn2 = dhn2 + _dot_nt(dup, wu[j])
        dh1n, dg2 = _rms_bwd(dhn2, h1h, r2, g2v)
        dh1 = dh2 + dh1n
        dh1_ref[...] = dh1
        dh1b = dh1.astype(BF16)
        dh1b_ref[...] = dh1b
        dmix = _dot_nt(dh1b, wo[...])
        dyp, dgp = _rms_bwd(dmix[:, :D_POOL], yph, rp, gp)
        dya, dga = _rms_bwd(dmix[:, D_POOL:], yah, ra, ga)
        dyp_ref[...] = dyp
        dya_ref[...] = dya
        sg_ref[0:1, :] += dgf
        sg_ref[1:2, :] += dg2
        sg_ref[2:3, :] += jnp.concatenate([dgp, dga], axis=1)
        sg_ref[3:4, :] += jnp.broadcast_to(jnp.sum(loss_rows, axis=0, keepdims=True), (1, D_MODEL))

    def tok(n):
        return pl.BlockSpec((tm, n), lambda i: (i, 0))

    def vec(n):
        return pl.BlockSpec((1, n), lambda i: (0, 0))

    any_spec = pl.BlockSpec(memory_space=pl.ANY)
    return pl.pallas_call(
        body,
        name="mlp_fwd_bwd",
        grid=(S // tm,),
        out_shape=[
            jax.ShapeDtypeStruct((S, D_MODEL), BF16),
            jax.ShapeDtypeStruct((S, D_MODEL), BF16),
            jax.ShapeDtypeStruct((S, D_FF), BF16),
            jax.ShapeDtypeStruct((S, D_FF), BF16),
            jax.ShapeDtypeStruct((S, D_MODEL), BF16),
            jax.ShapeDtypeStruct((S, D_MODEL), F32),
            jax.ShapeDtypeStruct((S, D_MODEL), BF16),
            jax.ShapeDtypeStruct((S, D_POOL), F32),
            jax.ShapeDtypeStruct((S, D_ATTN), F32),
            jax.ShapeDtypeStruct((SG_ROWS, D_MODEL), F32),
        ],
        in_specs=[tok(D_MODEL), tok(D_POOL), tok(D_ATTN), tok(D_MODEL), vec(D_POOL), vec(D_ATTN),
                  vec(D_MODEL), vec(D_MODEL), any_spec, any_spec, any_spec],
        out_specs=[tok(D_MODEL), tok(D_MODEL), tok(D_FF), tok(D_FF), tok(D_MODEL), tok(D_MODEL),
                   tok(D_MODEL), tok(D_POOL), tok(D_ATTN),
                   pl.BlockSpec((SG_ROWS, D_MODEL), lambda i: (0, 0))],
        scratch_shapes=[
            pltpu.VMEM((D_MODEL, D_MODEL), BF16),
            pltpu.VMEM((N_DEV, D_MODEL, fc), BF16),
            pltpu.VMEM((D_FF, D_MODEL), BF16),
            pltpu.VMEM((tm, D_FF), F32),
            pltpu.SemaphoreType.DMA((3,)),
        ],
        compiler_params=_params(("arbitrary",)),
    )(x, y_pool, y_attn, target, g_pool, g_attn, g2, gf, w_out, w_up, w_down)


def _wgrad(a, b, block_a, groups, name, travelling=()):
    n = len(travelling)
    S, ka = a.shape
    nb = b.shape[1]
    ts = min(1024, S)
    per = N_DEV // groups
    if block_a:
        ka //= groups
        blk = (ka // per, nb)
        a_spec = pl.BlockSpec((ts, ka), lambda g, s: (s, g))
        b_spec = pl.BlockSpec((ts, nb), lambda g, s: (s, 0))
    else:
        nb //= groups
        blk = (ka, nb // per)
        a_spec = pl.BlockSpec((ts, ka), lambda g, s: (s, 0))
        b_spec = pl.BlockSpec((ts, nb), lambda g, s: (s, g))
    steps = S // ts

    def body(a_ref, b_ref, *rest):
        o_ref, acc = rest[n], rest[2 * n + 1]
        g, s = pl.program_id(0), pl.program_id(1)

        def scatter():
            return _direct_exchange(rest[:n], rest[n + 1 : 2 * n + 1], rest[2 * n + 2 :], False)

        if n:
            @pl.when((g == 0) & (s == 0))
            def _():
                scatter()[0]()

        @pl.when(s == 0)
        def _():
            acc[...] = jnp.zeros_like(acc)

        acc[...] += _dot_tn(a_ref[...], b_ref[...])

        @pl.when(s == steps - 1)
        def _():
            for j in range(per):
                if block_a:
                    o_ref[j] = acc[blk[0] * j : blk[0] * (j + 1), :].astype(BF16)
                else:
                    o_ref[j] = acc[:, blk[1] * j : blk[1] * (j + 1)].astype(BF16)

        if n:
            @pl.when((g == groups - 1) & (s == steps - 1))
            def _():
                scatter()[1]()

    results = pl.pallas_call(
        body,
        name=name,
        grid=(groups, steps),
        out_shape=[jax.ShapeDtypeStruct((N_DEV,) + blk, BF16)] + _exchange_shapes(travelling, False),
        in_specs=[a_spec, b_spec] + [ANY_SPEC] * n,
        out_specs=[pl.BlockSpec((per,) + blk, lambda g, s: (g, 0, 0))] + [ANY_SPEC] * n,
        scratch_shapes=[pltpu.VMEM((ka, nb), F32)] + (_exchange_sems(n) if n else []),
        compiler_params=pltpu.CompilerParams(
            dimension_semantics=("arbitrary", "arbitrary"), vmem_limit_bytes=VMEM_LIMIT, has_side_effects=bool(n)
        ),
    )(a, b, *travelling)
    return results if n else results[0]


def _mixers_bwd(q, k, v, do, dyp, pooled, pool_w, scale, partials):
    S = q.shape[0]
    T = ATTN_TILE
    nq = S // T
    n = len(partials)
    steps = D_ATTN // LANES

    def body(q_ref, k_ref, v_ref, do_ref, dyp_ref, pooled_ref, pw_ref, sc_ref, *rest):
        dq_ref, dk_ref, dv_ref, du_ref, dsc_ref, dpw_ref = rest[n : n + 6]
        dk_acc, dv_acc, carry_s = rest[2 * n + 6 : 2 * n + 9]
        step = pl.program_id(0)

        def scatter():
            return _direct_exchange(rest[:n], rest[n + 6 : 2 * n + 6], rest[2 * n + 9 :], False)

        @pl.when(step == 0)
        def _():
            scatter()[0]()

        _pool_bwd_group(dyp_ref, pooled_ref, pw_ref, sc_ref, du_ref, dsc_ref, dpw_ref, step)

        dk_acc[...] = jnp.zeros_like(dk_acc)
        dv_acc[...] = jnp.zeros_like(dv_acc)
        lane = lax.broadcasted_iota(jnp.int32, (T, LANES), 1)
        heads = (lane < HEAD_DIM, lane >= HEAD_DIM)
        after2, before2, causal = _band_ones(T)
        zero = jnp.zeros((T, 1), F32)

        def key_tile(k0):
            kb = k_ref[pl.ds(k0, T), :]
            k_rows = jnp.concatenate([jnp.where(m, kb, jnp.zeros_like(kb)) for m in heads], axis=0)
            return kb, v_ref[pl.ds(k0, T), :], k_rows

        def q_tile(qi, nb):
            q0 = _tile_start(qi, T)
            qb = q_ref[pl.ds(q0, T), :]
            dob = do_ref[pl.ds(q0, T), :]
            qhs = [jnp.where(m, qb, jnp.zeros_like(qb)) for m in heads]
            dohs = [jnp.where(m, dob, 0.0).astype(BF16) for m in heads]
            q_rows = jnp.concatenate(qhs, axis=0)
            do_rows = jnp.concatenate(dohs, axis=0)

            def grad_tiles(starts, tiles, masks, carry_in, gates, st):
                nt = len(tiles)
                das = [[_dot_nt(doh, vb) for doh in dohs] for _, vb, _ in tiles]
                probs, gs = [], []
                for b in range(nt):
                    probs.append([_weights(gates[b][h][0], gates[b][h][1], carry_in[b][h], masks[b]) for h in range(2)])
                    gs.append([probs[b][h] * das[b][h] for h in range(2)])
                before = [[_dot(jnp.concatenate(_split2(g), axis=1), before2) for g in row] for row in gs]
                g_left, dzs = [st[0], st[1]], [None] * nt
                for b in reversed(range(nt)):
                    dzs[b] = []
                    for h in range(2):
                        sig = jnp.exp(gates[b][h][0])
                        dz = gs[b][h] * (1.0 - sig) - sig * (g_left[h] + before[b][h])
                        if masks[b] is not None:
                            dz = jnp.where(masks[b], dz, 0.0)
                        dzs[b].append(dz.astype(BF16))
                        g_left[h] = g_left[h] + jnp.sum(gs[b][h], axis=1, keepdims=True)
                dq = st[2] + _dot(
                    jnp.concatenate([dz for row in dzs for dz in row], axis=1),
                    jnp.concatenate([k_rows for _, _, k_rows in tiles], axis=0),
                )
                for b in range(nt):
                    dk_acc[pl.ds(starts[b], T), :] += _dot_tn(jnp.concatenate(dzs[b], axis=0), q_rows)
                    dv_acc[pl.ds(starts[b], T), :] += _dot_tn(
                        jnp.concatenate([a.astype(BF16) for a in probs[b]], axis=0), do_rows
                    )
                return g_left[0], g_left[1], dq

            starts = [_tile_start(qi - b, T) for b in range(nb)]
            tiles = [key_tile(k0) for k0 in starts]
            masks = [causal] + [None] * (nb - 1)
            gates = _log_gates(qhs, [kb for kb, _, _ in tiles], masks, after2)
            carries, carry_in = [zero, zero], []
            for b in range(nb):
                carry_in.append(list(carries))
                carries = [carries[h] + gates[b][h][2] for h in range(2)]

            st = (zero, zero, jnp.zeros((T, LANES), F32))
            if not isinstance(qi, int):
                k_left = qi - (nb - 1)

                def more(st):
                    return (st[0] > 0) & _reaches(jnp.maximum(st[1], st[2]))

                def right_to_left(st):
                    kj = st[0] - 1
                    carry_s[0, kj] = st[1]
                    carry_s[1, kj] = st[2]
                    kb = k_ref[pl.ds(pl.multiple_of(kj * T, T), T), :]
                    sums = [jnp.sum(_log1m(_dot_nt(qh, kb), None)[0], axis=1, keepdims=True) for qh in qhs]
                    return kj, st[1] + sums[0], st[2] + sums[1]

                k_first = lax.while_loop(more, right_to_left, (k_left, carries[0], carries[1]))[0]

                def left_to_right(kj, st):
                    k0 = pl.multiple_of(kj * T, T)
                    tile = key_tile(k0)
                    left_gates = _log_gates(qhs, [tile[0]], [None], after2)
                    return grad_tiles([k0], [tile], [None], [[carry_s[0, kj], carry_s[1, kj]]], left_gates, st)

                st = lax.fori_loop(k_first, k_left, left_to_right, st)
            st = grad_tiles(starts, tiles, masks, carry_in, gates, st)
            dq_ref[pl.ds(q0, T), :] = (st[2] * (HEAD_DIM**-0.5)).astype(BF16)

        for qi in range(min(ATTN_WINDOW - 1, nq)):
            q_tile(qi, qi + 1)

        def q_loop(qi, carry):
            q_tile(qi, ATTN_WINDOW)
            return carry

        lax.fori_loop(ATTN_WINDOW - 1, nq, q_loop, 0)
        dk_ref[...] = dk_acc[...].astype(BF16)
        dv_ref[...] = dv_acc[...].astype(BF16)

        @pl.when(step == steps - 1)
        def _():
            scatter()[1]()

    blk = pl.BlockSpec((S, LANES), lambda p: (0, p))
    sds = jax.ShapeDtypeStruct((S, D_ATTN), BF16)
    return pl.pallas_call(
        body,
        name="mixers_bwd",
        grid=(steps,),
        out_shape=[
            sds,
            sds,
            sds,
            jax.ShapeDtypeStruct((S, D_POOL), BF16),
            jax.ShapeDtypeStruct((8, D_POOL), F32),
            jax.ShapeDtypeStruct((N_GROUPS, GROUP_DIM, GROUP_DIM), BF16),
        ]
        + _exchange_shapes(partials, False),
        in_specs=[blk] * 6 + [_POOL_W_SPEC, _POOL_SCALE_SPEC] + [ANY_SPEC] * n,
        out_specs=[blk] * 4 + [pl.BlockSpec((8, GROUP_DIM), lambda p: (0, p)), _POOL_W_SPEC] + [ANY_SPEC] * n,
        scratch_shapes=[pltpu.VMEM((S, LANES), F32), pltpu.VMEM((S, LANES), F32), pltpu.VMEM((2, nq, T, 1), F32)]
        + _exchange_sems(n),
        compiler_params=pltpu.CompilerParams(
            dimension_semantics=("arbitrary",), vmem_limit_bytes=VMEM_LIMIT, has_side_effects=True
        ),
    )(q, k, v, do, dyp, pooled, pool_w, scale, *partials)


def _pool_bwd_group(dy_ref, pooled_ref, pw_ref, sc_ref, du_ref, dsc_ref, dpw_ref, g):
    S = dy_ref.shape[0]
    R = min(POOL_CHUNK, S)
    nc = S // R
    w = jnp.left_shift(jnp.int32(2), g)
    pw = pw_ref[0].astype(BF16)
    sc = sc_ref[...]
    d = lax.broadcasted_iota(jnp.int32, (R, R), 1) - lax.broadcasted_iota(jnp.int32, (R, R), 0)
    bt_cur = jnp.where((d >= 0) & (d < w), 1.0, 0.0).astype(BF16)
    dn = lax.broadcasted_iota(jnp.int32, (R, HALO), 1) + R - lax.broadcasted_iota(jnp.int32, (R, HALO), 0)
    bt_next = jnp.where(dn < w, 1.0, 0.0).astype(BF16)

    def per_count(dpl, r0):
        n = dpl.shape[0]
        count = jnp.minimum(r0 + lax.broadcasted_iota(jnp.int32, (n, 1), 0) + 1, w).astype(F32)
        return jnp.concatenate(_split2(dpl / count), axis=1)

    def chunk(r0, last, sums):
        dyv = dy_ref[pl.ds(r0, R), :]
        pooled = pooled_ref[pl.ds(r0, R), :]
        dmapped = (dyv * sc).astype(BF16)
        dsc = sums[0] + jnp.sum(dyv * _dot(pooled, pw), axis=0, keepdims=True)
        dpw = sums[1] + _dot_tn(pooled, dmapped)
        dpl = _dot_nt(dmapped, pw)
        wide = _dot(bt_cur, per_count(dpl, r0))
        if not last:
            nxt = pl.multiple_of(r0 + R, R)
            dpl_next = _dot_nt((dy_ref[pl.ds(nxt, HALO), :] * sc).astype(BF16), pw)
            wide = wide + _dot(bt_next, per_count(dpl_next, nxt))
        du_ref[pl.ds(r0, R), :] = (_lane_sum(wide, 2) - dpl).astype(BF16)
        return dsc, dpw

    sums = (jnp.zeros((1, GROUP_DIM), F32), jnp.zeros((GROUP_DIM, GROUP_DIM), F32))
    sums = lax.fori_loop(0, nc - 1, lambda c, s: chunk(pl.multiple_of(c * R, R), False, s), sums)
    dsc, dpw = chunk((nc - 1) * R, True, sums)
    dsc_ref[...] = jnp.zeros_like(dsc_ref)
    dsc_ref[0:1, :] = dsc
    dpw_ref[0] = dpw.astype(dpw_ref.dtype)


def _bwd_in(du, dq, dk, dv, w_in, x, dh1, g1):
    S = x.shape[0]
    tm = min(512, S)

    def body(du_ref, dq_ref, dk_ref, dv_ref, w_ref, x_ref, dh1_ref, g_ref, dx_ref, dproj_ref, dg_ref):
        @pl.when(pl.program_id(0) == 0)
        def _():
            dg_ref[...] = jnp.zeros_like(dg_ref)

        parts = (du_ref[...], dq_ref[...], dk_ref[...], dv_ref[...])
        dhn = jnp.zeros((tm, D_MODEL), F32)
        for j in range(N_DEV):
            piece = parts[j // 2][:, 256 * (j % 2) : 256 * (j % 2 + 1)]
            dproj_ref[:, 256 * j : 256 * (j + 1)] = piece
            dhn = dhn + _dot_nt(piece, w_ref[j])
        xv = x_ref[...]
        r = _rstd(xv)
        dxn, dg = _rms_bwd(dhn, xv * r, r, g_ref[...])
        dx_ref[...] = dh1_ref[...] + dxn
        dg_ref[0:1, :] += dg

    half = pl.BlockSpec((tm, D_POOL), lambda i: (i, 0))
    full = pl.BlockSpec((tm, D_MODEL), lambda i: (i, 0))
    return pl.pallas_call(
        body,
        name="bwd_in",
        grid=(S // tm,),
        out_shape=[
            jax.ShapeDtypeStruct((S, D_MODEL), F32),
            jax.ShapeDtypeStruct((S, D_IN_PROJ), BF16),
            jax.ShapeDtypeStruct((8, D_MODEL), F32),
        ],
        in_specs=[half, half, half, half,
                  pl.BlockSpec((N_DEV, D_MODEL, 256), lambda i: (0, 0, 0)),
                  full, full, pl.BlockSpec((1, D_MODEL), lambda i: (0, 0))],
        out_specs=[full, pl.BlockSpec((tm, D_IN_PROJ), lambda i: (i, 0)),
                   pl.BlockSpec((8, D_MODEL), lambda i: (0, 0))],
        compiler_params=_params(("arbitrary",)),
    )(du, dq, dk, dv, w_in, x, dh1, g1)


def _rows(a):
    a = a.reshape(-1, LANES)
    pad = (-a.shape[0]) % 8
    return jnp.pad(a, ((0, pad), (0, 0))) if pad else a


def kernel(x, norm1_g, w_in, pool_w, pool_scale, pool_out_g, attn_out_g, w_out, norm2_g, w_up, w_down, final_g, loss_target, m_norm1_g, m_w_in, m_pool_w, m_pool_scale, m_pool_out_g, m_attn_out_g, m_w_out, m_norm2_g, m_w_up, m_w_down, m_final_g, v_norm1_g, v_w_in, v_pool_w, v_pool_scale, v_pool_out_g, v_attn_out_g, v_w_out, v_norm2_g, v_w_up, v_w_down, v_final_g):
    S = x.shape[1]
    xs = x.reshape(S, D_MODEL)
    tgt = loss_target.reshape(S, D_MODEL)
    row = lambda a: a.reshape(1, -1)

    (w_in_g,) = _exchange([w_in.astype(BF16)], True, "gather_w_in")
    hn, u_pool, q, k, v = _fwd_in(xs, row(norm1_g), w_in_g)
    y_attn, pooled, y_pool, w_out_g, w_up_g, w_down_g = _mixers_fwd(
        q, k, v, u_pool, pool_w, row(pool_scale), [w_out.astype(BF16), w_up.astype(BF16), w_down.astype(BF16)]
    )
    w_out_full = w_out_g.reshape(D_MODEL, D_MODEL)
    w_down_full = w_down_g.reshape(D_FF, D_MODEL)
    mixed, hn2, act, dup, dh2b, dh1, dh1b, dyp, dya, sg = _mlp_fwd_bwd(
        xs, y_pool, y_attn, tgt, row(pool_out_g), row(attn_out_g), row(norm2_g), row(final_g),
        w_out_full, w_up_g, w_down_full,
    )
    gp_down = _wgrad(act, dh2b, True, 2, "wgrad_down")
    gp_up = _wgrad(hn2, dup, False, 2, "wgrad_up")
    gp_out = _wgrad(mixed, dh1b, True, 1, "wgrad_out")
    dq, dk, dv, du, dsc, dpw, land_up, land_down = _mixers_bwd(
        q, k, v, dya, dyp, pooled, pool_w, row(pool_scale), [gp_up, gp_down]
    )
    dx, dproj, dg1 = _bwd_in(du, dq, dk, dv, w_in_g, xs, dh1, row(norm1_g))
    gp_in, land_out = _wgrad(hn, dproj, False, 1, "wgrad_in", [gp_out])

    def vectors(final, norm2, pool_out, attn_out, norm1, scale, last):
        pieces = [final, norm2, jnp.concatenate([pool_out, attn_out]), norm1, scale, last]
        return jnp.concatenate([_rows(p) for p in pieces], axis=0)

    no_loss = jnp.zeros((8, LANES), F32)
    smalls = [
        (
            vectors(sg[0], sg[1], sg[2][:D_POOL], sg[2][D_POOL:], dg1[0], dsc[0], sg[3]),
            vectors(final_g, norm2_g, pool_out_g, attn_out_g, norm1_g, pool_scale, no_loss),
            vectors(m_final_g, m_norm2_g, m_pool_out_g, m_attn_out_g, m_norm1_g, m_pool_scale, no_loss),
            vectors(v_final_g, v_norm2_g, v_pool_out_g, v_attn_out_g, v_norm1_g, v_pool_scale, no_loss),
        ),
        tuple(a.reshape(N_GROUPS * GROUP_DIM, GROUP_DIM) for a in (dpw, pool_w, m_pool_w, v_pool_w)),
    ]
    tail = _reduce_adam_tail(
        [land_out, land_up, land_down], [w_out, w_up, w_down], [m_w_out, m_w_up, m_w_down],
        [v_w_out, v_w_up, v_w_down], gp_in, (w_in, m_w_in, v_w_in), smalls,
    )
    big = {name: tail[4 * t : 4 * t + 4] for t, name in enumerate(("w_out", "w_up", "w_down", "w_in"))}

    def unpack(vec, pw):
        out = {}
        for i, name in enumerate(("final_g", "norm2_g", "mix_g", "norm1_g", "pool_scale")):
            out[name] = vec[8 * i : 8 * i + 8].reshape(-1)
        out["pool_scale"] = out["pool_scale"][:D_POOL]
        out["pool_out_g"], out["attn_out_g"] = out["mix_g"][:D_POOL], out["mix_g"][D_POOL:]
        out["pool_w"] = pw.reshape(N_GROUPS, GROUP_DIM, GROUP_DIM)
        return out, vec[40, 0]

    small_out = [unpack(tail[16 + i], tail[20 + i]) for i in range(4)]
    loss = small_out[0][1]
    order = ("norm1_g", "w_in", "pool_w", "pool_scale", "pool_out_g", "attn_out_g", "w_out", "norm2_g", "w_up",
             "w_down", "final_g")
    outs = [loss, dx.reshape(1, S, D_MODEL)]
    for i in range(4):
        for name in order:
            outs.append(big[name][i] if name in big else small_out[i][0][name])
    return tuple(outs)
```

```python
import functools

import jax
import jax.numpy as jnp
from jax import lax
from jax.experimental import pallas as pl
from jax.experimental.pallas import tpu as pltpu

F32 = jnp.float32
BF16 = jnp.bfloat16
MESH = pl.DeviceIdType.MESH

N_DEV = 8
D_MODEL = 1024
D_POOL = 512
D_ATTN = 512
N_GROUPS = 4
GROUP_DIM = 128
HEAD_DIM = 64
D_FF = 4096
D_IN_PROJ = 2048
EPS = 1e-6
HALO = 16
ATTN_TILE = 128
LANES = 128
EXP_UNDERFLOW = -104.0

ADAM_LR = 0.001
ADAM_B1 = 0.9
ADAM_B2 = 0.999
ADAM_EPS = 1e-08
ADAM_WD = 0.01
ADAM_STEP = 10

VMEM_LIMIT = 56 * 1024 * 1024


def _params(semantics=None, vmem=VMEM_LIMIT):
    return pltpu.CompilerParams(dimension_semantics=semantics, vmem_limit_bytes=vmem)


def _dot(a, b):
    return jnp.dot(a, b, preferred_element_type=F32)


def _dot_nt(a, b):
    return lax.dot_general(a, b, (((1,), (1,)), ((), ())), preferred_element_type=F32)


def _dot_tn(a, b):
    return lax.dot_general(a, b, (((0,), (0,)), ((), ())), preferred_element_type=F32)


def _split2(x):
    hi = x.astype(BF16)
    lo = (x - hi.astype(F32)).astype(BF16)
    return hi, lo


def _split3(x):
    hi = x.astype(BF16)
    r = x - hi.astype(F32)
    mid = r.astype(BF16)
    lo = (r - mid.astype(F32)).astype(BF16)
    return hi, mid, lo


def _rstd(h):
    return lax.rsqrt(jnp.mean(h * h, axis=-1, keepdims=True) + EPS)


def _rms_bwd(dout, hhat, r, g):
    dg = jnp.sum(dout * hhat, axis=0, keepdims=True)
    dxh = dout * g
    dh = r * (dxh - hhat * jnp.mean(dxh * hhat, axis=-1, keepdims=True))
    return dh, dg


def _my_index():
    return 4 * lax.axis_index("x") + 2 * lax.axis_index("y") + lax.axis_index("c")


def _peer(k):
    x, y, c = lax.axis_index("x"), lax.axis_index("y"), lax.axis_index("c")
    px = 1 - x if (k >> 2) & 1 else x
    py = 1 - y if (k >> 1) & 1 else y
    pc = 1 - c if k & 1 else c
    return (px, py, pc), 4 * px + 2 * py + pc


N_PEERS = N_DEV - 1
ANY_SPEC = pl.BlockSpec(memory_space=pl.ANY)


def _exchange_sems(n):
    return [
        pltpu.SemaphoreType.DMA((n * N_PEERS,)),
        pltpu.SemaphoreType.DMA((n * N_PEERS,)),
        pltpu.SemaphoreType.DMA((n,)),
    ]


def _exchange_shapes(blocks, gather):
    if gather:
        return [jax.ShapeDtypeStruct((N_DEV,) + b.shape, b.dtype) for b in blocks]
    return [jax.ShapeDtypeStruct(b.shape, b.dtype) for b in blocks]


def _direct_exchange(ins, outs, sems, gather):
    send_sems, recv_sems, local_sems = sems
    me = _my_index()
    own, sends, recvs = [], [], []
    for t in range(len(ins)):
        own.append(pltpu.make_async_copy(ins[t] if gather else ins[t].at[me], outs[t].at[me], local_sems.at[t]))
        for k in range(1, N_DEV):
            peer, peer_idx = _peer(k)
            src = ins[t] if gather else ins[t].at[peer_idx]
            for dst, bucket in ((outs[t].at[me], sends), (outs[t].at[peer_idx], recvs)):
                bucket.append(
                    pltpu.make_async_remote_copy(
                        src_ref=src,
                        dst_ref=dst,
                        send_sem=send_sems.at[t * N_PEERS + k - 1],
                        recv_sem=recv_sems.at[t * N_PEERS + k - 1],
                        device_id=peer,
                        device_id_type=MESH,
                    )
                )

    def start():
        for cp in own + sends:
            cp.start()

    def finish():
        for cp in recvs:
            cp.wait_recv()
        for cp in sends:
            cp.wait_send()
        for cp in own:
            cp.wait()

    return start, finish


def _two_level_gather(ins, outs, sems):
    send_sems, recv_sems, local_sems = sems
    x, y, c = lax.axis_index("x"), lax.axis_index("y"), lax.axis_index("c")
    me, sibling = (x, y, c), (x, y, 1 - c)
    chips = [(1 - x, y), (x, 1 - y), (1 - x, 1 - y)]

    def copy(t, k, block, to, from_input=False):
        slot = outs[t].at[4 * block[0] + 2 * block[1] + block[2]]
        return pltpu.make_async_remote_copy(
            src_ref=ins[t] if from_input else slot,
            dst_ref=slot,
            send_sem=send_sems.at[t * N_PEERS + k],
            recv_sem=recv_sems.at[t * N_PEERS + k],
            device_id=to,
            device_id_type=MESH,
        )

    arrays = range(len(ins))
    own = [pltpu.make_async_copy(ins[t], outs[t].at[4 * x + 2 * y + c], local_sems.at[t]) for t in arrays]
    first = [copy(t, 1 + j, me, (*chip, c), True) for t in arrays for j, chip in enumerate(chips)]
    first += [copy(t, 0, me, sibling, True) for t in arrays]
    passed = [(copy(t, 1 + j, (*chip, c), me), copy(t, 4 + j, (*chip, c), sibling))
              for t in arrays for j, chip in enumerate(chips)]
    last = [copy(t, 0, sibling, me) for t in arrays]
    last += [copy(t, 4 + j, (*chip, 1 - c), me) for t in arrays for j, chip in enumerate(chips)]

    def start():
        for cp in own + first:
            cp.start()

    def forward():
        for landed, onward in passed:
            landed.wait_recv()
            onward.start()

    def finish():
        for cp in last:
            cp.wait_recv()
        for cp in first + [onward for _, onward in passed]:
            cp.wait_send()
        for cp in own:
            cp.wait()

    return start, forward, finish


N_CHIPS = 4
PAIR_SEMS = (N_CHIPS, N_CHIPS, N_CHIPS - 1, N_CHIPS - 1)


def _pair_reduce_scatter(gp_hbm, own, pair, summed, land, local_sem, sems):
    d2d_send, d2d_recv, ici_send, ici_recv = sems
    x, y, c = lax.axis_index("x"), lax.axis_index("y"), lax.axis_index("c")
    my_chip = 2 * x + y
    sibling = (x, y, 1 - c)
    chips = [(1 - x, y), (x, 1 - y), (1 - x, 1 - y)]
    local = [pltpu.make_async_copy(gp_hbm.at[2 * j + c], own.at[j], local_sem.at[0]) for j in range(N_CHIPS)]
    to_sibling = [
        pltpu.make_async_remote_copy(
            src_ref=gp_hbm.at[2 * j + 1 - c], dst_ref=pair.at[j], send_sem=d2d_send.at[j], recv_sem=d2d_recv.at[j],
            device_id=sibling, device_id_type=MESH,
        )
        for j in range(N_CHIPS)
    ]
    to_chips, from_chips = [], []
    for r, (px, py) in enumerate(chips):
        for dst, bucket in ((land.at[my_chip], to_chips), (land.at[2 * px + py], from_chips)):
            bucket.append(
                pltpu.make_async_remote_copy(
                    src_ref=summed.at[2 * px + py], dst_ref=dst, send_sem=ici_send.at[r], recv_sem=ici_recv.at[r],
                    device_id=(px, py, c), device_id_type=MESH,
                )
            )

    def start():
        for cp in local + to_sibling:
            cp.start()

    def middle():
        for cp in to_sibling:
            cp.wait_recv()
        pltpu.make_async_copy(gp_hbm.at[pl.ds(0, N_CHIPS)], own, local_sem.at[0]).wait()
        for j in range(N_CHIPS):
            summed[j] = (own[j].astype(F32) + pair[j].astype(F32)).astype(BF16)
        for cp in to_chips:
            cp.start()

    def finish():
        for cp in from_chips:
            cp.wait_recv()
        for cp in to_chips + to_sibling:
            cp.wait_send()
        land[my_chip] = summed[my_chip]

    return start, middle, finish


def _exchange(blocks, gather, name):
    n = len(blocks)

    def body(*refs):
        if gather:
            stages = _two_level_gather(refs[:n], refs[n : 2 * n], refs[2 * n :])
        else:
            stages = _direct_exchange(refs[:n], refs[n : 2 * n], refs[2 * n :], False)
        for stage in stages:
            stage()

    return pl.pallas_call(
        body,
        name=name,
        out_shape=_exchange_shapes(blocks, gather),
        in_specs=[ANY_SPEC] * n,
        out_specs=[ANY_SPEC] * n,
        scratch_shapes=_exchange_sems(n),
        compiler_params=pltpu.CompilerParams(has_side_effects=True),
    )(*blocks)


def _adam(w, g, m, v):
    m2 = ADAM_B1 * m + (1.0 - ADAM_B1) * g
    v2 = ADAM_B2 * v + (1.0 - ADAM_B2) * jnp.square(g)
    m_hat = m2 / (1.0 - ADAM_B1**ADAM_STEP)
    v_hat = v2 / (1.0 - ADAM_B2**ADAM_STEP)
    delta = -ADAM_LR * (m_hat / (jnp.sqrt(v_hat) + ADAM_EPS) + ADAM_WD * w)
    return delta, m2, v2


TAIL_STEPS = 8


def _reduce_adam_tail(lands, ws, ms, vs, gp_last, last, smalls):
    nw, ns = len(ws), len(smalls)
    tiles = [w.shape[0] // TAIL_STEPS for w in ws]
    blk = gp_last.shape[1:]

    def body(*refs):
        gp_hbm = refs[0]
        part_hbm = refs[1 : 1 + ns]
        refs = refs[1 + ns :]
        land_refs = refs[:nw]
        w_refs, m_refs, v_refs = (refs[nw * (i + 1) : nw * (i + 2)] for i in range(3))
        last_refs = refs[4 * nw : 4 * nw + 3]
        refs = refs[4 * nw + 3 :]
        sw_refs, sm_refs, sv_refs = (refs[ns * i : ns * (i + 1)] for i in range(3))
        outs = refs[3 * ns :]
        big_out = outs[: 4 * nw]
        last_out = outs[4 * nw : 4 * nw + 4]
        small_out = outs[4 * nw + 4 : 4 * nw + 4 + 4 * ns]
        scratch = outs[4 * nw + 4 + 4 * ns :]
        small_land = scratch[:ns]
        pair_bufs = scratch[ns : ns + 4]
        local_sem = scratch[ns + 4]
        pair_sems = scratch[ns + 5 : ns + 9]
        small_sems = scratch[ns + 9 :]
        step = pl.program_id(0)

        def pair():
            return _pair_reduce_scatter(gp_hbm, *pair_bufs, local_sem, pair_sems)

        def small():
            return _direct_exchange(part_hbm, small_land, small_sems, True)

        @pl.when(step == 0)
        def _():
            pair()[0]()
            small()[0]()

        @pl.when(step == 1)
        def _():
            pair()[1]()

        for t in range(nw):
            g = land_refs[t][0].astype(F32)
            for s in range(1, N_DEV):
                g = g + land_refs[t][s].astype(F32)
            for ref, val in zip(big_out[4 * t : 4 * t + 4], (g,) + _adam(w_refs[t][...], g, m_refs[t][...], v_refs[t][...])):
                ref[...] = val

        @pl.when(step == TAIL_STEPS - 1)
        def _():
            pair()[2]()
            small()[1]()
            land = pair_bufs[3]
            g = land[0].astype(F32)
            for chip in range(1, N_CHIPS):
                g = g + land[chip].astype(F32)
            for ref, val in zip(last_out, (g,) + _adam(last_refs[0][...], g, last_refs[1][...], last_refs[2][...])):
                ref[...] = val
            for i in range(ns):
                g = small_land[i][0].astype(F32)
                for s in range(1, N_DEV):
                    g = g + small_land[i][s].astype(F32)
                results = (g,) + _adam(sw_refs[i][...], g, sm_refs[i][...], sv_refs[i][...])
                for ref, val in zip(small_out[4 * i : 4 * i + 4], results):
                    ref[...] = val

    def tile(t):
        return pl.BlockSpec((tiles[t], ws[t].shape[1]), lambda i: (i, 0))

    def land_tile(t):
        return pl.BlockSpec((N_DEV, tiles[t], ws[t].shape[1]), lambda i: (0, i, 0))

    def whole(shape):
        return pl.BlockSpec(shape, lambda step: (0, 0))

    big_sds = [jax.ShapeDtypeStruct(ws[t].shape, F32) for t in range(nw) for _ in range(4)]
    small_sds = [jax.ShapeDtypeStruct(smalls[i][0].shape, F32) for i in range(ns) for _ in range(4)]
    return pl.pallas_call(
        body,
        name="reduce_adam_tail",
        grid=(TAIL_STEPS,),
        out_shape=big_sds + [jax.ShapeDtypeStruct(blk, F32)] * 4 + small_sds,
        in_specs=[ANY_SPEC] * (1 + ns)
        + [land_tile(t) for t in range(nw)]
        + [tile(t) for _ in range(3) for t in range(nw)]
        + [whole(blk)] * 3
        + [whole(smalls[i][0].shape) for _ in range(3) for i in range(ns)],
        out_specs=[tile(t) for t in range(nw) for _ in range(4)]
        + [whole(blk)] * 4
        + [whole(smalls[i][0].shape) for i in range(ns) for _ in range(4)],
        scratch_shapes=[pltpu.VMEM((N_DEV,) + smalls[i][0].shape, smalls[i][0].dtype) for i in range(ns)]
        + [pltpu.VMEM((N_CHIPS,) + blk, BF16)] * 4
        + [pltpu.SemaphoreType.DMA((1,))]
        + [pltpu.SemaphoreType.DMA((count,)) for count in PAIR_SEMS]
        + _exchange_sems(ns),
        compiler_params=pltpu.CompilerParams(
            dimension_semantics=("arbitrary",), vmem_limit_bytes=VMEM_LIMIT, has_side_effects=True
        ),
    )(
        gp_last, *[s[0] for s in smalls], *lands, *ws, *ms, *vs, *last,
        *[s[1] for s in smalls], *[s[2] for s in smalls], *[s[3] for s in smalls],
    )


def _fwd_in(x, g1, w_in):
    S = x.shape[0]
    tm = min(512, S)

    def body(x_ref, g_ref, w_ref, hn_ref, u_ref, q_ref, k_ref, v_ref):
        xv = x_ref[...]
        hn = (xv * _rstd(xv) * g_ref[...]).astype(BF16)
        hn_ref[...] = hn
        outs = (u_ref, q_ref, k_ref, v_ref)
        for j in range(N_DEV):
            p = _dot(hn, w_ref[j])
            cols = slice(256 * (j % 2), 256 * (j % 2 + 1))
            if j // 2 == 0:
                u_ref[:, cols] = p
            elif j // 2 == 1:
                q_ref[:, cols] = (p * (HEAD_DIM**-0.5)).astype(BF16)
            else:
                outs[j // 2][:, cols] = p.astype(BF16)

    half = pl.BlockSpec((tm, D_POOL), lambda i: (i, 0))
    return pl.pallas_call(
        body,
        name="fwd_in",
        grid=(S // tm,),
        out_shape=[
            jax.ShapeDtypeStruct((S, D_MODEL), BF16),
            jax.ShapeDtypeStruct((S, D_POOL), F32),
            jax.ShapeDtypeStruct((S, D_ATTN), BF16),
            jax.ShapeDtypeStruct((S, D_ATTN), BF16),
            jax.ShapeDtypeStruct((S, D_ATTN), BF16),
        ],
        in_specs=[
            pl.BlockSpec((tm, D_MODEL), lambda i: (i, 0)),
            pl.BlockSpec((1, D_MODEL), lambda i: (0, 0)),
            pl.BlockSpec((N_DEV, D_MODEL, 256), lambda i: (0, 0, 0)),
        ],
        out_specs=[pl.BlockSpec((tm, D_MODEL), lambda i: (i, 0)), half, half, half, half],
        compiler_params=_params(("parallel",)),
    )(x, g1, w_in)


POOL_CHUNK = 256
POOL_UNROLL = 3
_POOL_W_SPEC = pl.BlockSpec((1, GROUP_DIM, GROUP_DIM), lambda g: (g, 0, 0))
_POOL_SCALE_SPEC = pl.BlockSpec((1, GROUP_DIM), lambda g: (0, g))


def _lane_sum(wide, n):
    out = wide[:, :GROUP_DIM]
    for i in range(1, n):
        out = out + wide[:, GROUP_DIM * i : GROUP_DIM * (i + 1)]
    return out


def _pool_fwd_group(u_ref, pw_ref, sc_ref, pooled_ref, y_ref, g):
    S = u_ref.shape[0]
    R = min(POOL_CHUNK, S)
    w = jnp.left_shift(jnp.int32(2), g)
    d = lax.broadcasted_iota(jnp.int32, (R, R), 0) - lax.broadcasted_iota(jnp.int32, (R, R), 1)
    b_cur = jnp.where((d >= 0) & (d < w), 1.0, 0.0).astype(BF16)
    dp = lax.broadcasted_iota(jnp.int32, (R, HALO), 0) + HALO - lax.broadcasted_iota(jnp.int32, (R, HALO), 1)
    b_prev = jnp.where(dp < w, 1.0, 0.0).astype(BF16)
    pw = pw_ref[0].astype(BF16)
    sc = sc_ref[...]

    def chunk(r0, first):
        cur = u_ref[pl.ds(r0, R), :]
        wide = _dot(b_cur, jnp.concatenate(_split3(cur), axis=1))
        if not first:
            prev = u_ref[pl.ds(pl.multiple_of(r0 - HALO, HALO), HALO), :]
            wide = wide + _dot(b_prev, jnp.concatenate(_split3(prev), axis=1))
        count = jnp.minimum(r0 + lax.broadcasted_iota(jnp.int32, (R, 1), 0) + 1, w).astype(F32)
        pooled = (_lane_sum(wide, 3) / count - cur).astype(BF16)
        pooled_ref[pl.ds(r0, R), :] = pooled
        y_ref[pl.ds(r0, R), :] = _dot(pooled, pw) * sc

    chunk(0, True)

    def rest(c, carry):
        chunk(pl.multiple_of(c * R, R), False)
        return carry

    lax.fori_loop(1, S // R, rest, 0, unroll=POOL_UNROLL)


ATTN_WINDOW = 3
ATTN_TOGETHER = 2


def _band_ones(T):
    row = lax.broadcasted_iota(jnp.int32, (T, T), 0)
    col = lax.broadcasted_iota(jnp.int32, (T, T), 1)
    after = jnp.where(row > col, 1.0, 0.0).astype(BF16)
    before = jnp.where(row < col, 1.0, 0.0).astype(BF16)
    return jnp.concatenate([after, after], axis=0), jnp.concatenate([before, before], axis=0), col < row


def _log1m(z, mask):
    sp = jnp.log(1.0 + jnp.exp(-jnp.abs(z)))
    l1m = -jnp.maximum(z, 0.0) - sp
    return (l1m if mask is None else jnp.where(mask, l1m, 0.0)), sp


def _log_gates_of(jobs, after2):
    zs = [_dot_nt(qh, kb) for qh, kb, _ in jobs]
    terms = []
    for z, (_, _, mask) in zip(zs, jobs):
        l1m, sp = _log1m(z, mask)
        terms.append(
            (jnp.minimum(z, 0.0) - sp, jnp.concatenate(_split2(l1m), axis=1), jnp.sum(l1m, axis=1, keepdims=True))
        )
    return [(ls, _dot(split, after2), total) for ls, split, total in terms]


def _log_gates(qhs, kbs, masks, after2):
    flat = _log_gates_of([(qh, kb, mask) for kb, mask in zip(kbs, masks) for qh in qhs], after2)
    return [flat[len(qhs) * b : len(qhs) * (b + 1)] for b in range(len(kbs))]


def _tile_start(index, T):
    return index * T if isinstance(index, int) else pl.multiple_of(index * T, T)


def _weights(ls, tail, carry, mask):
    a = jnp.exp(ls + tail + carry)
    return a if mask is None else jnp.where(mask, a, 0.0)


def _reaches(carry):
    return jnp.max(carry) > EXP_UNDERFLOW


def _mixers_fwd(q, k, v, u, pool_w, scale, shards):
    S = q.shape[0]
    T = ATTN_TILE
    nq = S // T
    n = len(shards)
    steps = D_ATTN // LANES

    def body(q_ref, k_ref, v_ref, u_ref, pw_ref, sc_ref, *rest):
        o_ref, pooled_ref, yp_ref = rest[n : n + 3]
        step = pl.program_id(0)

        def gather():
            return _two_level_gather(rest[:n], rest[n + 3 : 2 * n + 3], rest[2 * n + 3 :])

        @pl.when(step == 0)
        def _():
            gather()[0]()

        _pool_fwd_group(u_ref, pw_ref, sc_ref, pooled_ref, yp_ref, step)

        lane = lax.broadcasted_iota(jnp.int32, (T, LANES), 1)
        heads = (lane < HEAD_DIM, lane >= HEAD_DIM)
        after2, _, causal = _band_ones(T)
        zero = jnp.zeros((T, 1), F32)

        def key_tile(k0):
            vb = v_ref[pl.ds(k0, T), :]
            return k_ref[pl.ds(k0, T), :], jnp.concatenate([jnp.where(m, vb, jnp.zeros_like(vb)) for m in heads], axis=0)

        def q_tiles(qis, nb):
            queries, jobs = [], []
            for qi in qis:
                q0 = _tile_start(qi, T)
                qb = q_ref[pl.ds(q0, T), :]
                qhs = [jnp.where(m, qb, jnp.zeros_like(qb)) for m in heads]
                tiles = [key_tile(_tile_start(qi - b, T)) for b in range(nb)]
                queries.append((q0, qhs, [v_rows for _, v_rows in tiles]))
                jobs += [(qh, kb, causal if b == 0 else None) for b, (kb, _) in enumerate(tiles) for qh in qhs]
            gates = _log_gates_of(jobs, after2)
            states = []
            for i, (q0, qhs, values) in enumerate(queries):
                carries, probs = [zero, zero], []
                for b in range(nb):
                    for h in range(2):
                        ls, tail, total = gates[2 * nb * i + 2 * b + h]
                        probs.append(_weights(ls, tail, carries[h], causal if b == 0 else None).astype(BF16))
                        carries[h] = carries[h] + total
                states.append((carries, _dot(jnp.concatenate(probs, axis=1), jnp.concatenate(values, axis=0))))

            def more(st):
                return (st[0] > 0) & _reaches(jnp.maximum(st[1], st[2]))

            for qi, (q0, qhs, _), (carries, acc) in zip(qis, queries, states):

                def k_step(st, qhs=qhs):
                    kj = st[0] - 1
                    kb, v_rows = key_tile(pl.multiple_of(kj * T, T))
                    new, probs = [], []
                    for (ls, tail, total), carry in zip(_log_gates(qhs, [kb], [None], after2)[0], st[1:3]):
                        probs.append(_weights(ls, tail, carry, None).astype(BF16))
                        new.append(carry + total)
                    return kj, new[0], new[1], st[3] + _dot(jnp.concatenate(probs, axis=1), v_rows)

                if not isinstance(qi, int):
                    acc = lax.while_loop(more, k_step, (qi - (nb - 1), carries[0], carries[1], acc))[3]
                o_ref[pl.ds(q0, T), :] = acc

        head_tiles = min(ATTN_WINDOW - 1, nq)
        for qi in range(head_tiles):
            q_tiles([qi], qi + 1)
        assert (nq - head_tiles) % ATTN_TOGETHER == 0

        def q_loop(i, carry):
            first = head_tiles + ATTN_TOGETHER * i
            q_tiles([first + j for j in range(ATTN_TOGETHER)], ATTN_WINDOW)
            return carry

        lax.fori_loop(0, (nq - head_tiles) // ATTN_TOGETHER, q_loop, 0)

        @pl.when(step == steps - 1)
        def _():
            _, forward, finish = gather()
            forward()
            finish()

    blk = pl.BlockSpec((S, LANES), lambda p: (0, p))
    return pl.pallas_call(
        body,
        name="mixers_fwd",
        grid=(steps,),
        out_shape=[
            jax.ShapeDtypeStruct((S, D_ATTN), F32),
            jax.ShapeDtypeStruct((S, D_POOL), BF16),
            jax.ShapeDtypeStruct((S, D_POOL), F32),
        ]
        + _exchange_shapes(shards, True),
        in_specs=[blk, blk, blk, blk, _POOL_W_SPEC, _POOL_SCALE_SPEC] + [ANY_SPEC] * n,
        out_specs=[blk, blk, blk] + [ANY_SPEC] * n,
        scratch_shapes=_exchange_sems(n),
        compiler_params=pltpu.CompilerParams(
            dimension_semantics=("arbitrary",), vmem_limit_bytes=VMEM_LIMIT, has_side_effects=True
        ),
    )(q, k, v, u, pool_w, scale, *shards)


SG_ROWS = 8


def _mlp_fwd_bwd(x, y_pool, y_attn, target, g_pool, g_attn, g2, gf, w_out, w_up, w_down):
    S = x.shape[0]
    tm = min(256, S)
    fc = D_FF // N_DEV

    def body(x_ref, yp_ref, ya_ref, t_ref, gp_ref, ga_ref, g2_ref, gf_ref, wo_hbm, wu_hbm, wd_hbm,
             mixed_ref, hn2_ref, act_ref, dup_ref, dh2_ref, dh1_ref, dh1b_ref, dyp_ref, dya_ref, sg_ref,
             wo, wu, wd, up_s, sems):
        @pl.when(pl.program_id(0) == 0)
        def _():
            copies = [
                pltpu.make_async_copy(wo_hbm, wo, sems.at[0]),
                pltpu.make_async_copy(wu_hbm, wu, sems.at[1]),
                pltpu.make_async_copy(wd_hbm, wd, sems.at[2]),
            ]
            for cp in copies:
                cp.start()
            for cp in copies:
                cp.wait()
            sg_ref[...] = jnp.zeros_like(sg_ref)

        gp, ga, g2v, gfv = gp_ref[...], ga_ref[...], g2_ref[...], gf_ref[...]
        yp, ya = yp_ref[...], ya_ref[...]
        rp, ra = _rstd(yp), _rstd(ya)
        yph, yah = yp * rp, ya * ra
        mixed = jnp.concatenate([(yph * gp).astype(BF16), (yah * ga).astype(BF16)], axis=1)
        mixed_ref[...] = mixed
        h1 = x_ref[...] + _dot(mixed, wo[...])
        r2 = _rstd(h1)
        h1h = h1 * r2
        hn2 = (h1h * g2v).astype(BF16)
        hn2_ref[...] = hn2
        h2 = h1
        for j in range(N_DEV):
            cols = slice(fc * j, fc * (j + 1))
            up = _dot(hn2, wu[j])
            up_s[:, cols] = up
            act = jnp.square(jnp.maximum(up, 0.0)).astype(BF16)
            act_ref[:, cols] = act
            h2 = h2 + _dot(act, wd[cols, :])
        rf = _rstd(h2)
        h2h = h2 * rf
        diff = h2h * gfv - t_ref[...]
        loss_rows = 0.5 * jnp.mean(diff * diff, axis=-1, keepdims=True)
        dy = diff * (1.0 / D_MODEL)
        dh2, dgf = _rms_bwd(dy, h2h, rf, gfv)
        dh2b = dh2.astype(BF16)
        dh2_ref[...] = dh2b
        dhn2 = jnp.zeros((tm, D_MODEL), F32)
        for j in range(N_DEV):
            cols = slice(fc * j, fc * (j + 1))
            dact = _dot_nt(dh2b, wd[cols, :])
            dup = (dact * (2.0 * jnp.maximum(up_s[:, cols], 0.0))).astype(BF16)
            dup_ref[:, cols] = dup
            dhn2 = dhn2 + _dot_nt(dup, wu[j])
        dh1n, dg2 = _rms_bwd(dhn2, h1h, r2, g2v)
        dh1 = dh2 + dh1n
        dh1_ref[...] = dh1
        dh1b = dh1.astype(BF16)
        dh1b_ref[...] = dh1b
        dmix = _dot_nt(dh1b, wo[...])
        dyp, dgp = _rms_bwd(dmix[:, :D_POOL], yph, rp, gp)
        dya, dga = _rms_bwd(dmix[:, D_POOL:], yah, ra, ga)
        dyp_ref[...] = dyp
        dya_ref[...] = dya
        sg_ref[0:1, :] += dgf
        sg_ref[1:2, :] += dg2
        sg_ref[2:3, :] += jnp.concatenate([dgp, dga], axis=1)
        sg_ref[3:4, :] += jnp.broadcast_to(jnp.sum(loss_rows, axis=0, keepdims=True), (1, D_MODEL))

    def tok(n):
        return pl.BlockSpec((tm, n), lambda i: (i, 0))

    def vec(n):
        return pl.BlockSpec((1, n), lambda i: (0, 0))

    any_spec = pl.BlockSpec(memory_space=pl.ANY)
    return pl.pallas_call(
        body,
        name="mlp_fwd_bwd",
        grid=(S // tm,),
        out_shape=[
            jax.ShapeDtypeStruct((S, D_MODEL), BF16),
            jax.ShapeDtypeStruct((S, D_MODEL), BF16),
            jax.ShapeDtypeStruct((S, D_FF), BF16),
            jax.ShapeDtypeStruct((S, D_FF), BF16),
            jax.ShapeDtypeStruct((S, D_MODEL), BF16),
            jax.ShapeDtypeStruct((S, D_MODEL), F32),
            jax.ShapeDtypeStruct((S, D_MODEL), BF16),
            jax.ShapeDtypeStruct((S, D_POOL), F32),
            jax.ShapeDtypeStruct((S, D_ATTN), F32),
            jax.ShapeDtypeStruct((SG_ROWS, D_MODEL), F32),
        ],
        in_specs=[tok(D_MODEL), tok(D_POOL), tok(D_ATTN), tok(D_MODEL), vec(D_POOL), vec(D_ATTN),
                  vec(D_MODEL), vec(D_MODEL), any_spec, any_spec, any_spec],
        out_specs=[tok(D_MODEL), tok(D_MODEL), tok(D_FF), tok(D_FF), tok(D_MODEL), tok(D_MODEL),
                   tok(D_MODEL), tok(D_POOL), tok(D_ATTN),
                   pl.BlockSpec((SG_ROWS, D_MODEL), lambda i: (0, 0))],
        scratch_shapes=[
            pltpu.VMEM((D_MODEL, D_MODEL), BF16),
            pltpu.VMEM((N_DEV, D_MODEL, fc), BF16),
            pltpu.VMEM((D_FF, D_MODEL), BF16),
            pltpu.VMEM((tm, D_FF), F32),
            pltpu.SemaphoreType.DMA((3,)),
        ],
        compiler_params=_params(("arbitrary",)),
    )(x, y_pool, y_attn, target, g_pool, g_attn, g2, gf, w_out, w_up, w_down)


def _wgrad(a, b, block_a, groups, name, travelling=()):
    n = len(travelling)
    S, ka = a.shape
    nb = b.shape[1]
    ts = min(1024, S)
    per = N_DEV // groups
    if block_a:
        ka //= groups
        blk = (ka // per, nb)
        a_spec = pl.BlockSpec((ts, ka), lambda g, s: (s, g))
        b_spec = pl.BlockSpec((ts, nb), lambda g, s: (s, 0))
    else:
        nb //= groups
        blk = (ka, nb // per)
        a_spec = pl.BlockSpec((ts, ka), lambda g, s: (s, 0))
        b_spec = pl.BlockSpec((ts, nb), lambda g, s: (s, g))
    steps = S // ts

    def body(a_ref, b_ref, *rest):
        o_ref, acc = rest[n], rest[2 * n + 1]
        g, s = pl.program_id(0), pl.program_id(1)

        def scatter():
            return _direct_exchange(rest[:n], rest[n + 1 : 2 * n + 1], rest[2 * n + 2 :], False)

        if n:
            @pl.when((g == 0) & (s == 0))
            def _():
                scatter()[0]()

        @pl.when(s == 0)
        def _():
            acc[...] = jnp.zeros_like(acc)

        acc[...] += _dot_tn(a_ref[...], b_ref[...])

        @pl.when(s == steps - 1)
        def _():
            for j in range(per):
                if block_a:
                    o_ref[j] = acc[blk[0] * j : blk[0] * (j + 1), :].astype(BF16)
                else:
                    o_ref[j] = acc[:, blk[1] * j : blk[1] * (j + 1)].astype(BF16)

        if n:
            @pl.when((g == groups - 1) & (s == steps - 1))
            def _():
                scatter()[1]()

    results = pl.pallas_call(
        body,
        name=name,
        grid=(groups, steps),
        out_shape=[jax.ShapeDtypeStruct((N_DEV,) + blk, BF16)] + _exchange_shapes(travelling, False),
        in_specs=[a_spec, b_spec] + [ANY_SPEC] * n,
        out_specs=[pl.BlockSpec((per,) + blk, lambda g, s: (g, 0, 0))] + [ANY_SPEC] * n,
        scratch_shapes=[pltpu.VMEM((ka, nb), F32)] + (_exchange_sems(n) if n else []),
        compiler_params=pltpu.CompilerParams(
            dimension_semantics=("arbitrary", "arbitrary"), vmem_limit_bytes=VMEM_LIMIT, has_side_effects=bool(n)
        ),
    )(a, b, *travelling)
    return results if n else results[0]


def _mixers_bwd(q, k, v, do, dyp, pooled, pool_w, scale, partials):
    S = q.shape[0]
    T = ATTN_TILE
    nq = S // T
    n = len(partials)
    steps = D_ATTN // LANES

    def body(q_ref, k_ref, v_ref, do_ref, dyp_ref, pooled_ref, pw_ref, sc_ref, *rest):
        dq_ref, dk_ref, dv_ref, du_ref, dsc_ref, dpw_ref = rest[n : n + 6]
        dk_acc, dv_acc, carry_s = rest[2 * n + 6 : 2 * n + 9]
        step = pl.program_id(0)

        def scatter():
            return _direct_exchange(rest[:n], rest[n + 6 : 2 * n + 6], rest[2 * n + 9 :], False)

        @pl.when(step == 0)
        def _():
            scatter()[0]()

        _pool_bwd_group(dyp_ref, pooled_ref, pw_ref, sc_ref, du_ref, dsc_ref, dpw_ref, step)

        dk_acc[...] = jnp.zeros_like(dk_acc)
        dv_acc[...] = jnp.zeros_like(dv_acc)
        lane = lax.broadcasted_iota(jnp.int32, (T, LANES), 1)
        heads = (lane < HEAD_DIM, lane >= HEAD_DIM)
        after2, before2, causal = _band_ones(T)
        zero = jnp.zeros((T, 1), F32)

        def key_tile(k0):
            kb = k_ref[pl.ds(k0, T), :]
            k_rows = jnp.concatenate([jnp.where(m, kb, jnp.zeros_like(kb)) for m in heads], axis=0)
            return kb, v_ref[pl.ds(k0, T), :], k_rows

        def grads(items):
            das = [[[_dot_nt(doh, vb) for doh in qs[1]] for _, vb, _ in tiles] for qs, _, tiles, _, _, _, _ in items]
            probs, gs = [], []
            for (_, _, tiles, masks, carry_in, gates, _), da in zip(items, das):
                probs.append([[_weights(gates[b][h][0], gates[b][h][1], carry_in[b][h], masks[b]) for h in range(2)]
                              for b in range(len(tiles))])
                gs.append([[probs[-1][b][h] * da[b][h] for h in range(2)] for b in range(len(tiles))])
            before = [[[_dot(jnp.concatenate(_split2(g), axis=1), before2) for g in row] for row in item] for item in gs]
            dzs, lefts = [], []
            for i, (_, _, tiles, masks, _, gates, st) in enumerate(items):
                g_left, item_dz = [st[0], st[1]], [None] * len(tiles)
                for b in reversed(range(len(tiles))):
                    item_dz[b] = []
                    for h in range(2):
                        sig = jnp.exp(gates[b][h][0])
                        dz = gs[i][b][h] * (1.0 - sig) - sig * (g_left[h] + before[i][b][h])
                        if masks[b] is not None:
                            dz = jnp.where(masks[b], dz, 0.0)
                        item_dz[b].append(dz.astype(BF16))
                        g_left[h] = g_left[h] + jnp.sum(gs[i][b][h], axis=1, keepdims=True)
                dzs.append(item_dz)
                lefts.append(g_left)
            out = []
            for (_, _, tiles, _, _, _, st), item_dz, g_left in zip(items, dzs, lefts):
                dq = st[2] + _dot(
                    jnp.concatenate([dz for row in item_dz for dz in row], axis=1),
                    jnp.concatenate([k_rows for _, _, k_rows in tiles], axis=0),
                )
                out.append((g_left[0], g_left[1], dq))
            for (qs, starts, tiles, _, _, _, _), item_dz, item_probs in zip(items, dzs, probs):
                for b in range(len(tiles)):
                    dk_acc[pl.ds(starts[b], T), :] += _dot_tn(jnp.concatenate(item_dz[b], axis=0), qs[2])
                    dv_acc[pl.ds(starts[b], T), :] += _dot_tn(
                        jnp.concatenate([a.astype(BF16) for a in item_probs[b]], axis=0), qs[3]
                    )
            return out

        def more(st):
            return (st[0] > 0) & _reaches(jnp.maximum(st[1], st[2]))

        def q_tiles(qis, nb):
            masks = [causal] + [None] * (nb - 1)
            prepared, jobs = [], []
            for qi in qis:
                q0 = _tile_start(qi, T)
                qb = q_ref[pl.ds(q0, T), :]
                dob = do_ref[pl.ds(q0, T), :]
                qhs = [jnp.where(m, qb, jnp.zeros_like(qb)) for m in heads]
                dohs = [jnp.where(m, dob, 0.0).astype(BF16) for m in heads]
                queries = (qhs, dohs, jnp.concatenate(qhs, axis=0), jnp.concatenate(dohs, axis=0))
                starts = [_tile_start(qi - b, T) for b in range(nb)]
                tiles = [key_tile(k0) for k0 in starts]
                prepared.append((q0, queries, starts, tiles))
                jobs += [(qh, kb, mask) for (kb, _, _), mask in zip(tiles, masks) for qh in qhs]
            flat = _log_gates_of(jobs, after2)
            items = []
            for i, (qi, (q0, queries, starts, tiles)) in enumerate(zip(qis, prepared)):
                gates = [flat[2 * nb * i + 2 * b : 2 * nb * i + 2 * b + 2] for b in range(nb)]
                carries, carry_in = [zero, zero], []
                for b in range(nb):
                    carry_in.append(list(carries))
                    carries = [carries[h] + gates[b][h][2] for h in range(2)]
                st = (zero, zero, jnp.zeros((T, LANES), F32))
                if not isinstance(qi, int):
                    k_left = qi - (nb - 1)
                    qhs = queries[0]

                    def right_to_left(st, qhs=qhs):
                        kj = st[0] - 1
                        carry_s[0, kj] = st[1]
                        carry_s[1, kj] = st[2]
                        kb = k_ref[pl.ds(pl.multiple_of(kj * T, T), T), :]
                        sums = [jnp.sum(_log1m(_dot_nt(qh, kb), None)[0], axis=1, keepdims=True) for qh in qhs]
                        return kj, st[1] + sums[0], st[2] + sums[1]

                    k_first = lax.while_loop(more, right_to_left, (k_left, carries[0], carries[1]))[0]

                    def left_to_right(kj, st, queries=queries):
                        k0 = pl.multiple_of(kj * T, T)
                        tile = key_tile(k0)
                        left_gates = _log_gates(queries[0], [tile[0]], [None], after2)
                        carry = [[carry_s[0, kj], carry_s[1, kj]]]
                        return grads([(queries, [k0], [tile], [None], carry, left_gates, st)])[0]

                    st = lax.fori_loop(k_first, k_left, left_to_right, st)
                items.append((queries, starts, tiles, masks, carry_in, gates, st))
            for (q0, _, _, _), st in zip(prepared, grads(items)):
                dq_ref[pl.ds(q0, T), :] = (st[2] * (HEAD_DIM**-0.5)).astype(BF16)

        head_tiles = min(ATTN_WINDOW - 1, nq)
        for qi in range(head_tiles):
            q_tiles([qi], qi + 1)
        assert (nq - head_tiles) % ATTN_TOGETHER == 0

        def q_loop(i, carry):
            first = head_tiles + ATTN_TOGETHER * i
            q_tiles([first + j for j in range(ATTN_TOGETHER)], ATTN_WINDOW)
            return carry

        lax.fori_loop(0, (nq - head_tiles) // ATTN_TOGETHER, q_loop, 0)
        dk_ref[...] = dk_acc[...].astype(BF16)
        dv_ref[...] = dv_acc[...].astype(BF16)

        @pl.when(step == steps - 1)
        def _():
            scatter()[1]()

    blk = pl.BlockSpec((S, LANES), lambda p: (0, p))
    sds = jax.ShapeDtypeStruct((S, D_ATTN), BF16)
    return pl.pallas_call(
        body,
        name="mixers_bwd",
        grid=(steps,),
        out_shape=[
            sds,
            sds,
            sds,
            jax.ShapeDtypeStruct((S, D_POOL), BF16),
            jax.ShapeDtypeStruct((8, D_POOL), F32),
            jax.ShapeDtypeStruct((N_GROUPS, GROUP_DIM, GROUP_DIM), BF16),
        ]
        + _exchange_shapes(partials, False),
        in_specs=[blk] * 6 + [_POOL_W_SPEC, _POOL_SCALE_SPEC] + [ANY_SPEC] * n,
        out_specs=[blk] * 4 + [pl.BlockSpec((8, GROUP_DIM), lambda p: (0, p)), _POOL_W_SPEC] + [ANY_SPEC] * n,
        scratch_shapes=[pltpu.VMEM((S, LANES), F32), pltpu.VMEM((S, LANES), F32), pltpu.VMEM((2, nq, T, 1), F32)]
        + _exchange_sems(n),
        compiler_params=pltpu.CompilerParams(
            dimension_semantics=("arbitrary",), vmem_limit_bytes=VMEM_LIMIT, has_side_effects=True
        ),
    )(q, k, v, do, dyp, pooled, pool_w, scale, *partials)


def _pool_bwd_group(dy_ref, pooled_ref, pw_ref, sc_ref, du_ref, dsc_ref, dpw_ref, g):
    S = dy_ref.shape[0]
    R = min(POOL_CHUNK, S)
    nc = S // R
    w = jnp.left_shift(jnp.int32(2), g)
    pw = pw_ref[0].astype(BF16)
    sc = sc_ref[...]
    d = lax.broadcasted_iota(jnp.int32, (R, R), 1) - lax.broadcasted_iota(jnp.int32, (R, R), 0)
    bt_cur = jnp.where((d >= 0) & (d < w), 1.0, 0.0).astype(BF16)
    dn = lax.broadcasted_iota(jnp.int32, (R, HALO), 1) + R - lax.broadcasted_iota(jnp.int32, (R, HALO), 0)
    bt_next = jnp.where(dn < w, 1.0, 0.0).astype(BF16)

    def per_count(dpl, r0):
        n = dpl.shape[0]
        count = jnp.minimum(r0 + lax.broadcasted_iota(jnp.int32, (n, 1), 0) + 1, w).astype(F32)
        return jnp.concatenate(_split2(dpl / count), axis=1)

    def chunk(r0, last, sums):
        dyv = dy_ref[pl.ds(r0, R), :]
        pooled = pooled_ref[pl.ds(r0, R), :]
        dmapped = (dyv * sc).astype(BF16)
        dsc = sums[0] + jnp.sum(dyv * _dot(pooled, pw), axis=0, keepdims=True)
        dpw = sums[1] + _dot_tn(pooled, dmapped)
        dpl = _dot_nt(dmapped, pw)
        wide = _dot(bt_cur, per_count(dpl, r0))
        if not last:
            nxt = pl.multiple_of(r0 + R, R)
            dpl_next = _dot_nt((dy_ref[pl.ds(nxt, HALO), :] * sc).astype(BF16), pw)
            wide = wide + _dot(bt_next, per_count(dpl_next, nxt))
        du_ref[pl.ds(r0, R), :] = (_lane_sum(wide, 2) - dpl).astype(BF16)
        return dsc, dpw

    sums = (jnp.zeros((1, GROUP_DIM), F32), jnp.zeros((GROUP_DIM, GROUP_DIM), F32))
    sums = lax.fori_loop(
        0, nc - 1, lambda c, s: chunk(pl.multiple_of(c * R, R), False, s), sums, unroll=POOL_UNROLL
    )
    dsc, dpw = chunk((nc - 1) * R, True, sums)
    dsc_ref[...] = jnp.zeros_like(dsc_ref)
    dsc_ref[0:1, :] = dsc
    dpw_ref[0] = dpw.astype(dpw_ref.dtype)


def _bwd_in(du, dq, dk, dv, w_in, x, dh1, g1):
    S = x.shape[0]
    tm = min(512, S)

    def body(du_ref, dq_ref, dk_ref, dv_ref, w_ref, x_ref, dh1_ref, g_ref, dx_ref, dproj_ref, dg_ref):
        @pl.when(pl.program_id(0) == 0)
        def _():
            dg_ref[...] = jnp.zeros_like(dg_ref)

        parts = (du_ref[...], dq_ref[...], dk_ref[...], dv_ref[...])
        dhn = jnp.zeros((tm, D_MODEL), F32)
        for j in range(N_DEV):
            piece = parts[j // 2][:, 256 * (j % 2) : 256 * (j % 2 + 1)]
            dproj_ref[:, 256 * j : 256 * (j + 1)] = piece
            dhn = dhn + _dot_nt(piece, w_ref[j])
        xv = x_ref[...]
        r = _rstd(xv)
        dxn, dg = _rms_bwd(dhn, xv * r, r, g_ref[...])
        dx_ref[...] = dh1_ref[...] + dxn
        dg_ref[0:1, :] += dg

    half = pl.BlockSpec((tm, D_POOL), lambda i: (i, 0))
    full = pl.BlockSpec((tm, D_MODEL), lambda i: (i, 0))
    return pl.pallas_call(
        body,
        name="bwd_in",
        grid=(S // tm,),
        out_shape=[
            jax.ShapeDtypeStruct((S, D_MODEL), F32),
            jax.ShapeDtypeStruct((S, D_IN_PROJ), BF16),
            jax.ShapeDtypeStruct((8, D_MODEL), F32),
        ],
        in_specs=[half, half, half, half,
                  pl.BlockSpec((N_DEV, D_MODEL, 256), lambda i: (0, 0, 0)),
                  full, full, pl.BlockSpec((1, D_MODEL), lambda i: (0, 0))],
        out_specs=[full, pl.BlockSpec((tm, D_IN_PROJ), lambda i: (i, 0)),
                   pl.BlockSpec((8, D_MODEL), lambda i: (0, 0))],
        compiler_params=_params(("arbitrary",)),
    )(du, dq, dk, dv, w_in, x, dh1, g1)


def _rows(a):
    a = a.reshape(-1, LANES)
    pad = (-a.shape[0]) % 8
    return jnp.pad(a, ((0, pad), (0, 0))) if pad else a


def kernel(x, norm1_g, w_in, pool_w, pool_scale, pool_out_g, attn_out_g, w_out, norm2_g, w_up, w_down, final_g, loss_target, m_norm1_g, m_w_in, m_pool_w, m_pool_scale, m_pool_out_g, m_attn_out_g, m_w_out, m_norm2_g, m_w_up, m_w_down, m_final_g, v_norm1_g, v_w_in, v_pool_w, v_pool_scale, v_pool_out_g, v_attn_out_g, v_w_out, v_norm2_g, v_w_up, v_w_down, v_final_g):
    S = x.shape[1]
    xs = x.reshape(S, D_MODEL)
    tgt = loss_target.reshape(S, D_MODEL)
    row = lambda a: a.reshape(1, -1)

    (w_in_g,) = _exchange([w_in.astype(BF16)], True, "gather_w_in")
    hn, u_pool, q, k, v = _fwd_in(xs, row(norm1_g), w_in_g)
    y_attn, pooled, y_pool, w_out_g, w_up_g, w_down_g = _mixers_fwd(
        q, k, v, u_pool, pool_w, row(pool_scale), [w_out.astype(BF16), w_up.astype(BF16), w_down.astype(BF16)]
    )
    w_out_full = w_out_g.reshape(D_MODEL, D_MODEL)
    w_down_full = w_down_g.reshape(D_FF, D_MODEL)
    mixed, hn2, act, dup, dh2b, dh1, dh1b, dyp, dya, sg = _mlp_fwd_bwd(
        xs, y_pool, y_attn, tgt, row(pool_out_g), row(attn_out_g), row(norm2_g), row(final_g),
        w_out_full, w_up_g, w_down_full,
    )
    gp_down = _wgrad(act, dh2b, True, 2, "wgrad_down")
    gp_up = _wgrad(hn2, dup, False, 2, "wgrad_up")
    gp_out = _wgrad(mixed, dh1b, True, 1, "wgrad_out")
    dq, dk, dv, du, dsc, dpw, land_up, land_down = _mixers_bwd(
        q, k, v, dya, dyp, pooled, pool_w, row(pool_scale), [gp_up, gp_down]
    )
    dx, dproj, dg1 = _bwd_in(du, dq, dk, dv, w_in_g, xs, dh1, row(norm1_g))
    gp_in, land_out = _wgrad(hn, dproj, False, 1, "wgrad_in", [gp_out])

    def vectors(final, norm2, pool_out, attn_out, norm1, scale, last):
        pieces = [final, norm2, jnp.concatenate([pool_out, attn_out]), norm1, scale, last]
        return jnp.concatenate([_rows(p) for p in pieces], axis=0)

    no_loss = jnp.zeros((8, LANES), F32)
    smalls = [
        (
            vectors(sg[0], sg[1], sg[2][:D_POOL], sg[2][D_POOL:], dg1[0], dsc[0], sg[3]),
            vectors(final_g, norm2_g, pool_out_g, attn_out_g, norm1_g, pool_scale, no_loss),
            vectors(m_final_g, m_norm2_g, m_pool_out_g, m_attn_out_g, m_norm1_g, m_pool_scale, no_loss),
            vectors(v_final_g, v_norm2_g, v_pool_out_g, v_attn_out_g, v_norm1_g, v_pool_scale, no_loss),
        ),
        tuple(a.reshape(N_GROUPS * GROUP_DIM, GROUP_DIM) for a in (dpw, pool_w, m_pool_w, v_pool_w)),
    ]
    tail = _reduce_adam_tail(
        [land_out, land_up, land_down], [w_out, w_up, w_down], [m_w_out, m_w_up, m_w_down],
        [v_w_out, v_w_up, v_w_down], gp_in, (w_in, m_w_in, v_w_in), smalls,
    )
    big = {name: tail[4 * t : 4 * t + 4] for t, name in enumerate(("w_out", "w_up", "w_down", "w_in"))}

    def unpack(vec, pw):
        out = {}
        for i, name in enumerate(("final_g", "norm2_g", "mix_g", "norm1_g", "pool_scale")):
            out[name] = vec[8 * i : 8 * i + 8].reshape(-1)
        out["pool_scale"] = out["pool_scale"][:D_POOL]
        out["pool_out_g"], out["attn_out_g"] = out["mix_g"][:D_POOL], out["mix_g"][D_POOL:]
        out["pool_w"] = pw.reshape(N_GROUPS, GROUP_DIM, GROUP_DIM)
        return out, vec[40, 0]

    small_out = [unpack(tail[16 + i], tail[20 + i]) for i in range(4)]
    loss = small_out[0][1]
    order = ("norm1_g", "w_in", "pool_w", "pool_scale", "pool_out_g", "attn_out_g", "w_out", "norm2_g", "w_up",
             "w_down", "final_g")
    outs = [loss, dx.reshape(1, S, D_MODEL)]
    for i in range(4):
        for name in order:
            outs.append(big[name][i] if name in big else small_out[i][0][name])
    return tuple(outs)
```

```python
import functools

import jax
import jax.numpy as jnp
from jax import lax
from jax.experimental import pallas as pl
from jax.experimental.pallas import tpu as pltpu

F32 = jnp.float32
BF16 = jnp.bfloat16
MESH = pl.DeviceIdType.MESH

N_DEV = 8
D_MODEL = 1024
D_POOL = 512
D_ATTN = 512
N_GROUPS = 4
GROUP_DIM = 128
HEAD_DIM = 64
D_FF = 4096
D_IN_PROJ = 2048
EPS = 1e-6
HALO = 16
ATTN_TILE = 128
LANES = 128
EXP_UNDERFLOW = -104.0

ADAM_LR = 0.001
ADAM_B1 = 0.9
ADAM_B2 = 0.999
ADAM_EPS = 1e-08
ADAM_WD = 0.01
ADAM_STEP = 10

VMEM_LIMIT = 56 * 1024 * 1024


def _params(semantics=None, vmem=VMEM_LIMIT):
    return pltpu.CompilerParams(dimension_semantics=semantics, vmem_limit_bytes=vmem)


def _dot(a, b):
    return jnp.dot(a, b, preferred_element_type=F32)


def _dot_nt(a, b):
    return lax.dot_general(a, b, (((1,), (1,)), ((), ())), preferred_element_type=F32)


def _dot_tn(a, b):
    return lax.dot_general(a, b, (((0,), (0,)), ((), ())), preferred_element_type=F32)


def _split2(x):
    hi = x.astype(BF16)
    lo = (x - hi.astype(F32)).astype(BF16)
    return hi, lo


def _split3(x):
    hi = x.astype(BF16)
    r = x - hi.astype(F32)
    mid = r.astype(BF16)
    lo = (r - mid.astype(F32)).astype(BF16)
    return hi, mid, lo


def _rstd(h):
    return lax.rsqrt(jnp.mean(h * h, axis=-1, keepdims=True) + EPS)


def _rms_bwd(dout, hhat, r, g):
    dg = jnp.sum(dout * hhat, axis=0, keepdims=True)
    dxh = dout * g
    dh = r * (dxh - hhat * jnp.mean(dxh * hhat, axis=-1, keepdims=True))
    return dh, dg


def _my_index():
    return 4 * lax.axis_index("x") + 2 * lax.axis_index("y") + lax.axis_index("c")


def _peer(k):
    x, y, c = lax.axis_index("x"), lax.axis_index("y"), lax.axis_index("c")
    px = 1 - x if (k >> 2) & 1 else x
    py = 1 - y if (k >> 1) & 1 else y
    pc = 1 - c if k & 1 else c
    return (px, py, pc), 4 * px + 2 * py + pc


N_PEERS = N_DEV - 1
ANY_SPEC = pl.BlockSpec(memory_space=pl.ANY)


def _exchange_sems(n):
    return [
        pltpu.SemaphoreType.DMA((n * N_PEERS,)),
        pltpu.SemaphoreType.DMA((n * N_PEERS,)),
        pltpu.SemaphoreType.DMA((n,)),
    ]


def _exchange_shapes(blocks, gather):
    if gather:
        return [jax.ShapeDtypeStruct((N_DEV,) + b.shape, b.dtype) for b in blocks]
    return [jax.ShapeDtypeStruct(b.shape, b.dtype) for b in blocks]


def _direct_exchange(ins, outs, sems, gather):
    send_sems, recv_sems, local_sems = sems
    me = _my_index()
    own, sends, recvs = [], [], []
    for t in range(len(ins)):
        own.append(pltpu.make_async_copy(ins[t] if gather else ins[t].at[me], outs[t].at[me], local_sems.at[t]))
        for k in range(1, N_DEV):
            peer, peer_idx = _peer(k)
            src = ins[t] if gather else ins[t].at[peer_idx]
            for dst, bucket in ((outs[t].at[me], sends), (outs[t].at[peer_idx], recvs)):
                bucket.append(
                    pltpu.make_async_remote_copy(
                        src_ref=src,
                        dst_ref=dst,
                        send_sem=send_sems.at[t * N_PEERS + k - 1],
                        recv_sem=recv_sems.at[t * N_PEERS + k - 1],
                        device_id=peer,
                        device_id_type=MESH,
                    )
                )

    def start():
        for cp in own + sends:
            cp.start()

    def finish():
        for cp in recvs:
            cp.wait_recv()
        for cp in sends:
            cp.wait_send()
        for cp in own:
            cp.wait()

    return start, finish


def _two_level_gather(ins, outs, sems):
    send_sems, recv_sems, local_sems = sems
    x, y, c = lax.axis_index("x"), lax.axis_index("y"), lax.axis_index("c")
    me, sibling = (x, y, c), (x, y, 1 - c)
    chips = [(1 - x, y), (x, 1 - y), (1 - x, 1 - y)]

    def copy(t, k, block, to, from_input=False):
        slot = outs[t].at[4 * block[0] + 2 * block[1] + block[2]]
        return pltpu.make_async_remote_copy(
            src_ref=ins[t] if from_input else slot,
            dst_ref=slot,
            send_sem=send_sems.at[t * N_PEERS + k],
            recv_sem=recv_sems.at[t * N_PEERS + k],
            device_id=to,
            device_id_type=MESH,
        )

    arrays = range(len(ins))
    own = [pltpu.make_async_copy(ins[t], outs[t].at[4 * x + 2 * y + c], local_sems.at[t]) for t in arrays]
    first = [copy(t, 1 + j, me, (*chip, c), True) for t in arrays for j, chip in enumerate(chips)]
    first += [copy(t, 0, me, sibling, True) for t in arrays]
    passed = [(copy(t, 1 + j, (*chip, c), me), copy(t, 4 + j, (*chip, c), sibling))
              for t in arrays for j, chip in enumerate(chips)]
    last = [copy(t, 0, sibling, me) for t in arrays]
    last += [copy(t, 4 + j, (*chip, 1 - c), me) for t in arrays for j, chip in enumerate(chips)]

    def start():
        for cp in own + first:
            cp.start()

    def forward():
        for landed, onward in passed:
            landed.wait_recv()
            onward.start()

    def finish():
        for cp in last:
            cp.wait_recv()
        for cp in first + [onward for _, onward in passed]:
            cp.wait_send()
        for cp in own:
            cp.wait()

    return start, forward, finish


N_CHIPS = 4
PAIR_SEMS = (N_CHIPS, N_CHIPS, N_CHIPS - 1, N_CHIPS - 1)


def _pair_reduce_scatter(gp_hbm, own, pair, summed, land, local_sem, sems):
    d2d_send, d2d_recv, ici_send, ici_recv = sems
    x, y, c = lax.axis_index("x"), lax.axis_index("y"), lax.axis_index("c")
    my_chip = 2 * x + y
    sibling = (x, y, 1 - c)
    chips = [(1 - x, y), (x, 1 - y), (1 - x, 1 - y)]
    local = [pltpu.make_async_copy(gp_hbm.at[2 * j + c], own.at[j], local_sem.at[0]) for j in range(N_CHIPS)]
    to_sibling = [
        pltpu.make_async_remote_copy(
            src_ref=gp_hbm.at[2 * j + 1 - c], dst_ref=pair.at[j], send_sem=d2d_send.at[j], recv_sem=d2d_recv.at[j],
            device_id=sibling, device_id_type=MESH,
        )
        for j in range(N_CHIPS)
    ]
    to_chips, from_chips = [], []
    for r, (px, py) in enumerate(chips):
        for dst, bucket in ((land.at[my_chip], to_chips), (land.at[2 * px + py], from_chips)):
            bucket.append(
                pltpu.make_async_remote_copy(
                    src_ref=summed.at[2 * px + py], dst_ref=dst, send_sem=ici_send.at[r], recv_sem=ici_recv.at[r],
                    device_id=(px, py, c), device_id_type=MESH,
                )
            )

    def start():
        for cp in local + to_sibling:
            cp.start()

    def middle():
        for cp in to_sibling:
            cp.wait_recv()
        pltpu.make_async_copy(gp_hbm.at[pl.ds(0, N_CHIPS)], own, local_sem.at[0]).wait()
        for j in range(N_CHIPS):
            summed[j] = (own[j].astype(F32) + pair[j].astype(F32)).astype(BF16)
        for cp in to_chips:
            cp.start()

    def finish():
        for cp in from_chips:
            cp.wait_recv()
        for cp in to_chips + to_sibling:
            cp.wait_send()
        land[my_chip] = summed[my_chip]

    return start, middle, finish


def _exchange(blocks, gather, name):
    n = len(blocks)

    def body(*refs):
        if gather:
            stages = _two_level_gather(refs[:n], refs[n : 2 * n], refs[2 * n :])
        else:
            stages = _direct_exchange(refs[:n], refs[n : 2 * n], refs[2 * n :], False)
        for stage in stages:
            stage()

    return pl.pallas_call(
        body,
        name=name,
        out_shape=_exchange_shapes(blocks, gather),
        in_specs=[ANY_SPEC] * n,
        out_specs=[ANY_SPEC] * n,
        scratch_shapes=_exchange_sems(n),
        compiler_params=pltpu.CompilerParams(has_side_effects=True),
    )(*blocks)


def _adam(w, g, m, v):
    m2 = ADAM_B1 * m + (1.0 - ADAM_B1) * g
    v2 = ADAM_B2 * v + (1.0 - ADAM_B2) * jnp.square(g)
    m_hat = m2 / (1.0 - ADAM_B1**ADAM_STEP)
    v_hat = v2 / (1.0 - ADAM_B2**ADAM_STEP)
    delta = -ADAM_LR * (m_hat / (jnp.sqrt(v_hat) + ADAM_EPS) + ADAM_WD * w)
    return delta, m2, v2


TAIL_STEPS = 8


def _reduce_adam_tail(lands, ws, ms, vs, gp_last, last, smalls):
    nw, ns = len(ws), len(smalls)
    tiles = [w.shape[0] // TAIL_STEPS for w in ws]
    blk = gp_last.shape[1:]
    entries = [(i, e) for i, (_, es) in enumerate(smalls) for e in es]
    n_params = 3 * sum(e[0] is not None for _, e in entries)
    n_small_out = sum(4 if e[0] is not None else 1 for _, e in entries)

    def body(*refs):
        gp_hbm = refs[0]
        part_hbm = refs[1 : 1 + ns]
        refs = refs[1 + ns :]
        land_refs = refs[:nw]
        w_refs, m_refs, v_refs = (refs[nw * (i + 1) : nw * (i + 2)] for i in range(3))
        last_refs = refs[4 * nw : 4 * nw + 3]
        refs = refs[4 * nw + 3 :]
        param_refs = refs[:n_params]
        outs = refs[n_params:]
        big_out = outs[: 4 * nw]
        last_out = outs[4 * nw : 4 * nw + 4]
        small_out = outs[4 * nw + 4 : 4 * nw + 4 + n_small_out]
        scratch = outs[4 * nw + 4 + n_small_out :]
        small_land = scratch[:ns]
        pair_bufs = scratch[ns : ns + 4]
        local_sem = scratch[ns + 4]
        pair_sems = scratch[ns + 5 : ns + 9]
        small_sems = scratch[ns + 9 :]
        step = pl.program_id(0)

        def pair():
            return _pair_reduce_scatter(gp_hbm, *pair_bufs, local_sem, pair_sems)

        def small():
            return _direct_exchange(part_hbm, small_land, small_sems, True)

        @pl.when(step == 0)
        def _():
            pair()[0]()
            small()[0]()

        @pl.when(step == 1)
        def _():
            pair()[1]()

        for t in range(nw):
            g = land_refs[t][0].astype(F32)
            for s in range(1, N_DEV):
                g = g + land_refs[t][s].astype(F32)
            for ref, val in zip(big_out[4 * t : 4 * t + 4], (g,) + _adam(w_refs[t][...], g, m_refs[t][...], v_refs[t][...])):
                ref[...] = val

        @pl.when(step == TAIL_STEPS - 1)
        def _():
            pair()[2]()
            small()[1]()
            land = pair_bufs[3]
            g = land[0].astype(F32)
            for chip in range(1, N_CHIPS):
                g = g + land[chip].astype(F32)
            for ref, val in zip(last_out, (g,) + _adam(last_refs[0][...], g, last_refs[1][...], last_refs[2][...])):
                ref[...] = val
            sums = []
            for i in range(ns):
                g = small_land[i][0].astype(F32)
                for s in range(1, N_DEV):
                    g = g + small_land[i][s].astype(F32)
                sums.append(g)
            params, results = list(param_refs), list(small_out)
            for i, (w, _, _, where) in entries:
                g = sums[i]
                if where is not None:
                    shape = w.shape if w is not None else where[2:]
                    g = g[where[0] : where[0] + shape[0], where[1] : where[1] + shape[1]]
                if w is None:
                    results.pop(0)[...] = g
                    continue
                w_ref, m_ref, v_ref = params[:3]
                del params[:3]
                for val in (g,) + _adam(w_ref[...], g, m_ref[...], v_ref[...]):
                    results.pop(0)[...] = val

    def tile(t):
        return pl.BlockSpec((tiles[t], ws[t].shape[1]), lambda i: (i, 0))

    def land_tile(t):
        return pl.BlockSpec((N_DEV, tiles[t], ws[t].shape[1]), lambda i: (0, i, 0))

    def whole(shape):
        return pl.BlockSpec(shape, lambda step: (0, 0))

    big_sds = [jax.ShapeDtypeStruct(ws[t].shape, F32) for t in range(nw) for _ in range(4)]
    params, small_shapes = [], []
    for _, (w, m, v, where) in entries:
        if w is None:
            small_shapes.append(tuple(where[2:]))
        else:
            params += [w, m, v]
            small_shapes += [w.shape] * 4
    return pl.pallas_call(
        body,
        name="reduce_adam_tail",
        grid=(TAIL_STEPS,),
        out_shape=big_sds
        + [jax.ShapeDtypeStruct(blk, F32)] * 4
        + [jax.ShapeDtypeStruct(shape, F32) for shape in small_shapes],
        in_specs=[ANY_SPEC] * (1 + ns)
        + [land_tile(t) for t in range(nw)]
        + [tile(t) for _ in range(3) for t in range(nw)]
        + [whole(blk)] * 3
        + [whole(p.shape) for p in params],
        out_specs=[tile(t) for t in range(nw) for _ in range(4)]
        + [whole(blk)] * 4
        + [whole(shape) for shape in small_shapes],
        scratch_shapes=[pltpu.VMEM((N_DEV,) + smalls[i][0].shape, smalls[i][0].dtype) for i in range(ns)]
        + [pltpu.VMEM((N_CHIPS,) + blk, BF16)] * 4
        + [pltpu.SemaphoreType.DMA((1,))]
        + [pltpu.SemaphoreType.DMA((count,)) for count in PAIR_SEMS]
        + _exchange_sems(ns),
        compiler_params=pltpu.CompilerParams(
            dimension_semantics=("arbitrary",), vmem_limit_bytes=VMEM_LIMIT, has_side_effects=True
        ),
    )(gp_last, *[s[0] for s in smalls], *lands, *ws, *ms, *vs, *last, *params)


def _fwd_in(x, g1, w_in):
    S = x.shape[0]
    tm = min(512, S)

    def body(x_ref, g_ref, w_ref, hn_ref, u_ref, q_ref, k_ref, v_ref):
        xv = x_ref[...]
        hn = (xv * _rstd(xv) * g_ref[...]).astype(BF16)
        hn_ref[...] = hn
        outs = (u_ref, q_ref, k_ref, v_ref)
        for j in range(N_DEV):
            p = _dot(hn, w_ref[j])
            cols = slice(256 * (j % 2), 256 * (j % 2 + 1))
            if j // 2 == 0:
                u_ref[:, cols] = p
            elif j // 2 == 1:
                q_ref[:, cols] = (p * (HEAD_DIM**-0.5)).astype(BF16)
            else:
                outs[j // 2][:, cols] = p.astype(BF16)

    half = pl.BlockSpec((tm, D_POOL), lambda i: (i, 0))
    return pl.pallas_call(
        body,
        name="fwd_in",
        grid=(S // tm,),
        out_shape=[
            jax.ShapeDtypeStruct((S, D_MODEL), BF16),
            jax.ShapeDtypeStruct((S, D_POOL), F32),
            jax.ShapeDtypeStruct((S, D_ATTN), BF16),
            jax.ShapeDtypeStruct((S, D_ATTN), BF16),
            jax.ShapeDtypeStruct((S, D_ATTN), BF16),
        ],
        in_specs=[
            pl.BlockSpec((tm, D_MODEL), lambda i: (i, 0)),
            pl.BlockSpec((1, D_MODEL), lambda i: (0, 0)),
            pl.BlockSpec((N_DEV, D_MODEL, 256), lambda i: (0, 0, 0)),
        ],
        out_specs=[pl.BlockSpec((tm, D_MODEL), lambda i: (i, 0)), half, half, half, half],
        compiler_params=_params(("parallel",)),
    )(x, g1, w_in)


POOL_CHUNK = 256
POOL_UNROLL = 3
_POOL_W_SPEC = pl.BlockSpec((1, GROUP_DIM, GROUP_DIM), lambda g: (g, 0, 0))
_POOL_SCALE_SPEC = pl.BlockSpec((1, GROUP_DIM), lambda g: (0, g))


def _lane_sum(wide, n):
    out = wide[:, :GROUP_DIM]
    for i in range(1, n):
        out = out + wide[:, GROUP_DIM * i : GROUP_DIM * (i + 1)]
    return out


def _pool_fwd_group(u_ref, pw_ref, sc_ref, pooled_ref, y_ref, g):
    S = u_ref.shape[0]
    R = min(POOL_CHUNK, S)
    w = jnp.left_shift(jnp.int32(2), g)
    d = lax.broadcasted_iota(jnp.int32, (R, R), 0) - lax.broadcasted_iota(jnp.int32, (R, R), 1)
    b_cur = jnp.where((d >= 0) & (d < w), 1.0, 0.0).astype(BF16)
    dp = lax.broadcasted_iota(jnp.int32, (R, HALO), 0) + HALO - lax.broadcasted_iota(jnp.int32, (R, HALO), 1)
    b_prev = jnp.where(dp < w, 1.0, 0.0).astype(BF16)
    pw = pw_ref[0].astype(BF16)
    sc = sc_ref[...]

    def chunk(r0, first):
        cur = u_ref[pl.ds(r0, R), :]
        wide = _dot(b_cur, jnp.concatenate(_split3(cur), axis=1))
        if not first:
            prev = u_ref[pl.ds(pl.multiple_of(r0 - HALO, HALO), HALO), :]
            wide = wide + _dot(b_prev, jnp.concatenate(_split3(prev), axis=1))
        count = jnp.minimum(r0 + lax.broadcasted_iota(jnp.int32, (R, 1), 0) + 1, w).astype(F32)
        pooled = (_lane_sum(wide, 3) / count - cur).astype(BF16)
        pooled_ref[pl.ds(r0, R), :] = pooled
        y_ref[pl.ds(r0, R), :] = _dot(pooled, pw) * sc

    chunk(0, True)

    def rest(c, carry):
        chunk(pl.multiple_of(c * R, R), False)
        return carry

    lax.fori_loop(1, S // R, rest, 0, unroll=POOL_UNROLL)


ATTN_WINDOW = 3
ATTN_TOGETHER = 2


def _band_ones(T):
    row = lax.broadcasted_iota(jnp.int32, (T, T), 0)
    col = lax.broadcasted_iota(jnp.int32, (T, T), 1)
    after = jnp.where(row > col, 1.0, 0.0).astype(BF16)
    before = jnp.where(row < col, 1.0, 0.0).astype(BF16)
    return jnp.concatenate([after, after], axis=0), jnp.concatenate([before, before], axis=0), col < row


def _log1m(z, mask):
    sp = jnp.log(1.0 + jnp.exp(-jnp.abs(z)))
    l1m = -jnp.maximum(z, 0.0) - sp
    return (l1m if mask is None else jnp.where(mask, l1m, 0.0)), sp


def _log_gates_of(jobs, after2):
    zs = [_dot_nt(qh, kb) for qh, kb, _ in jobs]
    terms = []
    for z, (_, _, mask) in zip(zs, jobs):
        l1m, sp = _log1m(z, mask)
        terms.append(
            (jnp.minimum(z, 0.0) - sp, jnp.concatenate(_split2(l1m), axis=1), jnp.sum(l1m, axis=1, keepdims=True))
        )
    return [(ls, _dot(split, after2), total) for ls, split, total in terms]


def _log_gates(qhs, kbs, masks, after2):
    flat = _log_gates_of([(qh, kb, mask) for kb, mask in zip(kbs, masks) for qh in qhs], after2)
    return [flat[len(qhs) * b : len(qhs) * (b + 1)] for b in range(len(kbs))]


def _tile_start(index, T):
    return index * T if isinstance(index, int) else pl.multiple_of(index * T, T)


def _weights(ls, tail, carry, mask):
    a = jnp.exp(ls + tail + carry)
    return a if mask is None else jnp.where(mask, a, 0.0)


def _reaches(carry):
    return jnp.max(carry) > EXP_UNDERFLOW


def _mixers_fwd(q, k, v, u, pool_w, scale, shards):
    S = q.shape[0]
    T = ATTN_TILE
    nq = S // T
    n = len(shards)
    steps = D_ATTN // LANES

    def body(q_ref, k_ref, v_ref, u_ref, pw_ref, sc_ref, *rest):
        o_ref, pooled_ref, yp_ref = rest[n : n + 3]
        step = pl.program_id(0)

        def gather():
            return _two_level_gather(rest[:n], rest[n + 3 : 2 * n + 3], rest[2 * n + 3 :])

        @pl.when(step == 0)
        def _():
            gather()[0]()

        _pool_fwd_group(u_ref, pw_ref, sc_ref, pooled_ref, yp_ref, step)

        lane = lax.broadcasted_iota(jnp.int32, (T, LANES), 1)
        heads = (lane < HEAD_DIM, lane >= HEAD_DIM)
        after2, _, causal = _band_ones(T)
        zero = jnp.zeros((T, 1), F32)

        def key_tile(k0):
            vb = v_ref[pl.ds(k0, T), :]
            return k_ref[pl.ds(k0, T), :], jnp.concatenate([jnp.where(m, vb, jnp.zeros_like(vb)) for m in heads], axis=0)

        def q_tiles(qis, nb):
            queries, jobs = [], []
            for qi in qis:
                q0 = _tile_start(qi, T)
                qb = q_ref[pl.ds(q0, T), :]
                qhs = [jnp.where(m, qb, jnp.zeros_like(qb)) for m in heads]
                tiles = [key_tile(_tile_start(qi - b, T)) for b in range(nb)]
                queries.append((q0, qhs, [v_rows for _, v_rows in tiles]))
                jobs += [(qh, kb, causal if b == 0 else None) for b, (kb, _) in enumerate(tiles) for qh in qhs]
            gates = _log_gates_of(jobs, after2)
            states = []
            for i, (q0, qhs, values) in enumerate(queries):
                carries, probs = [zero, zero], []
                for b in range(nb):
                    for h in range(2):
                        ls, tail, total = gates[2 * nb * i + 2 * b + h]
                        probs.append(_weights(ls, tail, carries[h], causal if b == 0 else None).astype(BF16))
                        carries[h] = carries[h] + total
                states.append((carries, _dot(jnp.concatenate(probs, axis=1), jnp.concatenate(values, axis=0))))

            def more(st):
                return (st[0] > 0) & _reaches(jnp.maximum(st[1], st[2]))

            for qi, (q0, qhs, _), (carries, acc) in zip(qis, queries, states):

                def k_step(st, qhs=qhs):
                    kj = st[0] - 1
                    kb, v_rows = key_tile(pl.multiple_of(kj * T, T))
                    new, probs = [], []
                    for (ls, tail, total), carry in zip(_log_gates(qhs, [kb], [None], after2)[0], st[1:3]):
                        probs.append(_weights(ls, tail, carry, None).astype(BF16))
                        new.append(carry + total)
                    return kj, new[0], new[1], st[3] + _dot(jnp.concatenate(probs, axis=1), v_rows)

                if not isinstance(qi, int):
                    acc = lax.while_loop(more, k_step, (qi - (nb - 1), carries[0], carries[1], acc))[3]
                o_ref[pl.ds(q0, T), :] = acc

        head_tiles = min(ATTN_WINDOW - 1, nq)
        for qi in range(head_tiles):
            q_tiles([qi], qi + 1)
        assert (nq - head_tiles) % ATTN_TOGETHER == 0

        def q_loop(i, carry):
            first = head_tiles + ATTN_TOGETHER * i
            q_tiles([first + j for j in range(ATTN_TOGETHER)], ATTN_WINDOW)
            return carry

        lax.fori_loop(0, (nq - head_tiles) // ATTN_TOGETHER, q_loop, 0)

        @pl.when(step == steps - 1)
        def _():
            _, forward, finish = gather()
            forward()
            finish()

    blk = pl.BlockSpec((S, LANES), lambda p: (0, p))
    return pl.pallas_call(
        body,
        name="mixers_fwd",
        grid=(steps,),
        out_shape=[
            jax.ShapeDtypeStruct((S, D_ATTN), F32),
            jax.ShapeDtypeStruct((S, D_POOL), BF16),
            jax.ShapeDtypeStruct((S, D_POOL), F32),
        ]
        + _exchange_shapes(shards, True),
        in_specs=[blk, blk, blk, blk, _POOL_W_SPEC, _POOL_SCALE_SPEC] + [ANY_SPEC] * n,
        out_specs=[blk, blk, blk] + [ANY_SPEC] * n,
        scratch_shapes=_exchange_sems(n),
        compiler_params=pltpu.CompilerParams(
            dimension_semantics=("arbitrary",), vmem_limit_bytes=VMEM_LIMIT, has_side_effects=True
        ),
    )(q, k, v, u, pool_w, scale, *shards)


SG_ROWS = 8


def _mlp_fwd_bwd(x, y_pool, y_attn, target, g_pool, g_attn, g2, gf, w_out, w_up, w_down):
    S = x.shape[0]
    tm = min(256, S)
    fc = D_FF // N_DEV

    def body(x_ref, yp_ref, ya_ref, t_ref, gp_ref, ga_ref, g2_ref, gf_ref, wo_hbm, wu_hbm, wd_hbm,
             mixed_ref, hn2_ref, act_ref, dup_ref, dh2_ref, dh1_ref, dh1b_ref, dyp_ref, dya_ref, sg_ref,
             wo, wu, wd, up_s, sems):
        @pl.when(pl.program_id(0) == 0)
        def _():
            copies = [
                pltpu.make_async_copy(wo_hbm, wo, sems.at[0]),
                pltpu.make_async_copy(wu_hbm, wu, sems.at[1]),
                pltpu.make_async_copy(wd_hbm, wd, sems.at[2]),
            ]
            for cp in copies:
                cp.start()
            for cp in copies:
                cp.wait()
            sg_ref[...] = jnp.zeros_like(sg_ref)

        gp, ga, g2v, gfv = gp_ref[...], ga_ref[...], g2_ref[...], gf_ref[...]
        yp, ya = yp_ref[...], ya_ref[...]
        rp, ra = _rstd(yp), _rstd(ya)
        yph, yah = yp * rp, ya * ra
        mixed = jnp.concatenate([(yph * gp).astype(BF16), (yah * ga).astype(BF16)], axis=1)
        mixed_ref[...] = mixed
        h1 = x_ref[...] + _dot(mixed, wo[...])
        r2 = _rstd(h1)
        h1h = h1 * r2
        hn2 = (h1h * g2v).astype(BF16)
        hn2_ref[...] = hn2
        h2 = h1
        for j in range(N_DEV):
            cols = slice(fc * j, fc * (j + 1))
            up = _dot(hn2, wu[j])
            up_s[:, cols] = up
            act = jnp.square(jnp.maximum(up, 0.0)).astype(BF16)
            act_ref[:, cols] = act
            h2 = h2 + _dot(act, wd[cols, :])
        rf = _rstd(h2)
        h2h = h2 * rf
        diff = h2h * gfv - t_ref[...]
        loss_rows = 0.5 * jnp.mean(diff * diff, axis=-1, keepdims=True)
        dy = diff * (1.0 / D_MODEL)
        dh2, dgf = _rms_bwd(dy, h2h, rf, gfv)
        dh2b = dh2.astype(BF16)
        dh2_ref[...] = dh2b
        dhn2 = jnp.zeros((tm, D_MODEL), F32)
        for j in range(N_DEV):
            cols = slice(fc * j, fc * (j + 1))
            dact = _dot_nt(dh2b, wd[cols, :])
            dup = (dact * (2.0 * jnp.maximum(up_s[:, cols], 0.0))).astype(BF16)
            dup_ref[:, cols] = dup
            dhn2 = dhn2 + _dot_nt(dup, wu[j])
        dh1n, dg2 = _rms_bwd(dhn2, h1h, r2, g2v)
        dh1 = dh2 + dh1n
        dh1_ref[...] = dh1
        dh1b = dh1.astype(BF16)
        dh1b_ref[...] = dh1b
        dmix = _dot_nt(dh1b, wo[...])
        dyp, dgp = _rms_bwd(dmix[:, :D_POOL], yph, rp, gp)
        dya, dga = _rms_bwd(dmix[:, D_POOL:], yah, ra, ga)
        dyp_ref[...] = dyp
        dya_ref[...] = dya
        sg_ref[0:1, :] += dgf
        sg_ref[1:2, :] += dg2
        sg_ref[2:3, :] += jnp.concatenate([dgp, dga], axis=1)
        sg_ref[3:4, :] += jnp.broadcast_to(jnp.sum(loss_rows, axis=0, keepdims=True), (1, D_MODEL))

    def tok(n):
        return pl.BlockSpec((tm, n), lambda i: (i, 0))

    def vec(n):
        return pl.BlockSpec((1, n), lambda i: (0, 0))

    any_spec = pl.BlockSpec(memory_space=pl.ANY)
    return pl.pallas_call(
        body,
        name="mlp_fwd_bwd",
        grid=(S // tm,),
        out_shape=[
            jax.ShapeDtypeStruct((S, D_MODEL), BF16),
            jax.ShapeDtypeStruct((S, D_MODEL), BF16),
            jax.ShapeDtypeStruct((S, D_FF), BF16),
            jax.ShapeDtypeStruct((S, D_FF), BF16),
            jax.ShapeDtypeStruct((S, D_MODEL), BF16),
            jax.ShapeDtypeStruct((S, D_MODEL), F32),
            jax.ShapeDtypeStruct((S, D_MODEL), BF16),
            jax.ShapeDtypeStruct((S, D_POOL), F32),
            jax.ShapeDtypeStruct((S, D_ATTN), F32),
            jax.ShapeDtypeStruct((SG_ROWS, D_MODEL), F32),
        ],
        in_specs=[tok(D_MODEL), tok(D_POOL), tok(D_ATTN), tok(D_MODEL), vec(D_POOL), vec(D_ATTN),
                  vec(D_MODEL), vec(D_MODEL), any_spec, any_spec, any_spec],
        out_specs=[tok(D_MODEL), tok(D_MODEL), tok(D_FF), tok(D_FF), tok(D_MODEL), tok(D_MODEL),
                   tok(D_MODEL), tok(D_POOL), tok(D_ATTN),
                   pl.BlockSpec((SG_ROWS, D_MODEL), lambda i: (0, 0))],
        scratch_shapes=[
            pltpu.VMEM((D_MODEL, D_MODEL), BF16),
            pltpu.VMEM((N_DEV, D_MODEL, fc), BF16),
            pltpu.VMEM((D_FF, D_MODEL), BF16),
            pltpu.VMEM((tm, D_FF), F32),
            pltpu.SemaphoreType.DMA((3,)),
        ],
        compiler_params=_params(("arbitrary",)),
    )(x, y_pool, y_attn, target, g_pool, g_attn, g2, gf, w_out, w_up, w_down)


def _wgrad(a, b, block_a, groups, name, travelling=()):
    n = len(travelling)
    S, ka = a.shape
    nb = b.shape[1]
    ts = min(1024, S)
    per = N_DEV // groups
    if block_a:
        ka //= groups
        blk = (ka // per, nb)
        a_spec = pl.BlockSpec((ts, ka), lambda g, s: (s, g))
        b_spec = pl.BlockSpec((ts, nb), lambda g, s: (s, 0))
    else:
        nb //= groups
        blk = (ka, nb // per)
        a_spec = pl.BlockSpec((ts, ka), lambda g, s: (s, 0))
        b_spec = pl.BlockSpec((ts, nb), lambda g, s: (s, g))
    steps = S // ts

    def body(a_ref, b_ref, *rest):
        o_ref, acc = rest[n], rest[2 * n + 1]
        g, s = pl.program_id(0), pl.program_id(1)

        def scatter():
            return _direct_exchange(rest[:n], rest[n + 1 : 2 * n + 1], rest[2 * n + 2 :], False)

        if n:
            @pl.when((g == 0) & (s == 0))
            def _():
                scatter()[0]()

        @pl.when(s == 0)
        def _():
            acc[...] = jnp.zeros_like(acc)

        acc[...] += _dot_tn(a_ref[...], b_ref[...])

        @pl.when(s == steps - 1)
        def _():
            for j in range(per):
                if block_a:
                    o_ref[j] = acc[blk[0] * j : blk[0] * (j + 1), :].astype(BF16)
                else:
                    o_ref[j] = acc[:, blk[1] * j : blk[1] * (j + 1)].astype(BF16)

        if n:
            @pl.when((g == groups - 1) & (s == steps - 1))
            def _():
                scatter()[1]()

    results = pl.pallas_call(
        body,
        name=name,
        grid=(groups, steps),
        out_shape=[jax.ShapeDtypeStruct((N_DEV,) + blk, BF16)] + _exchange_shapes(travelling, False),
        in_specs=[a_spec, b_spec] + [ANY_SPEC] * n,
        out_specs=[pl.BlockSpec((per,) + blk, lambda g, s: (g, 0, 0))] + [ANY_SPEC] * n,
        scratch_shapes=[pltpu.VMEM((ka, nb), F32)] + (_exchange_sems(n) if n else []),
        compiler_params=pltpu.CompilerParams(
            dimension_semantics=("arbitrary", "arbitrary"), vmem_limit_bytes=VMEM_LIMIT, has_side_effects=bool(n)
        ),
    )(a, b, *travelling)
    return results if n else results[0]


def _mixers_bwd(q, k, v, do, dyp, pooled, pool_w, scale, partials):
    S = q.shape[0]
    T = ATTN_TILE
    nq = S // T
    n = len(partials)
    steps = D_ATTN // LANES

    def body(q_ref, k_ref, v_ref, do_ref, dyp_ref, pooled_ref, pw_ref, sc_ref, *rest):
        dq_ref, dk_ref, dv_ref, du_ref, dsc_ref, dpw_ref = rest[n : n + 6]
        dk_acc, dv_acc, carry_s = rest[2 * n + 6 : 2 * n + 9]
        step = pl.program_id(0)

        def scatter():
            return _direct_exchange(rest[:n], rest[n + 6 : 2 * n + 6], rest[2 * n + 9 :], False)

        @pl.when(step == 0)
        def _():
            scatter()[0]()

        _pool_bwd_group(dyp_ref, pooled_ref, pw_ref, sc_ref, du_ref, dsc_ref, dpw_ref, step)

        dk_acc[...] = jnp.zeros_like(dk_acc)
        dv_acc[...] = jnp.zeros_like(dv_acc)
        lane = lax.broadcasted_iota(jnp.int32, (T, LANES), 1)
        heads = (lane < HEAD_DIM, lane >= HEAD_DIM)
        after2, before2, causal = _band_ones(T)
        zero = jnp.zeros((T, 1), F32)

        def key_tile(k0):
            kb = k_ref[pl.ds(k0, T), :]
            k_rows = jnp.concatenate([jnp.where(m, kb, jnp.zeros_like(kb)) for m in heads], axis=0)
            return kb, v_ref[pl.ds(k0, T), :], k_rows

        def grads(items):
            das = [[[_dot_nt(doh, vb) for doh in qs[1]] for _, vb, _ in tiles] for qs, _, tiles, _, _, _, _ in items]
            probs, gs = [], []
            for (_, _, tiles, masks, carry_in, gates, _), da in zip(items, das):
                probs.append([[_weights(gates[b][h][0], gates[b][h][1], carry_in[b][h], masks[b]) for h in range(2)]
                              for b in range(len(tiles))])
                gs.append([[probs[-1][b][h] * da[b][h] for h in range(2)] for b in range(len(tiles))])
            before = [[[_dot(jnp.concatenate(_split2(g), axis=1), before2) for g in row] for row in item] for item in gs]
            dzs, lefts = [], []
            for i, (_, _, tiles, masks, _, gates, st) in enumerate(items):
                g_left, item_dz = [st[0], st[1]], [None] * len(tiles)
                for b in reversed(range(len(tiles))):
                    item_dz[b] = []
                    for h in range(2):
                        sig = jnp.exp(gates[b][h][0])
                        dz = gs[i][b][h] * (1.0 - sig) - sig * (g_left[h] + before[i][b][h])
                        if masks[b] is not None:
                            dz = jnp.where(masks[b], dz, 0.0)
                        item_dz[b].append(dz.astype(BF16))
                        g_left[h] = g_left[h] + jnp.sum(gs[i][b][h], axis=1, keepdims=True)
                dzs.append(item_dz)
                lefts.append(g_left)
            out = []
            for (_, _, tiles, _, _, _, st), item_dz, g_left in zip(items, dzs, lefts):
                dq = st[2] + _dot(
                    jnp.concatenate([dz for row in item_dz for dz in row], axis=1),
                    jnp.concatenate([k_rows for _, _, k_rows in tiles], axis=0),
                )
                out.append((g_left[0], g_left[1], dq))
            for (qs, starts, tiles, _, _, _, _), item_dz, item_probs in zip(items, dzs, probs):
                for b in range(len(tiles)):
                    dk_acc[pl.ds(starts[b], T), :] += _dot_tn(jnp.concatenate(item_dz[b], axis=0), qs[2])
                    dv_acc[pl.ds(starts[b], T), :] += _dot_tn(
                        jnp.concatenate([a.astype(BF16) for a in item_probs[b]], axis=0), qs[3]
                    )
            return out

        def more(st):
            return (st[0] > 0) & _reaches(jnp.maximum(st[1], st[2]))

        def q_tiles(qis, nb):
            masks = [causal] + [None] * (nb - 1)
            prepared, jobs = [], []
            for qi in qis:
                q0 = _tile_start(qi, T)
                qb = q_ref[pl.ds(q0, T), :]
                dob = do_ref[pl.ds(q0, T), :]
                qhs = [jnp.where(m, qb, jnp.zeros_like(qb)) for m in heads]
                dohs = [jnp.where(m, dob, 0.0).astype(BF16) for m in heads]
                queries = (qhs, dohs, jnp.concatenate(qhs, axis=0), jnp.concatenate(dohs, axis=0))
                starts = [_tile_start(qi - b, T) for b in range(nb)]
                tiles = [key_tile(k0) for k0 in starts]
                prepared.append((q0, queries, starts, tiles))
                jobs += [(qh, kb, mask) for (kb, _, _), mask in zip(tiles, masks) for qh in qhs]
            flat = _log_gates_of(jobs, after2)
            items = []
            for i, (qi, (q0, queries, starts, tiles)) in enumerate(zip(qis, prepared)):
                gates = [flat[2 * nb * i + 2 * b : 2 * nb * i + 2 * b + 2] for b in range(nb)]
                carries, carry_in = [zero, zero], []
                for b in range(nb):
                    carry_in.append(list(carries))
                    carries = [carries[h] + gates[b][h][2] for h in range(2)]
                st = (zero, zero, jnp.zeros((T, LANES), F32))
                if not isinstance(qi, int):
                    k_left = qi - (nb - 1)
                    qhs = queries[0]

                    def right_to_left(st, qhs=qhs):
                        kj = st[0] - 1
                        carry_s[0, kj] = st[1]
                        carry_s[1, kj] = st[2]
                        kb = k_ref[pl.ds(pl.multiple_of(kj * T, T), T), :]
                        sums = [jnp.sum(_log1m(_dot_nt(qh, kb), None)[0], axis=1, keepdims=True) for qh in qhs]
                        return kj, st[1] + sums[0], st[2] + sums[1]

                    k_first = lax.while_loop(more, right_to_left, (k_left, carries[0], carries[1]))[0]

                    def left_to_right(kj, st, queries=queries):
                        k0 = pl.multiple_of(kj * T, T)
                        tile = key_tile(k0)
                        left_gates = _log_gates(queries[0], [tile[0]], [None], after2)
                        carry = [[carry_s[0, kj], carry_s[1, kj]]]
                        return grads([(queries, [k0], [tile], [None], carry, left_gates, st)])[0]

                    st = lax.fori_loop(k_first, k_left, left_to_right, st)
                items.append((queries, starts, tiles, masks, carry_in, gates, st))
            for (q0, _, _, _), st in zip(prepared, grads(items)):
                dq_ref[pl.ds(q0, T), :] = (st[2] * (HEAD_DIM**-0.5)).astype(BF16)

        head_tiles = min(ATTN_WINDOW - 1, nq)
        for qi in range(head_tiles):
            q_tiles([qi], qi + 1)
        assert (nq - head_tiles) % ATTN_TOGETHER == 0

        def q_loop(i, carry):
            first = head_tiles + ATTN_TOGETHER * i
            q_tiles([first + j for j in range(ATTN_TOGETHER)], ATTN_WINDOW)
            return carry

        lax.fori_loop(0, (nq - head_tiles) // ATTN_TOGETHER, q_loop, 0)
        dk_ref[...] = dk_acc[...].astype(BF16)
        dv_ref[...] = dv_acc[...].astype(BF16)

        @pl.when(step == steps - 1)
        def _():
            scatter()[1]()

    blk = pl.BlockSpec((S, LANES), lambda p: (0, p))
    sds = jax.ShapeDtypeStruct((S, D_ATTN), BF16)
    return pl.pallas_call(
        body,
        name="mixers_bwd",
        grid=(steps,),
        out_shape=[
            sds,
            sds,
            sds,
            jax.ShapeDtypeStruct((S, D_POOL), BF16),
            jax.ShapeDtypeStruct((8, D_POOL), F32),
            jax.ShapeDtypeStruct((N_GROUPS, GROUP_DIM, GROUP_DIM), BF16),
        ]
        + _exchange_shapes(partials, False),
        in_specs=[blk] * 6 + [_POOL_W_SPEC, _POOL_SCALE_SPEC] + [ANY_SPEC] * n,
        out_specs=[blk] * 4 + [pl.BlockSpec((8, GROUP_DIM), lambda p: (0, p)), _POOL_W_SPEC] + [ANY_SPEC] * n,
        scratch_shapes=[pltpu.VMEM((S, LANES), F32), pltpu.VMEM((S, LANES), F32), pltpu.VMEM((2, nq, T, 1), F32)]
        + _exchange_sems(n),
        compiler_params=pltpu.CompilerParams(
            dimension_semantics=("arbitrary",), vmem_limit_bytes=VMEM_LIMIT, has_side_effects=True
        ),
    )(q, k, v, do, dyp, pooled, pool_w, scale, *partials)


def _pool_bwd_group(dy_ref, pooled_ref, pw_ref, sc_ref, du_ref, dsc_ref, dpw_ref, g):
    S = dy_ref.shape[0]
    R = min(POOL_CHUNK, S)
    nc = S // R
    w = jnp.left_shift(jnp.int32(2), g)
    pw = pw_ref[0].astype(BF16)
    sc = sc_ref[...]
    d = lax.broadcasted_iota(jnp.int32, (R, R), 1) - lax.broadcasted_iota(jnp.int32, (R, R), 0)
    bt_cur = jnp.where((d >= 0) & (d < w), 1.0, 0.0).astype(BF16)
    dn = lax.broadcasted_iota(jnp.int32, (R, HALO), 1) + R - lax.broadcasted_iota(jnp.int32, (R, HALO), 0)
    bt_next = jnp.where(dn < w, 1.0, 0.0).astype(BF16)

    def per_count(dpl, r0):
        n = dpl.shape[0]
        count = jnp.minimum(r0 + lax.broadcasted_iota(jnp.int32, (n, 1), 0) + 1, w).astype(F32)
        return jnp.concatenate(_split2(dpl / count), axis=1)

    def chunk(r0, last, sums):
        dyv = dy_ref[pl.ds(r0, R), :]
        pooled = pooled_ref[pl.ds(r0, R), :]
        dmapped = (dyv * sc).astype(BF16)
        dsc = sums[0] + jnp.sum(dyv * _dot(pooled, pw), axis=0, keepdims=True)
        dpw = sums[1] + _dot_tn(pooled, dmapped)
        dpl = _dot_nt(dmapped, pw)
        wide = _dot(bt_cur, per_count(dpl, r0))
        if not last:
            nxt = pl.multiple_of(r0 + R, R)
            dpl_next = _dot_nt((dy_ref[pl.ds(nxt, HALO), :] * sc).astype(BF16), pw)
            wide = wide + _dot(bt_next, per_count(dpl_next, nxt))
        du_ref[pl.ds(r0, R), :] = (_lane_sum(wide, 2) - dpl).astype(BF16)
        return dsc, dpw

    sums = (jnp.zeros((1, GROUP_DIM), F32), jnp.zeros((GROUP_DIM, GROUP_DIM), F32))
    sums = lax.fori_loop(
        0, nc - 1, lambda c, s: chunk(pl.multiple_of(c * R, R), False, s), sums, unroll=POOL_UNROLL
    )
    dsc, dpw = chunk((nc - 1) * R, True, sums)
    dsc_ref[...] = jnp.zeros_like(dsc_ref)
    dsc_ref[0:1, :] = dsc
    dpw_ref[0] = dpw.astype(dpw_ref.dtype)


def _bwd_in(du, dq, dk, dv, w_in, x, dh1, g1):
    S = x.shape[0]
    tm = min(512, S)

    def body(du_ref, dq_ref, dk_ref, dv_ref, w_ref, x_ref, dh1_ref, g_ref, dx_ref, dproj_ref, dg_ref):
        @pl.when(pl.program_id(0) == 0)
        def _():
            dg_ref[...] = jnp.zeros_like(dg_ref)

        parts = (du_ref[...], dq_ref[...], dk_ref[...], dv_ref[...])
        dhn = jnp.zeros((tm, D_MODEL), F32)
        for j in range(N_DEV):
            piece = parts[j // 2][:, 256 * (j % 2) : 256 * (j % 2 + 1)]
            dproj_ref[:, 256 * j : 256 * (j + 1)] = piece
            dhn = dhn + _dot_nt(piece, w_ref[j])
        xv = x_ref[...]
        r = _rstd(xv)
        dxn, dg = _rms_bwd(dhn, xv * r, r, g_ref[...])
        dx_ref[...] = dh1_ref[...] + dxn
        dg_ref[0:1, :] += dg

    half = pl.BlockSpec((tm, D_POOL), lambda i: (i, 0))
    full = pl.BlockSpec((tm, D_MODEL), lambda i: (i, 0))
    return pl.pallas_call(
        body,
        name="bwd_in",
        grid=(S // tm,),
        out_shape=[
            jax.ShapeDtypeStruct((S, D_MODEL), F32),
            jax.ShapeDtypeStruct((S, D_IN_PROJ), BF16),
            jax.ShapeDtypeStruct((8, D_MODEL), F32),
        ],
        in_specs=[half, half, half, half,
                  pl.BlockSpec((N_DEV, D_MODEL, 256), lambda i: (0, 0, 0)),
                  full, full, pl.BlockSpec((1, D_MODEL), lambda i: (0, 0))],
        out_specs=[full, pl.BlockSpec((tm, D_IN_PROJ), lambda i: (i, 0)),
                   pl.BlockSpec((8, D_MODEL), lambda i: (0, 0))],
        compiler_params=_params(("arbitrary",)),
    )(du, dq, dk, dv, w_in, x, dh1, g1)


def kernel(x, norm1_g, w_in, pool_w, pool_scale, pool_out_g, attn_out_g, w_out, norm2_g, w_up, w_down, final_g, loss_target, m_norm1_g, m_w_in, m_pool_w, m_pool_scale, m_pool_out_g, m_attn_out_g, m_w_out, m_norm2_g, m_w_up, m_w_down, m_final_g, v_norm1_g, v_w_in, v_pool_w, v_pool_scale, v_pool_out_g, v_attn_out_g, v_w_out, v_norm2_g, v_w_up, v_w_down, v_final_g):
    S = x.shape[1]
    xs = x.reshape(S, D_MODEL)
    tgt = loss_target.reshape(S, D_MODEL)
    row = lambda a: a.reshape(1, -1)

    (w_in_g,) = _exchange([w_in.astype(BF16)], True, "gather_w_in")
    hn, u_pool, q, k, v = _fwd_in(xs, row(norm1_g), w_in_g)
    y_attn, pooled, y_pool, w_out_g, w_up_g, w_down_g = _mixers_fwd(
        q, k, v, u_pool, pool_w, row(pool_scale), [w_out.astype(BF16), w_up.astype(BF16), w_down.astype(BF16)]
    )
    w_out_full = w_out_g.reshape(D_MODEL, D_MODEL)
    w_down_full = w_down_g.reshape(D_FF, D_MODEL)
    mixed, hn2, act, dup, dh2b, dh1, dh1b, dyp, dya, sg = _mlp_fwd_bwd(
        xs, y_pool, y_attn, tgt, row(pool_out_g), row(attn_out_g), row(norm2_g), row(final_g),
        w_out_full, w_up_g, w_down_full,
    )
    gp_down = _wgrad(act, dh2b, True, 2, "wgrad_down")
    gp_up = _wgrad(hn2, dup, False, 2, "wgrad_up")
    gp_out = _wgrad(mixed, dh1b, True, 1, "wgrad_out")
    dq, dk, dv, du, dsc, dpw, land_up, land_down = _mixers_bwd(
        q, k, v, dya, dyp, pooled, pool_w, row(pool_scale), [gp_up, gp_down]
    )
    dx, dproj, dg1 = _bwd_in(du, dq, dk, dv, w_in_g, xs, dh1, row(norm1_g))
    gp_in, land_out = _wgrad(hn, dproj, False, 1, "wgrad_in", [gp_out])

    def vector(w, m, v, at_row, at_col=0):
        return row(w), row(m), row(v), (at_row, at_col)

    flat_pool = lambda a: a.reshape(N_GROUPS * GROUP_DIM, GROUP_DIM)
    smalls = [
        (sg, [
            vector(final_g, m_final_g, v_final_g, 0),
            vector(norm2_g, m_norm2_g, v_norm2_g, 1),
            vector(pool_out_g, m_pool_out_g, v_pool_out_g, 2),
            vector(attn_out_g, m_attn_out_g, v_attn_out_g, 2, D_POOL),
            (None, None, None, (3, 0, 1, LANES)),
        ]),
        (dg1, [vector(norm1_g, m_norm1_g, v_norm1_g, 0)]),
        (dsc, [vector(pool_scale, m_pool_scale, v_pool_scale, 0)]),
        (flat_pool(dpw), [(flat_pool(pool_w), flat_pool(m_pool_w), flat_pool(v_pool_w), None)]),
    ]
    tail = _reduce_adam_tail(
        [land_out, land_up, land_down], [w_out, w_up, w_down], [m_w_out, m_w_up, m_w_down],
        [v_w_out, v_w_up, v_w_down], gp_in, (w_in, m_w_in, v_w_in), smalls,
    )
    results = {name: tail[4 * t : 4 * t + 4] for t, name in enumerate(("w_out", "w_up", "w_down", "w_in"))}
    shapes = dict(final_g=final_g.shape, norm2_g=norm2_g.shape, pool_out_g=pool_out_g.shape,
                  attn_out_g=attn_out_g.shape, norm1_g=norm1_g.shape, pool_scale=pool_scale.shape, pool_w=pool_w.shape)
    at = 16
    for name in ("final_g", "norm2_g", "pool_out_g", "attn_out_g", "loss", "norm1_g", "pool_scale", "pool_w"):
        if name == "loss":
            loss = tail[at][0, 0]
            at += 1
            continue
        results[name] = [a.reshape(shapes[name]) for a in tail[at : at + 4]]
        at += 4
    order = ("norm1_g", "w_in", "pool_w", "pool_scale", "pool_out_g", "attn_out_g", "w_out", "norm2_g", "w_up",
             "w_down", "final_g")
    outs = [loss, dx.reshape(1, S, D_MODEL)]
    for i in range(4):
        outs += [results[name][i] for name in order]
    return tuple(outs)
```

```python
import functools

import jax
import jax.numpy as jnp
from jax import lax
from jax.experimental import pallas as pl
from jax.experimental.pallas import tpu as pltpu

F32 = jnp.float32
BF16 = jnp.bfloat16
MESH = pl.DeviceIdType.MESH

N_DEV = 8
D_MODEL = 1024
D_POOL = 512
D_ATTN = 512
N_GROUPS = 4
GROUP_DIM = 128
HEAD_DIM = 64
D_FF = 4096
D_IN_PROJ = 2048
EPS = 1e-6
HALO = 16
ATTN_TILE = 128
LANES = 128
EXP_UNDERFLOW = -104.0

ADAM_LR = 0.001
ADAM_B1 = 0.9
ADAM_B2 = 0.999
ADAM_EPS = 1e-08
ADAM_WD = 0.01
ADAM_STEP = 10

VMEM_LIMIT = 56 * 1024 * 1024


def _params(semantics=None, vmem=VMEM_LIMIT):
    return pltpu.CompilerParams(dimension_semantics=semantics, vmem_limit_bytes=vmem)


def _dot(a, b):
    return jnp.dot(a, b, preferred_element_type=F32)


def _dot_nt(a, b):
    return lax.dot_general(a, b, (((1,), (1,)), ((), ())), preferred_element_type=F32)


def _dot_tn(a, b):
    return lax.dot_general(a, b, (((0,), (0,)), ((), ())), preferred_element_type=F32)


def _split2(x):
    hi = x.astype(BF16)
    lo = (x - hi.astype(F32)).astype(BF16)
    return hi, lo


def _split3(x):
    hi = x.astype(BF16)
    r = x - hi.astype(F32)
    mid = r.astype(BF16)
    lo = (r - mid.astype(F32)).astype(BF16)
    return hi, mid, lo


def _rstd(h):
    return lax.rsqrt(jnp.mean(h * h, axis=-1, keepdims=True) + EPS)


def _rms_bwd(dout, hhat, r, g):
    dg = jnp.sum(dout * hhat, axis=0, keepdims=True)
    dxh = dout * g
    dh = r * (dxh - hhat * jnp.mean(dxh * hhat, axis=-1, keepdims=True))
    return dh, dg


def _my_index():
    return 4 * lax.axis_index("x") + 2 * lax.axis_index("y") + lax.axis_index("c")


def _peer(k):
    x, y, c = lax.axis_index("x"), lax.axis_index("y"), lax.axis_index("c")
    px = 1 - x if (k >> 2) & 1 else x
    py = 1 - y if (k >> 1) & 1 else y
    pc = 1 - c if k & 1 else c
    return (px, py, pc), 4 * px + 2 * py + pc


N_PEERS = N_DEV - 1
ANY_SPEC = pl.BlockSpec(memory_space=pl.ANY)


def _exchange_sems(n):
    return [
        pltpu.SemaphoreType.DMA((n * N_PEERS,)),
        pltpu.SemaphoreType.DMA((n * N_PEERS,)),
        pltpu.SemaphoreType.DMA((n,)),
    ]


def _exchange_shapes(blocks, gather):
    if gather:
        return [jax.ShapeDtypeStruct((N_DEV,) + b.shape, b.dtype) for b in blocks]
    return [jax.ShapeDtypeStruct(b.shape, b.dtype) for b in blocks]


def _direct_exchange(ins, outs, sems, gather):
    send_sems, recv_sems, local_sems = sems
    me = _my_index()
    own, sends, recvs = [], [], []
    for t in range(len(ins)):
        own.append(pltpu.make_async_copy(ins[t] if gather else ins[t].at[me], outs[t].at[me], local_sems.at[t]))
        for k in range(1, N_DEV):
            peer, peer_idx = _peer(k)
            src = ins[t] if gather else ins[t].at[peer_idx]
            for dst, bucket in ((outs[t].at[me], sends), (outs[t].at[peer_idx], recvs)):
                bucket.append(
                    pltpu.make_async_remote_copy(
                        src_ref=src,
                        dst_ref=dst,
                        send_sem=send_sems.at[t * N_PEERS + k - 1],
                        recv_sem=recv_sems.at[t * N_PEERS + k - 1],
                        device_id=peer,
                        device_id_type=MESH,
                    )
                )

    def start():
        for cp in own + sends:
            cp.start()

    def finish():
        for cp in recvs:
            cp.wait_recv()
        for cp in sends:
            cp.wait_send()
        for cp in own:
            cp.wait()

    return start, finish


def _two_level_gather(ins, outs, sems):
    send_sems, recv_sems, local_sems = sems
    x, y, c = lax.axis_index("x"), lax.axis_index("y"), lax.axis_index("c")
    me, sibling = (x, y, c), (x, y, 1 - c)
    chips = [(1 - x, y), (x, 1 - y), (1 - x, 1 - y)]

    def copy(t, k, block, to, from_input=False):
        slot = outs[t].at[4 * block[0] + 2 * block[1] + block[2]]
        return pltpu.make_async_remote_copy(
            src_ref=ins[t] if from_input else slot,
            dst_ref=slot,
            send_sem=send_sems.at[t * N_PEERS + k],
            recv_sem=recv_sems.at[t * N_PEERS + k],
            device_id=to,
            device_id_type=MESH,
        )

    arrays = range(len(ins))
    own = [pltpu.make_async_copy(ins[t], outs[t].at[4 * x + 2 * y + c], local_sems.at[t]) for t in arrays]
    first = [copy(t, 1 + j, me, (*chip, c), True) for t in arrays for j, chip in enumerate(chips)]
    first += [copy(t, 0, me, sibling, True) for t in arrays]
    passed = [(copy(t, 1 + j, (*chip, c), me), copy(t, 4 + j, (*chip, c), sibling))
              for t in arrays for j, chip in enumerate(chips)]
    last = [copy(t, 0, sibling, me) for t in arrays]
    last += [copy(t, 4 + j, (*chip, 1 - c), me) for t in arrays for j, chip in enumerate(chips)]

    def start():
        for cp in own + first:
            cp.start()

    def forward():
        for landed, onward in passed:
            landed.wait_recv()
            onward.start()

    def finish():
        for cp in last:
            cp.wait_recv()
        for cp in first + [onward for _, onward in passed]:
            cp.wait_send()
        for cp in own:
            cp.wait()

    return start, forward, finish


N_CHIPS = 4
PAIR_SEMS = (N_CHIPS, N_CHIPS, N_CHIPS - 1, N_CHIPS - 1)


def _pair_reduce_scatter(gp_hbm, own, pair, summed, land, local_sem, sems):
    d2d_send, d2d_recv, ici_send, ici_recv = sems
    x, y, c = lax.axis_index("x"), lax.axis_index("y"), lax.axis_index("c")
    my_chip = 2 * x + y
    sibling = (x, y, 1 - c)
    chips = [(1 - x, y), (x, 1 - y), (1 - x, 1 - y)]
    local = [pltpu.make_async_copy(gp_hbm.at[2 * j + c], own.at[j], local_sem.at[0]) for j in range(N_CHIPS)]
    to_sibling = [
        pltpu.make_async_remote_copy(
            src_ref=gp_hbm.at[2 * j + 1 - c], dst_ref=pair.at[j], send_sem=d2d_send.at[j], recv_sem=d2d_recv.at[j],
            device_id=sibling, device_id_type=MESH,
        )
        for j in range(N_CHIPS)
    ]
    to_chips, from_chips = [], []
    for r, (px, py) in enumerate(chips):
        for dst, bucket in ((land.at[my_chip], to_chips), (land.at[2 * px + py], from_chips)):
            bucket.append(
                pltpu.make_async_remote_copy(
                    src_ref=summed.at[2 * px + py], dst_ref=dst, send_sem=ici_send.at[r], recv_sem=ici_recv.at[r],
                    device_id=(px, py, c), device_id_type=MESH,
                )
            )

    def start():
        for cp in local + to_sibling:
            cp.start()

    def middle():
        for cp in to_sibling:
            cp.wait_recv()
        pltpu.make_async_copy(gp_hbm.at[pl.ds(0, N_CHIPS)], own, local_sem.at[0]).wait()
        for j in range(N_CHIPS):
            summed[j] = (own[j].astype(F32) + pair[j].astype(F32)).astype(BF16)
        for cp in to_chips:
            cp.start()

    def finish():
        for cp in from_chips:
            cp.wait_recv()
        for cp in to_chips + to_sibling:
            cp.wait_send()
        land[my_chip] = summed[my_chip]

    return start, middle, finish


def _exchange(blocks, gather, name):
    n = len(blocks)

    def body(*refs):
        if gather:
            stages = _two_level_gather(refs[:n], refs[n : 2 * n], refs[2 * n :])
        else:
            stages = _direct_exchange(refs[:n], refs[n : 2 * n], refs[2 * n :], False)
        for stage in stages:
            stage()

    return pl.pallas_call(
        body,
        name=name,
        out_shape=_exchange_shapes(blocks, gather),
        in_specs=[ANY_SPEC] * n,
        out_specs=[ANY_SPEC] * n,
        scratch_shapes=_exchange_sems(n),
        compiler_params=pltpu.CompilerParams(has_side_effects=True),
    )(*blocks)


def _adam(w, g, m, v):
    m2 = ADAM_B1 * m + (1.0 - ADAM_B1) * g
    v2 = ADAM_B2 * v + (1.0 - ADAM_B2) * jnp.square(g)
    m_hat = m2 / (1.0 - ADAM_B1**ADAM_STEP)
    v_hat = v2 / (1.0 - ADAM_B2**ADAM_STEP)
    delta = -ADAM_LR * (m_hat / (jnp.sqrt(v_hat) + ADAM_EPS) + ADAM_WD * w)
    return delta, m2, v2


TAIL_STEPS = 8


def _reduce_adam_tail(lands, ws, ms, vs, gp_last, last, smalls):
    nw, ns = len(ws), len(smalls)
    tiles = [w.shape[0] // TAIL_STEPS for w in ws]
    blk = gp_last.shape[1:]
    entries = [(i, e) for i, (_, es) in enumerate(smalls) for e in es]
    n_params = 3 * sum(e[0] is not None for _, e in entries)
    n_small_out = sum(4 if e[0] is not None else 1 for _, e in entries)

    def body(*refs):
        gp_hbm = refs[0]
        part_hbm = refs[1 : 1 + ns]
        refs = refs[1 + ns :]
        land_refs = refs[:nw]
        w_refs, m_refs, v_refs = (refs[nw * (i + 1) : nw * (i + 2)] for i in range(3))
        last_refs = refs[4 * nw : 4 * nw + 3]
        refs = refs[4 * nw + 3 :]
        param_refs = refs[:n_params]
        outs = refs[n_params:]
        big_out = outs[: 4 * nw]
        last_out = outs[4 * nw : 4 * nw + 4]
        small_out = outs[4 * nw + 4 : 4 * nw + 4 + n_small_out]
        scratch = outs[4 * nw + 4 + n_small_out :]
        small_land = scratch[:ns]
        pair_bufs = scratch[ns : ns + 4]
        local_sem = scratch[ns + 4]
        pair_sems = scratch[ns + 5 : ns + 9]
        small_sems = scratch[ns + 9 :]
        step = pl.program_id(0)

        def pair():
            return _pair_reduce_scatter(gp_hbm, *pair_bufs, local_sem, pair_sems)

        def small():
            return _direct_exchange(part_hbm, small_land, small_sems, True)

        @pl.when(step == 0)
        def _():
            pair()[0]()
            small()[0]()

        @pl.when(step == 1)
        def _():
            pair()[1]()

        for t in range(nw):
            g = land_refs[t][0].astype(F32)
            for s in range(1, N_DEV):
                g = g + land_refs[t][s].astype(F32)
            for ref, val in zip(big_out[4 * t : 4 * t + 4], (g,) + _adam(w_refs[t][...], g, m_refs[t][...], v_refs[t][...])):
                ref[...] = val

        @pl.when(step == TAIL_STEPS - 1)
        def _():
            pair()[2]()
            small()[1]()
            land = pair_bufs[3]
            g = land[0].astype(F32)
            for chip in range(1, N_CHIPS):
                g = g + land[chip].astype(F32)
            for ref, val in zip(last_out, (g,) + _adam(last_refs[0][...], g, last_refs[1][...], last_refs[2][...])):
                ref[...] = val
            sums = []
            for i in range(ns):
                g = small_land[i][0].astype(F32)
                for s in range(1, N_DEV):
                    g = g + small_land[i][s].astype(F32)
                sums.append(g)
            params, results = list(param_refs), list(small_out)
            for i, (w, _, _, where) in entries:
                g = sums[i]
                if where is not None:
                    shape = w.shape if w is not None else where[2:]
                    g = g[where[0] : where[0] + shape[0], where[1] : where[1] + shape[1]]
                if w is None:
                    results.pop(0)[...] = g
                    continue
                w_ref, m_ref, v_ref = params[:3]
                del params[:3]
                for val in (g,) + _adam(w_ref[...], g, m_ref[...], v_ref[...]):
                    results.pop(0)[...] = val

    def tile(t):
        return pl.BlockSpec((tiles[t], ws[t].shape[1]), lambda i: (i, 0))

    def land_tile(t):
        return pl.BlockSpec((N_DEV, tiles[t], ws[t].shape[1]), lambda i: (0, i, 0))

    def whole(shape):
        return pl.BlockSpec(shape, lambda step: (0, 0))

    big_sds = [jax.ShapeDtypeStruct(ws[t].shape, F32) for t in range(nw) for _ in range(4)]
    params, small_shapes = [], []
    for _, (w, m, v, where) in entries:
        if w is None:
            small_shapes.append(tuple(where[2:]))
        else:
            params += [w, m, v]
            small_shapes += [w.shape] * 4
    return pl.pallas_call(
        body,
        name="reduce_adam_tail",
        grid=(TAIL_STEPS,),
        out_shape=big_sds
        + [jax.ShapeDtypeStruct(blk, F32)] * 4
        + [jax.ShapeDtypeStruct(shape, F32) for shape in small_shapes],
        in_specs=[ANY_SPEC] * (1 + ns)
        + [land_tile(t) for t in range(nw)]
        + [tile(t) for _ in range(3) for t in range(nw)]
        + [whole(blk)] * 3
        + [whole(p.shape) for p in params],
        out_specs=[tile(t) for t in range(nw) for _ in range(4)]
        + [whole(blk)] * 4
        + [whole(shape) for shape in small_shapes],
        scratch_shapes=[pltpu.VMEM((N_DEV,) + smalls[i][0].shape, smalls[i][0].dtype) for i in range(ns)]
        + [pltpu.VMEM((N_CHIPS,) + blk, BF16)] * 4
        + [pltpu.SemaphoreType.DMA((1,))]
        + [pltpu.SemaphoreType.DMA((count,)) for count in PAIR_SEMS]
        + _exchange_sems(ns),
        compiler_params=pltpu.CompilerParams(
            dimension_semantics=("arbitrary",), vmem_limit_bytes=VMEM_LIMIT, has_side_effects=True
        ),
    )(gp_last, *[s[0] for s in smalls], *lands, *ws, *ms, *vs, *last, *params)


def _fwd_in(x, g1, w_in):
    S = x.shape[0]
    tm = min(512, S)

    def body(x_ref, g_ref, w_ref, hn_ref, u_ref, q_ref, k_ref, v_ref):
        xv = x_ref[...]
        hn = (xv * _rstd(xv) * g_ref[...]).astype(BF16)
        hn_ref[...] = hn
        outs = (u_ref, q_ref, k_ref, v_ref)
        for j in range(N_DEV):
            p = _dot(hn, w_ref[j])
            cols = slice(256 * (j % 2), 256 * (j % 2 + 1))
            if j // 2 == 0:
                u_ref[:, cols] = p
            elif j // 2 == 1:
                q_ref[:, cols] = (p * (HEAD_DIM**-0.5)).astype(BF16)
            else:
                outs[j // 2][:, cols] = p.astype(BF16)

    half = pl.BlockSpec((tm, D_POOL), lambda i: (i, 0))
    return pl.pallas_call(
        body,
        name="fwd_in",
        grid=(S // tm,),
        out_shape=[
            jax.ShapeDtypeStruct((S, D_MODEL), BF16),
            jax.ShapeDtypeStruct((S, D_POOL), F32),
            jax.ShapeDtypeStruct((S, D_ATTN), BF16),
            jax.ShapeDtypeStruct((S, D_ATTN), BF16),
            jax.ShapeDtypeStruct((S, D_ATTN), BF16),
        ],
        in_specs=[
            pl.BlockSpec((tm, D_MODEL), lambda i: (i, 0)),
            pl.BlockSpec((1, D_MODEL), lambda i: (0, 0)),
            pl.BlockSpec((N_DEV, D_MODEL, 256), lambda i: (0, 0, 0)),
        ],
        out_specs=[pl.BlockSpec((tm, D_MODEL), lambda i: (i, 0)), half, half, half, half],
        compiler_params=_params(("parallel",)),
    )(x, g1, w_in)


POOL_CHUNK = 256
POOL_UNROLL = 3
_POOL_W_SPEC = pl.BlockSpec((1, GROUP_DIM, GROUP_DIM), lambda g: (g, 0, 0))
_POOL_SCALE_SPEC = pl.BlockSpec((1, GROUP_DIM), lambda g: (0, g))


def _lane_sum(wide, n):
    out = wide[:, :GROUP_DIM]
    for i in range(1, n):
        out = out + wide[:, GROUP_DIM * i : GROUP_DIM * (i + 1)]
    return out


def _pool_fwd_group(u_ref, pw_ref, sc_ref, pooled_ref, y_ref, g):
    S = u_ref.shape[0]
    R = min(POOL_CHUNK, S)
    w = jnp.left_shift(jnp.int32(2), g)
    d = lax.broadcasted_iota(jnp.int32, (R, R), 0) - lax.broadcasted_iota(jnp.int32, (R, R), 1)
    b_cur = jnp.where((d >= 0) & (d < w), 1.0, 0.0).astype(BF16)
    dp = lax.broadcasted_iota(jnp.int32, (R, HALO), 0) + HALO - lax.broadcasted_iota(jnp.int32, (R, HALO), 1)
    b_prev = jnp.where(dp < w, 1.0, 0.0).astype(BF16)
    pw = pw_ref[0].astype(BF16)
    sc = sc_ref[...]

    def chunk(r0, first):
        cur = u_ref[pl.ds(r0, R), :]
        wide = _dot(b_cur, jnp.concatenate(_split3(cur), axis=1))
        if not first:
            prev = u_ref[pl.ds(pl.multiple_of(r0 - HALO, HALO), HALO), :]
            wide = wide + _dot(b_prev, jnp.concatenate(_split3(prev), axis=1))
        count = jnp.minimum(r0 + lax.broadcasted_iota(jnp.int32, (R, 1), 0) + 1, w).astype(F32)
        pooled = (_lane_sum(wide, 3) / count - cur).astype(BF16)
        pooled_ref[pl.ds(r0, R), :] = pooled
        y_ref[pl.ds(r0, R), :] = _dot(pooled, pw) * sc

    chunk(0, True)

    def rest(c, carry):
        chunk(pl.multiple_of(c * R, R), False)
        return carry

    lax.fori_loop(1, S // R, rest, 0, unroll=POOL_UNROLL)


ATTN_WINDOW = 3
ATTN_TOGETHER = 2


def _band_ones(T):
    row = lax.broadcasted_iota(jnp.int32, (T, T), 0)
    col = lax.broadcasted_iota(jnp.int32, (T, T), 1)
    after = jnp.where(row > col, 1.0, 0.0).astype(BF16)
    before = jnp.where(row < col, 1.0, 0.0).astype(BF16)
    return jnp.concatenate([after, after], axis=0), jnp.concatenate([before, before], axis=0), col < row


def _log1m(z, mask):
    sp = jnp.log(1.0 + jnp.exp(-jnp.abs(z)))
    l1m = -jnp.maximum(z, 0.0) - sp
    return (l1m if mask is None else jnp.where(mask, l1m, 0.0)), sp


def _log_gates_of(jobs, after2):
    zs = [_dot_nt(qh, kb) for qh, kb, _ in jobs]
    terms = []
    for z, (_, _, mask) in zip(zs, jobs):
        l1m, sp = _log1m(z, mask)
        terms.append(
            (jnp.minimum(z, 0.0) - sp, jnp.concatenate(_split2(l1m), axis=1), jnp.sum(l1m, axis=1, keepdims=True))
        )
    return [(ls, _dot(split, after2), total) for ls, split, total in terms]


def _log_gates(qhs, kbs, masks, after2):
    flat = _log_gates_of([(qh, kb, mask) for kb, mask in zip(kbs, masks) for qh in qhs], after2)
    return [flat[len(qhs) * b : len(qhs) * (b + 1)] for b in range(len(kbs))]


def _tile_start(index, T):
    return index * T if isinstance(index, int) else pl.multiple_of(index * T, T)


def _weights(ls, tail, carry, mask):
    a = jnp.exp(ls + tail + carry)
    return a if mask is None else jnp.where(mask, a, 0.0)


def _reaches(carry):
    return jnp.max(carry) > EXP_UNDERFLOW


def _mixers_fwd(q, k, v, u, pool_w, scale, shards):
    S = q.shape[0]
    T = ATTN_TILE
    nq = S // T
    n = len(shards)
    steps = D_ATTN // LANES

    def body(q_ref, k_ref, v_ref, u_ref, pw_ref, sc_ref, *rest):
        o_ref, pooled_ref, yp_ref = rest[n : n + 3]
        step = pl.program_id(0)

        def gather():
            return _two_level_gather(rest[:n], rest[n + 3 : 2 * n + 3], rest[2 * n + 3 :])

        @pl.when(step == 0)
        def _():
            gather()[0]()

        _pool_fwd_group(u_ref, pw_ref, sc_ref, pooled_ref, yp_ref, step)

        lane = lax.broadcasted_iota(jnp.int32, (T, LANES), 1)
        heads = (lane < HEAD_DIM, lane >= HEAD_DIM)
        after2, _, causal = _band_ones(T)
        zero = jnp.zeros((T, 1), F32)

        def key_tile(k0):
            vb = v_ref[pl.ds(k0, T), :]
            return k_ref[pl.ds(k0, T), :], jnp.concatenate([jnp.where(m, vb, jnp.zeros_like(vb)) for m in heads], axis=0)

        def q_tiles(qis, nb):
            queries, jobs = [], []
            for qi in qis:
                q0 = _tile_start(qi, T)
                qb = q_ref[pl.ds(q0, T), :]
                qhs = [jnp.where(m, qb, jnp.zeros_like(qb)) for m in heads]
                tiles = [key_tile(_tile_start(qi - b, T)) for b in range(nb)]
                queries.append((q0, qhs, [v_rows for _, v_rows in tiles]))
                jobs += [(qh, kb, causal if b == 0 else None) for b, (kb, _) in enumerate(tiles) for qh in qhs]
            gates = _log_gates_of(jobs, after2)
            states = []
            for i, (q0, qhs, values) in enumerate(queries):
                carries, probs = [zero, zero], []
                for b in range(nb):
                    for h in range(2):
                        ls, tail, total = gates[2 * nb * i + 2 * b + h]
                        probs.append(_weights(ls, tail, carries[h], causal if b == 0 else None).astype(BF16))
                        carries[h] = carries[h] + total
                states.append((carries, _dot(jnp.concatenate(probs, axis=1), jnp.concatenate(values, axis=0))))

            def more(st):
                return (st[0] > 0) & _reaches(jnp.maximum(st[1], st[2]))

            for qi, (q0, qhs, _), (carries, acc) in zip(qis, queries, states):

                def k_step(st, qhs=qhs):
                    kj = st[0] - 1
                    kb, v_rows = key_tile(pl.multiple_of(kj * T, T))
                    new, probs = [], []
                    for (ls, tail, total), carry in zip(_log_gates(qhs, [kb], [None], after2)[0], st[1:3]):
                        probs.append(_weights(ls, tail, carry, None).astype(BF16))
                        new.append(carry + total)
                    return kj, new[0], new[1], st[3] + _dot(jnp.concatenate(probs, axis=1), v_rows)

                if not isinstance(qi, int):
                    acc = lax.while_loop(more, k_step, (qi - (nb - 1), carries[0], carries[1], acc))[3]
                o_ref[pl.ds(q0, T), :] = acc

        head_tiles = min(ATTN_WINDOW - 1, nq)
        for qi in range(head_tiles):
            q_tiles([qi], qi + 1)
        assert (nq - head_tiles) % ATTN_TOGETHER == 0

        def q_loop(i, carry):
            first = head_tiles + ATTN_TOGETHER * i
            q_tiles([first + j for j in range(ATTN_TOGETHER)], ATTN_WINDOW)
            return carry

        lax.fori_loop(0, (nq - head_tiles) // ATTN_TOGETHER, q_loop, 0)

        @pl.when(step == steps - 1)
        def _():
            _, forward, finish = gather()
            forward()
            finish()

    blk = pl.BlockSpec((S, LANES), lambda p: (0, p))
    return pl.pallas_call(
        body,
        name="mixers_fwd",
        grid=(steps,),
        out_shape=[
            jax.ShapeDtypeStruct((S, D_ATTN), F32),
            jax.ShapeDtypeStruct((S, D_POOL), BF16),
            jax.ShapeDtypeStruct((S, D_POOL), F32),
        ]
        + _exchange_shapes(shards, True),
        in_specs=[blk, blk, blk, blk, _POOL_W_SPEC, _POOL_SCALE_SPEC] + [ANY_SPEC] * n,
        out_specs=[blk, blk, blk] + [ANY_SPEC] * n,
        scratch_shapes=_exchange_sems(n),
        compiler_params=pltpu.CompilerParams(
            dimension_semantics=("arbitrary",), vmem_limit_bytes=VMEM_LIMIT, has_side_effects=True
        ),
    )(q, k, v, u, pool_w, scale, *shards)


MLP_AHEAD = 1
SG_ROWS = 8


def _mlp_fwd_bwd(x, y_pool, y_attn, target, g_pool, g_attn, g2, gf, w_out, w_up, w_down):
    S = x.shape[0]
    tm = min(256, S)
    fc = D_FF // N_DEV

    def body(x_ref, yp_ref, ya_ref, t_ref, gp_ref, ga_ref, g2_ref, gf_ref, wo_hbm, wu_hbm, wd_hbm,
             mixed_ref, hn2_ref, act_ref, dup_ref, dh2_ref, dh1_ref, dh1b_ref, dyp_ref, dya_ref, sg_ref,
             wo, wu, wd, up_s, sems):
        @pl.when(pl.program_id(0) == 0)
        def _():
            copies = [
                pltpu.make_async_copy(wo_hbm, wo, sems.at[0]),
                pltpu.make_async_copy(wu_hbm, wu, sems.at[1]),
                pltpu.make_async_copy(wd_hbm, wd, sems.at[2]),
            ]
            for cp in copies:
                cp.start()
            for cp in copies:
                cp.wait()
            sg_ref[...] = jnp.zeros_like(sg_ref)

        gp, ga, g2v, gfv = gp_ref[...], ga_ref[...], g2_ref[...], gf_ref[...]
        yp, ya = yp_ref[...], ya_ref[...]
        rp, ra = _rstd(yp), _rstd(ya)
        yph, yah = yp * rp, ya * ra
        mixed = jnp.concatenate([(yph * gp).astype(BF16), (yah * ga).astype(BF16)], axis=1)
        mixed_ref[...] = mixed
        h1 = x_ref[...] + _dot(mixed, wo[...])
        r2 = _rstd(h1)
        h1h = h1 * r2
        hn2 = (h1h * g2v).astype(BF16)
        hn2_ref[...] = hn2
        h2 = h1
        ups = [_dot(hn2, wu[j]) for j in range(MLP_AHEAD)]
        for j in range(N_DEV):
            cols = slice(fc * j, fc * (j + 1))
            if j + MLP_AHEAD < N_DEV:
                ups.append(_dot(hn2, wu[j + MLP_AHEAD]))
            up = ups.pop(0)
            up_s[:, cols] = up
            act = jnp.square(jnp.maximum(up, 0.0)).astype(BF16)
            act_ref[:, cols] = act
            h2 = h2 + _dot(act, wd[cols, :])
        rf = _rstd(h2)
        h2h = h2 * rf
        diff = h2h * gfv - t_ref[...]
        loss_rows = 0.5 * jnp.mean(diff * diff, axis=-1, keepdims=True)
        dy = diff * (1.0 / D_MODEL)
        dh2, dgf = _rms_bwd(dy, h2h, rf, gfv)
        dh2b = dh2.astype(BF16)
        dh2_ref[...] = dh2b
        dhn2 = jnp.zeros((tm, D_MODEL), F32)
        dacts = [_dot_nt(dh2b, wd[fc * j : fc * (j + 1), :]) for j in range(MLP_AHEAD)]
        for j in range(N_DEV):
            cols = slice(fc * j, fc * (j + 1))
            if j + MLP_AHEAD < N_DEV:
                dacts.append(_dot_nt(dh2b, wd[fc * (j + MLP_AHEAD) : fc * (j + MLP_AHEAD + 1), :]))
            dup = (dacts.pop(0) * (2.0 * jnp.maximum(up_s[:, cols], 0.0))).astype(BF16)
            dup_ref[:, cols] = dup
            dhn2 = dhn2 + _dot_nt(dup, wu[j])
        dh1n, dg2 = _rms_bwd(dhn2, h1h, r2, g2v)
        dh1 = dh2 + dh1n
        dh1_ref[...] = dh1
        dh1b = dh1.astype(BF16)
        dh1b_ref[...] = dh1b
        dmix = _dot_nt(dh1b, wo[...])
        dyp, dgp = _rms_bwd(dmix[:, :D_POOL], yph, rp, gp)
        dya, dga = _rms_bwd(dmix[:, D_POOL:], yah, ra, ga)
        dyp_ref[...] = dyp
        dya_ref[...] = dya
        sg_ref[0:1, :] += dgf
        sg_ref[1:2, :] += dg2
        sg_ref[2:3, :] += jnp.concatenate([dgp, dga], axis=1)
        sg_ref[3:4, :] += jnp.broadcast_to(jnp.sum(loss_rows, axis=0, keepdims=True), (1, D_MODEL))

    def tok(n):
        return pl.BlockSpec((tm, n), lambda i: (i, 0))

    def vec(n):
        return pl.BlockSpec((1, n), lambda i: (0, 0))

    any_spec = pl.BlockSpec(memory_space=pl.ANY)
    return pl.pallas_call(
        body,
        name="mlp_fwd_bwd",
        grid=(S // tm,),
        out_shape=[
            jax.ShapeDtypeStruct((S, D_MODEL), BF16),
            jax.ShapeDtypeStruct((S, D_MODEL), BF16),
            jax.ShapeDtypeStruct((S, D_FF), BF16),
            jax.ShapeDtypeStruct((S, D_FF), BF16),
            jax.ShapeDtypeStruct((S, D_MODEL), BF16),
            jax.ShapeDtypeStruct((S, D_MODEL), F32),
            jax.ShapeDtypeStruct((S, D_MODEL), BF16),
            jax.ShapeDtypeStruct((S, D_POOL), F32),
            jax.ShapeDtypeStruct((S, D_ATTN), F32),
            jax.ShapeDtypeStruct((SG_ROWS, D_MODEL), F32),
        ],
        in_specs=[tok(D_MODEL), tok(D_POOL), tok(D_ATTN), tok(D_MODEL), vec(D_POOL), vec(D_ATTN),
                  vec(D_MODEL), vec(D_MODEL), any_spec, any_spec, any_spec],
        out_specs=[tok(D_MODEL), tok(D_MODEL), tok(D_FF), tok(D_FF), tok(D_MODEL), tok(D_MODEL),
                   tok(D_MODEL), tok(D_POOL), tok(D_ATTN),
                   pl.BlockSpec((SG_ROWS, D_MODEL), lambda i: (0, 0))],
        scratch_shapes=[
            pltpu.VMEM((D_MODEL, D_MODEL), BF16),
            pltpu.VMEM((N_DEV, D_MODEL, fc), BF16),
            pltpu.VMEM((D_FF, D_MODEL), BF16),
            pltpu.VMEM((tm, D_FF), F32),
            pltpu.SemaphoreType.DMA((3,)),
        ],
        compiler_params=_params(("arbitrary",)),
    )(x, y_pool, y_attn, target, g_pool, g_attn, g2, gf, w_out, w_up, w_down)


def _wgrad(a, b, block_a, groups, name, travelling=()):
    n = len(travelling)
    S, ka = a.shape
    nb = b.shape[1]
    ts = min(1024, S)
    per = N_DEV // groups
    if block_a:
        ka //= groups
        blk = (ka // per, nb)
        a_spec = pl.BlockSpec((ts, ka), lambda g, s: (s, g))
        b_spec = pl.BlockSpec((ts, nb), lambda g, s: (s, 0))
    else:
        nb //= groups
        blk = (ka, nb // per)
        a_spec = pl.BlockSpec((ts, ka), lambda g, s: (s, 0))
        b_spec = pl.BlockSpec((ts, nb), lambda g, s: (s, g))
    steps = S // ts

    def body(a_ref, b_ref, *rest):
        o_ref, acc = rest[n], rest[2 * n + 1]
        g, s = pl.program_id(0), pl.program_id(1)

        def scatter():
            return _direct_exchange(rest[:n], rest[n + 1 : 2 * n + 1], rest[2 * n + 2 :], False)

        if n:
            @pl.when((g == 0) & (s == 0))
            def _():
                scatter()[0]()

        @pl.when(s == 0)
        def _():
            acc[...] = jnp.zeros_like(acc)

        acc[...] += _dot_tn(a_ref[...], b_ref[...])

        @pl.when(s == steps - 1)
        def _():
            for j in range(per):
                if block_a:
                    o_ref[j] = acc[blk[0] * j : blk[0] * (j + 1), :].astype(BF16)
                else:
                    o_ref[j] = acc[:, blk[1] * j : blk[1] * (j + 1)].astype(BF16)

        if n:
            @pl.when((g == groups - 1) & (s == steps - 1))
            def _():
                scatter()[1]()

    results = pl.pallas_call(
        body,
        name=name,
        grid=(groups, steps),
        out_shape=[jax.ShapeDtypeStruct((N_DEV,) + blk, BF16)] + _exchange_shapes(travelling, False),
        in_specs=[a_spec, b_spec] + [ANY_SPEC] * n,
        out_specs=[pl.BlockSpec((per,) + blk, lambda g, s: (g, 0, 0))] + [ANY_SPEC] * n,
        scratch_shapes=[pltpu.VMEM((ka, nb), F32)] + (_exchange_sems(n) if n else []),
        compiler_params=pltpu.CompilerParams(
            dimension_semantics=("arbitrary", "arbitrary"), vmem_limit_bytes=VMEM_LIMIT, has_side_effects=bool(n)
        ),
    )(a, b, *travelling)
    return results if n else results[0]


def _mixers_bwd(q, k, v, do, dyp, pooled, pool_w, scale, partials):
    S = q.shape[0]
    T = ATTN_TILE
    nq = S // T
    n = len(partials)
    steps = D_ATTN // LANES

    def body(q_ref, k_ref, v_ref, do_ref, dyp_ref, pooled_ref, pw_ref, sc_ref, *rest):
        dq_ref, dk_ref, dv_ref, du_ref, dsc_ref, dpw_ref = rest[n : n + 6]
        dk_acc, dv_acc, carry_s = rest[2 * n + 6 : 2 * n + 9]
        step = pl.program_id(0)

        def scatter():
            return _direct_exchange(rest[:n], rest[n + 6 : 2 * n + 6], rest[2 * n + 9 :], False)

        @pl.when(step == 0)
        def _():
            scatter()[0]()

        _pool_bwd_group(dyp_ref, pooled_ref, pw_ref, sc_ref, du_ref, dsc_ref, dpw_ref, step)

        dk_acc[...] = jnp.zeros_like(dk_acc)
        dv_acc[...] = jnp.zeros_like(dv_acc)
        lane = lax.broadcasted_iota(jnp.int32, (T, LANES), 1)
        heads = (lane < HEAD_DIM, lane >= HEAD_DIM)
        after2, before2, causal = _band_ones(T)
        zero = jnp.zeros((T, 1), F32)

        def key_tile(k0):
            kb = k_ref[pl.ds(k0, T), :]
            k_rows = jnp.concatenate([jnp.where(m, kb, jnp.zeros_like(kb)) for m in heads], axis=0)
            return kb, v_ref[pl.ds(k0, T), :], k_rows

        def grads(items):
            das = [[[_dot_nt(doh, vb) for doh in qs[1]] for _, vb, _ in tiles] for qs, _, tiles, _, _, _, _ in items]
            probs, gs = [], []
            for (_, _, tiles, masks, carry_in, gates, _), da in zip(items, das):
                probs.append([[_weights(gates[b][h][0], gates[b][h][1], carry_in[b][h], masks[b]) for h in range(2)]
                              for b in range(len(tiles))])
                gs.append([[probs[-1][b][h] * da[b][h] for h in range(2)] for b in range(len(tiles))])
            before = [[[_dot(jnp.concatenate(_split2(g), axis=1), before2) for g in row] for row in item] for item in gs]
            dzs, lefts = [], []
            for i, (_, _, tiles, masks, _, gates, st) in enumerate(items):
                g_left, item_dz = [st[0], st[1]], [None] * len(tiles)
                for b in reversed(range(len(tiles))):
                    item_dz[b] = []
                    for h in range(2):
                        sig = jnp.exp(gates[b][h][0])
                        dz = gs[i][b][h] * (1.0 - sig) - sig * (g_left[h] + before[i][b][h])
                        if masks[b] is not None:
                            dz = jnp.where(masks[b], dz, 0.0)
                        item_dz[b].append(dz.astype(BF16))
                        g_left[h] = g_left[h] + jnp.sum(gs[i][b][h], axis=1, keepdims=True)
                dzs.append(item_dz)
                lefts.append(g_left)
            out = []
            for (_, _, tiles, _, _, _, st), item_dz, g_left in zip(items, dzs, lefts):
                dq = st[2] + _dot(
                    jnp.concatenate([dz for row in item_dz for dz in row], axis=1),
                    jnp.concatenate([k_rows for _, _, k_rows in tiles], axis=0),
                )
                out.append((g_left[0], g_left[1], dq))
            for (qs, starts, tiles, _, _, _, _), item_dz, item_probs in zip(items, dzs, probs):
                for b in range(len(tiles)):
                    dk_acc[pl.ds(starts[b], T), :] += _dot_tn(jnp.concatenate(item_dz[b], axis=0), qs[2])
                    dv_acc[pl.ds(starts[b], T), :] += _dot_tn(
                        jnp.concatenate([a.astype(BF16) for a in item_probs[b]], axis=0), qs[3]
                    )
            return out

        def more(st):
            return (st[0] > 0) & _reaches(jnp.maximum(st[1], st[2]))

        def q_tiles(qis, nb):
            masks = [causal] + [None] * (nb - 1)
            prepared, jobs = [], []
            for qi in qis:
                q0 = _tile_start(qi, T)
                qb = q_ref[pl.ds(q0, T), :]
                dob = do_ref[pl.ds(q0, T), :]
                qhs = [jnp.where(m, qb, jnp.zeros_like(qb)) for m in heads]
                dohs = [jnp.where(m, dob, 0.0).astype(BF16) for m in heads]
                queries = (qhs, dohs, jnp.concatenate(qhs, axis=0), jnp.concatenate(dohs, axis=0))
                starts = [_tile_start(qi - b, T) for b in range(nb)]
                tiles = [key_tile(k0) for k0 in starts]
                prepared.append((q0, queries, starts, tiles))
                jobs += [(qh, kb, mask) for (kb, _, _), mask in zip(tiles, masks) for qh in qhs]
            flat = _log_gates_of(jobs, after2)
            items = []
            for i, (qi, (q0, queries, starts, tiles)) in enumerate(zip(qis, prepared)):
                gates = [flat[2 * nb * i + 2 * b : 2 * nb * i + 2 * b + 2] for b in range(nb)]
                carries, carry_in = [zero, zero], []
                for b in range(nb):
                    carry_in.append(list(carries))
                    carries = [carries[h] + gates[b][h][2] for h in range(2)]
                st = (zero, zero, jnp.zeros((T, LANES), F32))
                if not isinstance(qi, int):
                    k_left = qi - (nb - 1)
                    qhs = queries[0]

                    def right_to_left(st, qhs=qhs):
                        kj = st[0] - 1
                        carry_s[0, kj] = st[1]
                        carry_s[1, kj] = st[2]
                        kb = k_ref[pl.ds(pl.multiple_of(kj * T, T), T), :]
                        sums = [jnp.sum(_log1m(_dot_nt(qh, kb), None)[0], axis=1, keepdims=True) for qh in qhs]
                        return kj, st[1] + sums[0], st[2] + sums[1]

                    k_first = lax.while_loop(more, right_to_left, (k_left, carries[0], carries[1]))[0]

                    def left_to_right(kj, st, queries=queries):
                        k0 = pl.multiple_of(kj * T, T)
                        tile = key_tile(k0)
                        left_gates = _log_gates(queries[0], [tile[0]], [None], after2)
                        carry = [[carry_s[0, kj], carry_s[1, kj]]]
                        return grads([(queries, [k0], [tile], [None], carry, left_gates, st)])[0]

                    st = lax.fori_loop(k_first, k_left, left_to_right, st)
                items.append((queries, starts, tiles, masks, carry_in, gates, st))
            for (q0, _, _, _), st in zip(prepared, grads(items)):
                dq_ref[pl.ds(q0, T), :] = (st[2] * (HEAD_DIM**-0.5)).astype(BF16)

        head_tiles = min(ATTN_WINDOW - 1, nq)
        for qi in range(head_tiles):
            q_tiles([qi], qi + 1)
        assert (nq - head_tiles) % ATTN_TOGETHER == 0

        def q_loop(i, carry):
            first = head_tiles + ATTN_TOGETHER * i
            q_tiles([first + j for j in range(ATTN_TOGETHER)], ATTN_WINDOW)
            return carry

        lax.fori_loop(0, (nq - head_tiles) // ATTN_TOGETHER, q_loop, 0)
        dk_ref[...] = dk_acc[...].astype(BF16)
        dv_ref[...] = dv_acc[...].astype(BF16)

        @pl.when(step == steps - 1)
        def _():
            scatter()[1]()

    blk = pl.BlockSpec((S, LANES), lambda p: (0, p))
    sds = jax.ShapeDtypeStruct((S, D_ATTN), BF16)
    return pl.pallas_call(
        body,
        name="mixers_bwd",
        grid=(steps,),
        out_shape=[
            sds,
            sds,
            sds,
            jax.ShapeDtypeStruct((S, D_POOL), BF16),
            jax.ShapeDtypeStruct((8, D_POOL), F32),
            jax.ShapeDtypeStruct((N_GROUPS, GROUP_DIM, GROUP_DIM), BF16),
        ]
        + _exchange_shapes(partials, False),
        in_specs=[blk] * 6 + [_POOL_W_SPEC, _POOL_SCALE_SPEC] + [ANY_SPEC] * n,
        out_specs=[blk] * 4 + [pl.BlockSpec((8, GROUP_DIM), lambda p: (0, p)), _POOL_W_SPEC] + [ANY_SPEC] * n,
        scratch_shapes=[pltpu.VMEM((S, LANES), F32), pltpu.VMEM((S, LANES), F32), pltpu.VMEM((2, nq, T, 1), F32)]
        + _exchange_sems(n),
        compiler_params=pltpu.CompilerParams(
            dimension_semantics=("arbitrary",), vmem_limit_bytes=VMEM_LIMIT, has_side_effects=True
        ),
    )(q, k, v, do, dyp, pooled, pool_w, scale, *partials)


def _pool_bwd_group(dy_ref, pooled_ref, pw_ref, sc_ref, du_ref, dsc_ref, dpw_ref, g):
    S = dy_ref.shape[0]
    R = min(POOL_CHUNK, S)
    nc = S // R
    w = jnp.left_shift(jnp.int32(2), g)
    pw = pw_ref[0].astype(BF16)
    sc = sc_ref[...]
    d = lax.broadcasted_iota(jnp.int32, (R, R), 1) - lax.broadcasted_iota(jnp.int32, (R, R), 0)
    bt_cur = jnp.where((d >= 0) & (d < w), 1.0, 0.0).astype(BF16)
    dn = lax.broadcasted_iota(jnp.int32, (R, HALO), 1) + R - lax.broadcasted_iota(jnp.int32, (R, HALO), 0)
    bt_next = jnp.where(dn < w, 1.0, 0.0).astype(BF16)

    def per_count(dpl, r0):
        n = dpl.shape[0]
        count = jnp.minimum(r0 + lax.broadcasted_iota(jnp.int32, (n, 1), 0) + 1, w).astype(F32)
        return jnp.concatenate(_split2(dpl / count), axis=1)

    def chunk(r0, last, sums):
        dyv = dy_ref[pl.ds(r0, R), :]
        pooled = pooled_ref[pl.ds(r0, R), :]
        dmapped = (dyv * sc).astype(BF16)
        dsc = sums[0] + jnp.sum(dyv * _dot(pooled, pw), axis=0, keepdims=True)
        dpw = sums[1] + _dot_tn(pooled, dmapped)
        dpl = _dot_nt(dmapped, pw)
        wide = _dot(bt_cur, per_count(dpl, r0))
        if not last:
            nxt = pl.multiple_of(r0 + R, R)
            dpl_next = _dot_nt((dy_ref[pl.ds(nxt, HALO), :] * sc).astype(BF16), pw)
            wide = wide + _dot(bt_next, per_count(dpl_next, nxt))
        du_ref[pl.ds(r0, R), :] = (_lane_sum(wide, 2) - dpl).astype(BF16)
        return dsc, dpw

    sums = (jnp.zeros((1, GROUP_DIM), F32), jnp.zeros((GROUP_DIM, GROUP_DIM), F32))
    sums = lax.fori_loop(
        0, nc - 1, lambda c, s: chunk(pl.multiple_of(c * R, R), False, s), sums, unroll=POOL_UNROLL
    )
    dsc, dpw = chunk((nc - 1) * R, True, sums)
    dsc_ref[...] = jnp.zeros_like(dsc_ref)
    dsc_ref[0:1, :] = dsc
    dpw_ref[0] = dpw.astype(dpw_ref.dtype)


def _bwd_in(du, dq, dk, dv, w_in, x, dh1, g1):
    S = x.shape[0]
    tm = min(512, S)

    def body(du_ref, dq_ref, dk_ref, dv_ref, w_ref, x_ref, dh1_ref, g_ref, dx_ref, dproj_ref, dg_ref):
        @pl.when(pl.program_id(0) == 0)
        def _():
            dg_ref[...] = jnp.zeros_like(dg_ref)

        parts = (du_ref[...], dq_ref[...], dk_ref[...], dv_ref[...])
        dhn = jnp.zeros((tm, D_MODEL), F32)
        for j in range(N_DEV):
            piece = parts[j // 2][:, 256 * (j % 2) : 256 * (j % 2 + 1)]
            dproj_ref[:, 256 * j : 256 * (j + 1)] = piece
            dhn = dhn + _dot_nt(piece, w_ref[j])
        xv = x_ref[...]
        r = _rstd(xv)
        dxn, dg = _rms_bwd(dhn, xv * r, r, g_ref[...])
        dx_ref[...] = dh1_ref[...] + dxn
        dg_ref[0:1, :] += dg

    half = pl.BlockSpec((tm, D_POOL), lambda i: (i, 0))
    full = pl.BlockSpec((tm, D_MODEL), lambda i: (i, 0))
    return pl.pallas_call(
        body,
        name="bwd_in",
        grid=(S // tm,),
        out_shape=[
            jax.ShapeDtypeStruct((S, D_MODEL), F32),
            jax.ShapeDtypeStruct((S, D_IN_PROJ), BF16),
            jax.ShapeDtypeStruct((8, D_MODEL), F32),
        ],
        in_specs=[half, half, half, half,
                  pl.BlockSpec((N_DEV, D_MODEL, 256), lambda i: (0, 0, 0)),
                  full, full, pl.BlockSpec((1, D_MODEL), lambda i: (0, 0))],
        out_specs=[full, pl.BlockSpec((tm, D_IN_PROJ), lambda i: (i, 0)),
                   pl.BlockSpec((8, D_MODEL), lambda i: (0, 0))],
        compiler_params=_params(("arbitrary",)),
    )(du, dq, dk, dv, w_in, x, dh1, g1)


def _rows(a):
    a = a.reshape(-1, LANES)
    pad = (-a.shape[0]) % 8
    return jnp.pad(a, ((0, pad), (0, 0))) if pad else a


def kernel(x, norm1_g, w_in, pool_w, pool_scale, pool_out_g, attn_out_g, w_out, norm2_g, w_up, w_down, final_g, loss_target, m_norm1_g, m_w_in, m_pool_w, m_pool_scale, m_pool_out_g, m_attn_out_g, m_w_out, m_norm2_g, m_w_up, m_w_down, m_final_g, v_norm1_g, v_w_in, v_pool_w, v_pool_scale, v_pool_out_g, v_attn_out_g, v_w_out, v_norm2_g, v_w_up, v_w_down, v_final_g):
    S = x.shape[1]
    xs = x.reshape(S, D_MODEL)
    tgt = loss_target.reshape(S, D_MODEL)
    row = lambda a: a.reshape(1, -1)

    (w_in_g,) = _exchange([w_in.astype(BF16)], True, "gather_w_in")
    hn, u_pool, q, k, v = _fwd_in(xs, row(norm1_g), w_in_g)
    y_attn, pooled, y_pool, w_out_g, w_up_g, w_down_g = _mixers_fwd(
        q, k, v, u_pool, pool_w, row(pool_scale), [w_out.astype(BF16), w_up.astype(BF16), w_down.astype(BF16)]
    )
    w_out_full = w_out_g.reshape(D_MODEL, D_MODEL)
    w_down_full = w_down_g.reshape(D_FF, D_MODEL)
    mixed, hn2, act, dup, dh2b, dh1, dh1b, dyp, dya, sg = _mlp_fwd_bwd(
        xs, y_pool, y_attn, tgt, row(pool_out_g), row(attn_out_g), row(norm2_g), row(final_g),
        w_out_full, w_up_g, w_down_full,
    )
    gp_down = _wgrad(act, dh2b, True, 2, "wgrad_down")
    gp_up = _wgrad(hn2, dup, False, 2, "wgrad_up")
    gp_out = _wgrad(mixed, dh1b, True, 1, "wgrad_out")
    dq, dk, dv, du, dsc, dpw, land_up, land_down = _mixers_bwd(
        q, k, v, dya, dyp, pooled, pool_w, row(pool_scale), [gp_up, gp_down]
    )
    dx, dproj, dg1 = _bwd_in(du, dq, dk, dv, w_in_g, xs, dh1, row(norm1_g))
    gp_in, land_out = _wgrad(hn, dproj, False, 1, "wgrad_in", [gp_out])

    def vectors(final, norm2, pool_out, attn_out, norm1, scale, last):
        pieces = [final, norm2, jnp.concatenate([pool_out, attn_out]), norm1, scale, last]
        return jnp.concatenate([_rows(p) for p in pieces], axis=0)

    no_loss = jnp.zeros((8, LANES), F32)
    flat_pool = lambda a: a.reshape(N_GROUPS * GROUP_DIM, GROUP_DIM)
    smalls = [
        (
            vectors(sg[0], sg[1], sg[2][:D_POOL], sg[2][D_POOL:], dg1[0], dsc[0], sg[3]),
            [(
                vectors(final_g, norm2_g, pool_out_g, attn_out_g, norm1_g, pool_scale, no_loss),
                vectors(m_final_g, m_norm2_g, m_pool_out_g, m_attn_out_g, m_norm1_g, m_pool_scale, no_loss),
                vectors(v_final_g, v_norm2_g, v_pool_out_g, v_attn_out_g, v_norm1_g, v_pool_scale, no_loss),
                None,
            )],
        ),
        (flat_pool(dpw), [(flat_pool(pool_w), flat_pool(m_pool_w), flat_pool(v_pool_w), None)]),
    ]
    tail = _reduce_adam_tail(
        [land_out, land_up, land_down], [w_out, w_up, w_down], [m_w_out, m_w_up, m_w_down],
        [v_w_out, v_w_up, v_w_down], gp_in, (w_in, m_w_in, v_w_in), smalls,
    )
    big = {name: tail[4 * t : 4 * t + 4] for t, name in enumerate(("w_out", "w_up", "w_down", "w_in"))}

    def unpack(vec, pw):
        out = {}
        for i, name in enumerate(("final_g", "norm2_g", "mix_g", "norm1_g", "pool_scale")):
            out[name] = vec[8 * i : 8 * i + 8].reshape(-1)
        out["pool_scale"] = out["pool_scale"][:D_POOL]
        out["pool_out_g"], out["attn_out_g"] = out["mix_g"][:D_POOL], out["mix_g"][D_POOL:]
        out["pool_w"] = pw.reshape(N_GROUPS, GROUP_DIM, GROUP_DIM)
        return out, vec[40, 0]

    small_out = [unpack(tail[16 + i], tail[20 + i]) for i in range(4)]
    loss = small_out[0][1]
    order = ("norm1_g", "w_in", "pool_w", "pool_scale", "pool_out_g", "attn_out_g", "w_out", "norm2_g", "w_up",
             "w_down", "final_g")
    outs = [loss, dx.reshape(1, S, D_MODEL)]
    for i in range(4):
        for name in order:
            outs.append(big[name][i] if name in big else small_out[i][0][name])
    return tuple(outs)
```

```python
import functools

import jax
import jax.numpy as jnp
from jax import lax
from jax.experimental import pallas as pl
from jax.experimental.pallas import tpu as pltpu

F32 = jnp.float32
BF16 = jnp.bfloat16
MESH = pl.DeviceIdType.MESH

N_DEV = 8
D_MODEL = 1024
D_POOL = 512
D_ATTN = 512
N_GROUPS = 4
GROUP_DIM = 128
HEAD_DIM = 64
D_FF = 4096
D_IN_PROJ = 2048
EPS = 1e-6
HALO = 16
ATTN_TILE = 128
LANES = 128
EXP_UNDERFLOW = -104.0

ADAM_LR = 0.001
ADAM_B1 = 0.9
ADAM_B2 = 0.999
ADAM_EPS = 1e-08
ADAM_WD = 0.01
ADAM_STEP = 10

VMEM_LIMIT = 56 * 1024 * 1024


def _params(semantics=None, vmem=VMEM_LIMIT):
    return pltpu.CompilerParams(dimension_semantics=semantics, vmem_limit_bytes=vmem)


def _dot(a, b):
    return jnp.dot(a, b, preferred_element_type=F32)


def _dot_nt(a, b):
    return lax.dot_general(a, b, (((1,), (1,)), ((), ())), preferred_element_type=F32)


def _dot_tn(a, b):
    return lax.dot_general(a, b, (((0,), (0,)), ((), ())), preferred_element_type=F32)


def _split2(x):
    hi = x.astype(BF16)
    lo = (x - hi.astype(F32)).astype(BF16)
    return hi, lo


def _split3(x):
    hi = x.astype(BF16)
    r = x - hi.astype(F32)
    mid = r.astype(BF16)
    lo = (r - mid.astype(F32)).astype(BF16)
    return hi, mid, lo


def _rstd(h):
    return lax.rsqrt(jnp.mean(h * h, axis=-1, keepdims=True) + EPS)


def _rms_bwd(dout, hhat, r, g):
    dg = jnp.sum(dout * hhat, axis=0, keepdims=True)
    dxh = dout * g
    dh = r * (dxh - hhat * jnp.mean(dxh * hhat, axis=-1, keepdims=True))
    return dh, dg


def _my_index():
    return 4 * lax.axis_index("x") + 2 * lax.axis_index("y") + lax.axis_index("c")


def _peer(k):
    x, y, c = lax.axis_index("x"), lax.axis_index("y"), lax.axis_index("c")
    px = 1 - x if (k >> 2) & 1 else x
    py = 1 - y if (k >> 1) & 1 else y
    pc = 1 - c if k & 1 else c
    return (px, py, pc), 4 * px + 2 * py + pc


N_PEERS = N_DEV - 1
ANY_SPEC = pl.BlockSpec(memory_space=pl.ANY)


def _exchange_sems(n):
    return [
        pltpu.SemaphoreType.DMA((n * N_PEERS,)),
        pltpu.SemaphoreType.DMA((n * N_PEERS,)),
        pltpu.SemaphoreType.DMA((n,)),
    ]


def _exchange_shapes(blocks, gather):
    if gather:
        return [jax.ShapeDtypeStruct((N_DEV,) + b.shape, b.dtype) for b in blocks]
    return [jax.ShapeDtypeStruct(b.shape, b.dtype) for b in blocks]


def _direct_exchange(ins, outs, sems, gather):
    send_sems, recv_sems, local_sems = sems
    me = _my_index()
    own, sends, recvs = [], [], []
    for t in range(len(ins)):
        own.append(pltpu.make_async_copy(ins[t] if gather else ins[t].at[me], outs[t].at[me], local_sems.at[t]))
        for k in range(1, N_DEV):
            peer, peer_idx = _peer(k)
            src = ins[t] if gather else ins[t].at[peer_idx]
            for dst, bucket in ((outs[t].at[me], sends), (outs[t].at[peer_idx], recvs)):
                bucket.append(
                    pltpu.make_async_remote_copy(
                        src_ref=src,
                        dst_ref=dst,
                        send_sem=send_sems.at[t * N_PEERS + k - 1],
                        recv_sem=recv_sems.at[t * N_PEERS + k - 1],
                        device_id=peer,
                        device_id_type=MESH,
                    )
                )

    def start():
        for cp in own + sends:
            cp.start()

    def finish():
        for cp in recvs:
            cp.wait_recv()
        for cp in sends:
            cp.wait_send()
        for cp in own:
            cp.wait()

    return start, finish


def _two_level_gather(ins, outs, sems):
    send_sems, recv_sems, local_sems = sems
    x, y, c = lax.axis_index("x"), lax.axis_index("y"), lax.axis_index("c")
    me, sibling = (x, y, c), (x, y, 1 - c)
    chips = [(1 - x, y), (x, 1 - y), (1 - x, 1 - y)]

    def copy(t, k, block, to, from_input=False):
        slot = outs[t].at[4 * block[0] + 2 * block[1] + block[2]]
        return pltpu.make_async_remote_copy(
            src_ref=ins[t] if from_input else slot,
            dst_ref=slot,
            send_sem=send_sems.at[t * N_PEERS + k],
            recv_sem=recv_sems.at[t * N_PEERS + k],
            device_id=to,
            device_id_type=MESH,
        )

    arrays = range(len(ins))
    own = [pltpu.make_async_copy(ins[t], outs[t].at[4 * x + 2 * y + c], local_sems.at[t]) for t in arrays]
    first = [copy(t, 1 + j, me, (*chip, c), True) for t in arrays for j, chip in enumerate(chips)]
    first += [copy(t, 0, me, sibling, True) for t in arrays]
    passed = [(copy(t, 1 + j, (*chip, c), me), copy(t, 4 + j, (*chip, c), sibling))
              for t in arrays for j, chip in enumerate(chips)]
    last = [copy(t, 0, sibling, me) for t in arrays]
    last += [copy(t, 4 + j, (*chip, 1 - c), me) for t in arrays for j, chip in enumerate(chips)]

    def start():
        for cp in own + first:
            cp.start()

    def forward():
        for landed, onward in passed:
            landed.wait_recv()
            onward.start()

    def finish():
        for cp in last:
            cp.wait_recv()
        for cp in first + [onward for _, onward in passed]:
            cp.wait_send()
        for cp in own:
            cp.wait()

    return start, forward, finish


N_CHIPS = 4
PAIR_SEMS = (N_CHIPS, N_CHIPS, N_CHIPS - 1, N_CHIPS - 1)


def _pair_reduce_scatter(gp_hbm, own, pair, summed, land, local_sem, sems):
    d2d_send, d2d_recv, ici_send, ici_recv = sems
    x, y, c = lax.axis_index("x"), lax.axis_index("y"), lax.axis_index("c")
    my_chip = 2 * x + y
    sibling = (x, y, 1 - c)
    chips = [(1 - x, y), (x, 1 - y), (1 - x, 1 - y)]
    local = [pltpu.make_async_copy(gp_hbm.at[2 * j + c], own.at[j], local_sem.at[0]) for j in range(N_CHIPS)]
    to_sibling = [
        pltpu.make_async_remote_copy(
            src_ref=gp_hbm.at[2 * j + 1 - c], dst_ref=pair.at[j], send_sem=d2d_send.at[j], recv_sem=d2d_recv.at[j],
            device_id=sibling, device_id_type=MESH,
        )
        for j in range(N_CHIPS)
    ]
    to_chips, from_chips = [], []
    for r, (px, py) in enumerate(chips):
        for dst, bucket in ((land.at[my_chip], to_chips), (land.at[2 * px + py], from_chips)):
            bucket.append(
                pltpu.make_async_remote_copy(
                    src_ref=summed.at[2 * px + py], dst_ref=dst, send_sem=ici_send.at[r], recv_sem=ici_recv.at[r],
                    device_id=(px, py, c), device_id_type=MESH,
                )
            )

    def start():
        for cp in local + to_sibling:
            cp.start()

    def middle():
        for cp in to_sibling:
            cp.wait_recv()
        pltpu.make_async_copy(gp_hbm.at[pl.ds(0, N_CHIPS)], own, local_sem.at[0]).wait()
        for j in range(N_CHIPS):
            summed[j] = (own[j].astype(F32) + pair[j].astype(F32)).astype(BF16)
        for cp in to_chips:
            cp.start()

    def finish():
        for cp in from_chips:
            cp.wait_recv()
        for cp in to_chips + to_sibling:
            cp.wait_send()
        land[my_chip] = summed[my_chip]

    return start, middle, finish


def _exchange(blocks, gather, name):
    n = len(blocks)

    def body(*refs):
        if gather:
            stages = _two_level_gather(refs[:n], refs[n : 2 * n], refs[2 * n :])
        else:
            stages = _direct_exchange(refs[:n], refs[n : 2 * n], refs[2 * n :], False)
        for stage in stages:
            stage()

    return pl.pallas_call(
        body,
        name=name,
        out_shape=_exchange_shapes(blocks, gather),
        in_specs=[ANY_SPEC] * n,
        out_specs=[ANY_SPEC] * n,
        scratch_shapes=_exchange_sems(n),
        compiler_params=pltpu.CompilerParams(has_side_effects=True),
    )(*blocks)


def _adam(w, g, m, v):
    m2 = ADAM_B1 * m + (1.0 - ADAM_B1) * g
    v2 = ADAM_B2 * v + (1.0 - ADAM_B2) * jnp.square(g)
    m_hat = m2 / (1.0 - ADAM_B1**ADAM_STEP)
    v_hat = v2 / (1.0 - ADAM_B2**ADAM_STEP)
    delta = -ADAM_LR * (m_hat / (jnp.sqrt(v_hat) + ADAM_EPS) + ADAM_WD * w)
    return delta, m2, v2


TAIL_STEPS = 8


def _reduce_adam_tail(lands, ws, ms, vs, gp_last, last, smalls):
    nw, ns = len(ws), len(smalls)
    tiles = [w.shape[0] // TAIL_STEPS for w in ws]
    blk = gp_last.shape[1:]
    entries = [(i, e) for i, (_, es) in enumerate(smalls) for e in es]
    n_params = 3 * sum(e[0] is not None for _, e in entries)
    n_small_out = sum(4 if e[0] is not None else 1 for _, e in entries)

    def body(*refs):
        gp_hbm = refs[0]
        part_hbm = refs[1 : 1 + ns]
        refs = refs[1 + ns :]
        land_refs = refs[:nw]
        w_refs, m_refs, v_refs = (refs[nw * (i + 1) : nw * (i + 2)] for i in range(3))
        last_refs = refs[4 * nw : 4 * nw + 3]
        refs = refs[4 * nw + 3 :]
        param_refs = refs[:n_params]
        outs = refs[n_params:]
        big_out = outs[: 4 * nw]
        last_out = outs[4 * nw : 4 * nw + 4]
        small_out = outs[4 * nw + 4 : 4 * nw + 4 + n_small_out]
        scratch = outs[4 * nw + 4 + n_small_out :]
        small_land = scratch[:ns]
        pair_bufs = scratch[ns : ns + 4]
        local_sem = scratch[ns + 4]
        pair_sems = scratch[ns + 5 : ns + 9]
        small_sems = scratch[ns + 9 :]
        step = pl.program_id(0)

        def pair():
            return _pair_reduce_scatter(gp_hbm, *pair_bufs, local_sem, pair_sems)

        def small():
            return _direct_exchange(part_hbm, small_land, small_sems, True)

        @pl.when(step == 0)
        def _():
            pair()[0]()
            small()[0]()

        @pl.when(step == 1)
        def _():
            pair()[1]()

        for t in range(nw):
            g = land_refs[t][0].astype(F32)
            for s in range(1, N_DEV):
                g = g + land_refs[t][s].astype(F32)
            for ref, val in zip(big_out[4 * t : 4 * t + 4], (g,) + _adam(w_refs[t][...], g, m_refs[t][...], v_refs[t][...])):
                ref[...] = val

        @pl.when(step == TAIL_STEPS - 1)
        def _():
            pair()[2]()
            small()[1]()
            land = pair_bufs[3]
            g = land[0].astype(F32)
            for chip in range(1, N_CHIPS):
                g = g + land[chip].astype(F32)
            for ref, val in zip(last_out, (g,) + _adam(last_refs[0][...], g, last_refs[1][...], last_refs[2][...])):
                ref[...] = val
            sums = []
            for i in range(ns):
                g = small_land[i][0].astype(F32)
                for s in range(1, N_DEV):
                    g = g + small_land[i][s].astype(F32)
                sums.append(g)
            params, results = list(param_refs), list(small_out)
            for i, (w, _, _, where) in entries:
                g = sums[i]
                if where is not None:
                    shape = w.shape if w is not None else where[2:]
                    g = g[where[0] : where[0] + shape[0], where[1] : where[1] + shape[1]]
                if w is None:
                    results.pop(0)[...] = g
                    continue
                w_ref, m_ref, v_ref = params[:3]
                del params[:3]
                for val in (g,) + _adam(w_ref[...], g, m_ref[...], v_ref[...]):
                    results.pop(0)[...] = val

    def tile(t):
        return pl.BlockSpec((tiles[t], ws[t].shape[1]), lambda i: (i, 0))

    def land_tile(t):
        return pl.BlockSpec((N_DEV, tiles[t], ws[t].shape[1]), lambda i: (0, i, 0))

    def whole(shape):
        return pl.BlockSpec(shape, lambda step: (0, 0))

    big_sds = [jax.ShapeDtypeStruct(ws[t].shape, F32) for t in range(nw) for _ in range(4)]
    params, small_shapes = [], []
    for _, (w, m, v, where) in entries:
        if w is None:
            small_shapes.append(tuple(where[2:]))
        else:
            params += [w, m, v]
            small_shapes += [w.shape] * 4
    return pl.pallas_call(
        body,
        name="reduce_adam_tail",
        grid=(TAIL_STEPS,),
        out_shape=big_sds
        + [jax.ShapeDtypeStruct(blk, F32)] * 4
        + [jax.ShapeDtypeStruct(shape, F32) for shape in small_shapes],
        in_specs=[ANY_SPEC] * (1 + ns)
        + [land_tile(t) for t in range(nw)]
        + [tile(t) for _ in range(3) for t in range(nw)]
        + [whole(blk)] * 3
        + [whole(p.shape) for p in params],
        out_specs=[tile(t) for t in range(nw) for _ in range(4)]
        + [whole(blk)] * 4
        + [whole(shape) for shape in small_shapes],
        scratch_shapes=[pltpu.VMEM((N_DEV,) + smalls[i][0].shape, smalls[i][0].dtype) for i in range(ns)]
        + [pltpu.VMEM((N_CHIPS,) + blk, BF16)] * 4
        + [pltpu.SemaphoreType.DMA((1,))]
        + [pltpu.SemaphoreType.DMA((count,)) for count in PAIR_SEMS]
        + _exchange_sems(ns),
        compiler_params=pltpu.CompilerParams(
            dimension_semantics=("arbitrary",), vmem_limit_bytes=VMEM_LIMIT, has_side_effects=True
        ),
    )(gp_last, *[s[0] for s in smalls], *lands, *ws, *ms, *vs, *last, *params)


def _fwd_in(x, g1, w_in):
    S = x.shape[0]
    tm = min(512, S)

    def body(x_ref, g_ref, w_ref, hn_ref, u_ref, q_ref, k_ref, v_ref):
        xv = x_ref[...]
        hn = (xv * _rstd(xv) * g_ref[...]).astype(BF16)
        hn_ref[...] = hn
        outs = (u_ref, q_ref, k_ref, v_ref)
        for j in range(N_DEV):
            p = _dot(hn, w_ref[j])
            cols = slice(256 * (j % 2), 256 * (j % 2 + 1))
            if j // 2 == 0:
                u_ref[:, cols] = p
            elif j // 2 == 1:
                q_ref[:, cols] = (p * (HEAD_DIM**-0.5)).astype(BF16)
            else:
                outs[j // 2][:, cols] = p.astype(BF16)

    half = pl.BlockSpec((tm, D_POOL), lambda i: (i, 0))
    return pl.pallas_call(
        body,
        name="fwd_in",
        grid=(S // tm,),
        out_shape=[
            jax.ShapeDtypeStruct((S, D_MODEL), BF16),
            jax.ShapeDtypeStruct((S, D_POOL), F32),
            jax.ShapeDtypeStruct((S, D_ATTN), BF16),
            jax.ShapeDtypeStruct((S, D_ATTN), BF16),
            jax.ShapeDtypeStruct((S, D_ATTN), BF16),
        ],
        in_specs=[
            pl.BlockSpec((tm, D_MODEL), lambda i: (i, 0)),
            pl.BlockSpec((1, D_MODEL), lambda i: (0, 0)),
            pl.BlockSpec((N_DEV, D_MODEL, 256), lambda i: (0, 0, 0)),
        ],
        out_specs=[pl.BlockSpec((tm, D_MODEL), lambda i: (i, 0)), half, half, half, half],
        compiler_params=_params(("parallel",)),
    )(x, g1, w_in)


POOL_CHUNK = 256
POOL_UNROLL = 3
_POOL_W_SPEC = pl.BlockSpec((1, GROUP_DIM, GROUP_DIM), lambda g: (g, 0, 0))
_POOL_SCALE_SPEC = pl.BlockSpec((1, GROUP_DIM), lambda g: (0, g))


def _lane_sum(wide, n):
    out = wide[:, :GROUP_DIM]
    for i in range(1, n):
        out = out + wide[:, GROUP_DIM * i : GROUP_DIM * (i + 1)]
    return out


def _pool_fwd_group(u_ref, pw_ref, sc_ref, pooled_ref, y_ref, g):
    S = u_ref.shape[0]
    R = min(POOL_CHUNK, S)
    w = jnp.left_shift(jnp.int32(2), g)
    d = lax.broadcasted_iota(jnp.int32, (R, R), 0) - lax.broadcasted_iota(jnp.int32, (R, R), 1)
    b_cur = jnp.where((d >= 0) & (d < w), 1.0, 0.0).astype(BF16)
    dp = lax.broadcasted_iota(jnp.int32, (R, HALO), 0) + HALO - lax.broadcasted_iota(jnp.int32, (R, HALO), 1)
    b_prev = jnp.where(dp < w, 1.0, 0.0).astype(BF16)
    pw = pw_ref[0].astype(BF16)
    sc = sc_ref[...]

    def chunk(r0, first):
        cur = u_ref[pl.ds(r0, R), :]
        wide = _dot(b_cur, jnp.concatenate(_split3(cur), axis=1))
        if not first:
            prev = u_ref[pl.ds(pl.multiple_of(r0 - HALO, HALO), HALO), :]
            wide = wide + _dot(b_prev, jnp.concatenate(_split3(prev), axis=1))
        count = jnp.minimum(r0 + lax.broadcasted_iota(jnp.int32, (R, 1), 0) + 1, w).astype(F32)
        pooled = (_lane_sum(wide, 3) / count - cur).astype(BF16)
        pooled_ref[pl.ds(r0, R), :] = pooled
        y_ref[pl.ds(r0, R), :] = _dot(pooled, pw) * sc

    chunk(0, True)

    def rest(c, carry):
        chunk(pl.multiple_of(c * R, R), False)
        return carry

    lax.fori_loop(1, S // R, rest, 0, unroll=POOL_UNROLL)


ATTN_WINDOW = 3
ATTN_TOGETHER = 2


def _band_ones(T):
    row = lax.broadcasted_iota(jnp.int32, (T, T), 0)
    col = lax.broadcasted_iota(jnp.int32, (T, T), 1)
    after = jnp.where(row > col, 1.0, 0.0).astype(BF16)
    before = jnp.where(row < col, 1.0, 0.0).astype(BF16)
    return jnp.concatenate([after, after], axis=0), jnp.concatenate([before, before], axis=0), col < row


def _log1m(z, mask):
    sp = jnp.log(1.0 + jnp.exp(-jnp.abs(z)))
    l1m = -jnp.maximum(z, 0.0) - sp
    return (l1m if mask is None else jnp.where(mask, l1m, 0.0)), sp


def _log_gates_of(jobs, after2):
    zs = [_dot_nt(qh, kb) for qh, kb, _ in jobs]
    terms = []
    for z, (_, _, mask) in zip(zs, jobs):
        l1m, sp = _log1m(z, mask)
        terms.append(
            (jnp.minimum(z, 0.0) - sp, jnp.concatenate(_split2(l1m), axis=1), jnp.sum(l1m, axis=1, keepdims=True))
        )
    return [(ls, _dot(split, after2), total) for ls, split, total in terms]


def _log_gates(qhs, kbs, masks, after2):
    flat = _log_gates_of([(qh, kb, mask) for kb, mask in zip(kbs, masks) for qh in qhs], after2)
    return [flat[len(qhs) * b : len(qhs) * (b + 1)] for b in range(len(kbs))]


def _tile_start(index, T):
    return index * T if isinstance(index, int) else pl.multiple_of(index * T, T)


def _weights(ls, tail, carry, mask):
    a = jnp.exp(ls + tail + carry)
    return a if mask is None else jnp.where(mask, a, 0.0)


def _reaches(carry):
    return jnp.max(carry) > EXP_UNDERFLOW


def _mixers_fwd(q, k, v, u, pool_w, scale, shards):
    S = q.shape[0]
    T = ATTN_TILE
    nq = S // T
    n = len(shards)
    steps = D_ATTN // LANES

    def body(q_ref, k_ref, v_ref, u_ref, pw_ref, sc_ref, *rest):
        o_ref, pooled_ref, yp_ref = rest[n : n + 3]
        step = pl.program_id(0)

        def gather():
            return _two_level_gather(rest[:n], rest[n + 3 : 2 * n + 3], rest[2 * n + 3 :])

        @pl.when(step == 0)
        def _():
            gather()[0]()

        _pool_fwd_group(u_ref, pw_ref, sc_ref, pooled_ref, yp_ref, step)

        lane = lax.broadcasted_iota(jnp.int32, (T, LANES), 1)
        heads = (lane < HEAD_DIM, lane >= HEAD_DIM)
        after2, _, causal = _band_ones(T)
        zero = jnp.zeros((T, 1), F32)

        def key_tile(k0):
            vb = v_ref[pl.ds(k0, T), :]
            return k_ref[pl.ds(k0, T), :], jnp.concatenate([jnp.where(m, vb, jnp.zeros_like(vb)) for m in heads], axis=0)

        def q_tiles(qis, nb):
            queries, jobs = [], []
            for qi in qis:
                q0 = _tile_start(qi, T)
                qb = q_ref[pl.ds(q0, T), :]
                qhs = [jnp.where(m, qb, jnp.zeros_like(qb)) for m in heads]
                tiles = [key_tile(_tile_start(qi - b, T)) for b in range(nb)]
                queries.append((q0, qhs, [v_rows for _, v_rows in tiles]))
                jobs += [(qh, kb, causal if b == 0 else None) for b, (kb, _) in enumerate(tiles) for qh in qhs]
            gates = _log_gates_of(jobs, after2)
            states = []
            for i, (q0, qhs, values) in enumerate(queries):
                carries, probs = [zero, zero], []
                for b in range(nb):
                    for h in range(2):
                        ls, tail, total = gates[2 * nb * i + 2 * b + h]
                        probs.append(_weights(ls, tail, carries[h], causal if b == 0 else None).astype(BF16))
                        carries[h] = carries[h] + total
                states.append((carries, _dot(jnp.concatenate(probs, axis=1), jnp.concatenate(values, axis=0))))

            def more(st):
                return (st[0] > 0) & _reaches(jnp.maximum(st[1], st[2]))

            for qi, (q0, qhs, _), (carries, acc) in zip(qis, queries, states):

                def k_step(st, qhs=qhs):
                    kj = st[0] - 1
                    kb, v_rows = key_tile(pl.multiple_of(kj * T, T))
                    new, probs = [], []
                    for (ls, tail, total), carry in zip(_log_gates(qhs, [kb], [None], after2)[0], st[1:3]):
                        probs.append(_weights(ls, tail, carry, None).astype(BF16))
                        new.append(carry + total)
                    return kj, new[0], new[1], st[3] + _dot(jnp.concatenate(probs, axis=1), v_rows)

                if not isinstance(qi, int):
                    acc = lax.while_loop(more, k_step, (qi - (nb - 1), carries[0], carries[1], acc))[3]
                o_ref[pl.ds(q0, T), :] = acc

        head_tiles = min(ATTN_WINDOW - 1, nq)
        for qi in range(head_tiles):
            q_tiles([qi], qi + 1)
        assert (nq - head_tiles) % ATTN_TOGETHER == 0

        def q_loop(i, carry):
            first = head_tiles + ATTN_TOGETHER * i
            q_tiles([first + j for j in range(ATTN_TOGETHER)], ATTN_WINDOW)
            return carry

        lax.fori_loop(0, (nq - head_tiles) // ATTN_TOGETHER, q_loop, 0)

        @pl.when(step == steps - 1)
        def _():
            _, forward, finish = gather()
            forward()
            finish()

    blk = pl.BlockSpec((S, LANES), lambda p: (0, p))
    return pl.pallas_call(
        body,
        name="mixers_fwd",
        grid=(steps,),
        out_shape=[
            jax.ShapeDtypeStruct((S, D_ATTN), F32),
            jax.ShapeDtypeStruct((S, D_POOL), BF16),
            jax.ShapeDtypeStruct((S, D_POOL), F32),
        ]
        + _exchange_shapes(shards, True),
        in_specs=[blk, blk, blk, blk, _POOL_W_SPEC, _POOL_SCALE_SPEC] + [ANY_SPEC] * n,
        out_specs=[blk, blk, blk] + [ANY_SPEC] * n,
        scratch_shapes=_exchange_sems(n),
        compiler_params=pltpu.CompilerParams(
            dimension_semantics=("arbitrary",), vmem_limit_bytes=VMEM_LIMIT, has_side_effects=True
        ),
    )(q, k, v, u, pool_w, scale, *shards)


MLP_AHEAD = 1
SG_ROWS = 8


def _mlp_fwd_bwd(x, y_pool, y_attn, target, g_pool, g_attn, g2, gf, w_out, w_up, w_down):
    S = x.shape[0]
    tm = min(256, S)
    fc = D_FF // N_DEV

    def body(x_ref, yp_ref, ya_ref, t_ref, gp_ref, ga_ref, g2_ref, gf_ref, wo_hbm, wu_hbm, wd_hbm,
             mixed_ref, hn2_ref, act_ref, dup_ref, dh2_ref, dh1_ref, dh1b_ref, dyp_ref, dya_ref, sg_ref,
             wo, wu, wd, up_s, sems):
        @pl.when(pl.program_id(0) == 0)
        def _():
            copies = [
                pltpu.make_async_copy(wo_hbm, wo, sems.at[0]),
                pltpu.make_async_copy(wu_hbm, wu, sems.at[1]),
                pltpu.make_async_copy(wd_hbm, wd, sems.at[2]),
            ]
            for cp in copies:
                cp.start()
            for cp in copies:
                cp.wait()
            sg_ref[...] = jnp.zeros_like(sg_ref)

        gp, ga, g2v, gfv = gp_ref[...], ga_ref[...], g2_ref[...], gf_ref[...]
        yp, ya = yp_ref[...], ya_ref[...]
        rp, ra = _rstd(yp), _rstd(ya)
        yph, yah = yp * rp, ya * ra
        mixed = jnp.concatenate([(yph * gp).astype(BF16), (yah * ga).astype(BF16)], axis=1)
        mixed_ref[...] = mixed
        h1 = x_ref[...] + _dot(mixed, wo[...])
        r2 = _rstd(h1)
        h1h = h1 * r2
        hn2 = (h1h * g2v).astype(BF16)
        hn2_ref[...] = hn2
        h2 = h1
        ups = [_dot(hn2, wu[j]) for j in range(MLP_AHEAD)]
        for j in range(N_DEV):
            cols = slice(fc * j, fc * (j + 1))
            if j + MLP_AHEAD < N_DEV:
                ups.append(_dot(hn2, wu[j + MLP_AHEAD]))
            up = ups.pop(0)
            up_s[:, cols] = up
            act = jnp.square(jnp.maximum(up, 0.0)).astype(BF16)
            act_ref[:, cols] = act
            h2 = h2 + _dot(act, wd[cols, :])
        rf = _rstd(h2)
        h2h = h2 * rf
        diff = h2h * gfv - t_ref[...]
        loss_rows = 0.5 * jnp.mean(diff * diff, axis=-1, keepdims=True)
        dy = diff * (1.0 / D_MODEL)
        dh2, dgf = _rms_bwd(dy, h2h, rf, gfv)
        dh2b = dh2.astype(BF16)
        dh2_ref[...] = dh2b
        dhn2 = jnp.zeros((tm, D_MODEL), F32)
        dacts = [_dot_nt(dh2b, wd[fc * j : fc * (j + 1), :]) for j in range(MLP_AHEAD)]
        for j in range(N_DEV):
            cols = slice(fc * j, fc * (j + 1))
            if j + MLP_AHEAD < N_DEV:
                dacts.append(_dot_nt(dh2b, wd[fc * (j + MLP_AHEAD) : fc * (j + MLP_AHEAD + 1), :]))
            dup = (dacts.pop(0) * (2.0 * jnp.maximum(up_s[:, cols], 0.0))).astype(BF16)
            dup_ref[:, cols] = dup
            dhn2 = dhn2 + _dot_nt(dup, wu[j])
        dh1n, dg2 = _rms_bwd(dhn2, h1h, r2, g2v)
        dh1 = dh2 + dh1n
        dh1_ref[...] = dh1
        dh1b = dh1.astype(BF16)
        dh1b_ref[...] = dh1b
        dmix = _dot_nt(dh1b, wo[...])
        dyp, dgp = _rms_bwd(dmix[:, :D_POOL], yph, rp, gp)
        dya, dga = _rms_bwd(dmix[:, D_POOL:], yah, ra, ga)
        dyp_ref[...] = dyp
        dya_ref[...] = dya
        sg_ref[0:1, :] += dgf
        sg_ref[1:2, :] += dg2
        sg_ref[2:3, :] += jnp.concatenate([dgp, dga], axis=1)
        sg_ref[3:4, :] += jnp.broadcast_to(jnp.sum(loss_rows, axis=0, keepdims=True), (1, D_MODEL))

    def tok(n):
        return pl.BlockSpec((tm, n), lambda i: (i, 0))

    def vec(n):
        return pl.BlockSpec((1, n), lambda i: (0, 0))

    any_spec = pl.BlockSpec(memory_space=pl.ANY)
    return pl.pallas_call(
        body,
        name="mlp_fwd_bwd",
        grid=(S // tm,),
        out_shape=[
            jax.ShapeDtypeStruct((S, D_MODEL), BF16),
            jax.ShapeDtypeStruct((S, D_MODEL), BF16),
            jax.ShapeDtypeStruct((S, D_FF), BF16),
            jax.ShapeDtypeStruct((S, D_FF), BF16),
            jax.ShapeDtypeStruct((S, D_MODEL), BF16),
            jax.ShapeDtypeStruct((S, D_MODEL), F32),
            jax.ShapeDtypeStruct((S, D_MODEL), BF16),
            jax.ShapeDtypeStruct((S, D_POOL), F32),
            jax.ShapeDtypeStruct((S, D_ATTN), F32),
            jax.ShapeDtypeStruct((SG_ROWS, D_MODEL), F32),
        ],
        in_specs=[tok(D_MODEL), tok(D_POOL), tok(D_ATTN), tok(D_MODEL), vec(D_POOL), vec(D_ATTN),
                  vec(D_MODEL), vec(D_MODEL), any_spec, any_spec, any_spec],
        out_specs=[tok(D_MODEL), tok(D_MODEL), tok(D_FF), tok(D_FF), tok(D_MODEL), tok(D_MODEL),
                   tok(D_MODEL), tok(D_POOL), tok(D_ATTN),
                   pl.BlockSpec((SG_ROWS, D_MODEL), lambda i: (0, 0))],
        scratch_shapes=[
            pltpu.VMEM((D_MODEL, D_MODEL), BF16),
            pltpu.VMEM((N_DEV, D_MODEL, fc), BF16),
            pltpu.VMEM((D_FF, D_MODEL), BF16),
            pltpu.VMEM((tm, D_FF), F32),
            pltpu.SemaphoreType.DMA((3,)),
        ],
        compiler_params=_params(("arbitrary",)),
    )(x, y_pool, y_attn, target, g_pool, g_attn, g2, gf, w_out, w_up, w_down)


def _wgrad(a, b, block_a, groups, name, travelling=()):
    n = len(travelling)
    S, ka = a.shape
    nb = b.shape[1]
    ts = min(2048, S)
    per = N_DEV // groups
    if block_a:
        ka //= groups
        blk = (ka // per, nb)
        a_spec = pl.BlockSpec((ts, ka), lambda g, s: (s, g))
        b_spec = pl.BlockSpec((ts, nb), lambda g, s: (s, 0))
    else:
        nb //= groups
        blk = (ka, nb // per)
        a_spec = pl.BlockSpec((ts, ka), lambda g, s: (s, 0))
        b_spec = pl.BlockSpec((ts, nb), lambda g, s: (s, g))
    steps = S // ts

    def body(a_ref, b_ref, *rest):
        o_ref, acc = rest[n], rest[2 * n + 1]
        g, s = pl.program_id(0), pl.program_id(1)

        def scatter():
            return _direct_exchange(rest[:n], rest[n + 1 : 2 * n + 1], rest[2 * n + 2 :], False)

        if n:
            @pl.when((g == 0) & (s == 0))
            def _():
                scatter()[0]()

        @pl.when(s == 0)
        def _():
            acc[...] = jnp.zeros_like(acc)

        acc[...] += _dot_tn(a_ref[...], b_ref[...])

        @pl.when(s == steps - 1)
        def _():
            for j in range(per):
                if block_a:
                    o_ref[j] = acc[blk[0] * j : blk[0] * (j + 1), :].astype(BF16)
                else:
                    o_ref[j] = acc[:, blk[1] * j : blk[1] * (j + 1)].astype(BF16)

        if n:
            @pl.when((g == groups - 1) & (s == steps - 1))
            def _():
                scatter()[1]()

    results = pl.pallas_call(
        body,
        name=name,
        grid=(groups, steps),
        out_shape=[jax.ShapeDtypeStruct((N_DEV,) + blk, BF16)] + _exchange_shapes(travelling, False),
        in_specs=[a_spec, b_spec] + [ANY_SPEC] * n,
        out_specs=[pl.BlockSpec((per,) + blk, lambda g, s: (g, 0, 0))] + [ANY_SPEC] * n,
        scratch_shapes=[pltpu.VMEM((ka, nb), F32)] + (_exchange_sems(n) if n else []),
        compiler_params=pltpu.CompilerParams(
            dimension_semantics=("arbitrary", "arbitrary"), vmem_limit_bytes=VMEM_LIMIT, has_side_effects=bool(n)
        ),
    )(a, b, *travelling)
    return results if n else results[0]


def _mixers_bwd(q, k, v, do, dyp, pooled, pool_w, scale, partials):
    S = q.shape[0]
    T = ATTN_TILE
    nq = S // T
    n = len(partials)
    steps = D_ATTN // LANES

    def body(q_ref, k_ref, v_ref, do_ref, dyp_ref, pooled_ref, pw_ref, sc_ref, *rest):
        dq_ref, dk_ref, dv_ref, du_ref, dsc_ref, dpw_ref = rest[n : n + 6]
        dk_acc, dv_acc, carry_s = rest[2 * n + 6 : 2 * n + 9]
        step = pl.program_id(0)

        def scatter():
            return _direct_exchange(rest[:n], rest[n + 6 : 2 * n + 6], rest[2 * n + 9 :], False)

        @pl.when(step == 0)
        def _():
            scatter()[0]()

        _pool_bwd_group(dyp_ref, pooled_ref, pw_ref, sc_ref, du_ref, dsc_ref, dpw_ref, step)

        dk_acc[...] = jnp.zeros_like(dk_acc)
        dv_acc[...] = jnp.zeros_like(dv_acc)
        lane = lax.broadcasted_iota(jnp.int32, (T, LANES), 1)
        heads = (lane < HEAD_DIM, lane >= HEAD_DIM)
        after2, before2, causal = _band_ones(T)
        zero = jnp.zeros((T, 1), F32)

        def key_tile(k0):
            kb = k_ref[pl.ds(k0, T), :]
            k_rows = jnp.concatenate([jnp.where(m, kb, jnp.zeros_like(kb)) for m in heads], axis=0)
            return kb, v_ref[pl.ds(k0, T), :], k_rows

        def grads(items):
            das = [[[_dot_nt(doh, vb) for doh in qs[1]] for _, vb, _ in tiles] for qs, _, tiles, _, _, _, _ in items]
            probs, gs = [], []
            for (_, _, tiles, masks, carry_in, gates, _), da in zip(items, das):
                probs.append([[_weights(gates[b][h][0], gates[b][h][1], carry_in[b][h], masks[b]) for h in range(2)]
                              for b in range(len(tiles))])
                gs.append([[probs[-1][b][h] * da[b][h] for h in range(2)] for b in range(len(tiles))])
            before = [[[_dot(jnp.concatenate(_split2(g), axis=1), before2) for g in row] for row in item] for item in gs]
            dzs, lefts = [], []
            for i, (_, _, tiles, masks, _, gates, st) in enumerate(items):
                g_left, item_dz = [st[0], st[1]], [None] * len(tiles)
                for b in reversed(range(len(tiles))):
                    item_dz[b] = []
                    for h in range(2):
                        sig = jnp.exp(gates[b][h][0])
                        dz = gs[i][b][h] * (1.0 - sig) - sig * (g_left[h] + before[i][b][h])
                        if masks[b] is not None:
                            dz = jnp.where(masks[b], dz, 0.0)
                        item_dz[b].append(dz.astype(BF16))
                        g_left[h] = g_left[h] + jnp.sum(gs[i][b][h], axis=1, keepdims=True)
                dzs.append(item_dz)
                lefts.append(g_left)
            out = []
            for (_, _, tiles, _, _, _, st), item_dz, g_left in zip(items, dzs, lefts):
                dq = st[2] + _dot(
                    jnp.concatenate([dz for row in item_dz for dz in row], axis=1),
                    jnp.concatenate([k_rows for _, _, k_rows in tiles], axis=0),
                )
                out.append((g_left[0], g_left[1], dq))
            for (qs, starts, tiles, _, _, _, _), item_dz, item_probs in zip(items, dzs, probs):
                for b in range(len(tiles)):
                    dk_acc[pl.ds(starts[b], T), :] += _dot_tn(jnp.concatenate(item_dz[b], axis=0), qs[2])
                    dv_acc[pl.ds(starts[b], T), :] += _dot_tn(
                        jnp.concatenate([a.astype(BF16) for a in item_probs[b]], axis=0), qs[3]
                    )
            return out

        def more(st):
            return (st[0] > 0) & _reaches(jnp.maximum(st[1], st[2]))

        def q_tiles(qis, nb):
            masks = [causal] + [None] * (nb - 1)
            prepared, jobs = [], []
            for qi in qis:
                q0 = _tile_start(qi, T)
                qb = q_ref[pl.ds(q0, T), :]
                dob = do_ref[pl.ds(q0, T), :]
                qhs = [jnp.where(m, qb, jnp.zeros_like(qb)) for m in heads]
                dohs = [jnp.where(m, dob, 0.0).astype(BF16) for m in heads]
                queries = (qhs, dohs, jnp.concatenate(qhs, axis=0), jnp.concatenate(dohs, axis=0))
                starts = [_tile_start(qi - b, T) for b in range(nb)]
                tiles = [key_tile(k0) for k0 in starts]
                prepared.append((q0, queries, starts, tiles))
                jobs += [(qh, kb, mask) for (kb, _, _), mask in zip(tiles, masks) for qh in qhs]
            flat = _log_gates_of(jobs, after2)
            items = []
            for i, (qi, (q0, queries, starts, tiles)) in enumerate(zip(qis, prepared)):
                gates = [flat[2 * nb * i + 2 * b : 2 * nb * i + 2 * b + 2] for b in range(nb)]
                carries, carry_in = [zero, zero], []
                for b in range(nb):
                    carry_in.append(list(carries))
                    carries = [carries[h] + gates[b][h][2] for h in range(2)]
                st = (zero, zero, jnp.zeros((T, LANES), F32))
                if not isinstance(qi, int):
                    k_left = qi - (nb - 1)
                    qhs = queries[0]

                    def right_to_left(st, qhs=qhs):
                        kj = st[0] - 1
                        carry_s[0, kj] = st[1]
                        carry_s[1, kj] = st[2]
                        kb = k_ref[pl.ds(pl.multiple_of(kj * T, T), T), :]
                        sums = [jnp.sum(_log1m(_dot_nt(qh, kb), None)[0], axis=1, keepdims=True) for qh in qhs]
                        return kj, st[1] + sums[0], st[2] + sums[1]

                    k_first = lax.while_loop(more, right_to_left, (k_left, carries[0], carries[1]))[0]

                    def left_to_right(kj, st, queries=queries):
                        k0 = pl.multiple_of(kj * T, T)
                        tile = key_tile(k0)
                        left_gates = _log_gates(queries[0], [tile[0]], [None], after2)
                        carry = [[carry_s[0, kj], carry_s[1, kj]]]
                        return grads([(queries, [k0], [tile], [None], carry, left_gates, st)])[0]

                    st = lax.fori_loop(k_first, k_left, left_to_right, st)
                items.append((queries, starts, tiles, masks, carry_in, gates, st))
            for (q0, _, _, _), st in zip(prepared, grads(items)):
                dq_ref[pl.ds(q0, T), :] = (st[2] * (HEAD_DIM**-0.5)).astype(BF16)

        head_tiles = min(ATTN_WINDOW - 1, nq)
        for qi in range(head_tiles):
            q_tiles([qi], qi + 1)
        assert (nq - head_tiles) % ATTN_TOGETHER == 0

        def q_loop(i, carry):
            first = head_tiles + ATTN_TOGETHER * i
            q_tiles([first + j for j in range(ATTN_TOGETHER)], ATTN_WINDOW)
            return carry

        lax.fori_loop(0, (nq - head_tiles) // ATTN_TOGETHER, q_loop, 0)
        dk_ref[...] = dk_acc[...].astype(BF16)
        dv_ref[...] = dv_acc[...].astype(BF16)

        @pl.when(step == steps - 1)
        def _():
            scatter()[1]()

    blk = pl.BlockSpec((S, LANES), lambda p: (0, p))
    sds = jax.ShapeDtypeStruct((S, D_ATTN), BF16)
    return pl.pallas_call(
        body,
        name="mixers_bwd",
        grid=(steps,),
        out_shape=[
            sds,
            sds,
            sds,
            jax.ShapeDtypeStruct((S, D_POOL), BF16),
            jax.ShapeDtypeStruct((8, D_POOL), F32),
            jax.ShapeDtypeStruct((N_GROUPS, GROUP_DIM, GROUP_DIM), BF16),
        ]
        + _exchange_shapes(partials, False),
        in_specs=[blk] * 6 + [_POOL_W_SPEC, _POOL_SCALE_SPEC] + [ANY_SPEC] * n,
        out_specs=[blk] * 4 + [pl.BlockSpec((8, GROUP_DIM), lambda p: (0, p)), _POOL_W_SPEC] + [ANY_SPEC] * n,
        scratch_shapes=[pltpu.VMEM((S, LANES), F32), pltpu.VMEM((S, LANES), F32), pltpu.VMEM((2, nq, T, 1), F32)]
        + _exchange_sems(n),
        compiler_params=pltpu.CompilerParams(
            dimension_semantics=("arbitrary",), vmem_limit_bytes=VMEM_LIMIT, has_side_effects=True
        ),
    )(q, k, v, do, dyp, pooled, pool_w, scale, *partials)


def _pool_bwd_group(dy_ref, pooled_ref, pw_ref, sc_ref, du_ref, dsc_ref, dpw_ref, g):
    S = dy_ref.shape[0]
    R = min(POOL_CHUNK, S)
    nc = S // R
    w = jnp.left_shift(jnp.int32(2), g)
    pw = pw_ref[0].astype(BF16)
    sc = sc_ref[...]
    d = lax.broadcasted_iota(jnp.int32, (R, R), 1) - lax.broadcasted_iota(jnp.int32, (R, R), 0)
    bt_cur = jnp.where((d >= 0) & (d < w), 1.0, 0.0).astype(BF16)
    dn = lax.broadcasted_iota(jnp.int32, (R, HALO), 1) + R - lax.broadcasted_iota(jnp.int32, (R, HALO), 0)
    bt_next = jnp.where(dn < w, 1.0, 0.0).astype(BF16)

    def per_count(dpl, r0):
        n = dpl.shape[0]
        count = jnp.minimum(r0 + lax.broadcasted_iota(jnp.int32, (n, 1), 0) + 1, w).astype(F32)
        return jnp.concatenate(_split2(dpl / count), axis=1)

    def chunk(r0, last, sums):
        dyv = dy_ref[pl.ds(r0, R), :]
        pooled = pooled_ref[pl.ds(r0, R), :]
        dmapped = (dyv * sc).astype(BF16)
        dsc = sums[0] + jnp.sum(dyv * _dot(pooled, pw), axis=0, keepdims=True)
        dpw = sums[1] + _dot_tn(pooled, dmapped)
        dpl = _dot_nt(dmapped, pw)
        wide = _dot(bt_cur, per_count(dpl, r0))
        if not last:
            nxt = pl.multiple_of(r0 + R, R)
            dpl_next = _dot_nt((dy_ref[pl.ds(nxt, HALO), :] * sc).astype(BF16), pw)
            wide = wide + _dot(bt_next, per_count(dpl_next, nxt))
        du_ref[pl.ds(r0, R), :] = (_lane_sum(wide, 2) - dpl).astype(BF16)
        return dsc, dpw

    sums = (jnp.zeros((1, GROUP_DIM), F32), jnp.zeros((GROUP_DIM, GROUP_DIM), F32))
    sums = lax.fori_loop(
        0, nc - 1, lambda c, s: chunk(pl.multiple_of(c * R, R), False, s), sums, unroll=POOL_UNROLL
    )
    dsc, dpw = chunk((nc - 1) * R, True, sums)
    dsc_ref[...] = jnp.zeros_like(dsc_ref)
    dsc_ref[0:1, :] = dsc
    dpw_ref[0] = dpw.astype(dpw_ref.dtype)


def _bwd_in(du, dq, dk, dv, w_in, x, dh1, g1):
    S = x.shape[0]
    tm = min(512, S)

    def body(du_ref, dq_ref, dk_ref, dv_ref, w_ref, x_ref, dh1_ref, g_ref, dx_ref, dproj_ref, dg_ref):
        @pl.when(pl.program_id(0) == 0)
        def _():
            dg_ref[...] = jnp.zeros_like(dg_ref)

        parts = (du_ref[...], dq_ref[...], dk_ref[...], dv_ref[...])
        dhn = jnp.zeros((tm, D_MODEL), F32)
        for j in range(N_DEV):
            piece = parts[j // 2][:, 256 * (j % 2) : 256 * (j % 2 + 1)]
            dproj_ref[:, 256 * j : 256 * (j + 1)] = piece
            dhn = dhn + _dot_nt(piece, w_ref[j])
        xv = x_ref[...]
        r = _rstd(xv)
        dxn, dg = _rms_bwd(dhn, xv * r, r, g_ref[...])
        dx_ref[...] = dh1_ref[...] + dxn
        dg_ref[0:1, :] += dg

    half = pl.BlockSpec((tm, D_POOL), lambda i: (i, 0))
    full = pl.BlockSpec((tm, D_MODEL), lambda i: (i, 0))
    return pl.pallas_call(
        body,
        name="bwd_in",
        grid=(S // tm,),
        out_shape=[
            jax.ShapeDtypeStruct((S, D_MODEL), F32),
            jax.ShapeDtypeStruct((S, D_IN_PROJ), BF16),
            jax.ShapeDtypeStruct((8, D_MODEL), F32),
        ],
        in_specs=[half, half, half, half,
                  pl.BlockSpec((N_DEV, D_MODEL, 256), lambda i: (0, 0, 0)),
                  full, full, pl.BlockSpec((1, D_MODEL), lambda i: (0, 0))],
        out_specs=[full, pl.BlockSpec((tm, D_IN_PROJ), lambda i: (i, 0)),
                   pl.BlockSpec((8, D_MODEL), lambda i: (0, 0))],
        compiler_params=_params(("arbitrary",)),
    )(du, dq, dk, dv, w_in, x, dh1, g1)


def _rows(a):
    a = a.reshape(-1, LANES)
    pad = (-a.shape[0]) % 8
    return jnp.pad(a, ((0, pad), (0, 0))) if pad else a


def kernel(x, norm1_g, w_in, pool_w, pool_scale, pool_out_g, attn_out_g, w_out, norm2_g, w_up, w_down, final_g, loss_target, m_norm1_g, m_w_in, m_pool_w, m_pool_scale, m_pool_out_g, m_attn_out_g, m_w_out, m_norm2_g, m_w_up, m_w_down, m_final_g, v_norm1_g, v_w_in, v_pool_w, v_pool_scale, v_pool_out_g, v_attn_out_g, v_w_out, v_norm2_g, v_w_up, v_w_down, v_final_g):
    S = x.shape[1]
    xs = x.reshape(S, D_MODEL)
    tgt = loss_target.reshape(S, D_MODEL)
    row = lambda a: a.reshape(1, -1)

    (w_in_g,) = _exchange([w_in.astype(BF16)], True, "gather_w_in")
    hn, u_pool, q, k, v = _fwd_in(xs, row(norm1_g), w_in_g)
    y_attn, pooled, y_pool, w_out_g, w_up_g, w_down_g = _mixers_fwd(
        q, k, v, u_pool, pool_w, row(pool_scale), [w_out.astype(BF16), w_up.astype(BF16), w_down.astype(BF16)]
    )
    w_out_full = w_out_g.reshape(D_MODEL, D_MODEL)
    w_down_full = w_down_g.reshape(D_FF, D_MODEL)
    mixed, hn2, act, dup, dh2b, dh1, dh1b, dyp, dya, sg = _mlp_fwd_bwd(
        xs, y_pool, y_attn, tgt, row(pool_out_g), row(attn_out_g), row(norm2_g), row(final_g),
        w_out_full, w_up_g, w_down_full,
    )
    gp_down = _wgrad(act, dh2b, True, 2, "wgrad_down")
    gp_up = _wgrad(hn2, dup, False, 2, "wgrad_up")
    gp_out = _wgrad(mixed, dh1b, True, 1, "wgrad_out")
    dq, dk, dv, du, dsc, dpw, land_up, land_down = _mixers_bwd(
        q, k, v, dya, dyp, pooled, pool_w, row(pool_scale), [gp_up, gp_down]
    )
    dx, dproj, dg1 = _bwd_in(du, dq, dk, dv, w_in_g, xs, dh1, row(norm1_g))
    gp_in, land_out = _wgrad(hn, dproj, False, 1, "wgrad_in", [gp_out])

    def vectors(final, norm2, pool_out, attn_out, norm1, scale, last):
        pieces = [final, norm2, jnp.concatenate([pool_out, attn_out]), norm1, scale, last]
        return jnp.concatenate([_rows(p) for p in pieces], axis=0)

    no_loss = jnp.zeros((8, LANES), F32)
    flat_pool = lambda a: a.reshape(N_GROUPS * GROUP_DIM, GROUP_DIM)
    smalls = [
        (
            vectors(sg[0], sg[1], sg[2][:D_POOL], sg[2][D_POOL:], dg1[0], dsc[0], sg[3]),
            [(
                vectors(final_g, norm2_g, pool_out_g, attn_out_g, norm1_g, pool_scale, no_loss),
                vectors(m_final_g, m_norm2_g, m_pool_out_g, m_attn_out_g, m_norm1_g, m_pool_scale, no_loss),
                vectors(v_final_g, v_norm2_g, v_pool_out_g, v_attn_out_g, v_norm1_g, v_pool_scale, no_loss),
                None,
            )],
        ),
        (flat_pool(dpw), [(flat_pool(pool_w), flat_pool(m_pool_w), flat_pool(v_pool_w), None)]),
    ]
    tail = _reduce_adam_tail(
        [land_out, land_up, land_down], [w_out, w_up, w_down], [m_w_out, m_w_up, m_w_down],
        [v_w_out, v_w_up, v_w_down], gp_in, (w_in, m_w_in, v_w_in), smalls,
    )
    big = {name: tail[4 * t : 4 * t + 4] for t, name in enumerate(("w_out", "w_up", "w_down", "w_in"))}

    def unpack(vec, pw):
        out = {}
        for i, name in enumerate(("final_g", "norm2_g", "mix_g", "norm1_g", "pool_scale")):
            out[name] = vec[8 * i : 8 * i + 8].reshape(-1)
        out["pool_scale"] = out["pool_scale"][:D_POOL]
        out["pool_out_g"], out["attn_out_g"] = out["mix_g"][:D_POOL], out["mix_g"][D_POOL:]
        out["pool_w"] = pw.reshape(N_GROUPS, GROUP_DIM, GROUP_DIM)
        return out, vec[40, 0]

    small_out = [unpack(tail[16 + i], tail[20 + i]) for i in range(4)]
    loss = small_out[0][1]
    order = ("norm1_g", "w_in", "pool_w", "pool_scale", "pool_out_g", "attn_out_g", "w_out", "norm2_g", "w_up",
             "w_down", "final_g")
    outs = [loss, dx.reshape(1, S, D_MODEL)]
    for i in range(4):
        for name in order:
            outs.append(big[name][i] if name in big else small_out[i][0][name])
    return tuple(outs)
```

```python
import jax
import jax.numpy as jnp
from jax import lax
from jax.experimental import pallas as pl
from jax.experimental.pallas import tpu as pltpu

F32 = jnp.float32
BF16 = jnp.bfloat16
MESH = pl.DeviceIdType.MESH

N_DEV = 8
D_MODEL = 1024
D_POOL = 512
D_ATTN = 512
N_GROUPS = 4
GROUP_DIM = 128
HEAD_DIM = 64
D_FF = 4096
D_IN_PROJ = 2048
EPS = 1e-6
HALO = 16
ATTN_TILE = 128
LANES = 128
EXP_UNDERFLOW = -104.0

ADAM_LR = 0.001
ADAM_B1 = 0.9
ADAM_B2 = 0.999
ADAM_EPS = 1e-08
ADAM_WD = 0.01
ADAM_STEP = 10

VMEM_LIMIT = 56 * 1024 * 1024


def _params(semantics=None, vmem=VMEM_LIMIT):
    return pltpu.CompilerParams(dimension_semantics=semantics, vmem_limit_bytes=vmem)


def _dot(a, b):
    return jnp.dot(a, b, preferred_element_type=F32)


def _dot_nt(a, b):
    return lax.dot_general(a, b, (((1,), (1,)), ((), ())), preferred_element_type=F32)


def _dot_tn(a, b):
    return lax.dot_general(a, b, (((0,), (0,)), ((), ())), preferred_element_type=F32)


def _split2(x):
    hi = x.astype(BF16)
    lo = (x - hi.astype(F32)).astype(BF16)
    return hi, lo


def _split3(x):
    hi = x.astype(BF16)
    r = x - hi.astype(F32)
    mid = r.astype(BF16)
    lo = (r - mid.astype(F32)).astype(BF16)
    return hi, mid, lo


def _rstd(h):
    return lax.rsqrt(jnp.mean(h * h, axis=-1, keepdims=True) + EPS)


def _rms_bwd(dout, hhat, r, g):
    dg = jnp.sum(dout * hhat, axis=0, keepdims=True)
    dxh = dout * g
    dh = r * (dxh - hhat * jnp.mean(dxh * hhat, axis=-1, keepdims=True))
    return dh, dg


def _my_index():
    return 4 * lax.axis_index("x") + 2 * lax.axis_index("y") + lax.axis_index("c")


def _peer(k):
    x, y, c = lax.axis_index("x"), lax.axis_index("y"), lax.axis_index("c")
    px = 1 - x if (k >> 2) & 1 else x
    py = 1 - y if (k >> 1) & 1 else y
    pc = 1 - c if k & 1 else c
    return (px, py, pc), 4 * px + 2 * py + pc


N_PEERS = N_DEV - 1
ANY_SPEC = pl.BlockSpec(memory_space=pl.ANY)


def _exchange_sems(n):
    return [
        pltpu.SemaphoreType.DMA((n * N_PEERS,)),
        pltpu.SemaphoreType.DMA((n * N_PEERS,)),
        pltpu.SemaphoreType.DMA((n,)),
    ]


def _exchange_shapes(blocks, gather):
    if gather:
        return [jax.ShapeDtypeStruct((N_DEV,) + b.shape, b.dtype) for b in blocks]
    return [jax.ShapeDtypeStruct(b.shape, b.dtype) for b in blocks]


def _direct_exchange(ins, outs, sems, gather):
    send_sems, recv_sems, local_sems = sems
    me = _my_index()
    own, sends, recvs = [], [], []
    for t in range(len(ins)):
        own.append(pltpu.make_async_copy(ins[t] if gather else ins[t].at[me], outs[t].at[me], local_sems.at[t]))
        for k in range(1, N_DEV):
            peer, peer_idx = _peer(k)
            src = ins[t] if gather else ins[t].at[peer_idx]
            for dst, bucket in ((outs[t].at[me], sends), (outs[t].at[peer_idx], recvs)):
                bucket.append(
                    pltpu.make_async_remote_copy(
                        src_ref=src,
                        dst_ref=dst,
                        send_sem=send_sems.at[t * N_PEERS + k - 1],
                        recv_sem=recv_sems.at[t * N_PEERS + k - 1],
                        device_id=peer,
                        device_id_type=MESH,
                    )
                )

    def start():
        for cp in own + sends:
            cp.start()

    def finish():
        for cp in recvs:
            cp.wait_recv()
        for cp in sends:
            cp.wait_send()
        for cp in own:
            cp.wait()

    return start, finish


def _flip(a, bit):
    return a + bit - 2 * a * bit


def _two_level_gather(ins, outs, sems):
    send_sems, recv_sems, local_sems = sems
    x, y, c = lax.axis_index("x"), lax.axis_index("y"), lax.axis_index("c")
    me, sibling = (x, y, c), (x, y, 1 - c)
    x_nbr, y_nbr, diagonal = (1 - x, y, c), (x, 1 - y, c), (1 - x, 1 - y, c)
    relay_of = (_flip(x, 1 - c), _flip(y, c), c)
    relay_to = (_flip(x, c), _flip(y, 1 - c), c)

    def copy(t, k, block, to, from_input=False):
        slot = outs[t].at[4 * block[0] + 2 * block[1] + block[2]]
        return pltpu.make_async_remote_copy(
            src_ref=ins[t] if from_input else slot,
            dst_ref=slot,
            send_sem=send_sems.at[t * N_PEERS + k],
            recv_sem=recv_sems.at[t * N_PEERS + k],
            device_id=to,
            device_id_type=MESH,
        )

    arrays = range(len(ins))
    own = [pltpu.make_async_copy(ins[t], outs[t].at[4 * x + 2 * y + c], local_sems.at[t]) for t in arrays]
    first = [copy(t, k, me, to, True) for t in arrays for k, to in ((1, x_nbr), (2, y_nbr), (0, sibling))]
    landed = [copy(t, k, block, me) for t in arrays for k, block in ((1, x_nbr), (2, y_nbr))]
    relays = [copy(t, 3, relay_of, relay_to) for t in arrays]
    passed = [copy(t, 3 + k, block, sibling) for t in arrays for k, block in ((1, x_nbr), (2, y_nbr))]
    relayed = [(copy(t, 3, diagonal, me), copy(t, 6, diagonal, sibling)) for t in arrays]
    last = [copy(t, 0, sibling, me) for t in arrays]
    last += [copy(t, 3 + k, (*block[:2], 1 - c), me) for t in arrays for k, block in ((1, x_nbr), (2, y_nbr), (3, diagonal))]

    def start():
        for cp in own + first:
            cp.start()

    def relay():
        for cp in landed:
            cp.wait_recv()
        for cp in relays + passed:
            cp.start()

    def forward():
        for arrived, onward in relayed:
            arrived.wait_recv()
            onward.start()

    def finish():
        for cp in last:
            cp.wait_recv()
        for cp in first + relays + passed + [onward for _, onward in relayed]:
            cp.wait_send()
        for cp in own:
            cp.wait()

    return start, relay, forward, finish


N_CHIPS = 4
PAIR_SEMS = (N_CHIPS, N_CHIPS, N_CHIPS - 1, N_CHIPS - 1)


def _pair_reduce_scatter(gp_hbm, own, pair, summed, land, local_sem, sems):
    d2d_send, d2d_recv, ici_send, ici_recv = sems
    x, y, c = lax.axis_index("x"), lax.axis_index("y"), lax.axis_index("c")
    my_chip = 2 * x + y
    sibling = (x, y, 1 - c)
    chips = [(1 - x, y), (x, 1 - y), (1 - x, 1 - y)]
    local = [pltpu.make_async_copy(gp_hbm.at[2 * j + c], own.at[j], local_sem.at[0]) for j in range(N_CHIPS)]
    to_sibling = [
        pltpu.make_async_remote_copy(
            src_ref=gp_hbm.at[2 * j + 1 - c], dst_ref=pair.at[j], send_sem=d2d_send.at[j], recv_sem=d2d_recv.at[j],
            device_id=sibling, device_id_type=MESH,
        )
        for j in range(N_CHIPS)
    ]
    to_chips, from_chips = [], []
    for r, (px, py) in enumerate(chips):
        for dst, bucket in ((land.at[my_chip], to_chips), (land.at[2 * px + py], from_chips)):
            bucket.append(
                pltpu.make_async_remote_copy(
                    src_ref=summed.at[2 * px + py], dst_ref=dst, send_sem=ici_send.at[r], recv_sem=ici_recv.at[r],
                    device_id=(px, py, c), device_id_type=MESH,
                )
            )

    def start():
        for cp in local + to_sibling:
            cp.start()

    def middle():
        for cp in to_sibling:
            cp.wait_recv()
        pltpu.make_async_copy(gp_hbm.at[pl.ds(0, N_CHIPS)], own, local_sem.at[0]).wait()
        for j in range(N_CHIPS):
            summed[j] = (own[j].astype(F32) + pair[j].astype(F32)).astype(BF16)
        for cp in to_chips:
            cp.start()

    def finish():
        for cp in from_chips:
            cp.wait_recv()
        for cp in to_chips + to_sibling:
            cp.wait_send()
        land[my_chip] = summed[my_chip]

    return start, middle, finish


def _exchange(blocks, gather, name):
    n = len(blocks)

    def body(*refs):
        if gather:
            stages = _two_level_gather(refs[:n], refs[n : 2 * n], refs[2 * n :])
        else:
            stages = _direct_exchange(refs[:n], refs[n : 2 * n], refs[2 * n :], False)
        for stage in stages:
            stage()

    return pl.pallas_call(
        body,
        name=name,
        out_shape=_exchange_shapes(blocks, gather),
        in_specs=[ANY_SPEC] * n,
        out_specs=[ANY_SPEC] * n,
        scratch_shapes=_exchange_sems(n),
        compiler_params=pltpu.CompilerParams(has_side_effects=True),
    )(*blocks)


def _adam(w, g, m, v):
    m2 = ADAM_B1 * m + (1.0 - ADAM_B1) * g
    v2 = ADAM_B2 * v + (1.0 - ADAM_B2) * jnp.square(g)
    m_hat = m2 / (1.0 - ADAM_B1**ADAM_STEP)
    v_hat = v2 / (1.0 - ADAM_B2**ADAM_STEP)
    delta = -ADAM_LR * (m_hat / (jnp.sqrt(v_hat) + ADAM_EPS) + ADAM_WD * w)
    return delta, m2, v2


TAIL_STEPS = 8


def _reduce_adam_tail(lands, ws, ms, vs, gp_last, last, smalls):
    nw, ns = len(ws), len(smalls)
    tiles = [w.shape[0] // TAIL_STEPS for w in ws]
    blk = gp_last.shape[1:]
    entries = [(i, e) for i, (_, es) in enumerate(smalls) for e in es]
    n_params = 3 * sum(e[0] is not None for _, e in entries)
    n_small_out = sum(4 if e[0] is not None else 1 for _, e in entries)

    def body(*refs):
        gp_hbm = refs[0]
        part_hbm = refs[1 : 1 + ns]
        refs = refs[1 + ns :]
        land_refs = refs[:nw]
        w_refs, m_refs, v_refs = (refs[nw * (i + 1) : nw * (i + 2)] for i in range(3))
        last_refs = refs[4 * nw : 4 * nw + 3]
        refs = refs[4 * nw + 3 :]
        param_refs = refs[:n_params]
        outs = refs[n_params:]
        big_out = outs[: 4 * nw]
        last_out = outs[4 * nw : 4 * nw + 4]
        small_out = outs[4 * nw + 4 : 4 * nw + 4 + n_small_out]
        scratch = outs[4 * nw + 4 + n_small_out :]
        small_land = scratch[:ns]
        pair_bufs = scratch[ns : ns + 4]
        local_sem = scratch[ns + 4]
        pair_sems = scratch[ns + 5 : ns + 9]
        small_sems = scratch[ns + 9 :]
        step = pl.program_id(0)

        def pair():
            return _pair_reduce_scatter(gp_hbm, *pair_bufs, local_sem, pair_sems)

        def small():
            return _direct_exchange(part_hbm, small_land, small_sems, True)

        @pl.when(step == 0)
        def _():
            pair()[0]()
            small()[0]()

        @pl.when(step == 1)
        def _():
            pair()[1]()

        for t in range(nw):
            g = land_refs[t][0].astype(F32)
            for s in range(1, N_DEV):
                g = g + land_refs[t][s].astype(F32)
            for ref, val in zip(big_out[4 * t : 4 * t + 4], (g,) + _adam(w_refs[t][...], g, m_refs[t][...], v_refs[t][...])):
                ref[...] = val

        @pl.when(step == TAIL_STEPS - 1)
        def _():
            pair()[2]()
            small()[1]()
            land = pair_bufs[3]
            g = land[0].astype(F32)
            for chip in range(1, N_CHIPS):
                g = g + land[chip].astype(F32)
            for ref, val in zip(last_out, (g,) + _adam(last_refs[0][...], g, last_refs[1][...], last_refs[2][...])):
                ref[...] = val
            sums = []
            for i in range(ns):
                g = small_land[i][0].astype(F32)
                for s in range(1, N_DEV):
                    g = g + small_land[i][s].astype(F32)
                sums.append(g)
            params, results = list(param_refs), list(small_out)
            for i, (w, _, _, where) in entries:
                g = sums[i]
                if where is not None:
                    shape = w.shape if w is not None else where[2:]
                    g = g[where[0] : where[0] + shape[0], where[1] : where[1] + shape[1]]
                if w is None:
                    results.pop(0)[...] = g
                    continue
                w_ref, m_ref, v_ref = params[:3]
                del params[:3]
                for val in (g,) + _adam(w_ref[...], g, m_ref[...], v_ref[...]):
                    results.pop(0)[...] = val

    def tile(t):
        return pl.BlockSpec((tiles[t], ws[t].shape[1]), lambda i: (i, 0))

    def land_tile(t):
        return pl.BlockSpec((N_DEV, tiles[t], ws[t].shape[1]), lambda i: (0, i, 0))

    def whole(shape):
        return pl.BlockSpec(shape, lambda step: (0, 0))

    big_sds = [jax.ShapeDtypeStruct(ws[t].shape, F32) for t in range(nw) for _ in range(4)]
    params, small_shapes = [], []
    for _, (w, m, v, where) in entries:
        if w is None:
            small_shapes.append(tuple(where[2:]))
        else:
            params += [w, m, v]
            small_shapes += [w.shape] * 4
    return pl.pallas_call(
        body,
        name="reduce_adam_tail",
        grid=(TAIL_STEPS,),
        out_shape=big_sds
        + [jax.ShapeDtypeStruct(blk, F32)] * 4
        + [jax.ShapeDtypeStruct(shape, F32) for shape in small_shapes],
        in_specs=[ANY_SPEC] * (1 + ns)
        + [land_tile(t) for t in range(nw)]
        + [tile(t) for _ in range(3) for t in range(nw)]
        + [whole(blk)] * 3
        + [whole(p.shape) for p in params],
        out_specs=[tile(t) for t in range(nw) for _ in range(4)]
        + [whole(blk)] * 4
        + [whole(shape) for shape in small_shapes],
        scratch_shapes=[pltpu.VMEM((N_DEV,) + smalls[i][0].shape, smalls[i][0].dtype) for i in range(ns)]
        + [pltpu.VMEM((N_CHIPS,) + blk, BF16)] * 4
        + [pltpu.SemaphoreType.DMA((1,))]
        + [pltpu.SemaphoreType.DMA((count,)) for count in PAIR_SEMS]
        + _exchange_sems(ns),
        compiler_params=pltpu.CompilerParams(
            dimension_semantics=("arbitrary",), vmem_limit_bytes=VMEM_LIMIT, has_side_effects=True
        ),
    )(gp_last, *[s[0] for s in smalls], *lands, *ws, *ms, *vs, *last, *params)


def _fwd_in(x, g1, w_in):
    S = x.shape[0]
    tm = min(512, S)

    def body(x_ref, g_ref, w_ref, hn_ref, u_ref, q_ref, k_ref, v_ref):
        xv = x_ref[...]
        hn = (xv * _rstd(xv) * g_ref[...]).astype(BF16)
        hn_ref[...] = hn
        outs = (u_ref, q_ref, k_ref, v_ref)
        for j in range(N_DEV):
            p = _dot(hn, w_ref[j])
            cols = slice(256 * (j % 2), 256 * (j % 2 + 1))
            if j // 2 == 0:
                u_ref[:, cols] = p
            elif j // 2 == 1:
                q_ref[:, cols] = (p * (HEAD_DIM**-0.5)).astype(BF16)
            else:
                outs[j // 2][:, cols] = p.astype(BF16)

    half = pl.BlockSpec((tm, D_POOL), lambda i: (i, 0))
    return pl.pallas_call(
        body,
        name="fwd_in",
        grid=(S // tm,),
        out_shape=[
            jax.ShapeDtypeStruct((S, D_MODEL), BF16),
            jax.ShapeDtypeStruct((S, D_POOL), F32),
            jax.ShapeDtypeStruct((S, D_ATTN), BF16),
            jax.ShapeDtypeStruct((S, D_ATTN), BF16),
            jax.ShapeDtypeStruct((S, D_ATTN), BF16),
        ],
        in_specs=[
            pl.BlockSpec((tm, D_MODEL), lambda i: (i, 0)),
            pl.BlockSpec((1, D_MODEL), lambda i: (0, 0)),
            pl.BlockSpec((N_DEV, D_MODEL, 256), lambda i: (0, 0, 0)),
        ],
        out_specs=[pl.BlockSpec((tm, D_MODEL), lambda i: (i, 0)), half, half, half, half],
        compiler_params=_params(("parallel",)),
    )(x, g1, w_in)


POOL_CHUNK = 256
POOL_UNROLL = 3
_POOL_W_SPEC = pl.BlockSpec((1, GROUP_DIM, GROUP_DIM), lambda g: (g, 0, 0))
_POOL_SCALE_SPEC = pl.BlockSpec((1, GROUP_DIM), lambda g: (0, g))


def _lane_sum(wide, n):
    out = wide[:, :GROUP_DIM]
    for i in range(1, n):
        out = out + wide[:, GROUP_DIM * i : GROUP_DIM * (i + 1)]
    return out


def _pool_fwd_group(u_ref, pw_ref, sc_ref, pooled_ref, y_ref, g):
    S = u_ref.shape[0]
    R = min(POOL_CHUNK, S)
    w = jnp.left_shift(jnp.int32(2), g)
    d = lax.broadcasted_iota(jnp.int32, (R, R), 0) - lax.broadcasted_iota(jnp.int32, (R, R), 1)
    b_cur = jnp.where((d >= 0) & (d < w), 1.0, 0.0).astype(BF16)
    dp = lax.broadcasted_iota(jnp.int32, (R, HALO), 0) + HALO - lax.broadcasted_iota(jnp.int32, (R, HALO), 1)
    b_prev = jnp.where(dp < w, 1.0, 0.0).astype(BF16)
    pw = pw_ref[0].astype(BF16)
    sc = sc_ref[...]

    def chunk(r0, first):
        cur = u_ref[pl.ds(r0, R), :]
        wide = _dot(b_cur, jnp.concatenate(_split3(cur), axis=1))
        if not first:
            prev = u_ref[pl.ds(pl.multiple_of(r0 - HALO, HALO), HALO), :]
            wide = wide + _dot(b_prev, jnp.concatenate(_split3(prev), axis=1))
        count = jnp.minimum(r0 + lax.broadcasted_iota(jnp.int32, (R, 1), 0) + 1, w).astype(F32)
        pooled = (_lane_sum(wide, 3) / count - cur).astype(BF16)
        pooled_ref[pl.ds(r0, R), :] = pooled
        y_ref[pl.ds(r0, R), :] = _dot(pooled, pw) * sc

    chunk(0, True)

    def rest(c, carry):
        chunk(pl.multiple_of(c * R, R), False)
        return carry

    lax.fori_loop(1, S // R, rest, 0, unroll=POOL_UNROLL)


ATTN_WINDOW = 3
ATTN_TOGETHER = 2


def _band_ones(T):
    row = lax.broadcasted_iota(jnp.int32, (T, T), 0)
    col = lax.broadcasted_iota(jnp.int32, (T, T), 1)
    after = jnp.where(row > col, 1.0, 0.0).astype(BF16)
    before = jnp.where(row < col, 1.0, 0.0).astype(BF16)
    return jnp.concatenate([after, after], axis=0), jnp.concatenate([before, before], axis=0), col < row


def _log1m(z, mask):
    sp = jnp.log(1.0 + jnp.exp(-jnp.abs(z)))
    l1m = -jnp.maximum(z, 0.0) - sp
    return (l1m if mask is None else jnp.where(mask, l1m, 0.0)), sp


def _log_gates_of(jobs, after2):
    zs = [_dot_nt(qh, kb) for qh, kb, _ in jobs]
    terms = []
    for z, (_, _, mask) in zip(zs, jobs):
        l1m, sp = _log1m(z, mask)
        terms.append(
            (jnp.minimum(z, 0.0) - sp, jnp.concatenate(_split2(l1m), axis=1), jnp.sum(l1m, axis=1, keepdims=True))
        )
    return [(ls, _dot(split, after2), total) for ls, split, total in terms]


def _log_gates(qhs, kbs, masks, after2):
    flat = _log_gates_of([(qh, kb, mask) for kb, mask in zip(kbs, masks) for qh in qhs], after2)
    return [flat[len(qhs) * b : len(qhs) * (b + 1)] for b in range(len(kbs))]


def _tile_start(index, T):
    return index * T if isinstance(index, int) else pl.multiple_of(index * T, T)


def _weights(ls, tail, carry, mask):
    a = jnp.exp(ls + tail + carry)
    return a if mask is None else jnp.where(mask, a, 0.0)


def _reaches(carry):
    return jnp.max(carry) > EXP_UNDERFLOW


def _mixers_fwd(q, k, v, u, pool_w, scale, shards):
    S = q.shape[0]
    T = ATTN_TILE
    nq = S // T
    n = len(shards)
    steps = D_ATTN // LANES

    def body(q_ref, k_ref, v_ref, u_ref, pw_ref, sc_ref, *rest):
        o_ref, pooled_ref, yp_ref = rest[n : n + 3]
        step = pl.program_id(0)

        def gather():
            return _two_level_gather(rest[:n], rest[n + 3 : 2 * n + 3], rest[2 * n + 3 :])

        @pl.when(step == 0)
        def _():
            gather()[0]()

        @pl.when(step == steps // 2)
        def _():
            gather()[1]()

        _pool_fwd_group(u_ref, pw_ref, sc_ref, pooled_ref, yp_ref, step)

        lane = lax.broadcasted_iota(jnp.int32, (T, LANES), 1)
        heads = (lane < HEAD_DIM, lane >= HEAD_DIM)
        after2, _, causal = _band_ones(T)
        zero = jnp.zeros((T, 1), F32)

        def key_tile(k0):
            vb = v_ref[pl.ds(k0, T), :]
            return k_ref[pl.ds(k0, T), :], jnp.concatenate([jnp.where(m, vb, jnp.zeros_like(vb)) for m in heads], axis=0)

        def q_tiles(qis, nb):
            queries, jobs = [], []
            for qi in qis:
                q0 = _tile_start(qi, T)
                qb = q_ref[pl.ds(q0, T), :]
                qhs = [jnp.where(m, qb, jnp.zeros_like(qb)) for m in heads]
                tiles = [key_tile(_tile_start(qi - b, T)) for b in range(nb)]
                queries.append((q0, qhs, [v_rows for _, v_rows in tiles]))
                jobs += [(qh, kb, causal if b == 0 else None) for b, (kb, _) in enumerate(tiles) for qh in qhs]
            gates = _log_gates_of(jobs, after2)
            states = []
            for i, (q0, qhs, values) in enumerate(queries):
                carries, probs = [zero, zero], []
                for b in range(nb):
                    for h in range(2):
                        ls, tail, total = gates[2 * nb * i + 2 * b + h]
                        probs.append(_weights(ls, tail, carries[h], causal if b == 0 else None).astype(BF16))
                        carries[h] = carries[h] + total
                states.append((carries, _dot(jnp.concatenate(probs, axis=1), jnp.concatenate(values, axis=0))))

            def more(st):
                return (st[0] > 0) & _reaches(jnp.maximum(st[1], st[2]))

            for qi, (q0, qhs, _), (carries, acc) in zip(qis, queries, states):

                def k_step(st, qhs=qhs):
                    kj = st[0] - 1
                    kb, v_rows = key_tile(pl.multiple_of(kj * T, T))
                    new, probs = [], []
                    for (ls, tail, total), carry in zip(_log_gates(qhs, [kb], [None], after2)[0], st[1:3]):
                        probs.append(_weights(ls, tail, carry, None).astype(BF16))
                        new.append(carry + total)
                    return kj, new[0], new[1], st[3] + _dot(jnp.concatenate(probs, axis=1), v_rows)

                if not isinstance(qi, int):
                    acc = lax.while_loop(more, k_step, (qi - (nb - 1), carries[0], carries[1], acc))[3]
                o_ref[pl.ds(q0, T), :] = acc

        head_tiles = min(ATTN_WINDOW - 1, nq)
        for qi in range(head_tiles):
            q_tiles([qi], qi + 1)
        assert (nq - head_tiles) % ATTN_TOGETHER == 0

        def q_loop(i, carry):
            first = head_tiles + ATTN_TOGETHER * i
            q_tiles([first + j for j in range(ATTN_TOGETHER)], ATTN_WINDOW)
            return carry

        lax.fori_loop(0, (nq - head_tiles) // ATTN_TOGETHER, q_loop, 0)

        @pl.when(step == steps - 1)
        def _():
            _, _, forward, finish = gather()
            forward()
            finish()

    blk = pl.BlockSpec((S, LANES), lambda p: (0, p))
    return pl.pallas_call(
        body,
        name="mixers_fwd",
        grid=(steps,),
        out_shape=[
            jax.ShapeDtypeStruct((S, D_ATTN), F32),
            jax.ShapeDtypeStruct((S, D_POOL), BF16),
            jax.ShapeDtypeStruct((S, D_POOL), F32),
        ]
        + _exchange_shapes(shards, True),
        in_specs=[blk, blk, blk, blk, _POOL_W_SPEC, _POOL_SCALE_SPEC] + [ANY_SPEC] * n,
        out_specs=[blk, blk, blk] + [ANY_SPEC] * n,
        scratch_shapes=_exchange_sems(n),
        compiler_params=pltpu.CompilerParams(
            dimension_semantics=("arbitrary",), vmem_limit_bytes=VMEM_LIMIT, has_side_effects=True
        ),
    )(q, k, v, u, pool_w, scale, *shards)


MLP_AHEAD = 1
SG_ROWS = 8


def _mlp_fwd_bwd(x, y_pool, y_attn, target, g_pool, g_attn, g2, gf, w_out, w_up, w_down):
    S = x.shape[0]
    tm = min(256, S)
    fc = D_FF // N_DEV

    def body(x_ref, yp_ref, ya_ref, t_ref, gp_ref, ga_ref, g2_ref, gf_ref, wo_hbm, wu_hbm, wd_hbm,
             mixed_ref, hn2_ref, act_ref, dup_ref, dh2_ref, dh1_ref, dh1b_ref, dyp_ref, dya_ref, sg_ref,
             wo, wu, wd, up_s, sems):
        @pl.when(pl.program_id(0) == 0)
        def _():
            copies = [
                pltpu.make_async_copy(wo_hbm, wo, sems.at[0]),
                pltpu.make_async_copy(wu_hbm, wu, sems.at[1]),
                pltpu.make_async_copy(wd_hbm, wd, sems.at[2]),
            ]
            for cp in copies:
                cp.start()
            for cp in copies:
                cp.wait()
            sg_ref[...] = jnp.zeros_like(sg_ref)

        gp, ga, g2v, gfv = gp_ref[...], ga_ref[...], g2_ref[...], gf_ref[...]
        yp, ya = yp_ref[...], ya_ref[...]
        rp, ra = _rstd(yp), _rstd(ya)
        yph, yah = yp * rp, ya * ra
        mixed = jnp.concatenate([(yph * gp).astype(BF16), (yah * ga).astype(BF16)], axis=1)
        mixed_ref[...] = mixed
        h1 = x_ref[...] + _dot(mixed, wo[...])
        r2 = _rstd(h1)
        h1h = h1 * r2
        hn2 = (h1h * g2v).astype(BF16)
        hn2_ref[...] = hn2
        h2 = h1
        ups = [_dot(hn2, wu[j]) for j in range(MLP_AHEAD)]
        for j in range(N_DEV):
            cols = slice(fc * j, fc * (j + 1))
            if j + MLP_AHEAD < N_DEV:
                ups.append(_dot(hn2, wu[j + MLP_AHEAD]))
            up = ups.pop(0)
            up_s[:, cols] = up
            act = jnp.square(jnp.maximum(up, 0.0)).astype(BF16)
            act_ref[:, cols] = act
            h2 = h2 + _dot(act, wd[cols, :])
        rf = _rstd(h2)
        h2h = h2 * rf
        diff = h2h * gfv - t_ref[...]
        loss_rows = 0.5 * jnp.mean(diff * diff, axis=-1, keepdims=True)
        dy = diff * (1.0 / D_MODEL)
        dh2, dgf = _rms_bwd(dy, h2h, rf, gfv)
        dh2b = dh2.astype(BF16)
        dh2_ref[...] = dh2b
        dhn2 = jnp.zeros((tm, D_MODEL), F32)
        dacts = [_dot_nt(dh2b, wd[fc * j : fc * (j + 1), :]) for j in range(MLP_AHEAD)]
        for j in range(N_DEV):
            cols = slice(fc * j, fc * (j + 1))
            if j + MLP_AHEAD < N_DEV:
                dacts.append(_dot_nt(dh2b, wd[fc * (j + MLP_AHEAD) : fc * (j + MLP_AHEAD + 1), :]))
            dup = (dacts.pop(0) * (2.0 * jnp.maximum(up_s[:, cols], 0.0))).astype(BF16)
            dup_ref[:, cols] = dup
            dhn2 = dhn2 + _dot_nt(dup, wu[j])
        dh1n, dg2 = _rms_bwd(dhn2, h1h, r2, g2v)
        dh1 = dh2 + dh1n
        dh1_ref[...] = dh1
        dh1b = dh1.astype(BF16)
        dh1b_ref[...] = dh1b
        dmix = _dot_nt(dh1b, wo[...])
        dyp, dgp = _rms_bwd(dmix[:, :D_POOL], yph, rp, gp)
        dya, dga = _rms_bwd(dmix[:, D_POOL:], yah, ra, ga)
        dyp_ref[...] = dyp
        dya_ref[...] = dya
        sg_ref[0:1, :] += dgf
        sg_ref[1:2, :] += dg2
        sg_ref[2:3, :] += jnp.concatenate([dgp, dga], axis=1)
        sg_ref[3:4, :] += jnp.broadcast_to(jnp.sum(loss_rows, axis=0, keepdims=True), (1, D_MODEL))

    def tok(n):
        return pl.BlockSpec((tm, n), lambda i: (i, 0))

    def vec(n):
        return pl.BlockSpec((1, n), lambda i: (0, 0))

    any_spec = pl.BlockSpec(memory_space=pl.ANY)
    return pl.pallas_call(
        body,
        name="mlp_fwd_bwd",
        grid=(S // tm,),
        out_shape=[
            jax.ShapeDtypeStruct((S, D_MODEL), BF16),
            jax.ShapeDtypeStruct((S, D_MODEL), BF16),
            jax.ShapeDtypeStruct((S, D_FF), BF16),
            jax.ShapeDtypeStruct((S, D_FF), BF16),
            jax.ShapeDtypeStruct((S, D_MODEL), BF16),
            jax.ShapeDtypeStruct((S, D_MODEL), F32),
            jax.ShapeDtypeStruct((S, D_MODEL), BF16),
            jax.ShapeDtypeStruct((S, D_POOL), F32),
            jax.ShapeDtypeStruct((S, D_ATTN), F32),
            jax.ShapeDtypeStruct((SG_ROWS, D_MODEL), F32),
        ],
        in_specs=[tok(D_MODEL), tok(D_POOL), tok(D_ATTN), tok(D_MODEL), vec(D_POOL), vec(D_ATTN),
                  vec(D_MODEL), vec(D_MODEL), any_spec, any_spec, any_spec],
        out_specs=[tok(D_MODEL), tok(D_MODEL), tok(D_FF), tok(D_FF), tok(D_MODEL), tok(D_MODEL),
                   tok(D_MODEL), tok(D_POOL), tok(D_ATTN),
                   pl.BlockSpec((SG_ROWS, D_MODEL), lambda i: (0, 0))],
        scratch_shapes=[
            pltpu.VMEM((D_MODEL, D_MODEL), BF16),
            pltpu.VMEM((N_DEV, D_MODEL, fc), BF16),
            pltpu.VMEM((D_FF, D_MODEL), BF16),
            pltpu.VMEM((tm, D_FF), F32),
            pltpu.SemaphoreType.DMA((3,)),
        ],
        compiler_params=_params(("arbitrary",)),
    )(x, y_pool, y_attn, target, g_pool, g_attn, g2, gf, w_out, w_up, w_down)


def _wgrad(a, b, block_a, groups, name, travelling=()):
    n = len(travelling)
    S, ka = a.shape
    nb = b.shape[1]
    ts = min(1024, S)
    per = N_DEV // groups
    if block_a:
        ka //= groups
        blk = (ka // per, nb)
        a_spec = pl.BlockSpec((ts, ka), lambda g, s: (s, g))
        b_spec = pl.BlockSpec((ts, nb), lambda g, s: (s, 0))
    else:
        nb //= groups
        blk = (ka, nb // per)
        a_spec = pl.BlockSpec((ts, ka), lambda g, s: (s, 0))
        b_spec = pl.BlockSpec((ts, nb), lambda g, s: (s, g))
    steps = S // ts

    def body(a_ref, b_ref, *rest):
        o_ref, acc = rest[n], rest[2 * n + 1]
        g, s = pl.program_id(0), pl.program_id(1)

        def scatter():
            return _direct_exchange(rest[:n], rest[n + 1 : 2 * n + 1], rest[2 * n + 2 :], False)

        if n:
            @pl.when((g == 0) & (s == 0))
            def _():
                scatter()[0]()

        @pl.when(s == 0)
        def _():
            acc[...] = jnp.zeros_like(acc)

        acc[...] += _dot_tn(a_ref[...], b_ref[...])

        @pl.when(s == steps - 1)
        def _():
            for j in range(per):
                if block_a:
                    o_ref[j] = acc[blk[0] * j : blk[0] * (j + 1), :].astype(BF16)
                else:
                    o_ref[j] = acc[:, blk[1] * j : blk[1] * (j + 1)].astype(BF16)

        if n:
            @pl.when((g == groups - 1) & (s == steps - 1))
            def _():
                scatter()[1]()

    results = pl.pallas_call(
        body,
        name=name,
        grid=(groups, steps),
        out_shape=[jax.ShapeDtypeStruct((N_DEV,) + blk, BF16)] + _exchange_shapes(travelling, False),
        in_specs=[a_spec, b_spec] + [ANY_SPEC] * n,
        out_specs=[pl.BlockSpec((per,) + blk, lambda g, s: (g, 0, 0))] + [ANY_SPEC] * n,
        scratch_shapes=[pltpu.VMEM((ka, nb), F32)] + (_exchange_sems(n) if n else []),
        compiler_params=pltpu.CompilerParams(
            dimension_semantics=("arbitrary", "arbitrary"), vmem_limit_bytes=VMEM_LIMIT, has_side_effects=bool(n)
        ),
    )(a, b, *travelling)
    return results if n else results[0]


def _mixers_bwd(q, k, v, do, dyp, pooled, pool_w, scale, partials):
    S = q.shape[0]
    T = ATTN_TILE
    nq = S // T
    n = len(partials)
    steps = D_ATTN // LANES

    def body(q_ref, k_ref, v_ref, do_ref, dyp_ref, pooled_ref, pw_ref, sc_ref, *rest):
        dq_ref, dk_ref, dv_ref, du_ref, dsc_ref, dpw_ref = rest[n : n + 6]
        dk_acc, dv_acc, carry_s = rest[2 * n + 6 : 2 * n + 9]
        step = pl.program_id(0)

        def scatter():
            return _direct_exchange(rest[:n], rest[n + 6 : 2 * n + 6], rest[2 * n + 9 :], False)

        @pl.when(step == 0)
        def _():
            scatter()[0]()

        _pool_bwd_group(dyp_ref, pooled_ref, pw_ref, sc_ref, du_ref, dsc_ref, dpw_ref, step)

        dk_acc[...] = jnp.zeros_like(dk_acc)
        dv_acc[...] = jnp.zeros_like(dv_acc)
        lane = lax.broadcasted_iota(jnp.int32, (T, LANES), 1)
        heads = (lane < HEAD_DIM, lane >= HEAD_DIM)
        after2, before2, causal = _band_ones(T)
        zero = jnp.zeros((T, 1), F32)

        def key_tile(k0):
            kb = k_ref[pl.ds(k0, T), :]
            k_rows = jnp.concatenate([jnp.where(m, kb, jnp.zeros_like(kb)) for m in heads], axis=0)
            return kb, v_ref[pl.ds(k0, T), :], k_rows

        def grads(items):
            das = [[[_dot_nt(doh, vb) for doh in qs[1]] for _, vb, _ in tiles] for qs, _, tiles, _, _, _, _ in items]
            probs, gs = [], []
            for (_, _, tiles, masks, carry_in, gates, _), da in zip(items, das):
                probs.append([[_weights(gates[b][h][0], gates[b][h][1], carry_in[b][h], masks[b]) for h in range(2)]
                              for b in range(len(tiles))])
                gs.append([[probs[-1][b][h] * da[b][h] for h in range(2)] for b in range(len(tiles))])
            before = [[[_dot(jnp.concatenate(_split2(g), axis=1), before2) for g in row] for row in item] for item in gs]
            dzs, lefts = [], []
            for i, (_, _, tiles, masks, _, gates, st) in enumerate(items):
                g_left, item_dz = [st[0], st[1]], [None] * len(tiles)
                for b in reversed(range(len(tiles))):
                    item_dz[b] = []
                    for h in range(2):
                        sig = jnp.exp(gates[b][h][0])
                        dz = gs[i][b][h] * (1.0 - sig) - sig * (g_left[h] + before[i][b][h])
                        if masks[b] is not None:
                            dz = jnp.where(masks[b], dz, 0.0)
                        item_dz[b].append(dz.astype(BF16))
                        g_left[h] = g_left[h] + jnp.sum(gs[i][b][h], axis=1, keepdims=True)
                dzs.append(item_dz)
                lefts.append(g_left)
            out = []
            for (_, _, tiles, _, _, _, st), item_dz, g_left in zip(items, dzs, lefts):
                dq = st[2] + _dot(
                    jnp.concatenate([dz for row in item_dz for dz in row], axis=1),
                    jnp.concatenate([k_rows for _, _, k_rows in tiles], axis=0),
                )
                out.append((g_left[0], g_left[1], dq))
            for (qs, starts, tiles, _, _, _, _), item_dz, item_probs in zip(items, dzs, probs):
                for b in range(len(tiles)):
                    dk_acc[pl.ds(starts[b], T), :] += _dot_tn(jnp.concatenate(item_dz[b], axis=0), qs[2])
                    dv_acc[pl.ds(starts[b], T), :] += _dot_tn(
                        jnp.concatenate([a.astype(BF16) for a in item_probs[b]], axis=0), qs[3]
                    )
            return out

        def more(st):
            return (st[0] > 0) & _reaches(jnp.maximum(st[1], st[2]))

        def q_tiles(qis, nb):
            masks = [causal] + [None] * (nb - 1)
            prepared, jobs = [], []
            for qi in qis:
                q0 = _tile_start(qi, T)
                qb = q_ref[pl.ds(q0, T), :]
                dob = do_ref[pl.ds(q0, T), :]
                qhs = [jnp.where(m, qb, jnp.zeros_like(qb)) for m in heads]
                dohs = [jnp.where(m, dob, 0.0).astype(BF16) for m in heads]
                queries = (qhs, dohs, jnp.concatenate(qhs, axis=0), jnp.concatenate(dohs, axis=0))
                starts = [_tile_start(qi - b, T) for b in range(nb)]
                tiles = [key_tile(k0) for k0 in starts]
                prepared.append((q0, queries, starts, tiles))
                jobs += [(qh, kb, mask) for (kb, _, _), mask in zip(tiles, masks) for qh in qhs]
            flat = _log_gates_of(jobs, after2)
            items = []
            for i, (qi, (q0, queries, starts, tiles)) in enumerate(zip(qis, prepared)):
                gates = [flat[2 * nb * i + 2 * b : 2 * nb * i + 2 * b + 2] for b in range(nb)]
                carries, carry_in = [zero, zero], []
                for b in range(nb):
                    carry_in.append(list(carries))
                    carries = [carries[h] + gates[b][h][2] for h in range(2)]
                st = (zero, zero, jnp.zeros((T, LANES), F32))
                if not isinstance(qi, int):
                    k_left = qi - (nb - 1)
                    qhs = queries[0]

                    def right_to_left(st, qhs=qhs):
                        kj = st[0] - 1
                        carry_s[0, kj] = st[1]
                        carry_s[1, kj] = st[2]
                        kb = k_ref[pl.ds(pl.multiple_of(kj * T, T), T), :]
                        sums = [jnp.sum(_log1m(_dot_nt(qh, kb), None)[0], axis=1, keepdims=True) for qh in qhs]
                        return kj, st[1] + sums[0], st[2] + sums[1]

                    k_first = lax.while_loop(more, right_to_left, (k_left, carries[0], carries[1]))[0]

                    def left_to_right(kj, st, queries=queries):
                        k0 = pl.multiple_of(kj * T, T)
                        tile = key_tile(k0)
                        left_gates = _log_gates(queries[0], [tile[0]], [None], after2)
                        carry = [[carry_s[0, kj], carry_s[1, kj]]]
                        return grads([(queries, [k0], [tile], [None], carry, left_gates, st)])[0]

                    st = lax.fori_loop(k_first, k_left, left_to_right, st)
                items.append((queries, starts, tiles, masks, carry_in, gates, st))
            for (q0, _, _, _), st in zip(prepared, grads(items)):
                dq_ref[pl.ds(q0, T), :] = (st[2] * (HEAD_DIM**-0.5)).astype(BF16)

        head_tiles = min(ATTN_WINDOW - 1, nq)
        for qi in range(head_tiles):
            q_tiles([qi], qi + 1)
        assert (nq - head_tiles) % ATTN_TOGETHER == 0

        def q_loop(i, carry):
            first = head_tiles + ATTN_TOGETHER * i
            q_tiles([first + j for j in range(ATTN_TOGETHER)], ATTN_WINDOW)
            return carry

        lax.fori_loop(0, (nq - head_tiles) // ATTN_TOGETHER, q_loop, 0)
        dk_ref[...] = dk_acc[...].astype(BF16)
        dv_ref[...] = dv_acc[...].astype(BF16)

        @pl.when(step == steps - 1)
        def _():
            scatter()[1]()

    blk = pl.BlockSpec((S, LANES), lambda p: (0, p))
    sds = jax.ShapeDtypeStruct((S, D_ATTN), BF16)
    return pl.pallas_call(
        body,
        name="mixers_bwd",
        grid=(steps,),
        out_shape=[
            sds,
            sds,
            sds,
            jax.ShapeDtypeStruct((S, D_POOL), BF16),
            jax.ShapeDtypeStruct((8, D_POOL), F32),
            jax.ShapeDtypeStruct((N_GROUPS, GROUP_DIM, GROUP_DIM), BF16),
        ]
        + _exchange_shapes(partials, False),
        in_specs=[blk] * 6 + [_POOL_W_SPEC, _POOL_SCALE_SPEC] + [ANY_SPEC] * n,
        out_specs=[blk] * 4 + [pl.BlockSpec((8, GROUP_DIM), lambda p: (0, p)), _POOL_W_SPEC] + [ANY_SPEC] * n,
        scratch_shapes=[pltpu.VMEM((S, LANES), F32), pltpu.VMEM((S, LANES), F32), pltpu.VMEM((2, nq, T, 1), F32)]
        + _exchange_sems(n),
        compiler_params=pltpu.CompilerParams(
            dimension_semantics=("arbitrary",), vmem_limit_bytes=VMEM_LIMIT, has_side_effects=True
        ),
    )(q, k, v, do, dyp, pooled, pool_w, scale, *partials)


def _pool_bwd_group(dy_ref, pooled_ref, pw_ref, sc_ref, du_ref, dsc_ref, dpw_ref, g):
    S = dy_ref.shape[0]
    R = min(POOL_CHUNK, S)
    nc = S // R
    w = jnp.left_shift(jnp.int32(2), g)
    pw = pw_ref[0].astype(BF16)
    sc = sc_ref[...]
    d = lax.broadcasted_iota(jnp.int32, (R, R), 1) - lax.broadcasted_iota(jnp.int32, (R, R), 0)
    bt_cur = jnp.where((d >= 0) & (d < w), 1.0, 0.0).astype(BF16)
    dn = lax.broadcasted_iota(jnp.int32, (R, HALO), 1) + R - lax.broadcasted_iota(jnp.int32, (R, HALO), 0)
    bt_next = jnp.where(dn < w, 1.0, 0.0).astype(BF16)

    def per_count(dpl, r0):
        n = dpl.shape[0]
        count = jnp.minimum(r0 + lax.broadcasted_iota(jnp.int32, (n, 1), 0) + 1, w).astype(F32)
        return jnp.concatenate(_split2(dpl / count), axis=1)

    def chunk(r0, last, sums):
        dyv = dy_ref[pl.ds(r0, R), :]
        pooled = pooled_ref[pl.ds(r0, R), :]
        dmapped = (dyv * sc).astype(BF16)
        dsc = sums[0] + jnp.sum(dyv * _dot(pooled, pw), axis=0, keepdims=True)
        dpw = sums[1] + _dot_tn(pooled, dmapped)
        dpl = _dot_nt(dmapped, pw)
        wide = _dot(bt_cur, per_count(dpl, r0))
        if not last:
            nxt = pl.multiple_of(r0 + R, R)
            dpl_next = _dot_nt((dy_ref[pl.ds(nxt, HALO), :] * sc).astype(BF16), pw)
            wide = wide + _dot(bt_next, per_count(dpl_next, nxt))
        du_ref[pl.ds(r0, R), :] = (_lane_sum(wide, 2) - dpl).astype(BF16)
        return dsc, dpw

    sums = (jnp.zeros((1, GROUP_DIM), F32), jnp.zeros((GROUP_DIM, GROUP_DIM), F32))
    sums = lax.fori_loop(
        0, nc - 1, lambda c, s: chunk(pl.multiple_of(c * R, R), False, s), sums, unroll=POOL_UNROLL
    )
    dsc, dpw = chunk((nc - 1) * R, True, sums)
    dsc_ref[...] = jnp.zeros_like(dsc_ref)
    dsc_ref[0:1, :] = dsc
    dpw_ref[0] = dpw.astype(dpw_ref.dtype)


def _bwd_in(du, dq, dk, dv, w_in, x, dh1, g1):
    S = x.shape[0]
    tm = min(512, S)

    def body(du_ref, dq_ref, dk_ref, dv_ref, w_ref, x_ref, dh1_ref, g_ref, dx_ref, dproj_ref, dg_ref):
        @pl.when(pl.program_id(0) == 0)
        def _():
            dg_ref[...] = jnp.zeros_like(dg_ref)

        parts = (du_ref[...], dq_ref[...], dk_ref[...], dv_ref[...])
        dhn = jnp.zeros((tm, D_MODEL), F32)
        for j in range(N_DEV):
            piece = parts[j // 2][:, 256 * (j % 2) : 256 * (j % 2 + 1)]
            dproj_ref[:, 256 * j : 256 * (j + 1)] = piece
            dhn = dhn + _dot_nt(piece, w_ref[j])
        xv = x_ref[...]
        r = _rstd(xv)
        dxn, dg = _rms_bwd(dhn, xv * r, r, g_ref[...])
        dx_ref[...] = dh1_ref[...] + dxn
        dg_ref[0:1, :] += dg

    half = pl.BlockSpec((tm, D_POOL), lambda i: (i, 0))
    full = pl.BlockSpec((tm, D_MODEL), lambda i: (i, 0))
    return pl.pallas_call(
        body,
        name="bwd_in",
        grid=(S // tm,),
        out_shape=[
            jax.ShapeDtypeStruct((S, D_MODEL), F32),
            jax.ShapeDtypeStruct((S, D_IN_PROJ), BF16),
            jax.ShapeDtypeStruct((8, D_MODEL), F32),
        ],
        in_specs=[half, half, half, half,
                  pl.BlockSpec((N_DEV, D_MODEL, 256), lambda i: (0, 0, 0)),
                  full, full, pl.BlockSpec((1, D_MODEL), lambda i: (0, 0))],
        out_specs=[full, pl.BlockSpec((tm, D_IN_PROJ), lambda i: (i, 0)),
                   pl.BlockSpec((8, D_MODEL), lambda i: (0, 0))],
        compiler_params=_params(("arbitrary",)),
    )(du, dq, dk, dv, w_in, x, dh1, g1)


def _rows(a):
    a = a.reshape(-1, LANES)
    pad = (-a.shape[0]) % 8
    return jnp.pad(a, ((0, pad), (0, 0))) if pad else a


def kernel(x, norm1_g, w_in, pool_w, pool_scale, pool_out_g, attn_out_g, w_out, norm2_g, w_up, w_down, final_g, loss_target, m_norm1_g, m_w_in, m_pool_w, m_pool_scale, m_pool_out_g, m_attn_out_g, m_w_out, m_norm2_g, m_w_up, m_w_down, m_final_g, v_norm1_g, v_w_in, v_pool_w, v_pool_scale, v_pool_out_g, v_attn_out_g, v_w_out, v_norm2_g, v_w_up, v_w_down, v_final_g):
    S = x.shape[1]
    xs = x.reshape(S, D_MODEL)
    tgt = loss_target.reshape(S, D_MODEL)
    row = lambda a: a.reshape(1, -1)

    (w_in_g,) = _exchange([w_in.astype(BF16)], True, "gather_w_in")
    hn, u_pool, q, k, v = _fwd_in(xs, row(norm1_g), w_in_g)
    y_attn, pooled, y_pool, w_out_g, w_up_g, w_down_g = _mixers_fwd(
        q, k, v, u_pool, pool_w, row(pool_scale), [w_out.astype(BF16), w_up.astype(BF16), w_down.astype(BF16)]
    )
    w_out_full = w_out_g.reshape(D_MODEL, D_MODEL)
    w_down_full = w_down_g.reshape(D_FF, D_MODEL)
    mixed, hn2, act, dup, dh2b, dh1, dh1b, dyp, dya, sg = _mlp_fwd_bwd(
        xs, y_pool, y_attn, tgt, row(pool_out_g), row(attn_out_g), row(norm2_g), row(final_g),
        w_out_full, w_up_g, w_down_full,
    )
    gp_down = _wgrad(act, dh2b, True, 2, "wgrad_down")
    gp_up = _wgrad(hn2, dup, False, 2, "wgrad_up")
    gp_out = _wgrad(mixed, dh1b, True, 1, "wgrad_out")
    dq, dk, dv, du, dsc, dpw, land_up, land_down = _mixers_bwd(
        q, k, v, dya, dyp, pooled, pool_w, row(pool_scale), [gp_up, gp_down]
    )
    dx, dproj, dg1 = _bwd_in(du, dq, dk, dv, w_in_g, xs, dh1, row(norm1_g))
    gp_in, land_out = _wgrad(hn, dproj, False, 1, "wgrad_in", [gp_out])

    def vectors(final, norm2, pool_out, attn_out, norm1, scale, last):
        pieces = [final, norm2, jnp.concatenate([pool_out, attn_out]), norm1, scale, last]
        return jnp.concatenate([_rows(p) for p in pieces], axis=0)

    no_loss = jnp.zeros((8, LANES), F32)
    flat_pool = lambda a: a.reshape(N_GROUPS * GROUP_DIM, GROUP_DIM)
    smalls = [
        (
            vectors(sg[0], sg[1], sg[2][:D_POOL], sg[2][D_POOL:], dg1[0], dsc[0], sg[3]),
            [(
                vectors(final_g, norm2_g, pool_out_g, attn_out_g, norm1_g, pool_scale, no_loss),
                vectors(m_final_g, m_norm2_g, m_pool_out_g, m_attn_out_g, m_norm1_g, m_pool_scale, no_loss),
                vectors(v_final_g, v_norm2_g, v_pool_out_g, v_attn_out_g, v_norm1_g, v_pool_scale, no_loss),
                None,
            )],
        ),
        (flat_pool(dpw), [(flat_pool(pool_w), flat_pool(m_pool_w), flat_pool(v_pool_w), None)]),
    ]
    tail = _reduce_adam_tail(
        [land_out, land_up, land_down], [w_out, w_up, w_down], [m_w_out, m_w_up, m_w_down],
        [v_w_out, v_w_up, v_w_down], gp_in, (w_in, m_w_in, v_w_in), smalls,
    )
    big = {name: tail[4 * t : 4 * t + 4] for t, name in enumerate(("w_out", "w_up", "w_down", "w_in"))}

    def unpack(vec, pw):
        out = {}
        for i, name in enumerate(("final_g", "norm2_g", "mix_g", "norm1_g", "pool_scale")):
            out[name] = vec[8 * i : 8 * i + 8].reshape(-1)
        out["pool_scale"] = out["pool_scale"][:D_POOL]
        out["pool_out_g"], out["attn_out_g"] = out["mix_g"][:D_POOL], out["mix_g"][D_POOL:]
        out["pool_w"] = pw.reshape(N_GROUPS, GROUP_DIM, GROUP_DIM)
        return out, vec[40, 0]

    small_out = [unpack(tail[16 + i], tail[20 + i]) for i in range(4)]
    loss = small_out[0][1]
    order = ("norm1_g", "w_in", "pool_w", "pool_scale", "pool_out_g", "attn_out_g", "w_out", "norm2_g", "w_up",
             "w_down", "final_g")
    outs = [loss, dx.reshape(1, S, D_MODEL)]
    for i in range(4):
        for name in order:
            outs.append(big[name][i] if name in big else small_out[i][0][name])
    return tuple(outs)
```

```python
import jax
import jax.numpy as jnp
from jax import lax
from jax.experimental import pallas as pl
from jax.experimental.pallas import tpu as pltpu

F32 = jnp.float32
BF16 = jnp.bfloat16
MESH = pl.DeviceIdType.MESH

N_DEV = 8
D_MODEL = 1024
D_POOL = 512
D_ATTN = 512
N_GROUPS = 4
GROUP_DIM = 128
HEAD_DIM = 64
D_FF = 4096
D_IN_PROJ = 2048
EPS = 1e-6
HALO = 16
ATTN_TILE = 128
LANES = 128
EXP_UNDERFLOW = -104.0

ADAM_LR = 0.001
ADAM_B1 = 0.9
ADAM_B2 = 0.999
ADAM_EPS = 1e-08
ADAM_WD = 0.01
ADAM_STEP = 10

VMEM_LIMIT = 56 * 1024 * 1024


def _params(semantics=None, vmem=VMEM_LIMIT):
    return pltpu.CompilerParams(dimension_semantics=semantics, vmem_limit_bytes=vmem)


def _dot(a, b):
    return jnp.dot(a, b, preferred_element_type=F32)


def _dot_nt(a, b):
    return lax.dot_general(a, b, (((1,), (1,)), ((), ())), preferred_element_type=F32)


def _dot_tn(a, b):
    return lax.dot_general(a, b, (((0,), (0,)), ((), ())), preferred_element_type=F32)


def _split2(x):
    hi = x.astype(BF16)
    lo = (x - hi.astype(F32)).astype(BF16)
    return hi, lo


def _split3(x):
    hi = x.astype(BF16)
    r = x - hi.astype(F32)
    mid = r.astype(BF16)
    lo = (r - mid.astype(F32)).astype(BF16)
    return hi, mid, lo


def _rstd(h):
    return lax.rsqrt(jnp.mean(h * h, axis=-1, keepdims=True) + EPS)


def _rms_bwd(dout, hhat, r, g):
    dg = jnp.sum(dout * hhat, axis=0, keepdims=True)
    dxh = dout * g
    dh = r * (dxh - hhat * jnp.mean(dxh * hhat, axis=-1, keepdims=True))
    return dh, dg


def _my_index():
    return 4 * lax.axis_index("x") + 2 * lax.axis_index("y") + lax.axis_index("c")


def _peer(k):
    x, y, c = lax.axis_index("x"), lax.axis_index("y"), lax.axis_index("c")
    px = 1 - x if (k >> 2) & 1 else x
    py = 1 - y if (k >> 1) & 1 else y
    pc = 1 - c if k & 1 else c
    return (px, py, pc), 4 * px + 2 * py + pc


N_PEERS = N_DEV - 1
ANY_SPEC = pl.BlockSpec(memory_space=pl.ANY)


def _exchange_sems(n):
    return [
        pltpu.SemaphoreType.DMA((n * N_PEERS,)),
        pltpu.SemaphoreType.DMA((n * N_PEERS,)),
        pltpu.SemaphoreType.DMA((n,)),
    ]


def _exchange_shapes(blocks, gather):
    if gather:
        return [jax.ShapeDtypeStruct((N_DEV,) + b.shape, b.dtype) for b in blocks]
    return [jax.ShapeDtypeStruct(b.shape, b.dtype) for b in blocks]


def _direct_exchange(ins, outs, sems, gather):
    send_sems, recv_sems, local_sems = sems
    me = _my_index()
    own, sends, recvs = [], [], []
    for t in range(len(ins)):
        own.append(pltpu.make_async_copy(ins[t] if gather else ins[t].at[me], outs[t].at[me], local_sems.at[t]))
        for k in range(1, N_DEV):
            peer, peer_idx = _peer(k)
            src = ins[t] if gather else ins[t].at[peer_idx]
            for dst, bucket in ((outs[t].at[me], sends), (outs[t].at[peer_idx], recvs)):
                bucket.append(
                    pltpu.make_async_remote_copy(
                        src_ref=src,
                        dst_ref=dst,
                        send_sem=send_sems.at[t * N_PEERS + k - 1],
                        recv_sem=recv_sems.at[t * N_PEERS + k - 1],
                        device_id=peer,
                        device_id_type=MESH,
                    )
                )

    def start():
        for cp in own + sends:
            cp.start()

    def finish():
        for cp in recvs:
            cp.wait_recv()
        for cp in sends:
            cp.wait_send()
        for cp in own:
            cp.wait()

    return start, finish


def _flip(a, bit):
    return a + bit - 2 * a * bit


def _two_level_gather(ins, outs, sems):
    send_sems, recv_sems, local_sems = sems
    x, y, c = lax.axis_index("x"), lax.axis_index("y"), lax.axis_index("c")
    me, sibling = (x, y, c), (x, y, 1 - c)
    x_nbr, y_nbr, diagonal = (1 - x, y, c), (x, 1 - y, c), (1 - x, 1 - y, c)
    relay_of = (_flip(x, 1 - c), _flip(y, c), c)
    relay_to = (_flip(x, c), _flip(y, 1 - c), c)

    def copy(t, k, block, to, from_input=False):
        slot = outs[t].at[4 * block[0] + 2 * block[1] + block[2]]
        return pltpu.make_async_remote_copy(
            src_ref=ins[t] if from_input else slot,
            dst_ref=slot,
            send_sem=send_sems.at[t * N_PEERS + k],
            recv_sem=recv_sems.at[t * N_PEERS + k],
            device_id=to,
            device_id_type=MESH,
        )

    arrays = range(len(ins))
    own = [pltpu.make_async_copy(ins[t], outs[t].at[4 * x + 2 * y + c], local_sems.at[t]) for t in arrays]
    first = [copy(t, k, me, to, True) for t in arrays for k, to in ((1, x_nbr), (2, y_nbr), (0, sibling))]
    landed = [copy(t, k, block, me) for t in arrays for k, block in ((1, x_nbr), (2, y_nbr))]
    relays = [copy(t, 3, relay_of, relay_to) for t in arrays]
    passed = [copy(t, 3 + k, block, sibling) for t in arrays for k, block in ((1, x_nbr), (2, y_nbr))]
    relayed = [(copy(t, 3, diagonal, me), copy(t, 6, diagonal, sibling)) for t in arrays]
    last = [copy(t, 0, sibling, me) for t in arrays]
    last += [copy(t, 3 + k, (*block[:2], 1 - c), me) for t in arrays for k, block in ((1, x_nbr), (2, y_nbr), (3, diagonal))]

    def start():
        for cp in own + first:
            cp.start()

    def relay():
        for cp in landed:
            cp.wait_recv()
        for cp in relays + passed:
            cp.start()

    def forward():
        for arrived, onward in relayed:
            arrived.wait_recv()
            onward.start()

    def finish():
        for cp in last:
            cp.wait_recv()
        for cp in first + relays + passed + [onward for _, onward in relayed]:
            cp.wait_send()
        for cp in own:
            cp.wait()

    return start, relay, forward, finish


N_CHIPS = 4
PAIR_SEMS = (N_CHIPS, N_CHIPS, N_CHIPS - 1, N_CHIPS - 1)


def _pair_reduce_scatter(gp_hbm, own, pair, summed, land, local_sem, sems):
    d2d_send, d2d_recv, ici_send, ici_recv = sems
    x, y, c = lax.axis_index("x"), lax.axis_index("y"), lax.axis_index("c")
    my_chip = 2 * x + y
    sibling = (x, y, 1 - c)
    chips = [(1 - x, y), (x, 1 - y), (1 - x, 1 - y)]
    local = [pltpu.make_async_copy(gp_hbm.at[2 * j + c], own.at[j], local_sem.at[0]) for j in range(N_CHIPS)]
    to_sibling = [
        pltpu.make_async_remote_copy(
            src_ref=gp_hbm.at[2 * j + 1 - c], dst_ref=pair.at[j], send_sem=d2d_send.at[j], recv_sem=d2d_recv.at[j],
            device_id=sibling, device_id_type=MESH,
        )
        for j in range(N_CHIPS)
    ]
    to_chips, from_chips = [], []
    for r, (px, py) in enumerate(chips):
        for dst, bucket in ((land.at[my_chip], to_chips), (land.at[2 * px + py], from_chips)):
            bucket.append(
                pltpu.make_async_remote_copy(
                    src_ref=summed.at[2 * px + py], dst_ref=dst, send_sem=ici_send.at[r], recv_sem=ici_recv.at[r],
                    device_id=(px, py, c), device_id_type=MESH,
                )
            )

    def start():
        for cp in local + to_sibling:
            cp.start()

    def middle():
        for cp in to_sibling:
            cp.wait_recv()
        pltpu.make_async_copy(gp_hbm.at[pl.ds(0, N_CHIPS)], own, local_sem.at[0]).wait()
        for j in range(N_CHIPS):
            summed[j] = (own[j].astype(F32) + pair[j].astype(F32)).astype(BF16)
        for cp in to_chips:
            cp.start()

    def finish():
        for cp in from_chips:
            cp.wait_recv()
        for cp in to_chips + to_sibling:
            cp.wait_send()
        land[my_chip] = summed[my_chip]

    return start, middle, finish


def _exchange(blocks, gather, name):
    n = len(blocks)

    def body(*refs):
        if gather:
            stages = _two_level_gather(refs[:n], refs[n : 2 * n], refs[2 * n :])
        else:
            stages = _direct_exchange(refs[:n], refs[n : 2 * n], refs[2 * n :], False)
        for stage in stages:
            stage()

    return pl.pallas_call(
        body,
        name=name,
        out_shape=_exchange_shapes(blocks, gather),
        in_specs=[ANY_SPEC] * n,
        out_specs=[ANY_SPEC] * n,
        scratch_shapes=_exchange_sems(n),
        compiler_params=pltpu.CompilerParams(has_side_effects=True),
    )(*blocks)


def _adam(w, g, m, v):
    m2 = ADAM_B1 * m + (1.0 - ADAM_B1) * g
    v2 = ADAM_B2 * v + (1.0 - ADAM_B2) * jnp.square(g)
    m_hat = m2 / (1.0 - ADAM_B1**ADAM_STEP)
    v_hat = v2 / (1.0 - ADAM_B2**ADAM_STEP)
    delta = -ADAM_LR * (m_hat / (jnp.sqrt(v_hat) + ADAM_EPS) + ADAM_WD * w)
    return delta, m2, v2


TAIL_STEPS = 8


def _reduce_adam_tail(lands, ws, ms, vs, gp_last, last, smalls):
    nw, ns = len(ws), len(smalls)
    tiles = [w.shape[0] // TAIL_STEPS for w in ws]
    blk = gp_last.shape[1:]
    entries = [(i, e) for i, (_, es) in enumerate(smalls) for e in es]
    n_params = 3 * sum(e[0] is not None for _, e in entries)
    n_small_out = sum(4 if e[0] is not None else 1 for _, e in entries)

    def body(*refs):
        gp_hbm = refs[0]
        part_hbm = refs[1 : 1 + ns]
        refs = refs[1 + ns :]
        land_refs = refs[:nw]
        w_refs, m_refs, v_refs = (refs[nw * (i + 1) : nw * (i + 2)] for i in range(3))
        last_refs = refs[4 * nw : 4 * nw + 3]
        refs = refs[4 * nw + 3 :]
        param_refs = refs[:n_params]
        outs = refs[n_params:]
        big_out = outs[: 4 * nw]
        last_out = outs[4 * nw : 4 * nw + 4]
        small_out = outs[4 * nw + 4 : 4 * nw + 4 + n_small_out]
        scratch = outs[4 * nw + 4 + n_small_out :]
        small_land = scratch[:ns]
        pair_bufs = scratch[ns : ns + 4]
        local_sem = scratch[ns + 4]
        pair_sems = scratch[ns + 5 : ns + 9]
        small_sems = scratch[ns + 9 :]
        step = pl.program_id(0)

        def pair():
            return _pair_reduce_scatter(gp_hbm, *pair_bufs, local_sem, pair_sems)

        def small():
            return _direct_exchange(part_hbm, small_land, small_sems, True)

        @pl.when(step == 0)
        def _():
            pair()[0]()
            small()[0]()

        @pl.when(step == 1)
        def _():
            pair()[1]()

        for t in range(nw):
            g = land_refs[t][0].astype(F32)
            for s in range(1, N_DEV):
                g = g + land_refs[t][s].astype(F32)
            for ref, val in zip(big_out[4 * t : 4 * t + 4], (g,) + _adam(w_refs[t][...], g, m_refs[t][...], v_refs[t][...])):
                ref[...] = val

        @pl.when(step == TAIL_STEPS - 1)
        def _():
            pair()[2]()
            small()[1]()
            land = pair_bufs[3]
            g = land[0].astype(F32)
            for chip in range(1, N_CHIPS):
                g = g + land[chip].astype(F32)
            for ref, val in zip(last_out, (g,) + _adam(last_refs[0][...], g, last_refs[1][...], last_refs[2][...])):
                ref[...] = val
            sums = []
            for i in range(ns):
                g = small_land[i][0].astype(F32)
                for s in range(1, N_DEV):
                    g = g + small_land[i][s].astype(F32)
                sums.append(g)
            params, results = list(param_refs), list(small_out)
            for i, (w, _, _, where) in entries:
                g = sums[i]
                if where is not None:
                    shape = w.shape if w is not None else where[2:]
                    g = g[where[0] : where[0] + shape[0], where[1] : where[1] + shape[1]]
                if w is None:
                    results.pop(0)[...] = g
                    continue
                w_ref, m_ref, v_ref = params[:3]
                del params[:3]
                for val in (g,) + _adam(w_ref[...], g, m_ref[...], v_ref[...]):
                    results.pop(0)[...] = val

    def tile(t):
        return pl.BlockSpec((tiles[t], ws[t].shape[1]), lambda i: (i, 0))

    def land_tile(t):
        return pl.BlockSpec((N_DEV, tiles[t], ws[t].shape[1]), lambda i: (0, i, 0))

    def whole(shape):
        return pl.BlockSpec(shape, lambda step: (0, 0))

    big_sds = [jax.ShapeDtypeStruct(ws[t].shape, F32) for t in range(nw) for _ in range(4)]
    params, small_shapes = [], []
    for _, (w, m, v, where) in entries:
        if w is None:
            small_shapes.append(tuple(where[2:]))
        else:
            params += [w, m, v]
            small_shapes += [w.shape] * 4
    return pl.pallas_call(
        body,
        name="reduce_adam_tail",
        grid=(TAIL_STEPS,),
        out_shape=big_sds
        + [jax.ShapeDtypeStruct(blk, F32)] * 4
        + [jax.ShapeDtypeStruct(shape, F32) for shape in small_shapes],
        in_specs=[ANY_SPEC] * (1 + ns)
        + [land_tile(t) for t in range(nw)]
        + [tile(t) for _ in range(3) for t in range(nw)]
        + [whole(blk)] * 3
        + [whole(p.shape) for p in params],
        out_specs=[tile(t) for t in range(nw) for _ in range(4)]
        + [whole(blk)] * 4
        + [whole(shape) for shape in small_shapes],
        scratch_shapes=[pltpu.VMEM((N_DEV,) + smalls[i][0].shape, smalls[i][0].dtype) for i in range(ns)]
        + [pltpu.VMEM((N_CHIPS,) + blk, BF16)] * 4
        + [pltpu.SemaphoreType.DMA((1,))]
        + [pltpu.SemaphoreType.DMA((count,)) for count in PAIR_SEMS]
        + _exchange_sems(ns),
        compiler_params=pltpu.CompilerParams(
            dimension_semantics=("arbitrary",), vmem_limit_bytes=VMEM_LIMIT, has_side_effects=True
        ),
    )(gp_last, *[s[0] for s in smalls], *lands, *ws, *ms, *vs, *last, *params)


def _fwd_in(x, g1, w_in):
    S = x.shape[0]
    tm = min(512, S)

    def body(x_ref, g_ref, w_ref, hn_ref, u_ref, q_ref, k_ref, v_ref):
        xv = x_ref[...]
        hn = (xv * _rstd(xv) * g_ref[...]).astype(BF16)
        hn_ref[...] = hn
        outs = (u_ref, q_ref, k_ref, v_ref)
        for j in range(N_DEV):
            p = _dot(hn, w_ref[j])
            cols = slice(256 * (j % 2), 256 * (j % 2 + 1))
            if j // 2 == 0:
                u_ref[:, cols] = p
            elif j // 2 == 1:
                q_ref[:, cols] = (p * (HEAD_DIM**-0.5)).astype(BF16)
            else:
                outs[j // 2][:, cols] = p.astype(BF16)

    half = pl.BlockSpec((tm, D_POOL), lambda i: (i, 0))
    return pl.pallas_call(
        body,
        name="fwd_in",
        grid=(S // tm,),
        out_shape=[
            jax.ShapeDtypeStruct((S, D_MODEL), BF16),
            jax.ShapeDtypeStruct((S, D_POOL), F32),
            jax.ShapeDtypeStruct((S, D_ATTN), BF16),
            jax.ShapeDtypeStruct((S, D_ATTN), BF16),
            jax.ShapeDtypeStruct((S, D_ATTN), BF16),
        ],
        in_specs=[
            pl.BlockSpec((tm, D_MODEL), lambda i: (i, 0)),
            pl.BlockSpec((1, D_MODEL), lambda i: (0, 0)),
            pl.BlockSpec((N_DEV, D_MODEL, 256), lambda i: (0, 0, 0)),
        ],
        out_specs=[pl.BlockSpec((tm, D_MODEL), lambda i: (i, 0)), half, half, half, half],
        compiler_params=_params(("parallel",)),
    )(x, g1, w_in)


POOL_CHUNK = 256
POOL_UNROLL = 5
_POOL_W_SPEC = pl.BlockSpec((1, GROUP_DIM, GROUP_DIM), lambda g: (g, 0, 0))
_POOL_SCALE_SPEC = pl.BlockSpec((1, GROUP_DIM), lambda g: (0, g))


def _lane_sum(wide, n):
    out = wide[:, :GROUP_DIM]
    for i in range(1, n):
        out = out + wide[:, GROUP_DIM * i : GROUP_DIM * (i + 1)]
    return out


def _pool_fwd_group(u_ref, pw_ref, sc_ref, pooled_ref, y_ref, g):
    S = u_ref.shape[0]
    R = min(POOL_CHUNK, S)
    w = jnp.left_shift(jnp.int32(2), g)
    d = lax.broadcasted_iota(jnp.int32, (R, R), 0) - lax.broadcasted_iota(jnp.int32, (R, R), 1)
    b_cur = jnp.where((d >= 0) & (d < w), 1.0, 0.0).astype(BF16)
    dp = lax.broadcasted_iota(jnp.int32, (R, HALO), 0) + HALO - lax.broadcasted_iota(jnp.int32, (R, HALO), 1)
    b_prev = jnp.where(dp < w, 1.0, 0.0).astype(BF16)
    pw = pw_ref[0].astype(BF16)
    sc = sc_ref[...]

    def chunk(r0, first):
        cur = u_ref[pl.ds(r0, R), :]
        wide = _dot(b_cur, jnp.concatenate(_split3(cur), axis=1))
        if not first:
            prev = u_ref[pl.ds(pl.multiple_of(r0 - HALO, HALO), HALO), :]
            wide = wide + _dot(b_prev, jnp.concatenate(_split3(prev), axis=1))
        count = jnp.minimum(r0 + lax.broadcasted_iota(jnp.int32, (R, 1), 0) + 1, w).astype(F32)
        pooled = (_lane_sum(wide, 3) / count - cur).astype(BF16)
        pooled_ref[pl.ds(r0, R), :] = pooled
        y_ref[pl.ds(r0, R), :] = _dot(pooled, pw) * sc

    chunk(0, True)

    def rest(c, carry):
        chunk(pl.multiple_of(c * R, R), False)
        return carry

    lax.fori_loop(1, S // R, rest, 0, unroll=POOL_UNROLL)


ATTN_WINDOW = 3
ATTN_TOGETHER = 2
ATTN_TOGETHER_FWD = 3


def _band_ones(T):
    row = lax.broadcasted_iota(jnp.int32, (T, T), 0)
    col = lax.broadcasted_iota(jnp.int32, (T, T), 1)
    after = jnp.where(row > col, 1.0, 0.0).astype(BF16)
    before = jnp.where(row < col, 1.0, 0.0).astype(BF16)
    return jnp.concatenate([after, after], axis=0), jnp.concatenate([before, before], axis=0), col < row


def _log1m(z, mask):
    sp = jnp.log(1.0 + jnp.exp(-jnp.abs(z)))
    l1m = -jnp.maximum(z, 0.0) - sp
    return (l1m if mask is None else jnp.where(mask, l1m, 0.0)), sp


def _log_gates_of(jobs, after2):
    zs = [_dot_nt(qh, kb) for qh, kb, _ in jobs]
    terms = []
    for z, (_, _, mask) in zip(zs, jobs):
        l1m, sp = _log1m(z, mask)
        terms.append(
            (jnp.minimum(z, 0.0) - sp, jnp.concatenate(_split2(l1m), axis=1), jnp.sum(l1m, axis=1, keepdims=True))
        )
    return [(ls, _dot(split, after2), total) for ls, split, total in terms]


def _log_gates(qhs, kbs, masks, after2):
    flat = _log_gates_of([(qh, kb, mask) for kb, mask in zip(kbs, masks) for qh in qhs], after2)
    return [flat[len(qhs) * b : len(qhs) * (b + 1)] for b in range(len(kbs))]


def _tile_start(index, T):
    return index * T if isinstance(index, int) else pl.multiple_of(index * T, T)


def _weights(ls, tail, carry, mask):
    a = jnp.exp(ls + tail + carry)
    return a if mask is None else jnp.where(mask, a, 0.0)


def _reaches(carry):
    return jnp.max(carry) > EXP_UNDERFLOW


def _mixers_fwd(q, k, v, u, pool_w, scale, shards):
    S = q.shape[0]
    T = ATTN_TILE
    nq = S // T
    n = len(shards)
    steps = D_ATTN // LANES

    def body(q_ref, k_ref, v_ref, u_ref, pw_ref, sc_ref, *rest):
        o_ref, pooled_ref, yp_ref = rest[n : n + 3]
        step = pl.program_id(0)

        def gather():
            return _two_level_gather(rest[:n], rest[n + 3 : 2 * n + 3], rest[2 * n + 3 :])

        @pl.when(step == 0)
        def _():
            gather()[0]()

        @pl.when(step == steps // 2)
        def _():
            gather()[1]()

        _pool_fwd_group(u_ref, pw_ref, sc_ref, pooled_ref, yp_ref, step)

        lane = lax.broadcasted_iota(jnp.int32, (T, LANES), 1)
        heads = (lane < HEAD_DIM, lane >= HEAD_DIM)
        after2, _, causal = _band_ones(T)
        zero = jnp.zeros((T, 1), F32)

        def key_tile(k0):
            vb = v_ref[pl.ds(k0, T), :]
            return k_ref[pl.ds(k0, T), :], jnp.concatenate([jnp.where(m, vb, jnp.zeros_like(vb)) for m in heads], axis=0)

        def q_tiles(qis, nb):
            queries, jobs = [], []
            for qi in qis:
                q0 = _tile_start(qi, T)
                qb = q_ref[pl.ds(q0, T), :]
                qhs = [jnp.where(m, qb, jnp.zeros_like(qb)) for m in heads]
                tiles = [key_tile(_tile_start(qi - b, T)) for b in range(nb)]
                queries.append((q0, qhs, [v_rows for _, v_rows in tiles]))
                jobs += [(qh, kb, causal if b == 0 else None) for b, (kb, _) in enumerate(tiles) for qh in qhs]
            gates = _log_gates_of(jobs, after2)
            states = []
            for i, (q0, qhs, values) in enumerate(queries):
                carries, probs = [zero, zero], []
                for b in range(nb):
                    for h in range(2):
                        ls, tail, total = gates[2 * nb * i + 2 * b + h]
                        probs.append(_weights(ls, tail, carries[h], causal if b == 0 else None).astype(BF16))
                        carries[h] = carries[h] + total
                states.append((carries, _dot(jnp.concatenate(probs, axis=1), jnp.concatenate(values, axis=0))))

            def more(st):
                return (st[0] > 0) & _reaches(jnp.maximum(st[1], st[2]))

            for qi, (q0, qhs, _), (carries, acc) in zip(qis, queries, states):

                def k_step(st, qhs=qhs):
                    kj = st[0] - 1
                    kb, v_rows = key_tile(pl.multiple_of(kj * T, T))
                    new, probs = [], []
                    for (ls, tail, total), carry in zip(_log_gates(qhs, [kb], [None], after2)[0], st[1:3]):
                        probs.append(_weights(ls, tail, carry, None).astype(BF16))
                        new.append(carry + total)
                    return kj, new[0], new[1], st[3] + _dot(jnp.concatenate(probs, axis=1), v_rows)

                if not isinstance(qi, int):
                    acc = lax.while_loop(more, k_step, (qi - (nb - 1), carries[0], carries[1], acc))[3]
                o_ref[pl.ds(q0, T), :] = acc

        head_tiles = min(ATTN_WINDOW - 1, nq)
        for qi in range(head_tiles):
            q_tiles([qi], qi + 1)
        assert (nq - head_tiles) % ATTN_TOGETHER_FWD == 0

        def q_loop(i, carry):
            first = head_tiles + ATTN_TOGETHER_FWD * i
            q_tiles([first + j for j in range(ATTN_TOGETHER_FWD)], ATTN_WINDOW)
            return carry

        lax.fori_loop(0, (nq - head_tiles) // ATTN_TOGETHER_FWD, q_loop, 0)

        @pl.when(step == steps - 1)
        def _():
            _, _, forward, finish = gather()
            forward()
            finish()

    blk = pl.BlockSpec((S, LANES), lambda p: (0, p))
    return pl.pallas_call(
        body,
        name="mixers_fwd",
        grid=(steps,),
        out_shape=[
            jax.ShapeDtypeStruct((S, D_ATTN), F32),
            jax.ShapeDtypeStruct((S, D_POOL), BF16),
            jax.ShapeDtypeStruct((S, D_POOL), F32),
        ]
        + _exchange_shapes(shards, True),
        in_specs=[blk, blk, blk, blk, _POOL_W_SPEC, _POOL_SCALE_SPEC] + [ANY_SPEC] * n,
        out_specs=[blk, blk, blk] + [ANY_SPEC] * n,
        scratch_shapes=_exchange_sems(n),
        compiler_params=pltpu.CompilerParams(
            dimension_semantics=("arbitrary",), vmem_limit_bytes=VMEM_LIMIT, has_side_effects=True
        ),
    )(q, k, v, u, pool_w, scale, *shards)


MLP_AHEAD = 1
SG_ROWS = 8


def _mlp_fwd_bwd(x, y_pool, y_attn, target, g_pool, g_attn, g2, gf, w_out, w_up, w_down):
    S = x.shape[0]
    tm = min(256, S)
    fc = D_FF // N_DEV

    def body(x_ref, yp_ref, ya_ref, t_ref, gp_ref, ga_ref, g2_ref, gf_ref, wo_hbm, wu_hbm, wd_hbm,
             mixed_ref, hn2_ref, act_ref, dup_ref, dh2_ref, dh1_ref, dh1b_ref, dyp_ref, dya_ref, sg_ref,
             wo, wu, wd, up_s, sems):
        @pl.when(pl.program_id(0) == 0)
        def _():
            copies = [
                pltpu.make_async_copy(wo_hbm, wo, sems.at[0]),
                pltpu.make_async_copy(wu_hbm, wu, sems.at[1]),
                pltpu.make_async_copy(wd_hbm, wd, sems.at[2]),
            ]
            for cp in copies:
                cp.start()
            for cp in copies:
                cp.wait()
            sg_ref[...] = jnp.zeros_like(sg_ref)

        gp, ga, g2v, gfv = gp_ref[...], ga_ref[...], g2_ref[...], gf_ref[...]
        yp, ya = yp_ref[...], ya_ref[...]
        rp, ra = _rstd(yp), _rstd(ya)
        yph, yah = yp * rp, ya * ra
        mixed = jnp.concatenate([(yph * gp).astype(BF16), (yah * ga).astype(BF16)], axis=1)
        mixed_ref[...] = mixed
        h1 = x_ref[...] + _dot(mixed, wo[...])
        r2 = _rstd(h1)
        h1h = h1 * r2
        hn2 = (h1h * g2v).astype(BF16)
        hn2_ref[...] = hn2
        h2 = h1
        ups = [_dot(hn2, wu[j]) for j in range(MLP_AHEAD)]
        for j in range(N_DEV):
            cols = slice(fc * j, fc * (j + 1))
            if j + MLP_AHEAD < N_DEV:
                ups.append(_dot(hn2, wu[j + MLP_AHEAD]))
            up = ups.pop(0)
            up_s[:, cols] = up
            act = jnp.square(jnp.maximum(up, 0.0)).astype(BF16)
            act_ref[:, cols] = act
            h2 = h2 + _dot(act, wd[cols, :])
        rf = _rstd(h2)
        h2h = h2 * rf
        diff = h2h * gfv - t_ref[...]
        loss_rows = 0.5 * jnp.mean(diff * diff, axis=-1, keepdims=True)
        dy = diff * (1.0 / D_MODEL)
        dh2, dgf = _rms_bwd(dy, h2h, rf, gfv)
        dh2b = dh2.astype(BF16)
        dh2_ref[...] = dh2b
        dhn2 = jnp.zeros((tm, D_MODEL), F32)
        dacts = [_dot_nt(dh2b, wd[fc * j : fc * (j + 1), :]) for j in range(MLP_AHEAD)]
        for j in range(N_DEV):
            cols = slice(fc * j, fc * (j + 1))
            if j + MLP_AHEAD < N_DEV:
                dacts.append(_dot_nt(dh2b, wd[fc * (j + MLP_AHEAD) : fc * (j + MLP_AHEAD + 1), :]))
            dup = (dacts.pop(0) * (2.0 * jnp.maximum(up_s[:, cols], 0.0))).astype(BF16)
            dup_ref[:, cols] = dup
            dhn2 = dhn2 + _dot_nt(dup, wu[j])
        dh1n, dg2 = _rms_bwd(dhn2, h1h, r2, g2v)
        dh1 = dh2 + dh1n
        dh1_ref[...] = dh1
        dh1b = dh1.astype(BF16)
        dh1b_ref[...] = dh1b
        dmix = _dot_nt(dh1b, wo[...])
        dyp, dgp = _rms_bwd(dmix[:, :D_POOL], yph, rp, gp)
        dya, dga = _rms_bwd(dmix[:, D_POOL:], yah, ra, ga)
        dyp_ref[...] = dyp
        dya_ref[...] = dya
        sg_ref[0:1, :] += dgf
        sg_ref[1:2, :] += dg2
        sg_ref[2:3, :] += jnp.concatenate([dgp, dga], axis=1)
        sg_ref[3:4, :] += jnp.broadcast_to(jnp.sum(loss_rows, axis=0, keepdims=True), (1, D_MODEL))

    def tok(n):
        return pl.BlockSpec((tm, n), lambda i: (i, 0))

    def vec(n):
        return pl.BlockSpec((1, n), lambda i: (0, 0))

    any_spec = pl.BlockSpec(memory_space=pl.ANY)
    return pl.pallas_call(
        body,
        name="mlp_fwd_bwd",
        grid=(S // tm,),
        out_shape=[
            jax.ShapeDtypeStruct((S, D_MODEL), BF16),
            jax.ShapeDtypeStruct((S, D_MODEL), BF16),
            jax.ShapeDtypeStruct((S, D_FF), BF16),
            jax.ShapeDtypeStruct((S, D_FF), BF16),
            jax.ShapeDtypeStruct((S, D_MODEL), BF16),
            jax.ShapeDtypeStruct((S, D_MODEL), F32),
            jax.ShapeDtypeStruct((S, D_MODEL), BF16),
            jax.ShapeDtypeStruct((S, D_POOL), F32),
            jax.ShapeDtypeStruct((S, D_ATTN), F32),
            jax.ShapeDtypeStruct((SG_ROWS, D_MODEL), F32),
        ],
        in_specs=[tok(D_MODEL), tok(D_POOL), tok(D_ATTN), tok(D_MODEL), vec(D_POOL), vec(D_ATTN),
                  vec(D_MODEL), vec(D_MODEL), any_spec, any_spec, any_spec],
        out_specs=[tok(D_MODEL), tok(D_MODEL), tok(D_FF), tok(D_FF), tok(D_MODEL), tok(D_MODEL),
                   tok(D_MODEL), tok(D_POOL), tok(D_ATTN),
                   pl.BlockSpec((SG_ROWS, D_MODEL), lambda i: (0, 0))],
        scratch_shapes=[
            pltpu.VMEM((D_MODEL, D_MODEL), BF16),
            pltpu.VMEM((N_DEV, D_MODEL, fc), BF16),
            pltpu.VMEM((D_FF, D_MODEL), BF16),
            pltpu.VMEM((tm, D_FF), F32),
            pltpu.SemaphoreType.DMA((3,)),
        ],
        compiler_params=_params(("arbitrary",)),
    )(x, y_pool, y_attn, target, g_pool, g_attn, g2, gf, w_out, w_up, w_down)


def _wgrad(a, b, block_a, groups, name, travelling=()):
    n = len(travelling)
    S, ka = a.shape
    nb = b.shape[1]
    ts = min(1024, S)
    per = N_DEV // groups
    if block_a:
        ka //= groups
        blk = (ka // per, nb)
        a_spec = pl.BlockSpec((ts, ka), lambda g, s: (s, g))
        b_spec = pl.BlockSpec((ts, nb), lambda g, s: (s, 0))
    else:
        nb //= groups
        blk = (ka, nb // per)
        a_spec = pl.BlockSpec((ts, ka), lambda g, s: (s, 0))
        b_spec = pl.BlockSpec((ts, nb), lambda g, s: (s, g))
    steps = S // ts

    def body(a_ref, b_ref, *rest):
        o_ref, acc = rest[n], rest[2 * n + 1]
        g, s = pl.program_id(0), pl.program_id(1)

        def scatter():
            return _direct_exchange(rest[:n], rest[n + 1 : 2 * n + 1], rest[2 * n + 2 :], False)

        if n:
            @pl.when((g == 0) & (s == 0))
            def _():
                scatter()[0]()

        @pl.when(s == 0)
        def _():
            acc[...] = jnp.zeros_like(acc)

        acc[...] += _dot_tn(a_ref[...], b_ref[...])

        @pl.when(s == steps - 1)
        def _():
            for j in range(per):
                if block_a:
                    o_ref[j] = acc[blk[0] * j : blk[0] * (j + 1), :].astype(BF16)
                else:
                    o_ref[j] = acc[:, blk[1] * j : blk[1] * (j + 1)].astype(BF16)

        if n:
            @pl.when((g == groups - 1) & (s == steps - 1))
            def _():
                scatter()[1]()

    results = pl.pallas_call(
        body,
        name=name,
        grid=(groups, steps),
        out_shape=[jax.ShapeDtypeStruct((N_DEV,) + blk, BF16)] + _exchange_shapes(travelling, False),
        in_specs=[a_spec, b_spec] + [ANY_SPEC] * n,
        out_specs=[pl.BlockSpec((per,) + blk, lambda g, s: (g, 0, 0))] + [ANY_SPEC] * n,
        scratch_shapes=[pltpu.VMEM((ka, nb), F32)] + (_exchange_sems(n) if n else []),
        compiler_params=pltpu.CompilerParams(
            dimension_semantics=("arbitrary", "arbitrary"), vmem_limit_bytes=VMEM_LIMIT, has_side_effects=bool(n)
        ),
    )(a, b, *travelling)
    return results if n else results[0]


def _mixers_bwd(q, k, v, do, dyp, pooled, pool_w, scale, partials):
    S = q.shape[0]
    T = ATTN_TILE
    nq = S // T
    n = len(partials)
    steps = D_ATTN // LANES

    def body(q_ref, k_ref, v_ref, do_ref, dyp_ref, pooled_ref, pw_ref, sc_ref, *rest):
        dq_ref, dk_ref, dv_ref, du_ref, dsc_ref, dpw_ref = rest[n : n + 6]
        dk_acc, dv_acc, carry_s = rest[2 * n + 6 : 2 * n + 9]
        step = pl.program_id(0)

        def scatter():
            return _direct_exchange(rest[:n], rest[n + 6 : 2 * n + 6], rest[2 * n + 9 :], False)

        @pl.when(step == 0)
        def _():
            scatter()[0]()

        _pool_bwd_group(dyp_ref, pooled_ref, pw_ref, sc_ref, du_ref, dsc_ref, dpw_ref, step)

        dk_acc[...] = jnp.zeros_like(dk_acc)
        dv_acc[...] = jnp.zeros_like(dv_acc)
        lane = lax.broadcasted_iota(jnp.int32, (T, LANES), 1)
        heads = (lane < HEAD_DIM, lane >= HEAD_DIM)
        after2, before2, causal = _band_ones(T)
        zero = jnp.zeros((T, 1), F32)

        def key_tile(k0):
            kb = k_ref[pl.ds(k0, T), :]
            k_rows = jnp.concatenate([jnp.where(m, kb, jnp.zeros_like(kb)) for m in heads], axis=0)
            return kb, v_ref[pl.ds(k0, T), :], k_rows

        def grads(items):
            das = [[[_dot_nt(doh, vb) for doh in qs[1]] for _, vb, _ in tiles] for qs, _, tiles, _, _, _, _ in items]
            probs, gs = [], []
            for (_, _, tiles, masks, carry_in, gates, _), da in zip(items, das):
                probs.append([[_weights(gates[b][h][0], gates[b][h][1], carry_in[b][h], masks[b]) for h in range(2)]
                              for b in range(len(tiles))])
                gs.append([[probs[-1][b][h] * da[b][h] for h in range(2)] for b in range(len(tiles))])
            before = [[[_dot(jnp.concatenate(_split2(g), axis=1), before2) for g in row] for row in item] for item in gs]
            dzs, lefts = [], []
            for i, (_, _, tiles, masks, _, gates, st) in enumerate(items):
                g_left, item_dz = [st[0], st[1]], [None] * len(tiles)
                for b in reversed(range(len(tiles))):
                    item_dz[b] = []
                    for h in range(2):
                        sig = jnp.exp(gates[b][h][0])
                        dz = gs[i][b][h] * (1.0 - sig) - sig * (g_left[h] + before[i][b][h])
                        if masks[b] is not None:
                            dz = jnp.where(masks[b], dz, 0.0)
                        item_dz[b].append(dz.astype(BF16))
                        g_left[h] = g_left[h] + jnp.sum(gs[i][b][h], axis=1, keepdims=True)
                dzs.append(item_dz)
                lefts.append(g_left)
            out = []
            for (_, _, tiles, _, _, _, st), item_dz, g_left in zip(items, dzs, lefts):
                dq = st[2] + _dot(
                    jnp.concatenate([dz for row in item_dz for dz in row], axis=1),
                    jnp.concatenate([k_rows for _, _, k_rows in tiles], axis=0),
                )
                out.append((g_left[0], g_left[1], dq))
            for (qs, starts, tiles, _, _, _, _), item_dz, item_probs in zip(items, dzs, probs):
                for b in range(len(tiles)):
                    dk_acc[pl.ds(starts[b], T), :] += _dot_tn(jnp.concatenate(item_dz[b], axis=0), qs[2])
                    dv_acc[pl.ds(starts[b], T), :] += _dot_tn(
                        jnp.concatenate([a.astype(BF16) for a in item_probs[b]], axis=0), qs[3]
                    )
            return out

        def more(st):
            return (st[0] > 0) & _reaches(jnp.maximum(st[1], st[2]))

        def q_tiles(qis, nb):
            masks = [causal] + [None] * (nb - 1)
            prepared, jobs = [], []
            for qi in qis:
                q0 = _tile_start(qi, T)
                qb = q_ref[pl.ds(q0, T), :]
                dob = do_ref[pl.ds(q0, T), :]
                qhs = [jnp.where(m, qb, jnp.zeros_like(qb)) for m in heads]
                dohs = [jnp.where(m, dob, 0.0).astype(BF16) for m in heads]
                queries = (qhs, dohs, jnp.concatenate(qhs, axis=0), jnp.concatenate(dohs, axis=0))
                starts = [_tile_start(qi - b, T) for b in range(nb)]
                tiles = [key_tile(k0) for k0 in starts]
                prepared.append((q0, queries, starts, tiles))
                jobs += [(qh, kb, mask) for (kb, _, _), mask in zip(tiles, masks) for qh in qhs]
            flat = _log_gates_of(jobs, after2)
            items = []
            for i, (qi, (q0, queries, starts, tiles)) in enumerate(zip(qis, prepared)):
                gates = [flat[2 * nb * i + 2 * b : 2 * nb * i + 2 * b + 2] for b in range(nb)]
                carries, carry_in = [zero, zero], []
                for b in range(nb):
                    carry_in.append(list(carries))
                    carries = [carries[h] + gates[b][h][2] for h in range(2)]
                st = (zero, zero, jnp.zeros((T, LANES), F32))
                if not isinstance(qi, int):
                    k_left = qi - (nb - 1)
                    qhs = queries[0]

                    def right_to_left(st, qhs=qhs):
                        kj = st[0] - 1
                        carry_s[0, kj] = st[1]
                        carry_s[1, kj] = st[2]
                        kb = k_ref[pl.ds(pl.multiple_of(kj * T, T), T), :]
                        sums = [jnp.sum(_log1m(_dot_nt(qh, kb), None)[0], axis=1, keepdims=True) for qh in qhs]
                        return kj, st[1] + sums[0], st[2] + sums[1]

                    k_first = lax.while_loop(more, right_to_left, (k_left, carries[0], carries[1]))[0]

                    def left_to_right(kj, st, queries=queries):
                        k0 = pl.multiple_of(kj * T, T)
                        tile = key_tile(k0)
                        left_gates = _log_gates(queries[0], [tile[0]], [None], after2)
                        carry = [[carry_s[0, kj], carry_s[1, kj]]]
                        return grads([(queries, [k0], [tile], [None], carry, left_gates, st)])[0]

                    st = lax.fori_loop(k_first, k_left, left_to_right, st)
                items.append((queries, starts, tiles, masks, carry_in, gates, st))
            for (q0, _, _, _), st in zip(prepared, grads(items)):
                dq_ref[pl.ds(q0, T), :] = (st[2] * (HEAD_DIM**-0.5)).astype(BF16)

        head_tiles = min(ATTN_WINDOW - 1, nq)
        for qi in range(head_tiles):
            q_tiles([qi], qi + 1)
        assert (nq - head_tiles) % ATTN_TOGETHER == 0

        def q_loop(i, carry):
            first = head_tiles + ATTN_TOGETHER * i
            q_tiles([first + j for j in range(ATTN_TOGETHER)], ATTN_WINDOW)
            return carry

        lax.fori_loop(0, (nq - head_tiles) // ATTN_TOGETHER, q_loop, 0)
        dk_ref[...] = dk_acc[...].astype(BF16)
        dv_ref[...] = dv_acc[...].astype(BF16)

        @pl.when(step == steps - 1)
        def _():
            scatter()[1]()

    blk = pl.BlockSpec((S, LANES), lambda p: (0, p))
    sds = jax.ShapeDtypeStruct((S, D_ATTN), BF16)
    return pl.pallas_call(
        body,
        name="mixers_bwd",
        grid=(steps,),
        out_shape=[
            sds,
            sds,
            sds,
            jax.ShapeDtypeStruct((S, D_POOL), BF16),
            jax.ShapeDtypeStruct((8, D_POOL), F32),
            jax.ShapeDtypeStruct((N_GROUPS, GROUP_DIM, GROUP_DIM), BF16),
        ]
        + _exchange_shapes(partials, False),
        in_specs=[blk] * 6 + [_POOL_W_SPEC, _POOL_SCALE_SPEC] + [ANY_SPEC] * n,
        out_specs=[blk] * 4 + [pl.BlockSpec((8, GROUP_DIM), lambda p: (0, p)), _POOL_W_SPEC] + [ANY_SPEC] * n,
        scratch_shapes=[pltpu.VMEM((S, LANES), F32), pltpu.VMEM((S, LANES), F32), pltpu.VMEM((2, nq, T, 1), F32)]
        + _exchange_sems(n),
        compiler_params=pltpu.CompilerParams(
            dimension_semantics=("arbitrary",), vmem_limit_bytes=VMEM_LIMIT, has_side_effects=True
        ),
    )(q, k, v, do, dyp, pooled, pool_w, scale, *partials)


def _pool_bwd_group(dy_ref, pooled_ref, pw_ref, sc_ref, du_ref, dsc_ref, dpw_ref, g):
    S = dy_ref.shape[0]
    R = min(POOL_CHUNK, S)
    nc = S // R
    w = jnp.left_shift(jnp.int32(2), g)
    pw = pw_ref[0].astype(BF16)
    sc = sc_ref[...]
    d = lax.broadcasted_iota(jnp.int32, (R, R), 1) - lax.broadcasted_iota(jnp.int32, (R, R), 0)
    bt_cur = jnp.where((d >= 0) & (d < w), 1.0, 0.0).astype(BF16)
    dn = lax.broadcasted_iota(jnp.int32, (R, HALO), 1) + R - lax.broadcasted_iota(jnp.int32, (R, HALO), 0)
    bt_next = jnp.where(dn < w, 1.0, 0.0).astype(BF16)

    def per_count(dpl, r0):
        n = dpl.shape[0]
        count = jnp.minimum(r0 + lax.broadcasted_iota(jnp.int32, (n, 1), 0) + 1, w).astype(F32)
        return jnp.concatenate(_split2(dpl / count), axis=1)

    def chunk(r0, last, sums):
        dyv = dy_ref[pl.ds(r0, R), :]
        pooled = pooled_ref[pl.ds(r0, R), :]
        dmapped = (dyv * sc).astype(BF16)
        dsc = sums[0] + jnp.sum(dyv * _dot(pooled, pw), axis=0, keepdims=True)
        dpw = sums[1] + _dot_tn(pooled, dmapped)
        dpl = _dot_nt(dmapped, pw)
        wide = _dot(bt_cur, per_count(dpl, r0))
        if not last:
            nxt = pl.multiple_of(r0 + R, R)
            dpl_next = _dot_nt((dy_ref[pl.ds(nxt, HALO), :] * sc).astype(BF16), pw)
            wide = wide + _dot(bt_next, per_count(dpl_next, nxt))
        du_ref[pl.ds(r0, R), :] = (_lane_sum(wide, 2) - dpl).astype(BF16)
        return dsc, dpw

    sums = (jnp.zeros((1, GROUP_DIM), F32), jnp.zeros((GROUP_DIM, GROUP_DIM), F32))
    sums = lax.fori_loop(
        0, nc - 1, lambda c, s: chunk(pl.multiple_of(c * R, R), False, s), sums, unroll=POOL_UNROLL
    )
    dsc, dpw = chunk((nc - 1) * R, True, sums)
    dsc_ref[...] = jnp.zeros_like(dsc_ref)
    dsc_ref[0:1, :] = dsc
    dpw_ref[0] = dpw.astype(dpw_ref.dtype)


def _bwd_in(du, dq, dk, dv, w_in, x, dh1, g1):
    S = x.shape[0]
    tm = min(512, S)

    def body(du_ref, dq_ref, dk_ref, dv_ref, w_ref, x_ref, dh1_ref, g_ref, dx_ref, dproj_ref, dg_ref):
        @pl.when(pl.program_id(0) == 0)
        def _():
            dg_ref[...] = jnp.zeros_like(dg_ref)

        parts = (du_ref[...], dq_ref[...], dk_ref[...], dv_ref[...])
        dhn = jnp.zeros((tm, D_MODEL), F32)
        for j in range(N_DEV):
            piece = parts[j // 2][:, 256 * (j % 2) : 256 * (j % 2 + 1)]
            dproj_ref[:, 256 * j : 256 * (j + 1)] = piece
            dhn = dhn + _dot_nt(piece, w_ref[j])
        xv = x_ref[...]
        r = _rstd(xv)
        dxn, dg = _rms_bwd(dhn, xv * r, r, g_ref[...])
        dx_ref[...] = dh1_ref[...] + dxn
        dg_ref[0:1, :] += dg

    half = pl.BlockSpec((tm, D_POOL), lambda i: (i, 0))
    full = pl.BlockSpec((tm, D_MODEL), lambda i: (i, 0))
    return pl.pallas_call(
        body,
        name="bwd_in",
        grid=(S // tm,),
        out_shape=[
            jax.ShapeDtypeStruct((S, D_MODEL), F32),
            jax.ShapeDtypeStruct((S, D_IN_PROJ), BF16),
            jax.ShapeDtypeStruct((8, D_MODEL), F32),
        ],
        in_specs=[half, half, half, half,
                  pl.BlockSpec((N_DEV, D_MODEL, 256), lambda i: (0, 0, 0)),
                  full, full, pl.BlockSpec((1, D_MODEL), lambda i: (0, 0))],
        out_specs=[full, pl.BlockSpec((tm, D_IN_PROJ), lambda i: (i, 0)),
                   pl.BlockSpec((8, D_MODEL), lambda i: (0, 0))],
        compiler_params=_params(("arbitrary",)),
    )(du, dq, dk, dv, w_in, x, dh1, g1)


def _rows(a):
    a = a.reshape(-1, LANES)
    pad = (-a.shape[0]) % 8
    return jnp.pad(a, ((0, pad), (0, 0))) if pad else a


def kernel(x, norm1_g, w_in, pool_w, pool_scale, pool_out_g, attn_out_g, w_out, norm2_g, w_up, w_down, final_g, loss_target, m_norm1_g, m_w_in, m_pool_w, m_pool_scale, m_pool_out_g, m_attn_out_g, m_w_out, m_norm2_g, m_w_up, m_w_down, m_final_g, v_norm1_g, v_w_in, v_pool_w, v_pool_scale, v_pool_out_g, v_attn_out_g, v_w_out, v_norm2_g, v_w_up, v_w_down, v_final_g):
    S = x.shape[1]
    xs = x.reshape(S, D_MODEL)
    tgt = loss_target.reshape(S, D_MODEL)
    row = lambda a: a.reshape(1, -1)

    (w_in_g,) = _exchange([w_in.astype(BF16)], True, "gather_w_in")
    hn, u_pool, q, k, v = _fwd_in(xs, row(norm1_g), w_in_g)
    y_attn, pooled, y_pool, w_out_g, w_up_g, w_down_g = _mixers_fwd(
        q, k, v, u_pool, pool_w, row(pool_scale), [w_out.astype(BF16), w_up.astype(BF16), w_down.astype(BF16)]
    )
    w_out_full = w_out_g.reshape(D_MODEL, D_MODEL)
    w_down_full = w_down_g.reshape(D_FF, D_MODEL)
    mixed, hn2, act, dup, dh2b, dh1, dh1b, dyp, dya, sg = _mlp_fwd_bwd(
        xs, y_pool, y_attn, tgt, row(pool_out_g), row(attn_out_g), row(norm2_g), row(final_g),
        w_out_full, w_up_g, w_down_full,
    )
    gp_down = _wgrad(act, dh2b, True, 2, "wgrad_down")
    gp_up = _wgrad(hn2, dup, False, 2, "wgrad_up")
    gp_out = _wgrad(mixed, dh1b, True, 1, "wgrad_out")
    dq, dk, dv, du, dsc, dpw, land_up, land_down = _mixers_bwd(
        q, k, v, dya, dyp, pooled, pool_w, row(pool_scale), [gp_up, gp_down]
    )
    dx, dproj, dg1 = _bwd_in(du, dq, dk, dv, w_in_g, xs, dh1, row(norm1_g))
    gp_in, land_out = _wgrad(hn, dproj, False, 1, "wgrad_in", [gp_out])

    def vectors(final, norm2, pool_out, attn_out, norm1, scale, last):
        pieces = [final, norm2, jnp.concatenate([pool_out, attn_out]), norm1, scale, last]
        return jnp.concatenate([_rows(p) for p in pieces], axis=0)

    no_loss = jnp.zeros((8, LANES), F32)
    flat_pool = lambda a: a.reshape(N_GROUPS * GROUP_DIM, GROUP_DIM)
    smalls = [
        (
            vectors(sg[0], sg[1], sg[2][:D_POOL], sg[2][D_POOL:], dg1[0], dsc[0], sg[3]),
            [(
                vectors(final_g, norm2_g, pool_out_g, attn_out_g, norm1_g, pool_scale, no_loss),
                vectors(m_final_g, m_norm2_g, m_pool_out_g, m_attn_out_g, m_norm1_g, m_pool_scale, no_loss),
                vectors(v_final_g, v_norm2_g, v_pool_out_g, v_attn_out_g, v_norm1_g, v_pool_scale, no_loss),
                None,
            )],
        ),
        (flat_pool(dpw), [(flat_pool(pool_w), flat_pool(m_pool_w), flat_pool(v_pool_w), None)]),
    ]
    tail = _reduce_adam_tail(
        [land_out, land_up, land_down], [w_out, w_up, w_down], [m_w_out, m_w_up, m_w_down],
        [v_w_out, v_w_up, v_w_down], gp_in, (w_in, m_w_in, v_w_in), smalls,
    )
    big = {name: tail[4 * t : 4 * t + 4] for t, name in enumerate(("w_out", "w_up", "w_down", "w_in"))}

    def unpack(vec, pw):
        out = {}
        for i, name in enumerate(("final_g", "norm2_g", "mix_g", "norm1_g", "pool_scale")):
            out[name] = vec[8 * i : 8 * i + 8].reshape(-1)
        out["pool_scale"] = out["pool_scale"][:D_POOL]
        out["pool_out_g"], out["attn_out_g"] = out["mix_g"][:D_POOL], out["mix_g"][D_POOL:]
        out["pool_w"] = pw.reshape(N_GROUPS, GROUP_DIM, GROUP_DIM)
        return out, vec[40, 0]

    small_out = [unpack(tail[16 + i], tail[20 + i]) for i in range(4)]
    loss = small_out[0][1]
    order = ("norm1_g", "w_in", "pool_w", "pool_scale", "pool_out_g", "attn_out_g", "w_out", "norm2_g", "w_up",
             "w_down", "final_g")
    outs = [loss, dx.reshape(1, S, D_MODEL)]
    for i in range(4):
        for name in order:
            outs.append(big[name][i] if name in big else small_out[i][0][name])
    return tuple(outs)
```

```python
import jax
import jax.numpy as jnp
from jax import lax
from jax.experimental import pallas as pl
from jax.experimental.pallas import tpu as pltpu

F32 = jnp.float32
BF16 = jnp.bfloat16
MESH = pl.DeviceIdType.MESH

N_DEV = 8
D_MODEL = 1024
D_POOL = 512
D_ATTN = 512
N_GROUPS = 4
GROUP_DIM = 128
HEAD_DIM = 64
D_FF = 4096
D_IN_PROJ = 2048
EPS = 1e-6
HALO = 16
ATTN_TILE = 128
LANES = 128
EXP_UNDERFLOW = -104.0

ADAM_LR = 0.001
ADAM_B1 = 0.9
ADAM_B2 = 0.999
ADAM_EPS = 1e-08
ADAM_WD = 0.01
ADAM_STEP = 10

VMEM_LIMIT = 56 * 1024 * 1024


def _params(semantics=None, vmem=VMEM_LIMIT):
    return pltpu.CompilerParams(dimension_semantics=semantics, vmem_limit_bytes=vmem)


def _dot(a, b):
    return jnp.dot(a, b, preferred_element_type=F32)


def _dot_nt(a, b):
    return lax.dot_general(a, b, (((1,), (1,)), ((), ())), preferred_element_type=F32)


def _dot_tn(a, b):
    return lax.dot_general(a, b, (((0,), (0,)), ((), ())), preferred_element_type=F32)


def _split2(x):
    hi = x.astype(BF16)
    lo = (x - hi.astype(F32)).astype(BF16)
    return hi, lo


def _split3(x):
    hi = x.astype(BF16)
    r = x - hi.astype(F32)
    mid = r.astype(BF16)
    lo = (r - mid.astype(F32)).astype(BF16)
    return hi, mid, lo


def _rstd(h):
    return lax.rsqrt(jnp.mean(h * h, axis=-1, keepdims=True) + EPS)


def _rms_bwd(dout, hhat, r, g):
    dg = jnp.sum(dout * hhat, axis=0, keepdims=True)
    dxh = dout * g
    dh = r * (dxh - hhat * jnp.mean(dxh * hhat, axis=-1, keepdims=True))
    return dh, dg


def _my_index():
    return 4 * lax.axis_index("x") + 2 * lax.axis_index("y") + lax.axis_index("c")


def _peer(k):
    x, y, c = lax.axis_index("x"), lax.axis_index("y"), lax.axis_index("c")
    px = 1 - x if (k >> 2) & 1 else x
    py = 1 - y if (k >> 1) & 1 else y
    pc = 1 - c if k & 1 else c
    return (px, py, pc), 4 * px + 2 * py + pc


N_PEERS = N_DEV - 1
ANY_SPEC = pl.BlockSpec(memory_space=pl.ANY)


def _exchange_sems(n):
    return [
        pltpu.SemaphoreType.DMA((n * N_PEERS,)),
        pltpu.SemaphoreType.DMA((n * N_PEERS,)),
        pltpu.SemaphoreType.DMA((n,)),
    ]


def _exchange_shapes(blocks, gather):
    if gather:
        return [jax.ShapeDtypeStruct((N_DEV,) + b.shape, b.dtype) for b in blocks]
    return [jax.ShapeDtypeStruct(b.shape, b.dtype) for b in blocks]


def _direct_exchange(ins, outs, sems, gather):
    send_sems, recv_sems, local_sems = sems
    me = _my_index()
    own, sends, recvs = [], [], []
    for t in range(len(ins)):
        own.append(pltpu.make_async_copy(ins[t] if gather else ins[t].at[me], outs[t].at[me], local_sems.at[t]))
        for k in range(1, N_DEV):
            peer, peer_idx = _peer(k)
            src = ins[t] if gather else ins[t].at[peer_idx]
            for dst, bucket in ((outs[t].at[me], sends), (outs[t].at[peer_idx], recvs)):
                bucket.append(
                    pltpu.make_async_remote_copy(
                        src_ref=src,
                        dst_ref=dst,
                        send_sem=send_sems.at[t * N_PEERS + k - 1],
                        recv_sem=recv_sems.at[t * N_PEERS + k - 1],
                        device_id=peer,
                        device_id_type=MESH,
                    )
                )

    def start():
        for cp in own + sends:
            cp.start()

    def finish():
        for cp in recvs:
            cp.wait_recv()
        for cp in sends:
            cp.wait_send()
        for cp in own:
            cp.wait()

    return start, finish


def _flip(a, bit):
    return a + bit - 2 * a * bit


def _two_level_gather(ins, outs, sems):
    send_sems, recv_sems, local_sems = sems
    x, y, c = lax.axis_index("x"), lax.axis_index("y"), lax.axis_index("c")
    me, sibling = (x, y, c), (x, y, 1 - c)
    x_nbr, y_nbr, diagonal = (1 - x, y, c), (x, 1 - y, c), (1 - x, 1 - y, c)
    relay_of = (_flip(x, 1 - c), _flip(y, c), c)
    relay_to = (_flip(x, c), _flip(y, 1 - c), c)

    def copy(t, k, block, to, from_input=False):
        slot = outs[t].at[4 * block[0] + 2 * block[1] + block[2]]
        return pltpu.make_async_remote_copy(
            src_ref=ins[t] if from_input else slot,
            dst_ref=slot,
            send_sem=send_sems.at[t * N_PEERS + k],
            recv_sem=recv_sems.at[t * N_PEERS + k],
            device_id=to,
            device_id_type=MESH,
        )

    arrays = range(len(ins))
    own = [pltpu.make_async_copy(ins[t], outs[t].at[4 * x + 2 * y + c], local_sems.at[t]) for t in arrays]
    first = [copy(t, k, me, to, True) for t in arrays for k, to in ((1, x_nbr), (2, y_nbr), (0, sibling))]
    landed = [copy(t, k, block, me) for t in arrays for k, block in ((1, x_nbr), (2, y_nbr))]
    relays = [copy(t, 3, relay_of, relay_to) for t in arrays]
    passed = [copy(t, 3 + k, block, sibling) for t in arrays for k, block in ((1, x_nbr), (2, y_nbr))]
    relayed = [(copy(t, 3, diagonal, me), copy(t, 6, diagonal, sibling)) for t in arrays]
    last = [copy(t, 0, sibling, me) for t in arrays]
    last += [copy(t, 3 + k, (*block[:2], 1 - c), me) for t in arrays for k, block in ((1, x_nbr), (2, y_nbr), (3, diagonal))]

    def start():
        for cp in own + first:
            cp.start()

    def relay():
        for cp in landed:
            cp.wait_recv()
        for cp in relays + passed:
            cp.start()

    def forward():
        for arrived, onward in relayed:
            arrived.wait_recv()
            onward.start()

    def finish():
        for cp in last:
            cp.wait_recv()
        for cp in first + relays + passed + [onward for _, onward in relayed]:
            cp.wait_send()
        for cp in own:
            cp.wait()

    return start, relay, forward, finish


N_CHIPS = 4
PAIR_SEMS = (N_CHIPS, N_CHIPS, N_CHIPS - 1, N_CHIPS - 1)


def _pair_reduce_scatter(gp_hbm, own, pair, summed, land, local_sem, sems):
    d2d_send, d2d_recv, ici_send, ici_recv = sems
    x, y, c = lax.axis_index("x"), lax.axis_index("y"), lax.axis_index("c")
    my_chip = 2 * x + y
    sibling = (x, y, 1 - c)
    chips = [(1 - x, y), (x, 1 - y), (1 - x, 1 - y)]
    local = [pltpu.make_async_copy(gp_hbm.at[2 * j + c], own.at[j], local_sem.at[0]) for j in range(N_CHIPS)]
    to_sibling = [
        pltpu.make_async_remote_copy(
            src_ref=gp_hbm.at[2 * j + 1 - c], dst_ref=pair.at[j], send_sem=d2d_send.at[j], recv_sem=d2d_recv.at[j],
            device_id=sibling, device_id_type=MESH,
        )
        for j in range(N_CHIPS)
    ]
    to_chips, from_chips = [], []
    for r, (px, py) in enumerate(chips):
        for dst, bucket in ((land.at[my_chip], to_chips), (land.at[2 * px + py], from_chips)):
            bucket.append(
                pltpu.make_async_remote_copy(
                    src_ref=summed.at[2 * px + py], dst_ref=dst, send_sem=ici_send.at[r], recv_sem=ici_recv.at[r],
                    device_id=(px, py, c), device_id_type=MESH,
                )
            )

    def start():
        for cp in local + to_sibling:
            cp.start()

    def middle():
        for cp in to_sibling:
            cp.wait_recv()
        pltpu.make_async_copy(gp_hbm.at[pl.ds(0, N_CHIPS)], own, local_sem.at[0]).wait()
        for j in range(N_CHIPS):
            summed[j] = (own[j].astype(F32) + pair[j].astype(F32)).astype(BF16)
        for cp in to_chips:
            cp.start()

    def finish():
        for cp in from_chips:
            cp.wait_recv()
        for cp in to_chips + to_sibling:
            cp.wait_send()
        land[my_chip] = summed[my_chip]

    return start, middle, finish


def _exchange(blocks, gather, name):
    n = len(blocks)

    def body(*refs):
        if gather:
            stages = _two_level_gather(refs[:n], refs[n : 2 * n], refs[2 * n :])
        else:
            stages = _direct_exchange(refs[:n], refs[n : 2 * n], refs[2 * n :], False)
        for stage in stages:
            stage()

    return pl.pallas_call(
        body,
        name=name,
        out_shape=_exchange_shapes(blocks, gather),
        in_specs=[ANY_SPEC] * n,
        out_specs=[ANY_SPEC] * n,
        scratch_shapes=_exchange_sems(n),
        compiler_params=pltpu.CompilerParams(has_side_effects=True),
    )(*blocks)


def _adam(w, g, m, v):
    m2 = ADAM_B1 * m + (1.0 - ADAM_B1) * g
    v2 = ADAM_B2 * v + (1.0 - ADAM_B2) * jnp.square(g)
    m_hat = m2 / (1.0 - ADAM_B1**ADAM_STEP)
    v_hat = v2 / (1.0 - ADAM_B2**ADAM_STEP)
    delta = -ADAM_LR * (m_hat / (jnp.sqrt(v_hat) + ADAM_EPS) + ADAM_WD * w)
    return delta, m2, v2


TAIL_STEPS = 8


def _reduce_adam_tail(lands, ws, ms, vs, gp_last, last, smalls):
    nw, ns = len(ws), len(smalls)
    tiles = [w.shape[0] // TAIL_STEPS for w in ws]
    blk = gp_last.shape[1:]
    entries = [(i, e) for i, (_, es) in enumerate(smalls) for e in es]
    n_params = 3 * sum(e[0] is not None for _, e in entries)
    n_small_out = sum(4 if e[0] is not None else 1 for _, e in entries)

    def body(*refs):
        gp_hbm = refs[0]
        part_hbm = refs[1 : 1 + ns]
        refs = refs[1 + ns :]
        land_refs = refs[:nw]
        w_refs, m_refs, v_refs = (refs[nw * (i + 1) : nw * (i + 2)] for i in range(3))
        last_refs = refs[4 * nw : 4 * nw + 3]
        refs = refs[4 * nw + 3 :]
        param_refs = refs[:n_params]
        outs = refs[n_params:]
        big_out = outs[: 4 * nw]
        last_out = outs[4 * nw : 4 * nw + 4]
        small_out = outs[4 * nw + 4 : 4 * nw + 4 + n_small_out]
        scratch = outs[4 * nw + 4 + n_small_out :]
        small_land = scratch[:ns]
        pair_bufs = scratch[ns : ns + 4]
        local_sem = scratch[ns + 4]
        pair_sems = scratch[ns + 5 : ns + 9]
        small_sems = scratch[ns + 9 :]
        step = pl.program_id(0)

        def pair():
            return _pair_reduce_scatter(gp_hbm, *pair_bufs, local_sem, pair_sems)

        def small():
            return _direct_exchange(part_hbm, small_land, small_sems, True)

        @pl.when(step == 0)
        def _():
            pair()[0]()
            small()[0]()

        @pl.when(step == 1)
        def _():
            pair()[1]()

        for t in range(nw):
            g = land_refs[t][0].astype(F32)
            for s in range(1, N_DEV):
                g = g + land_refs[t][s].astype(F32)
            for ref, val in zip(big_out[4 * t : 4 * t + 4], (g,) + _adam(w_refs[t][...], g, m_refs[t][...], v_refs[t][...])):
                ref[...] = val

        @pl.when(step == TAIL_STEPS - 1)
        def _():
            pair()[2]()
            small()[1]()
            land = pair_bufs[3]
            g = land[0].astype(F32)
            for chip in range(1, N_CHIPS):
                g = g + land[chip].astype(F32)
            for ref, val in zip(last_out, (g,) + _adam(last_refs[0][...], g, last_refs[1][...], last_refs[2][...])):
                ref[...] = val
            sums = []
            for i in range(ns):
                g = small_land[i][0].astype(F32)
                for s in range(1, N_DEV):
                    g = g + small_land[i][s].astype(F32)
                sums.append(g)
            params, results = list(param_refs), list(small_out)
            for i, (w, _, _, where) in entries:
                g = sums[i]
                if where is not None:
                    shape = w.shape if w is not None else where[2:]
                    g = g[where[0] : where[0] + shape[0], where[1] : where[1] + shape[1]]
                if w is None:
                    results.pop(0)[...] = g
                    continue
                w_ref, m_ref, v_ref = params[:3]
                del params[:3]
                for val in (g,) + _adam(w_ref[...], g, m_ref[...], v_ref[...]):
                    results.pop(0)[...] = val

    def tile(t):
        return pl.BlockSpec((tiles[t], ws[t].shape[1]), lambda i: (i, 0))

    def land_tile(t):
        return pl.BlockSpec((N_DEV, tiles[t], ws[t].shape[1]), lambda i: (0, i, 0))

    def whole(shape):
        return pl.BlockSpec(shape, lambda step: (0, 0))

    big_sds = [jax.ShapeDtypeStruct(ws[t].shape, F32) for t in range(nw) for _ in range(4)]
    params, small_shapes = [], []
    for _, (w, m, v, where) in entries:
        if w is None:
            small_shapes.append(tuple(where[2:]))
        else:
            params += [w, m, v]
            small_shapes += [w.shape] * 4
    return pl.pallas_call(
        body,
        name="reduce_adam_tail",
        grid=(TAIL_STEPS,),
        out_shape=big_sds
        + [jax.ShapeDtypeStruct(blk, F32)] * 4
        + [jax.ShapeDtypeStruct(shape, F32) for shape in small_shapes],
        in_specs=[ANY_SPEC] * (1 + ns)
        + [land_tile(t) for t in range(nw)]
        + [tile(t) for _ in range(3) for t in range(nw)]
        + [whole(blk)] * 3
        + [whole(p.shape) for p in params],
        out_specs=[tile(t) for t in range(nw) for _ in range(4)]
        + [whole(blk)] * 4
        + [whole(shape) for shape in small_shapes],
        scratch_shapes=[pltpu.VMEM((N_DEV,) + smalls[i][0].shape, smalls[i][0].dtype) for i in range(ns)]
        + [pltpu.VMEM((N_CHIPS,) + blk, BF16)] * 4
        + [pltpu.SemaphoreType.DMA((1,))]
        + [pltpu.SemaphoreType.DMA((count,)) for count in PAIR_SEMS]
        + _exchange_sems(ns),
        compiler_params=pltpu.CompilerParams(
            dimension_semantics=("arbitrary",), vmem_limit_bytes=VMEM_LIMIT, has_side_effects=True
        ),
    )(gp_last, *[s[0] for s in smalls], *lands, *ws, *ms, *vs, *last, *params)


VECTOR_ROWS = 48


def _fwd_in(x, g1, w_in, shards, vector_sets):
    S = x.shape[0]
    tm = min(512, S)
    steps = S // tm
    n = len(shards)
    nv = sum(len(vs) for vs in vector_sets)

    def body(x_ref, g_ref, w_ref, *rest):
        hn_ref, u_ref, q_ref, k_ref, v_ref = rest[n + nv : n + nv + 5]
        for src, dst in zip(rest[:n], rest[n + nv + 5 :]):
            dst[...] = src[...].astype(BF16)
        vectors = list(rest[n : n + nv])
        for vs, packed in zip(vector_sets, rest[2 * n + nv + 5 :]):
            at = 0
            for _ in vs:
                ref = vectors.pop(0)
                packed[at : at + ref.shape[0], :] = ref[...]
                at += ref.shape[0]
            packed[at:, :] = jnp.zeros((VECTOR_ROWS - at, LANES), F32)
        xv = x_ref[...]
        hn = (xv * _rstd(xv) * g_ref[...]).astype(BF16)
        hn_ref[...] = hn
        outs = (u_ref, q_ref, k_ref, v_ref)
        for j in range(N_DEV):
            p = _dot(hn, w_ref[j])
            cols = slice(256 * (j % 2), 256 * (j % 2 + 1))
            if j // 2 == 0:
                u_ref[:, cols] = p
            elif j // 2 == 1:
                q_ref[:, cols] = (p * (HEAD_DIM**-0.5)).astype(BF16)
            else:
                outs[j // 2][:, cols] = p.astype(BF16)

    half = pl.BlockSpec((tm, D_POOL), lambda i: (i, 0))
    shard_tiles = [pl.BlockSpec((s.shape[0] // steps, s.shape[1]), lambda i: (i, 0)) for s in shards]
    return pl.pallas_call(
        body,
        name="fwd_in",
        grid=(steps,),
        out_shape=[
            jax.ShapeDtypeStruct((S, D_MODEL), BF16),
            jax.ShapeDtypeStruct((S, D_POOL), F32),
            jax.ShapeDtypeStruct((S, D_ATTN), BF16),
            jax.ShapeDtypeStruct((S, D_ATTN), BF16),
            jax.ShapeDtypeStruct((S, D_ATTN), BF16),
        ]
        + [jax.ShapeDtypeStruct(s.shape, BF16) for s in shards]
        + [jax.ShapeDtypeStruct((VECTOR_ROWS, LANES), F32)] * len(vector_sets),
        in_specs=[
            pl.BlockSpec((tm, D_MODEL), lambda i: (i, 0)),
            pl.BlockSpec((1, D_MODEL), lambda i: (0, 0)),
            pl.BlockSpec((N_DEV, D_MODEL, 256), lambda i: (0, 0, 0)),
        ]
        + shard_tiles
        + [pl.BlockSpec(v.shape, lambda i: (0, 0)) for vs in vector_sets for v in vs],
        out_specs=[pl.BlockSpec((tm, D_MODEL), lambda i: (i, 0)), half, half, half, half]
        + shard_tiles
        + [pl.BlockSpec((VECTOR_ROWS, LANES), lambda i: (0, 0))] * len(vector_sets),
        compiler_params=_params(("arbitrary",)),
    )(x, g1, w_in, *shards, *[v for vs in vector_sets for v in vs])


POOL_CHUNK = 256
POOL_UNROLL = 5
_POOL_W_SPEC = pl.BlockSpec((1, GROUP_DIM, GROUP_DIM), lambda g: (g, 0, 0))
_POOL_SCALE_SPEC = pl.BlockSpec((1, GROUP_DIM), lambda g: (0, g))


def _lane_sum(wide, n):
    out = wide[:, :GROUP_DIM]
    for i in range(1, n):
        out = out + wide[:, GROUP_DIM * i : GROUP_DIM * (i + 1)]
    return out


def _pool_fwd_group(u_ref, pw_ref, sc_ref, pooled_ref, y_ref, g):
    S = u_ref.shape[0]
    R = min(POOL_CHUNK, S)
    w = jnp.left_shift(jnp.int32(2), g)
    d = lax.broadcasted_iota(jnp.int32, (R, R), 0) - lax.broadcasted_iota(jnp.int32, (R, R), 1)
    b_cur = jnp.where((d >= 0) & (d < w), 1.0, 0.0).astype(BF16)
    dp = lax.broadcasted_iota(jnp.int32, (R, HALO), 0) + HALO - lax.broadcasted_iota(jnp.int32, (R, HALO), 1)
    b_prev = jnp.where(dp < w, 1.0, 0.0).astype(BF16)
    pw = pw_ref[0].astype(BF16)
    sc = sc_ref[...]

    def chunk(r0, first):
        cur = u_ref[pl.ds(r0, R), :]
        wide = _dot(b_cur, jnp.concatenate(_split3(cur), axis=1))
        if not first:
            prev = u_ref[pl.ds(pl.multiple_of(r0 - HALO, HALO), HALO), :]
            wide = wide + _dot(b_prev, jnp.concatenate(_split3(prev), axis=1))
        count = jnp.minimum(r0 + lax.broadcasted_iota(jnp.int32, (R, 1), 0) + 1, w).astype(F32)
        pooled = (_lane_sum(wide, 3) / count - cur).astype(BF16)
        pooled_ref[pl.ds(r0, R), :] = pooled
        y_ref[pl.ds(r0, R), :] = _dot(pooled, pw) * sc

    chunk(0, True)

    def rest(c, carry):
        chunk(pl.multiple_of(c * R, R), False)
        return carry

    lax.fori_loop(1, S // R, rest, 0, unroll=POOL_UNROLL)


ATTN_WINDOW = 3
ATTN_TOGETHER = 2
ATTN_TOGETHER_FWD = 3


def _band_ones(T):
    row = lax.broadcasted_iota(jnp.int32, (T, T), 0)
    col = lax.broadcasted_iota(jnp.int32, (T, T), 1)
    after = jnp.where(row > col, 1.0, 0.0).astype(BF16)
    before = jnp.where(row < col, 1.0, 0.0).astype(BF16)
    return jnp.concatenate([after, after], axis=0), jnp.concatenate([before, before], axis=0), col < row


def _log1m(z, mask):
    sp = jnp.log(1.0 + jnp.exp(-jnp.abs(z)))
    l1m = -jnp.maximum(z, 0.0) - sp
    return (l1m if mask is None else jnp.where(mask, l1m, 0.0)), sp


def _log_gates_of(jobs, after2):
    zs = [_dot_nt(qh, kb) for qh, kb, _ in jobs]
    terms = []
    for z, (_, _, mask) in zip(zs, jobs):
        l1m, sp = _log1m(z, mask)
        terms.append(
            (jnp.minimum(z, 0.0) - sp, jnp.concatenate(_split2(l1m), axis=1), jnp.sum(l1m, axis=1, keepdims=True))
        )
    return [(ls, _dot(split, after2), total) for ls, split, total in terms]


def _log_gates(qhs, kbs, masks, after2):
    flat = _log_gates_of([(qh, kb, mask) for kb, mask in zip(kbs, masks) for qh in qhs], after2)
    return [flat[len(qhs) * b : len(qhs) * (b + 1)] for b in range(len(kbs))]


def _tile_start(index, T):
    return index * T if isinstance(index, int) else pl.multiple_of(index * T, T)


def _weights(ls, tail, carry, mask):
    a = jnp.exp(ls + tail + carry)
    return a if mask is None else jnp.where(mask, a, 0.0)


def _reaches(carry):
    return jnp.max(carry) > EXP_UNDERFLOW


def _mixers_fwd(q, k, v, u, pool_w, scale, shards):
    S = q.shape[0]
    T = ATTN_TILE
    nq = S // T
    n = len(shards)
    steps = D_ATTN // LANES

    def body(q_ref, k_ref, v_ref, u_ref, pw_ref, sc_ref, *rest):
        o_ref, pooled_ref, yp_ref = rest[n : n + 3]
        step = pl.program_id(0)

        def gather():
            return _two_level_gather(rest[:n], rest[n + 3 : 2 * n + 3], rest[2 * n + 3 :])

        @pl.when(step == 0)
        def _():
            gather()[0]()

        @pl.when(step == steps // 2)
        def _():
            gather()[1]()

        _pool_fwd_group(u_ref, pw_ref, sc_ref, pooled_ref, yp_ref, step)

        lane = lax.broadcasted_iota(jnp.int32, (T, LANES), 1)
        heads = (lane < HEAD_DIM, lane >= HEAD_DIM)
        after2, _, causal = _band_ones(T)
        zero = jnp.zeros((T, 1), F32)

        def key_tile(k0):
            vb = v_ref[pl.ds(k0, T), :]
            return k_ref[pl.ds(k0, T), :], jnp.concatenate([jnp.where(m, vb, jnp.zeros_like(vb)) for m in heads], axis=0)

        def q_tiles(qis, nb):
            queries, jobs = [], []
            for qi in qis:
                q0 = _tile_start(qi, T)
                qb = q_ref[pl.ds(q0, T), :]
                qhs = [jnp.where(m, qb, jnp.zeros_like(qb)) for m in heads]
                tiles = [key_tile(_tile_start(qi - b, T)) for b in range(nb)]
                queries.append((q0, qhs, [v_rows for _, v_rows in tiles]))
                jobs += [(qh, kb, causal if b == 0 else None) for b, (kb, _) in enumerate(tiles) for qh in qhs]
            gates = _log_gates_of(jobs, after2)
            states = []
            for i, (q0, qhs, values) in enumerate(queries):
                carries, probs = [zero, zero], []
                for b in range(nb):
                    for h in range(2):
                        ls, tail, total = gates[2 * nb * i + 2 * b + h]
                        probs.append(_weights(ls, tail, carries[h], causal if b == 0 else None).astype(BF16))
                        carries[h] = carries[h] + total
                states.append((carries, _dot(jnp.concatenate(probs, axis=1), jnp.concatenate(values, axis=0))))

            def more(st):
                return (st[0] > 0) & _reaches(jnp.maximum(st[1], st[2]))

            for qi, (q0, qhs, _), (carries, acc) in zip(qis, queries, states):

                def k_step(st, qhs=qhs):
                    kj = st[0] - 1
                    kb, v_rows = key_tile(pl.multiple_of(kj * T, T))
                    new, probs = [], []
                    for (ls, tail, total), carry in zip(_log_gates(qhs, [kb], [None], after2)[0], st[1:3]):
                        probs.append(_weights(ls, tail, carry, None).astype(BF16))
                        new.append(carry + total)
                    return kj, new[0], new[1], st[3] + _dot(jnp.concatenate(probs, axis=1), v_rows)

                if not isinstance(qi, int):
                    acc = lax.while_loop(more, k_step, (qi - (nb - 1), carries[0], carries[1], acc))[3]
                o_ref[pl.ds(q0, T), :] = acc

        head_tiles = min(ATTN_WINDOW - 1, nq)
        for qi in range(head_tiles):
            q_tiles([qi], qi + 1)
        assert (nq - head_tiles) % ATTN_TOGETHER_FWD == 0

        def q_loop(i, carry):
            first = head_tiles + ATTN_TOGETHER_FWD * i
            q_tiles([first + j for j in range(ATTN_TOGETHER_FWD)], ATTN_WINDOW)
            return carry

        lax.fori_loop(0, (nq - head_tiles) // ATTN_TOGETHER_FWD, q_loop, 0)

        @pl.when(step == steps - 1)
        def _():
            _, _, forward, finish = gather()
            forward()
            finish()

    blk = pl.BlockSpec((S, LANES), lambda p: (0, p))
    return pl.pallas_call(
        body,
        name="mixers_fwd",
        grid=(steps,),
        out_shape=[
            jax.ShapeDtypeStruct((S, D_ATTN), F32),
            jax.ShapeDtypeStruct((S, D_POOL), BF16),
            jax.ShapeDtypeStruct((S, D_POOL), F32),
        ]
        + _exchange_shapes(shards, True),
        in_specs=[blk, blk, blk, blk, _POOL_W_SPEC, _POOL_SCALE_SPEC] + [ANY_SPEC] * n,
        out_specs=[blk, blk, blk] + [ANY_SPEC] * n,
        scratch_shapes=_exchange_sems(n),
        compiler_params=pltpu.CompilerParams(
            dimension_semantics=("arbitrary",), vmem_limit_bytes=VMEM_LIMIT, has_side_effects=True
        ),
    )(q, k, v, u, pool_w, scale, *shards)


MLP_AHEAD = 1
SG_ROWS = 8


def _mlp_fwd_bwd(x, y_pool, y_attn, target, g_pool, g_attn, g2, gf, w_out, w_up, w_down):
    S = x.shape[0]
    tm = min(256, S)
    fc = D_FF // N_DEV

    def body(x_ref, yp_ref, ya_ref, t_ref, gp_ref, ga_ref, g2_ref, gf_ref, wo_hbm, wu_hbm, wd_hbm,
             mixed_ref, hn2_ref, act_ref, dup_ref, dh2_ref, dh1_ref, dh1b_ref, dyp_ref, dya_ref, sg_ref,
             wo, wu, wd, up_s, sems):
        @pl.when(pl.program_id(0) == 0)
        def _():
            copies = [
                pltpu.make_async_copy(wo_hbm, wo, sems.at[0]),
                pltpu.make_async_copy(wu_hbm, wu, sems.at[1]),
                pltpu.make_async_copy(wd_hbm, wd, sems.at[2]),
            ]
            for cp in copies:
                cp.start()
            for cp in copies:
                cp.wait()
            sg_ref[...] = jnp.zeros_like(sg_ref)

        gp, ga, g2v, gfv = gp_ref[...], ga_ref[...], g2_ref[...], gf_ref[...]
        yp, ya = yp_ref[...], ya_ref[...]
        rp, ra = _rstd(yp), _rstd(ya)
        yph, yah = yp * rp, ya * ra
        mixed = jnp.concatenate([(yph * gp).astype(BF16), (yah * ga).astype(BF16)], axis=1)
        mixed_ref[...] = mixed
        h1 = x_ref[...] + _dot(mixed, wo[...])
        r2 = _rstd(h1)
        h1h = h1 * r2
        hn2 = (h1h * g2v).astype(BF16)
        hn2_ref[...] = hn2
        h2 = h1
        ups = [_dot(hn2, wu[j]) for j in range(MLP_AHEAD)]
        for j in range(N_DEV):
            cols = slice(fc * j, fc * (j + 1))
            if j + MLP_AHEAD < N_DEV:
                ups.append(_dot(hn2, wu[j + MLP_AHEAD]))
            up = ups.pop(0)
            up_s[:, cols] = up
            act = jnp.square(jnp.maximum(up, 0.0)).astype(BF16)
            act_ref[:, cols] = act
            h2 = h2 + _dot(act, wd[cols, :])
        rf = _rstd(h2)
        h2h = h2 * rf
        diff = h2h * gfv - t_ref[...]
        loss_rows = 0.5 * jnp.mean(diff * diff, axis=-1, keepdims=True)
        dy = diff * (1.0 / D_MODEL)
        dh2, dgf = _rms_bwd(dy, h2h, rf, gfv)
        dh2b = dh2.astype(BF16)
        dh2_ref[...] = dh2b
        dhn2 = jnp.zeros((tm, D_MODEL), F32)
        dacts = [_dot_nt(dh2b, wd[fc * j : fc * (j + 1), :]) for j in range(MLP_AHEAD)]
        for j in range(N_DEV):
            cols = slice(fc * j, fc * (j + 1))
            if j + MLP_AHEAD < N_DEV:
                dacts.append(_dot_nt(dh2b, wd[fc * (j + MLP_AHEAD) : fc * (j + MLP_AHEAD + 1), :]))
            dup = (dacts.pop(0) * (2.0 * jnp.maximum(up_s[:, cols], 0.0))).astype(BF16)
            dup_ref[:, cols] = dup
            dhn2 = dhn2 + _dot_nt(dup, wu[j])
        dh1n, dg2 = _rms_bwd(dhn2, h1h, r2, g2v)
        dh1 = dh2 + dh1n
        dh1_ref[...] = dh1
        dh1b = dh1.astype(BF16)
        dh1b_ref[...] = dh1b
        dmix = _dot_nt(dh1b, wo[...])
        dyp, dgp = _rms_bwd(dmix[:, :D_POOL], yph, rp, gp)
        dya, dga = _rms_bwd(dmix[:, D_POOL:], yah, ra, ga)
        dyp_ref[...] = dyp
        dya_ref[...] = dya
        sg_ref[0:1, :] += dgf
        sg_ref[1:2, :] += dg2
        sg_ref[2:3, :] += jnp.concatenate([dgp, dga], axis=1)
        sg_ref[3:4, :] += jnp.broadcast_to(jnp.sum(loss_rows, axis=0, keepdims=True), (1, D_MODEL))

    def tok(n):
        return pl.BlockSpec((tm, n), lambda i: (i, 0))

    def vec(n):
        return pl.BlockSpec((1, n), lambda i: (0, 0))

    any_spec = pl.BlockSpec(memory_space=pl.ANY)
    return pl.pallas_call(
        body,
        name="mlp_fwd_bwd",
        grid=(S // tm,),
        out_shape=[
            jax.ShapeDtypeStruct((S, D_MODEL), BF16),
            jax.ShapeDtypeStruct((S, D_MODEL), BF16),
            jax.ShapeDtypeStruct((S, D_FF), BF16),
            jax.ShapeDtypeStruct((S, D_FF), BF16),
            jax.ShapeDtypeStruct((S, D_MODEL), BF16),
            jax.ShapeDtypeStruct((S, D_MODEL), F32),
            jax.ShapeDtypeStruct((S, D_MODEL), BF16),
            jax.ShapeDtypeStruct((S, D_POOL), F32),
            jax.ShapeDtypeStruct((S, D_ATTN), F32),
            jax.ShapeDtypeStruct((SG_ROWS, D_MODEL), F32),
        ],
        in_specs=[tok(D_MODEL), tok(D_POOL), tok(D_ATTN), tok(D_MODEL), vec(D_POOL), vec(D_ATTN),
                  vec(D_MODEL), vec(D_MODEL), any_spec, any_spec, any_spec],
        out_specs=[tok(D_MODEL), tok(D_MODEL), tok(D_FF), tok(D_FF), tok(D_MODEL), tok(D_MODEL),
                   tok(D_MODEL), tok(D_POOL), tok(D_ATTN),
                   pl.BlockSpec((SG_ROWS, D_MODEL), lambda i: (0, 0))],
        scratch_shapes=[
            pltpu.VMEM((D_MODEL, D_MODEL), BF16),
            pltpu.VMEM((N_DEV, D_MODEL, fc), BF16),
            pltpu.VMEM((D_FF, D_MODEL), BF16),
            pltpu.VMEM((tm, D_FF), F32),
            pltpu.SemaphoreType.DMA((3,)),
        ],
        compiler_params=_params(("arbitrary",)),
    )(x, y_pool, y_attn, target, g_pool, g_attn, g2, gf, w_out, w_up, w_down)


def _wgrad(a, b, block_a, groups, name, travelling=()):
    n = len(travelling)
    S, ka = a.shape
    nb = b.shape[1]
    ts = min(1024, S)
    per = N_DEV // groups
    if block_a:
        ka //= groups
        blk = (ka // per, nb)
        a_spec = pl.BlockSpec((ts, ka), lambda g, s: (s, g))
        b_spec = pl.BlockSpec((ts, nb), lambda g, s: (s, 0))
    else:
        nb //= groups
        blk = (ka, nb // per)
        a_spec = pl.BlockSpec((ts, ka), lambda g, s: (s, 0))
        b_spec = pl.BlockSpec((ts, nb), lambda g, s: (s, g))
    steps = S // ts

    def body(a_ref, b_ref, *rest):
        o_ref, acc = rest[n], rest[2 * n + 1]
        g, s = pl.program_id(0), pl.program_id(1)

        def scatter():
            return _direct_exchange(rest[:n], rest[n + 1 : 2 * n + 1], rest[2 * n + 2 :], False)

        if n:
            @pl.when((g == 0) & (s == 0))
            def _():
                scatter()[0]()

        @pl.when(s == 0)
        def _():
            acc[...] = jnp.zeros_like(acc)

        acc[...] += _dot_tn(a_ref[...], b_ref[...])

        @pl.when(s == steps - 1)
        def _():
            for j in range(per):
                if block_a:
                    o_ref[j] = acc[blk[0] * j : blk[0] * (j + 1), :].astype(BF16)
                else:
                    o_ref[j] = acc[:, blk[1] * j : blk[1] * (j + 1)].astype(BF16)

        if n:
            @pl.when((g == groups - 1) & (s == steps - 1))
            def _():
                scatter()[1]()

    results = pl.pallas_call(
        body,
        name=name,
        grid=(groups, steps),
        out_shape=[jax.ShapeDtypeStruct((N_DEV,) + blk, BF16)] + _exchange_shapes(travelling, False),
        in_specs=[a_spec, b_spec] + [ANY_SPEC] * n,
        out_specs=[pl.BlockSpec((per,) + blk, lambda g, s: (g, 0, 0))] + [ANY_SPEC] * n,
        scratch_shapes=[pltpu.VMEM((ka, nb), F32)] + (_exchange_sems(n) if n else []),
        compiler_params=pltpu.CompilerParams(
            dimension_semantics=("arbitrary", "arbitrary"), vmem_limit_bytes=VMEM_LIMIT, has_side_effects=bool(n)
        ),
    )(a, b, *travelling)
    return results if n else results[0]


def _mixers_bwd(q, k, v, do, dyp, pooled, pool_w, scale, partials):
    S = q.shape[0]
    T = ATTN_TILE
    nq = S // T
    n = len(partials)
    steps = D_ATTN // LANES

    def body(q_ref, k_ref, v_ref, do_ref, dyp_ref, pooled_ref, pw_ref, sc_ref, *rest):
        dq_ref, dk_ref, dv_ref, du_ref, dsc_ref, dpw_ref = rest[n : n + 6]
        dk_acc, dv_acc, carry_s = rest[2 * n + 6 : 2 * n + 9]
        step = pl.program_id(0)

        def scatter():
            return _direct_exchange(rest[:n], rest[n + 6 : 2 * n + 6], rest[2 * n + 9 :], False)

        @pl.when(step == 0)
        def _():
            scatter()[0]()

        _pool_bwd_group(dyp_ref, pooled_ref, pw_ref, sc_ref, du_ref, dsc_ref, dpw_ref, step)

        dk_acc[...] = jnp.zeros_like(dk_acc)
        dv_acc[...] = jnp.zeros_like(dv_acc)
        lane = lax.broadcasted_iota(jnp.int32, (T, LANES), 1)
        heads = (lane < HEAD_DIM, lane >= HEAD_DIM)
        after2, before2, causal = _band_ones(T)
        zero = jnp.zeros((T, 1), F32)

        def key_tile(k0):
            kb = k_ref[pl.ds(k0, T), :]
            k_rows = jnp.concatenate([jnp.where(m, kb, jnp.zeros_like(kb)) for m in heads], axis=0)
            return kb, v_ref[pl.ds(k0, T), :], k_rows

        def grads(items):
            das = [[[_dot_nt(doh, vb) for doh in qs[1]] for _, vb, _ in tiles] for qs, _, tiles, _, _, _, _ in items]
            probs, gs = [], []
            for (_, _, tiles, masks, carry_in, gates, _), da in zip(items, das):
                probs.append([[_weights(gates[b][h][0], gates[b][h][1], carry_in[b][h], masks[b]) for h in range(2)]
                              for b in range(len(tiles))])
                gs.append([[probs[-1][b][h] * da[b][h] for h in range(2)] for b in range(len(tiles))])
            before = [[[_dot(jnp.concatenate(_split2(g), axis=1), before2) for g in row] for row in item] for item in gs]
            dzs, lefts = [], []
            for i, (_, _, tiles, masks, _, gates, st) in enumerate(items):
                g_left, item_dz = [st[0], st[1]], [None] * len(tiles)
                for b in reversed(range(len(tiles))):
                    item_dz[b] = []
                    for h in range(2):
                        sig = jnp.exp(gates[b][h][0])
                        dz = gs[i][b][h] * (1.0 - sig) - sig * (g_left[h] + before[i][b][h])
                        if masks[b] is not None:
                            dz = jnp.where(masks[b], dz, 0.0)
                        item_dz[b].append(dz.astype(BF16))
                        g_left[h] = g_left[h] + jnp.sum(gs[i][b][h], axis=1, keepdims=True)
                dzs.append(item_dz)
                lefts.append(g_left)
            out = []
            for (_, _, tiles, _, _, _, st), item_dz, g_left in zip(items, dzs, lefts):
                dq = st[2] + _dot(
                    jnp.concatenate([dz for row in item_dz for dz in row], axis=1),
                    jnp.concatenate([k_rows for _, _, k_rows in tiles], axis=0),
                )
                out.append((g_left[0], g_left[1], dq))
            for (qs, starts, tiles, _, _, _, _), item_dz, item_probs in zip(items, dzs, probs):
                for b in range(len(tiles)):
                    dk_acc[pl.ds(starts[b], T), :] += _dot_tn(jnp.concatenate(item_dz[b], axis=0), qs[2])
                    dv_acc[pl.ds(starts[b], T), :] += _dot_tn(
                        jnp.concatenate([a.astype(BF16) for a in item_probs[b]], axis=0), qs[3]
                    )
            return out

        def more(st):
            return (st[0] > 0) & _reaches(jnp.maximum(st[1], st[2]))

        def q_tiles(qis, nb):
            masks = [causal] + [None] * (nb - 1)
            prepared, jobs = [], []
            for qi in qis:
                q0 = _tile_start(qi, T)
                qb = q_ref[pl.ds(q0, T), :]
                dob = do_ref[pl.ds(q0, T), :]
                qhs = [jnp.where(m, qb, jnp.zeros_like(qb)) for m in heads]
                dohs = [jnp.where(m, dob, 0.0).astype(BF16) for m in heads]
                queries = (qhs, dohs, jnp.concatenate(qhs, axis=0), jnp.concatenate(dohs, axis=0))
                starts = [_tile_start(qi - b, T) for b in range(nb)]
                tiles = [key_tile(k0) for k0 in starts]
                prepared.append((q0, queries, starts, tiles))
                jobs += [(qh, kb, mask) for (kb, _, _), mask in zip(tiles, masks) for qh in qhs]
            flat = _log_gates_of(jobs, after2)
            items = []
            for i, (qi, (q0, queries, starts, tiles)) in enumerate(zip(qis, prepared)):
                gates = [flat[2 * nb * i + 2 * b : 2 * nb * i + 2 * b + 2] for b in range(nb)]
                carries, carry_in = [zero, zero], []
                for b in range(nb):
                    carry_in.append(list(carries))
                    carries = [carries[h] + gates[b][h][2] for h in range(2)]
                st = (zero, zero, jnp.zeros((T, LANES), F32))
                if not isinstance(qi, int):
                    k_left = qi - (nb - 1)
                    qhs = queries[0]

                    def right_to_left(st, qhs=qhs):
                        kj = st[0] - 1
                        carry_s[0, kj] = st[1]
                        carry_s[1, kj] = st[2]
                        kb = k_ref[pl.ds(pl.multiple_of(kj * T, T), T), :]
                        sums = [jnp.sum(_log1m(_dot_nt(qh, kb), None)[0], axis=1, keepdims=True) for qh in qhs]
                        return kj, st[1] + sums[0], st[2] + sums[1]

                    k_first = lax.while_loop(more, right_to_left, (k_left, carries[0], carries[1]))[0]

                    def left_to_right(kj, st, queries=queries):
                        k0 = pl.multiple_of(kj * T, T)
                        tile = key_tile(k0)
                        left_gates = _log_gates(queries[0], [tile[0]], [None], after2)
                        carry = [[carry_s[0, kj], carry_s[1, kj]]]
                        return grads([(queries, [k0], [tile], [None], carry, left_gates, st)])[0]

                    st = lax.fori_loop(k_first, k_left, left_to_right, st)
                items.append((queries, starts, tiles, masks, carry_in, gates, st))
            for (q0, _, _, _), st in zip(prepared, grads(items)):
                dq_ref[pl.ds(q0, T), :] = (st[2] * (HEAD_DIM**-0.5)).astype(BF16)

        head_tiles = min(ATTN_WINDOW - 1, nq)
        for qi in range(head_tiles):
            q_tiles([qi], qi + 1)
        assert (nq - head_tiles) % ATTN_TOGETHER == 0

        def q_loop(i, carry):
            first = head_tiles + ATTN_TOGETHER * i
            q_tiles([first + j for j in range(ATTN_TOGETHER)], ATTN_WINDOW)
            return carry

        lax.fori_loop(0, (nq - head_tiles) // ATTN_TOGETHER, q_loop, 0)
        dk_ref[...] = dk_acc[...].astype(BF16)
        dv_ref[...] = dv_acc[...].astype(BF16)

        @pl.when(step == steps - 1)
        def _():
            scatter()[1]()

    blk = pl.BlockSpec((S, LANES), lambda p: (0, p))
    sds = jax.ShapeDtypeStruct((S, D_ATTN), BF16)
    return pl.pallas_call(
        body,
        name="mixers_bwd",
        grid=(steps,),
        out_shape=[
            sds,
            sds,
            sds,
            jax.ShapeDtypeStruct((S, D_POOL), BF16),
            jax.ShapeDtypeStruct((8, D_POOL), F32),
            jax.ShapeDtypeStruct((N_GROUPS, GROUP_DIM, GROUP_DIM), BF16),
        ]
        + _exchange_shapes(partials, False),
        in_specs=[blk] * 6 + [_POOL_W_SPEC, _POOL_SCALE_SPEC] + [ANY_SPEC] * n,
        out_specs=[blk] * 4 + [pl.BlockSpec((8, GROUP_DIM), lambda p: (0, p)), _POOL_W_SPEC] + [ANY_SPEC] * n,
        scratch_shapes=[pltpu.VMEM((S, LANES), F32), pltpu.VMEM((S, LANES), F32), pltpu.VMEM((2, nq, T, 1), F32)]
        + _exchange_sems(n),
        compiler_params=pltpu.CompilerParams(
            dimension_semantics=("arbitrary",), vmem_limit_bytes=VMEM_LIMIT, has_side_effects=True
        ),
    )(q, k, v, do, dyp, pooled, pool_w, scale, *partials)


def _pool_bwd_group(dy_ref, pooled_ref, pw_ref, sc_ref, du_ref, dsc_ref, dpw_ref, g):
    S = dy_ref.shape[0]
    R = min(POOL_CHUNK, S)
    nc = S // R
    w = jnp.left_shift(jnp.int32(2), g)
    pw = pw_ref[0].astype(BF16)
    sc = sc_ref[...]
    d = lax.broadcasted_iota(jnp.int32, (R, R), 1) - lax.broadcasted_iota(jnp.int32, (R, R), 0)
    bt_cur = jnp.where((d >= 0) & (d < w), 1.0, 0.0).astype(BF16)
    dn = lax.broadcasted_iota(jnp.int32, (R, HALO), 1) + R - lax.broadcasted_iota(jnp.int32, (R, HALO), 0)
    bt_next = jnp.where(dn < w, 1.0, 0.0).astype(BF16)

    def per_count(dpl, r0):
        n = dpl.shape[0]
        count = jnp.minimum(r0 + lax.broadcasted_iota(jnp.int32, (n, 1), 0) + 1, w).astype(F32)
        return jnp.concatenate(_split2(dpl / count), axis=1)

    def chunk(r0, last, sums):
        dyv = dy_ref[pl.ds(r0, R), :]
        pooled = pooled_ref[pl.ds(r0, R), :]
        dmapped = (dyv * sc).astype(BF16)
        dsc = sums[0] + jnp.sum(dyv * _dot(pooled, pw), axis=0, keepdims=True)
        dpw = sums[1] + _dot_tn(pooled, dmapped)
        dpl = _dot_nt(dmapped, pw)
        wide = _dot(bt_cur, per_count(dpl, r0))
        if not last:
            nxt = pl.multiple_of(r0 + R, R)
            dpl_next = _dot_nt((dy_ref[pl.ds(nxt, HALO), :] * sc).astype(BF16), pw)
            wide = wide + _dot(bt_next, per_count(dpl_next, nxt))
        du_ref[pl.ds(r0, R), :] = (_lane_sum(wide, 2) - dpl).astype(BF16)
        return dsc, dpw

    sums = (jnp.zeros((1, GROUP_DIM), F32), jnp.zeros((GROUP_DIM, GROUP_DIM), F32))
    sums = lax.fori_loop(
        0, nc - 1, lambda c, s: chunk(pl.multiple_of(c * R, R), False, s), sums, unroll=POOL_UNROLL
    )
    dsc, dpw = chunk((nc - 1) * R, True, sums)
    dsc_ref[...] = jnp.zeros_like(dsc_ref)
    dsc_ref[0:1, :] = dsc
    dpw_ref[0] = dpw.astype(dpw_ref.dtype)


def _bwd_in(du, dq, dk, dv, w_in, x, dh1, g1):
    S = x.shape[0]
    tm = min(512, S)

    def body(du_ref, dq_ref, dk_ref, dv_ref, w_ref, x_ref, dh1_ref, g_ref, dx_ref, dproj_ref, dg_ref):
        @pl.when(pl.program_id(0) == 0)
        def _():
            dg_ref[...] = jnp.zeros_like(dg_ref)

        parts = (du_ref[...], dq_ref[...], dk_ref[...], dv_ref[...])
        dhn = jnp.zeros((tm, D_MODEL), F32)
        for j in range(N_DEV):
            piece = parts[j // 2][:, 256 * (j % 2) : 256 * (j % 2 + 1)]
            dproj_ref[:, 256 * j : 256 * (j + 1)] = piece
            dhn = dhn + _dot_nt(piece, w_ref[j])
        xv = x_ref[...]
        r = _rstd(xv)
        dxn, dg = _rms_bwd(dhn, xv * r, r, g_ref[...])
        dx_ref[...] = dh1_ref[...] + dxn
        dg_ref[0:1, :] += dg

    half = pl.BlockSpec((tm, D_POOL), lambda i: (i, 0))
    full = pl.BlockSpec((tm, D_MODEL), lambda i: (i, 0))
    return pl.pallas_call(
        body,
        name="bwd_in",
        grid=(S // tm,),
        out_shape=[
            jax.ShapeDtypeStruct((S, D_MODEL), F32),
            jax.ShapeDtypeStruct((S, D_IN_PROJ), BF16),
            jax.ShapeDtypeStruct((8, D_MODEL), F32),
        ],
        in_specs=[half, half, half, half,
                  pl.BlockSpec((N_DEV, D_MODEL, 256), lambda i: (0, 0, 0)),
                  full, full, pl.BlockSpec((1, D_MODEL), lambda i: (0, 0))],
        out_specs=[full, pl.BlockSpec((tm, D_IN_PROJ), lambda i: (i, 0)),
                   pl.BlockSpec((8, D_MODEL), lambda i: (0, 0))],
        compiler_params=_params(("arbitrary",)),
    )(du, dq, dk, dv, w_in, x, dh1, g1)


def _rows(a):
    a = a.reshape(-1, LANES)
    pad = (-a.shape[0]) % 8
    return jnp.pad(a, ((0, pad), (0, 0))) if pad else a


def kernel(x, norm1_g, w_in, pool_w, pool_scale, pool_out_g, attn_out_g, w_out, norm2_g, w_up, w_down, final_g, loss_target, m_norm1_g, m_w_in, m_pool_w, m_pool_scale, m_pool_out_g, m_attn_out_g, m_w_out, m_norm2_g, m_w_up, m_w_down, m_final_g, v_norm1_g, v_w_in, v_pool_w, v_pool_scale, v_pool_out_g, v_attn_out_g, v_w_out, v_norm2_g, v_w_up, v_w_down, v_final_g):
    S = x.shape[1]
    xs = x.reshape(S, D_MODEL)
    tgt = loss_target.reshape(S, D_MODEL)
    row = lambda a: a.reshape(1, -1)

    (w_in_g,) = _exchange([w_in.astype(BF16)], True, "gather_w_in")
    lanes = lambda a: a.reshape(-1, LANES)
    vector_sets = [
        [lanes(a) for a in (final_g, norm2_g, pool_out_g, attn_out_g, norm1_g, pool_scale)],
        [lanes(a) for a in (m_final_g, m_norm2_g, m_pool_out_g, m_attn_out_g, m_norm1_g, m_pool_scale)],
        [lanes(a) for a in (v_final_g, v_norm2_g, v_pool_out_g, v_attn_out_g, v_norm1_g, v_pool_scale)],
    ]
    hn, u_pool, q, k, v, *prepared = _fwd_in(xs, row(norm1_g), w_in_g, [w_out, w_up, w_down], vector_sets)
    shards, packed_vectors = prepared[:3], prepared[3:]
    y_attn, pooled, y_pool, w_out_g, w_up_g, w_down_g = _mixers_fwd(q, k, v, u_pool, pool_w, row(pool_scale), shards)
    w_out_full = w_out_g.reshape(D_MODEL, D_MODEL)
    w_down_full = w_down_g.reshape(D_FF, D_MODEL)
    mixed, hn2, act, dup, dh2b, dh1, dh1b, dyp, dya, sg = _mlp_fwd_bwd(
        xs, y_pool, y_attn, tgt, row(pool_out_g), row(attn_out_g), row(norm2_g), row(final_g),
        w_out_full, w_up_g, w_down_full,
    )
    gp_down = _wgrad(act, dh2b, True, 2, "wgrad_down")
    gp_up = _wgrad(hn2, dup, False, 2, "wgrad_up")
    gp_out = _wgrad(mixed, dh1b, True, 1, "wgrad_out")
    dq, dk, dv, du, dsc, dpw, land_up, land_down = _mixers_bwd(
        q, k, v, dya, dyp, pooled, pool_w, row(pool_scale), [gp_up, gp_down]
    )
    dx, dproj, dg1 = _bwd_in(du, dq, dk, dv, w_in_g, xs, dh1, row(norm1_g))
    gp_in, land_out = _wgrad(hn, dproj, False, 1, "wgrad_in", [gp_out])

    partial_vectors = jnp.concatenate([_rows(p) for p in (sg[0], sg[1], sg[2], dg1[0], dsc[0], sg[3])], axis=0)
    flat_pool = lambda a: a.reshape(N_GROUPS * GROUP_DIM, GROUP_DIM)
    smalls = [
        (partial_vectors, [(*packed_vectors, None)]),
        (flat_pool(dpw), [(flat_pool(pool_w), flat_pool(m_pool_w), flat_pool(v_pool_w), None)]),
    ]
    tail = _reduce_adam_tail(
        [land_out, land_up, land_down], [w_out, w_up, w_down], [m_w_out, m_w_up, m_w_down],
        [v_w_out, v_w_up, v_w_down], gp_in, (w_in, m_w_in, v_w_in), smalls,
    )
    big = {name: tail[4 * t : 4 * t + 4] for t, name in enumerate(("w_out", "w_up", "w_down", "w_in"))}

    def unpack(vec, pw):
        out = {}
        for i, name in enumerate(("final_g", "norm2_g", "mix_g", "norm1_g", "pool_scale")):
            out[name] = vec[8 * i : 8 * i + 8].reshape(-1)
        out["pool_scale"] = out["pool_scale"][:D_POOL]
        out["pool_out_g"], out["attn_out_g"] = out["mix_g"][:D_POOL], out["mix_g"][D_POOL:]
        out["pool_w"] = pw.reshape(N_GROUPS, GROUP_DIM, GROUP_DIM)
        return out, vec[40, 0]

    small_out = [unpack(tail[16 + i], tail[20 + i]) for i in range(4)]
    loss = small_out[0][1]
    order = ("norm1_g", "w_in", "pool_w", "pool_scale", "pool_out_g", "attn_out_g", "w_out", "norm2_g", "w_up",
             "w_down", "final_g")
    outs = [loss, dx.reshape(1, S, D_MODEL)]
    for i in range(4):
        for name in order:
            outs.append(big[name][i] if name in big else small_out[i][0][name])
    return tuple(outs)
```

```python
import jax
import jax.numpy as jnp
from jax import lax
from jax.experimental import pallas as pl
from jax.experimental.pallas import tpu as pltpu

F32 = jnp.float32
BF16 = jnp.bfloat16
MESH = pl.DeviceIdType.MESH

N_DEV = 8
D_MODEL = 1024
D_POOL = 512
D_ATTN = 512
N_GROUPS = 4
GROUP_DIM = 128
HEAD_DIM = 64
D_FF = 4096
D_IN_PROJ = 2048
EPS = 1e-6
HALO = 16
ATTN_TILE = 128
LANES = 128
EXP_UNDERFLOW = -104.0

ADAM_LR = 0.001
ADAM_B1 = 0.9
ADAM_B2 = 0.999
ADAM_EPS = 1e-08
ADAM_WD = 0.01
ADAM_STEP = 10

VMEM_LIMIT = 56 * 1024 * 1024


def _params(semantics=None, vmem=VMEM_LIMIT):
    return pltpu.CompilerParams(dimension_semantics=semantics, vmem_limit_bytes=vmem)


def _dot(a, b):
    return jnp.dot(a, b, preferred_element_type=F32)


def _dot_nt(a, b):
    return lax.dot_general(a, b, (((1,), (1,)), ((), ())), preferred_element_type=F32)


def _dot_tn(a, b):
    return lax.dot_general(a, b, (((0,), (0,)), ((), ())), preferred_element_type=F32)


def _split2(x):
    hi = x.astype(BF16)
    lo = (x - hi.astype(F32)).astype(BF16)
    return hi, lo


def _split3(x):
    hi = x.astype(BF16)
    r = x - hi.astype(F32)
    mid = r.astype(BF16)
    lo = (r - mid.astype(F32)).astype(BF16)
    return hi, mid, lo


def _rstd(h):
    return lax.rsqrt(jnp.mean(h * h, axis=-1, keepdims=True) + EPS)


def _rms_bwd(dout, hhat, r, g):
    dg = jnp.sum(dout * hhat, axis=0, keepdims=True)
    dxh = dout * g
    dh = r * (dxh - hhat * jnp.mean(dxh * hhat, axis=-1, keepdims=True))
    return dh, dg


def _my_index():
    return 4 * lax.axis_index("x") + 2 * lax.axis_index("y") + lax.axis_index("c")


def _peer(k):
    x, y, c = lax.axis_index("x"), lax.axis_index("y"), lax.axis_index("c")
    px = 1 - x if (k >> 2) & 1 else x
    py = 1 - y if (k >> 1) & 1 else y
    pc = 1 - c if k & 1 else c
    return (px, py, pc), 4 * px + 2 * py + pc


N_PEERS = N_DEV - 1
ANY_SPEC = pl.BlockSpec(memory_space=pl.ANY)


def _exchange_sems(n):
    return [
        pltpu.SemaphoreType.DMA((n * N_PEERS,)),
        pltpu.SemaphoreType.DMA((n * N_PEERS,)),
        pltpu.SemaphoreType.DMA((n,)),
    ]


def _exchange_shapes(blocks, gather):
    if gather:
        return [jax.ShapeDtypeStruct((N_DEV,) + b.shape, b.dtype) for b in blocks]
    return [jax.ShapeDtypeStruct(b.shape, b.dtype) for b in blocks]


def _direct_exchange(ins, outs, sems, gather):
    send_sems, recv_sems, local_sems = sems
    me = _my_index()
    own, sends, recvs = [], [], []
    for t in range(len(ins)):
        own.append(pltpu.make_async_copy(ins[t] if gather else ins[t].at[me], outs[t].at[me], local_sems.at[t]))
        for k in range(1, N_DEV):
            peer, peer_idx = _peer(k)
            src = ins[t] if gather else ins[t].at[peer_idx]
            for dst, bucket in ((outs[t].at[me], sends), (outs[t].at[peer_idx], recvs)):
                bucket.append(
                    pltpu.make_async_remote_copy(
                        src_ref=src,
                        dst_ref=dst,
                        send_sem=send_sems.at[t * N_PEERS + k - 1],
                        recv_sem=recv_sems.at[t * N_PEERS + k - 1],
                        device_id=peer,
                        device_id_type=MESH,
                    )
                )

    def start():
        for cp in own + sends:
            cp.start()

    def finish():
        for cp in recvs:
            cp.wait_recv()
        for cp in sends:
            cp.wait_send()
        for cp in own:
            cp.wait()

    return start, finish


def _flip(a, bit):
    return a + bit - 2 * a * bit


def _two_level_gather(ins, outs, sems):
    send_sems, recv_sems, local_sems = sems
    x, y, c = lax.axis_index("x"), lax.axis_index("y"), lax.axis_index("c")
    me, sibling = (x, y, c), (x, y, 1 - c)
    x_nbr, y_nbr, diagonal = (1 - x, y, c), (x, 1 - y, c), (1 - x, 1 - y, c)
    relay_of = (_flip(x, 1 - c), _flip(y, c), c)
    relay_to = (_flip(x, c), _flip(y, 1 - c), c)

    def copy(t, k, block, to, from_input=False):
        slot = outs[t].at[4 * block[0] + 2 * block[1] + block[2]]
        return pltpu.make_async_remote_copy(
            src_ref=ins[t] if from_input else slot,
            dst_ref=slot,
            send_sem=send_sems.at[t * N_PEERS + k],
            recv_sem=recv_sems.at[t * N_PEERS + k],
            device_id=to,
            device_id_type=MESH,
        )

    arrays = range(len(ins))
    own = [pltpu.make_async_copy(ins[t], outs[t].at[4 * x + 2 * y + c], local_sems.at[t]) for t in arrays]
    first = [copy(t, k, me, to, True) for t in arrays for k, to in ((1, x_nbr), (2, y_nbr), (0, sibling))]
    landed = [copy(t, k, block, me) for t in arrays for k, block in ((1, x_nbr), (2, y_nbr))]
    relays = [copy(t, 3, relay_of, relay_to) for t in arrays]
    passed = [copy(t, 3 + k, block, sibling) for t in arrays for k, block in ((1, x_nbr), (2, y_nbr))]
    relayed = [(copy(t, 3, diagonal, me), copy(t, 6, diagonal, sibling)) for t in arrays]
    last = [copy(t, 0, sibling, me) for t in arrays]
    last += [copy(t, 3 + k, (*block[:2], 1 - c), me) for t in arrays for k, block in ((1, x_nbr), (2, y_nbr), (3, diagonal))]

    def start():
        for cp in own + first:
            cp.start()

    def relay():
        for cp in landed:
            cp.wait_recv()
        for cp in relays + passed:
            cp.start()

    def forward():
        for arrived, onward in relayed:
            arrived.wait_recv()
            onward.start()

    def finish():
        for cp in last:
            cp.wait_recv()
        for cp in first + relays + passed + [onward for _, onward in relayed]:
            cp.wait_send()
        for cp in own:
            cp.wait()

    return start, relay, forward, finish


N_CHIPS = 4
PAIR_SEMS = (N_CHIPS, N_CHIPS, N_CHIPS - 1, N_CHIPS - 1)


def _pair_reduce_scatter(gp_hbm, own, pair, summed, land, local_sem, sems):
    d2d_send, d2d_recv, ici_send, ici_recv = sems
    x, y, c = lax.axis_index("x"), lax.axis_index("y"), lax.axis_index("c")
    my_chip = 2 * x + y
    sibling = (x, y, 1 - c)
    chips = [(1 - x, y), (x, 1 - y), (1 - x, 1 - y)]
    local = [pltpu.make_async_copy(gp_hbm.at[2 * j + c], own.at[j], local_sem.at[0]) for j in range(N_CHIPS)]
    to_sibling = [
        pltpu.make_async_remote_copy(
            src_ref=gp_hbm.at[2 * j + 1 - c], dst_ref=pair.at[j], send_sem=d2d_send.at[j], recv_sem=d2d_recv.at[j],
            device_id=sibling, device_id_type=MESH,
        )
        for j in range(N_CHIPS)
    ]
    to_chips, from_chips = [], []
    for r, (px, py) in enumerate(chips):
        for dst, bucket in ((land.at[my_chip], to_chips), (land.at[2 * px + py], from_chips)):
            bucket.append(
                pltpu.make_async_remote_copy(
                    src_ref=summed.at[2 * px + py], dst_ref=dst, send_sem=ici_send.at[r], recv_sem=ici_recv.at[r],
                    device_id=(px, py, c), device_id_type=MESH,
                )
            )

    def start():
        for cp in local + to_sibling:
            cp.start()

    def middle():
        for cp in to_sibling:
            cp.wait_recv()
        pltpu.make_async_copy(gp_hbm.at[pl.ds(0, N_CHIPS)], own, local_sem.at[0]).wait()
        for j in range(N_CHIPS):
            summed[j] = (own[j].astype(F32) + pair[j].astype(F32)).astype(BF16)
        for cp in to_chips:
            cp.start()

    def finish():
        for cp in from_chips:
            cp.wait_recv()
        for cp in to_chips + to_sibling:
            cp.wait_send()
        land[my_chip] = summed[my_chip]

    return start, middle, finish


def _exchange(blocks, gather, name):
    n = len(blocks)

    def body(*refs):
        if gather:
            stages = _two_level_gather(refs[:n], refs[n : 2 * n], refs[2 * n :])
        else:
            stages = _direct_exchange(refs[:n], refs[n : 2 * n], refs[2 * n :], False)
        for stage in stages:
            stage()

    return pl.pallas_call(
        body,
        name=name,
        out_shape=_exchange_shapes(blocks, gather),
        in_specs=[ANY_SPEC] * n,
        out_specs=[ANY_SPEC] * n,
        scratch_shapes=_exchange_sems(n),
        compiler_params=pltpu.CompilerParams(has_side_effects=True),
    )(*blocks)


def _adam(w, g, m, v):
    m2 = ADAM_B1 * m + (1.0 - ADAM_B1) * g
    v2 = ADAM_B2 * v + (1.0 - ADAM_B2) * jnp.square(g)
    m_hat = m2 / (1.0 - ADAM_B1**ADAM_STEP)
    v_hat = v2 / (1.0 - ADAM_B2**ADAM_STEP)
    delta = -ADAM_LR * (m_hat / (jnp.sqrt(v_hat) + ADAM_EPS) + ADAM_WD * w)
    return delta, m2, v2


TAIL_STEPS = 4


def _reduce_adam_tail(lands, ws, ms, vs, gp_last, last, smalls):
    nw, ns = len(ws), len(smalls)
    tiles = [w.shape[0] // TAIL_STEPS for w in ws]
    blk = gp_last.shape[1:]
    entries = [(i, e) for i, (_, es) in enumerate(smalls) for e in es]
    n_params = 3 * sum(e[0] is not None for _, e in entries)
    n_small_out = sum(4 if e[0] is not None else 1 for _, e in entries)

    def body(*refs):
        gp_hbm = refs[0]
        part_hbm = refs[1 : 1 + ns]
        refs = refs[1 + ns :]
        land_refs = refs[:nw]
        w_refs, m_refs, v_refs = (refs[nw * (i + 1) : nw * (i + 2)] for i in range(3))
        last_refs = refs[4 * nw : 4 * nw + 3]
        refs = refs[4 * nw + 3 :]
        param_refs = refs[:n_params]
        outs = refs[n_params:]
        big_out = outs[: 4 * nw]
        last_out = outs[4 * nw : 4 * nw + 4]
        small_out = outs[4 * nw + 4 : 4 * nw + 4 + n_small_out]
        scratch = outs[4 * nw + 4 + n_small_out :]
        small_land = scratch[:ns]
        pair_bufs = scratch[ns : ns + 4]
        local_sem = scratch[ns + 4]
        pair_sems = scratch[ns + 5 : ns + 9]
        small_sems = scratch[ns + 9 :]
        step = pl.program_id(0)

        def pair():
            return _pair_reduce_scatter(gp_hbm, *pair_bufs, local_sem, pair_sems)

        def small():
            return _direct_exchange(part_hbm, small_land, small_sems, True)

        @pl.when(step == 0)
        def _():
            pair()[0]()
            small()[0]()

        @pl.when(step == 1)
        def _():
            pair()[1]()

        for t in range(nw):
            g = land_refs[t][0].astype(F32)
            for s in range(1, N_DEV):
                g = g + land_refs[t][s].astype(F32)
            for ref, val in zip(big_out[4 * t : 4 * t + 4], (g,) + _adam(w_refs[t][...], g, m_refs[t][...], v_refs[t][...])):
                ref[...] = val

        @pl.when(step == TAIL_STEPS - 1)
        def _():
            pair()[2]()
            small()[1]()
            land = pair_bufs[3]
            g = land[0].astype(F32)
            for chip in range(1, N_CHIPS):
                g = g + land[chip].astype(F32)
            for ref, val in zip(last_out, (g,) + _adam(last_refs[0][...], g, last_refs[1][...], last_refs[2][...])):
                ref[...] = val
            sums = []
            for i in range(ns):
                g = small_land[i][0].astype(F32)
                for s in range(1, N_DEV):
                    g = g + small_land[i][s].astype(F32)
                sums.append(g)
            params, results = list(param_refs), list(small_out)
            for i, (w, _, _, where) in entries:
                g = sums[i]
                if where is not None:
                    shape = w.shape if w is not None else where[2:]
                    g = g[where[0] : where[0] + shape[0], where[1] : where[1] + shape[1]]
                if w is None:
                    results.pop(0)[...] = g
                    continue
                w_ref, m_ref, v_ref = params[:3]
                del params[:3]
                for val in (g,) + _adam(w_ref[...], g, m_ref[...], v_ref[...]):
                    results.pop(0)[...] = val

    def tile(t):
        return pl.BlockSpec((tiles[t], ws[t].shape[1]), lambda i: (i, 0))

    def land_tile(t):
        return pl.BlockSpec((N_DEV, tiles[t], ws[t].shape[1]), lambda i: (0, i, 0))

    def whole(shape):
        return pl.BlockSpec(shape, lambda step: (0, 0))

    big_sds = [jax.ShapeDtypeStruct(ws[t].shape, F32) for t in range(nw) for _ in range(4)]
    params, small_shapes = [], []
    for _, (w, m, v, where) in entries:
        if w is None:
            small_shapes.append(tuple(where[2:]))
        else:
            params += [w, m, v]
            small_shapes += [w.shape] * 4
    return pl.pallas_call(
        body,
        name="reduce_adam_tail",
        grid=(TAIL_STEPS,),
        out_shape=big_sds
        + [jax.ShapeDtypeStruct(blk, F32)] * 4
        + [jax.ShapeDtypeStruct(shape, F32) for shape in small_shapes],
        in_specs=[ANY_SPEC] * (1 + ns)
        + [land_tile(t) for t in range(nw)]
        + [tile(t) for _ in range(3) for t in range(nw)]
        + [whole(blk)] * 3
        + [whole(p.shape) for p in params],
        out_specs=[tile(t) for t in range(nw) for _ in range(4)]
        + [whole(blk)] * 4
        + [whole(shape) for shape in small_shapes],
        scratch_shapes=[pltpu.VMEM((N_DEV,) + smalls[i][0].shape, smalls[i][0].dtype) for i in range(ns)]
        + [pltpu.VMEM((N_CHIPS,) + blk, BF16)] * 4
        + [pltpu.SemaphoreType.DMA((1,))]
        + [pltpu.SemaphoreType.DMA((count,)) for count in PAIR_SEMS]
        + _exchange_sems(ns),
        compiler_params=pltpu.CompilerParams(
            dimension_semantics=("arbitrary",), vmem_limit_bytes=VMEM_LIMIT, has_side_effects=True
        ),
    )(gp_last, *[s[0] for s in smalls], *lands, *ws, *ms, *vs, *last, *params)


VECTOR_ROWS = 48


def _fwd_in(x, g1, w_in, shards, vector_sets):
    S = x.shape[0]
    tm = min(512, S)
    steps = S // tm
    n = len(shards)
    nv = sum(len(vs) for vs in vector_sets)

    def body(x_ref, g_ref, w_ref, *rest):
        hn_ref, u_ref, q_ref, k_ref, v_ref = rest[n + nv : n + nv + 5]
        for src, dst in zip(rest[:n], rest[n + nv + 5 :]):
            dst[...] = src[...].astype(BF16)
        vectors = list(rest[n : n + nv])
        for vs, packed in zip(vector_sets, rest[2 * n + nv + 5 :]):
            at = 0
            for _ in vs:
                ref = vectors.pop(0)
                packed[at : at + ref.shape[0], :] = ref[...]
                at += ref.shape[0]
            packed[at:, :] = jnp.zeros((VECTOR_ROWS - at, LANES), F32)
        xv = x_ref[...]
        hn = (xv * _rstd(xv) * g_ref[...]).astype(BF16)
        hn_ref[...] = hn
        outs = (u_ref, q_ref, k_ref, v_ref)
        for j in range(N_DEV):
            p = _dot(hn, w_ref[j])
            cols = slice(256 * (j % 2), 256 * (j % 2 + 1))
            if j // 2 == 0:
                u_ref[:, cols] = p
            elif j // 2 == 1:
                q_ref[:, cols] = (p * (HEAD_DIM**-0.5)).astype(BF16)
            else:
                outs[j // 2][:, cols] = p.astype(BF16)

    half = pl.BlockSpec((tm, D_POOL), lambda i: (i, 0))
    shard_tiles = [pl.BlockSpec((s.shape[0] // steps, s.shape[1]), lambda i: (i, 0)) for s in shards]
    return pl.pallas_call(
        body,
        name="fwd_in",
        grid=(steps,),
        out_shape=[
            jax.ShapeDtypeStruct((S, D_MODEL), BF16),
            jax.ShapeDtypeStruct((S, D_POOL), F32),
            jax.ShapeDtypeStruct((S, D_ATTN), BF16),
            jax.ShapeDtypeStruct((S, D_ATTN), BF16),
            jax.ShapeDtypeStruct((S, D_ATTN), BF16),
        ]
        + [jax.ShapeDtypeStruct(s.shape, BF16) for s in shards]
        + [jax.ShapeDtypeStruct((VECTOR_ROWS, LANES), F32)] * len(vector_sets),
        in_specs=[
            pl.BlockSpec((tm, D_MODEL), lambda i: (i, 0)),
            pl.BlockSpec((1, D_MODEL), lambda i: (0, 0)),
            pl.BlockSpec((N_DEV, D_MODEL, 256), lambda i: (0, 0, 0)),
        ]
        + shard_tiles
        + [pl.BlockSpec(v.shape, lambda i: (0, 0)) for vs in vector_sets for v in vs],
        out_specs=[pl.BlockSpec((tm, D_MODEL), lambda i: (i, 0)), half, half, half, half]
        + shard_tiles
        + [pl.BlockSpec((VECTOR_ROWS, LANES), lambda i: (0, 0))] * len(vector_sets),
        compiler_params=_params(("arbitrary",)),
    )(x, g1, w_in, *shards, *[v for vs in vector_sets for v in vs])


POOL_CHUNK = 256
POOL_UNROLL = 5
_POOL_W_SPEC = pl.BlockSpec((1, GROUP_DIM, GROUP_DIM), lambda g: (g, 0, 0))
_POOL_SCALE_SPEC = pl.BlockSpec((1, GROUP_DIM), lambda g: (0, g))


def _lane_sum(wide, n):
    out = wide[:, :GROUP_DIM]
    for i in range(1, n):
        out = out + wide[:, GROUP_DIM * i : GROUP_DIM * (i + 1)]
    return out


def _pool_fwd_group(u_ref, pw_ref, sc_ref, pooled_ref, y_ref, g):
    S = u_ref.shape[0]
    R = min(POOL_CHUNK, S)
    w = jnp.left_shift(jnp.int32(2), g)
    d = lax.broadcasted_iota(jnp.int32, (R, R), 0) - lax.broadcasted_iota(jnp.int32, (R, R), 1)
    b_cur = jnp.where((d >= 0) & (d < w), 1.0, 0.0).astype(BF16)
    dp = lax.broadcasted_iota(jnp.int32, (R, HALO), 0) + HALO - lax.broadcasted_iota(jnp.int32, (R, HALO), 1)
    b_prev = jnp.where(dp < w, 1.0, 0.0).astype(BF16)
    pw = pw_ref[0].astype(BF16)
    sc = sc_ref[...]

    def chunk(r0, first):
        cur = u_ref[pl.ds(r0, R), :]
        wide = _dot(b_cur, jnp.concatenate(_split3(cur), axis=1))
        if not first:
            prev = u_ref[pl.ds(pl.multiple_of(r0 - HALO, HALO), HALO), :]
            wide = wide + _dot(b_prev, jnp.concatenate(_split3(prev), axis=1))
        count = jnp.minimum(r0 + lax.broadcasted_iota(jnp.int32, (R, 1), 0) + 1, w).astype(F32)
        pooled = (_lane_sum(wide, 3) / count - cur).astype(BF16)
        pooled_ref[pl.ds(r0, R), :] = pooled
        y_ref[pl.ds(r0, R), :] = _dot(pooled, pw) * sc

    chunk(0, True)

    def rest(c, carry):
        chunk(pl.multiple_of(c * R, R), False)
        return carry

    lax.fori_loop(1, S // R, rest, 0, unroll=POOL_UNROLL)


ATTN_WINDOW = 3
ATTN_TOGETHER = 2
ATTN_TOGETHER_FWD = 3


def _band_ones(T):
    row = lax.broadcasted_iota(jnp.int32, (T, T), 0)
    col = lax.broadcasted_iota(jnp.int32, (T, T), 1)
    after = jnp.where(row > col, 1.0, 0.0).astype(BF16)
    before = jnp.where(row < col, 1.0, 0.0).astype(BF16)
    return jnp.concatenate([after, after], axis=0), jnp.concatenate([before, before], axis=0), col < row


def _log1m(z, mask):
    sp = jnp.log(1.0 + jnp.exp(-jnp.abs(z)))
    l1m = -jnp.maximum(z, 0.0) - sp
    return (l1m if mask is None else jnp.where(mask, l1m, 0.0)), sp


def _log_gates_of(jobs, after2):
    zs = [_dot_nt(qh, kb) for qh, kb, _ in jobs]
    terms = []
    for z, (_, _, mask) in zip(zs, jobs):
        l1m, sp = _log1m(z, mask)
        terms.append(
            (jnp.minimum(z, 0.0) - sp, jnp.concatenate(_split2(l1m), axis=1), jnp.sum(l1m, axis=1, keepdims=True))
        )
    return [(ls, _dot(split, after2), total) for ls, split, total in terms]


def _log_gates(qhs, kbs, masks, after2):
    flat = _log_gates_of([(qh, kb, mask) for kb, mask in zip(kbs, masks) for qh in qhs], after2)
    return [flat[len(qhs) * b : len(qhs) * (b + 1)] for b in range(len(kbs))]


def _tile_start(index, T):
    return index * T if isinstance(index, int) else pl.multiple_of(index * T, T)


def _weights(ls, tail, carry, mask):
    a = jnp.exp(ls + tail + carry)
    return a if mask is None else jnp.where(mask, a, 0.0)


def _reaches(carry):
    return jnp.max(carry) > EXP_UNDERFLOW


def _mixers_fwd(q, k, v, u, pool_w, scale, shards):
    S = q.shape[0]
    T = ATTN_TILE
    nq = S // T
    n = len(shards)
    steps = D_ATTN // LANES

    def body(q_ref, k_ref, v_ref, u_ref, pw_ref, sc_ref, *rest):
        o_ref, pooled_ref, yp_ref = rest[n : n + 3]
        step = pl.program_id(0)

        def gather():
            return _two_level_gather(rest[:n], rest[n + 3 : 2 * n + 3], rest[2 * n + 3 :])

        @pl.when(step == 0)
        def _():
            gather()[0]()

        @pl.when(step == steps // 2)
        def _():
            gather()[1]()

        _pool_fwd_group(u_ref, pw_ref, sc_ref, pooled_ref, yp_ref, step)

        lane = lax.broadcasted_iota(jnp.int32, (T, LANES), 1)
        heads = (lane < HEAD_DIM, lane >= HEAD_DIM)
        after2, _, causal = _band_ones(T)
        zero = jnp.zeros((T, 1), F32)

        def key_tile(k0):
            vb = v_ref[pl.ds(k0, T), :]
            return k_ref[pl.ds(k0, T), :], jnp.concatenate([jnp.where(m, vb, jnp.zeros_like(vb)) for m in heads], axis=0)

        def q_tiles(qis, nb):
            queries, jobs = [], []
            for qi in qis:
                q0 = _tile_start(qi, T)
                qb = q_ref[pl.ds(q0, T), :]
                qhs = [jnp.where(m, qb, jnp.zeros_like(qb)) for m in heads]
                tiles = [key_tile(_tile_start(qi - b, T)) for b in range(nb)]
                queries.append((q0, qhs, [v_rows for _, v_rows in tiles]))
                jobs += [(qh, kb, causal if b == 0 else None) for b, (kb, _) in enumerate(tiles) for qh in qhs]
            gates = _log_gates_of(jobs, after2)
            states = []
            for i, (q0, qhs, values) in enumerate(queries):
                carries, probs = [zero, zero], []
                for b in range(nb):
                    for h in range(2):
                        ls, tail, total = gates[2 * nb * i + 2 * b + h]
                        probs.append(_weights(ls, tail, carries[h], causal if b == 0 else None).astype(BF16))
                        carries[h] = carries[h] + total
                states.append((carries, _dot(jnp.concatenate(probs, axis=1), jnp.concatenate(values, axis=0))))

            def more(st):
                return (st[0] > 0) & _reaches(jnp.maximum(st[1], st[2]))

            for qi, (q0, qhs, _), (carries, acc) in zip(qis, queries, states):

                def k_step(st, qhs=qhs):
                    kj = st[0] - 1
                    kb, v_rows = key_tile(pl.multiple_of(kj * T, T))
                    new, probs = [], []
                    for (ls, tail, total), carry in zip(_log_gates(qhs, [kb], [None], after2)[0], st[1:3]):
                        probs.append(_weights(ls, tail, carry, None).astype(BF16))
                        new.append(carry + total)
                    return kj, new[0], new[1], st[3] + _dot(jnp.concatenate(probs, axis=1), v_rows)

                if not isinstance(qi, int):
                    acc = lax.while_loop(more, k_step, (qi - (nb - 1), carries[0], carries[1], acc))[3]
                o_ref[pl.ds(q0, T), :] = acc

        head_tiles = min(ATTN_WINDOW - 1, nq)
        for qi in range(head_tiles):
            q_tiles([qi], qi + 1)
        assert (nq - head_tiles) % ATTN_TOGETHER_FWD == 0

        def q_loop(i, carry):
            first = head_tiles + ATTN_TOGETHER_FWD * i
            q_tiles([first + j for j in range(ATTN_TOGETHER_FWD)], ATTN_WINDOW)
            return carry

        lax.fori_loop(0, (nq - head_tiles) // ATTN_TOGETHER_FWD, q_loop, 0)

        @pl.when(step == steps - 1)
        def _():
            _, _, forward, finish = gather()
            forward()
            finish()

    blk = pl.BlockSpec((S, LANES), lambda p: (0, p))
    return pl.pallas_call(
        body,
        name="mixers_fwd",
        grid=(steps,),
        out_shape=[
            jax.ShapeDtypeStruct((S, D_ATTN), F32),
            jax.ShapeDtypeStruct((S, D_POOL), BF16),
            jax.ShapeDtypeStruct((S, D_POOL), F32),
        ]
        + _exchange_shapes(shards, True),
        in_specs=[blk, blk, blk, blk, _POOL_W_SPEC, _POOL_SCALE_SPEC] + [ANY_SPEC] * n,
        out_specs=[blk, blk, blk] + [ANY_SPEC] * n,
        scratch_shapes=_exchange_sems(n),
        compiler_params=pltpu.CompilerParams(
            dimension_semantics=("arbitrary",), vmem_limit_bytes=VMEM_LIMIT, has_side_effects=True
        ),
    )(q, k, v, u, pool_w, scale, *shards)


MLP_AHEAD = 1
SG_ROWS = 8


def _mlp_fwd_bwd(x, y_pool, y_attn, target, g_pool, g_attn, g2, gf, w_out, w_up, w_down):
    S = x.shape[0]
    tm = min(256, S)
    fc = D_FF // N_DEV

    def body(x_ref, yp_ref, ya_ref, t_ref, gp_ref, ga_ref, g2_ref, gf_ref, wo_hbm, wu_hbm, wd_hbm,
             mixed_ref, hn2_ref, act_ref, dup_ref, dh2_ref, dh1_ref, dh1b_ref, dyp_ref, dya_ref, sg_ref,
             wo, wu, wd, up_s, sems):
        @pl.when(pl.program_id(0) == 0)
        def _():
            copies = [
                pltpu.make_async_copy(wo_hbm, wo, sems.at[0]),
                pltpu.make_async_copy(wu_hbm, wu, sems.at[1]),
                pltpu.make_async_copy(wd_hbm, wd, sems.at[2]),
            ]
            for cp in copies:
                cp.start()
            for cp in copies:
                cp.wait()
            sg_ref[...] = jnp.zeros_like(sg_ref)

        gp, ga, g2v, gfv = gp_ref[...], ga_ref[...], g2_ref[...], gf_ref[...]
        yp, ya = yp_ref[...], ya_ref[...]
        rp, ra = _rstd(yp), _rstd(ya)
        yph, yah = yp * rp, ya * ra
        mixed = jnp.concatenate([(yph * gp).astype(BF16), (yah * ga).astype(BF16)], axis=1)
        mixed_ref[...] = mixed
        h1 = x_ref[...] + _dot(mixed, wo[...])
        r2 = _rstd(h1)
        h1h = h1 * r2
        hn2 = (h1h * g2v).astype(BF16)
        hn2_ref[...] = hn2
        h2 = h1
        ups = [_dot(hn2, wu[j]) for j in range(MLP_AHEAD)]
        for j in range(N_DEV):
            cols = slice(fc * j, fc * (j + 1))
            if j + MLP_AHEAD < N_DEV:
                ups.append(_dot(hn2, wu[j + MLP_AHEAD]))
            up = ups.pop(0)
            up_s[:, cols] = up
            act = jnp.square(jnp.maximum(up, 0.0)).astype(BF16)
            act_ref[:, cols] = act
            h2 = h2 + _dot(act, wd[cols, :])
        rf = _rstd(h2)
        h2h = h2 * rf
        diff = h2h * gfv - t_ref[...]
        loss_rows = 0.5 * jnp.mean(diff * diff, axis=-1, keepdims=True)
        dy = diff * (1.0 / D_MODEL)
        dh2, dgf = _rms_bwd(dy, h2h, rf, gfv)
        dh2b = dh2.astype(BF16)
        dh2_ref[...] = dh2b
        dhn2 = jnp.zeros((tm, D_MODEL), F32)
        dacts = [_dot_nt(dh2b, wd[fc * j : fc * (j + 1), :]) for j in range(MLP_AHEAD)]
        for j in range(N_DEV):
            cols = slice(fc * j, fc * (j + 1))
            if j + MLP_AHEAD < N_DEV:
                dacts.append(_dot_nt(dh2b, wd[fc * (j + MLP_AHEAD) : fc * (j + MLP_AHEAD + 1), :]))
            dup = (dacts.pop(0) * (2.0 * jnp.maximum(up_s[:, cols], 0.0))).astype(BF16)
            dup_ref[:, cols] = dup
            dhn2 = dhn2 + _dot_nt(dup, wu[j])
        dh1n, dg2 = _rms_bwd(dhn2, h1h, r2, g2v)
        dh1 = dh2 + dh1n
        dh1_ref[...] = dh1
        dh1b = dh1.astype(BF16)
        dh1b_ref[...] = dh1b
        dmix = _dot_nt(dh1b, wo[...])
        dyp, dgp = _rms_bwd(dmix[:, :D_POOL], yph, rp, gp)
        dya, dga = _rms_bwd(dmix[:, D_POOL:], yah, ra, ga)
        dyp_ref[...] = dyp
        dya_ref[...] = dya
        sg_ref[0:1, :] += dgf
        sg_ref[1:2, :] += dg2
        sg_ref[2:3, :] += jnp.concatenate([dgp, dga], axis=1)
        sg_ref[3:4, :] += jnp.broadcast_to(jnp.sum(loss_rows, axis=0, keepdims=True), (1, D_MODEL))

    def tok(n):
        return pl.BlockSpec((tm, n), lambda i: (i, 0))

    def vec(n):
        return pl.BlockSpec((1, n), lambda i: (0, 0))

    any_spec = pl.BlockSpec(memory_space=pl.ANY)
    return pl.pallas_call(
        body,
        name="mlp_fwd_bwd",
        grid=(S // tm,),
        out_shape=[
            jax.ShapeDtypeStruct((S, D_MODEL), BF16),
            jax.ShapeDtypeStruct((S, D_MODEL), BF16),
            jax.ShapeDtypeStruct((S, D_FF), BF16),
            jax.ShapeDtypeStruct((S, D_FF), BF16),
            jax.ShapeDtypeStruct((S, D_MODEL), BF16),
            jax.ShapeDtypeStruct((S, D_MODEL), F32),
            jax.ShapeDtypeStruct((S, D_MODEL), BF16),
            jax.ShapeDtypeStruct((S, D_POOL), F32),
            jax.ShapeDtypeStruct((S, D_ATTN), F32),
            jax.ShapeDtypeStruct((SG_ROWS, D_MODEL), F32),
        ],
        in_specs=[tok(D_MODEL), tok(D_POOL), tok(D_ATTN), tok(D_MODEL), vec(D_POOL), vec(D_ATTN),
                  vec(D_MODEL), vec(D_MODEL), any_spec, any_spec, any_spec],
        out_specs=[tok(D_MODEL), tok(D_MODEL), tok(D_FF), tok(D_FF), tok(D_MODEL), tok(D_MODEL),
                   tok(D_MODEL), tok(D_POOL), tok(D_ATTN),
                   pl.BlockSpec((SG_ROWS, D_MODEL), lambda i: (0, 0))],
        scratch_shapes=[
            pltpu.VMEM((D_MODEL, D_MODEL), BF16),
            pltpu.VMEM((N_DEV, D_MODEL, fc), BF16),
            pltpu.VMEM((D_FF, D_MODEL), BF16),
            pltpu.VMEM((tm, D_FF), F32),
            pltpu.SemaphoreType.DMA((3,)),
        ],
        compiler_params=_params(("arbitrary",)),
    )(x, y_pool, y_attn, target, g_pool, g_attn, g2, gf, w_out, w_up, w_down)


def _wgrad(a, b, block_a, groups, name, travelling=()):
    n = len(travelling)
    S, ka = a.shape
    nb = b.shape[1]
    ts = min(1024, S)
    per = N_DEV // groups
    if block_a:
        ka //= groups
        blk = (ka // per, nb)
        a_spec = pl.BlockSpec((ts, ka), lambda g, s: (s, g))
        b_spec = pl.BlockSpec((ts, nb), lambda g, s: (s, 0))
    else:
        nb //= groups
        blk = (ka, nb // per)
        a_spec = pl.BlockSpec((ts, ka), lambda g, s: (s, 0))
        b_spec = pl.BlockSpec((ts, nb), lambda g, s: (s, g))
    steps = S // ts

    def body(a_ref, b_ref, *rest):
        o_ref, acc = rest[n], rest[2 * n + 1]
        g, s = pl.program_id(0), pl.program_id(1)

        def scatter():
            return _direct_exchange(rest[:n], rest[n + 1 : 2 * n + 1], rest[2 * n + 2 :], False)

        if n:
            @pl.when((g == 0) & (s == 0))
            def _():
                scatter()[0]()

        @pl.when(s == 0)
        def _():
            acc[...] = jnp.zeros_like(acc)

        acc[...] += _dot_tn(a_ref[...], b_ref[...])

        @pl.when(s == steps - 1)
        def _():
            for j in range(per):
                if block_a:
                    o_ref[j] = acc[blk[0] * j : blk[0] * (j + 1), :].astype(BF16)
                else:
                    o_ref[j] = acc[:, blk[1] * j : blk[1] * (j + 1)].astype(BF16)

        if n:
            @pl.when((g == groups - 1) & (s == steps - 1))
            def _():
                scatter()[1]()

    results = pl.pallas_call(
        body,
        name=name,
        grid=(groups, steps),
        out_shape=[jax.ShapeDtypeStruct((N_DEV,) + blk, BF16)] + _exchange_shapes(travelling, False),
        in_specs=[a_spec, b_spec] + [ANY_SPEC] * n,
        out_specs=[pl.BlockSpec((per,) + blk, lambda g, s: (g, 0, 0))] + [ANY_SPEC] * n,
        scratch_shapes=[pltpu.VMEM((ka, nb), F32)] + (_exchange_sems(n) if n else []),
        compiler_params=pltpu.CompilerParams(
            dimension_semantics=("arbitrary", "arbitrary"), vmem_limit_bytes=VMEM_LIMIT, has_side_effects=bool(n)
        ),
    )(a, b, *travelling)
    return results if n else results[0]


def _mixers_bwd(q, k, v, do, dyp, pooled, pool_w, scale, partials):
    S = q.shape[0]
    T = ATTN_TILE
    nq = S // T
    n = len(partials)
    steps = D_ATTN // LANES

    def body(q_ref, k_ref, v_ref, do_ref, dyp_ref, pooled_ref, pw_ref, sc_ref, *rest):
        dq_ref, dk_ref, dv_ref, du_ref, dsc_ref, dpw_ref = rest[n : n + 6]
        dk_acc, dv_acc, carry_s = rest[2 * n + 6 : 2 * n + 9]
        step = pl.program_id(0)

        def scatter():
            return _direct_exchange(rest[:n], rest[n + 6 : 2 * n + 6], rest[2 * n + 9 :], False)

        @pl.when(step == 0)
        def _():
            scatter()[0]()

        _pool_bwd_group(dyp_ref, pooled_ref, pw_ref, sc_ref, du_ref, dsc_ref, dpw_ref, step)

        dk_acc[...] = jnp.zeros_like(dk_acc)
        dv_acc[...] = jnp.zeros_like(dv_acc)
        lane = lax.broadcasted_iota(jnp.int32, (T, LANES), 1)
        heads = (lane < HEAD_DIM, lane >= HEAD_DIM)
        after2, before2, causal = _band_ones(T)
        zero = jnp.zeros((T, 1), F32)

        def key_tile(k0):
            kb = k_ref[pl.ds(k0, T), :]
            k_rows = jnp.concatenate([jnp.where(m, kb, jnp.zeros_like(kb)) for m in heads], axis=0)
            return kb, v_ref[pl.ds(k0, T), :], k_rows

        def grads(items):
            das = [[[_dot_nt(doh, vb) for doh in qs[1]] for _, vb, _ in tiles] for qs, _, tiles, _, _, _, _ in items]
            probs, gs = [], []
            for (_, _, tiles, masks, carry_in, gates, _), da in zip(items, das):
                probs.append([[_weights(gates[b][h][0], gates[b][h][1], carry_in[b][h], masks[b]) for h in range(2)]
                              for b in range(len(tiles))])
                gs.append([[probs[-1][b][h] * da[b][h] for h in range(2)] for b in range(len(tiles))])
            before = [[[_dot(jnp.concatenate(_split2(g), axis=1), before2) for g in row] for row in item] for item in gs]
            dzs, lefts = [], []
            for i, (_, _, tiles, masks, _, gates, st) in enumerate(items):
                g_left, item_dz = [st[0], st[1]], [None] * len(tiles)
                for b in reversed(range(len(tiles))):
                    item_dz[b] = []
                    for h in range(2):
                        sig = jnp.exp(gates[b][h][0])
                        dz = gs[i][b][h] * (1.0 - sig) - sig * (g_left[h] + before[i][b][h])
                        if masks[b] is not None:
                            dz = jnp.where(masks[b], dz, 0.0)
                        item_dz[b].append(dz.astype(BF16))
                        g_left[h] = g_left[h] + jnp.sum(gs[i][b][h], axis=1, keepdims=True)
                dzs.append(item_dz)
                lefts.append(g_left)
            out = []
            for (_, _, tiles, _, _, _, st), item_dz, g_left in zip(items, dzs, lefts):
                dq = st[2] + _dot(
                    jnp.concatenate([dz for row in item_dz for dz in row], axis=1),
                    jnp.concatenate([k_rows for _, _, k_rows in tiles], axis=0),
                )
                out.append((g_left[0], g_left[1], dq))
            for (qs, starts, tiles, _, _, _, _), item_dz, item_probs in zip(items, dzs, probs):
                for b in range(len(tiles)):
                    dk_acc[pl.ds(starts[b], T), :] += _dot_tn(jnp.concatenate(item_dz[b], axis=0), qs[2])
                    dv_acc[pl.ds(starts[b], T), :] += _dot_tn(
                        jnp.concatenate([a.astype(BF16) for a in item_probs[b]], axis=0), qs[3]
                    )
            return out

        def more(st):
            return (st[0] > 0) & _reaches(jnp.maximum(st[1], st[2]))

        def q_tiles(qis, nb):
            masks = [causal] + [None] * (nb - 1)
            prepared, jobs = [], []
            for qi in qis:
                q0 = _tile_start(qi, T)
                qb = q_ref[pl.ds(q0, T), :]
                dob = do_ref[pl.ds(q0, T), :]
                qhs = [jnp.where(m, qb, jnp.zeros_like(qb)) for m in heads]
                dohs = [jnp.where(m, dob, 0.0).astype(BF16) for m in heads]
                queries = (qhs, dohs, jnp.concatenate(qhs, axis=0), jnp.concatenate(dohs, axis=0))
                starts = [_tile_start(qi - b, T) for b in range(nb)]
                tiles = [key_tile(k0) for k0 in starts]
                prepared.append((q0, queries, starts, tiles))
                jobs += [(qh, kb, mask) for (kb, _, _), mask in zip(tiles, masks) for qh in qhs]
            flat = _log_gates_of(jobs, after2)
            items = []
            for i, (qi, (q0, queries, starts, tiles)) in enumerate(zip(qis, prepared)):
                gates = [flat[2 * nb * i + 2 * b : 2 * nb * i + 2 * b + 2] for b in range(nb)]
                carries, carry_in = [zero, zero], []
                for b in range(nb):
                    carry_in.append(list(carries))
                    carries = [carries[h] + gates[b][h][2] for h in range(2)]
                st = (zero, zero, jnp.zeros((T, LANES), F32))
                if not isinstance(qi, int):
                    k_left = qi - (nb - 1)
                    qhs = queries[0]

                    def right_to_left(st, qhs=qhs):
                        kj = st[0] - 1
                        carry_s[0, kj] = st[1]
                        carry_s[1, kj] = st[2]
                        kb = k_ref[pl.ds(pl.multiple_of(kj * T, T), T), :]
                        sums = [jnp.sum(_log1m(_dot_nt(qh, kb), None)[0], axis=1, keepdims=True) for qh in qhs]
                        return kj, st[1] + sums[0], st[2] + sums[1]

                    k_first = lax.while_loop(more, right_to_left, (k_left, carries[0], carries[1]))[0]

                    def left_to_right(kj, st, queries=queries):
                        k0 = pl.multiple_of(kj * T, T)
                        tile = key_tile(k0)
                        left_gates = _log_gates(queries[0], [tile[0]], [None], after2)
                        carry = [[carry_s[0, kj], carry_s[1, kj]]]
                        return grads([(queries, [k0], [tile], [None], carry, left_gates, st)])[0]

                    st = lax.fori_loop(k_first, k_left, left_to_right, st)
                items.append((queries, starts, tiles, masks, carry_in, gates, st))
            for (q0, _, _, _), st in zip(prepared, grads(items)):
                dq_ref[pl.ds(q0, T), :] = (st[2] * (HEAD_DIM**-0.5)).astype(BF16)

        head_tiles = min(ATTN_WINDOW - 1, nq)
        for qi in range(head_tiles):
            q_tiles([qi], qi + 1)
        assert (nq - head_tiles) % ATTN_TOGETHER == 0

        def q_loop(i, carry):
            first = head_tiles + ATTN_TOGETHER * i
            q_tiles([first + j for j in range(ATTN_TOGETHER)], ATTN_WINDOW)
            return carry

        lax.fori_loop(0, (nq - head_tiles) // ATTN_TOGETHER, q_loop, 0)
        dk_ref[...] = dk_acc[...].astype(BF16)
        dv_ref[...] = dv_acc[...].astype(BF16)

        @pl.when(step == steps - 1)
        def _():
            scatter()[1]()

    blk = pl.BlockSpec((S, LANES), lambda p: (0, p))
    sds = jax.ShapeDtypeStruct((S, D_ATTN), BF16)
    return pl.pallas_call(
        body,
        name="mixers_bwd",
        grid=(steps,),
        out_shape=[
            sds,
            sds,
            sds,
            jax.ShapeDtypeStruct((S, D_POOL), BF16),
            jax.ShapeDtypeStruct((8, D_POOL), F32),
            jax.ShapeDtypeStruct((N_GROUPS, GROUP_DIM, GROUP_DIM), BF16),
        ]
        + _exchange_shapes(partials, False),
        in_specs=[blk] * 6 + [_POOL_W_SPEC, _POOL_SCALE_SPEC] + [ANY_SPEC] * n,
        out_specs=[blk] * 4 + [pl.BlockSpec((8, GROUP_DIM), lambda p: (0, p)), _POOL_W_SPEC] + [ANY_SPEC] * n,
        scratch_shapes=[pltpu.VMEM((S, LANES), F32), pltpu.VMEM((S, LANES), F32), pltpu.VMEM((2, nq, T, 1), F32)]
        + _exchange_sems(n),
        compiler_params=pltpu.CompilerParams(
            dimension_semantics=("arbitrary",), vmem_limit_bytes=VMEM_LIMIT, has_side_effects=True
        ),
    )(q, k, v, do, dyp, pooled, pool_w, scale, *partials)


def _pool_bwd_group(dy_ref, pooled_ref, pw_ref, sc_ref, du_ref, dsc_ref, dpw_ref, g):
    S = dy_ref.shape[0]
    R = min(POOL_CHUNK, S)
    nc = S // R
    w = jnp.left_shift(jnp.int32(2), g)
    pw = pw_ref[0].astype(BF16)
    sc = sc_ref[...]
    d = lax.broadcasted_iota(jnp.int32, (R, R), 1) - lax.broadcasted_iota(jnp.int32, (R, R), 0)
    bt_cur = jnp.where((d >= 0) & (d < w), 1.0, 0.0).astype(BF16)
    dn = lax.broadcasted_iota(jnp.int32, (R, HALO), 1) + R - lax.broadcasted_iota(jnp.int32, (R, HALO), 0)
    bt_next = jnp.where(dn < w, 1.0, 0.0).astype(BF16)

    def per_count(dpl, r0):
        n = dpl.shape[0]
        count = jnp.minimum(r0 + lax.broadcasted_iota(jnp.int32, (n, 1), 0) + 1, w).astype(F32)
        return jnp.concatenate(_split2(dpl / count), axis=1)

    def chunk(r0, last, sums):
        dyv = dy_ref[pl.ds(r0, R), :]
        pooled = pooled_ref[pl.ds(r0, R), :]
        dmapped = (dyv * sc).astype(BF16)
        dsc = sums[0] + jnp.sum(dyv * _dot(pooled, pw), axis=0, keepdims=True)
        dpw = sums[1] + _dot_tn(pooled, dmapped)
        dpl = _dot_nt(dmapped, pw)
        wide = _dot(bt_cur, per_count(dpl, r0))
        if not last:
            nxt = pl.multiple_of(r0 + R, R)
            dpl_next = _dot_nt((dy_ref[pl.ds(nxt, HALO), :] * sc).astype(BF16), pw)
            wide = wide + _dot(bt_next, per_count(dpl_next, nxt))
        du_ref[pl.ds(r0, R), :] = (_lane_sum(wide, 2) - dpl).astype(BF16)
        return dsc, dpw

    sums = (jnp.zeros((1, GROUP_DIM), F32), jnp.zeros((GROUP_DIM, GROUP_DIM), F32))
    sums = lax.fori_loop(
        0, nc - 1, lambda c, s: chunk(pl.multiple_of(c * R, R), False, s), sums, unroll=POOL_UNROLL
    )
    dsc, dpw = chunk((nc - 1) * R, True, sums)
    dsc_ref[...] = jnp.zeros_like(dsc_ref)
    dsc_ref[0:1, :] = dsc
    dpw_ref[0] = dpw.astype(dpw_ref.dtype)


def _bwd_in(du, dq, dk, dv, w_in, x, dh1, g1):
    S = x.shape[0]
    tm = min(512, S)

    def body(du_ref, dq_ref, dk_ref, dv_ref, w_ref, x_ref, dh1_ref, g_ref, dx_ref, dproj_ref, dg_ref):
        @pl.when(pl.program_id(0) == 0)
        def _():
            dg_ref[...] = jnp.zeros_like(dg_ref)

        parts = (du_ref[...], dq_ref[...], dk_ref[...], dv_ref[...])
        dhn = jnp.zeros((tm, D_MODEL), F32)
        for j in range(N_DEV):
            piece = parts[j // 2][:, 256 * (j % 2) : 256 * (j % 2 + 1)]
            dproj_ref[:, 256 * j : 256 * (j + 1)] = piece
            dhn = dhn + _dot_nt(piece, w_ref[j])
        xv = x_ref[...]
        r = _rstd(xv)
        dxn, dg = _rms_bwd(dhn, xv * r, r, g_ref[...])
        dx_ref[...] = dh1_ref[...] + dxn
        dg_ref[0:1, :] += dg

    half = pl.BlockSpec((tm, D_POOL), lambda i: (i, 0))
    full = pl.BlockSpec((tm, D_MODEL), lambda i: (i, 0))
    return pl.pallas_call(
        body,
        name="bwd_in",
        grid=(S // tm,),
        out_shape=[
            jax.ShapeDtypeStruct((S, D_MODEL), F32),
            jax.ShapeDtypeStruct((S, D_IN_PROJ), BF16),
            jax.ShapeDtypeStruct((8, D_MODEL), F32),
        ],
        in_specs=[half, half, half, half,
                  pl.BlockSpec((N_DEV, D_MODEL, 256), lambda i: (0, 0, 0)),
                  full, full, pl.BlockSpec((1, D_MODEL), lambda i: (0, 0))],
        out_specs=[full, pl.BlockSpec((tm, D_IN_PROJ), lambda i: (i, 0)),
                   pl.BlockSpec((8, D_MODEL), lambda i: (0, 0))],
        compiler_params=_params(("arbitrary",)),
    )(du, dq, dk, dv, w_in, x, dh1, g1)


def _rows(a):
    a = a.reshape(-1, LANES)
    pad = (-a.shape[0]) % 8
    return jnp.pad(a, ((0, pad), (0, 0))) if pad else a


def kernel(x, norm1_g, w_in, pool_w, pool_scale, pool_out_g, attn_out_g, w_out, norm2_g, w_up, w_down, final_g, loss_target, m_norm1_g, m_w_in, m_pool_w, m_pool_scale, m_pool_out_g, m_attn_out_g, m_w_out, m_norm2_g, m_w_up, m_w_down, m_final_g, v_norm1_g, v_w_in, v_pool_w, v_pool_scale, v_pool_out_g, v_attn_out_g, v_w_out, v_norm2_g, v_w_up, v_w_down, v_final_g):
    S = x.shape[1]
    xs = x.reshape(S, D_MODEL)
    tgt = loss_target.reshape(S, D_MODEL)
    row = lambda a: a.reshape(1, -1)

    (w_in_g,) = _exchange([w_in.astype(BF16)], True, "gather_w_in")
    lanes = lambda a: a.reshape(-1, LANES)
    vector_sets = [
        [lanes(a) for a in (final_g, norm2_g, pool_out_g, attn_out_g, norm1_g, pool_scale)],
        [lanes(a) for a in (m_final_g, m_norm2_g, m_pool_out_g, m_attn_out_g, m_norm1_g, m_pool_scale)],
        [lanes(a) for a in (v_final_g, v_norm2_g, v_pool_out_g, v_attn_out_g, v_norm1_g, v_pool_scale)],
    ]
    hn, u_pool, q, k, v, *prepared = _fwd_in(xs, row(norm1_g), w_in_g, [w_out, w_up, w_down], vector_sets)
    shards, packed_vectors = prepared[:3], prepared[3:]
    y_attn, pooled, y_pool, w_out_g, w_up_g, w_down_g = _mixers_fwd(q, k, v, u_pool, pool_w, row(pool_scale), shards)
    w_out_full = w_out_g.reshape(D_MODEL, D_MODEL)
    w_down_full = w_down_g.reshape(D_FF, D_MODEL)
    mixed, hn2, act, dup, dh2b, dh1, dh1b, dyp, dya, sg = _mlp_fwd_bwd(
        xs, y_pool, y_attn, tgt, row(pool_out_g), row(attn_out_g), row(norm2_g), row(final_g),
        w_out_full, w_up_g, w_down_full,
    )
    gp_down = _wgrad(act, dh2b, True, 2, "wgrad_down")
    gp_up = _wgrad(hn2, dup, False, 2, "wgrad_up")
    gp_out = _wgrad(mixed, dh1b, True, 1, "wgrad_out")
    dq, dk, dv, du, dsc, dpw, land_up, land_down = _mixers_bwd(
        q, k, v, dya, dyp, pooled, pool_w, row(pool_scale), [gp_up, gp_down]
    )
    dx, dproj, dg1 = _bwd_in(du, dq, dk, dv, w_in_g, xs, dh1, row(norm1_g))
    gp_in, land_out = _wgrad(hn, dproj, False, 1, "wgrad_in", [gp_out])

    partial_vectors = jnp.concatenate([_rows(p) for p in (sg[0], sg[1], sg[2], dg1[0], dsc[0], sg[3])], axis=0)
    flat_pool = lambda a: a.reshape(N_GROUPS * GROUP_DIM, GROUP_DIM)
    smalls = [
        (partial_vectors, [(*packed_vectors, None)]),
        (flat_pool(dpw), [(flat_pool(pool_w), flat_pool(m_pool_w), flat_pool(v_pool_w), None)]),
    ]
    tail = _reduce_adam_tail(
        [land_out, land_up, land_down], [w_out, w_up, w_down], [m_w_out, m_w_up, m_w_down],
        [v_w_out, v_w_up, v_w_down], gp_in, (w_in, m_w_in, v_w_in), smalls,
    )
    big = {name: tail[4 * t : 4 * t + 4] for t, name in enumerate(("w_out", "w_up", "w_down", "w_in"))}

    def unpack(vec, pw):
        out = {}
        for i, name in enumerate(("final_g", "norm2_g", "mix_g", "norm1_g", "pool_scale")):
            out[name] = vec[8 * i : 8 * i + 8].reshape(-1)
        out["pool_scale"] = out["pool_scale"][:D_POOL]
        out["pool_out_g"], out["attn_out_g"] = out["mix_g"][:D_POOL], out["mix_g"][D_POOL:]
        out["pool_w"] = pw.reshape(N_GROUPS, GROUP_DIM, GROUP_DIM)
        return out, vec[40, 0]

    small_out = [unpack(tail[16 + i], tail[20 + i]) for i in range(4)]
    loss = small_out[0][1]
    order = ("norm1_g", "w_in", "pool_w", "pool_scale", "pool_out_g", "attn_out_g", "w_out", "norm2_g", "w_up",
             "w_down", "final_g")
    outs = [loss, dx.reshape(1, S, D_MODEL)]
    for i in range(4):
        for name in order:
            outs.append(big[name][i] if name in big else small_out[i][0][name])
    return tuple(outs)
```

```python
import jax
import jax.numpy as jnp
from jax import lax
from jax.experimental import pallas as pl
from jax.experimental.pallas import tpu as pltpu

F32 = jnp.float32
BF16 = jnp.bfloat16
MESH = pl.DeviceIdType.MESH

N_DEV = 8
D_MODEL = 1024
D_POOL = 512
D_ATTN = 512
N_GROUPS = 4
GROUP_DIM = 128
HEAD_DIM = 64
D_FF = 4096
D_IN_PROJ = 2048
EPS = 1e-6
HALO = 16
ATTN_TILE = 128
LANES = 128
EXP_UNDERFLOW = -104.0

ADAM_LR = 0.001
ADAM_B1 = 0.9
ADAM_B2 = 0.999
ADAM_EPS = 1e-08
ADAM_WD = 0.01
ADAM_STEP = 10

VMEM_LIMIT = 56 * 1024 * 1024


def _params(semantics=None, vmem=VMEM_LIMIT):
    return pltpu.CompilerParams(dimension_semantics=semantics, vmem_limit_bytes=vmem)


def _dot(a, b):
    return jnp.dot(a, b, preferred_element_type=F32)


def _dot_nt(a, b):
    return lax.dot_general(a, b, (((1,), (1,)), ((), ())), preferred_element_type=F32)


def _dot_tn(a, b):
    return lax.dot_general(a, b, (((0,), (0,)), ((), ())), preferred_element_type=F32)


def _split2(x):
    hi = x.astype(BF16)
    lo = (x - hi.astype(F32)).astype(BF16)
    return hi, lo


def _split3(x):
    hi = x.astype(BF16)
    r = x - hi.astype(F32)
    mid = r.astype(BF16)
    lo = (r - mid.astype(F32)).astype(BF16)
    return hi, mid, lo


def _rstd(h):
    return lax.rsqrt(jnp.mean(h * h, axis=-1, keepdims=True) + EPS)


def _rms_bwd(dout, hhat, r, g):
    dg = jnp.sum(dout * hhat, axis=0, keepdims=True)
    dxh = dout * g
    dh = r * (dxh - hhat * jnp.mean(dxh * hhat, axis=-1, keepdims=True))
    return dh, dg


def _my_index():
    return 4 * lax.axis_index("x") + 2 * lax.axis_index("y") + lax.axis_index("c")


def _peer(k):
    x, y, c = lax.axis_index("x"), lax.axis_index("y"), lax.axis_index("c")
    px = 1 - x if (k >> 2) & 1 else x
    py = 1 - y if (k >> 1) & 1 else y
    pc = 1 - c if k & 1 else c
    return (px, py, pc), 4 * px + 2 * py + pc


N_PEERS = N_DEV - 1
ANY_SPEC = pl.BlockSpec(memory_space=pl.ANY)


def _exchange_sems(n):
    return [
        pltpu.SemaphoreType.DMA((n * N_PEERS,)),
        pltpu.SemaphoreType.DMA((n * N_PEERS,)),
        pltpu.SemaphoreType.DMA((n,)),
    ]


def _exchange_shapes(blocks, gather):
    if gather:
        return [jax.ShapeDtypeStruct((N_DEV,) + b.shape, b.dtype) for b in blocks]
    return [jax.ShapeDtypeStruct(b.shape, b.dtype) for b in blocks]


def _direct_exchange(ins, outs, sems, gather):
    send_sems, recv_sems, local_sems = sems
    me = _my_index()
    own, sends, recvs = [], [], []
    for t in range(len(ins)):
        own.append(pltpu.make_async_copy(ins[t] if gather else ins[t].at[me], outs[t].at[me], local_sems.at[t]))
        for k in range(1, N_DEV):
            peer, peer_idx = _peer(k)
            src = ins[t] if gather else ins[t].at[peer_idx]
            for dst, bucket in ((outs[t].at[me], sends), (outs[t].at[peer_idx], recvs)):
                bucket.append(
                    pltpu.make_async_remote_copy(
                        src_ref=src,
                        dst_ref=dst,
                        send_sem=send_sems.at[t * N_PEERS + k - 1],
                        recv_sem=recv_sems.at[t * N_PEERS + k - 1],
                        device_id=peer,
                        device_id_type=MESH,
                    )
                )

    def start():
        for cp in own + sends:
            cp.start()

    def finish():
        for cp in recvs:
            cp.wait_recv()
        for cp in sends:
            cp.wait_send()
        for cp in own:
            cp.wait()

    return start, finish


def _flip(a, bit):
    return a + bit - 2 * a * bit


def _two_level_gather(ins, outs, sems):
    send_sems, recv_sems, local_sems = sems
    x, y, c = lax.axis_index("x"), lax.axis_index("y"), lax.axis_index("c")
    me, sibling = (x, y, c), (x, y, 1 - c)
    x_nbr, y_nbr, diagonal = (1 - x, y, c), (x, 1 - y, c), (1 - x, 1 - y, c)
    relay_of = (_flip(x, 1 - c), _flip(y, c), c)
    relay_to = (_flip(x, c), _flip(y, 1 - c), c)

    def copy(t, k, block, to, from_input=False):
        slot = outs[t].at[4 * block[0] + 2 * block[1] + block[2]]
        return pltpu.make_async_remote_copy(
            src_ref=ins[t] if from_input else slot,
            dst_ref=slot,
            send_sem=send_sems.at[t * N_PEERS + k],
            recv_sem=recv_sems.at[t * N_PEERS + k],
            device_id=to,
            device_id_type=MESH,
        )

    arrays = range(len(ins))
    own = [pltpu.make_async_copy(ins[t], outs[t].at[4 * x + 2 * y + c], local_sems.at[t]) for t in arrays]
    first = [copy(t, k, me, to, True) for t in arrays for k, to in ((1, x_nbr), (2, y_nbr), (0, sibling))]
    landed = [copy(t, k, block, me) for t in arrays for k, block in ((1, x_nbr), (2, y_nbr))]
    relays = [copy(t, 3, relay_of, relay_to) for t in arrays]
    passed = [copy(t, 3 + k, block, sibling) for t in arrays for k, block in ((1, x_nbr), (2, y_nbr))]
    relayed = [(copy(t, 3, diagonal, me), copy(t, 6, diagonal, sibling)) for t in arrays]
    last = [copy(t, 0, sibling, me) for t in arrays]
    last += [copy(t, 3 + k, (*block[:2], 1 - c), me) for t in arrays for k, block in ((1, x_nbr), (2, y_nbr), (3, diagonal))]

    def start():
        for cp in own + first:
            cp.start()

    def relay():
        for cp in landed:
            cp.wait_recv()
        for cp in relays + passed:
            cp.start()

    def forward():
        for arrived, onward in relayed:
            arrived.wait_recv()
            onward.start()

    def finish():
        for cp in last:
            cp.wait_recv()
        for cp in first + relays + passed + [onward for _, onward in relayed]:
            cp.wait_send()
        for cp in own:
            cp.wait()

    return start, relay, forward, finish


N_CHIPS = 4
PAIR_SEMS = (N_CHIPS, N_CHIPS, N_CHIPS - 1, N_CHIPS - 1)


def _pair_reduce_scatter(gp_hbm, own, pair, summed, land, local_sem, sems):
    d2d_send, d2d_recv, ici_send, ici_recv = sems
    x, y, c = lax.axis_index("x"), lax.axis_index("y"), lax.axis_index("c")
    my_chip = 2 * x + y
    sibling = (x, y, 1 - c)
    chips = [(1 - x, y), (x, 1 - y), (1 - x, 1 - y)]
    local = [pltpu.make_async_copy(gp_hbm.at[2 * j + c], own.at[j], local_sem.at[0]) for j in range(N_CHIPS)]
    to_sibling = [
        pltpu.make_async_remote_copy(
            src_ref=gp_hbm.at[2 * j + 1 - c], dst_ref=pair.at[j], send_sem=d2d_send.at[j], recv_sem=d2d_recv.at[j],
            device_id=sibling, device_id_type=MESH,
        )
        for j in range(N_CHIPS)
    ]
    to_chips, from_chips = [], []
    for r, (px, py) in enumerate(chips):
        for dst, bucket in ((land.at[my_chip], to_chips), (land.at[2 * px + py], from_chips)):
            bucket.append(
                pltpu.make_async_remote_copy(
                    src_ref=summed.at[2 * px + py], dst_ref=dst, send_sem=ici_send.at[r], recv_sem=ici_recv.at[r],
                    device_id=(px, py, c), device_id_type=MESH,
                )
            )

    def start():
        for cp in local + to_sibling:
            cp.start()

    def middle():
        for cp in to_sibling:
            cp.wait_recv()
        pltpu.make_async_copy(gp_hbm.at[pl.ds(0, N_CHIPS)], own, local_sem.at[0]).wait()
        for j in range(N_CHIPS):
            summed[j] = (own[j].astype(F32) + pair[j].astype(F32)).astype(BF16)
        for cp in to_chips:
            cp.start()

    def finish():
        for cp in from_chips:
            cp.wait_recv()
        for cp in to_chips + to_sibling:
            cp.wait_send()
        land[my_chip] = summed[my_chip]

    return start, middle, finish


def _exchange(blocks, gather, name):
    n = len(blocks)

    def body(*refs):
        if gather:
            stages = _two_level_gather(refs[:n], refs[n : 2 * n], refs[2 * n :])
        else:
            stages = _direct_exchange(refs[:n], refs[n : 2 * n], refs[2 * n :], False)
        for stage in stages:
            stage()

    return pl.pallas_call(
        body,
        name=name,
        out_shape=_exchange_shapes(blocks, gather),
        in_specs=[ANY_SPEC] * n,
        out_specs=[ANY_SPEC] * n,
        scratch_shapes=_exchange_sems(n),
        compiler_params=pltpu.CompilerParams(has_side_effects=True),
    )(*blocks)


def _adam(w, g, m, v):
    m2 = ADAM_B1 * m + (1.0 - ADAM_B1) * g
    v2 = ADAM_B2 * v + (1.0 - ADAM_B2) * jnp.square(g)
    m_hat = m2 / (1.0 - ADAM_B1**ADAM_STEP)
    v_hat = v2 / (1.0 - ADAM_B2**ADAM_STEP)
    delta = -ADAM_LR * (m_hat / (jnp.sqrt(v_hat) + ADAM_EPS) + ADAM_WD * w)
    return delta, m2, v2


TAIL_STEPS = 8


def _reduce_adam_tail(lands, ws, ms, vs, gp_last, last, smalls):
    nw, ns = len(ws), len(smalls)
    tiles = [w.shape[0] // TAIL_STEPS for w in ws]
    blk = gp_last.shape[1:]
    entries = [(i, e) for i, (_, es) in enumerate(smalls) for e in es]
    n_params = 3 * sum(e[0] is not None for _, e in entries)
    n_small_out = sum(4 if e[0] is not None else 1 for _, e in entries)

    def body(*refs):
        gp_hbm = refs[0]
        small_land = refs[1 : 1 + ns]
        refs = refs[1 + ns :]
        land_refs = refs[:nw]
        w_refs, m_refs, v_refs = (refs[nw * (i + 1) : nw * (i + 2)] for i in range(3))
        last_refs = refs[4 * nw : 4 * nw + 3]
        refs = refs[4 * nw + 3 :]
        param_refs = refs[:n_params]
        outs = refs[n_params:]
        big_out = outs[: 4 * nw]
        last_out = outs[4 * nw : 4 * nw + 4]
        small_out = outs[4 * nw + 4 : 4 * nw + 4 + n_small_out]
        scratch = outs[4 * nw + 4 + n_small_out :]
        pair_bufs = scratch[:4]
        local_sem = scratch[4]
        pair_sems = scratch[5:9]
        step = pl.program_id(0)

        def pair():
            return _pair_reduce_scatter(gp_hbm, *pair_bufs, local_sem, pair_sems)

        @pl.when(step == 0)
        def _():
            pair()[0]()

        @pl.when(step == 1)
        def _():
            pair()[1]()

        for t in range(nw):
            g = land_refs[t][0].astype(F32)
            for s in range(1, N_DEV):
                g = g + land_refs[t][s].astype(F32)
            for ref, val in zip(big_out[4 * t : 4 * t + 4], (g,) + _adam(w_refs[t][...], g, m_refs[t][...], v_refs[t][...])):
                ref[...] = val

        @pl.when(step == TAIL_STEPS - 1)
        def _():
            pair()[2]()
            land = pair_bufs[3]
            g = land[0].astype(F32)
            for chip in range(1, N_CHIPS):
                g = g + land[chip].astype(F32)
            for ref, val in zip(last_out, (g,) + _adam(last_refs[0][...], g, last_refs[1][...], last_refs[2][...])):
                ref[...] = val
            sums = []
            for i in range(ns):
                g = small_land[i][0].astype(F32)
                for s in range(1, N_DEV):
                    g = g + small_land[i][s].astype(F32)
                sums.append(g)
            params, results = list(param_refs), list(small_out)
            for i, (w, _, _, where) in entries:
                g = sums[i]
                if where is not None:
                    shape = w.shape if w is not None else where[2:]
                    g = g[where[0] : where[0] + shape[0], where[1] : where[1] + shape[1]]
                if w is None:
                    results.pop(0)[...] = g
                    continue
                w_ref, m_ref, v_ref = params[:3]
                del params[:3]
                for val in (g,) + _adam(w_ref[...], g, m_ref[...], v_ref[...]):
                    results.pop(0)[...] = val

    def tile(t):
        return pl.BlockSpec((tiles[t], ws[t].shape[1]), lambda i: (i, 0))

    def land_tile(t):
        return pl.BlockSpec((N_DEV, tiles[t], ws[t].shape[1]), lambda i: (0, i, 0))

    def whole(shape):
        return pl.BlockSpec(shape, lambda step: (0, 0))

    big_sds = [jax.ShapeDtypeStruct(ws[t].shape, F32) for t in range(nw) for _ in range(4)]
    params, small_shapes = [], []
    for _, (w, m, v, where) in entries:
        if w is None:
            small_shapes.append(tuple(where[2:]))
        else:
            params += [w, m, v]
            small_shapes += [w.shape] * 4
    return pl.pallas_call(
        body,
        name="reduce_adam_tail",
        grid=(TAIL_STEPS,),
        out_shape=big_sds
        + [jax.ShapeDtypeStruct(blk, F32)] * 4
        + [jax.ShapeDtypeStruct(shape, F32) for shape in small_shapes],
        in_specs=[ANY_SPEC]
        + [pl.BlockSpec(smalls[i][0].shape, lambda step: (0, 0, 0)) for i in range(ns)]
        + [land_tile(t) for t in range(nw)]
        + [tile(t) for _ in range(3) for t in range(nw)]
        + [whole(blk)] * 3
        + [whole(p.shape) for p in params],
        out_specs=[tile(t) for t in range(nw) for _ in range(4)]
        + [whole(blk)] * 4
        + [whole(shape) for shape in small_shapes],
        scratch_shapes=[pltpu.VMEM((N_CHIPS,) + blk, BF16)] * 4
        + [pltpu.SemaphoreType.DMA((1,))]
        + [pltpu.SemaphoreType.DMA((count,)) for count in PAIR_SEMS],
        compiler_params=pltpu.CompilerParams(
            dimension_semantics=("arbitrary",), vmem_limit_bytes=VMEM_LIMIT, has_side_effects=True
        ),
    )(gp_last, *[s[0] for s in smalls], *lands, *ws, *ms, *vs, *last, *params)


VECTOR_ROWS = 48


def _fwd_in(x, g1, w_in, shards, vector_sets):
    S = x.shape[0]
    tm = min(512, S)
    steps = S // tm
    n = len(shards)
    nv = sum(len(vs) for vs in vector_sets)

    def body(x_ref, g_ref, w_ref, *rest):
        hn_ref, u_ref, q_ref, k_ref, v_ref = rest[n + nv : n + nv + 5]
        for src, dst in zip(rest[:n], rest[n + nv + 5 :]):
            dst[...] = src[...].astype(BF16)
        vectors = list(rest[n : n + nv])
        for vs, packed in zip(vector_sets, rest[2 * n + nv + 5 :]):
            at = 0
            for _ in vs:
                ref = vectors.pop(0)
                packed[at : at + ref.shape[0], :] = ref[...]
                at += ref.shape[0]
            packed[at:, :] = jnp.zeros((VECTOR_ROWS - at, LANES), F32)
        xv = x_ref[...]
        hn = (xv * _rstd(xv) * g_ref[...]).astype(BF16)
        hn_ref[...] = hn
        outs = (u_ref, q_ref, k_ref, v_ref)
        for j in range(N_DEV):
            p = _dot(hn, w_ref[j])
            cols = slice(256 * (j % 2), 256 * (j % 2 + 1))
            if j // 2 == 0:
                u_ref[:, cols] = p
            elif j // 2 == 1:
                q_ref[:, cols] = (p * (HEAD_DIM**-0.5)).astype(BF16)
            else:
                outs[j // 2][:, cols] = p.astype(BF16)

    half = pl.BlockSpec((tm, D_POOL), lambda i: (i, 0))
    shard_tiles = [pl.BlockSpec((s.shape[0] // steps, s.shape[1]), lambda i: (i, 0)) for s in shards]
    return pl.pallas_call(
        body,
        name="fwd_in",
        grid=(steps,),
        out_shape=[
            jax.ShapeDtypeStruct((S, D_MODEL), BF16),
            jax.ShapeDtypeStruct((S, D_POOL), F32),
            jax.ShapeDtypeStruct((S, D_ATTN), BF16),
            jax.ShapeDtypeStruct((S, D_ATTN), BF16),
            jax.ShapeDtypeStruct((S, D_ATTN), BF16),
        ]
        + [jax.ShapeDtypeStruct(s.shape, BF16) for s in shards]
        + [jax.ShapeDtypeStruct((VECTOR_ROWS, LANES), F32)] * len(vector_sets),
        in_specs=[
            pl.BlockSpec((tm, D_MODEL), lambda i: (i, 0)),
            pl.BlockSpec((1, D_MODEL), lambda i: (0, 0)),
            pl.BlockSpec((N_DEV, D_MODEL, 256), lambda i: (0, 0, 0)),
        ]
        + shard_tiles
        + [pl.BlockSpec(v.shape, lambda i: (0, 0)) for vs in vector_sets for v in vs],
        out_specs=[pl.BlockSpec((tm, D_MODEL), lambda i: (i, 0)), half, half, half, half]
        + shard_tiles
        + [pl.BlockSpec((VECTOR_ROWS, LANES), lambda i: (0, 0))] * len(vector_sets),
        compiler_params=_params(("arbitrary",)),
    )(x, g1, w_in, *shards, *[v for vs in vector_sets for v in vs])


POOL_CHUNK = 256
POOL_UNROLL = 5
_POOL_W_SPEC = pl.BlockSpec((1, GROUP_DIM, GROUP_DIM), lambda g: (g, 0, 0))
_POOL_SCALE_SPEC = pl.BlockSpec((1, GROUP_DIM), lambda g: (0, g))


def _lane_sum(wide, n):
    out = wide[:, :GROUP_DIM]
    for i in range(1, n):
        out = out + wide[:, GROUP_DIM * i : GROUP_DIM * (i + 1)]
    return out


def _pool_fwd_group(u_ref, pw_ref, sc_ref, pooled_ref, y_ref, g):
    S = u_ref.shape[0]
    R = min(POOL_CHUNK, S)
    w = jnp.left_shift(jnp.int32(2), g)
    d = lax.broadcasted_iota(jnp.int32, (R, R), 0) - lax.broadcasted_iota(jnp.int32, (R, R), 1)
    b_cur = jnp.where((d >= 0) & (d < w), 1.0, 0.0).astype(BF16)
    dp = lax.broadcasted_iota(jnp.int32, (R, HALO), 0) + HALO - lax.broadcasted_iota(jnp.int32, (R, HALO), 1)
    b_prev = jnp.where(dp < w, 1.0, 0.0).astype(BF16)
    pw = pw_ref[0].astype(BF16)
    sc = sc_ref[...]

    def chunk(r0, first):
        cur = u_ref[pl.ds(r0, R), :]
        wide = _dot(b_cur, jnp.concatenate(_split3(cur), axis=1))
        if not first:
            prev = u_ref[pl.ds(pl.multiple_of(r0 - HALO, HALO), HALO), :]
            wide = wide + _dot(b_prev, jnp.concatenate(_split3(prev), axis=1))
        count = jnp.minimum(r0 + lax.broadcasted_iota(jnp.int32, (R, 1), 0) + 1, w).astype(F32)
        pooled = (_lane_sum(wide, 3) / count - cur).astype(BF16)
        pooled_ref[pl.ds(r0, R), :] = pooled
        y_ref[pl.ds(r0, R), :] = _dot(pooled, pw) * sc

    chunk(0, True)

    def rest(c, carry):
        chunk(pl.multiple_of(c * R, R), False)
        return carry

    lax.fori_loop(1, S // R, rest, 0, unroll=POOL_UNROLL)


ATTN_WINDOW = 3
ATTN_TOGETHER = 2
ATTN_TOGETHER_FWD = 3


def _band_ones(T):
    row = lax.broadcasted_iota(jnp.int32, (T, T), 0)
    col = lax.broadcasted_iota(jnp.int32, (T, T), 1)
    after = jnp.where(row > col, 1.0, 0.0).astype(BF16)
    before = jnp.where(row < col, 1.0, 0.0).astype(BF16)
    return jnp.concatenate([after, after], axis=0), jnp.concatenate([before, before], axis=0), col < row


def _log1m(z, mask):
    sp = jnp.log(1.0 + jnp.exp(-jnp.abs(z)))
    l1m = -jnp.maximum(z, 0.0) - sp
    return (l1m if mask is None else jnp.where(mask, l1m, 0.0)), sp


def _log_gates_of(jobs, after2):
    zs = [_dot_nt(qh, kb) for qh, kb, _ in jobs]
    terms = []
    for z, (_, _, mask) in zip(zs, jobs):
        l1m, sp = _log1m(z, mask)
        terms.append(
            (jnp.minimum(z, 0.0) - sp, jnp.concatenate(_split2(l1m), axis=1), jnp.sum(l1m, axis=1, keepdims=True))
        )
    return [(ls, _dot(split, after2), total) for ls, split, total in terms]


def _log_gates(qhs, kbs, masks, after2):
    flat = _log_gates_of([(qh, kb, mask) for kb, mask in zip(kbs, masks) for qh in qhs], after2)
    return [flat[len(qhs) * b : len(qhs) * (b + 1)] for b in range(len(kbs))]


def _tile_start(index, T):
    return index * T if isinstance(index, int) else pl.multiple_of(index * T, T)


def _weights(ls, tail, carry, mask):
    a = jnp.exp(ls + tail + carry)
    return a if mask is None else jnp.where(mask, a, 0.0)


def _reaches(carry):
    return jnp.max(carry) > EXP_UNDERFLOW


def _mixers_fwd(q, k, v, u, pool_w, scale, shards):
    S = q.shape[0]
    T = ATTN_TILE
    nq = S // T
    n = len(shards)
    steps = D_ATTN // LANES

    def body(q_ref, k_ref, v_ref, u_ref, pw_ref, sc_ref, *rest):
        o_ref, pooled_ref, yp_ref = rest[n : n + 3]
        step = pl.program_id(0)

        def gather():
            return _two_level_gather(rest[:n], rest[n + 3 : 2 * n + 3], rest[2 * n + 3 :])

        @pl.when(step == 0)
        def _():
            gather()[0]()

        @pl.when(step == steps // 2)
        def _():
            gather()[1]()

        _pool_fwd_group(u_ref, pw_ref, sc_ref, pooled_ref, yp_ref, step)

        lane = lax.broadcasted_iota(jnp.int32, (T, LANES), 1)
        heads = (lane < HEAD_DIM, lane >= HEAD_DIM)
        after2, _, causal = _band_ones(T)
        zero = jnp.zeros((T, 1), F32)

        def key_tile(k0):
            vb = v_ref[pl.ds(k0, T), :]
            return k_ref[pl.ds(k0, T), :], jnp.concatenate([jnp.where(m, vb, jnp.zeros_like(vb)) for m in heads], axis=0)

        def q_tiles(qis, nb):
            queries, jobs = [], []
            for qi in qis:
                q0 = _tile_start(qi, T)
                qb = q_ref[pl.ds(q0, T), :]
                qhs = [jnp.where(m, qb, jnp.zeros_like(qb)) for m in heads]
                tiles = [key_tile(_tile_start(qi - b, T)) for b in range(nb)]
                queries.append((q0, qhs, [v_rows for _, v_rows in tiles]))
                jobs += [(qh, kb, causal if b == 0 else None) for b, (kb, _) in enumerate(tiles) for qh in qhs]
            gates = _log_gates_of(jobs, after2)
            states = []
            for i, (q0, qhs, values) in enumerate(queries):
                carries, probs = [zero, zero], []
                for b in range(nb):
                    for h in range(2):
                        ls, tail, total = gates[2 * nb * i + 2 * b + h]
                        probs.append(_weights(ls, tail, carries[h], causal if b == 0 else None).astype(BF16))
                        carries[h] = carries[h] + total
                states.append((carries, _dot(jnp.concatenate(probs, axis=1), jnp.concatenate(values, axis=0))))

            def more(st):
                return (st[0] > 0) & _reaches(jnp.maximum(st[1], st[2]))

            for qi, (q0, qhs, _), (carries, acc) in zip(qis, queries, states):

                def k_step(st, qhs=qhs):
                    kj = st[0] - 1
                    kb, v_rows = key_tile(pl.multiple_of(kj * T, T))
                    new, probs = [], []
                    for (ls, tail, total), carry in zip(_log_gates(qhs, [kb], [None], after2)[0], st[1:3]):
                        probs.append(_weights(ls, tail, carry, None).astype(BF16))
                        new.append(carry + total)
                    return kj, new[0], new[1], st[3] + _dot(jnp.concatenate(probs, axis=1), v_rows)

                if not isinstance(qi, int):
                    acc = lax.while_loop(more, k_step, (qi - (nb - 1), carries[0], carries[1], acc))[3]
                o_ref[pl.ds(q0, T), :] = acc

        head_tiles = min(ATTN_WINDOW - 1, nq)
        for qi in range(head_tiles):
            q_tiles([qi], qi + 1)
        assert (nq - head_tiles) % ATTN_TOGETHER_FWD == 0

        def q_loop(i, carry):
            first = head_tiles + ATTN_TOGETHER_FWD * i
            q_tiles([first + j for j in range(ATTN_TOGETHER_FWD)], ATTN_WINDOW)
            return carry

        lax.fori_loop(0, (nq - head_tiles) // ATTN_TOGETHER_FWD, q_loop, 0)

        @pl.when(step == steps - 1)
        def _():
            _, _, forward, finish = gather()
            forward()
            finish()

    blk = pl.BlockSpec((S, LANES), lambda p: (0, p))
    return pl.pallas_call(
        body,
        name="mixers_fwd",
        grid=(steps,),
        out_shape=[
            jax.ShapeDtypeStruct((S, D_ATTN), F32),
            jax.ShapeDtypeStruct((S, D_POOL), BF16),
            jax.ShapeDtypeStruct((S, D_POOL), F32),
        ]
        + _exchange_shapes(shards, True),
        in_specs=[blk, blk, blk, blk, _POOL_W_SPEC, _POOL_SCALE_SPEC] + [ANY_SPEC] * n,
        out_specs=[blk, blk, blk] + [ANY_SPEC] * n,
        scratch_shapes=_exchange_sems(n),
        compiler_params=pltpu.CompilerParams(
            dimension_semantics=("arbitrary",), vmem_limit_bytes=VMEM_LIMIT, has_side_effects=True
        ),
    )(q, k, v, u, pool_w, scale, *shards)


MLP_AHEAD = 1
SG_ROWS = 8


def _mlp_fwd_bwd(x, y_pool, y_attn, target, g_pool, g_attn, g2, gf, w_out, w_up, w_down):
    S = x.shape[0]
    tm = min(256, S)
    fc = D_FF // N_DEV

    def body(x_ref, yp_ref, ya_ref, t_ref, gp_ref, ga_ref, g2_ref, gf_ref, wo_hbm, wu_hbm, wd_hbm,
             mixed_ref, hn2_ref, act_ref, dup_ref, dh2_ref, dh1_ref, dh1b_ref, dyp_ref, dya_ref, sg_ref,
             wo, wu, wd, up_s, sems):
        @pl.when(pl.program_id(0) == 0)
        def _():
            copies = [
                pltpu.make_async_copy(wo_hbm, wo, sems.at[0]),
                pltpu.make_async_copy(wu_hbm, wu, sems.at[1]),
                pltpu.make_async_copy(wd_hbm, wd, sems.at[2]),
            ]
            for cp in copies:
                cp.start()
            for cp in copies:
                cp.wait()
            sg_ref[...] = jnp.zeros_like(sg_ref)

        gp, ga, g2v, gfv = gp_ref[...], ga_ref[...], g2_ref[...], gf_ref[...]
        yp, ya = yp_ref[...], ya_ref[...]
        rp, ra = _rstd(yp), _rstd(ya)
        yph, yah = yp * rp, ya * ra
        mixed = jnp.concatenate([(yph * gp).astype(BF16), (yah * ga).astype(BF16)], axis=1)
        mixed_ref[...] = mixed
        h1 = x_ref[...] + _dot(mixed, wo[...])
        r2 = _rstd(h1)
        h1h = h1 * r2
        hn2 = (h1h * g2v).astype(BF16)
        hn2_ref[...] = hn2
        h2 = h1
        ups = [_dot(hn2, wu[j]) for j in range(MLP_AHEAD)]
        for j in range(N_DEV):
            cols = slice(fc * j, fc * (j + 1))
            if j + MLP_AHEAD < N_DEV:
                ups.append(_dot(hn2, wu[j + MLP_AHEAD]))
            up = ups.pop(0)
            up_s[:, cols] = up
            act = jnp.square(jnp.maximum(up, 0.0)).astype(BF16)
            act_ref[:, cols] = act
            h2 = h2 + _dot(act, wd[cols, :])
        rf = _rstd(h2)
        h2h = h2 * rf
        diff = h2h * gfv - t_ref[...]
        loss_rows = 0.5 * jnp.mean(diff * diff, axis=-1, keepdims=True)
        dy = diff * (1.0 / D_MODEL)
        dh2, dgf = _rms_bwd(dy, h2h, rf, gfv)
        dh2b = dh2.astype(BF16)
        dh2_ref[...] = dh2b
        dhn2 = jnp.zeros((tm, D_MODEL), F32)
        dacts = [_dot_nt(dh2b, wd[fc * j : fc * (j + 1), :]) for j in range(MLP_AHEAD)]
        for j in range(N_DEV):
            cols = slice(fc * j, fc * (j + 1))
            if j + MLP_AHEAD < N_DEV:
                dacts.append(_dot_nt(dh2b, wd[fc * (j + MLP_AHEAD) : fc * (j + MLP_AHEAD + 1), :]))
            dup = (dacts.pop(0) * (2.0 * jnp.maximum(up_s[:, cols], 0.0))).astype(BF16)
            dup_ref[:, cols] = dup
            dhn2 = dhn2 + _dot_nt(dup, wu[j])
        dh1n, dg2 = _rms_bwd(dhn2, h1h, r2, g2v)
        dh1 = dh2 + dh1n
        dh1_ref[...] = dh1
        dh1b = dh1.astype(BF16)
        dh1b_ref[...] = dh1b
        dmix = _dot_nt(dh1b, wo[...])
        dyp, dgp = _rms_bwd(dmix[:, :D_POOL], yph, rp, gp)
        dya, dga = _rms_bwd(dmix[:, D_POOL:], yah, ra, ga)
        dyp_ref[...] = dyp
        dya_ref[...] = dya
        sg_ref[0:1, :] += dgf
        sg_ref[1:2, :] += dg2
        sg_ref[2:3, :] += jnp.concatenate([dgp, dga], axis=1)
        sg_ref[3:4, :] += jnp.broadcast_to(jnp.sum(loss_rows, axis=0, keepdims=True), (1, D_MODEL))

    def tok(n):
        return pl.BlockSpec((tm, n), lambda i: (i, 0))

    def vec(n):
        return pl.BlockSpec((1, n), lambda i: (0, 0))

    any_spec = pl.BlockSpec(memory_space=pl.ANY)
    return pl.pallas_call(
        body,
        name="mlp_fwd_bwd",
        grid=(S // tm,),
        out_shape=[
            jax.ShapeDtypeStruct((S, D_MODEL), BF16),
            jax.ShapeDtypeStruct((S, D_MODEL), BF16),
            jax.ShapeDtypeStruct((S, D_FF), BF16),
            jax.ShapeDtypeStruct((S, D_FF), BF16),
            jax.ShapeDtypeStruct((S, D_MODEL), BF16),
            jax.ShapeDtypeStruct((S, D_MODEL), F32),
            jax.ShapeDtypeStruct((S, D_MODEL), BF16),
            jax.ShapeDtypeStruct((S, D_POOL), F32),
            jax.ShapeDtypeStruct((S, D_ATTN), F32),
            jax.ShapeDtypeStruct((SG_ROWS, D_MODEL), F32),
        ],
        in_specs=[tok(D_MODEL), tok(D_POOL), tok(D_ATTN), tok(D_MODEL), vec(D_POOL), vec(D_ATTN),
                  vec(D_MODEL), vec(D_MODEL), any_spec, any_spec, any_spec],
        out_specs=[tok(D_MODEL), tok(D_MODEL), tok(D_FF), tok(D_FF), tok(D_MODEL), tok(D_MODEL),
                   tok(D_MODEL), tok(D_POOL), tok(D_ATTN),
                   pl.BlockSpec((SG_ROWS, D_MODEL), lambda i: (0, 0))],
        scratch_shapes=[
            pltpu.VMEM((D_MODEL, D_MODEL), BF16),
            pltpu.VMEM((N_DEV, D_MODEL, fc), BF16),
            pltpu.VMEM((D_FF, D_MODEL), BF16),
            pltpu.VMEM((tm, D_FF), F32),
            pltpu.SemaphoreType.DMA((3,)),
        ],
        compiler_params=_params(("arbitrary",)),
    )(x, y_pool, y_attn, target, g_pool, g_attn, g2, gf, w_out, w_up, w_down)


def _wgrad(a, b, block_a, groups, name, travelling=(), gather=False):
    n = len(travelling)
    S, ka = a.shape
    nb = b.shape[1]
    ts = min(1024, S)
    per = N_DEV // groups
    if block_a:
        ka //= groups
        blk = (ka // per, nb)
        a_spec = pl.BlockSpec((ts, ka), lambda g, s: (s, g))
        b_spec = pl.BlockSpec((ts, nb), lambda g, s: (s, 0))
    else:
        nb //= groups
        blk = (ka, nb // per)
        a_spec = pl.BlockSpec((ts, ka), lambda g, s: (s, 0))
        b_spec = pl.BlockSpec((ts, nb), lambda g, s: (s, g))
    steps = S // ts

    def body(a_ref, b_ref, *rest):
        o_ref, acc = rest[n], rest[2 * n + 1]
        g, s = pl.program_id(0), pl.program_id(1)

        def scatter():
            return _direct_exchange(rest[:n], rest[n + 1 : 2 * n + 1], rest[2 * n + 2 :], gather)

        if n:
            @pl.when((g == 0) & (s == 0))
            def _():
                scatter()[0]()

        @pl.when(s == 0)
        def _():
            acc[...] = jnp.zeros_like(acc)

        acc[...] += _dot_tn(a_ref[...], b_ref[...])

        @pl.when(s == steps - 1)
        def _():
            for j in range(per):
                if block_a:
                    o_ref[j] = acc[blk[0] * j : blk[0] * (j + 1), :].astype(BF16)
                else:
                    o_ref[j] = acc[:, blk[1] * j : blk[1] * (j + 1)].astype(BF16)

        if n:
            @pl.when((g == groups - 1) & (s == steps - 1))
            def _():
                scatter()[1]()

    results = pl.pallas_call(
        body,
        name=name,
        grid=(groups, steps),
        out_shape=[jax.ShapeDtypeStruct((N_DEV,) + blk, BF16)] + _exchange_shapes(travelling, gather),
        in_specs=[a_spec, b_spec] + [ANY_SPEC] * n,
        out_specs=[pl.BlockSpec((per,) + blk, lambda g, s: (g, 0, 0))] + [ANY_SPEC] * n,
        scratch_shapes=[pltpu.VMEM((ka, nb), F32)] + (_exchange_sems(n) if n else []),
        compiler_params=pltpu.CompilerParams(
            dimension_semantics=("arbitrary", "arbitrary"), vmem_limit_bytes=VMEM_LIMIT, has_side_effects=bool(n)
        ),
    )(a, b, *travelling)
    return results if n else results[0]


def _mixers_bwd(q, k, v, do, dyp, pooled, pool_w, scale, partials):
    S = q.shape[0]
    T = ATTN_TILE
    nq = S // T
    n = len(partials)
    steps = D_ATTN // LANES

    def body(q_ref, k_ref, v_ref, do_ref, dyp_ref, pooled_ref, pw_ref, sc_ref, *rest):
        dq_ref, dk_ref, dv_ref, du_ref, dsc_ref, dpw_ref = rest[n : n + 6]
        dk_acc, dv_acc, carry_s = rest[2 * n + 6 : 2 * n + 9]
        step = pl.program_id(0)

        def scatter():
            return _direct_exchange(rest[:n], rest[n + 6 : 2 * n + 6], rest[2 * n + 9 :], False)

        @pl.when(step == 0)
        def _():
            scatter()[0]()

        _pool_bwd_group(dyp_ref, pooled_ref, pw_ref, sc_ref, du_ref, dsc_ref, dpw_ref, step)

        dk_acc[...] = jnp.zeros_like(dk_acc)
        dv_acc[...] = jnp.zeros_like(dv_acc)
        lane = lax.broadcasted_iota(jnp.int32, (T, LANES), 1)
        heads = (lane < HEAD_DIM, lane >= HEAD_DIM)
        after2, before2, causal = _band_ones(T)
        zero = jnp.zeros((T, 1), F32)

        def key_tile(k0):
            kb = k_ref[pl.ds(k0, T), :]
            k_rows = jnp.concatenate([jnp.where(m, kb, jnp.zeros_like(kb)) for m in heads], axis=0)
            return kb, v_ref[pl.ds(k0, T), :], k_rows

        def grads(items):
            das = [[[_dot_nt(doh, vb) for doh in qs[1]] for _, vb, _ in tiles] for qs, _, tiles, _, _, _, _ in items]
            probs, gs = [], []
            for (_, _, tiles, masks, carry_in, gates, _), da in zip(items, das):
                probs.append([[_weights(gates[b][h][0], gates[b][h][1], carry_in[b][h], masks[b]) for h in range(2)]
                              for b in range(len(tiles))])
                gs.append([[probs[-1][b][h] * da[b][h] for h in range(2)] for b in range(len(tiles))])
            before = [[[_dot(jnp.concatenate(_split2(g), axis=1), before2) for g in row] for row in item] for item in gs]
            dzs, lefts = [], []
            for i, (_, _, tiles, masks, _, gates, st) in enumerate(items):
                g_left, item_dz = [st[0], st[1]], [None] * len(tiles)
                for b in reversed(range(len(tiles))):
                    item_dz[b] = []
                    for h in range(2):
                        sig = jnp.exp(gates[b][h][0])
                        dz = gs[i][b][h] * (1.0 - sig) - sig * (g_left[h] + before[i][b][h])
                        if masks[b] is not None:
                            dz = jnp.where(masks[b], dz, 0.0)
                        item_dz[b].append(dz.astype(BF16))
                        g_left[h] = g_left[h] + jnp.sum(gs[i][b][h], axis=1, keepdims=True)
                dzs.append(item_dz)
                lefts.append(g_left)
            out = []
            for (_, _, tiles, _, _, _, st), item_dz, g_left in zip(items, dzs, lefts):
                dq = st[2] + _dot(
                    jnp.concatenate([dz for row in item_dz for dz in row], axis=1),
                    jnp.concatenate([k_rows for _, _, k_rows in tiles], axis=0),
                )
                out.append((g_left[0], g_left[1], dq))
            for (qs, starts, tiles, _, _, _, _), item_dz, item_probs in zip(items, dzs, probs):
                for b in range(len(tiles)):
                    dk_acc[pl.ds(starts[b], T), :] += _dot_tn(jnp.concatenate(item_dz[b], axis=0), qs[2])
                    dv_acc[pl.ds(starts[b], T), :] += _dot_tn(
                        jnp.concatenate([a.astype(BF16) for a in item_probs[b]], axis=0), qs[3]
                    )
            return out

        def more(st):
            return (st[0] > 0) & _reaches(jnp.maximum(st[1], st[2]))

        def q_tiles(qis, nb):
            masks = [causal] + [None] * (nb - 1)
            prepared, jobs = [], []
            for qi in qis:
                q0 = _tile_start(qi, T)
                qb = q_ref[pl.ds(q0, T), :]
                dob = do_ref[pl.ds(q0, T), :]
                qhs = [jnp.where(m, qb, jnp.zeros_like(qb)) for m in heads]
                dohs = [jnp.where(m, dob, 0.0).astype(BF16) for m in heads]
                queries = (qhs, dohs, jnp.concatenate(qhs, axis=0), jnp.concatenate(dohs, axis=0))
                starts = [_tile_start(qi - b, T) for b in range(nb)]
                tiles = [key_tile(k0) for k0 in starts]
                prepared.append((q0, queries, starts, tiles))
                jobs += [(qh, kb, mask) for (kb, _, _), mask in zip(tiles, masks) for qh in qhs]
            flat = _log_gates_of(jobs, after2)
            items = []
            for i, (qi, (q0, queries, starts, tiles)) in enumerate(zip(qis, prepared)):
                gates = [flat[2 * nb * i + 2 * b : 2 * nb * i + 2 * b + 2] for b in range(nb)]
                carries, carry_in = [zero, zero], []
                for b in range(nb):
                    carry_in.append(list(carries))
                    carries = [carries[h] + gates[b][h][2] for h in range(2)]
                st = (zero, zero, jnp.zeros((T, LANES), F32))
                if not isinstance(qi, int):
                    k_left = qi - (nb - 1)
                    qhs = queries[0]

                    def right_to_left(st, qhs=qhs):
                        kj = st[0] - 1
                        carry_s[0, kj] = st[1]
                        carry_s[1, kj] = st[2]
                        kb = k_ref[pl.ds(pl.multiple_of(kj * T, T), T), :]
                        sums = [jnp.sum(_log1m(_dot_nt(qh, kb), None)[0], axis=1, keepdims=True) for qh in qhs]
                        return kj, st[1] + sums[0], st[2] + sums[1]

                    k_first = lax.while_loop(more, right_to_left, (k_left, carries[0], carries[1]))[0]

                    def left_to_right(kj, st, queries=queries):
                        k0 = pl.multiple_of(kj * T, T)
                        tile = key_tile(k0)
                        left_gates = _log_gates(queries[0], [tile[0]], [None], after2)
                        carry = [[carry_s[0, kj], carry_s[1, kj]]]
                        return grads([(queries, [k0], [tile], [None], carry, left_gates, st)])[0]

                    st = lax.fori_loop(k_first, k_left, left_to_right, st)
                items.append((queries, starts, tiles, masks, carry_in, gates, st))
            for (q0, _, _, _), st in zip(prepared, grads(items)):
                dq_ref[pl.ds(q0, T), :] = (st[2] * (HEAD_DIM**-0.5)).astype(BF16)

        head_tiles = min(ATTN_WINDOW - 1, nq)
        for qi in range(head_tiles):
            q_tiles([qi], qi + 1)
        assert (nq - head_tiles) % ATTN_TOGETHER == 0

        def q_loop(i, carry):
            first = head_tiles + ATTN_TOGETHER * i
            q_tiles([first + j for j in range(ATTN_TOGETHER)], ATTN_WINDOW)
            return carry

        lax.fori_loop(0, (nq - head_tiles) // ATTN_TOGETHER, q_loop, 0)
        dk_ref[...] = dk_acc[...].astype(BF16)
        dv_ref[...] = dv_acc[...].astype(BF16)

        @pl.when(step == steps - 1)
        def _():
            scatter()[1]()

    blk = pl.BlockSpec((S, LANES), lambda p: (0, p))
    sds = jax.ShapeDtypeStruct((S, D_ATTN), BF16)
    return pl.pallas_call(
        body,
        name="mixers_bwd",
        grid=(steps,),
        out_shape=[
            sds,
            sds,
            sds,
            jax.ShapeDtypeStruct((S, D_POOL), BF16),
            jax.ShapeDtypeStruct((8, D_POOL), F32),
            jax.ShapeDtypeStruct((N_GROUPS, GROUP_DIM, GROUP_DIM), BF16),
        ]
        + _exchange_shapes(partials, False),
        in_specs=[blk] * 6 + [_POOL_W_SPEC, _POOL_SCALE_SPEC] + [ANY_SPEC] * n,
        out_specs=[blk] * 4 + [pl.BlockSpec((8, GROUP_DIM), lambda p: (0, p)), _POOL_W_SPEC] + [ANY_SPEC] * n,
        scratch_shapes=[pltpu.VMEM((S, LANES), F32), pltpu.VMEM((S, LANES), F32), pltpu.VMEM((2, nq, T, 1), F32)]
        + _exchange_sems(n),
        compiler_params=pltpu.CompilerParams(
            dimension_semantics=("arbitrary",), vmem_limit_bytes=VMEM_LIMIT, has_side_effects=True
        ),
    )(q, k, v, do, dyp, pooled, pool_w, scale, *partials)


def _pool_bwd_group(dy_ref, pooled_ref, pw_ref, sc_ref, du_ref, dsc_ref, dpw_ref, g):
    S = dy_ref.shape[0]
    R = min(POOL_CHUNK, S)
    nc = S // R
    w = jnp.left_shift(jnp.int32(2), g)
    pw = pw_ref[0].astype(BF16)
    sc = sc_ref[...]
    d = lax.broadcasted_iota(jnp.int32, (R, R), 1) - lax.broadcasted_iota(jnp.int32, (R, R), 0)
    bt_cur = jnp.where((d >= 0) & (d < w), 1.0, 0.0).astype(BF16)
    dn = lax.broadcasted_iota(jnp.int32, (R, HALO), 1) + R - lax.broadcasted_iota(jnp.int32, (R, HALO), 0)
    bt_next = jnp.where(dn < w, 1.0, 0.0).astype(BF16)

    def per_count(dpl, r0):
        n = dpl.shape[0]
        count = jnp.minimum(r0 + lax.broadcasted_iota(jnp.int32, (n, 1), 0) + 1, w).astype(F32)
        return jnp.concatenate(_split2(dpl / count), axis=1)

    def chunk(r0, last, sums):
        dyv = dy_ref[pl.ds(r0, R), :]
        pooled = pooled_ref[pl.ds(r0, R), :]
        dmapped = (dyv * sc).astype(BF16)
        dsc = sums[0] + jnp.sum(dyv * _dot(pooled, pw), axis=0, keepdims=True)
        dpw = sums[1] + _dot_tn(pooled, dmapped)
        dpl = _dot_nt(dmapped, pw)
        wide = _dot(bt_cur, per_count(dpl, r0))
        if not last:
            nxt = pl.multiple_of(r0 + R, R)
            dpl_next = _dot_nt((dy_ref[pl.ds(nxt, HALO), :] * sc).astype(BF16), pw)
            wide = wide + _dot(bt_next, per_count(dpl_next, nxt))
        du_ref[pl.ds(r0, R), :] = (_lane_sum(wide, 2) - dpl).astype(BF16)
        return dsc, dpw

    sums = (jnp.zeros((1, GROUP_DIM), F32), jnp.zeros((GROUP_DIM, GROUP_DIM), F32))
    sums = lax.fori_loop(
        0, nc - 1, lambda c, s: chunk(pl.multiple_of(c * R, R), False, s), sums, unroll=POOL_UNROLL
    )
    dsc, dpw = chunk((nc - 1) * R, True, sums)
    dsc_ref[...] = jnp.zeros_like(dsc_ref)
    dsc_ref[0:1, :] = dsc
    dpw_ref[0] = dpw.astype(dpw_ref.dtype)


def _bwd_in(du, dq, dk, dv, w_in, x, dh1, g1):
    S = x.shape[0]
    tm = min(512, S)

    def body(du_ref, dq_ref, dk_ref, dv_ref, w_ref, x_ref, dh1_ref, g_ref, dx_ref, dproj_ref, dg_ref):
        @pl.when(pl.program_id(0) == 0)
        def _():
            dg_ref[...] = jnp.zeros_like(dg_ref)

        parts = (du_ref[...], dq_ref[...], dk_ref[...], dv_ref[...])
        dhn = jnp.zeros((tm, D_MODEL), F32)
        for j in range(N_DEV):
            piece = parts[j // 2][:, 256 * (j % 2) : 256 * (j % 2 + 1)]
            dproj_ref[:, 256 * j : 256 * (j + 1)] = piece
            dhn = dhn + _dot_nt(piece, w_ref[j])
        xv = x_ref[...]
        r = _rstd(xv)
        dxn, dg = _rms_bwd(dhn, xv * r, r, g_ref[...])
        dx_ref[...] = dh1_ref[...] + dxn
        dg_ref[0:1, :] += dg

    half = pl.BlockSpec((tm, D_POOL), lambda i: (i, 0))
    full = pl.BlockSpec((tm, D_MODEL), lambda i: (i, 0))
    return pl.pallas_call(
        body,
        name="bwd_in",
        grid=(S // tm,),
        out_shape=[
            jax.ShapeDtypeStruct((S, D_MODEL), F32),
            jax.ShapeDtypeStruct((S, D_IN_PROJ), BF16),
            jax.ShapeDtypeStruct((8, D_MODEL), F32),
        ],
        in_specs=[half, half, half, half,
                  pl.BlockSpec((N_DEV, D_MODEL, 256), lambda i: (0, 0, 0)),
                  full, full, pl.BlockSpec((1, D_MODEL), lambda i: (0, 0))],
        out_specs=[full, pl.BlockSpec((tm, D_IN_PROJ), lambda i: (i, 0)),
                   pl.BlockSpec((8, D_MODEL), lambda i: (0, 0))],
        compiler_params=_params(("arbitrary",)),
    )(du, dq, dk, dv, w_in, x, dh1, g1)


def _rows(a):
    a = a.reshape(-1, LANES)
    pad = (-a.shape[0]) % 8
    return jnp.pad(a, ((0, pad), (0, 0))) if pad else a


def kernel(x, norm1_g, w_in, pool_w, pool_scale, pool_out_g, attn_out_g, w_out, norm2_g, w_up, w_down, final_g, loss_target, m_norm1_g, m_w_in, m_pool_w, m_pool_scale, m_pool_out_g, m_attn_out_g, m_w_out, m_norm2_g, m_w_up, m_w_down, m_final_g, v_norm1_g, v_w_in, v_pool_w, v_pool_scale, v_pool_out_g, v_attn_out_g, v_w_out, v_norm2_g, v_w_up, v_w_down, v_final_g):
    S = x.shape[1]
    xs = x.reshape(S, D_MODEL)
    tgt = loss_target.reshape(S, D_MODEL)
    row = lambda a: a.reshape(1, -1)

    (w_in_g,) = _exchange([w_in.astype(BF16)], True, "gather_w_in")
    lanes = lambda a: a.reshape(-1, LANES)
    vector_sets = [
        [lanes(a) for a in (final_g, norm2_g, pool_out_g, attn_out_g, norm1_g, pool_scale)],
        [lanes(a) for a in (m_final_g, m_norm2_g, m_pool_out_g, m_attn_out_g, m_norm1_g, m_pool_scale)],
        [lanes(a) for a in (v_final_g, v_norm2_g, v_pool_out_g, v_attn_out_g, v_norm1_g, v_pool_scale)],
    ]
    hn, u_pool, q, k, v, *prepared = _fwd_in(xs, row(norm1_g), w_in_g, [w_out, w_up, w_down], vector_sets)
    shards, packed_vectors = prepared[:3], prepared[3:]
    y_attn, pooled, y_pool, w_out_g, w_up_g, w_down_g = _mixers_fwd(q, k, v, u_pool, pool_w, row(pool_scale), shards)
    w_out_full = w_out_g.reshape(D_MODEL, D_MODEL)
    w_down_full = w_down_g.reshape(D_FF, D_MODEL)
    mixed, hn2, act, dup, dh2b, dh1, dh1b, dyp, dya, sg = _mlp_fwd_bwd(
        xs, y_pool, y_attn, tgt, row(pool_out_g), row(attn_out_g), row(norm2_g), row(final_g),
        w_out_full, w_up_g, w_down_full,
    )
    gp_out = _wgrad(mixed, dh1b, True, 1, "wgrad_out")
    gp_down, land_out = _wgrad(act, dh2b, True, 2, "wgrad_down", [gp_out])
    gp_up = _wgrad(hn2, dup, False, 2, "wgrad_up")
    dq, dk, dv, du, dsc, dpw, land_up, land_down = _mixers_bwd(
        q, k, v, dya, dyp, pooled, pool_w, row(pool_scale), [gp_up, gp_down]
    )
    dx, dproj, dg1 = _bwd_in(du, dq, dk, dv, w_in_g, xs, dh1, row(norm1_g))

    partial_vectors = jnp.concatenate([_rows(p) for p in (sg[0], sg[1], sg[2], dg1[0], dsc[0], sg[3])], axis=0)
    flat_pool = lambda a: a.reshape(N_GROUPS * GROUP_DIM, GROUP_DIM)
    gp_in, landed_vectors, landed_pool = _wgrad(
        hn, dproj, False, 1, "wgrad_in", [partial_vectors, flat_pool(dpw)], gather=True
    )
    smalls = [
        (landed_vectors, [(*packed_vectors, None)]),
        (landed_pool, [(flat_pool(pool_w), flat_pool(m_pool_w), flat_pool(v_pool_w), None)]),
    ]
    tail = _reduce_adam_tail(
        [land_out, land_up, land_down], [w_out, w_up, w_down], [m_w_out, m_w_up, m_w_down],
        [v_w_out, v_w_up, v_w_down], gp_in, (w_in, m_w_in, v_w_in), smalls,
    )
    big = {name: tail[4 * t : 4 * t + 4] for t, name in enumerate(("w_out", "w_up", "w_down", "w_in"))}

    def unpack(vec, pw):
        out = {}
        for i, name in enumerate(("final_g", "norm2_g", "mix_g", "norm1_g", "pool_scale")):
            out[name] = vec[8 * i : 8 * i + 8].reshape(-1)
        out["pool_scale"] = out["pool_scale"][:D_POOL]
        out["pool_out_g"], out["attn_out_g"] = out["mix_g"][:D_POOL], out["mix_g"][D_POOL:]
        out["pool_w"] = pw.reshape(N_GROUPS, GROUP_DIM, GROUP_DIM)
        return out, vec[40, 0]

    small_out = [unpack(tail[16 + i], tail[20 + i]) for i in range(4)]
    loss = small_out[0][1]
    order = ("norm1_g", "w_in", "pool_w", "pool_scale", "pool_out_g", "attn_out_g", "w_out", "norm2_g", "w_up",
             "w_down", "final_g")
    outs = [loss, dx.reshape(1, S, D_MODEL)]
    for i in range(4):
        for name in order:
            outs.append(big[name][i] if name in big else small_out[i][0][name])
    return tuple(outs)
```

```python
import jax
import jax.numpy as jnp
from jax import lax
from jax.experimental import pallas as pl
from jax.experimental.pallas import tpu as pltpu

F32 = jnp.float32
BF16 = jnp.bfloat16
MESH = pl.DeviceIdType.MESH

N_DEV = 8
D_MODEL = 1024
D_POOL = 512
D_ATTN = 512
N_GROUPS = 4
GROUP_DIM = 128
HEAD_DIM = 64
D_FF = 4096
D_IN_PROJ = 2048
EPS = 1e-6
HALO = 16
ATTN_TILE = 128
LANES = 128
EXP_UNDERFLOW = -104.0

ADAM_LR = 0.001
ADAM_B1 = 0.9
ADAM_B2 = 0.999
ADAM_EPS = 1e-08
ADAM_WD = 0.01
ADAM_STEP = 10

VMEM_LIMIT = 56 * 1024 * 1024


def _params(semantics=None, vmem=VMEM_LIMIT):
    return pltpu.CompilerParams(dimension_semantics=semantics, vmem_limit_bytes=vmem)


def _dot(a, b):
    return jnp.dot(a, b, preferred_element_type=F32)


def _dot_nt(a, b):
    return lax.dot_general(a, b, (((1,), (1,)), ((), ())), preferred_element_type=F32)


def _dot_tn(a, b):
    return lax.dot_general(a, b, (((0,), (0,)), ((), ())), preferred_element_type=F32)


def _split2(x):
    hi = x.astype(BF16)
    lo = (x - hi.astype(F32)).astype(BF16)
    return hi, lo


def _split3(x):
    hi = x.astype(BF16)
    r = x - hi.astype(F32)
    mid = r.astype(BF16)
    lo = (r - mid.astype(F32)).astype(BF16)
    return hi, mid, lo


def _rstd(h):
    return lax.rsqrt(jnp.mean(h * h, axis=-1, keepdims=True) + EPS)


def _rms_bwd(dout, hhat, r, g):
    dg = jnp.sum(dout * hhat, axis=0, keepdims=True)
    dxh = dout * g
    dh = r * (dxh - hhat * jnp.mean(dxh * hhat, axis=-1, keepdims=True))
    return dh, dg


def _my_index():
    return 4 * lax.axis_index("x") + 2 * lax.axis_index("y") + lax.axis_index("c")


def _peer(k):
    x, y, c = lax.axis_index("x"), lax.axis_index("y"), lax.axis_index("c")
    px = 1 - x if (k >> 2) & 1 else x
    py = 1 - y if (k >> 1) & 1 else y
    pc = 1 - c if k & 1 else c
    return (px, py, pc), 4 * px + 2 * py + pc


N_PEERS = N_DEV - 1
ANY_SPEC = pl.BlockSpec(memory_space=pl.ANY)


def _exchange_sems(n):
    return [
        pltpu.SemaphoreType.DMA((n * N_PEERS,)),
        pltpu.SemaphoreType.DMA((n * N_PEERS,)),
        pltpu.SemaphoreType.DMA((n,)),
    ]


ALL_RELATIONS = tuple(range(N_DEV))


def _exchange_shapes(blocks, gather, relations=None):
    if gather:
        return [jax.ShapeDtypeStruct((N_DEV,) + b.shape, b.dtype) for b in blocks]
    slots = N_DEV if relations is None else len(relations)
    return [jax.ShapeDtypeStruct((slots,) + b.shape[1:], b.dtype) for b in blocks]


def _direct_exchange(ins, outs, sems, gather, relations=None):
    send_sems, recv_sems, local_sems = sems
    me = _my_index()
    own, sends, recvs = [], [], []
    for t in range(len(ins)):
        for slot, k in enumerate(ALL_RELATIONS if relations is None else relations):
            if k == 0:
                mine = outs[t].at[me if relations is None else slot]
                own.append(pltpu.make_async_copy(ins[t] if gather else ins[t].at[me], mine, local_sems.at[t]))
                continue
            peer, peer_idx = _peer(k)
            src = ins[t] if gather else ins[t].at[peer_idx]
            sent_to = outs[t].at[me if relations is None else slot]
            lands_in = outs[t].at[peer_idx if relations is None else slot]
            for dst, bucket in ((sent_to, sends), (lands_in, recvs)):
                bucket.append(
                    pltpu.make_async_remote_copy(
                        src_ref=src,
                        dst_ref=dst,
                        send_sem=send_sems.at[t * N_PEERS + k - 1],
                        recv_sem=recv_sems.at[t * N_PEERS + k - 1],
                        device_id=peer,
                        device_id_type=MESH,
                    )
                )

    def start():
        for cp in own + sends:
            cp.start()

    def finish():
        for cp in recvs:
            cp.wait_recv()
        for cp in sends:
            cp.wait_send()
        for cp in own:
            cp.wait()

    return start, finish


def _flip(a, bit):
    return a + bit - 2 * a * bit


def _two_level_gather(ins, outs, sems):
    send_sems, recv_sems, local_sems = sems
    x, y, c = lax.axis_index("x"), lax.axis_index("y"), lax.axis_index("c")
    me, sibling = (x, y, c), (x, y, 1 - c)
    x_nbr, y_nbr, diagonal = (1 - x, y, c), (x, 1 - y, c), (1 - x, 1 - y, c)
    relay_of = (_flip(x, 1 - c), _flip(y, c), c)
    relay_to = (_flip(x, c), _flip(y, 1 - c), c)

    def copy(t, k, block, to, from_input=False):
        slot = outs[t].at[4 * block[0] + 2 * block[1] + block[2]]
        return pltpu.make_async_remote_copy(
            src_ref=ins[t] if from_input else slot,
            dst_ref=slot,
            send_sem=send_sems.at[t * N_PEERS + k],
            recv_sem=recv_sems.at[t * N_PEERS + k],
            device_id=to,
            device_id_type=MESH,
        )

    arrays = range(len(ins))
    own = [pltpu.make_async_copy(ins[t], outs[t].at[4 * x + 2 * y + c], local_sems.at[t]) for t in arrays]
    first = [copy(t, k, me, to, True) for t in arrays for k, to in ((1, x_nbr), (2, y_nbr), (0, sibling))]
    landed = [copy(t, k, block, me) for t in arrays for k, block in ((1, x_nbr), (2, y_nbr))]
    relays = [copy(t, 3, relay_of, relay_to) for t in arrays]
    passed = [copy(t, 3 + k, block, sibling) for t in arrays for k, block in ((1, x_nbr), (2, y_nbr))]
    relayed = [(copy(t, 3, diagonal, me), copy(t, 6, diagonal, sibling)) for t in arrays]
    last = [copy(t, 0, sibling, me) for t in arrays]
    last += [copy(t, 3 + k, (*block[:2], 1 - c), me) for t in arrays for k, block in ((1, x_nbr), (2, y_nbr), (3, diagonal))]

    def start():
        for cp in own + first:
            cp.start()

    def relay():
        for cp in landed:
            cp.wait_recv()
        for cp in relays + passed:
            cp.start()

    def forward():
        for arrived, onward in relayed:
            arrived.wait_recv()
            onward.start()

    def finish():
        for cp in last:
            cp.wait_recv()
        for cp in first + relays + passed + [onward for _, onward in relayed]:
            cp.wait_send()
        for cp in own:
            cp.wait()

    return start, relay, forward, finish


N_CHIPS = 4
PAIR_SEMS = (N_CHIPS, N_CHIPS, N_CHIPS - 1, N_CHIPS - 1)


def _pair_reduce_scatter(gp_hbm, own, pair, summed, land, local_sem, sems):
    d2d_send, d2d_recv, ici_send, ici_recv = sems
    x, y, c = lax.axis_index("x"), lax.axis_index("y"), lax.axis_index("c")
    my_chip = 2 * x + y
    sibling = (x, y, 1 - c)
    chips = [(1 - x, y), (x, 1 - y), (1 - x, 1 - y)]
    local = [pltpu.make_async_copy(gp_hbm.at[2 * j + c], own.at[j], local_sem.at[0]) for j in range(N_CHIPS)]
    to_sibling = [
        pltpu.make_async_remote_copy(
            src_ref=gp_hbm.at[2 * j + 1 - c], dst_ref=pair.at[j], send_sem=d2d_send.at[j], recv_sem=d2d_recv.at[j],
            device_id=sibling, device_id_type=MESH,
        )
        for j in range(N_CHIPS)
    ]
    to_chips, from_chips = [], []
    for r, (px, py) in enumerate(chips):
        for dst, bucket in ((land.at[my_chip], to_chips), (land.at[2 * px + py], from_chips)):
            bucket.append(
                pltpu.make_async_remote_copy(
                    src_ref=summed.at[2 * px + py], dst_ref=dst, send_sem=ici_send.at[r], recv_sem=ici_recv.at[r],
                    device_id=(px, py, c), device_id_type=MESH,
                )
            )

    def start():
        for cp in local + to_sibling:
            cp.start()

    def middle():
        for cp in to_sibling:
            cp.wait_recv()
        pltpu.make_async_copy(gp_hbm.at[pl.ds(0, N_CHIPS)], own, local_sem.at[0]).wait()
        for j in range(N_CHIPS):
            summed[j] = (own[j].astype(F32) + pair[j].astype(F32)).astype(BF16)
        for cp in to_chips:
            cp.start()

    def finish():
        for cp in from_chips:
            cp.wait_recv()
        for cp in to_chips + to_sibling:
            cp.wait_send()
        land[my_chip] = summed[my_chip]

    return start, middle, finish


def _exchange(blocks, gather, name):
    n = len(blocks)

    def body(*refs):
        if gather:
            stages = _two_level_gather(refs[:n], refs[n : 2 * n], refs[2 * n :])
        else:
            stages = _direct_exchange(refs[:n], refs[n : 2 * n], refs[2 * n :], False)
        for stage in stages:
            stage()

    return pl.pallas_call(
        body,
        name=name,
        out_shape=_exchange_shapes(blocks, gather),
        in_specs=[ANY_SPEC] * n,
        out_specs=[ANY_SPEC] * n,
        scratch_shapes=_exchange_sems(n),
        compiler_params=pltpu.CompilerParams(has_side_effects=True),
    )(*blocks)


def _adam(w, g, m, v):
    m2 = ADAM_B1 * m + (1.0 - ADAM_B1) * g
    v2 = ADAM_B2 * v + (1.0 - ADAM_B2) * jnp.square(g)
    m_hat = m2 / (1.0 - ADAM_B1**ADAM_STEP)
    v_hat = v2 / (1.0 - ADAM_B2**ADAM_STEP)
    delta = -ADAM_LR * (m_hat / (jnp.sqrt(v_hat) + ADAM_EPS) + ADAM_WD * w)
    return delta, m2, v2


TAIL_STEPS = 8


def _reduce_adam_tail(lands, ws, ms, vs, gp_last, last, smalls):
    nw, ns = len(ws), len(smalls)
    n_lands = sum(len(arrays) for arrays in lands)
    tiles = [w.shape[0] // TAIL_STEPS for w in ws]
    blk = gp_last.shape[1:]
    entries = [(i, e) for i, (_, es) in enumerate(smalls) for e in es]
    n_params = 3 * sum(e[0] is not None for _, e in entries)
    n_small_out = sum(4 if e[0] is not None else 1 for _, e in entries)

    def body(*refs):
        gp_hbm = refs[0]
        small_land = refs[1 : 1 + ns]
        land_refs = list(refs[1 + ns : 1 + ns + n_lands])
        refs = refs[1 + ns + n_lands - nw :]
        w_refs, m_refs, v_refs = (refs[nw * (i + 1) : nw * (i + 2)] for i in range(3))
        last_refs = refs[4 * nw : 4 * nw + 3]
        refs = refs[4 * nw + 3 :]
        param_refs = refs[:n_params]
        outs = refs[n_params:]
        big_out = outs[: 4 * nw]
        last_out = outs[4 * nw : 4 * nw + 4]
        small_out = outs[4 * nw + 4 : 4 * nw + 4 + n_small_out]
        scratch = outs[4 * nw + 4 + n_small_out :]
        pair_bufs = scratch[:4]
        local_sem = scratch[4]
        pair_sems = scratch[5:9]
        step = pl.program_id(0)

        def pair():
            return _pair_reduce_scatter(gp_hbm, *pair_bufs, local_sem, pair_sems)

        @pl.when(step == 0)
        def _():
            pair()[0]()

        @pl.when(step == 1)
        def _():
            pair()[1]()

        for t in range(nw):
            g = None
            for _ in lands[t]:
                landed = land_refs.pop(0)
                for s in range(landed.shape[0]):
                    part = landed[s].astype(F32)
                    g = part if g is None else g + part
            for ref, val in zip(big_out[4 * t : 4 * t + 4], (g,) + _adam(w_refs[t][...], g, m_refs[t][...], v_refs[t][...])):
                ref[...] = val

        @pl.when(step == TAIL_STEPS - 1)
        def _():
            pair()[2]()
            land = pair_bufs[3]
            g = land[0].astype(F32)
            for chip in range(1, N_CHIPS):
                g = g + land[chip].astype(F32)
            for ref, val in zip(last_out, (g,) + _adam(last_refs[0][...], g, last_refs[1][...], last_refs[2][...])):
                ref[...] = val
            sums = []
            for i in range(ns):
                g = small_land[i][0].astype(F32)
                for s in range(1, N_DEV):
                    g = g + small_land[i][s].astype(F32)
                sums.append(g)
            params, results = list(param_refs), list(small_out)
            for i, (w, _, _, where) in entries:
                g = sums[i]
                if where is not None:
                    shape = w.shape if w is not None else where[2:]
                    g = g[where[0] : where[0] + shape[0], where[1] : where[1] + shape[1]]
                if w is None:
                    results.pop(0)[...] = g
                    continue
                w_ref, m_ref, v_ref = params[:3]
                del params[:3]
                for val in (g,) + _adam(w_ref[...], g, m_ref[...], v_ref[...]):
                    results.pop(0)[...] = val

    def tile(t):
        return pl.BlockSpec((tiles[t], ws[t].shape[1]), lambda i: (i, 0))

    def land_tile(t, landed):
        return pl.BlockSpec((landed.shape[0], tiles[t], ws[t].shape[1]), lambda i: (0, i, 0))

    def whole(shape):
        return pl.BlockSpec(shape, lambda step: (0, 0))

    big_sds = [jax.ShapeDtypeStruct(ws[t].shape, F32) for t in range(nw) for _ in range(4)]
    params, small_shapes = [], []
    for _, (w, m, v, where) in entries:
        if w is None:
            small_shapes.append(tuple(where[2:]))
        else:
            params += [w, m, v]
            small_shapes += [w.shape] * 4
    return pl.pallas_call(
        body,
        name="reduce_adam_tail",
        grid=(TAIL_STEPS,),
        out_shape=big_sds
        + [jax.ShapeDtypeStruct(blk, F32)] * 4
        + [jax.ShapeDtypeStruct(shape, F32) for shape in small_shapes],
        in_specs=[ANY_SPEC]
        + [pl.BlockSpec(smalls[i][0].shape, lambda step: (0, 0, 0)) for i in range(ns)]
        + [land_tile(t, landed) for t in range(nw) for landed in lands[t]]
        + [tile(t) for _ in range(3) for t in range(nw)]
        + [whole(blk)] * 3
        + [whole(p.shape) for p in params],
        out_specs=[tile(t) for t in range(nw) for _ in range(4)]
        + [whole(blk)] * 4
        + [whole(shape) for shape in small_shapes],
        scratch_shapes=[pltpu.VMEM((N_CHIPS,) + blk, BF16)] * 4
        + [pltpu.SemaphoreType.DMA((1,))]
        + [pltpu.SemaphoreType.DMA((count,)) for count in PAIR_SEMS],
        compiler_params=pltpu.CompilerParams(
            dimension_semantics=("arbitrary",), vmem_limit_bytes=VMEM_LIMIT, has_side_effects=True
        ),
    )(gp_last, *[s[0] for s in smalls], *[a for arrays in lands for a in arrays], *ws, *ms, *vs, *last, *params)


VECTOR_ROWS = 48


def _fwd_in(x, g1, w_in, shards, vector_sets):
    S = x.shape[0]
    tm = min(512, S)
    steps = S // tm
    n = len(shards)
    nv = sum(len(vs) for vs in vector_sets)

    def body(x_ref, g_ref, w_ref, *rest):
        hn_ref, u_ref, q_ref, k_ref, v_ref = rest[n + nv : n + nv + 5]
        for src, dst in zip(rest[:n], rest[n + nv + 5 :]):
            dst[...] = src[...].astype(BF16)
        vectors = list(rest[n : n + nv])
        for vs, packed in zip(vector_sets, rest[2 * n + nv + 5 :]):
            at = 0
            for _ in vs:
                ref = vectors.pop(0)
                packed[at : at + ref.shape[0], :] = ref[...]
                at += ref.shape[0]
            packed[at:, :] = jnp.zeros((VECTOR_ROWS - at, LANES), F32)
        xv = x_ref[...]
        hn = (xv * _rstd(xv) * g_ref[...]).astype(BF16)
        hn_ref[...] = hn
        outs = (u_ref, q_ref, k_ref, v_ref)
        for j in range(N_DEV):
            p = _dot(hn, w_ref[j])
            cols = slice(256 * (j % 2), 256 * (j % 2 + 1))
            if j // 2 == 0:
                u_ref[:, cols] = p
            elif j // 2 == 1:
                q_ref[:, cols] = (p * (HEAD_DIM**-0.5)).astype(BF16)
            else:
                outs[j // 2][:, cols] = p.astype(BF16)

    half = pl.BlockSpec((tm, D_POOL), lambda i: (i, 0))
    shard_tiles = [pl.BlockSpec((s.shape[0] // steps, s.shape[1]), lambda i: (i, 0)) for s in shards]
    return pl.pallas_call(
        body,
        name="fwd_in",
        grid=(steps,),
        out_shape=[
            jax.ShapeDtypeStruct((S, D_MODEL), BF16),
            jax.ShapeDtypeStruct((S, D_POOL), F32),
            jax.ShapeDtypeStruct((S, D_ATTN), BF16),
            jax.ShapeDtypeStruct((S, D_ATTN), BF16),
            jax.ShapeDtypeStruct((S, D_ATTN), BF16),
        ]
        + [jax.ShapeDtypeStruct(s.shape, BF16) for s in shards]
        + [jax.ShapeDtypeStruct((VECTOR_ROWS, LANES), F32)] * len(vector_sets),
        in_specs=[
            pl.BlockSpec((tm, D_MODEL), lambda i: (i, 0)),
            pl.BlockSpec((1, D_MODEL), lambda i: (0, 0)),
            pl.BlockSpec((N_DEV, D_MODEL, 256), lambda i: (0, 0, 0)),
        ]
        + shard_tiles
        + [pl.BlockSpec(v.shape, lambda i: (0, 0)) for vs in vector_sets for v in vs],
        out_specs=[pl.BlockSpec((tm, D_MODEL), lambda i: (i, 0)), half, half, half, half]
        + shard_tiles
        + [pl.BlockSpec((VECTOR_ROWS, LANES), lambda i: (0, 0))] * len(vector_sets),
        compiler_params=_params(("arbitrary",)),
    )(x, g1, w_in, *shards, *[v for vs in vector_sets for v in vs])


POOL_CHUNK = 256
POOL_UNROLL = 5
_POOL_W_SPEC = pl.BlockSpec((1, GROUP_DIM, GROUP_DIM), lambda g: (g, 0, 0))
_POOL_SCALE_SPEC = pl.BlockSpec((1, GROUP_DIM), lambda g: (0, g))


def _lane_sum(wide, n):
    out = wide[:, :GROUP_DIM]
    for i in range(1, n):
        out = out + wide[:, GROUP_DIM * i : GROUP_DIM * (i + 1)]
    return out


def _pool_fwd_group(u_ref, pw_ref, sc_ref, pooled_ref, y_ref, g):
    S = u_ref.shape[0]
    R = min(POOL_CHUNK, S)
    w = jnp.left_shift(jnp.int32(2), g)
    d = lax.broadcasted_iota(jnp.int32, (R, R), 0) - lax.broadcasted_iota(jnp.int32, (R, R), 1)
    b_cur = jnp.where((d >= 0) & (d < w), 1.0, 0.0).astype(BF16)
    dp = lax.broadcasted_iota(jnp.int32, (R, HALO), 0) + HALO - lax.broadcasted_iota(jnp.int32, (R, HALO), 1)
    b_prev = jnp.where(dp < w, 1.0, 0.0).astype(BF16)
    pw = pw_ref[0].astype(BF16)
    sc = sc_ref[...]

    def chunk(r0, first):
        cur = u_ref[pl.ds(r0, R), :]
        wide = _dot(b_cur, jnp.concatenate(_split3(cur), axis=1))
        if not first:
            prev = u_ref[pl.ds(pl.multiple_of(r0 - HALO, HALO), HALO), :]
            wide = wide + _dot(b_prev, jnp.concatenate(_split3(prev), axis=1))
        count = jnp.minimum(r0 + lax.broadcasted_iota(jnp.int32, (R, 1), 0) + 1, w).astype(F32)
        pooled = (_lane_sum(wide, 3) / count - cur).astype(BF16)
        pooled_ref[pl.ds(r0, R), :] = pooled
        y_ref[pl.ds(r0, R), :] = _dot(pooled, pw) * sc

    chunk(0, True)

    def rest(c, carry):
        chunk(pl.multiple_of(c * R, R), False)
        return carry

    lax.fori_loop(1, S // R, rest, 0, unroll=POOL_UNROLL)


ATTN_WINDOW = 3
ATTN_TOGETHER = 2
ATTN_TOGETHER_FWD = 3


def _band_ones(T):
    row = lax.broadcasted_iota(jnp.int32, (T, T), 0)
    col = lax.broadcasted_iota(jnp.int32, (T, T), 1)
    after = jnp.where(row > col, 1.0, 0.0).astype(BF16)
    before = jnp.where(row < col, 1.0, 0.0).astype(BF16)
    return jnp.concatenate([after, after], axis=0), jnp.concatenate([before, before], axis=0), col < row


def _log1m(z, mask):
    sp = jnp.log(1.0 + jnp.exp(-jnp.abs(z)))
    l1m = -jnp.maximum(z, 0.0) - sp
    return (l1m if mask is None else jnp.where(mask, l1m, 0.0)), sp


def _log_gates_of(jobs, after2):
    zs = [_dot_nt(qh, kb) for qh, kb, _ in jobs]
    terms = []
    for z, (_, _, mask) in zip(zs, jobs):
        l1m, sp = _log1m(z, mask)
        terms.append(
            (jnp.minimum(z, 0.0) - sp, jnp.concatenate(_split2(l1m), axis=1), jnp.sum(l1m, axis=1, keepdims=True))
        )
    return [(ls, _dot(split, after2), total) for ls, split, total in terms]


def _log_gates(qhs, kbs, masks, after2):
    flat = _log_gates_of([(qh, kb, mask) for kb, mask in zip(kbs, masks) for qh in qhs], after2)
    return [flat[len(qhs) * b : len(qhs) * (b + 1)] for b in range(len(kbs))]


def _tile_start(index, T):
    return index * T if isinstance(index, int) else pl.multiple_of(index * T, T)


def _weights(ls, tail, carry, mask):
    a = jnp.exp(ls + tail + carry)
    return a if mask is None else jnp.where(mask, a, 0.0)


def _reaches(carry):
    return jnp.max(carry) > EXP_UNDERFLOW


def _mixers_fwd(q, k, v, u, pool_w, scale, shards):
    S = q.shape[0]
    T = ATTN_TILE
    nq = S // T
    n = len(shards)
    steps = D_ATTN // LANES

    def body(q_ref, k_ref, v_ref, u_ref, pw_ref, sc_ref, *rest):
        o_ref, pooled_ref, yp_ref = rest[n : n + 3]
        step = pl.program_id(0)

        def gather():
            return _two_level_gather(rest[:n], rest[n + 3 : 2 * n + 3], rest[2 * n + 3 :])

        @pl.when(step == 0)
        def _():
            gather()[0]()

        @pl.when(step == steps // 2)
        def _():
            gather()[1]()

        _pool_fwd_group(u_ref, pw_ref, sc_ref, pooled_ref, yp_ref, step)

        lane = lax.broadcasted_iota(jnp.int32, (T, LANES), 1)
        heads = (lane < HEAD_DIM, lane >= HEAD_DIM)
        after2, _, causal = _band_ones(T)
        zero = jnp.zeros((T, 1), F32)

        def key_tile(k0):
            vb = v_ref[pl.ds(k0, T), :]
            return k_ref[pl.ds(k0, T), :], jnp.concatenate([jnp.where(m, vb, jnp.zeros_like(vb)) for m in heads], axis=0)

        def q_tiles(qis, nb):
            queries, jobs = [], []
            for qi in qis:
                q0 = _tile_start(qi, T)
                qb = q_ref[pl.ds(q0, T), :]
                qhs = [jnp.where(m, qb, jnp.zeros_like(qb)) for m in heads]
                tiles = [key_tile(_tile_start(qi - b, T)) for b in range(nb)]
                queries.append((q0, qhs, [v_rows for _, v_rows in tiles]))
                jobs += [(qh, kb, causal if b == 0 else None) for b, (kb, _) in enumerate(tiles) for qh in qhs]
            gates = _log_gates_of(jobs, after2)
            states = []
            for i, (q0, qhs, values) in enumerate(queries):
                carries, probs = [zero, zero], []
                for b in range(nb):
                    for h in range(2):
                        ls, tail, total = gates[2 * nb * i + 2 * b + h]
                        probs.append(_weights(ls, tail, carries[h], causal if b == 0 else None).astype(BF16))
                        carries[h] = carries[h] + total
                states.append((carries, _dot(jnp.concatenate(probs, axis=1), jnp.concatenate(values, axis=0))))

            def more(st):
                return (st[0] > 0) & _reaches(jnp.maximum(st[1], st[2]))

            for qi, (q0, qhs, _), (carries, acc) in zip(qis, queries, states):

                def k_step(st, qhs=qhs):
                    kj = st[0] - 1
                    kb, v_rows = key_tile(pl.multiple_of(kj * T, T))
                    new, probs = [], []
                    for (ls, tail, total), carry in zip(_log_gates(qhs, [kb], [None], after2)[0], st[1:3]):
                        probs.append(_weights(ls, tail, carry, None).astype(BF16))
                        new.append(carry + total)
                    return kj, new[0], new[1], st[3] + _dot(jnp.concatenate(probs, axis=1), v_rows)

                if not isinstance(qi, int):
                    acc = lax.while_loop(more, k_step, (qi - (nb - 1), carries[0], carries[1], acc))[3]
                o_ref[pl.ds(q0, T), :] = acc

        head_tiles = min(ATTN_WINDOW - 1, nq)
        for qi in range(head_tiles):
            q_tiles([qi], qi + 1)
        assert (nq - head_tiles) % ATTN_TOGETHER_FWD == 0

        def q_loop(i, carry):
            first = head_tiles + ATTN_TOGETHER_FWD * i
            q_tiles([first + j for j in range(ATTN_TOGETHER_FWD)], ATTN_WINDOW)
            return carry

        lax.fori_loop(0, (nq - head_tiles) // ATTN_TOGETHER_FWD, q_loop, 0)

        @pl.when(step == steps - 1)
        def _():
            _, _, forward, finish = gather()
            forward()
            finish()

    blk = pl.BlockSpec((S, LANES), lambda p: (0, p))
    return pl.pallas_call(
        body,
        name="mixers_fwd",
        grid=(steps,),
        out_shape=[
            jax.ShapeDtypeStruct((S, D_ATTN), F32),
            jax.ShapeDtypeStruct((S, D_POOL), BF16),
            jax.ShapeDtypeStruct((S, D_POOL), F32),
        ]
        + _exchange_shapes(shards, True),
        in_specs=[blk, blk, blk, blk, _POOL_W_SPEC, _POOL_SCALE_SPEC] + [ANY_SPEC] * n,
        out_specs=[blk, blk, blk] + [ANY_SPEC] * n,
        scratch_shapes=_exchange_sems(n),
        compiler_params=pltpu.CompilerParams(
            dimension_semantics=("arbitrary",), vmem_limit_bytes=VMEM_LIMIT, has_side_effects=True
        ),
    )(q, k, v, u, pool_w, scale, *shards)


MLP_AHEAD = 1
SG_ROWS = 8


def _mlp_fwd_bwd(x, y_pool, y_attn, target, g_pool, g_attn, g2, gf, w_out, w_up, w_down):
    S = x.shape[0]
    tm = min(256, S)
    fc = D_FF // N_DEV

    def body(x_ref, yp_ref, ya_ref, t_ref, gp_ref, ga_ref, g2_ref, gf_ref, wo_hbm, wu_hbm, wd_hbm,
             mixed_ref, hn2_ref, act_ref, dup_ref, dh2_ref, dh1_ref, dh1b_ref, dyp_ref, dya_ref, sg_ref,
             wo, wu, wd, up_s, sems):
        @pl.when(pl.program_id(0) == 0)
        def _():
            copies = [
                pltpu.make_async_copy(wo_hbm, wo, sems.at[0]),
                pltpu.make_async_copy(wu_hbm, wu, sems.at[1]),
                pltpu.make_async_copy(wd_hbm, wd, sems.at[2]),
            ]
            for cp in copies:
                cp.start()
            for cp in copies:
                cp.wait()
            sg_ref[...] = jnp.zeros_like(sg_ref)

        gp, ga, g2v, gfv = gp_ref[...], ga_ref[...], g2_ref[...], gf_ref[...]
        yp, ya = yp_ref[...], ya_ref[...]
        rp, ra = _rstd(yp), _rstd(ya)
        yph, yah = yp * rp, ya * ra
        mixed = jnp.concatenate([(yph * gp).astype(BF16), (yah * ga).astype(BF16)], axis=1)
        mixed_ref[...] = mixed
        h1 = x_ref[...] + _dot(mixed, wo[...])
        r2 = _rstd(h1)
        h1h = h1 * r2
        hn2 = (h1h * g2v).astype(BF16)
        hn2_ref[...] = hn2
        h2 = h1
        ups = [_dot(hn2, wu[j]) for j in range(MLP_AHEAD)]
        for j in range(N_DEV):
            cols = slice(fc * j, fc * (j + 1))
            if j + MLP_AHEAD < N_DEV:
                ups.append(_dot(hn2, wu[j + MLP_AHEAD]))
            up = ups.pop(0)
            up_s[:, cols] = up
            act = jnp.square(jnp.maximum(up, 0.0)).astype(BF16)
            act_ref[:, cols] = act
            h2 = h2 + _dot(act, wd[cols, :])
        rf = _rstd(h2)
        h2h = h2 * rf
        diff = h2h * gfv - t_ref[...]
        loss_rows = 0.5 * jnp.mean(diff * diff, axis=-1, keepdims=True)
        dy = diff * (1.0 / D_MODEL)
        dh2, dgf = _rms_bwd(dy, h2h, rf, gfv)
        dh2b = dh2.astype(BF16)
        dh2_ref[...] = dh2b
        dhn2 = jnp.zeros((tm, D_MODEL), F32)
        dacts = [_dot_nt(dh2b, wd[fc * j : fc * (j + 1), :]) for j in range(MLP_AHEAD)]
        for j in range(N_DEV):
            cols = slice(fc * j, fc * (j + 1))
            if j + MLP_AHEAD < N_DEV:
                dacts.append(_dot_nt(dh2b, wd[fc * (j + MLP_AHEAD) : fc * (j + MLP_AHEAD + 1), :]))
            dup = (dacts.pop(0) * (2.0 * jnp.maximum(up_s[:, cols], 0.0))).astype(BF16)
            dup_ref[:, cols] = dup
            dhn2 = dhn2 + _dot_nt(dup, wu[j])
        dh1n, dg2 = _rms_bwd(dhn2, h1h, r2, g2v)
        dh1 = dh2 + dh1n
        dh1_ref[...] = dh1
        dh1b = dh1.astype(BF16)
        dh1b_ref[...] = dh1b
        dmix = _dot_nt(dh1b, wo[...])
        dyp, dgp = _rms_bwd(dmix[:, :D_POOL], yph, rp, gp)
        dya, dga = _rms_bwd(dmix[:, D_POOL:], yah, ra, ga)
        dyp_ref[...] = dyp
        dya_ref[...] = dya
        sg_ref[0:1, :] += dgf
        sg_ref[1:2, :] += dg2
        sg_ref[2:3, :] += jnp.concatenate([dgp, dga], axis=1)
        sg_ref[3:4, :] += jnp.broadcast_to(jnp.sum(loss_rows, axis=0, keepdims=True), (1, D_MODEL))

    def tok(n):
        return pl.BlockSpec((tm, n), lambda i: (i, 0))

    def vec(n):
        return pl.BlockSpec((1, n), lambda i: (0, 0))

    any_spec = pl.BlockSpec(memory_space=pl.ANY)
    return pl.pallas_call(
        body,
        name="mlp_fwd_bwd",
        grid=(S // tm,),
        out_shape=[
            jax.ShapeDtypeStruct((S, D_MODEL), BF16),
            jax.ShapeDtypeStruct((S, D_MODEL), BF16),
            jax.ShapeDtypeStruct((S, D_FF), BF16),
            jax.ShapeDtypeStruct((S, D_FF), BF16),
            jax.ShapeDtypeStruct((S, D_MODEL), BF16),
            jax.ShapeDtypeStruct((S, D_MODEL), F32),
            jax.ShapeDtypeStruct((S, D_MODEL), BF16),
            jax.ShapeDtypeStruct((S, D_POOL), F32),
            jax.ShapeDtypeStruct((S, D_ATTN), F32),
            jax.ShapeDtypeStruct((SG_ROWS, D_MODEL), F32),
        ],
        in_specs=[tok(D_MODEL), tok(D_POOL), tok(D_ATTN), tok(D_MODEL), vec(D_POOL), vec(D_ATTN),
                  vec(D_MODEL), vec(D_MODEL), any_spec, any_spec, any_spec],
        out_specs=[tok(D_MODEL), tok(D_MODEL), tok(D_FF), tok(D_FF), tok(D_MODEL), tok(D_MODEL),
                   tok(D_MODEL), tok(D_POOL), tok(D_ATTN),
                   pl.BlockSpec((SG_ROWS, D_MODEL), lambda i: (0, 0))],
        scratch_shapes=[
            pltpu.VMEM((D_MODEL, D_MODEL), BF16),
            pltpu.VMEM((N_DEV, D_MODEL, fc), BF16),
            pltpu.VMEM((D_FF, D_MODEL), BF16),
            pltpu.VMEM((tm, D_FF), F32),
            pltpu.SemaphoreType.DMA((3,)),
        ],
        compiler_params=_params(("arbitrary",)),
    )(x, y_pool, y_attn, target, g_pool, g_attn, g2, gf, w_out, w_up, w_down)


def _wgrad(a, b, block_a, groups, name, travelling=(), gather=False, relations=None):
    n = len(travelling)
    S, ka = a.shape
    nb = b.shape[1]
    ts = min(1024, S)
    per = N_DEV // groups
    if block_a:
        ka //= groups
        blk = (ka // per, nb)
        a_spec = pl.BlockSpec((ts, ka), lambda g, s: (s, g))
        b_spec = pl.BlockSpec((ts, nb), lambda g, s: (s, 0))
    else:
        nb //= groups
        blk = (ka, nb // per)
        a_spec = pl.BlockSpec((ts, ka), lambda g, s: (s, 0))
        b_spec = pl.BlockSpec((ts, nb), lambda g, s: (s, g))
    steps = S // ts

    def body(a_ref, b_ref, *rest):
        o_ref, acc = rest[n], rest[2 * n + 1]
        g, s = pl.program_id(0), pl.program_id(1)

        def scatter():
            return _direct_exchange(rest[:n], rest[n + 1 : 2 * n + 1], rest[2 * n + 2 :], gather, relations)

        if n:
            @pl.when((g == 0) & (s == 0))
            def _():
                scatter()[0]()

        @pl.when(s == 0)
        def _():
            acc[...] = jnp.zeros_like(acc)

        acc[...] += _dot_tn(a_ref[...], b_ref[...])

        @pl.when(s == steps - 1)
        def _():
            for j in range(per):
                if block_a:
                    o_ref[j] = acc[blk[0] * j : blk[0] * (j + 1), :].astype(BF16)
                else:
                    o_ref[j] = acc[:, blk[1] * j : blk[1] * (j + 1)].astype(BF16)

        if n:
            @pl.when((g == groups - 1) & (s == steps - 1))
            def _():
                scatter()[1]()

    results = pl.pallas_call(
        body,
        name=name,
        grid=(groups, steps),
        out_shape=[jax.ShapeDtypeStruct((N_DEV,) + blk, BF16)] + _exchange_shapes(travelling, gather, relations),
        in_specs=[a_spec, b_spec] + [ANY_SPEC] * n,
        out_specs=[pl.BlockSpec((per,) + blk, lambda g, s: (g, 0, 0))] + [ANY_SPEC] * n,
        scratch_shapes=[pltpu.VMEM((ka, nb), F32)] + (_exchange_sems(n) if n else []),
        compiler_params=pltpu.CompilerParams(
            dimension_semantics=("arbitrary", "arbitrary"), vmem_limit_bytes=VMEM_LIMIT, has_side_effects=bool(n)
        ),
    )(a, b, *travelling)
    return results if n else results[0]


def _mixers_bwd(q, k, v, do, dyp, pooled, pool_w, scale, partials, relations):
    S = q.shape[0]
    T = ATTN_TILE
    nq = S // T
    n = len(partials)
    steps = D_ATTN // LANES

    def body(q_ref, k_ref, v_ref, do_ref, dyp_ref, pooled_ref, pw_ref, sc_ref, *rest):
        dq_ref, dk_ref, dv_ref, du_ref, dsc_ref, dpw_ref = rest[n : n + 6]
        dk_acc, dv_acc, carry_s = rest[2 * n + 6 : 2 * n + 9]
        step = pl.program_id(0)

        def scatter():
            sems = rest[2 * n + 9 :]
            return [
                _direct_exchange([rest[t]], [rest[n + 6 + t]], sems[3 * t : 3 * t + 3], False, relations[t])
                for t in range(n)
            ]

        @pl.when(step == 0)
        def _():
            for start, _ in scatter():
                start()

        _pool_bwd_group(dyp_ref, pooled_ref, pw_ref, sc_ref, du_ref, dsc_ref, dpw_ref, step)

        dk_acc[...] = jnp.zeros_like(dk_acc)
        dv_acc[...] = jnp.zeros_like(dv_acc)
        lane = lax.broadcasted_iota(jnp.int32, (T, LANES), 1)
        heads = (lane < HEAD_DIM, lane >= HEAD_DIM)
        after2, before2, causal = _band_ones(T)
        zero = jnp.zeros((T, 1), F32)

        def key_tile(k0):
            kb = k_ref[pl.ds(k0, T), :]
            k_rows = jnp.concatenate([jnp.where(m, kb, jnp.zeros_like(kb)) for m in heads], axis=0)
            return kb, v_ref[pl.ds(k0, T), :], k_rows

        def grads(items):
            das = [[[_dot_nt(doh, vb) for doh in qs[1]] for _, vb, _ in tiles] for qs, _, tiles, _, _, _, _ in items]
            probs, gs = [], []
            for (_, _, tiles, masks, carry_in, gates, _), da in zip(items, das):
                probs.append([[_weights(gates[b][h][0], gates[b][h][1], carry_in[b][h], masks[b]) for h in range(2)]
                              for b in range(len(tiles))])
                gs.append([[probs[-1][b][h] * da[b][h] for h in range(2)] for b in range(len(tiles))])
            before = [[[_dot(jnp.concatenate(_split2(g), axis=1), before2) for g in row] for row in item] for item in gs]
            dzs, lefts = [], []
            for i, (_, _, tiles, masks, _, gates, st) in enumerate(items):
                g_left, item_dz = [st[0], st[1]], [None] * len(tiles)
                for b in reversed(range(len(tiles))):
                    item_dz[b] = []
                    for h in range(2):
                        sig = jnp.exp(gates[b][h][0])
                        dz = gs[i][b][h] * (1.0 - sig) - sig * (g_left[h] + before[i][b][h])
                        if masks[b] is not None:
                            dz = jnp.where(masks[b], dz, 0.0)
                        item_dz[b].append(dz.astype(BF16))
                        g_left[h] = g_left[h] + jnp.sum(gs[i][b][h], axis=1, keepdims=True)
                dzs.append(item_dz)
                lefts.append(g_left)
            out = []
            for (_, _, tiles, _, _, _, st), item_dz, g_left in zip(items, dzs, lefts):
                dq = st[2] + _dot(
                    jnp.concatenate([dz for row in item_dz for dz in row], axis=1),
                    jnp.concatenate([k_rows for _, _, k_rows in tiles], axis=0),
                )
                out.append((g_left[0], g_left[1], dq))
            for (qs, starts, tiles, _, _, _, _), item_dz, item_probs in zip(items, dzs, probs):
                for b in range(len(tiles)):
                    dk_acc[pl.ds(starts[b], T), :] += _dot_tn(jnp.concatenate(item_dz[b], axis=0), qs[2])
                    dv_acc[pl.ds(starts[b], T), :] += _dot_tn(
                        jnp.concatenate([a.astype(BF16) for a in item_probs[b]], axis=0), qs[3]
                    )
            return out

        def more(st):
            return (st[0] > 0) & _reaches(jnp.maximum(st[1], st[2]))

        def q_tiles(qis, nb):
            masks = [causal] + [None] * (nb - 1)
            prepared, jobs = [], []
            for qi in qis:
                q0 = _tile_start(qi, T)
                qb = q_ref[pl.ds(q0, T), :]
                dob = do_ref[pl.ds(q0, T), :]
                qhs = [jnp.where(m, qb, jnp.zeros_like(qb)) for m in heads]
                dohs = [jnp.where(m, dob, 0.0).astype(BF16) for m in heads]
                queries = (qhs, dohs, jnp.concatenate(qhs, axis=0), jnp.concatenate(dohs, axis=0))
                starts = [_tile_start(qi - b, T) for b in range(nb)]
                tiles = [key_tile(k0) for k0 in starts]
                prepared.append((q0, queries, starts, tiles))
                jobs += [(qh, kb, mask) for (kb, _, _), mask in zip(tiles, masks) for qh in qhs]
            flat = _log_gates_of(jobs, after2)
            items = []
            for i, (qi, (q0, queries, starts, tiles)) in enumerate(zip(qis, prepared)):
                gates = [flat[2 * nb * i + 2 * b : 2 * nb * i + 2 * b + 2] for b in range(nb)]
                carries, carry_in = [zero, zero], []
                for b in range(nb):
                    carry_in.append(list(carries))
                    carries = [carries[h] + gates[b][h][2] for h in range(2)]
                st = (zero, zero, jnp.zeros((T, LANES), F32))
                if not isinstance(qi, int):
                    k_left = qi - (nb - 1)
                    qhs = queries[0]

                    def right_to_left(st, qhs=qhs):
                        kj = st[0] - 1
                        carry_s[0, kj] = st[1]
                        carry_s[1, kj] = st[2]
                        kb = k_ref[pl.ds(pl.multiple_of(kj * T, T), T), :]
                        sums = [jnp.sum(_log1m(_dot_nt(qh, kb), None)[0], axis=1, keepdims=True) for qh in qhs]
                        return kj, st[1] + sums[0], st[2] + sums[1]

                    k_first = lax.while_loop(more, right_to_left, (k_left, carries[0], carries[1]))[0]

                    def left_to_right(kj, st, queries=queries):
                        k0 = pl.multiple_of(kj * T, T)
                        tile = key_tile(k0)
                        left_gates = _log_gates(queries[0], [tile[0]], [None], after2)
                        carry = [[carry_s[0, kj], carry_s[1, kj]]]
                        return grads([(queries, [k0], [tile], [None], carry, left_gates, st)])[0]

                    st = lax.fori_loop(k_first, k_left, left_to_right, st)
                items.append((queries, starts, tiles, masks, carry_in, gates, st))
            for (q0, _, _, _), st in zip(prepared, grads(items)):
                dq_ref[pl.ds(q0, T), :] = (st[2] * (HEAD_DIM**-0.5)).astype(BF16)

        head_tiles = min(ATTN_WINDOW - 1, nq)
        for qi in range(head_tiles):
            q_tiles([qi], qi + 1)
        assert (nq - head_tiles) % ATTN_TOGETHER == 0

        def q_loop(i, carry):
            first = head_tiles + ATTN_TOGETHER * i
            q_tiles([first + j for j in range(ATTN_TOGETHER)], ATTN_WINDOW)
            return carry

        lax.fori_loop(0, (nq - head_tiles) // ATTN_TOGETHER, q_loop, 0)
        dk_ref[...] = dk_acc[...].astype(BF16)
        dv_ref[...] = dv_acc[...].astype(BF16)

        @pl.when(step == steps - 1)
        def _():
            for _, finish in scatter():
                finish()

    blk = pl.BlockSpec((S, LANES), lambda p: (0, p))
    sds = jax.ShapeDtypeStruct((S, D_ATTN), BF16)
    return pl.pallas_call(
        body,
        name="mixers_bwd",
        grid=(steps,),
        out_shape=[
            sds,
            sds,
            sds,
            jax.ShapeDtypeStruct((S, D_POOL), BF16),
            jax.ShapeDtypeStruct((8, D_POOL), F32),
            jax.ShapeDtypeStruct((N_GROUPS, GROUP_DIM, GROUP_DIM), BF16),
        ]
        + [_exchange_shapes([p], False, r)[0] for p, r in zip(partials, relations)],
        in_specs=[blk] * 6 + [_POOL_W_SPEC, _POOL_SCALE_SPEC] + [ANY_SPEC] * n,
        out_specs=[blk] * 4 + [pl.BlockSpec((8, GROUP_DIM), lambda p: (0, p)), _POOL_W_SPEC] + [ANY_SPEC] * n,
        scratch_shapes=[pltpu.VMEM((S, LANES), F32), pltpu.VMEM((S, LANES), F32), pltpu.VMEM((2, nq, T, 1), F32)]
        + [sem for _ in range(n) for sem in _exchange_sems(1)],
        compiler_params=pltpu.CompilerParams(
            dimension_semantics=("arbitrary",), vmem_limit_bytes=VMEM_LIMIT, has_side_effects=True
        ),
    )(q, k, v, do, dyp, pooled, pool_w, scale, *partials)


def _pool_bwd_group(dy_ref, pooled_ref, pw_ref, sc_ref, du_ref, dsc_ref, dpw_ref, g):
    S = dy_ref.shape[0]
    R = min(POOL_CHUNK, S)
    nc = S // R
    w = jnp.left_shift(jnp.int32(2), g)
    pw = pw_ref[0].astype(BF16)
    sc = sc_ref[...]
    d = lax.broadcasted_iota(jnp.int32, (R, R), 1) - lax.broadcasted_iota(jnp.int32, (R, R), 0)
    bt_cur = jnp.where((d >= 0) & (d < w), 1.0, 0.0).astype(BF16)
    dn = lax.broadcasted_iota(jnp.int32, (R, HALO), 1) + R - lax.broadcasted_iota(jnp.int32, (R, HALO), 0)
    bt_next = jnp.where(dn < w, 1.0, 0.0).astype(BF16)

    def per_count(dpl, r0):
        n = dpl.shape[0]
        count = jnp.minimum(r0 + lax.broadcasted_iota(jnp.int32, (n, 1), 0) + 1, w).astype(F32)
        return jnp.concatenate(_split2(dpl / count), axis=1)

    def chunk(r0, last, sums):
        dyv = dy_ref[pl.ds(r0, R), :]
        pooled = pooled_ref[pl.ds(r0, R), :]
        dmapped = (dyv * sc).astype(BF16)
        dsc = sums[0] + jnp.sum(dyv * _dot(pooled, pw), axis=0, keepdims=True)
        dpw = sums[1] + _dot_tn(pooled, dmapped)
        dpl = _dot_nt(dmapped, pw)
        wide = _dot(bt_cur, per_count(dpl, r0))
        if not last:
            nxt = pl.multiple_of(r0 + R, R)
            dpl_next = _dot_nt((dy_ref[pl.ds(nxt, HALO), :] * sc).astype(BF16), pw)
            wide = wide + _dot(bt_next, per_count(dpl_next, nxt))
        du_ref[pl.ds(r0, R), :] = (_lane_sum(wide, 2) - dpl).astype(BF16)
        return dsc, dpw

    sums = (jnp.zeros((1, GROUP_DIM), F32), jnp.zeros((GROUP_DIM, GROUP_DIM), F32))
    sums = lax.fori_loop(
        0, nc - 1, lambda c, s: chunk(pl.multiple_of(c * R, R), False, s), sums, unroll=POOL_UNROLL
    )
    dsc, dpw = chunk((nc - 1) * R, True, sums)
    dsc_ref[...] = jnp.zeros_like(dsc_ref)
    dsc_ref[0:1, :] = dsc
    dpw_ref[0] = dpw.astype(dpw_ref.dtype)


def _bwd_in(du, dq, dk, dv, w_in, x, dh1, g1):
    S = x.shape[0]
    tm = min(512, S)

    def body(du_ref, dq_ref, dk_ref, dv_ref, w_ref, x_ref, dh1_ref, g_ref, dx_ref, dproj_ref, dg_ref):
        @pl.when(pl.program_id(0) == 0)
        def _():
            dg_ref[...] = jnp.zeros_like(dg_ref)

        parts = (du_ref[...], dq_ref[...], dk_ref[...], dv_ref[...])
        dhn = jnp.zeros((tm, D_MODEL), F32)
        for j in range(N_DEV):
            piece = parts[j // 2][:, 256 * (j % 2) : 256 * (j % 2 + 1)]
            dproj_ref[:, 256 * j : 256 * (j + 1)] = piece
            dhn = dhn + _dot_nt(piece, w_ref[j])
        xv = x_ref[...]
        r = _rstd(xv)
        dxn, dg = _rms_bwd(dhn, xv * r, r, g_ref[...])
        dx_ref[...] = dh1_ref[...] + dxn
        dg_ref[0:1, :] += dg

    half = pl.BlockSpec((tm, D_POOL), lambda i: (i, 0))
    full = pl.BlockSpec((tm, D_MODEL), lambda i: (i, 0))
    return pl.pallas_call(
        body,
        name="bwd_in",
        grid=(S // tm,),
        out_shape=[
            jax.ShapeDtypeStruct((S, D_MODEL), F32),
            jax.ShapeDtypeStruct((S, D_IN_PROJ), BF16),
            jax.ShapeDtypeStruct((8, D_MODEL), F32),
        ],
        in_specs=[half, half, half, half,
                  pl.BlockSpec((N_DEV, D_MODEL, 256), lambda i: (0, 0, 0)),
                  full, full, pl.BlockSpec((1, D_MODEL), lambda i: (0, 0))],
        out_specs=[full, pl.BlockSpec((tm, D_IN_PROJ), lambda i: (i, 0)),
                   pl.BlockSpec((8, D_MODEL), lambda i: (0, 0))],
        compiler_params=_params(("arbitrary",)),
    )(du, dq, dk, dv, w_in, x, dh1, g1)


def _rows(a):
    a = a.reshape(-1, LANES)
    pad = (-a.shape[0]) % 8
    return jnp.pad(a, ((0, pad), (0, 0))) if pad else a


def kernel(x, norm1_g, w_in, pool_w, pool_scale, pool_out_g, attn_out_g, w_out, norm2_g, w_up, w_down, final_g, loss_target, m_norm1_g, m_w_in, m_pool_w, m_pool_scale, m_pool_out_g, m_attn_out_g, m_w_out, m_norm2_g, m_w_up, m_w_down, m_final_g, v_norm1_g, v_w_in, v_pool_w, v_pool_scale, v_pool_out_g, v_attn_out_g, v_w_out, v_norm2_g, v_w_up, v_w_down, v_final_g):
    S = x.shape[1]
    xs = x.reshape(S, D_MODEL)
    tgt = loss_target.reshape(S, D_MODEL)
    row = lambda a: a.reshape(1, -1)

    (w_in_g,) = _exchange([w_in.astype(BF16)], True, "gather_w_in")
    lanes = lambda a: a.reshape(-1, LANES)
    vector_sets = [
        [lanes(a) for a in (final_g, norm2_g, pool_out_g, attn_out_g, norm1_g, pool_scale)],
        [lanes(a) for a in (m_final_g, m_norm2_g, m_pool_out_g, m_attn_out_g, m_norm1_g, m_pool_scale)],
        [lanes(a) for a in (v_final_g, v_norm2_g, v_pool_out_g, v_attn_out_g, v_norm1_g, v_pool_scale)],
    ]
    hn, u_pool, q, k, v, *prepared = _fwd_in(xs, row(norm1_g), w_in_g, [w_out, w_up, w_down], vector_sets)
    shards, packed_vectors = prepared[:3], prepared[3:]
    y_attn, pooled, y_pool, w_out_g, w_up_g, w_down_g = _mixers_fwd(q, k, v, u_pool, pool_w, row(pool_scale), shards)
    w_out_full = w_out_g.reshape(D_MODEL, D_MODEL)
    w_down_full = w_down_g.reshape(D_FF, D_MODEL)
    mixed, hn2, act, dup, dh2b, dh1, dh1b, dyp, dya, sg = _mlp_fwd_bwd(
        xs, y_pool, y_attn, tgt, row(pool_out_g), row(attn_out_g), row(norm2_g), row(final_g),
        w_out_full, w_up_g, w_down_full,
    )
    gp_out = _wgrad(mixed, dh1b, True, 1, "wgrad_out")
    gp_down, land_out = _wgrad(act, dh2b, True, 2, "wgrad_down", [gp_out])
    early = (2, 4)
    late = tuple(k for k in ALL_RELATIONS if k not in early)
    gp_up, land_down_early = _wgrad(hn2, dup, False, 2, "wgrad_up", [gp_down], relations=early)
    dq, dk, dv, du, dsc, dpw, land_up, land_down_late = _mixers_bwd(
        q, k, v, dya, dyp, pooled, pool_w, row(pool_scale), [gp_up, gp_down], [None, late]
    )
    dx, dproj, dg1 = _bwd_in(du, dq, dk, dv, w_in_g, xs, dh1, row(norm1_g))

    partial_vectors = jnp.concatenate([_rows(p) for p in (sg[0], sg[1], sg[2], dg1[0], dsc[0], sg[3])], axis=0)
    flat_pool = lambda a: a.reshape(N_GROUPS * GROUP_DIM, GROUP_DIM)
    gp_in, landed_vectors, landed_pool = _wgrad(
        hn, dproj, False, 1, "wgrad_in", [partial_vectors, flat_pool(dpw)], gather=True
    )
    smalls = [
        (landed_vectors, [(*packed_vectors, None)]),
        (landed_pool, [(flat_pool(pool_w), flat_pool(m_pool_w), flat_pool(v_pool_w), None)]),
    ]
    tail = _reduce_adam_tail(
        [[land_out], [land_up], [land_down_late, land_down_early]], [w_out, w_up, w_down],
        [m_w_out, m_w_up, m_w_down], [v_w_out, v_w_up, v_w_down], gp_in, (w_in, m_w_in, v_w_in), smalls,
    )
    big = {name: tail[4 * t : 4 * t + 4] for t, name in enumerate(("w_out", "w_up", "w_down", "w_in"))}

    def unpack(vec, pw):
        out = {}
        for i, name in enumerate(("final_g", "norm2_g", "mix_g", "norm1_g", "pool_scale")):
            out[name] = vec[8 * i : 8 * i + 8].reshape(-1)
        out["pool_scale"] = out["pool_scale"][:D_POOL]
        out["pool_out_g"], out["attn_out_g"] = out["mix_g"][:D_POOL], out["mix_g"][D_POOL:]
        out["pool_w"] = pw.reshape(N_GROUPS, GROUP_DIM, GROUP_DIM)
        return out, vec[40, 0]

    small_out = [unpack(tail[16 + i], tail[20 + i]) for i in range(4)]
    loss = small_out[0][1]
    order = ("norm1_g", "w_in", "pool_w", "pool_scale", "pool_out_g", "attn_out_g", "w_out", "norm2_g", "w_up",
             "w_down", "final_g")
    outs = [loss, dx.reshape(1, S, D_MODEL)]
    for i in range(4):
        for name in order:
            outs.append(big[name][i] if name in big else small_out[i][0][name])
    return tuple(outs)
```

```python
import jax
import jax.numpy as jnp
from jax import lax
from jax.experimental import pallas as pl
from jax.experimental.pallas import tpu as pltpu

F32 = jnp.float32
BF16 = jnp.bfloat16
MESH = pl.DeviceIdType.MESH

N_DEV = 8
D_MODEL = 1024
D_POOL = 512
D_ATTN = 512
N_GROUPS = 4
GROUP_DIM = 128
HEAD_DIM = 64
D_FF = 4096
D_IN_PROJ = 2048
EPS = 1e-6
HALO = 16
ATTN_TILE = 128
LANES = 128
EXP_UNDERFLOW = -104.0

ADAM_LR = 0.001
ADAM_B1 = 0.9
ADAM_B2 = 0.999
ADAM_EPS = 1e-08
ADAM_WD = 0.01
ADAM_STEP = 10

VMEM_LIMIT = 56 * 1024 * 1024


def _params(semantics=None, vmem=VMEM_LIMIT):
    return pltpu.CompilerParams(dimension_semantics=semantics, vmem_limit_bytes=vmem)


def _dot(a, b):
    return jnp.dot(a, b, preferred_element_type=F32)


def _dot_nt(a, b):
    return lax.dot_general(a, b, (((1,), (1,)), ((), ())), preferred_element_type=F32)


def _dot_tn(a, b):
    return lax.dot_general(a, b, (((0,), (0,)), ((), ())), preferred_element_type=F32)


def _split2(x):
    hi = x.astype(BF16)
    lo = (x - hi.astype(F32)).astype(BF16)
    return hi, lo


def _split3(x):
    hi = x.astype(BF16)
    r = x - hi.astype(F32)
    mid = r.astype(BF16)
    lo = (r - mid.astype(F32)).astype(BF16)
    return hi, mid, lo


def _rstd(h):
    return lax.rsqrt(jnp.mean(h * h, axis=-1, keepdims=True) + EPS)


def _rms_bwd(dout, hhat, r, g):
    dg = jnp.sum(dout * hhat, axis=0, keepdims=True)
    dxh = dout * g
    dh = r * (dxh - hhat * jnp.mean(dxh * hhat, axis=-1, keepdims=True))
    return dh, dg


def _my_index():
    return 4 * lax.axis_index("x") + 2 * lax.axis_index("y") + lax.axis_index("c")


def _peer(k):
    x, y, c = lax.axis_index("x"), lax.axis_index("y"), lax.axis_index("c")
    px = 1 - x if (k >> 2) & 1 else x
    py = 1 - y if (k >> 1) & 1 else y
    pc = 1 - c if k & 1 else c
    return (px, py, pc), 4 * px + 2 * py + pc


N_PEERS = N_DEV - 1
ANY_SPEC = pl.BlockSpec(memory_space=pl.ANY)


def _exchange_sems(n):
    return [
        pltpu.SemaphoreType.DMA((n * N_PEERS,)),
        pltpu.SemaphoreType.DMA((n * N_PEERS,)),
        pltpu.SemaphoreType.DMA((n,)),
    ]


ALL_RELATIONS = tuple(range(N_DEV))


def _exchange_shapes(blocks, gather, relations=None):
    if gather:
        return [jax.ShapeDtypeStruct((N_DEV,) + b.shape, b.dtype) for b in blocks]
    slots = N_DEV if relations is None else len(relations)
    return [jax.ShapeDtypeStruct((slots,) + b.shape[1:], b.dtype) for b in blocks]


def _direct_exchange(ins, outs, sems, gather, relations=None):
    send_sems, recv_sems, local_sems = sems
    me = _my_index()
    own, sends, recvs = [], [], []
    for t in range(len(ins)):
        for slot, k in enumerate(ALL_RELATIONS if relations is None else relations):
            if k == 0:
                mine = outs[t].at[me if relations is None else slot]
                own.append(pltpu.make_async_copy(ins[t] if gather else ins[t].at[me], mine, local_sems.at[t]))
                continue
            peer, peer_idx = _peer(k)
            src = ins[t] if gather else ins[t].at[peer_idx]
            sent_to = outs[t].at[me if relations is None else slot]
            lands_in = outs[t].at[peer_idx if relations is None else slot]
            for dst, bucket in ((sent_to, sends), (lands_in, recvs)):
                bucket.append(
                    pltpu.make_async_remote_copy(
                        src_ref=src,
                        dst_ref=dst,
                        send_sem=send_sems.at[t * N_PEERS + k - 1],
                        recv_sem=recv_sems.at[t * N_PEERS + k - 1],
                        device_id=peer,
                        device_id_type=MESH,
                    )
                )

    def start():
        for cp in own + sends:
            cp.start()

    def finish():
        for cp in recvs:
            cp.wait_recv()
        for cp in sends:
            cp.wait_send()
        for cp in own:
            cp.wait()

    return start, finish


def _flip(a, bit):
    return a + bit - 2 * a * bit


def _two_level_gather(ins, outs, sems):
    send_sems, recv_sems, local_sems = sems
    x, y, c = lax.axis_index("x"), lax.axis_index("y"), lax.axis_index("c")
    me, sibling = (x, y, c), (x, y, 1 - c)
    x_nbr, y_nbr, diagonal = (1 - x, y, c), (x, 1 - y, c), (1 - x, 1 - y, c)
    relay_of = (_flip(x, 1 - c), _flip(y, c), c)
    relay_to = (_flip(x, c), _flip(y, 1 - c), c)

    def copy(t, k, block, to, from_input=False):
        slot = outs[t].at[4 * block[0] + 2 * block[1] + block[2]]
        return pltpu.make_async_remote_copy(
            src_ref=ins[t] if from_input else slot,
            dst_ref=slot,
            send_sem=send_sems.at[t * N_PEERS + k],
            recv_sem=recv_sems.at[t * N_PEERS + k],
            device_id=to,
            device_id_type=MESH,
        )

    arrays = range(len(ins))
    own = [pltpu.make_async_copy(ins[t], outs[t].at[4 * x + 2 * y + c], local_sems.at[t]) for t in arrays]
    first = [copy(t, k, me, to, True) for t in arrays for k, to in ((1, x_nbr), (2, y_nbr), (0, sibling))]
    landed = [copy(t, k, block, me) for t in arrays for k, block in ((1, x_nbr), (2, y_nbr))]
    relays = [copy(t, 3, relay_of, relay_to) for t in arrays]
    passed = [copy(t, 3 + k, block, sibling) for t in arrays for k, block in ((1, x_nbr), (2, y_nbr))]
    relayed = [(copy(t, 3, diagonal, me), copy(t, 6, diagonal, sibling)) for t in arrays]
    last = [copy(t, 0, sibling, me) for t in arrays]
    last += [copy(t, 3 + k, (*block[:2], 1 - c), me) for t in arrays for k, block in ((1, x_nbr), (2, y_nbr), (3, diagonal))]

    def start():
        for cp in own + first:
            cp.start()

    def relay():
        for cp in landed:
            cp.wait_recv()
        for cp in relays + passed:
            cp.start()

    def forward():
        for arrived, onward in relayed:
            arrived.wait_recv()
            onward.start()

    def finish():
        for cp in last:
            cp.wait_recv()
        for cp in first + relays + passed + [onward for _, onward in relayed]:
            cp.wait_send()
        for cp in own:
            cp.wait()

    return start, relay, forward, finish


N_CHIPS = 4
PAIR_SEMS = (N_CHIPS, N_CHIPS, N_CHIPS - 1, N_CHIPS - 1)


def _pair_reduce_scatter(gp_hbm, own, pair, summed, land, local_sem, sems):
    d2d_send, d2d_recv, ici_send, ici_recv = sems
    x, y, c = lax.axis_index("x"), lax.axis_index("y"), lax.axis_index("c")
    my_chip = 2 * x + y
    sibling = (x, y, 1 - c)
    chips = [(1 - x, y), (x, 1 - y), (1 - x, 1 - y)]
    local = [pltpu.make_async_copy(gp_hbm.at[2 * j + c], own.at[j], local_sem.at[0]) for j in range(N_CHIPS)]
    to_sibling = [
        pltpu.make_async_remote_copy(
            src_ref=gp_hbm.at[2 * j + 1 - c], dst_ref=pair.at[j], send_sem=d2d_send.at[j], recv_sem=d2d_recv.at[j],
            device_id=sibling, device_id_type=MESH,
        )
        for j in range(N_CHIPS)
    ]
    to_chips, from_chips = [], []
    for r, (px, py) in enumerate(chips):
        for dst, bucket in ((land.at[my_chip], to_chips), (land.at[2 * px + py], from_chips)):
            bucket.append(
                pltpu.make_async_remote_copy(
                    src_ref=summed.at[2 * px + py], dst_ref=dst, send_sem=ici_send.at[r], recv_sem=ici_recv.at[r],
                    device_id=(px, py, c), device_id_type=MESH,
                )
            )

    def start():
        for cp in local + to_sibling:
            cp.start()

    def middle():
        for cp in to_sibling:
            cp.wait_recv()
        pltpu.make_async_copy(gp_hbm.at[pl.ds(0, N_CHIPS)], own, local_sem.at[0]).wait()
        for j in range(N_CHIPS):
            summed[j] = (own[j].astype(F32) + pair[j].astype(F32)).astype(BF16)
        for cp in to_chips:
            cp.start()

    def finish():
        for cp in from_chips:
            cp.wait_recv()
        for cp in to_chips + to_sibling:
            cp.wait_send()
        land[my_chip] = summed[my_chip]

    return start, middle, finish


def _exchange(blocks, gather, name):
    n = len(blocks)

    def body(*refs):
        if gather:
            stages = _two_level_gather(refs[:n], refs[n : 2 * n], refs[2 * n :])
        else:
            stages = _direct_exchange(refs[:n], refs[n : 2 * n], refs[2 * n :], False)
        for stage in stages:
            stage()

    return pl.pallas_call(
        body,
        name=name,
        out_shape=_exchange_shapes(blocks, gather),
        in_specs=[ANY_SPEC] * n,
        out_specs=[ANY_SPEC] * n,
        scratch_shapes=_exchange_sems(n),
        compiler_params=pltpu.CompilerParams(has_side_effects=True),
    )(*blocks)


def _adam(w, g, m, v):
    m2 = ADAM_B1 * m + (1.0 - ADAM_B1) * g
    v2 = ADAM_B2 * v + (1.0 - ADAM_B2) * jnp.square(g)
    m_hat = m2 / (1.0 - ADAM_B1**ADAM_STEP)
    v_hat = v2 / (1.0 - ADAM_B2**ADAM_STEP)
    delta = -ADAM_LR * (m_hat / (jnp.sqrt(v_hat) + ADAM_EPS) + ADAM_WD * w)
    return delta, m2, v2


TAIL_STEPS = 8


def _reduce_adam_tail(lands, ws, ms, vs, gp_last, last, smalls):
    nw, ns = len(ws), len(smalls)
    n_lands = sum(len(arrays) for arrays in lands)
    tiles = [w.shape[0] // TAIL_STEPS for w in ws]
    blk = gp_last.shape[1:]
    entries = [(i, e) for i, (_, es) in enumerate(smalls) for e in es]
    n_params = 3 * sum(e[0] is not None for _, e in entries)
    n_small_out = sum(4 if e[0] is not None else 1 for _, e in entries)

    def body(*refs):
        gp_hbm = refs[0]
        small_land = refs[1 : 1 + ns]
        land_refs = list(refs[1 + ns : 1 + ns + n_lands])
        refs = refs[1 + ns + n_lands - nw :]
        w_refs, m_refs, v_refs = (refs[nw * (i + 1) : nw * (i + 2)] for i in range(3))
        last_refs = refs[4 * nw : 4 * nw + 3]
        refs = refs[4 * nw + 3 :]
        param_refs = refs[:n_params]
        outs = refs[n_params:]
        big_out = outs[: 4 * nw]
        last_out = outs[4 * nw : 4 * nw + 4]
        small_out = outs[4 * nw + 4 : 4 * nw + 4 + n_small_out]
        scratch = outs[4 * nw + 4 + n_small_out :]
        pair_bufs = scratch[:4]
        local_sem = scratch[4]
        pair_sems = scratch[5:9]
        step = pl.program_id(0)

        def pair():
            return _pair_reduce_scatter(gp_hbm, *pair_bufs, local_sem, pair_sems)

        @pl.when(step == 0)
        def _():
            pair()[0]()

        @pl.when(step == 1)
        def _():
            pair()[1]()

        for t in range(nw):
            g = None
            for _ in lands[t]:
                landed = land_refs.pop(0)
                for s in range(landed.shape[0]):
                    part = landed[s].astype(F32)
                    g = part if g is None else g + part
            for ref, val in zip(big_out[4 * t : 4 * t + 4], (g,) + _adam(w_refs[t][...], g, m_refs[t][...], v_refs[t][...])):
                ref[...] = val

        @pl.when(step == TAIL_STEPS - 1)
        def _():
            pair()[2]()
            land = pair_bufs[3]
            g = land[0].astype(F32)
            for chip in range(1, N_CHIPS):
                g = g + land[chip].astype(F32)
            for ref, val in zip(last_out, (g,) + _adam(last_refs[0][...], g, last_refs[1][...], last_refs[2][...])):
                ref[...] = val
            sums = []
            for i in range(ns):
                g = small_land[i][0].astype(F32)
                for s in range(1, N_DEV):
                    g = g + small_land[i][s].astype(F32)
                sums.append(g)
            params, results = list(param_refs), list(small_out)
            for i, (w, _, _, where) in entries:
                g = sums[i]
                if where is not None:
                    shape = w.shape if w is not None else where[2:]
                    g = g[where[0] : where[0] + shape[0], where[1] : where[1] + shape[1]]
                if w is None:
                    results.pop(0)[...] = g
                    continue
                w_ref, m_ref, v_ref = params[:3]
                del params[:3]
                for val in (g,) + _adam(w_ref[...], g, m_ref[...], v_ref[...]):
                    results.pop(0)[...] = val

    def tile(t):
        return pl.BlockSpec((tiles[t], ws[t].shape[1]), lambda i: (i, 0))

    def land_tile(t, landed):
        return pl.BlockSpec((landed.shape[0], tiles[t], ws[t].shape[1]), lambda i: (0, i, 0))

    def whole(shape):
        return pl.BlockSpec(shape, lambda step: (0, 0))

    big_sds = [jax.ShapeDtypeStruct(ws[t].shape, F32) for t in range(nw) for _ in range(4)]
    params, small_shapes = [], []
    for _, (w, m, v, where) in entries:
        if w is None:
            small_shapes.append(tuple(where[2:]))
        else:
            params += [w, m, v]
            small_shapes += [w.shape] * 4
    return pl.pallas_call(
        body,
        name="reduce_adam_tail",
        grid=(TAIL_STEPS,),
        out_shape=big_sds
        + [jax.ShapeDtypeStruct(blk, F32)] * 4
        + [jax.ShapeDtypeStruct(shape, F32) for shape in small_shapes],
        in_specs=[ANY_SPEC]
        + [pl.BlockSpec(smalls[i][0].shape, lambda step: (0, 0, 0)) for i in range(ns)]
        + [land_tile(t, landed) for t in range(nw) for landed in lands[t]]
        + [tile(t) for _ in range(3) for t in range(nw)]
        + [whole(blk)] * 3
        + [whole(p.shape) for p in params],
        out_specs=[tile(t) for t in range(nw) for _ in range(4)]
        + [whole(blk)] * 4
        + [whole(shape) for shape in small_shapes],
        scratch_shapes=[pltpu.VMEM((N_CHIPS,) + blk, BF16)] * 4
        + [pltpu.SemaphoreType.DMA((1,))]
        + [pltpu.SemaphoreType.DMA((count,)) for count in PAIR_SEMS],
        compiler_params=pltpu.CompilerParams(
            dimension_semantics=("arbitrary",), vmem_limit_bytes=VMEM_LIMIT, has_side_effects=True
        ),
    )(gp_last, *[s[0] for s in smalls], *[a for arrays in lands for a in arrays], *ws, *ms, *vs, *last, *params)


VECTOR_ROWS = 48


def _fwd_in(x, g1, w_in, shards, vector_sets):
    S = x.shape[0]
    tm = min(512, S)
    steps = S // tm
    n = len(shards)
    nv = sum(len(vs) for vs in vector_sets)

    def body(x_ref, g_ref, w_ref, *rest):
        hn_ref, u_ref, q_ref, k_ref, v_ref = rest[n + nv : n + nv + 5]
        for src, dst in zip(rest[:n], rest[n + nv + 5 :]):
            dst[...] = src[...].astype(BF16)
        vectors = list(rest[n : n + nv])
        for vs, packed in zip(vector_sets, rest[2 * n + nv + 5 :]):
            at = 0
            for _ in vs:
                ref = vectors.pop(0)
                packed[at : at + ref.shape[0], :] = ref[...]
                at += ref.shape[0]
            packed[at:, :] = jnp.zeros((VECTOR_ROWS - at, LANES), F32)
        xv = x_ref[...]
        hn = (xv * _rstd(xv) * g_ref[...]).astype(BF16)
        hn_ref[...] = hn
        outs = (u_ref, q_ref, k_ref, v_ref)
        for j in range(N_DEV):
            p = _dot(hn, w_ref[j])
            cols = slice(256 * (j % 2), 256 * (j % 2 + 1))
            if j // 2 == 0:
                u_ref[:, cols] = p
            elif j // 2 == 1:
                q_ref[:, cols] = (p * (HEAD_DIM**-0.5)).astype(BF16)
            else:
                outs[j // 2][:, cols] = p.astype(BF16)

    half = pl.BlockSpec((tm, D_POOL), lambda i: (i, 0))
    shard_tiles = [pl.BlockSpec((s.shape[0] // steps, s.shape[1]), lambda i: (i, 0)) for s in shards]
    return pl.pallas_call(
        body,
        name="fwd_in",
        grid=(steps,),
        out_shape=[
            jax.ShapeDtypeStruct((S, D_MODEL), BF16),
            jax.ShapeDtypeStruct((S, D_POOL), F32),
            jax.ShapeDtypeStruct((S, D_ATTN), BF16),
            jax.ShapeDtypeStruct((S, D_ATTN), BF16),
            jax.ShapeDtypeStruct((S, D_ATTN), BF16),
        ]
        + [jax.ShapeDtypeStruct(s.shape, BF16) for s in shards]
        + [jax.ShapeDtypeStruct((VECTOR_ROWS, LANES), F32)] * len(vector_sets),
        in_specs=[
            pl.BlockSpec((tm, D_MODEL), lambda i: (i, 0)),
            pl.BlockSpec((1, D_MODEL), lambda i: (0, 0)),
            pl.BlockSpec((N_DEV, D_MODEL, 256), lambda i: (0, 0, 0)),
        ]
        + shard_tiles
        + [pl.BlockSpec(v.shape, lambda i: (0, 0)) for vs in vector_sets for v in vs],
        out_specs=[pl.BlockSpec((tm, D_MODEL), lambda i: (i, 0)), half, half, half, half]
        + shard_tiles
        + [pl.BlockSpec((VECTOR_ROWS, LANES), lambda i: (0, 0))] * len(vector_sets),
        compiler_params=_params(("arbitrary",)),
    )(x, g1, w_in, *shards, *[v for vs in vector_sets for v in vs])


POOL_CHUNK = 256
POOL_UNROLL = 5
_POOL_W_SPEC = pl.BlockSpec((1, GROUP_DIM, GROUP_DIM), lambda g: (g, 0, 0))
_POOL_SCALE_SPEC = pl.BlockSpec((1, GROUP_DIM), lambda g: (0, g))


def _lane_sum(wide, n):
    out = wide[:, :GROUP_DIM]
    for i in range(1, n):
        out = out + wide[:, GROUP_DIM * i : GROUP_DIM * (i + 1)]
    return out


def _pool_fwd_group(u_ref, pw_ref, sc_ref, pooled_ref, y_ref, g):
    S = u_ref.shape[0]
    R = min(POOL_CHUNK, S)
    w = jnp.left_shift(jnp.int32(2), g)
    d = lax.broadcasted_iota(jnp.int32, (R, R), 0) - lax.broadcasted_iota(jnp.int32, (R, R), 1)
    b_cur = jnp.where((d >= 0) & (d < w), 1.0, 0.0).astype(BF16)
    dp = lax.broadcasted_iota(jnp.int32, (R, HALO), 0) + HALO - lax.broadcasted_iota(jnp.int32, (R, HALO), 1)
    b_prev = jnp.where(dp < w, 1.0, 0.0).astype(BF16)
    pw = pw_ref[0].astype(BF16)
    sc = sc_ref[...]

    def chunk(r0, first):
        cur = u_ref[pl.ds(r0, R), :]
        wide = _dot(b_cur, jnp.concatenate(_split3(cur), axis=1))
        if not first:
            prev = u_ref[pl.ds(pl.multiple_of(r0 - HALO, HALO), HALO), :]
            wide = wide + _dot(b_prev, jnp.concatenate(_split3(prev), axis=1))
        count = jnp.minimum(r0 + lax.broadcasted_iota(jnp.int32, (R, 1), 0) + 1, w).astype(F32)
        pooled = (_lane_sum(wide, 3) / count - cur).astype(BF16)
        pooled_ref[pl.ds(r0, R), :] = pooled
        y_ref[pl.ds(r0, R), :] = _dot(pooled, pw) * sc

    chunk(0, True)

    def rest(c, carry):
        chunk(pl.multiple_of(c * R, R), False)
        return carry

    lax.fori_loop(1, S // R, rest, 0, unroll=POOL_UNROLL)


ATTN_WINDOW = 3
ATTN_TOGETHER = 3
ATTN_TOGETHER_FWD = 3


def _band_ones(T):
    row = lax.broadcasted_iota(jnp.int32, (T, T), 0)
    col = lax.broadcasted_iota(jnp.int32, (T, T), 1)
    after = jnp.where(row > col, 1.0, 0.0).astype(BF16)
    before = jnp.where(row < col, 1.0, 0.0).astype(BF16)
    return jnp.concatenate([after, after], axis=0), jnp.concatenate([before, before], axis=0), col < row


def _log1m(z, mask):
    sp = jnp.log(1.0 + jnp.exp(-jnp.abs(z)))
    l1m = -jnp.maximum(z, 0.0) - sp
    return (l1m if mask is None else jnp.where(mask, l1m, 0.0)), sp


def _log_gates_of(jobs, after2):
    zs = [_dot_nt(qh, kb) for qh, kb, _ in jobs]
    terms = []
    for z, (_, _, mask) in zip(zs, jobs):
        l1m, sp = _log1m(z, mask)
        terms.append(
            (jnp.minimum(z, 0.0) - sp, jnp.concatenate(_split2(l1m), axis=1), jnp.sum(l1m, axis=1, keepdims=True))
        )
    return [(ls, _dot(split, after2), total) for ls, split, total in terms]


def _log_gates(qhs, kbs, masks, after2):
    flat = _log_gates_of([(qh, kb, mask) for kb, mask in zip(kbs, masks) for qh in qhs], after2)
    return [flat[len(qhs) * b : len(qhs) * (b + 1)] for b in range(len(kbs))]


def _tile_start(index, T):
    return index * T if isinstance(index, int) else pl.multiple_of(index * T, T)


def _weights(ls, tail, carry, mask):
    a = jnp.exp(ls + tail + carry)
    return a if mask is None else jnp.where(mask, a, 0.0)


def _reaches(carry):
    return jnp.max(carry) > EXP_UNDERFLOW


def _mixers_fwd(q, k, v, u, pool_w, scale, shards):
    S = q.shape[0]
    T = ATTN_TILE
    nq = S // T
    n = len(shards)
    steps = D_ATTN // LANES

    def body(q_ref, k_ref, v_ref, u_ref, pw_ref, sc_ref, *rest):
        o_ref, pooled_ref, yp_ref = rest[n : n + 3]
        step = pl.program_id(0)

        def gather():
            return _two_level_gather(rest[:n], rest[n + 3 : 2 * n + 3], rest[2 * n + 3 :])

        @pl.when(step == 0)
        def _():
            gather()[0]()

        @pl.when(step == steps // 2)
        def _():
            gather()[1]()

        _pool_fwd_group(u_ref, pw_ref, sc_ref, pooled_ref, yp_ref, step)

        lane = lax.broadcasted_iota(jnp.int32, (T, LANES), 1)
        heads = (lane < HEAD_DIM, lane >= HEAD_DIM)
        after2, _, causal = _band_ones(T)
        zero = jnp.zeros((T, 1), F32)

        def key_tile(k0):
            vb = v_ref[pl.ds(k0, T), :]
            return k_ref[pl.ds(k0, T), :], jnp.concatenate([jnp.where(m, vb, jnp.zeros_like(vb)) for m in heads], axis=0)

        def q_tiles(qis, nb):
            queries, jobs = [], []
            for qi in qis:
                q0 = _tile_start(qi, T)
                qb = q_ref[pl.ds(q0, T), :]
                qhs = [jnp.where(m, qb, jnp.zeros_like(qb)) for m in heads]
                tiles = [key_tile(_tile_start(qi - b, T)) for b in range(nb)]
                queries.append((q0, qhs, [v_rows for _, v_rows in tiles]))
                jobs += [(qh, kb, causal if b == 0 else None) for b, (kb, _) in enumerate(tiles) for qh in qhs]
            gates = _log_gates_of(jobs, after2)
            states = []
            for i, (q0, qhs, values) in enumerate(queries):
                carries, probs = [zero, zero], []
                for b in range(nb):
                    for h in range(2):
                        ls, tail, total = gates[2 * nb * i + 2 * b + h]
                        probs.append(_weights(ls, tail, carries[h], causal if b == 0 else None).astype(BF16))
                        carries[h] = carries[h] + total
                states.append((carries, _dot(jnp.concatenate(probs, axis=1), jnp.concatenate(values, axis=0))))

            def more(st):
                return (st[0] > 0) & _reaches(jnp.maximum(st[1], st[2]))

            for qi, (q0, qhs, _), (carries, acc) in zip(qis, queries, states):

                def k_step(st, qhs=qhs):
                    kj = st[0] - 1
                    kb, v_rows = key_tile(pl.multiple_of(kj * T, T))
                    new, probs = [], []
                    for (ls, tail, total), carry in zip(_log_gates(qhs, [kb], [None], after2)[0], st[1:3]):
                        probs.append(_weights(ls, tail, carry, None).astype(BF16))
                        new.append(carry + total)
                    return kj, new[0], new[1], st[3] + _dot(jnp.concatenate(probs, axis=1), v_rows)

                if not isinstance(qi, int):
                    acc = lax.while_loop(more, k_step, (qi - (nb - 1), carries[0], carries[1], acc))[3]
                o_ref[pl.ds(q0, T), :] = acc

        head_tiles = min(ATTN_WINDOW - 1, nq)
        for qi in range(head_tiles):
            q_tiles([qi], qi + 1)
        assert (nq - head_tiles) % ATTN_TOGETHER_FWD == 0

        def q_loop(i, carry):
            first = head_tiles + ATTN_TOGETHER_FWD * i
            q_tiles([first + j for j in range(ATTN_TOGETHER_FWD)], ATTN_WINDOW)
            return carry

        lax.fori_loop(0, (nq - head_tiles) // ATTN_TOGETHER_FWD, q_loop, 0)

        @pl.when(step == steps - 1)
        def _():
            _, _, forward, finish = gather()
            forward()
            finish()

    blk = pl.BlockSpec((S, LANES), lambda p: (0, p))
    return pl.pallas_call(
        body,
        name="mixers_fwd",
        grid=(steps,),
        out_shape=[
            jax.ShapeDtypeStruct((S, D_ATTN), F32),
            jax.ShapeDtypeStruct((S, D_POOL), BF16),
            jax.ShapeDtypeStruct((S, D_POOL), F32),
        ]
        + _exchange_shapes(shards, True),
        in_specs=[blk, blk, blk, blk, _POOL_W_SPEC, _POOL_SCALE_SPEC] + [ANY_SPEC] * n,
        out_specs=[blk, blk, blk] + [ANY_SPEC] * n,
        scratch_shapes=_exchange_sems(n),
        compiler_params=pltpu.CompilerParams(
            dimension_semantics=("arbitrary",), vmem_limit_bytes=VMEM_LIMIT, has_side_effects=True
        ),
    )(q, k, v, u, pool_w, scale, *shards)


MLP_AHEAD = 1
SG_ROWS = 8


def _mlp_fwd_bwd(x, y_pool, y_attn, target, g_pool, g_attn, g2, gf, w_out, w_up, w_down):
    S = x.shape[0]
    tm = min(256, S)
    fc = D_FF // N_DEV

    def body(x_ref, yp_ref, ya_ref, t_ref, gp_ref, ga_ref, g2_ref, gf_ref, wo_hbm, wu_hbm, wd_hbm,
             mixed_ref, hn2_ref, act_ref, dup_ref, dh2_ref, dh1_ref, dh1b_ref, dyp_ref, dya_ref, sg_ref,
             wo, wu, wd, up_s, sems):
        @pl.when(pl.program_id(0) == 0)
        def _():
            copies = [
                pltpu.make_async_copy(wo_hbm, wo, sems.at[0]),
                pltpu.make_async_copy(wu_hbm, wu, sems.at[1]),
                pltpu.make_async_copy(wd_hbm, wd, sems.at[2]),
            ]
            for cp in copies:
                cp.start()
            for cp in copies:
                cp.wait()
            sg_ref[...] = jnp.zeros_like(sg_ref)

        gp, ga, g2v, gfv = gp_ref[...], ga_ref[...], g2_ref[...], gf_ref[...]
        yp, ya = yp_ref[...], ya_ref[...]
        rp, ra = _rstd(yp), _rstd(ya)
        yph, yah = yp * rp, ya * ra
        mixed = jnp.concatenate([(yph * gp).astype(BF16), (yah * ga).astype(BF16)], axis=1)
        mixed_ref[...] = mixed
        h1 = x_ref[...] + _dot(mixed, wo[...])
        r2 = _rstd(h1)
        h1h = h1 * r2
        hn2 = (h1h * g2v).astype(BF16)
        hn2_ref[...] = hn2
        h2 = h1
        ups = [_dot(hn2, wu[j]) for j in range(MLP_AHEAD)]
        for j in range(N_DEV):
            cols = slice(fc * j, fc * (j + 1))
            if j + MLP_AHEAD < N_DEV:
                ups.append(_dot(hn2, wu[j + MLP_AHEAD]))
            up = ups.pop(0)
            up_s[:, cols] = up
            act = jnp.square(jnp.maximum(up, 0.0)).astype(BF16)
            act_ref[:, cols] = act
            h2 = h2 + _dot(act, wd[cols, :])
        rf = _rstd(h2)
        h2h = h2 * rf
        diff = h2h * gfv - t_ref[...]
        loss_rows = 0.5 * jnp.mean(diff * diff, axis=-1, keepdims=True)
        dy = diff * (1.0 / D_MODEL)
        dh2, dgf = _rms_bwd(dy, h2h, rf, gfv)
        dh2b = dh2.astype(BF16)
        dh2_ref[...] = dh2b
        dhn2 = jnp.zeros((tm, D_MODEL), F32)
        dacts = [_dot_nt(dh2b, wd[fc * j : fc * (j + 1), :]) for j in range(MLP_AHEAD)]
        for j in range(N_DEV):
            cols = slice(fc * j, fc * (j + 1))
            if j + MLP_AHEAD < N_DEV:
                dacts.append(_dot_nt(dh2b, wd[fc * (j + MLP_AHEAD) : fc * (j + MLP_AHEAD + 1), :]))
            dup = (dacts.pop(0) * (2.0 * jnp.maximum(up_s[:, cols], 0.0))).astype(BF16)
            dup_ref[:, cols] = dup
            dhn2 = dhn2 + _dot_nt(dup, wu[j])
        dh1n, dg2 = _rms_bwd(dhn2, h1h, r2, g2v)
        dh1 = dh2 + dh1n
        dh1_ref[...] = dh1
        dh1b = dh1.astype(BF16)
        dh1b_ref[...] = dh1b
        dmix = _dot_nt(dh1b, wo[...])
        dyp, dgp = _rms_bwd(dmix[:, :D_POOL], yph, rp, gp)
        dya, dga = _rms_bwd(dmix[:, D_POOL:], yah, ra, ga)
        dyp_ref[...] = dyp
        dya_ref[...] = dya
        sg_ref[0:1, :] += dgf
        sg_ref[1:2, :] += dg2
        sg_ref[2:3, :] += jnp.concatenate([dgp, dga], axis=1)
        sg_ref[3:4, :] += jnp.broadcast_to(jnp.sum(loss_rows, axis=0, keepdims=True), (1, D_MODEL))

    def tok(n):
        return pl.BlockSpec((tm, n), lambda i: (i, 0))

    def vec(n):
        return pl.BlockSpec((1, n), lambda i: (0, 0))

    any_spec = pl.BlockSpec(memory_space=pl.ANY)
    return pl.pallas_call(
        body,
        name="mlp_fwd_bwd",
        grid=(S // tm,),
        out_shape=[
            jax.ShapeDtypeStruct((S, D_MODEL), BF16),
            jax.ShapeDtypeStruct((S, D_MODEL), BF16),
            jax.ShapeDtypeStruct((S, D_FF), BF16),
            jax.ShapeDtypeStruct((S, D_FF), BF16),
            jax.ShapeDtypeStruct((S, D_MODEL), BF16),
            jax.ShapeDtypeStruct((S, D_MODEL), F32),
            jax.ShapeDtypeStruct((S, D_MODEL), BF16),
            jax.ShapeDtypeStruct((S, D_POOL), F32),
            jax.ShapeDtypeStruct((S, D_ATTN), F32),
            jax.ShapeDtypeStruct((SG_ROWS, D_MODEL), F32),
        ],
        in_specs=[tok(D_MODEL), tok(D_POOL), tok(D_ATTN), tok(D_MODEL), vec(D_POOL), vec(D_ATTN),
                  vec(D_MODEL), vec(D_MODEL), any_spec, any_spec, any_spec],
        out_specs=[tok(D_MODEL), tok(D_MODEL), tok(D_FF), tok(D_FF), tok(D_MODEL), tok(D_MODEL),
                   tok(D_MODEL), tok(D_POOL), tok(D_ATTN),
                   pl.BlockSpec((SG_ROWS, D_MODEL), lambda i: (0, 0))],
        scratch_shapes=[
            pltpu.VMEM((D_MODEL, D_MODEL), BF16),
            pltpu.VMEM((N_DEV, D_MODEL, fc), BF16),
            pltpu.VMEM((D_FF, D_MODEL), BF16),
            pltpu.VMEM((tm, D_FF), F32),
            pltpu.SemaphoreType.DMA((3,)),
        ],
        compiler_params=_params(("arbitrary",)),
    )(x, y_pool, y_attn, target, g_pool, g_attn, g2, gf, w_out, w_up, w_down)


def _wgrad(a, b, block_a, groups, name, travelling=(), gather=False, relations=None):
    n = len(travelling)
    S, ka = a.shape
    nb = b.shape[1]
    ts = min(1024, S)
    per = N_DEV // groups
    if block_a:
        ka //= groups
        blk = (ka // per, nb)
        a_spec = pl.BlockSpec((ts, ka), lambda g, s: (s, g))
        b_spec = pl.BlockSpec((ts, nb), lambda g, s: (s, 0))
    else:
        nb //= groups
        blk = (ka, nb // per)
        a_spec = pl.BlockSpec((ts, ka), lambda g, s: (s, 0))
        b_spec = pl.BlockSpec((ts, nb), lambda g, s: (s, g))
    steps = S // ts

    def body(a_ref, b_ref, *rest):
        o_ref, acc = rest[n], rest[2 * n + 1]
        g, s = pl.program_id(0), pl.program_id(1)

        def scatter():
            return _direct_exchange(rest[:n], rest[n + 1 : 2 * n + 1], rest[2 * n + 2 :], gather, relations)

        if n:
            @pl.when((g == 0) & (s == 0))
            def _():
                scatter()[0]()

        @pl.when(s == 0)
        def _():
            acc[...] = jnp.zeros_like(acc)

        acc[...] += _dot_tn(a_ref[...], b_ref[...])

        @pl.when(s == steps - 1)
        def _():
            for j in range(per):
                if block_a:
                    o_ref[j] = acc[blk[0] * j : blk[0] * (j + 1), :].astype(BF16)
                else:
                    o_ref[j] = acc[:, blk[1] * j : blk[1] * (j + 1)].astype(BF16)

        if n:
            @pl.when((g == groups - 1) & (s == steps - 1))
            def _():
                scatter()[1]()

    results = pl.pallas_call(
        body,
        name=name,
        grid=(groups, steps),
        out_shape=[jax.ShapeDtypeStruct((N_DEV,) + blk, BF16)] + _exchange_shapes(travelling, gather, relations),
        in_specs=[a_spec, b_spec] + [ANY_SPEC] * n,
        out_specs=[pl.BlockSpec((per,) + blk, lambda g, s: (g, 0, 0))] + [ANY_SPEC] * n,
        scratch_shapes=[pltpu.VMEM((ka, nb), F32)] + (_exchange_sems(n) if n else []),
        compiler_params=pltpu.CompilerParams(
            dimension_semantics=("arbitrary", "arbitrary"), vmem_limit_bytes=VMEM_LIMIT, has_side_effects=bool(n)
        ),
    )(a, b, *travelling)
    return results if n else results[0]


def _mixers_bwd(q, k, v, do, dyp, pooled, pool_w, scale, partials, relations):
    S = q.shape[0]
    T = ATTN_TILE
    nq = S // T
    n = len(partials)
    steps = D_ATTN // LANES

    def body(q_ref, k_ref, v_ref, do_ref, dyp_ref, pooled_ref, pw_ref, sc_ref, *rest):
        dq_ref, dk_ref, dv_ref, du_ref, dsc_ref, dpw_ref = rest[n : n + 6]
        dk_acc, dv_acc, carry_s = rest[2 * n + 6 : 2 * n + 9]
        step = pl.program_id(0)

        def scatter():
            sems = rest[2 * n + 9 :]
            return [
                _direct_exchange([rest[t]], [rest[n + 6 + t]], sems[3 * t : 3 * t + 3], False, relations[t])
                for t in range(n)
            ]

        @pl.when(step == 0)
        def _():
            for start, _ in scatter():
                start()

        _pool_bwd_group(dyp_ref, pooled_ref, pw_ref, sc_ref, du_ref, dsc_ref, dpw_ref, step)

        dk_acc[...] = jnp.zeros_like(dk_acc)
        dv_acc[...] = jnp.zeros_like(dv_acc)
        lane = lax.broadcasted_iota(jnp.int32, (T, LANES), 1)
        heads = (lane < HEAD_DIM, lane >= HEAD_DIM)
        after2, before2, causal = _band_ones(T)
        zero = jnp.zeros((T, 1), F32)

        def key_tile(k0):
            kb = k_ref[pl.ds(k0, T), :]
            k_rows = jnp.concatenate([jnp.where(m, kb, jnp.zeros_like(kb)) for m in heads], axis=0)
            return kb, v_ref[pl.ds(k0, T), :], k_rows

        def grads(items):
            das = [[[_dot_nt(doh, vb) for doh in qs[1]] for _, vb, _ in tiles] for qs, _, tiles, _, _, _, _ in items]
            probs, gs = [], []
            for (_, _, tiles, masks, carry_in, gates, _), da in zip(items, das):
                probs.append([[_weights(gates[b][h][0], gates[b][h][1], carry_in[b][h], masks[b]) for h in range(2)]
                              for b in range(len(tiles))])
                gs.append([[probs[-1][b][h] * da[b][h] for h in range(2)] for b in range(len(tiles))])
            before = [[[_dot(jnp.concatenate(_split2(g), axis=1), before2) for g in row] for row in item] for item in gs]
            dzs, lefts = [], []
            for i, (_, _, tiles, masks, _, gates, st) in enumerate(items):
                g_left, item_dz = [st[0], st[1]], [None] * len(tiles)
                for b in reversed(range(len(tiles))):
                    item_dz[b] = []
                    for h in range(2):
                        sig = jnp.exp(gates[b][h][0])
                        dz = gs[i][b][h] * (1.0 - sig) - sig * (g_left[h] + before[i][b][h])
                        if masks[b] is not None:
                            dz = jnp.where(masks[b], dz, 0.0)
                        item_dz[b].append(dz.astype(BF16))
                        g_left[h] = g_left[h] + jnp.sum(gs[i][b][h], axis=1, keepdims=True)
                dzs.append(item_dz)
                lefts.append(g_left)
            out = []
            for (_, _, tiles, _, _, _, st), item_dz, g_left in zip(items, dzs, lefts):
                dq = st[2] + _dot(
                    jnp.concatenate([dz for row in item_dz for dz in row], axis=1),
                    jnp.concatenate([k_rows for _, _, k_rows in tiles], axis=0),
                )
                out.append((g_left[0], g_left[1], dq))
            for (qs, starts, tiles, _, _, _, _), item_dz, item_probs in zip(items, dzs, probs):
                for b in range(len(tiles)):
                    dk_acc[pl.ds(starts[b], T), :] += _dot_tn(jnp.concatenate(item_dz[b], axis=0), qs[2])
                    dv_acc[pl.ds(starts[b], T), :] += _dot_tn(
                        jnp.concatenate([a.astype(BF16) for a in item_probs[b]], axis=0), qs[3]
                    )
            return out

        def more(st):
            return (st[0] > 0) & _reaches(jnp.maximum(st[1], st[2]))

        def q_tiles(qis, nb):
            masks = [causal] + [None] * (nb - 1)
            prepared, jobs = [], []
            for qi in qis:
                q0 = _tile_start(qi, T)
                qb = q_ref[pl.ds(q0, T), :]
                dob = do_ref[pl.ds(q0, T), :]
                qhs = [jnp.where(m, qb, jnp.zeros_like(qb)) for m in heads]
                dohs = [jnp.where(m, dob, 0.0).astype(BF16) for m in heads]
                queries = (qhs, dohs, jnp.concatenate(qhs, axis=0), jnp.concatenate(dohs, axis=0))
                starts = [_tile_start(qi - b, T) for b in range(nb)]
                tiles = [key_tile(k0) for k0 in starts]
                prepared.append((q0, queries, starts, tiles))
                jobs += [(qh, kb, mask) for (kb, _, _), mask in zip(tiles, masks) for qh in qhs]
            flat = _log_gates_of(jobs, after2)
            items = []
            for i, (qi, (q0, queries, starts, tiles)) in enumerate(zip(qis, prepared)):
                gates = [flat[2 * nb * i + 2 * b : 2 * nb * i + 2 * b + 2] for b in range(nb)]
                carries, carry_in = [zero, zero], []
                for b in range(nb):
                    carry_in.append(list(carries))
                    carries = [carries[h] + gates[b][h][2] for h in range(2)]
                st = (zero, zero, jnp.zeros((T, LANES), F32))
                if not isinstance(qi, int):
                    k_left = qi - (nb - 1)
                    qhs = queries[0]

                    def right_to_left(st, qhs=qhs):
                        kj = st[0] - 1
                        carry_s[0, kj] = st[1]
                        carry_s[1, kj] = st[2]
                        kb = k_ref[pl.ds(pl.multiple_of(kj * T, T), T), :]
                        sums = [jnp.sum(_log1m(_dot_nt(qh, kb), None)[0], axis=1, keepdims=True) for qh in qhs]
                        return kj, st[1] + sums[0], st[2] + sums[1]

                    k_first = lax.while_loop(more, right_to_left, (k_left, carries[0], carries[1]))[0]

                    def left_to_right(kj, st, queries=queries):
                        k0 = pl.multiple_of(kj * T, T)
                        tile = key_tile(k0)
                        left_gates = _log_gates(queries[0], [tile[0]], [None], after2)
                        carry = [[carry_s[0, kj], carry_s[1, kj]]]
                        return grads([(queries, [k0], [tile], [None], carry, left_gates, st)])[0]

                    st = lax.fori_loop(k_first, k_left, left_to_right, st)
                items.append((queries, starts, tiles, masks, carry_in, gates, st))
            for (q0, _, _, _), st in zip(prepared, grads(items)):
                dq_ref[pl.ds(q0, T), :] = (st[2] * (HEAD_DIM**-0.5)).astype(BF16)

        head_tiles = min(ATTN_WINDOW - 1, nq)
        for qi in range(head_tiles):
            q_tiles([qi], qi + 1)
        assert (nq - head_tiles) % ATTN_TOGETHER == 0

        def q_loop(i, carry):
            first = head_tiles + ATTN_TOGETHER * i
            q_tiles([first + j for j in range(ATTN_TOGETHER)], ATTN_WINDOW)
            return carry

        lax.fori_loop(0, (nq - head_tiles) // ATTN_TOGETHER, q_loop, 0)
        dk_ref[...] = dk_acc[...].astype(BF16)
        dv_ref[...] = dv_acc[...].astype(BF16)

        @pl.when(step == steps - 1)
        def _():
            for _, finish in scatter():
                finish()

    blk = pl.BlockSpec((S, LANES), lambda p: (0, p))
    sds = jax.ShapeDtypeStruct((S, D_ATTN), BF16)
    return pl.pallas_call(
        body,
        name="mixers_bwd",
        grid=(steps,),
        out_shape=[
            sds,
            sds,
            sds,
            jax.ShapeDtypeStruct((S, D_POOL), BF16),
            jax.ShapeDtypeStruct((8, D_POOL), F32),
            jax.ShapeDtypeStruct((N_GROUPS, GROUP_DIM, GROUP_DIM), BF16),
        ]
        + [_exchange_shapes([p], False, r)[0] for p, r in zip(partials, relations)],
        in_specs=[blk] * 6 + [_POOL_W_SPEC, _POOL_SCALE_SPEC] + [ANY_SPEC] * n,
        out_specs=[blk] * 4 + [pl.BlockSpec((8, GROUP_DIM), lambda p: (0, p)), _POOL_W_SPEC] + [ANY_SPEC] * n,
        scratch_shapes=[pltpu.VMEM((S, LANES), F32), pltpu.VMEM((S, LANES), F32), pltpu.VMEM((2, nq, T, 1), F32)]
        + [sem for _ in range(n) for sem in _exchange_sems(1)],
        compiler_params=pltpu.CompilerParams(
            dimension_semantics=("arbitrary",), vmem_limit_bytes=VMEM_LIMIT, has_side_effects=True
        ),
    )(q, k, v, do, dyp, pooled, pool_w, scale, *partials)


def _pool_bwd_group(dy_ref, pooled_ref, pw_ref, sc_ref, du_ref, dsc_ref, dpw_ref, g):
    S = dy_ref.shape[0]
    R = min(POOL_CHUNK, S)
    nc = S // R
    w = jnp.left_shift(jnp.int32(2), g)
    pw = pw_ref[0].astype(BF16)
    sc = sc_ref[...]
    d = lax.broadcasted_iota(jnp.int32, (R, R), 1) - lax.broadcasted_iota(jnp.int32, (R, R), 0)
    bt_cur = jnp.where((d >= 0) & (d < w), 1.0, 0.0).astype(BF16)
    dn = lax.broadcasted_iota(jnp.int32, (R, HALO), 1) + R - lax.broadcasted_iota(jnp.int32, (R, HALO), 0)
    bt_next = jnp.where(dn < w, 1.0, 0.0).astype(BF16)

    def per_count(dpl, r0):
        n = dpl.shape[0]
        count = jnp.minimum(r0 + lax.broadcasted_iota(jnp.int32, (n, 1), 0) + 1, w).astype(F32)
        return jnp.concatenate(_split2(dpl / count), axis=1)

    def chunk(r0, last, sums):
        dyv = dy_ref[pl.ds(r0, R), :]
        pooled = pooled_ref[pl.ds(r0, R), :]
        dmapped = (dyv * sc).astype(BF16)
        dsc = sums[0] + jnp.sum(dyv * _dot(pooled, pw), axis=0, keepdims=True)
        dpw = sums[1] + _dot_tn(pooled, dmapped)
        dpl = _dot_nt(dmapped, pw)
        wide = _dot(bt_cur, per_count(dpl, r0))
        if not last:
            nxt = pl.multiple_of(r0 + R, R)
            dpl_next = _dot_nt((dy_ref[pl.ds(nxt, HALO), :] * sc).astype(BF16), pw)
            wide = wide + _dot(bt_next, per_count(dpl_next, nxt))
        du_ref[pl.ds(r0, R), :] = (_lane_sum(wide, 2) - dpl).astype(BF16)
        return dsc, dpw

    sums = (jnp.zeros((1, GROUP_DIM), F32), jnp.zeros((GROUP_DIM, GROUP_DIM), F32))
    sums = lax.fori_loop(
        0, nc - 1, lambda c, s: chunk(pl.multiple_of(c * R, R), False, s), sums, unroll=POOL_UNROLL
    )
    dsc, dpw = chunk((nc - 1) * R, True, sums)
    dsc_ref[...] = jnp.zeros_like(dsc_ref)
    dsc_ref[0:1, :] = dsc
    dpw_ref[0] = dpw.astype(dpw_ref.dtype)


def _bwd_in(du, dq, dk, dv, w_in, x, dh1, g1):
    S = x.shape[0]
    tm = min(512, S)

    def body(du_ref, dq_ref, dk_ref, dv_ref, w_ref, x_ref, dh1_ref, g_ref, dx_ref, dproj_ref, dg_ref):
        @pl.when(pl.program_id(0) == 0)
        def _():
            dg_ref[...] = jnp.zeros_like(dg_ref)

        parts = (du_ref[...], dq_ref[...], dk_ref[...], dv_ref[...])
        dhn = jnp.zeros((tm, D_MODEL), F32)
        for j in range(N_DEV):
            piece = parts[j // 2][:, 256 * (j % 2) : 256 * (j % 2 + 1)]
            dproj_ref[:, 256 * j : 256 * (j + 1)] = piece
            dhn = dhn + _dot_nt(piece, w_ref[j])
        xv = x_ref[...]
        r = _rstd(xv)
        dxn, dg = _rms_bwd(dhn, xv * r, r, g_ref[...])
        dx_ref[...] = dh1_ref[...] + dxn
        dg_ref[0:1, :] += dg

    half = pl.BlockSpec((tm, D_POOL), lambda i: (i, 0))
    full = pl.BlockSpec((tm, D_MODEL), lambda i: (i, 0))
    return pl.pallas_call(
        body,
        name="bwd_in",
        grid=(S // tm,),
        out_shape=[
            jax.ShapeDtypeStruct((S, D_MODEL), F32),
            jax.ShapeDtypeStruct((S, D_IN_PROJ), BF16),
            jax.ShapeDtypeStruct((8, D_MODEL), F32),
        ],
        in_specs=[half, half, half, half,
                  pl.BlockSpec((N_DEV, D_MODEL, 256), lambda i: (0, 0, 0)),
                  full, full, pl.BlockSpec((1, D_MODEL), lambda i: (0, 0))],
        out_specs=[full, pl.BlockSpec((tm, D_IN_PROJ), lambda i: (i, 0)),
                   pl.BlockSpec((8, D_MODEL), lambda i: (0, 0))],
        compiler_params=_params(("arbitrary",)),
    )(du, dq, dk, dv, w_in, x, dh1, g1)


def _rows(a):
    a = a.reshape(-1, LANES)
    pad = (-a.shape[0]) % 8
    return jnp.pad(a, ((0, pad), (0, 0))) if pad else a


def kernel(x, norm1_g, w_in, pool_w, pool_scale, pool_out_g, attn_out_g, w_out, norm2_g, w_up, w_down, final_g, loss_target, m_norm1_g, m_w_in, m_pool_w, m_pool_scale, m_pool_out_g, m_attn_out_g, m_w_out, m_norm2_g, m_w_up, m_w_down, m_final_g, v_norm1_g, v_w_in, v_pool_w, v_pool_scale, v_pool_out_g, v_attn_out_g, v_w_out, v_norm2_g, v_w_up, v_w_down, v_final_g):
    S = x.shape[1]
    xs = x.reshape(S, D_MODEL)
    tgt = loss_target.reshape(S, D_MODEL)
    row = lambda a: a.reshape(1, -1)

    (w_in_g,) = _exchange([w_in.astype(BF16)], True, "gather_w_in")
    lanes = lambda a: a.reshape(-1, LANES)
    vector_sets = [
        [lanes(a) for a in (final_g, norm2_g, pool_out_g, attn_out_g, norm1_g, pool_scale)],
        [lanes(a) for a in (m_final_g, m_norm2_g, m_pool_out_g, m_attn_out_g, m_norm1_g, m_pool_scale)],
        [lanes(a) for a in (v_final_g, v_norm2_g, v_pool_out_g, v_attn_out_g, v_norm1_g, v_pool_scale)],
    ]
    hn, u_pool, q, k, v, *prepared = _fwd_in(xs, row(norm1_g), w_in_g, [w_out, w_up, w_down], vector_sets)
    shards, packed_vectors = prepared[:3], prepared[3:]
    y_attn, pooled, y_pool, w_out_g, w_up_g, w_down_g = _mixers_fwd(q, k, v, u_pool, pool_w, row(pool_scale), shards)
    w_out_full = w_out_g.reshape(D_MODEL, D_MODEL)
    w_down_full = w_down_g.reshape(D_FF, D_MODEL)
    mixed, hn2, act, dup, dh2b, dh1, dh1b, dyp, dya, sg = _mlp_fwd_bwd(
        xs, y_pool, y_attn, tgt, row(pool_out_g), row(attn_out_g), row(norm2_g), row(final_g),
        w_out_full, w_up_g, w_down_full,
    )
    gp_out = _wgrad(mixed, dh1b, True, 1, "wgrad_out")
    gp_down, land_out = _wgrad(act, dh2b, True, 2, "wgrad_down", [gp_out])
    early = (2, 4)
    late = tuple(k for k in ALL_RELATIONS if k not in early)
    gp_up, land_down_early = _wgrad(hn2, dup, False, 2, "wgrad_up", [gp_down], relations=early)
    dq, dk, dv, du, dsc, dpw, land_up, land_down_late = _mixers_bwd(
        q, k, v, dya, dyp, pooled, pool_w, row(pool_scale), [gp_up, gp_down], [None, late]
    )
    dx, dproj, dg1 = _bwd_in(du, dq, dk, dv, w_in_g, xs, dh1, row(norm1_g))

    partial_vectors = jnp.concatenate([_rows(p) for p in (sg[0], sg[1], sg[2], dg1[0], dsc[0], sg[3])], axis=0)
    flat_pool = lambda a: a.reshape(N_GROUPS * GROUP_DIM, GROUP_DIM)
    gp_in, landed_vectors, landed_pool = _wgrad(
        hn, dproj, False, 1, "wgrad_in", [partial_vectors, flat_pool(dpw)], gather=True
    )
    smalls = [
        (landed_vectors, [(*packed_vectors, None)]),
        (landed_pool, [(flat_pool(pool_w), flat_pool(m_pool_w), flat_pool(v_pool_w), None)]),
    ]
    tail = _reduce_adam_tail(
        [[land_out], [land_up], [land_down_late, land_down_early]], [w_out, w_up, w_down],
        [m_w_out, m_w_up, m_w_down], [v_w_out, v_w_up, v_w_down], gp_in, (w_in, m_w_in, v_w_in), smalls,
    )
    big = {name: tail[4 * t : 4 * t + 4] for t, name in enumerate(("w_out", "w_up", "w_down", "w_in"))}

    def unpack(vec, pw):
        out = {}
        for i, name in enumerate(("final_g", "norm2_g", "mix_g", "norm1_g", "pool_scale")):
            out[name] = vec[8 * i : 8 * i + 8].reshape(-1)
        out["pool_scale"] = out["pool_scale"][:D_POOL]
        out["pool_out_g"], out["attn_out_g"] = out["mix_g"][:D_POOL], out["mix_g"][D_POOL:]
        out["pool_w"] = pw.reshape(N_GROUPS, GROUP_DIM, GROUP_DIM)
        return out, vec[40, 0]

    small_out = [unpack(tail[16 + i], tail[20 + i]) for i in range(4)]
    loss = small_out[0][1]
    order = ("norm1_g", "w_in", "pool_w", "pool_scale", "pool_out_g", "attn_out_g", "w_out", "norm2_g", "w_up",
             "w_down", "final_g")
    outs = [loss, dx.reshape(1, S, D_MODEL)]
    for i in range(4):
        for name in order:
            outs.append(big[name][i] if name in big else small_out[i][0][name])
    return tuple(outs)
```

```python
import jax
import jax.numpy as jnp
from jax import lax
from jax.experimental import pallas as pl
from jax.experimental.pallas import tpu as pltpu

F32 = jnp.float32
BF16 = jnp.bfloat16
MESH = pl.DeviceIdType.MESH

N_DEV = 8
D_MODEL = 1024
D_POOL = 512
D_ATTN = 512
N_GROUPS = 4
GROUP_DIM = 128
HEAD_DIM = 64
D_FF = 4096
D_IN_PROJ = 2048
EPS = 1e-6
HALO = 16
ATTN_TILE = 128
LANES = 128
EXP_UNDERFLOW = -104.0

ADAM_LR = 0.001
ADAM_B1 = 0.9
ADAM_B2 = 0.999
ADAM_EPS = 1e-08
ADAM_WD = 0.01
ADAM_STEP = 10

VMEM_LIMIT = 56 * 1024 * 1024


def _params(semantics=None, vmem=VMEM_LIMIT):
    return pltpu.CompilerParams(dimension_semantics=semantics, vmem_limit_bytes=vmem)


def _dot(a, b):
    return jnp.dot(a, b, preferred_element_type=F32)


def _dot_nt(a, b):
    return lax.dot_general(a, b, (((1,), (1,)), ((), ())), preferred_element_type=F32)


def _dot_tn(a, b):
    return lax.dot_general(a, b, (((0,), (0,)), ((), ())), preferred_element_type=F32)


def _split2(x):
    hi = x.astype(BF16)
    lo = (x - hi.astype(F32)).astype(BF16)
    return hi, lo


def _split3(x):
    hi = x.astype(BF16)
    r = x - hi.astype(F32)
    mid = r.astype(BF16)
    lo = (r - mid.astype(F32)).astype(BF16)
    return hi, mid, lo


def _rstd(h):
    return lax.rsqrt(jnp.mean(h * h, axis=-1, keepdims=True) + EPS)


def _rms_bwd(dout, hhat, r, g):
    dg = jnp.sum(dout * hhat, axis=0, keepdims=True)
    dxh = dout * g
    dh = r * (dxh - hhat * jnp.mean(dxh * hhat, axis=-1, keepdims=True))
    return dh, dg


def _my_index():
    return 4 * lax.axis_index("x") + 2 * lax.axis_index("y") + lax.axis_index("c")


def _peer(k):
    x, y, c = lax.axis_index("x"), lax.axis_index("y"), lax.axis_index("c")
    px = 1 - x if (k >> 2) & 1 else x
    py = 1 - y if (k >> 1) & 1 else y
    pc = 1 - c if k & 1 else c
    return (px, py, pc), 4 * px + 2 * py + pc


N_PEERS = N_DEV - 1
ANY_SPEC = pl.BlockSpec(memory_space=pl.ANY)


def _exchange_sems(n):
    return [
        pltpu.SemaphoreType.DMA((n * N_PEERS,)),
        pltpu.SemaphoreType.DMA((n * N_PEERS,)),
        pltpu.SemaphoreType.DMA((n,)),
    ]


ALL_RELATIONS = tuple(range(N_DEV))


def _exchange_shapes(blocks, gather, relations=None):
    if gather:
        return [jax.ShapeDtypeStruct((N_DEV,) + b.shape, b.dtype) for b in blocks]
    slots = N_DEV if relations is None else len(relations)
    return [jax.ShapeDtypeStruct((slots,) + b.shape[1:], b.dtype) for b in blocks]


def _direct_exchange(ins, outs, sems, gather, relations=None):
    send_sems, recv_sems, local_sems = sems
    me = _my_index()
    own, sends, recvs = [], [], []
    for t in range(len(ins)):
        for slot, k in enumerate(ALL_RELATIONS if relations is None else relations):
            if k == 0:
                mine = outs[t].at[me if relations is None else slot]
                own.append(pltpu.make_async_copy(ins[t] if gather else ins[t].at[me], mine, local_sems.at[t]))
                continue
            peer, peer_idx = _peer(k)
            src = ins[t] if gather else ins[t].at[peer_idx]
            sent_to = outs[t].at[me if relations is None else slot]
            lands_in = outs[t].at[peer_idx if relations is None else slot]
            for dst, bucket in ((sent_to, sends), (lands_in, recvs)):
                bucket.append(
                    pltpu.make_async_remote_copy(
                        src_ref=src,
                        dst_ref=dst,
                        send_sem=send_sems.at[t * N_PEERS + k - 1],
                        recv_sem=recv_sems.at[t * N_PEERS + k - 1],
                        device_id=peer,
                        device_id_type=MESH,
                    )
                )

    def start():
        for cp in own + sends:
            cp.start()

    def finish():
        for cp in recvs:
            cp.wait_recv()
        for cp in sends:
            cp.wait_send()
        for cp in own:
            cp.wait()

    return start, finish


def _flip(a, bit):
    return a + bit - 2 * a * bit


def _two_level_gather(ins, outs, sems):
    send_sems, recv_sems, local_sems = sems
    x, y, c = lax.axis_index("x"), lax.axis_index("y"), lax.axis_index("c")
    me, sibling = (x, y, c), (x, y, 1 - c)
    x_nbr, y_nbr, diagonal = (1 - x, y, c), (x, 1 - y, c), (1 - x, 1 - y, c)
    relay_of = (_flip(x, 1 - c), _flip(y, c), c)
    relay_to = (_flip(x, c), _flip(y, 1 - c), c)

    def copy(t, k, block, to, from_input=False):
        slot = outs[t].at[4 * block[0] + 2 * block[1] + block[2]]
        return pltpu.make_async_remote_copy(
            src_ref=ins[t] if from_input else slot,
            dst_ref=slot,
            send_sem=send_sems.at[t * N_PEERS + k],
            recv_sem=recv_sems.at[t * N_PEERS + k],
            device_id=to,
            device_id_type=MESH,
        )

    arrays = range(len(ins))
    own = [pltpu.make_async_copy(ins[t], outs[t].at[4 * x + 2 * y + c], local_sems.at[t]) for t in arrays]
    first = [copy(t, k, me, to, True) for t in arrays for k, to in ((1, x_nbr), (2, y_nbr), (0, sibling))]
    landed = [copy(t, k, block, me) for t in arrays for k, block in ((1, x_nbr), (2, y_nbr))]
    relays = [copy(t, 3, relay_of, relay_to) for t in arrays]
    passed = [copy(t, 3 + k, block, sibling) for t in arrays for k, block in ((1, x_nbr), (2, y_nbr))]
    relayed = [(copy(t, 3, diagonal, me), copy(t, 6, diagonal, sibling)) for t in arrays]
    last = [copy(t, 0, sibling, me) for t in arrays]
    last += [copy(t, 3 + k, (*block[:2], 1 - c), me) for t in arrays for k, block in ((1, x_nbr), (2, y_nbr), (3, diagonal))]

    def start():
        for cp in own + first:
            cp.start()

    def relay():
        for cp in landed:
            cp.wait_recv()
        for cp in relays + passed:
            cp.start()

    def forward():
        for arrived, onward in relayed:
            arrived.wait_recv()
            onward.start()

    def finish():
        for cp in last:
            cp.wait_recv()
        for cp in first + relays + passed + [onward for _, onward in relayed]:
            cp.wait_send()
        for cp in own:
            cp.wait()

    return start, relay, forward, finish


N_CHIPS = 4
PAIR_SEMS = (N_CHIPS, N_CHIPS, N_CHIPS - 1, N_CHIPS - 1)


def _pair_reduce_scatter(gp_hbm, own, pair, summed, land, local_sem, sems):
    d2d_send, d2d_recv, ici_send, ici_recv = sems
    x, y, c = lax.axis_index("x"), lax.axis_index("y"), lax.axis_index("c")
    my_chip = 2 * x + y
    sibling = (x, y, 1 - c)
    chips = [(1 - x, y), (x, 1 - y), (1 - x, 1 - y)]
    local = [pltpu.make_async_copy(gp_hbm.at[2 * j + c], own.at[j], local_sem.at[0]) for j in range(N_CHIPS)]
    to_sibling = [
        pltpu.make_async_remote_copy(
            src_ref=gp_hbm.at[2 * j + 1 - c], dst_ref=pair.at[j], send_sem=d2d_send.at[j], recv_sem=d2d_recv.at[j],
            device_id=sibling, device_id_type=MESH,
        )
        for j in range(N_CHIPS)
    ]
    to_chips, from_chips = [], []
    for r, (px, py) in enumerate(chips):
        for dst, bucket in ((land.at[my_chip], to_chips), (land.at[2 * px + py], from_chips)):
            bucket.append(
                pltpu.make_async_remote_copy(
                    src_ref=summed.at[2 * px + py], dst_ref=dst, send_sem=ici_send.at[r], recv_sem=ici_recv.at[r],
                    device_id=(px, py, c), device_id_type=MESH,
                )
            )

    def start():
        for cp in local + to_sibling:
            cp.start()

    def middle():
        for cp in to_sibling:
            cp.wait_recv()
        pltpu.make_async_copy(gp_hbm.at[pl.ds(0, N_CHIPS)], own, local_sem.at[0]).wait()
        for j in range(N_CHIPS):
            summed[j] = (own[j].astype(F32) + pair[j].astype(F32)).astype(BF16)
        for cp in to_chips:
            cp.start()

    def finish():
        for cp in from_chips:
            cp.wait_recv()
        for cp in to_chips + to_sibling:
            cp.wait_send()
        land[my_chip] = summed[my_chip]

    return start, middle, finish


def _exchange(blocks, gather, name):
    n = len(blocks)

    def body(*refs):
        if gather:
            stages = _two_level_gather(refs[:n], refs[n : 2 * n], refs[2 * n :])
        else:
            stages = _direct_exchange(refs[:n], refs[n : 2 * n], refs[2 * n :], False)
        for stage in stages:
            stage()

    return pl.pallas_call(
        body,
        name=name,
        out_shape=_exchange_shapes(blocks, gather),
        in_specs=[ANY_SPEC] * n,
        out_specs=[ANY_SPEC] * n,
        scratch_shapes=_exchange_sems(n),
        compiler_params=pltpu.CompilerParams(has_side_effects=True),
    )(*blocks)


def _adam(w, g, m, v):
    m2 = ADAM_B1 * m + (1.0 - ADAM_B1) * g
    v2 = ADAM_B2 * v + (1.0 - ADAM_B2) * jnp.square(g)
    m_hat = m2 / (1.0 - ADAM_B1**ADAM_STEP)
    v_hat = v2 / (1.0 - ADAM_B2**ADAM_STEP)
    delta = -ADAM_LR * (m_hat / (jnp.sqrt(v_hat) + ADAM_EPS) + ADAM_WD * w)
    return delta, m2, v2


TAIL_STEPS = 8


def _reduce_adam_tail(lands, ws, ms, vs, gp_last, last, smalls):
    nw, ns = len(ws), len(smalls)
    n_lands = sum(len(arrays) for arrays in lands)
    tiles = [w.shape[0] // TAIL_STEPS for w in ws]
    blk = gp_last.shape[1:]
    entries = [(i, e) for i, (_, es) in enumerate(smalls) for e in es]
    n_params = 3 * sum(e[0] is not None for _, e in entries)
    n_small_out = sum(4 if e[0] is not None else 1 for _, e in entries)

    def body(*refs):
        gp_hbm = refs[0]
        small_land = refs[1 : 1 + ns]
        land_refs = list(refs[1 + ns : 1 + ns + n_lands])
        refs = refs[1 + ns + n_lands - nw :]
        w_refs, m_refs, v_refs = (refs[nw * (i + 1) : nw * (i + 2)] for i in range(3))
        last_refs = refs[4 * nw : 4 * nw + 3]
        refs = refs[4 * nw + 3 :]
        param_refs = refs[:n_params]
        outs = refs[n_params:]
        big_out = outs[: 4 * nw]
        last_out = outs[4 * nw : 4 * nw + 4]
        small_out = outs[4 * nw + 4 : 4 * nw + 4 + n_small_out]
        scratch = outs[4 * nw + 4 + n_small_out :]
        pair_bufs = scratch[:4]
        local_sem = scratch[4]
        pair_sems = scratch[5:9]
        step = pl.program_id(0)

        def pair():
            return _pair_reduce_scatter(gp_hbm, *pair_bufs, local_sem, pair_sems)

        @pl.when(step == 0)
        def _():
            pair()[0]()

        @pl.when(step == 1)
        def _():
            pair()[1]()

        for t in range(nw):
            g = None
            for _ in lands[t]:
                landed = land_refs.pop(0)
                for s in range(landed.shape[0]):
                    part = landed[s].astype(F32)
                    g = part if g is None else g + part
            for ref, val in zip(big_out[4 * t : 4 * t + 4], (g,) + _adam(w_refs[t][...], g, m_refs[t][...], v_refs[t][...])):
                ref[...] = val

        @pl.when(step == TAIL_STEPS - 1)
        def _():
            pair()[2]()
            land = pair_bufs[3]
            g = land[0].astype(F32)
            for chip in range(1, N_CHIPS):
                g = g + land[chip].astype(F32)
            for ref, val in zip(last_out, (g,) + _adam(last_refs[0][...], g, last_refs[1][...], last_refs[2][...])):
                ref[...] = val
            sums = []
            for i in range(ns):
                g = small_land[i][0].astype(F32)
                for s in range(1, N_DEV):
                    g = g + small_land[i][s].astype(F32)
                sums.append(g)
            params, results = list(param_refs), list(small_out)
            for i, (w, _, _, where) in entries:
                g = sums[i]
                if where is not None:
                    shape = w.shape if w is not None else where[2:]
                    g = g[where[0] : where[0] + shape[0], where[1] : where[1] + shape[1]]
                if w is None:
                    results.pop(0)[...] = g
                    continue
                w_ref, m_ref, v_ref = params[:3]
                del params[:3]
                for val in (g,) + _adam(w_ref[...], g, m_ref[...], v_ref[...]):
                    results.pop(0)[...] = val

    def tile(t):
        return pl.BlockSpec((tiles[t], ws[t].shape[1]), lambda i: (i, 0))

    def land_tile(t, landed):
        return pl.BlockSpec((landed.shape[0], tiles[t], ws[t].shape[1]), lambda i: (0, i, 0))

    def whole(shape):
        return pl.BlockSpec(shape, lambda step: (0, 0))

    big_sds = [jax.ShapeDtypeStruct(ws[t].shape, F32) for t in range(nw) for _ in range(4)]
    params, small_shapes = [], []
    for _, (w, m, v, where) in entries:
        if w is None:
            small_shapes.append(tuple(where[2:]))
        else:
            params += [w, m, v]
            small_shapes += [w.shape] * 4
    return pl.pallas_call(
        body,
        name="reduce_adam_tail",
        grid=(TAIL_STEPS,),
        out_shape=big_sds
        + [jax.ShapeDtypeStruct(blk, F32)] * 4
        + [jax.ShapeDtypeStruct(shape, F32) for shape in small_shapes],
        in_specs=[ANY_SPEC]
        + [pl.BlockSpec(smalls[i][0].shape, lambda step: (0, 0, 0)) for i in range(ns)]
        + [land_tile(t, landed) for t in range(nw) for landed in lands[t]]
        + [tile(t) for _ in range(3) for t in range(nw)]
        + [whole(blk)] * 3
        + [whole(p.shape) for p in params],
        out_specs=[tile(t) for t in range(nw) for _ in range(4)]
        + [whole(blk)] * 4
        + [whole(shape) for shape in small_shapes],
        scratch_shapes=[pltpu.VMEM((N_CHIPS,) + blk, BF16)] * 4
        + [pltpu.SemaphoreType.DMA((1,))]
        + [pltpu.SemaphoreType.DMA((count,)) for count in PAIR_SEMS],
        compiler_params=pltpu.CompilerParams(
            dimension_semantics=("arbitrary",), vmem_limit_bytes=VMEM_LIMIT, has_side_effects=True
        ),
    )(gp_last, *[s[0] for s in smalls], *[a for arrays in lands for a in arrays], *ws, *ms, *vs, *last, *params)


VECTOR_ROWS = 48


def _fwd_in(x, g1, w_in, shards, vector_sets):
    S = x.shape[0]
    tm = min(512, S)
    steps = S // tm
    n = len(shards)
    nv = sum(len(vs) for vs in vector_sets)

    def body(x_ref, g_ref, w_ref, *rest):
        hn_ref, u_ref, q_ref, k_ref, v_ref = rest[n + nv : n + nv + 5]
        for src, dst in zip(rest[:n], rest[n + nv + 5 :]):
            dst[...] = src[...].astype(BF16)
        vectors = list(rest[n : n + nv])
        for vs, packed in zip(vector_sets, rest[2 * n + nv + 5 :]):
            at = 0
            for _ in vs:
                ref = vectors.pop(0)
                packed[at : at + ref.shape[0], :] = ref[...]
                at += ref.shape[0]
            packed[at:, :] = jnp.zeros((VECTOR_ROWS - at, LANES), F32)
        xv = x_ref[...]
        hn = (xv * _rstd(xv) * g_ref[...]).astype(BF16)
        hn_ref[...] = hn
        outs = (u_ref, q_ref, k_ref, v_ref)
        for j in range(N_DEV):
            p = _dot(hn, w_ref[j])
            cols = slice(256 * (j % 2), 256 * (j % 2 + 1))
            if j // 2 == 0:
                u_ref[:, cols] = p
            elif j // 2 == 1:
                q_ref[:, cols] = (p * (HEAD_DIM**-0.5)).astype(BF16)
            else:
                outs[j // 2][:, cols] = p.astype(BF16)

    half = pl.BlockSpec((tm, D_POOL), lambda i: (i, 0))
    shard_tiles = [pl.BlockSpec((s.shape[0] // steps, s.shape[1]), lambda i: (i, 0)) for s in shards]
    return pl.pallas_call(
        body,
        name="fwd_in",
        grid=(steps,),
        out_shape=[
            jax.ShapeDtypeStruct((S, D_MODEL), BF16),
            jax.ShapeDtypeStruct((S, D_POOL), F32),
            jax.ShapeDtypeStruct((S, D_ATTN), BF16),
            jax.ShapeDtypeStruct((S, D_ATTN), BF16),
            jax.ShapeDtypeStruct((S, D_ATTN), BF16),
        ]
        + [jax.ShapeDtypeStruct(s.shape, BF16) for s in shards]
        + [jax.ShapeDtypeStruct((VECTOR_ROWS, LANES), F32)] * len(vector_sets),
        in_specs=[
            pl.BlockSpec((tm, D_MODEL), lambda i: (i, 0)),
            pl.BlockSpec((1, D_MODEL), lambda i: (0, 0)),
            pl.BlockSpec((N_DEV, D_MODEL, 256), lambda i: (0, 0, 0)),
        ]
        + shard_tiles
        + [pl.BlockSpec(v.shape, lambda i: (0, 0)) for vs in vector_sets for v in vs],
        out_specs=[pl.BlockSpec((tm, D_MODEL), lambda i: (i, 0)), half, half, half, half]
        + shard_tiles
        + [pl.BlockSpec((VECTOR_ROWS, LANES), lambda i: (0, 0))] * len(vector_sets),
        compiler_params=_params(("arbitrary",)),
    )(x, g1, w_in, *shards, *[v for vs in vector_sets for v in vs])


POOL_CHUNK = 256
POOL_UNROLL = 5
_POOL_W_SPEC = pl.BlockSpec((1, GROUP_DIM, GROUP_DIM), lambda g: (g, 0, 0))
_POOL_SCALE_SPEC = pl.BlockSpec((1, GROUP_DIM), lambda g: (0, g))


def _lane_sum(wide, n):
    out = wide[:, :GROUP_DIM]
    for i in range(1, n):
        out = out + wide[:, GROUP_DIM * i : GROUP_DIM * (i + 1)]
    return out


def _pool_fwd_group(u_ref, pw_ref, sc_ref, pooled_ref, y_ref, g):
    S = u_ref.shape[0]
    R = min(POOL_CHUNK, S)
    w = jnp.left_shift(jnp.int32(2), g)
    d = lax.broadcasted_iota(jnp.int32, (R, R), 0) - lax.broadcasted_iota(jnp.int32, (R, R), 1)
    b_cur = jnp.where((d >= 0) & (d < w), 1.0, 0.0).astype(BF16)
    dp = lax.broadcasted_iota(jnp.int32, (R, HALO), 0) + HALO - lax.broadcasted_iota(jnp.int32, (R, HALO), 1)
    b_prev = jnp.where(dp < w, 1.0, 0.0).astype(BF16)
    pw = pw_ref[0].astype(BF16)
    sc = sc_ref[...]

    def chunk(r0, first):
        cur = u_ref[pl.ds(r0, R), :]
        wide = _dot(b_cur, jnp.concatenate(_split3(cur), axis=1))
        if not first:
            prev = u_ref[pl.ds(pl.multiple_of(r0 - HALO, HALO), HALO), :]
            wide = wide + _dot(b_prev, jnp.concatenate(_split3(prev), axis=1))
        count = jnp.minimum(r0 + lax.broadcasted_iota(jnp.int32, (R, 1), 0) + 1, w).astype(F32)
        pooled = (_lane_sum(wide, 3) / count - cur).astype(BF16)
        pooled_ref[pl.ds(r0, R), :] = pooled
        y_ref[pl.ds(r0, R), :] = _dot(pooled, pw) * sc

    chunk(0, True)

    def rest(c, carry):
        chunk(pl.multiple_of(c * R, R), False)
        return carry

    lax.fori_loop(1, S // R, rest, 0, unroll=POOL_UNROLL)


ATTN_WINDOW = 3
ATTN_TOGETHER = 3
ATTN_TOGETHER_FWD = 6


def _band_ones(T):
    row = lax.broadcasted_iota(jnp.int32, (T, T), 0)
    col = lax.broadcasted_iota(jnp.int32, (T, T), 1)
    after = jnp.where(row > col, 1.0, 0.0).astype(BF16)
    before = jnp.where(row < col, 1.0, 0.0).astype(BF16)
    return jnp.concatenate([after, after], axis=0), jnp.concatenate([before, before], axis=0), col < row


def _log1m(z, mask):
    sp = jnp.log(1.0 + jnp.exp(-jnp.abs(z)))
    l1m = -jnp.maximum(z, 0.0) - sp
    return (l1m if mask is None else jnp.where(mask, l1m, 0.0)), sp


def _log_gates_of(jobs, after2):
    zs = [_dot_nt(qh, kb) for qh, kb, _ in jobs]
    terms = []
    for z, (_, _, mask) in zip(zs, jobs):
        l1m, sp = _log1m(z, mask)
        terms.append(
            (jnp.minimum(z, 0.0) - sp, jnp.concatenate(_split2(l1m), axis=1), jnp.sum(l1m, axis=1, keepdims=True))
        )
    return [(ls, _dot(split, after2), total) for ls, split, total in terms]


def _log_gates(qhs, kbs, masks, after2):
    flat = _log_gates_of([(qh, kb, mask) for kb, mask in zip(kbs, masks) for qh in qhs], after2)
    return [flat[len(qhs) * b : len(qhs) * (b + 1)] for b in range(len(kbs))]


def _tile_start(index, T):
    return index * T if isinstance(index, int) else pl.multiple_of(index * T, T)


def _weights(ls, tail, carry, mask):
    a = jnp.exp(ls + tail + carry)
    return a if mask is None else jnp.where(mask, a, 0.0)


def _reaches(carry):
    return jnp.max(carry) > EXP_UNDERFLOW


def _mixers_fwd(q, k, v, u, pool_w, scale, shards):
    S = q.shape[0]
    T = ATTN_TILE
    nq = S // T
    n = len(shards)
    steps = D_ATTN // LANES

    def body(q_ref, k_ref, v_ref, u_ref, pw_ref, sc_ref, *rest):
        o_ref, pooled_ref, yp_ref = rest[n : n + 3]
        step = pl.program_id(0)

        def gather():
            return _two_level_gather(rest[:n], rest[n + 3 : 2 * n + 3], rest[2 * n + 3 :])

        @pl.when(step == 0)
        def _():
            gather()[0]()

        @pl.when(step == steps // 2)
        def _():
            gather()[1]()

        _pool_fwd_group(u_ref, pw_ref, sc_ref, pooled_ref, yp_ref, step)

        lane = lax.broadcasted_iota(jnp.int32, (T, LANES), 1)
        heads = (lane < HEAD_DIM, lane >= HEAD_DIM)
        after2, _, causal = _band_ones(T)
        zero = jnp.zeros((T, 1), F32)

        def key_tile(k0):
            vb = v_ref[pl.ds(k0, T), :]
            return k_ref[pl.ds(k0, T), :], jnp.concatenate([jnp.where(m, vb, jnp.zeros_like(vb)) for m in heads], axis=0)

        def q_tiles(qis, nb):
            queries, jobs = [], []
            for qi in qis:
                q0 = _tile_start(qi, T)
                qb = q_ref[pl.ds(q0, T), :]
                qhs = [jnp.where(m, qb, jnp.zeros_like(qb)) for m in heads]
                tiles = [key_tile(_tile_start(qi - b, T)) for b in range(nb)]
                queries.append((q0, qhs, [v_rows for _, v_rows in tiles]))
                jobs += [(qh, kb, causal if b == 0 else None) for b, (kb, _) in enumerate(tiles) for qh in qhs]
            gates = _log_gates_of(jobs, after2)
            states = []
            for i, (q0, qhs, values) in enumerate(queries):
                carries, probs = [zero, zero], []
                for b in range(nb):
                    for h in range(2):
                        ls, tail, total = gates[2 * nb * i + 2 * b + h]
                        probs.append(_weights(ls, tail, carries[h], causal if b == 0 else None).astype(BF16))
                        carries[h] = carries[h] + total
                states.append((carries, _dot(jnp.concatenate(probs, axis=1), jnp.concatenate(values, axis=0))))

            def more(st):
                return (st[0] > 0) & _reaches(jnp.maximum(st[1], st[2]))

            for qi, (q0, qhs, _), (carries, acc) in zip(qis, queries, states):

                def k_step(st, qhs=qhs):
                    kj = st[0] - 1
                    kb, v_rows = key_tile(pl.multiple_of(kj * T, T))
                    new, probs = [], []
                    for (ls, tail, total), carry in zip(_log_gates(qhs, [kb], [None], after2)[0], st[1:3]):
                        probs.append(_weights(ls, tail, carry, None).astype(BF16))
                        new.append(carry + total)
                    return kj, new[0], new[1], st[3] + _dot(jnp.concatenate(probs, axis=1), v_rows)

                if not isinstance(qi, int):
                    acc = lax.while_loop(more, k_step, (qi - (nb - 1), carries[0], carries[1], acc))[3]
                o_ref[pl.ds(q0, T), :] = acc

        head_tiles = min(ATTN_WINDOW - 1, nq)
        for qi in range(head_tiles):
            q_tiles([qi], qi + 1)
        assert (nq - head_tiles) % ATTN_TOGETHER_FWD == 0

        def q_loop(i, carry):
            first = head_tiles + ATTN_TOGETHER_FWD * i
            q_tiles([first + j for j in range(ATTN_TOGETHER_FWD)], ATTN_WINDOW)
            return carry

        lax.fori_loop(0, (nq - head_tiles) // ATTN_TOGETHER_FWD, q_loop, 0)

        @pl.when(step == steps - 1)
        def _():
            _, _, forward, finish = gather()
            forward()
            finish()

    blk = pl.BlockSpec((S, LANES), lambda p: (0, p))
    return pl.pallas_call(
        body,
        name="mixers_fwd",
        grid=(steps,),
        out_shape=[
            jax.ShapeDtypeStruct((S, D_ATTN), F32),
            jax.ShapeDtypeStruct((S, D_POOL), BF16),
            jax.ShapeDtypeStruct((S, D_POOL), F32),
        ]
        + _exchange_shapes(shards, True),
        in_specs=[blk, blk, blk, blk, _POOL_W_SPEC, _POOL_SCALE_SPEC] + [ANY_SPEC] * n,
        out_specs=[blk, blk, blk] + [ANY_SPEC] * n,
        scratch_shapes=_exchange_sems(n),
        compiler_params=pltpu.CompilerParams(
            dimension_semantics=("arbitrary",), vmem_limit_bytes=VMEM_LIMIT, has_side_effects=True
        ),
    )(q, k, v, u, pool_w, scale, *shards)


MLP_AHEAD = 1
SG_ROWS = 8


def _mlp_fwd_bwd(x, y_pool, y_attn, target, g_pool, g_attn, g2, gf, w_out, w_up, w_down):
    S = x.shape[0]
    tm = min(256, S)
    fc = D_FF // N_DEV

    def body(x_ref, yp_ref, ya_ref, t_ref, gp_ref, ga_ref, g2_ref, gf_ref, wo_hbm, wu_hbm, wd_hbm,
             mixed_ref, hn2_ref, act_ref, dup_ref, dh2_ref, dh1_ref, dh1b_ref, dyp_ref, dya_ref, sg_ref,
             wo, wu, wd, up_s, sems):
        @pl.when(pl.program_id(0) == 0)
        def _():
            copies = [
                pltpu.make_async_copy(wo_hbm, wo, sems.at[0]),
                pltpu.make_async_copy(wu_hbm, wu, sems.at[1]),
                pltpu.make_async_copy(wd_hbm, wd, sems.at[2]),
            ]
            for cp in copies:
                cp.start()
            for cp in copies:
                cp.wait()
            sg_ref[...] = jnp.zeros_like(sg_ref)

        gp, ga, g2v, gfv = gp_ref[...], ga_ref[...], g2_ref[...], gf_ref[...]
        yp, ya = yp_ref[...], ya_ref[...]
        rp, ra = _rstd(yp), _rstd(ya)
        yph, yah = yp * rp, ya * ra
        mixed = jnp.concatenate([(yph * gp).astype(BF16), (yah * ga).astype(BF16)], axis=1)
        mixed_ref[...] = mixed
        h1 = x_ref[...] + _dot(mixed, wo[...])
        r2 = _rstd(h1)
        h1h = h1 * r2
        hn2 = (h1h * g2v).astype(BF16)
        hn2_ref[...] = hn2
        h2 = h1
        ups = [_dot(hn2, wu[j]) for j in range(MLP_AHEAD)]
        for j in range(N_DEV):
            cols = slice(fc * j, fc * (j + 1))
            if j + MLP_AHEAD < N_DEV:
                ups.append(_dot(hn2, wu[j + MLP_AHEAD]))
            up = ups.pop(0)
            up_s[:, cols] = up
            act = jnp.square(jnp.maximum(up, 0.0)).astype(BF16)
            act_ref[:, cols] = act
            h2 = h2 + _dot(act, wd[cols, :])
        rf = _rstd(h2)
        h2h = h2 * rf
        diff = h2h * gfv - t_ref[...]
        loss_rows = 0.5 * jnp.mean(diff * diff, axis=-1, keepdims=True)
        dy = diff * (1.0 / D_MODEL)
        dh2, dgf = _rms_bwd(dy, h2h, rf, gfv)
        dh2b = dh2.astype(BF16)
        dh2_ref[...] = dh2b
        dhn2 = jnp.zeros((tm, D_MODEL), F32)
        dacts = [_dot_nt(dh2b, wd[fc * j : fc * (j + 1), :]) for j in range(MLP_AHEAD)]
        for j in range(N_DEV):
            cols = slice(fc * j, fc * (j + 1))
            if j + MLP_AHEAD < N_DEV:
                dacts.append(_dot_nt(dh2b, wd[fc * (j + MLP_AHEAD) : fc * (j + MLP_AHEAD + 1), :]))
            dup = (dacts.pop(0) * (2.0 * jnp.maximum(up_s[:, cols], 0.0))).astype(BF16)
            dup_ref[:, cols] = dup
            dhn2 = dhn2 + _dot_nt(dup, wu[j])
        dh1n, dg2 = _rms_bwd(dhn2, h1h, r2, g2v)
        dh1 = dh2 + dh1n
        dh1_ref[...] = dh1
        dh1b = dh1.astype(BF16)
        dh1b_ref[...] = dh1b
        dmix = _dot_nt(dh1b, wo[...])
        dyp, dgp = _rms_bwd(dmix[:, :D_POOL], yph, rp, gp)
        dya, dga = _rms_bwd(dmix[:, D_POOL:], yah, ra, ga)
        dyp_ref[...] = dyp
        dya_ref[...] = dya
        sg_ref[0:1, :] += dgf
        sg_ref[1:2, :] += dg2
        sg_ref[2:3, :] += jnp.concatenate([dgp, dga], axis=1)
        sg_ref[3:4, :] += jnp.broadcast_to(jnp.sum(loss_rows, axis=0, keepdims=True), (1, D_MODEL))

    def tok(n):
        return pl.BlockSpec((tm, n), lambda i: (i, 0))

    def vec(n):
        return pl.BlockSpec((1, n), lambda i: (0, 0))

    any_spec = pl.BlockSpec(memory_space=pl.ANY)
    return pl.pallas_call(
        body,
        name="mlp_fwd_bwd",
        grid=(S // tm,),
        out_shape=[
            jax.ShapeDtypeStruct((S, D_MODEL), BF16),
            jax.ShapeDtypeStruct((S, D_MODEL), BF16),
            jax.ShapeDtypeStruct((S, D_FF), BF16),
            jax.ShapeDtypeStruct((S, D_FF), BF16),
            jax.ShapeDtypeStruct((S, D_MODEL), BF16),
            jax.ShapeDtypeStruct((S, D_MODEL), F32),
            jax.ShapeDtypeStruct((S, D_MODEL), BF16),
            jax.ShapeDtypeStruct((S, D_POOL), F32),
            jax.ShapeDtypeStruct((S, D_ATTN), F32),
            jax.ShapeDtypeStruct((SG_ROWS, D_MODEL), F32),
        ],
        in_specs=[tok(D_MODEL), tok(D_POOL), tok(D_ATTN), tok(D_MODEL), vec(D_POOL), vec(D_ATTN),
                  vec(D_MODEL), vec(D_MODEL), any_spec, any_spec, any_spec],
        out_specs=[tok(D_MODEL), tok(D_MODEL), tok(D_FF), tok(D_FF), tok(D_MODEL), tok(D_MODEL),
                   tok(D_MODEL), tok(D_POOL), tok(D_ATTN),
                   pl.BlockSpec((SG_ROWS, D_MODEL), lambda i: (0, 0))],
        scratch_shapes=[
            pltpu.VMEM((D_MODEL, D_MODEL), BF16),
            pltpu.VMEM((N_DEV, D_MODEL, fc), BF16),
            pltpu.VMEM((D_FF, D_MODEL), BF16),
            pltpu.VMEM((tm, D_FF), F32),
            pltpu.SemaphoreType.DMA((3,)),
        ],
        compiler_params=_params(("arbitrary",)),
    )(x, y_pool, y_attn, target, g_pool, g_attn, g2, gf, w_out, w_up, w_down)


def _wgrad(a, b, block_a, groups, name, travelling=(), gather=False, relations=None):
    n = len(travelling)
    S, ka = a.shape
    nb = b.shape[1]
    ts = min(1024, S)
    per = N_DEV // groups
    if block_a:
        ka //= groups
        blk = (ka // per, nb)
        a_spec = pl.BlockSpec((ts, ka), lambda g, s: (s, g))
        b_spec = pl.BlockSpec((ts, nb), lambda g, s: (s, 0))
    else:
        nb //= groups
        blk = (ka, nb // per)
        a_spec = pl.BlockSpec((ts, ka), lambda g, s: (s, 0))
        b_spec = pl.BlockSpec((ts, nb), lambda g, s: (s, g))
    steps = S // ts

    def body(a_ref, b_ref, *rest):
        o_ref, acc = rest[n], rest[2 * n + 1]
        g, s = pl.program_id(0), pl.program_id(1)

        def scatter():
            return _direct_exchange(rest[:n], rest[n + 1 : 2 * n + 1], rest[2 * n + 2 :], gather, relations)

        if n:
            @pl.when((g == 0) & (s == 0))
            def _():
                scatter()[0]()

        @pl.when(s == 0)
        def _():
            acc[...] = jnp.zeros_like(acc)

        acc[...] += _dot_tn(a_ref[...], b_ref[...])

        @pl.when(s == steps - 1)
        def _():
            for j in range(per):
                if block_a:
                    o_ref[j] = acc[blk[0] * j : blk[0] * (j + 1), :].astype(BF16)
                else:
                    o_ref[j] = acc[:, blk[1] * j : blk[1] * (j + 1)].astype(BF16)

        if n:
            @pl.when((g == groups - 1) & (s == steps - 1))
            def _():
                scatter()[1]()

    results = pl.pallas_call(
        body,
        name=name,
        grid=(groups, steps),
        out_shape=[jax.ShapeDtypeStruct((N_DEV,) + blk, BF16)] + _exchange_shapes(travelling, gather, relations),
        in_specs=[a_spec, b_spec] + [ANY_SPEC] * n,
        out_specs=[pl.BlockSpec((per,) + blk, lambda g, s: (g, 0, 0))] + [ANY_SPEC] * n,
        scratch_shapes=[pltpu.VMEM((ka, nb), F32)] + (_exchange_sems(n) if n else []),
        compiler_params=pltpu.CompilerParams(
            dimension_semantics=("arbitrary", "arbitrary"), vmem_limit_bytes=VMEM_LIMIT, has_side_effects=bool(n)
        ),
    )(a, b, *travelling)
    return results if n else results[0]


def _mixers_bwd(q, k, v, do, dyp, pooled, pool_w, scale, partials, relations):
    S = q.shape[0]
    T = ATTN_TILE
    nq = S // T
    n = len(partials)
    steps = D_ATTN // LANES

    def body(q_ref, k_ref, v_ref, do_ref, dyp_ref, pooled_ref, pw_ref, sc_ref, *rest):
        dq_ref, dk_ref, dv_ref, du_ref, dsc_ref, dpw_ref = rest[n : n + 6]
        dk_acc, dv_acc, carry_s = rest[2 * n + 6 : 2 * n + 9]
        step = pl.program_id(0)

        def scatter():
            sems = rest[2 * n + 9 :]
            return [
                _direct_exchange([rest[t]], [rest[n + 6 + t]], sems[3 * t : 3 * t + 3], False, relations[t])
                for t in range(n)
            ]

        @pl.when(step == 0)
        def _():
            for start, _ in scatter():
                start()

        _pool_bwd_group(dyp_ref, pooled_ref, pw_ref, sc_ref, du_ref, dsc_ref, dpw_ref, step)

        dk_acc[...] = jnp.zeros_like(dk_acc)
        dv_acc[...] = jnp.zeros_like(dv_acc)
        lane = lax.broadcasted_iota(jnp.int32, (T, LANES), 1)
        heads = (lane < HEAD_DIM, lane >= HEAD_DIM)
        after2, before2, causal = _band_ones(T)
        zero = jnp.zeros((T, 1), F32)

        def key_tile(k0):
            kb = k_ref[pl.ds(k0, T), :]
            k_rows = jnp.concatenate([jnp.where(m, kb, jnp.zeros_like(kb)) for m in heads], axis=0)
            return kb, v_ref[pl.ds(k0, T), :], k_rows

        def grads(items):
            das = [[[_dot_nt(doh, vb) for doh in qs[1]] for _, vb, _ in tiles] for qs, _, tiles, _, _, _, _ in items]
            probs, gs = [], []
            for (_, _, tiles, masks, carry_in, gates, _), da in zip(items, das):
                probs.append([[_weights(gates[b][h][0], gates[b][h][1], carry_in[b][h], masks[b]) for h in range(2)]
                              for b in range(len(tiles))])
                gs.append([[probs[-1][b][h] * da[b][h] for h in range(2)] for b in range(len(tiles))])
            before = [[[_dot(jnp.concatenate(_split2(g), axis=1), before2) for g in row] for row in item] for item in gs]
            dzs, lefts = [], []
            for i, (_, _, tiles, masks, _, gates, st) in enumerate(items):
                g_left, item_dz = [st[0], st[1]], [None] * len(tiles)
                for b in reversed(range(len(tiles))):
                    item_dz[b] = []
                    for h in range(2):
                        sig = jnp.exp(gates[b][h][0])
                        dz = gs[i][b][h] * (1.0 - sig) - sig * (g_left[h] + before[i][b][h])
                        if masks[b] is not None:
                            dz = jnp.where(masks[b], dz, 0.0)
                        item_dz[b].append(dz.astype(BF16))
                        g_left[h] = g_left[h] + jnp.sum(gs[i][b][h], axis=1, keepdims=True)
                dzs.append(item_dz)
                lefts.append(g_left)
            out = []
            for (_, _, tiles, _, _, _, st), item_dz, g_left in zip(items, dzs, lefts):
                dq = st[2] + _dot(
                    jnp.concatenate([dz for row in item_dz for dz in row], axis=1),
                    jnp.concatenate([k_rows for _, _, k_rows in tiles], axis=0),
                )
                out.append((g_left[0], g_left[1], dq))
            for (qs, starts, tiles, _, _, _, _), item_dz, item_probs in zip(items, dzs, probs):
                for b in range(len(tiles)):
                    dk_acc[pl.ds(starts[b], T), :] += _dot_tn(jnp.concatenate(item_dz[b], axis=0), qs[2])
                    dv_acc[pl.ds(starts[b], T), :] += _dot_tn(
                        jnp.concatenate([a.astype(BF16) for a in item_probs[b]], axis=0), qs[3]
                    )
            return out

        def more(st):
            return (st[0] > 0) & _reaches(jnp.maximum(st[1], st[2]))

        def q_tiles(qis, nb):
            masks = [causal] + [None] * (nb - 1)
            prepared, jobs = [], []
            for qi in qis:
                q0 = _tile_start(qi, T)
                qb = q_ref[pl.ds(q0, T), :]
                dob = do_ref[pl.ds(q0, T), :]
                qhs = [jnp.where(m, qb, jnp.zeros_like(qb)) for m in heads]
                dohs = [jnp.where(m, dob, 0.0).astype(BF16) for m in heads]
                queries = (qhs, dohs, jnp.concatenate(qhs, axis=0), jnp.concatenate(dohs, axis=0))
                starts = [_tile_start(qi - b, T) for b in range(nb)]
                tiles = [key_tile(k0) for k0 in starts]
                prepared.append((q0, queries, starts, tiles))
                jobs += [(qh, kb, mask) for (kb, _, _), mask in zip(tiles, masks) for qh in qhs]
            flat = _log_gates_of(jobs, after2)
            items = []
            for i, (qi, (q0, queries, starts, tiles)) in enumerate(zip(qis, prepared)):
                gates = [flat[2 * nb * i + 2 * b : 2 * nb * i + 2 * b + 2] for b in range(nb)]
                carries, carry_in = [zero, zero], []
                for b in range(nb):
                    carry_in.append(list(carries))
                    carries = [carries[h] + gates[b][h][2] for h in range(2)]
                st = (zero, zero, jnp.zeros((T, LANES), F32))
                if not isinstance(qi, int):
                    k_left = qi - (nb - 1)
                    qhs = queries[0]

                    def right_to_left(st, qhs=qhs):
                        kj = st[0] - 1
                        carry_s[0, kj] = st[1]
                        carry_s[1, kj] = st[2]
                        kb = k_ref[pl.ds(pl.multiple_of(kj * T, T), T), :]
                        sums = [jnp.sum(_log1m(_dot_nt(qh, kb), None)[0], axis=1, keepdims=True) for qh in qhs]
                        return kj, st[1] + sums[0], st[2] + sums[1]

                    k_first = lax.while_loop(more, right_to_left, (k_left, carries[0], carries[1]))[0]

                    def left_to_right(kj, st, queries=queries):
                        k0 = pl.multiple_of(kj * T, T)
                        tile = key_tile(k0)
                        left_gates = _log_gates(queries[0], [tile[0]], [None], after2)
                        carry = [[carry_s[0, kj], carry_s[1, kj]]]
                        return grads([(queries, [k0], [tile], [None], carry, left_gates, st)])[0]

                    st = lax.fori_loop(k_first, k_left, left_to_right, st)
                items.append((queries, starts, tiles, masks, carry_in, gates, st))
            for (q0, _, _, _), st in zip(prepared, grads(items)):
                dq_ref[pl.ds(q0, T), :] = (st[2] * (HEAD_DIM**-0.5)).astype(BF16)

        head_tiles = min(ATTN_WINDOW - 1, nq)
        for qi in range(head_tiles):
            q_tiles([qi], qi + 1)
        assert (nq - head_tiles) % ATTN_TOGETHER == 0

        def q_loop(i, carry):
            first = head_tiles + ATTN_TOGETHER * i
            q_tiles([first + j for j in range(ATTN_TOGETHER)], ATTN_WINDOW)
            return carry

        lax.fori_loop(0, (nq - head_tiles) // ATTN_TOGETHER, q_loop, 0)
        dk_ref[...] = dk_acc[...].astype(BF16)
        dv_ref[...] = dv_acc[...].astype(BF16)

        @pl.when(step == steps - 1)
        def _():
            for _, finish in scatter():
                finish()

    blk = pl.BlockSpec((S, LANES), lambda p: (0, p))
    sds = jax.ShapeDtypeStruct((S, D_ATTN), BF16)
    return pl.pallas_call(
        body,
        name="mixers_bwd",
        grid=(steps,),
        out_shape=[
            sds,
            sds,
            sds,
            jax.ShapeDtypeStruct((S, D_POOL), BF16),
            jax.ShapeDtypeStruct((8, D_POOL), F32),
            jax.ShapeDtypeStruct((N_GROUPS, GROUP_DIM, GROUP_DIM), BF16),
        ]
        + [_exchange_shapes([p], False, r)[0] for p, r in zip(partials, relations)],
        in_specs=[blk] * 6 + [_POOL_W_SPEC, _POOL_SCALE_SPEC] + [ANY_SPEC] * n,
        out_specs=[blk] * 4 + [pl.BlockSpec((8, GROUP_DIM), lambda p: (0, p)), _POOL_W_SPEC] + [ANY_SPEC] * n,
        scratch_shapes=[pltpu.VMEM((S, LANES), F32), pltpu.VMEM((S, LANES), F32), pltpu.VMEM((2, nq, T, 1), F32)]
        + [sem for _ in range(n) for sem in _exchange_sems(1)],
        compiler_params=pltpu.CompilerParams(
            dimension_semantics=("arbitrary",), vmem_limit_bytes=VMEM_LIMIT, has_side_effects=True
        ),
    )(q, k, v, do, dyp, pooled, pool_w, scale, *partials)


def _pool_bwd_group(dy_ref, pooled_ref, pw_ref, sc_ref, du_ref, dsc_ref, dpw_ref, g):
    S = dy_ref.shape[0]
    R = min(POOL_CHUNK, S)
    nc = S // R
    w = jnp.left_shift(jnp.int32(2), g)
    pw = pw_ref[0].astype(BF16)
    sc = sc_ref[...]
    d = lax.broadcasted_iota(jnp.int32, (R, R), 1) - lax.broadcasted_iota(jnp.int32, (R, R), 0)
    bt_cur = jnp.where((d >= 0) & (d < w), 1.0, 0.0).astype(BF16)
    dn = lax.broadcasted_iota(jnp.int32, (R, HALO), 1) + R - lax.broadcasted_iota(jnp.int32, (R, HALO), 0)
    bt_next = jnp.where(dn < w, 1.0, 0.0).astype(BF16)

    def per_count(dpl, r0):
        n = dpl.shape[0]
        count = jnp.minimum(r0 + lax.broadcasted_iota(jnp.int32, (n, 1), 0) + 1, w).astype(F32)
        return jnp.concatenate(_split2(dpl / count), axis=1)

    def chunk(r0, last, sums):
        dyv = dy_ref[pl.ds(r0, R), :]
        pooled = pooled_ref[pl.ds(r0, R), :]
        dmapped = (dyv * sc).astype(BF16)
        dsc = sums[0] + jnp.sum(dyv * _dot(pooled, pw), axis=0, keepdims=True)
        dpw = sums[1] + _dot_tn(pooled, dmapped)
        dpl = _dot_nt(dmapped, pw)
        wide = _dot(bt_cur, per_count(dpl, r0))
        if not last:
            nxt = pl.multiple_of(r0 + R, R)
            dpl_next = _dot_nt((dy_ref[pl.ds(nxt, HALO), :] * sc).astype(BF16), pw)
            wide = wide + _dot(bt_next, per_count(dpl_next, nxt))
        du_ref[pl.ds(r0, R), :] = (_lane_sum(wide, 2) - dpl).astype(BF16)
        return dsc, dpw

    sums = (jnp.zeros((1, GROUP_DIM), F32), jnp.zeros((GROUP_DIM, GROUP_DIM), F32))
    sums = lax.fori_loop(
        0, nc - 1, lambda c, s: chunk(pl.multiple_of(c * R, R), False, s), sums, unroll=POOL_UNROLL
    )
    dsc, dpw = chunk((nc - 1) * R, True, sums)
    dsc_ref[...] = jnp.zeros_like(dsc_ref)
    dsc_ref[0:1, :] = dsc
    dpw_ref[0] = dpw.astype(dpw_ref.dtype)


def _bwd_in(du, dq, dk, dv, w_in, x, dh1, g1):
    S = x.shape[0]
    tm = min(512, S)

    def body(du_ref, dq_ref, dk_ref, dv_ref, w_ref, x_ref, dh1_ref, g_ref, dx_ref, dproj_ref, dg_ref):
        @pl.when(pl.program_id(0) == 0)
        def _():
            dg_ref[...] = jnp.zeros_like(dg_ref)

        parts = (du_ref[...], dq_ref[...], dk_ref[...], dv_ref[...])
        dhn = jnp.zeros((tm, D_MODEL), F32)
        for j in range(N_DEV):
            piece = parts[j // 2][:, 256 * (j % 2) : 256 * (j % 2 + 1)]
            dproj_ref[:, 256 * j : 256 * (j + 1)] = piece
            dhn = dhn + _dot_nt(piece, w_ref[j])
        xv = x_ref[...]
        r = _rstd(xv)
        dxn, dg = _rms_bwd(dhn, xv * r, r, g_ref[...])
        dx_ref[...] = dh1_ref[...] + dxn
        dg_ref[0:1, :] += dg

    half = pl.BlockSpec((tm, D_POOL), lambda i: (i, 0))
    full = pl.BlockSpec((tm, D_MODEL), lambda i: (i, 0))
    return pl.pallas_call(
        body,
        name="bwd_in",
        grid=(S // tm,),
        out_shape=[
            jax.ShapeDtypeStruct((S, D_MODEL), F32),
            jax.ShapeDtypeStruct((S, D_IN_PROJ), BF16),
            jax.ShapeDtypeStruct((8, D_MODEL), F32),
        ],
        in_specs=[half, half, half, half,
                  pl.BlockSpec((N_DEV, D_MODEL, 256), lambda i: (0, 0, 0)),
                  full, full, pl.BlockSpec((1, D_MODEL), lambda i: (0, 0))],
        out_specs=[full, pl.BlockSpec((tm, D_IN_PROJ), lambda i: (i, 0)),
                   pl.BlockSpec((8, D_MODEL), lambda i: (0, 0))],
        compiler_params=_params(("arbitrary",)),
    )(du, dq, dk, dv, w_in, x, dh1, g1)


def _rows(a):
    a = a.reshape(-1, LANES)
    pad = (-a.shape[0]) % 8
    return jnp.pad(a, ((0, pad), (0, 0))) if pad else a


def kernel(x, norm1_g, w_in, pool_w, pool_scale, pool_out_g, attn_out_g, w_out, norm2_g, w_up, w_down, final_g, loss_target, m_norm1_g, m_w_in, m_pool_w, m_pool_scale, m_pool_out_g, m_attn_out_g, m_w_out, m_norm2_g, m_w_up, m_w_down, m_final_g, v_norm1_g, v_w_in, v_pool_w, v_pool_scale, v_pool_out_g, v_attn_out_g, v_w_out, v_norm2_g, v_w_up, v_w_down, v_final_g):
    S = x.shape[1]
    xs = x.reshape(S, D_MODEL)
    tgt = loss_target.reshape(S, D_MODEL)
    row = lambda a: a.reshape(1, -1)

    (w_in_g,) = _exchange([w_in.astype(BF16)], True, "gather_w_in")
    lanes = lambda a: a.reshape(-1, LANES)
    vector_sets = [
        [lanes(a) for a in (final_g, norm2_g, pool_out_g, attn_out_g, norm1_g, pool_scale)],
        [lanes(a) for a in (m_final_g, m_norm2_g, m_pool_out_g, m_attn_out_g, m_norm1_g, m_pool_scale)],
        [lanes(a) for a in (v_final_g, v_norm2_g, v_pool_out_g, v_attn_out_g, v_norm1_g, v_pool_scale)],
    ]
    hn, u_pool, q, k, v, *prepared = _fwd_in(xs, row(norm1_g), w_in_g, [w_out, w_up, w_down], vector_sets)
    shards, packed_vectors = prepared[:3], prepared[3:]
    y_attn, pooled, y_pool, w_out_g, w_up_g, w_down_g = _mixers_fwd(q, k, v, u_pool, pool_w, row(pool_scale), shards)
    w_out_full = w_out_g.reshape(D_MODEL, D_MODEL)
    w_down_full = w_down_g.reshape(D_FF, D_MODEL)
    mixed, hn2, act, dup, dh2b, dh1, dh1b, dyp, dya, sg = _mlp_fwd_bwd(
        xs, y_pool, y_attn, tgt, row(pool_out_g), row(attn_out_g), row(norm2_g), row(final_g),
        w_out_full, w_up_g, w_down_full,
    )
    gp_out = _wgrad(mixed, dh1b, True, 1, "wgrad_out")
    gp_down, land_out = _wgrad(act, dh2b, True, 2, "wgrad_down", [gp_out])
    early = (2, 4)
    late = tuple(k for k in ALL_RELATIONS if k not in early)
    gp_up, land_down_early = _wgrad(hn2, dup, False, 2, "wgrad_up", [gp_down], relations=early)
    dq, dk, dv, du, dsc, dpw, land_up, land_down_late = _mixers_bwd(
        q, k, v, dya, dyp, pooled, pool_w, row(pool_scale), [gp_up, gp_down], [None, late]
    )
    dx, dproj, dg1 = _bwd_in(du, dq, dk, dv, w_in_g, xs, dh1, row(norm1_g))

    partial_vectors = jnp.concatenate([_rows(p) for p in (sg[0], sg[1], sg[2], dg1[0], dsc[0], sg[3])], axis=0)
    flat_pool = lambda a: a.reshape(N_GROUPS * GROUP_DIM, GROUP_DIM)
    gp_in, landed_vectors, landed_pool = _wgrad(
        hn, dproj, False, 1, "wgrad_in", [partial_vectors, flat_pool(dpw)], gather=True
    )
    smalls = [
        (landed_vectors, [(*packed_vectors, None)]),
        (landed_pool, [(flat_pool(pool_w), flat_pool(m_pool_w), flat_pool(v_pool_w), None)]),
    ]
    tail = _reduce_adam_tail(
        [[land_out], [land_up], [land_down_late, land_down_early]], [w_out, w_up, w_down],
        [m_w_out, m_w_up, m_w_down], [v_w_out, v_w_up, v_w_down], gp_in, (w_in, m_w_in, v_w_in), smalls,
    )
    big = {name: tail[4 * t : 4 * t + 4] for t, name in enumerate(("w_out", "w_up", "w_down", "w_in"))}

    def unpack(vec, pw):
        out = {}
        for i, name in enumerate(("final_g", "norm2_g", "mix_g", "norm1_g", "pool_scale")):
            out[name] = vec[8 * i : 8 * i + 8].reshape(-1)
        out["pool_scale"] = out["pool_scale"][:D_POOL]
        out["pool_out_g"], out["attn_out_g"] = out["mix_g"][:D_POOL], out["mix_g"][D_POOL:]
        out["pool_w"] = pw.reshape(N_GROUPS, GROUP_DIM, GROUP_DIM)
        return out, vec[40, 0]

    small_out = [unpack(tail[16 + i], tail[20 + i]) for i in range(4)]
    loss = small_out[0][1]
    order = ("norm1_g", "w_in", "pool_w", "pool_scale", "pool_out_g", "attn_out_g", "w_out", "norm2_g", "w_up",
             "w_down", "final_g")
    outs = [loss, dx.reshape(1, S, D_MODEL)]
    for i in range(4):
        for name in order:
            outs.append(big[name][i] if name in big else small_out[i][0][name])
    return tuple(outs)
```
